```python
import math
import jax
import jax.numpy as jnp
from jax import lax
import numpy as np

D_MODEL = 1024
BATCH = 4
SEQ = 4096
DEPTH = 2

GRID_W = 64
CTX_LEN = 256
HEAD_DIM = 64
ROPE_THETA = 10000.0
NORM_EPS = 1e-6
NEG_INF = -1e30

A_HEADS = 4
A_KV_HEADS = 2
WINDOW = 128
A_BLOCK = 128
SSD_HEADS = 4
SSD_HEAD_DIM = 64
SSD_GROUPS = 2
SSD_STATE = 64
SSD_CHUNK = 128
SSD_CONV = 4
LRU_WIDTH = 256
LRU_BLOCKS = 4
LRU_BLOCK_DIM = LRU_WIDTH // LRU_BLOCKS
LRU_CONV = 4
LRU_C = 8.0
D_HEADS = 4
D_KV_HEADS = 2
Q_BLOCK = 128
N_EXPERTS = 64
N_EXPERT_GROUPS = 8
TOPK_GROUPS = 4
TOP_K = 8
EXPERT_HIDDEN = 256
SHARED_HIDDEN = 256
ROUTED_SCALE = 2.5
MOE_BLOCK = 128

A_Q = A_HEADS * HEAD_DIM
A_KV = A_KV_HEADS * HEAD_DIM
SSD_INNER = SSD_HEADS * SSD_HEAD_DIM
SSD_BC = SSD_GROUPS * SSD_STATE
SSD_CONV_CH = SSD_INNER + 2 * SSD_BC
D_Q = D_HEADS * HEAD_DIM
D_KV = D_KV_HEADS * HEAD_DIM
IN_SPLITS = (A_Q, A_KV, A_KV,
             SSD_INNER, SSD_INNER, SSD_BC, SSD_BC, SSD_HEADS, SSD_HEADS,
             LRU_WIDTH, LRU_WIDTH,
             D_Q, D_KV, D_KV)
IN_PROJ = sum(IN_SPLITS)
IN_CUTS = tuple(sum(IN_SPLITS[:i + 1]) for i in range(len(IN_SPLITS) - 1))
MIX_WIDTH = A_Q + SSD_INNER + LRU_WIDTH + D_Q

kernel_name = 'hybrid_diffusion_block'


def rms_norm(x, gain):
    xf = x.astype(jnp.float32)
    y = xf * lax.rsqrt(jnp.mean(xf * xf, axis=-1, keepdims=True) + NORM_EPS)
    return (y * gain.astype(jnp.float32)).astype(x.dtype)


def modulate(x, shift, scale):
    return x * (1 + scale) + shift


def split_heads(t):
    return t.reshape(t.shape[:-1] + (t.shape[-1] // HEAD_DIM, HEAD_DIM))


def seq_cat(u, v):
    return jnp.concatenate([u, v], axis=1)


def rev_cat(u, v):
    return jnp.concatenate([jnp.flip(u, 1), jnp.flip(v, 1)], axis=1)


def axial_rope_tables(n_rows):
    t = jnp.arange(n_rows * GRID_W)
    row = (t // GRID_W).astype(jnp.float32)
    col = (t % GRID_W).astype(jnp.float32)
    axis_dim = HEAD_DIM // 2
    inv_freq = ROPE_THETA ** (-jnp.arange(0, axis_dim, 2, dtype=jnp.float32) / axis_dim)
    ang = jnp.concatenate([row[:, None] * inv_freq, col[:, None] * inv_freq], axis=-1)
    return jnp.cos(ang), jnp.sin(ang)


def apply_rope(x, cos, sin):
    xf = x.astype(jnp.float32).reshape(x.shape[:-1] + (HEAD_DIM // 2, 2))
    x1, x2 = xf[..., 0], xf[..., 1]
    c = cos[None, :, None, :]
    s = sin[None, :, None, :]
    out = jnp.stack([x1 * c - x2 * s, x1 * s + x2 * c], axis=-1)
    return out.reshape(x.shape).astype(x.dtype)


def dense_gqa(q, k, v, sink=None):
    b, tq, hq, d = q.shape
    hkv = k.shape[2]
    grp = hq // hkv
    qg = q.reshape(b, tq, hkv, grp, d)
    s = jnp.einsum('bqhgd,bkhd->bhgqk', qg, k).astype(jnp.float32) * d ** -0.5
    if sink is not None:
        s_sink = jnp.broadcast_to(sink.astype(jnp.float32).reshape(1, hkv, grp, 1, 1), s.shape[:-1] + (1,))
        p = jax.nn.softmax(jnp.concatenate([s, s_sink], axis=-1), axis=-1)[..., :-1]
    else:
        p = jax.nn.softmax(s, axis=-1)
    o = jnp.einsum('bhgqk,bkhd->bqhgd', p.astype(v.dtype), v)
    return o.reshape(b, tq, hq * d)


def window_attention(q_l, k_l, v_l, q_c, k_c, v_c, sink, with_ctx):
    b, S, hq, d = q_l.shape
    hkv = k_l.shape[2]
    grp = hq // hkv
    nb = S // A_BLOCK
    n_ctx = k_c.shape[1]
    qb = q_l.reshape(b, nb, A_BLOCK, hkv, grp, d)

    def band(t):
        z = jnp.zeros((b, A_BLOCK) + t.shape[2:], t.dtype)
        tp = jnp.concatenate([z, t, z], axis=1).reshape((b, nb + 2, A_BLOCK) + t.shape[2:])
        return jnp.concatenate([tp[:, :-2], tp[:, 1:-1], tp[:, 2:]], axis=2)

    kb, vb = band(k_l), band(v_l)
    scale = d ** -0.5
    s_win = jnp.einsum('bnqhgd,bnkhd->bnhgqk', qb, kb).astype(jnp.float32) * scale
    q_off = jnp.arange(A_BLOCK)[:, None]
    k_off = jnp.arange(3 * A_BLOCK)[None, :] - A_BLOCK
    k_abs = jnp.arange(nb)[:, None, None] * A_BLOCK + k_off
    valid = (jnp.abs(k_off - q_off) <= WINDOW) & (k_abs >= 0) & (k_abs < S)
    s_win = jnp.where(valid[None, :, None, None], s_win, NEG_INF)
    s_ctx = jnp.einsum('bnqhgd,bkhd->bnhgqk', qb, k_c).astype(jnp.float32) * scale
    s_sink = jnp.broadcast_to(sink.astype(jnp.float32).reshape(1, 1, hkv, grp, 1, 1), s_win.shape[:-1] + (1,))
    p = jax.nn.softmax(jnp.concatenate([s_win, s_ctx, s_sink], axis=-1), axis=-1).astype(v_l.dtype)
    n_win = 3 * A_BLOCK
    o = (jnp.einsum('bnhgqk,bnkhd->bnqhgd', p[..., :n_win], vb)
         + jnp.einsum('bnhgqk,bkhd->bnqhgd', p[..., n_win:n_win + n_ctx], v_c))
    o_l = o.reshape(b, S, hq * d)
    o_c = dense_gqa(q_c, k_c, v_c, sink) if with_ctx else None
    return o_c, o_l


def global_attention(q_l, k_l, v_l, q_c, k_c, v_c, with_ctx):
    b, S, hq, d = q_l.shape
    nb = S // Q_BLOCK
    k_all = seq_cat(k_c, k_l)
    v_all = seq_cat(v_c, v_l)
    q_blocks = jnp.moveaxis(q_l.reshape(b, nb, Q_BLOCK, hq, d), 1, 0)
    o = lax.map(lambda qb: dense_gqa(qb, k_all, v_all), q_blocks)
    o_l = jnp.moveaxis(o, 0, 1).reshape(b, S, hq * d)
    o_c = dense_gqa(q_c, k_c, v_c) if with_ctx else None
    return o_c, o_l


def depthwise_conv(x, w, bias):
    width, ch = w.shape
    left = width // 2
    y = lax.conv_general_dilated(x, w[:, None, :], (1,), [(left, width - 1 - left)],
                                 dimension_numbers=('NWC', 'WIO', 'NWC'), feature_group_count=ch)
    return y + bias


def ssd_chunked_scan(x, dt, a_neg, bmat, cmat):
    b, T, h, p = x.shape
    n = bmat.shape[-1]
    nc = T // SSD_CHUNK
    xc = x.astype(jnp.float32).reshape(b, nc, SSD_CHUNK, h, p)
    bc = bmat.astype(jnp.float32).reshape(b, nc, SSD_CHUNK, h, n)
    cc = cmat.astype(jnp.float32).reshape(b, nc, SSD_CHUNK, h, n)
    dtc = dt.astype(jnp.float32).reshape(b, nc, SSD_CHUNK, h)
    a_cum = jnp.cumsum(dtc * a_neg, axis=2)
    seg = a_cum[:, :, :, None, :] - a_cum[:, :, None, :, :]
    lower = jnp.tril(jnp.ones((SSD_CHUNK, SSD_CHUNK), bool))
    decay = jnp.exp(jnp.where(lower[:, :, None], seg, -jnp.inf))
    w_ij = jnp.einsum('bcihn,bcjhn->bcijh', cc, bc) * decay * dtc[:, :, None]
    y_intra = jnp.einsum('bcijh,bcjhp->bcihp', w_ij, xc)
    to_end = jnp.exp(a_cum[:, :, -1:] - a_cum) * dtc
    states = jnp.einsum('bcjhn,bcjh,bcjhp->bchpn', bc, to_end, xc)
    chunk_decay = jnp.exp(a_cum[:, :, -1])

    def step(carry, inp):
        st, dec = inp
        return carry * dec[..., None, None] + st, carry

    _, start = lax.scan(step, jnp.zeros((b, h, p, n), jnp.float32),
                        (jnp.moveaxis(states, 1, 0), jnp.moveaxis(chunk_decay, 1, 0)))
    start = jnp.moveaxis(start, 0, 1)
    y_inter = jnp.einsum('bcihn,bchpn->bcihp', cc, start) * jnp.exp(a_cum)[..., None]
    return (y_intra + y_inter).reshape(b, T, h, p)


def ssd_mixer(parts_c, parts_l, conv_w, conv_b, dt_bias, a_log, d_skip, norm_gain, with_ctx):
    def prep(z, xs, bm, cm, dt_f, dt_b):
        b, t = xs.shape[:2]
        xbc = jax.nn.silu(depthwise_conv(jnp.concatenate([xs, bm, cm], axis=-1), conv_w, conv_b))
        xs, bm, cm = jnp.split(xbc, (SSD_INNER, SSD_INNER + SSD_BC), axis=-1)
        to_heads = lambda u: jnp.repeat(u.reshape(b, t, SSD_GROUPS, SSD_STATE), SSD_HEADS // SSD_GROUPS, axis=2)
        dt = jax.nn.softplus(jnp.stack([dt_f, dt_b]).astype(jnp.float32)
                             + dt_bias[:, None, None, :].astype(jnp.float32))
        return z, xs.reshape(b, t, SSD_HEADS, SSD_HEAD_DIM), to_heads(bm), to_heads(cm), dt

    z_c, x_c, b_c, c_c, dt_c = prep(*parts_c)
    z_l, x_l, b_l, c_l, dt_l = prep(*parts_l)
    n_ctx = x_c.shape[1]
    a_neg = -jnp.exp(a_log.astype(jnp.float32))
    y_f = ssd_chunked_scan(seq_cat(x_c, x_l), seq_cat(dt_c[0], dt_l[0]), a_neg[0],
                           seq_cat(b_c, b_l), seq_cat(c_c, c_l))
    y_b = ssd_chunked_scan(rev_cat(x_c, x_l), rev_cat(dt_c[1], dt_l[1]), a_neg[1],
                           rev_cat(b_c, b_l), rev_cat(c_c, c_l))

    def finish(y_fwd, y_bwd_rev, xs, z):
        y = (y_fwd + jnp.flip(y_bwd_rev, 1)).astype(xs.dtype) + xs * d_skip[:, None]
        y = y.reshape(y.shape[:2] + (SSD_INNER,))
        return rms_norm(y * jax.nn.silu(z), norm_gain)

    out_l = finish(y_f[:, n_ctx:], y_b[:, n_ctx:], x_l, z_l)
    out_c = finish(y_f[:, :n_ctx], y_b[:, :n_ctx], x_c, z_c) if with_ctx else None
    return out_c, out_l


def rglru_scan(u, w_a, b_a, w_i, b_i, lam):
    b, T, ch = u.shape
    uf = u.astype(jnp.float32)
    ub = uf.reshape(b, T, LRU_BLOCKS, LRU_BLOCK_DIM)

    def gate(w, bias):
        return jax.nn.sigmoid(jnp.einsum('btki,kij->btkj', ub, w.astype(jnp.float32)).reshape(b, T, ch)
                              + bias.astype(jnp.float32))

    r = gate(w_a, b_a)
    i = gate(w_i, b_i)
    log_a = -LRU_C * r * jax.nn.softplus(-lam.astype(jnp.float32))
    a = jnp.exp(log_a)
    inp = jnp.sqrt(-jnp.expm1(2.0 * log_a)) * (i * uf)

    def combine(left, right):
        a1, h1 = left
        a2, h2 = right
        return a1 * a2, a2 * h1 + h2

    _, hs = lax.associative_scan(combine, (a, inp), axis=1)
    return hs


def rglru_mixer(x_c, g_c, x_l, g_l, conv_w, conv_b, w_a, b_a, w_i, b_i, lam, with_ctx):
    u_c = depthwise_conv(x_c, conv_w, conv_b)
    u_l = depthwise_conv(x_l, conv_w, conv_b)
    n_ctx = u_c.shape[1]
    h_f = rglru_scan(seq_cat(u_c, u_l), w_a[0], b_a[0], w_i[0], b_i[0], lam[0])
    h_b = rglru_scan(rev_cat(u_c, u_l), w_a[1], b_a[1], w_i[1], b_i[1], lam[1])

    def finish(hf, hb_rev, g):
        return (hf + jnp.flip(hb_rev, 1)).astype(g.dtype) * jax.nn.gelu(g)

    out_l = finish(h_f[:, n_ctx:], h_b[:, n_ctx:], g_l)
    out_c = finish(h_f[:, :n_ctx], h_b[:, :n_ctx], g_c) if with_ctx else None
    return out_c, out_l


def token_mixing(x_c, x_l, mod_c, mod_l, rope_cos, rope_sin, g_pre, g_post, w_in, w_out,
                 a_sink, ssd_conv_w, ssd_conv_b, ssd_dt_bias, ssd_a_log, ssd_d, ssd_norm,
                 lru_conv_w, lru_conv_b, lru_w_a, lru_b_a, lru_w_i, lru_b_i, lru_lambda,
                 d_q_norm, d_k_norm, with_ctx):
    shift_c, scale_c, gate_c = mod_c
    shift_l, scale_l, gate_l = mod_l

    def project(x, shift, scale):
        h = modulate(rms_norm(x, g_pre), shift, scale)
        return jnp.split(h @ w_in, IN_CUTS, axis=-1)

    pc = project(x_c, shift_c, scale_c)
    pl = project(x_l, shift_l, scale_l)
    rope = lambda t: apply_rope(t, rope_cos, rope_sin)
    qk_norm = lambda t, g: rms_norm(split_heads(t), g)

    oa_c, oa_l = window_attention(rope(split_heads(pl[0])), rope(split_heads(pl[1])), split_heads(pl[2]),
                                  split_heads(pc[0]), split_heads(pc[1]), split_heads(pc[2]), a_sink, with_ctx)
    ob_c, ob_l = ssd_mixer(pc[3:9], pl[3:9], ssd_conv_w, ssd_conv_b, ssd_dt_bias, ssd_a_log, ssd_d, ssd_norm,
                           with_ctx)
    oc_c, oc_l = rglru_mixer(pc[9], pc[10], pl[9], pl[10], lru_conv_w, lru_conv_b, lru_w_a, lru_b_a,
                             lru_w_i, lru_b_i, lru_lambda, with_ctx)
    od_c, od_l = global_attention(rope(qk_norm(pl[11], d_q_norm)), rope(qk_norm(pl[12], d_k_norm)),
                                  split_heads(pl[13]), qk_norm(pc[11], d_q_norm), qk_norm(pc[12], d_k_norm),
                                  split_heads(pc[13]), with_ctx)

    def merge(x, outs, gate):
        return x + gate * rms_norm(jnp.concatenate(outs, axis=-1) @ w_out, g_post)

    x_l = merge(x_l, (oa_l, ob_l, oc_l, od_l), gate_l)
    if with_ctx:
        x_c = merge(x_c, (oa_c, ob_c, oc_c, od_c), gate_c)
    return x_c, x_l


def routed_experts(h, expert_idx, expert_w, w_gate, w_up, w_down):
    n_tok, d = h.shape
    n_assign = n_tok * TOP_K
    flat_e = expert_idx.reshape(-1)
    order = jnp.argsort(flat_e)
    e_sorted = flat_e[order]
    counts = jnp.bincount(flat_e, length=N_EXPERTS)
    padded = (counts + MOE_BLOCK - 1) // MOE_BLOCK * MOE_BLOCK
    start = jnp.cumsum(counts) - counts
    padded_end = jnp.cumsum(padded)
    dest = (padded_end - padded)[e_sorted] + jnp.arange(n_assign) - start[e_sorted]
    n_blocks = -(-n_assign // MOE_BLOCK) + N_EXPERTS
    n_slots = n_blocks * MOE_BLOCK
    slot_tok = jnp.zeros((n_slots,), jnp.int32).at[dest].set((order // TOP_K).astype(jnp.int32))
    slot_w = jnp.zeros((n_slots,), h.dtype).at[dest].set(expert_w.reshape(-1)[order].astype(h.dtype))
    block_e = jnp.minimum(jnp.searchsorted(padded_end, jnp.arange(n_blocks) * MOE_BLOCK, side='right'),
                          N_EXPERTS - 1)

    def run_block(args):
        tok, wt, e = args
        xb = h[tok]
        hid = jax.nn.silu(xb @ w_gate[e]) * (xb @ w_up[e])
        return (hid @ w_down[e]) * wt[:, None]

    y = lax.map(run_block, (slot_tok.reshape(n_blocks, MOE_BLOCK), slot_w.reshape(n_blocks, MOE_BLOCK), block_e))
    return jax.ops.segment_sum(y.reshape(n_slots, d), slot_tok, num_segments=n_tok)


def moe_ffn(h, router_w, router_bias, w_gate, w_up, w_down, sh_gate, sh_up, sh_down):
    n_tok = h.shape[0]
    scores = jax.nn.sigmoid((h @ router_w).astype(jnp.float32))
    biased = scores + router_bias.astype(jnp.float32)
    grouped = biased.reshape(n_tok, N_EXPERT_GROUPS, N_EXPERTS // N_EXPERT_GROUPS)
    group_score = jnp.sum(lax.top_k(grouped, 2)[0], axis=-1)
    _, top_groups = lax.top_k(group_score, TOPK_GROUPS)
    group_mask = jnp.sum(jax.nn.one_hot(top_groups, N_EXPERT_GROUPS, dtype=jnp.float32), axis=1) > 0
    expert_mask = jnp.repeat(group_mask, N_EXPERTS // N_EXPERT_GROUPS, axis=1)
    _, idx = lax.top_k(jnp.where(expert_mask, biased, -jnp.inf), TOP_K)
    w = jnp.take_along_axis(scores, idx, axis=1)
    w = w / jnp.sum(w, axis=-1, keepdims=True) * ROUTED_SCALE
    shared = (jax.nn.silu(h @ sh_gate) * (h @ sh_up)) @ sh_down
    return routed_experts(h, idx, w, w_gate, w_up, w_down) + shared


def channel_mixing(x_c, x_l, mod_c, mod_l, g_pre, g_post, router_w, router_bias, w_gate, w_up, w_down,
                   sh_gate, sh_up, sh_down, with_ctx):
    shift_c, scale_c, gate_c = mod_c
    shift_l, scale_l, gate_l = mod_l
    b, S, d = x_l.shape
    tokens = modulate(rms_norm(x_l, g_pre), shift_l, scale_l).reshape(b * S, d)
    if with_ctx:
        h_c = modulate(rms_norm(x_c, g_pre), shift_c, scale_c).reshape(-1, d)
        tokens = jnp.concatenate([h_c, tokens], axis=0)
    y = moe_ffn(tokens, router_w, router_bias, w_gate, w_up, w_down, sh_gate, sh_up, sh_down)
    n_c = tokens.shape[0] - b * S
    x_l = x_l + gate_l * rms_norm(y[n_c:].reshape(b, S, d), g_post)
    if with_ctx:
        x_c = x_c + gate_c * rms_norm(y[:n_c].reshape(x_c.shape), g_post)
    return x_c, x_l


def setup_inputs(seed: int = 0) -> dict:
    key = jax.random.key(seed)
    ks = iter(jax.random.split(key, 48))
    L = DEPTH
    f32 = jnp.float32

    def normal(shape, scale):
        return jax.random.normal(next(ks), shape, f32) * scale

    def gain(shape):
        return 1.0 + normal(shape, 0.05)

    def uniform(shape, lo, hi):
        return jax.random.uniform(next(ks), shape, f32, lo, hi)

    a_pow = uniform((L, 2, LRU_WIDTH), 0.9, 0.999) ** (1.0 / LRU_C)
    dt0 = jnp.exp(uniform((L, 2, SSD_HEADS), math.log(1e-3), math.log(1e-1)))
    return {
        'x': normal((BATCH, SEQ, D_MODEL), 1.0),
        'c': normal((BATCH, D_MODEL), 1.0),
        'ctx': normal((BATCH, CTX_LEN, D_MODEL), 1.0),
        'c_ctx': normal((D_MODEL,), 1.0),
        'w_ada': normal((L, D_MODEL, 6 * D_MODEL), 0.5 * D_MODEL ** -0.5),
        'b_ada': normal((L, 6 * D_MODEL), 0.02),
        'g_mix_pre': gain((L, D_MODEL)),
        'g_mix_post': gain((L, D_MODEL)),
        'g_ffn_pre': gain((L, D_MODEL)),
        'g_ffn_post': gain((L, D_MODEL)),
        'w_in': normal((L, D_MODEL, IN_PROJ), D_MODEL ** -0.5),
        'w_out': normal((L, MIX_WIDTH, D_MODEL), MIX_WIDTH ** -0.5),
        'a_sink': normal((L, A_HEADS), 0.5),
        'ssd_conv_w': normal((L, SSD_CONV, SSD_CONV_CH), SSD_CONV ** -0.5),
        'ssd_conv_b': normal((L, SSD_CONV_CH), 0.02),
        'ssd_dt_bias': dt0 + jnp.log(-jnp.expm1(-dt0)),
        'ssd_a_log': jnp.log(uniform((L, 2, SSD_HEADS), 1.0, 16.0)),
        'ssd_d': gain((L, SSD_HEADS)),
        'ssd_norm': gain((L, SSD_INNER)),
        'lru_conv_w': normal((L, LRU_CONV, LRU_WIDTH), LRU_CONV ** -0.5),
        'lru_conv_b': normal((L, LRU_WIDTH), 0.02),
        'lru_w_a': normal((L, 2, LRU_BLOCKS, LRU_BLOCK_DIM, LRU_BLOCK_DIM), LRU_BLOCK_DIM ** -0.5),
        'lru_b_a': normal((L, 2, LRU_WIDTH), 0.02),
        'lru_w_i': normal((L, 2, LRU_BLOCKS, LRU_BLOCK_DIM, LRU_BLOCK_DIM), LRU_BLOCK_DIM ** -0.5),
        'lru_b_i': normal((L, 2, LRU_WIDTH), 0.02),
        'lru_lambda': jnp.log(a_pow) - jnp.log1p(-a_pow),
        'd_q_norm': gain((L, HEAD_DIM)),
        'd_k_norm': gain((L, HEAD_DIM)),
        'router_w': normal((L, D_MODEL, N_EXPERTS), D_MODEL ** -0.5),
        'router_bias': normal((L, N_EXPERTS), 0.01),
        'exp_w_gate': normal((L, N_EXPERTS, D_MODEL, EXPERT_HIDDEN), D_MODEL ** -0.5),
        'exp_w_up': normal((L, N_EXPERTS, D_MODEL, EXPERT_HIDDEN), D_MODEL ** -0.5),
        'exp_w_down': normal((L, N_EXPERTS, EXPERT_HIDDEN, D_MODEL), EXPERT_HIDDEN ** -0.5),
        'sh_w_gate': normal((L, D_MODEL, SHARED_HIDDEN), D_MODEL ** -0.5),
        'sh_w_up': normal((L, D_MODEL, SHARED_HIDDEN), D_MODEL ** -0.5),
        'sh_w_down': normal((L, SHARED_HIDDEN, D_MODEL), SHARED_HIDDEN ** -0.5),
    }


def reference(x, c, ctx, c_ctx, w_ada, b_ada, g_mix_pre, g_mix_post, g_ffn_pre, g_ffn_post, w_in, w_out,
              a_sink, ssd_conv_w, ssd_conv_b, ssd_dt_bias, ssd_a_log, ssd_d, ssd_norm,
              lru_conv_w, lru_conv_b, lru_w_a, lru_b_a, lru_w_i, lru_b_i, lru_lambda,
              d_q_norm, d_k_norm, router_w, router_bias, exp_w_gate, exp_w_up, exp_w_down,
              sh_w_gate, sh_w_up, sh_w_down):
    b = x.shape[0]
    n_rows = x.shape[1] // GRID_W
    rope_cos, rope_sin = axial_rope_tables(n_rows)
    silu_c = jax.nn.silu(c)
    silu_cc = jax.nn.silu(c_ctx)
    x_c, x_l = ctx, x
    for l in range(DEPTH):
        with_ctx = l < DEPTH - 1
        mod_l = (silu_c @ w_ada[l] + b_ada[l]).reshape(b, 6, 1, D_MODEL).transpose(1, 0, 2, 3)
        mod_c = (silu_cc @ w_ada[l] + b_ada[l]).reshape(6, D_MODEL)
        x_c, x_l = token_mixing(x_c, x_l, mod_c[:3], mod_l[:3], rope_cos, rope_sin,
                                g_mix_pre[l], g_mix_post[l], w_in[l], w_out[l],
                                a_sink[l], ssd_conv_w[l], ssd_conv_b[l], ssd_dt_bias[l], ssd_a_log[l],
                                ssd_d[l], ssd_norm[l], lru_conv_w[l], lru_conv_b[l], lru_w_a[l], lru_b_a[l],
                                lru_w_i[l], lru_b_i[l], lru_lambda[l], d_q_norm[l], d_k_norm[l], with_ctx)
        x_c, x_l = channel_mixing(x_c, x_l, mod_c[3:], mod_l[3:], g_ffn_pre[l], g_ffn_post[l],
                                  router_w[l], router_bias[l], exp_w_gate[l], exp_w_up[l], exp_w_down[l],
                                  sh_w_gate[l], sh_w_up[l], sh_w_down[l], with_ctx)
    return x_l
```

```python
import functools
import math

import jax
import jax.numpy as jnp
from jax import lax
from jax.experimental import pallas as pl
from jax.experimental.pallas import tpu as pltpu

F32 = jnp.float32
BF16 = jnp.bfloat16

HEAD_DIM = 64
GRID_W = 64
ROPE_THETA = 10000.0
NORM_EPS = 1e-6
NEG_INF = -1e30
A_HEADS, A_KV_HEADS, WINDOW = 4, 2, 128
SSD_HEADS, SSD_GROUPS, SSD_STATE, SSD_CONV = 4, 2, 64, 4
LRU_WIDTH, LRU_BLOCKS, LRU_CONV, LRU_C = 256, 4, 4, 8.0
D_HEADS, D_KV_HEADS = 4, 2
N_EXPERTS, N_EXPERT_GROUPS, TOPK_GROUPS, TOP_K = 64, 8, 4, 8
EXPERT_HIDDEN, SHARED_HIDDEN = 256, 256
ROUTED_SCALE = 2.5

LANES = 128
SUBLANES = 8

TM = 512
T_CONV = 256
CHUNK = 128
TQ_GLOBAL = 256
KV_CHUNK = 512
T_DISPATCH = 256
T_COMBINE = 128
BM_EXPERT = 256
VMEM_LIMIT = 48 * 1024 * 1024

C_QA, C_KA, C_VA = 0, 256, 384
C_QD, C_KD, C_VD = 512, 768, 896
C_Z, C_XBC, C_DT = 1024, 1280, 1792
C_LX, C_LG = 1920, 2176
NP_IN = 2432


def _dot(a, b):
    return jnp.dot(a, b, preferred_element_type=F32)


def _dot_nt(a, b):
    return lax.dot_general(a, b, (((1,), (1,)), ((), ())), preferred_element_type=F32)


def _dot3(a, b):
    a1 = a.astype(BF16)
    r1 = a - a1.astype(F32)
    a2 = r1.astype(BF16)
    a3 = (r1 - a2.astype(F32)).astype(BF16)
    return _dot(a1, b) + _dot(a2, b) + _dot(a3, b)


def _dot3_left(a, b):
    b1 = b.astype(BF16)
    r1 = b - b1.astype(F32)
    b2 = r1.astype(BF16)
    b3 = (r1 - b2.astype(F32)).astype(BF16)
    return _dot(a, b1) + _dot(a, b2) + _dot(a, b3)


def _silu(x):
    return x * jax.nn.sigmoid(x)


def _softplus(x):
    return jnp.maximum(x, 0.0) + jnp.log1p(jnp.exp(-jnp.abs(x)))


def _rms(x, gain):
    return x * lax.rsqrt(jnp.mean(x * x, axis=-1, keepdims=True) + NORM_EPS) * gain


def _params(sem=None):
    return pltpu.CompilerParams(dimension_semantics=sem, vmem_limit_bytes=VMEM_LIMIT)


def _adaln_kernel(c_ref, w_ref, b_ref, o_ref):
    s = _silu(c_ref[...])
    o_ref[0] = _dot(s.astype(BF16), w_ref[0].astype(BF16)) + b_ref[0]


def _adaln(cin, w_ada, b_ada):
    depth, d, n6 = w_ada.shape
    tn = 1024
    return pl.pallas_call(
        _adaln_kernel,
        grid=(depth, n6 // tn),
        in_specs=[pl.BlockSpec((SUBLANES, d), lambda l, j: (0, 0)),
                  pl.BlockSpec((1, d, tn), lambda l, j: (l, 0, j)),
                  pl.BlockSpec((1, 1, tn), lambda l, j: (l, 0, j))],
        out_specs=pl.BlockSpec((1, SUBLANES, tn), lambda l, j: (l, 0, j)),
        out_shape=jax.ShapeDtypeStruct((depth, SUBLANES, n6), F32),
        compiler_params=_params(("parallel", "parallel")),
        name="adaln",
    )(cin, w_ada, b_ada.reshape(depth, 1, n6))


def _swap_halves(t):
    w = t.shape[1]
    lane = lax.broadcasted_iota(jnp.int32, (1, w), 1)
    first = (lane & 32) == 0
    return jnp.where(first, pltpu.roll(t, w - 32, axis=1), pltpu.roll(t, 32, axis=1))


def _inproj_kernel(x_ref, shift_ref, scale_ref, gpre_ref, w_ref, cos_ref, sin_ref, gq_ref, gk_ref, hm_ref,
                   qa_ref, kat_ref, va_ref, qd_ref, kdt_ref, vd_ref, z_ref, xbc_ref, dt_ref, lx_ref, lg_ref,
                   *, n_lat):
    i = pl.program_id(0)
    is_lat = i * TM < n_lat
    h = _rms(x_ref[...], gpre_ref[...])
    h = h * (1.0 + scale_ref[0]) + shift_ref[0]
    hb = h.astype(BF16)

    def sec(a, b):
        return _dot(hb, w_ref[:, a:b])

    cos = jnp.where(is_lat, cos_ref[...], 1.0)
    sin = jnp.where(is_lat, sin_ref[...], 0.0)

    def rope(t):
        w = t.shape[1]
        return t * cos[:, :w] + _swap_halves(t) * sin[:, :w]

    def head_norm(t, gain):
        w = t.shape[1]
        ms = _dot3(t * t, hm_ref[:w, :w])
        return t * lax.rsqrt(ms + NORM_EPS) * gain

    scale = HEAD_DIM ** -0.5
    qa_ref[...] = (rope(sec(C_QA, C_KA)) * scale).astype(BF16)
    kat_ref[...] = rope(sec(C_KA, C_VA)).T.astype(BF16)
    va_ref[...] = sec(C_VA, C_QD).astype(BF16)
    qd_ref[...] = (rope(head_norm(sec(C_QD, C_KD), gq_ref[...])) * scale).astype(BF16)
    kdt_ref[...] = rope(head_norm(sec(C_KD, C_VD), gk_ref[...])).T.astype(BF16)
    vd_ref[...] = sec(C_VD, C_Z).astype(BF16)
    z_ref[...] = sec(C_Z, C_XBC)
    xbc_ref[...] = sec(C_XBC, C_DT)
    dt_ref[...] = sec(C_DT, C_LX)
    lx_ref[...] = sec(C_LX, C_LG)
    lg_ref[...] = sec(C_LG, NP_IN)


def _mod_spec(chunk, n_lat, seq, batch, tile):
    def imap(i):
        row0 = i * tile
        seg = jnp.where(row0 < n_lat, row0 // seq, batch)
        return (seg * 6 + chunk, 0, 0)
    return imap


def _inproj(xu, mod, gpre, w_pad, cos_t, sin_t, gq, gk, hm, *, n_lat, seq, batch):
    n, d = xu.shape
    nt = n // TM
    spt = seq // TM
    row = lambda w: pl.BlockSpec((TM, w), lambda i: (i, 0))
    colT = pl.BlockSpec((LANES, TM), lambda i: (0, i))
    const = lambda a: pl.BlockSpec(a.shape, lambda i: (0,) * a.ndim)
    out_shapes = (
        jax.ShapeDtypeStruct((n, 256), BF16), jax.ShapeDtypeStruct((LANES, n), BF16),
        jax.ShapeDtypeStruct((n, LANES), BF16),
        jax.ShapeDtypeStruct((n, 256), BF16), jax.ShapeDtypeStruct((LANES, n), BF16),
        jax.ShapeDtypeStruct((n, LANES), BF16),
        jax.ShapeDtypeStruct((n, 256), F32), jax.ShapeDtypeStruct((n, 512), F32),
        jax.ShapeDtypeStruct((n, LANES), F32), jax.ShapeDtypeStruct((n, 256), F32),
        jax.ShapeDtypeStruct((n, 256), F32))
    return pl.pallas_call(
        functools.partial(_inproj_kernel, n_lat=n_lat),
        grid=(nt,),
        in_specs=[row(d),
                  pl.BlockSpec((1, 1, d), _mod_spec(0, n_lat, seq, batch, TM)),
                  pl.BlockSpec((1, 1, d), _mod_spec(1, n_lat, seq, batch, TM)),
                  const(gpre), const(w_pad),
                  pl.BlockSpec((TM, 256), lambda i: (i % spt, 0)),
                  pl.BlockSpec((TM, 256), lambda i: (i % spt, 0)),
                  const(gq), const(gk), const(hm)],
        out_specs=(row(256), colT, row(LANES), row(256), colT, row(LANES),
                   row(256), row(512), row(LANES), row(256), row(256)),
        out_shape=out_shapes,
        compiler_params=_params(("parallel",)),
        name="inproj",
    )(xu, mod, mod, gpre, w_pad, cos_t, sin_t, gq, gk, hm)


def _conv_kernel(xs_ref, xsp_ref, xsn_ref, xl_ref, xlp_ref, xln_ref, ws_ref, bs_ref, wl_ref, bl_ref,
                 os_ref, ol_ref, *, n_lat, seq, ctx_len):
    i = pl.program_id(0)
    row0 = i * T_CONV
    pos = jnp.where(row0 < n_lat, row0 % seq, (row0 - n_lat) % ctx_len)
    slen = jnp.where(row0 < n_lat, seq, ctx_len)
    first = pos == 0
    last = pos + T_CONV == slen
    row = lax.broadcasted_iota(jnp.int32, (T_CONV, 1), 0)

    def conv(x, prev, nxt, w, b):
        pm = jnp.where(first, 0.0, prev)
        nx = jnp.where(last, 0.0, nxt)
        xm1 = jnp.where(row == 0, pm[7:8, :], pltpu.roll(x, 1, axis=0))
        xm2 = jnp.where(row == 0, pm[6:7, :], jnp.where(row == 1, pm[7:8, :], pltpu.roll(x, 2, axis=0)))
        xp1 = jnp.where(row == T_CONV - 1, nx[0:1, :], pltpu.roll(x, T_CONV - 1, axis=0))
        return w[0:1, :] * xm2 + w[1:2, :] * xm1 + w[2:3, :] * x + w[3:4, :] * xp1 + b

    os_ref[...] = _silu(conv(xs_ref[...], xsp_ref[...], xsn_ref[...], ws_ref[...], bs_ref[...]))
    ol_ref[...] = conv(xl_ref[...], xlp_ref[...], xln_ref[...], wl_ref[...], bl_ref[...])


def _conv(xbc_raw, lx_raw, ws, bs, wl, bl, *, n_lat, seq, ctx_len):
    n = xbc_raw.shape[0]
    nt = n // T_CONV
    r8 = T_CONV // SUBLANES
    n8 = n // SUBLANES
    main = lambda w: pl.BlockSpec((T_CONV, w), lambda i: (i, 0))
    prev = lambda w: pl.BlockSpec((SUBLANES, w), lambda i: (jnp.maximum(i * r8 - 1, 0), 0))
    nxt = lambda w: pl.BlockSpec((SUBLANES, w), lambda i: (jnp.minimum((i + 1) * r8, n8 - 1), 0))
    const = lambda a: pl.BlockSpec(a.shape, lambda i: (0,) * a.ndim)
    return pl.pallas_call(
        functools.partial(_conv_kernel, n_lat=n_lat, seq=seq, ctx_len=ctx_len),
        grid=(nt,),
        in_specs=[main(512), prev(512), nxt(512), main(256), prev(256), nxt(256),
                  const(ws), const(bs), const(wl), const(bl)],
        out_specs=(main(512), main(256)),
        out_shape=(jax.ShapeDtypeStruct((n, 512), F32), jax.ShapeDtypeStruct((n, 256), F32)),
        compiler_params=_params(("parallel",)),
        name="conv",
    )(xbc_raw, xbc_raw, xbc_raw, lx_raw, lx_raw, lx_raw, ws, bs, wl, bl)


def _chunk_maps(batch, seq, ctx_len):
    ncx = ctx_len // CHUNK
    nl = seq // CHUNK
    lat_blocks = batch * nl

    def block(b, c):
        return jnp.where(c < ncx, lat_blocks + b * ncx + c, b * nl + (c - ncx))

    def fwd(b, k):
        return (block(b, k), 0)

    def bwd(b, k):
        c = jnp.where(k < ncx, ncx - 1 - k, ncx + (nl - 1 - (k - ncx)))
        return (block(b, c), 0)

    return fwd, bwd, ncx + nl


def _ssd_kernel(xf_ref, dtf_ref, xb_ref, dtb_ref, dtbias_ref, alog_ref, yf_ref, yb_ref, state_ref):
    k = pl.program_id(1)

    @pl.when(k == 0)
    def _():
        state_ref[...] = jnp.zeros_like(state_ref)

    ri = lax.broadcasted_iota(jnp.int32, (CHUNK, CHUNK), 0)
    ci = lax.broadcasted_iota(jnp.int32, (CHUNK, CHUNK), 1)
    lane_lo = ci < HEAD_DIM
    aneg = -jnp.exp(alog_ref[...])
    dtbias = dtbias_ref[...]

    for d, (x_ref, dt_ref, y_ref) in enumerate(((xf_ref, dtf_ref, yf_ref), (xb_ref, dtb_ref, yb_ref))):
        causal = (ri >= ci) if d == 0 else (ci >= ri)
        tmat = jnp.where(causal, 1.0, 0.0).astype(BF16)
        xs = x_ref[:, 0:256]
        bm = x_ref[:, 256:384]
        cm = x_ref[:, 384:512]
        dtp = _softplus(dt_ref[...] + dtbias)
        acum = _dot3_left(tmat, dtp * aneg)
        acum_t = acum.T
        bt = bm.T.astype(BF16)
        cmb = cm.astype(BF16)
        bmb = bm.astype(BF16)
        tot_row = CHUNK - 1 if d == 0 else 0
        for p in range(2):
            cmask = jnp.where(lane_lo if p == 0 else jnp.logical_not(lane_lo), cmb, jnp.zeros_like(cmb))
            cb = _dot_nt(cmask, bmb)
            cols, dts, ys = [], [], []
            x_pair = xs[:, p * LANES:(p + 1) * LANES]
            for j in range(2):
                col = 4 * d + 2 * p + j
                colb = jnp.broadcast_to(acum[:, col:col + 1], (CHUNK, CHUNK))
                rowb = jnp.broadcast_to(acum_t[col:col + 1, :], (CHUNK, CHUNK))
                cols.append(colb)
                dts.append(jnp.broadcast_to(dtp[:, col:col + 1], (CHUNK, CHUNK)))
            col_pair = jnp.where(lane_lo, cols[0], cols[1])
            dt_pair = jnp.where(lane_lo, dts[0], dts[1])
            xdt = x_pair * dt_pair
            xdt_b = xdt.astype(BF16)
            for j in range(2):
                col = 4 * d + 2 * p + j
                rowb = jnp.broadcast_to(acum_t[col:col + 1, :], (CHUNK, CHUNK))
                decay = jnp.exp(jnp.where(causal, cols[j] - rowb, NEG_INF))
                ys.append(_dot((cb * decay).astype(BF16), xdt_b))
            y_intra = jnp.where(lane_lo, ys[0], ys[1])
            s_old = state_ref[d, p]
            y_inter = _dot(cmask, s_old.astype(BF16)) * jnp.exp(col_pair)
            y_ref[:, p * LANES:(p + 1) * LANES] = y_intra + y_inter
            tot_pair = col_pair[tot_row:tot_row + 1, :]
            to_end = jnp.exp(tot_pair - col_pair)
            state_ref[d, p] = s_old * jnp.exp(tot_pair) + _dot(bt, (xdt * to_end).astype(BF16))


def _ssd(xbc, dt, dtbias_row, alog_row, *, batch, seq, ctx_len):
    n = xbc.shape[0]
    fwd, bwd, steps = _chunk_maps(batch, seq, ctx_len)
    const = lambda a: pl.BlockSpec(a.shape, lambda b, k: (0,) * a.ndim)
    return pl.pallas_call(
        _ssd_kernel,
        grid=(batch, steps),
        in_specs=[pl.BlockSpec((CHUNK, 512), fwd), pl.BlockSpec((CHUNK, LANES), fwd),
                  pl.BlockSpec((CHUNK, 512), bwd), pl.BlockSpec((CHUNK, LANES), bwd),
                  const(dtbias_row), const(alog_row)],
        out_specs=(pl.BlockSpec((CHUNK, 256), fwd), pl.BlockSpec((CHUNK, 256), bwd)),
        out_shape=(jax.ShapeDtypeStruct((n, 256), F32), jax.ShapeDtypeStruct((n, 256), F32)),
        scratch_shapes=[pltpu.VMEM((2, 2, CHUNK, LANES), F32)],
        compiler_params=_params(("parallel", "arbitrary")),
        name="ssd_scan",
    )(xbc, dt, xbc, dt, dtbias_row, alog_row)


def _linear_scan(a, b, reverse):
    n = a.shape[0]
    row = lax.broadcasted_iota(jnp.int32, (n, 1), 0)
    s = 1
    while s < n:
        if reverse:
            ok = row < n - s
            a_sh = jnp.where(ok, pltpu.roll(a, n - s, axis=0), 1.0)
            b_sh = jnp.where(ok, pltpu.roll(b, n - s, axis=0), 0.0)
        else:
            ok = row >= s
            a_sh = jnp.where(ok, pltpu.roll(a, s, axis=0), 1.0)
            b_sh = jnp.where(ok, pltpu.roll(b, s, axis=0), 0.0)
        b = b + a * b_sh
        a = a * a_sh
        s *= 2
    return a, b


def _lru_kernel(uf_ref, ub_ref, wg_ref, bg_ref, lam_ref, hf_ref, hb_ref, carry_ref):
    k = pl.program_id(1)

    @pl.when(k == 0)
    def _():
        carry_ref[...] = jnp.zeros_like(carry_ref)

    for d, (u_ref, h_ref) in enumerate(((uf_ref, hf_ref), (ub_ref, hb_ref))):
        u = u_ref[...]
        gates = _dot(u.astype(BF16), wg_ref[d]) + bg_ref[d]
        r = jax.nn.sigmoid(gates[:, :LRU_WIDTH])
        ig = jax.nn.sigmoid(gates[:, LRU_WIDTH:])
        log_a = -LRU_C * r * _softplus(-lam_ref[d])
        a = jnp.exp(log_a)
        inp = jnp.sqrt(-jnp.tanh(log_a) * (1.0 + a * a)) * (ig * u)
        a_cum, b_cum = _linear_scan(a, inp, reverse=(d == 1))
        h = b_cum + a_cum * carry_ref[d, 0:1, :]
        h_ref[...] = h
        last = 0 if d == 1 else CHUNK - 1
        carry_ref[d, 0:1, :] = h[last:last + 1, :]


def _lru(u, wg, bg, lam, *, batch, seq, ctx_len):
    n = u.shape[0]
    fwd, bwd, steps = _chunk_maps(batch, seq, ctx_len)
    const = lambda a: pl.BlockSpec(a.shape, lambda b, k: (0,) * a.ndim)
    return pl.pallas_call(
        _lru_kernel,
        grid=(batch, steps),
        in_specs=[pl.BlockSpec((CHUNK, LRU_WIDTH), fwd), pl.BlockSpec((CHUNK, LRU_WIDTH), bwd),
                  const(wg), const(bg), const(lam)],
        out_specs=(pl.BlockSpec((CHUNK, LRU_WIDTH), fwd), pl.BlockSpec((CHUNK, LRU_WIDTH), bwd)),
        out_shape=(jax.ShapeDtypeStruct((n, LRU_WIDTH), F32), jax.ShapeDtypeStruct((n, LRU_WIDTH), F32)),
        scratch_shapes=[pltpu.VMEM((2, SUBLANES, LRU_WIDTH), F32)],
        compiler_params=_params(("parallel", "arbitrary")),
        name="lru_scan",
    )(u, u, wg, bg, lam)


def _stack_heads(q, g):
    qf = q.astype(F32)
    lo = g * LANES
    return jnp.concatenate([qf[:, lo:lo + HEAD_DIM], qf[:, lo + HEAD_DIM:lo + LANES]], axis=0).astype(BF16)


def _flash_init(rows, sink_pair):
    if sink_pair is None:
        m = jnp.full((rows, 1), NEG_INF, F32)
        l = jnp.zeros((rows, 1), F32)
    else:
        half = lax.broadcasted_iota(jnp.int32, (rows, 1), 0) < rows // 2
        m = jnp.where(half, sink_pair[0], sink_pair[1]).astype(F32)
        l = jnp.ones((rows, 1), F32)
    return m, l, jnp.zeros((rows, LANES), F32)


def _flash_update(state, q2, kt, v, mask=None):
    m, l, acc = state
    s = _dot(q2, kt)
    if mask is not None:
        s = jnp.where(mask, s, NEG_INF)
    m_new = jnp.maximum(m, jnp.max(s, axis=-1, keepdims=True))
    alpha = jnp.exp(m - m_new)
    p = jnp.exp(s - m_new)
    l = alpha * l + jnp.sum(p, axis=-1, keepdims=True)
    acc = alpha * acc + _dot(p.astype(BF16), v)
    return m_new, l, acc


def _flash_finish(outs, tq):
    pieces = []
    for g, (acc, l) in enumerate(outs):
        o = acc[:, g * HEAD_DIM:(g + 1) * HEAD_DIM] / l
        pieces += [o[:tq], o[tq:]]
    return jnp.concatenate(pieces, axis=1)


def _dense_attn_kernel(*refs, tq, seg_lens, has_sink):
    refs = list(refs)
    sink_ref = refs.pop(0) if has_sink else None
    q_ref = refs.pop(0)
    o_ref = refs.pop()
    segs = [(refs[2 * i], refs[2 * i + 1], n) for i, n in enumerate(seg_lens)]
    q = q_ref[...]
    outs = []
    for g in range(2):
        q2 = _stack_heads(q, g)
        sink_pair = (sink_ref[2 * g], sink_ref[2 * g + 1]) if has_sink else None
        state = _flash_init(2 * tq, sink_pair)
        for kt_ref, v_ref, n_keys in segs:
            if n_keys <= KV_CHUNK:
                state = _flash_update(state, q2, kt_ref[g * HEAD_DIM:(g + 1) * HEAD_DIM, :], v_ref[...])
            else:
                def body(c, st, kt_ref=kt_ref, v_ref=v_ref):
                    off = pl.multiple_of(c * KV_CHUNK, KV_CHUNK)
                    return _flash_update(st, q2, kt_ref[g * HEAD_DIM:(g + 1) * HEAD_DIM, pl.ds(off, KV_CHUNK)],
                                         v_ref[pl.ds(off, KV_CHUNK), :])
                state = lax.fori_loop(0, n_keys // KV_CHUNK, body, state)
        outs.append((state[2], state[1]))
    o_ref[...] = _flash_finish(outs, tq).astype(o_ref.dtype)


def _dense_attn(q, kt, v, sink, *, q_row0, q_len, tq, segs, batch):
    n = q.shape[0]
    qpb = q_len // tq
    q0 = q_row0 // tq
    in_specs, args = [], []
    if sink is not None:
        in_specs.append(pl.BlockSpec(memory_space=pltpu.SMEM))
        args.append(sink)
    in_specs.append(pl.BlockSpec((tq, 256), lambda b, i: (q0 + b * qpb + i, 0)))
    args.append(q)
    for row0, klen in segs:
        k0 = row0 // klen
        in_specs.append(pl.BlockSpec((LANES, klen), lambda b, i, k0=k0: (0, k0 + b)))
        in_specs.append(pl.BlockSpec((klen, LANES), lambda b, i, k0=k0: (k0 + b, 0)))
        args += [kt, v]
    return pl.pallas_call(
        functools.partial(_dense_attn_kernel, tq=tq, seg_lens=tuple(s[1] for s in segs), has_sink=sink is not None),
        grid=(batch, qpb),
        in_specs=in_specs,
        out_specs=pl.BlockSpec((tq, 256), lambda b, i: (b * qpb + i, 0)),
        out_shape=jax.ShapeDtypeStruct((batch * q_len, 256), BF16),
        compiler_params=_params(("parallel", "parallel")),
        name="dense_attn",
    )(*args)


def _window_attn_kernel(sink_ref, q_ref, ktc_ref, vc_ref, ktp_ref, vp_ref, ktm_ref, vm_ref, ktn_ref, vn_ref, o_ref,
                        *, nb):
    n = pl.program_id(1)
    tq = CHUNK
    iq = lax.broadcasted_iota(jnp.int32, (2 * tq, CHUNK), 0) & (tq - 1)
    jk = lax.broadcasted_iota(jnp.int32, (2 * tq, CHUNK), 1)
    mask_prev = jnp.logical_and(jk >= iq, n > 0)
    mask_next = jnp.logical_and(jk <= iq, n < nb - 1)
    q = q_ref[...]
    outs = []
    for g in range(2):
        q2 = _stack_heads(q, g)
        rows = slice(g * HEAD_DIM, (g + 1) * HEAD_DIM)
        state = _flash_init(2 * tq, (sink_ref[2 * g], sink_ref[2 * g + 1]))
        state = _flash_update(state, q2, ktc_ref[rows, :], vc_ref[...])
        state = _flash_update(state, q2, ktm_ref[rows, :], vm_ref[...])
        state = _flash_update(state, q2, ktp_ref[rows, :], vp_ref[...], mask_prev)
        state = _flash_update(state, q2, ktn_ref[rows, :], vn_ref[...], mask_next)
        outs.append((state[2], state[1]))
    o_ref[...] = _flash_finish(outs, tq).astype(o_ref.dtype)


def _window_attn(q, kt, v, sink, *, batch, seq, ctx_len):
    nb = seq // CHUNK
    ctx0 = (batch * seq) // ctx_len

    def kspec(fn):
        return pl.BlockSpec((LANES, CHUNK), lambda b, n: (0, b * nb + fn(n)))

    def vspec(fn):
        return pl.BlockSpec((CHUNK, LANES), lambda b, n: (b * nb + fn(n), 0))

    prev = lambda n: jnp.maximum(n - 1, 0)
    cur = lambda n: n
    nxt = lambda n: jnp.minimum(n + 1, nb - 1)
    return pl.pallas_call(
        functools.partial(_window_attn_kernel, nb=nb),
        grid=(batch, nb),
        in_specs=[pl.BlockSpec(memory_space=pltpu.SMEM),
                  pl.BlockSpec((CHUNK, 256), lambda b, n: (b * nb + n, 0)),
                  pl.BlockSpec((LANES, ctx_len), lambda b, n: (0, ctx0 + b)),
                  pl.BlockSpec((ctx_len, LANES), lambda b, n: (ctx0 + b, 0)),
                  kspec(prev), vspec(prev), kspec(cur), vspec(cur), kspec(nxt), vspec(nxt)],
        out_specs=pl.BlockSpec((CHUNK, 256), lambda b, n: (b * nb + n, 0)),
        out_shape=jax.ShapeDtypeStruct((batch * seq, 256), BF16),
        compiler_params=_params(("parallel", "parallel")),
        name="window_attn",
    )(sink, q, kt, v, kt, v, kt, v, kt, v)


def _gelu_tanh(x):
    return 0.5 * x * (1.0 + jnp.tanh(math.sqrt(2.0 / math.pi) * (x + 0.044715 * (x * x * x))))


def _outproj_kernel(x_ref, gate_ref, gpost_ref, oa_ref, od_ref, yf_ref, yb_ref, xs_ref, z_ref, dsk_ref, gn_ref,
                    hf_ref, hb_ref, lg_ref, w_ref, o_ref):
    y_ssd = (yf_ref[...] + yb_ref[...] + xs_ref[...] * dsk_ref[...]) * _silu(z_ref[...])
    ob = _rms(y_ssd, gn_ref[...])
    oc = (hf_ref[...] + hb_ref[...]) * _gelu_tanh(lg_ref[...])
    y = (_dot(oa_ref[...], w_ref[0:256, :]) + _dot(ob.astype(BF16), w_ref[256:512, :])
         + _dot(oc.astype(BF16), w_ref[512:768, :]) + _dot(od_ref[...], w_ref[768:1024, :]))
    o_ref[...] = x_ref[...] + gate_ref[0] * _rms(y, gpost_ref[...])


def _outproj(xu, mod, gpost, oa, od, yf, yb, xbc, z, dsk, gn, hf, hb, lg, w_out, *, n_rows, n_lat, seq, batch):
    d = xu.shape[1]
    row = lambda w: pl.BlockSpec((TM, w), lambda i: (i, 0))
    const = lambda a: pl.BlockSpec(a.shape, lambda i: (0,) * a.ndim)
    return pl.pallas_call(
        _outproj_kernel,
        grid=(n_rows // TM,),
        in_specs=[row(d), pl.BlockSpec((1, 1, d), _mod_spec(2, n_lat, seq, batch, TM)), const(gpost),
                  row(256), row(256), row(256), row(256), row(256), row(256), const(dsk), const(gn),
                  row(256), row(256), row(256), const(w_out)],
        out_specs=row(d),
        out_shape=jax.ShapeDtypeStruct((n_rows, d), F32),
        compiler_params=_params(("parallel",)),
        name="outproj",
    )(xu, mod, gpost, oa, od, yf, yb, xbc, z, dsk, gn, hf, hb, lg, w_out)


def _router_kernel(x_ref, shift_ref, scale_ref, gpre_ref, rwt_ref, rb_ref,
                   hp_ref, ek_ref, pk_ref, wk_ref, cnt_ref, carry_ref):
    i = pl.program_id(0)

    @pl.when(i == 0)
    def _():
        carry_ref[...] = jnp.zeros_like(carry_ref)

    h = _rms(x_ref[...], gpre_ref[...])
    h = h * (1.0 + scale_ref[0]) + shift_ref[0]
    hb = h.astype(BF16)
    half = h.shape[1] // 2
    bits = lax.bitcast_convert_type(hb.astype(F32), jnp.uint32)
    hp_ref[...] = (bits[:, :half] & jnp.uint32(0xFFFF0000)) | (bits[:, half:] >> 16)

    scores = jax.nn.sigmoid(_dot_nt(rwt_ref[...], hb))
    biased = scores + rb_ref[...]
    gsz = N_EXPERTS // N_EXPERT_GROUPS
    sub = lax.broadcasted_iota(jnp.int32, (gsz, TM), 0)
    blocks, gscore = [], []
    for g in range(N_EXPERT_GROUPS):
        blk = biased[g * gsz:(g + 1) * gsz, :]
        m1 = jnp.max(blk, axis=0, keepdims=True)
        first = jnp.min(jnp.where(blk == m1, sub, gsz), axis=0, keepdims=True)
        m2 = jnp.max(jnp.where(sub == first, -jnp.inf, blk), axis=0, keepdims=True)
        blocks.append(blk)
        gscore.append(m1 + m2)
    masked = []
    for g in range(N_EXPERT_GROUPS):
        rank = jnp.zeros((1, TM), F32)
        for g2 in range(N_EXPERT_GROUPS):
            if g2 == g:
                continue
            beats = (gscore[g2] > gscore[g]) | ((gscore[g2] == gscore[g]) if g2 < g else False)
            rank = rank + jnp.where(beats, 1.0, 0.0)
        masked.append(jnp.where(rank < TOPK_GROUPS, blocks[g], -jnp.inf))
    vals = jnp.concatenate(masked, axis=0)
    eidx = lax.broadcasted_iota(jnp.int32, (N_EXPERTS, TM), 0)
    rank = jnp.zeros((N_EXPERTS, TM), F32)
    for e2 in range(N_EXPERTS):
        rowv = vals[e2:e2 + 1, :]
        beats = (rowv > vals) | ((rowv == vals) & (eidx > e2))
        rank = rank + jnp.where(beats, 1.0, 0.0)
    sel = rank < TOP_K
    self32 = jnp.where(sel, 1.0, 0.0)
    picked = jnp.where(sel, scores, 0.0)
    wdense = picked / jnp.sum(picked, axis=0, keepdims=True) * ROUTED_SCALE

    tr = lax.broadcasted_iota(jnp.int32, (TM, TM), 0)
    tc = lax.broadcasted_iota(jnp.int32, (TM, TM), 1)
    before = jnp.where(tr < tc, 1.0, 0.0).astype(BF16)
    selb = self32.astype(BF16)
    pos = _dot(selb, before) + carry_ref[:, 0:1]
    er = lax.broadcasted_iota(jnp.int32, (N_EXPERTS, N_EXPERTS), 0)
    ec = lax.broadcasted_iota(jnp.int32, (N_EXPERTS, N_EXPERTS), 1)
    lower = jnp.where(ec < er, 1.0, 0.0).astype(BF16)
    ksel = _dot(lower, selb)
    cnt = carry_ref[...] + jnp.sum(self32, axis=1, keepdims=True)
    carry_ref[...] = cnt
    cnt_ref[...] = cnt

    r8 = lax.broadcasted_iota(jnp.int32, (TOP_K, TM), 0)
    ek = jnp.zeros((TOP_K, TM), F32)
    pk = jnp.zeros((TOP_K, TM), F32)
    wk = jnp.zeros((TOP_K, TM), F32)
    ef = eidx.astype(F32)
    for k in range(TOP_K):
        one = sel & (ksel == float(k))
        ek = jnp.where(r8 == k, jnp.sum(jnp.where(one, ef, 0.0), axis=0, keepdims=True), ek)
        pk = jnp.where(r8 == k, jnp.sum(jnp.where(one, pos, 0.0), axis=0, keepdims=True), pk)
        wk = jnp.where(r8 == k, jnp.sum(jnp.where(one, wdense, 0.0), axis=0, keepdims=True), wk)
    ek_ref[...] = ek.astype(jnp.int32)
    pk_ref[...] = pk.astype(jnp.int32)
    wk_ref[...] = wk


def _router(xu, mod, gpre, rwt, rb, *, n_rows, n_lat, seq, batch):
    d = xu.shape[1]
    row = lambda w: pl.BlockSpec((TM, w), lambda i: (i, 0))
    col = pl.BlockSpec((TOP_K, TM), lambda i: (0, i))
    const = lambda a: pl.BlockSpec(a.shape, lambda i: (0,) * a.ndim)
    return pl.pallas_call(
        _router_kernel,
        grid=(n_rows // TM,),
        in_specs=[row(d), pl.BlockSpec((1, 1, d), _mod_spec(3, n_lat, seq, batch, TM)),
                  pl.BlockSpec((1, 1, d), _mod_spec(4, n_lat, seq, batch, TM)),
                  const(gpre), const(rwt), const(rb)],
        out_specs=(row(d // 2), col, col, col, pl.BlockSpec((N_EXPERTS, LANES), lambda i: (0, 0))),
        out_shape=(jax.ShapeDtypeStruct((n_rows, d // 2), jnp.uint32),
                   jax.ShapeDtypeStruct((TOP_K, n_rows), jnp.int32),
                   jax.ShapeDtypeStruct((TOP_K, n_rows), jnp.int32),
                   jax.ShapeDtypeStruct((TOP_K, n_rows), F32),
                   jax.ShapeDtypeStruct((N_EXPERTS, LANES), F32)),
        scratch_shapes=[pltpu.VMEM((N_EXPERTS, LANES), F32)],
        compiler_params=_params(("arbitrary",)),
        name="router",
    )(xu, mod, mod, gpre, rwt, rb)


def _row_copy(src_ref, src_row, dst_ref, dst_row, sem):
    return pltpu.make_async_copy(src_ref.at[pl.ds(src_row, 1)], dst_ref.at[pl.ds(dst_row, 1)], sem)


def _dispatch_kernel(dest_ref, hp_ref, zeros_ref, xs_ref, sem):
    del zeros_ref

    def issue(t, c):
        for k in range(TOP_K):
            _row_copy(hp_ref, t, xs_ref, dest_ref[k, t], sem).start()
        return c

    lax.fori_loop(0, T_DISPATCH, issue, 0)

    def drain(t, c):
        for k in range(TOP_K):
            _row_copy(hp_ref, 0, xs_ref, 0, sem).wait()
        return c

    lax.fori_loop(0, T_DISPATCH, drain, 0)


def _dispatch(dest, hp, n_slots):
    n, half = hp.shape
    zeros = jnp.zeros((n_slots, half), jnp.uint32)
    return pl.pallas_call(
        _dispatch_kernel,
        grid=(n // T_DISPATCH,),
        in_specs=[pl.BlockSpec((TOP_K, T_DISPATCH), lambda i: (0, i), memory_space=pltpu.SMEM),
                  pl.BlockSpec((T_DISPATCH, half), lambda i: (i, 0)),
                  pl.BlockSpec(memory_space=pl.ANY)],
        out_specs=pl.BlockSpec(memory_space=pl.ANY),
        out_shape=jax.ShapeDtypeStruct((n_slots, half), jnp.uint32),
        scratch_shapes=[pltpu.SemaphoreType.DMA(())],
        input_output_aliases={2: 0},
        compiler_params=_params(("arbitrary",)),
        name="moe_dispatch",
    )(dest, hp, zeros)


def _unpack_rows(words):
    hi = lax.bitcast_convert_type(words & jnp.uint32(0xFFFF0000), F32)
    lo = lax.bitcast_convert_type(words << 16, F32)
    return jnp.concatenate([hi, lo], axis=1).astype(BF16)


def _expert_kernel(be_ref, na_ref, xs_ref, wg_ref, wu_ref, wd_ref, ys_ref):
    del be_ref
    i = pl.program_id(0)

    @pl.when(i < na_ref[0])
    def _():
        xb = _unpack_rows(xs_ref[...])
        hid = _silu(_dot(xb, wg_ref[0].astype(BF16))) * _dot(xb, wu_ref[0].astype(BF16))
        ys_ref[...] = _dot(hid.astype(BF16), wd_ref[0].astype(BF16))


def _experts(block_e, n_active, xs, wg, wu, wd):
    n_slots, half = xs.shape
    d = half * 2
    nb = n_slots // BM_EXPERT
    blk = lambda i, be, na: jnp.minimum(i, na[0] - 1)
    grid_spec = pltpu.PrefetchScalarGridSpec(
        num_scalar_prefetch=2,
        grid=(nb,),
        in_specs=[pl.BlockSpec((BM_EXPERT, half), lambda i, be, na: (blk(i, be, na), 0)),
                  pl.BlockSpec((1, d, EXPERT_HIDDEN), lambda i, be, na: (be[blk(i, be, na)], 0, 0)),
                  pl.BlockSpec((1, d, EXPERT_HIDDEN), lambda i, be, na: (be[blk(i, be, na)], 0, 0)),
                  pl.BlockSpec((1, EXPERT_HIDDEN, d), lambda i, be, na: (be[blk(i, be, na)], 0, 0))],
        out_specs=pl.BlockSpec((BM_EXPERT, d), lambda i, be, na: (blk(i, be, na), 0)),
    )
    return pl.pallas_call(
        _expert_kernel,
        grid_spec=grid_spec,
        out_shape=jax.ShapeDtypeStruct((n_slots, d), F32),
        compiler_params=_params(("arbitrary",)),
        name="moe_experts",
    )(block_e, n_active, xs, wg, wu, wd)


def _combine_kernel(dest_ref, ys_ref, wk_ref, hp_ref, x_ref, gate_ref, gpost_ref, sg_ref, su_ref, sd_ref,
                    o_ref, gbuf, sem):
    def issue(t, c):
        for k in range(TOP_K):
            pltpu.make_async_copy(ys_ref.at[pl.ds(dest_ref[k, t], 1)], gbuf.at[k, pl.ds(t, 1)], sem).start()
        return c

    lax.fori_loop(0, T_COMBINE, issue, 0)
    hb = _unpack_rows(hp_ref[...])
    shared = _dot((_silu(_dot(hb, sg_ref[...])) * _dot(hb, su_ref[...])).astype(BF16), sd_ref[...])

    def drain(t, c):
        for k in range(TOP_K):
            pltpu.make_async_copy(ys_ref.at[pl.ds(0, 1)], gbuf.at[k, pl.ds(0, 1)], sem).wait()
        return c

    lax.fori_loop(0, T_COMBINE, drain, 0)
    w = wk_ref[...]
    y = shared
    for k in range(TOP_K):
        y = y + gbuf[k] * w[:, k:k + 1]
    o_ref[...] = x_ref[...] + gate_ref[0] * _rms(y, gpost_ref[...])


def _combine(dest, ys, wkt, hp, xu, mod, gpost, sg, su, sd, *, n_rows, n_lat, seq, batch):
    d = xu.shape[1]
    row = lambda w: pl.BlockSpec((T_COMBINE, w), lambda i: (i, 0))
    const = lambda a: pl.BlockSpec(a.shape, lambda i: (0,) * a.ndim)
    return pl.pallas_call(
        _combine_kernel,
        grid=(n_rows // T_COMBINE,),
        in_specs=[pl.BlockSpec((TOP_K, T_COMBINE), lambda i: (0, i), memory_space=pltpu.SMEM),
                  pl.BlockSpec(memory_space=pl.ANY),
                  row(TOP_K), row(d // 2), row(d),
                  pl.BlockSpec((1, 1, d), _mod_spec(5, n_lat, seq, batch, T_COMBINE)),
                  const(gpost), const(sg), const(su), const(sd)],
        out_specs=row(d),
        out_shape=jax.ShapeDtypeStruct((n_rows, d), F32),
        scratch_shapes=[pltpu.VMEM((TOP_K, T_COMBINE, d), F32), pltpu.SemaphoreType.DMA(())],
        compiler_params=_params(("arbitrary",)),
        name="moe_combine",
    )(dest, ys, wkt, hp, xu, mod, gpost, sg, su, sd)


def _deinterleave(w):
    cols = w.shape[-1]
    perm = jnp.concatenate([jnp.arange(0, HEAD_DIM, 2), jnp.arange(1, HEAD_DIM, 2)])
    idx = (jnp.arange(cols // HEAD_DIM)[:, None] * HEAD_DIM + perm[None, :]).reshape(-1)
    return w[..., idx]


def _pad_in_proj(w_in):
    d = w_in.shape[0]
    o = 0
    parts = {}
    for name, width in (("qa", 256), ("ka", 128), ("va", 128), ("z", 256), ("xs", 256), ("bm", 128), ("cm", 128),
                        ("dtf", 4), ("dtb", 4), ("lx", 256), ("lg", 256), ("qd", 256), ("kd", 128), ("vd", 128)):
        parts[name] = w_in[:, o:o + width]
        o += width
    dt = jnp.concatenate([parts["dtf"], parts["dtb"], jnp.zeros((d, LANES - 8), w_in.dtype)], axis=1)
    cols = [_deinterleave(parts["qa"]), _deinterleave(parts["ka"]), parts["va"],
            _deinterleave(parts["qd"]), _deinterleave(parts["kd"]), parts["vd"],
            parts["z"], parts["xs"], parts["bm"], parts["cm"], dt, parts["lx"], parts["lg"]]
    return jnp.concatenate(cols, axis=1).astype(BF16)


def _rope_tables(seq):
    t = jnp.arange(seq)
    rowp = (t // GRID_W).astype(F32)
    colp = (t % GRID_W).astype(F32)
    axis_dim = HEAD_DIM // 2
    inv_freq = ROPE_THETA ** (-jnp.arange(0, axis_dim, 2, dtype=F32) / axis_dim)
    ang = jnp.concatenate([rowp[:, None] * inv_freq, colp[:, None] * inv_freq], axis=-1)
    cos, sin = jnp.cos(ang), jnp.sin(ang)
    cos_h = jnp.concatenate([cos, cos], axis=-1)
    sin_h = jnp.concatenate([-sin, sin], axis=-1)
    return jnp.tile(cos_h, (1, 4)), jnp.tile(sin_h, (1, 4))


def _block_diag(w):
    nb, bd, _ = w.shape
    eye = jnp.eye(nb, dtype=w.dtype)
    return (eye[:, None, :, None] * w[:, :, None, :]).reshape(nb * bd, nb * bd)


def _lane_row(fwd, bwd):
    return jnp.concatenate([fwd, bwd, jnp.zeros((LANES - 8,), F32)]).reshape(1, LANES)


def kernel(x, c, ctx, c_ctx, w_ada, b_ada, g_mix_pre, g_mix_post, g_ffn_pre, g_ffn_post, w_in, w_out, a_sink,
           ssd_conv_w, ssd_conv_b, ssd_dt_bias, ssd_a_log, ssd_d, ssd_norm, lru_conv_w, lru_conv_b, lru_w_a,
           lru_b_a, lru_w_i, lru_b_i, lru_lambda, d_q_norm, d_k_norm, router_w, router_bias, exp_w_gate,
           exp_w_up, exp_w_down, sh_w_gate, sh_w_up, sh_w_down):
    batch, seq, d = x.shape
    ctx_len = ctx.shape[1]
    depth = w_ada.shape[0]
    n_lat = batch * seq
    n_ctx = batch * ctx_len
    n_all = n_lat + n_ctx
    assert seq % TM == 0 and n_ctx % TM == 0 and seq % T_CONV == 0 and ctx_len % T_CONV == 0
    assert ctx_len <= KV_CHUNK and seq % KV_CHUNK == 0 and seq % TQ_GLOBAL == 0 and batch + 1 <= SUBLANES

    xu = jnp.concatenate([x.reshape(n_lat, d), ctx.reshape(n_ctx, d)], axis=0)
    cin = jnp.concatenate([c, c_ctx[None, :], jnp.zeros((SUBLANES - batch - 1, d), F32)], axis=0)
    mod_all = _adaln(cin, w_ada, b_ada)
    cos_t, sin_t = _rope_tables(seq)
    hm = jnp.kron(jnp.eye(4, dtype=F32), jnp.full((HEAD_DIM, HEAD_DIM), 1.0 / HEAD_DIM, F32)).astype(BF16)

    for l in range(depth):
        with_ctx = l < depth - 1
        mod = mod_all[l].reshape(SUBLANES * 6, 1, d)
        gq = jnp.tile(_deinterleave(d_q_norm[l]), 4).reshape(1, 256)
        gk = jnp.tile(_deinterleave(d_k_norm[l]), 2).reshape(1, LANES)
        qa, kat, va, qd, kdt, vd, z, xbc_raw, dt, lx_raw, lg = _inproj(
            xu, mod, g_mix_pre[l].reshape(1, d), _pad_in_proj(w_in[l]), cos_t, sin_t, gq, gk, hm,
            n_lat=n_lat, seq=seq, batch=batch)

        xbc, lu = _conv(xbc_raw, lx_raw, ssd_conv_w[l], ssd_conv_b[l].reshape(1, -1),
                        lru_conv_w[l], lru_conv_b[l].reshape(1, -1), n_lat=n_lat, seq=seq, ctx_len=ctx_len)
        yf, yb = _ssd(xbc, dt, _lane_row(ssd_dt_bias[l, 0], ssd_dt_bias[l, 1]),
                      _lane_row(ssd_a_log[l, 0], ssd_a_log[l, 1]), batch=batch, seq=seq, ctx_len=ctx_len)
        wg = jnp.stack([jnp.concatenate([_block_diag(lru_w_a[l, dd]), _block_diag(lru_w_i[l, dd])], axis=1)
                        for dd in range(2)]).astype(BF16)
        bg = jnp.concatenate([lru_b_a[l], lru_b_i[l]], axis=1).reshape(2, 1, 2 * LRU_WIDTH)
        hf, hb = _lru(lu, wg, bg, lru_lambda[l].reshape(2, 1, LRU_WIDTH), batch=batch, seq=seq, ctx_len=ctx_len)

        oa = _window_attn(qa, kat, va, a_sink[l], batch=batch, seq=seq, ctx_len=ctx_len)
        od = _dense_attn(qd, kdt, vd, None, q_row0=0, q_len=seq, tq=TQ_GLOBAL,
                         segs=[(n_lat, ctx_len), (0, seq)], batch=batch)
        if with_ctx:
            oa_c = _dense_attn(qa, kat, va, a_sink[l], q_row0=n_lat, q_len=ctx_len, tq=ctx_len,
                               segs=[(n_lat, ctx_len)], batch=batch)
            od_c = _dense_attn(qd, kdt, vd, None, q_row0=n_lat, q_len=ctx_len, tq=ctx_len,
                               segs=[(n_lat, ctx_len)], batch=batch)
            oa = jnp.concatenate([oa, oa_c], axis=0)
            od = jnp.concatenate([od, od_c], axis=0)
        n_rows = n_all if with_ctx else n_lat

        dsk = jnp.repeat(ssd_d[l], HEAD_DIM).reshape(1, 256)
        xu_mid = _outproj(xu, mod, g_mix_post[l].reshape(1, d), oa, od, yf, yb, xbc, z, dsk,
                          ssd_norm[l].reshape(1, 256), hf, hb, lg, w_out[l].astype(BF16),
                          n_rows=n_rows, n_lat=n_lat, seq=seq, batch=batch)

        hp, ek, pk, wk, cnt = _router(xu_mid, mod, g_ffn_pre[l].reshape(1, d), router_w[l].T.astype(BF16),
                                      router_bias[l].reshape(N_EXPERTS, 1), n_rows=n_rows, n_lat=n_lat,
                                      seq=seq, batch=batch)
        counts = cnt[:, 0].astype(jnp.int32)
        padded = (counts + BM_EXPERT - 1) // BM_EXPERT * BM_EXPERT
        padded_end = jnp.cumsum(padded)
        offs = padded_end - padded
        n_blocks = (n_rows * TOP_K) // BM_EXPERT + N_EXPERTS
        n_active = (padded_end[-1] // BM_EXPERT).astype(jnp.int32).reshape(1)
        block_e = jnp.minimum(jnp.searchsorted(padded_end, jnp.arange(n_blocks) * BM_EXPERT, side='right'),
                              N_EXPERTS - 1).astype(jnp.int32)
        dest = offs[ek] + pk
        xs = _dispatch(dest, hp, n_blocks * BM_EXPERT)
        ys = _experts(block_e, n_active, xs, exp_w_gate[l], exp_w_up[l], exp_w_down[l])
        xu = _combine(dest, ys, wk.T, hp, xu_mid, mod, g_ffn_post[l].reshape(1, d), sh_w_gate[l].astype(BF16),
                      sh_w_up[l].astype(BF16), sh_w_down[l].astype(BF16),
                      n_rows=n_rows, n_lat=n_lat, seq=seq, batch=batch)
    return xu[:n_lat].reshape(batch, seq, d)
```

```python
import functools
import math

import jax
import jax.numpy as jnp
from jax import lax
from jax.experimental import pallas as pl
from jax.experimental.pallas import tpu as pltpu

F32 = jnp.float32
BF16 = jnp.bfloat16

HEAD_DIM = 64
GRID_W = 64
ROPE_THETA = 10000.0
NORM_EPS = 1e-6
NEG_INF = -1e30
A_HEADS, A_KV_HEADS, WINDOW = 4, 2, 128
SSD_HEADS, SSD_GROUPS, SSD_STATE, SSD_CONV = 4, 2, 64, 4
LRU_WIDTH, LRU_BLOCKS, LRU_CONV, LRU_C = 256, 4, 4, 8.0
D_HEADS, D_KV_HEADS = 4, 2
N_EXPERTS, N_EXPERT_GROUPS, TOPK_GROUPS, TOP_K = 64, 8, 4, 8
EXPERT_HIDDEN, SHARED_HIDDEN = 256, 256
ROUTED_SCALE = 2.5

LANES = 128
SUBLANES = 8

TM = 512
T_CONV = 256
CHUNK = 128
TQ_GLOBAL = 256
KV_CHUNK = 512
T_DISPATCH = 256
T_COMBINE = 128
BM_EXPERT = 256
VMEM_LIMIT = 48 * 1024 * 1024

C_QA, C_KA, C_VA = 0, 256, 384
C_QD, C_KD, C_VD = 512, 768, 896
C_Z, C_XBC, C_DT = 1024, 1280, 1792
C_LX, C_LG = 1920, 2176
NP_IN = 2432


def _dot(a, b):
    return jnp.dot(a, b, preferred_element_type=F32)


def _dot_nt(a, b):
    return lax.dot_general(a, b, (((1,), (1,)), ((), ())), preferred_element_type=F32)


def _dot3(a, b):
    a1 = a.astype(BF16)
    r1 = a - a1.astype(F32)
    a2 = r1.astype(BF16)
    a3 = (r1 - a2.astype(F32)).astype(BF16)
    return _dot(a1, b) + _dot(a2, b) + _dot(a3, b)


def _dot3_left(a, b):
    b1 = b.astype(BF16)
    r1 = b - b1.astype(F32)
    b2 = r1.astype(BF16)
    b3 = (r1 - b2.astype(F32)).astype(BF16)
    return _dot(a, b1) + _dot(a, b2) + _dot(a, b3)


def _silu(x):
    return x * jax.nn.sigmoid(x)


def _softplus(x):
    return jnp.maximum(x, 0.0) + jnp.log1p(jnp.exp(-jnp.abs(x)))


def _rms(x, gain):
    return x * lax.rsqrt(jnp.mean(x * x, axis=-1, keepdims=True) + NORM_EPS) * gain


def _params(sem=None):
    return pltpu.CompilerParams(dimension_semantics=sem, vmem_limit_bytes=VMEM_LIMIT)


def _adaln_kernel(c_ref, w_ref, b_ref, o_ref):
    s = _silu(c_ref[...])
    o_ref[0] = _dot(s.astype(BF16), w_ref[0].astype(BF16)) + b_ref[0]


def _adaln(cin, w_ada, b_ada):
    depth, d, n6 = w_ada.shape
    tn = 1024
    return pl.pallas_call(
        _adaln_kernel,
        grid=(depth, n6 // tn),
        in_specs=[pl.BlockSpec((SUBLANES, d), lambda l, j: (0, 0)),
                  pl.BlockSpec((1, d, tn), lambda l, j: (l, 0, j)),
                  pl.BlockSpec((1, 1, tn), lambda l, j: (l, 0, j))],
        out_specs=pl.BlockSpec((1, SUBLANES, tn), lambda l, j: (l, 0, j)),
        out_shape=jax.ShapeDtypeStruct((depth, SUBLANES, n6), F32),
        compiler_params=_params(("parallel", "parallel")),
        name="adaln",
    )(cin, w_ada, b_ada.reshape(depth, 1, n6))


def _swap_halves(t):
    w = t.shape[1]
    lane = lax.broadcasted_iota(jnp.int32, (1, w), 1)
    first = (lane & 32) == 0
    return jnp.where(first, pltpu.roll(t, w - 32, axis=1), pltpu.roll(t, 32, axis=1))


def _inproj_kernel(x_ref, shift_ref, scale_ref, gpre_ref, w_ref, cos_ref, sin_ref, gq_ref, gk_ref, hm_ref,
                   qa_ref, kat_ref, va_ref, qd_ref, kdt_ref, vd_ref, z_ref, xbc_ref, dt_ref, lx_ref, lg_ref,
                   *, n_lat):
    i = pl.program_id(0)
    is_lat = i * TM < n_lat
    h = _rms(x_ref[...], gpre_ref[...])
    h = h * (1.0 + scale_ref[0]) + shift_ref[0]
    hb = h.astype(BF16)

    def sec(a, b):
        return _dot(hb, w_ref[:, a:b])

    cos = jnp.where(is_lat, cos_ref[...], 1.0)
    sin = jnp.where(is_lat, sin_ref[...], 0.0)

    def rope(t):
        w = t.shape[1]
        return t * cos[:, :w] + _swap_halves(t) * sin[:, :w]

    def head_norm(t, gain):
        w = t.shape[1]
        ms = _dot3(t * t, hm_ref[:w, :w])
        return t * lax.rsqrt(ms + NORM_EPS) * gain

    scale = HEAD_DIM ** -0.5
    qa_ref[...] = (rope(sec(C_QA, C_KA)) * scale).astype(BF16)
    kat_ref[...] = rope(sec(C_KA, C_VA)).T.astype(BF16)
    va_ref[...] = sec(C_VA, C_QD).astype(BF16)
    qd_ref[...] = (rope(head_norm(sec(C_QD, C_KD), gq_ref[...])) * scale).astype(BF16)
    kdt_ref[...] = rope(head_norm(sec(C_KD, C_VD), gk_ref[...])).T.astype(BF16)
    vd_ref[...] = sec(C_VD, C_Z).astype(BF16)
    z_ref[...] = sec(C_Z, C_XBC)
    xbc_ref[...] = sec(C_XBC, C_DT)
    dt_ref[...] = sec(C_DT, C_LX)
    lx_ref[...] = sec(C_LX, C_LG)
    lg_ref[...] = sec(C_LG, NP_IN)


def _mod_spec(chunk, n_lat, seq, batch, tile):
    def imap(i):
        row0 = i * tile
        seg = jnp.where(row0 < n_lat, row0 // seq, batch)
        return (seg * 6 + chunk, 0, 0)
    return imap


def _inproj(xu, mod, gpre, w_pad, cos_t, sin_t, gq, gk, hm, *, n_lat, seq, batch):
    n, d = xu.shape
    nt = n // TM
    spt = seq // TM
    row = lambda w: pl.BlockSpec((TM, w), lambda i: (i, 0))
    colT = pl.BlockSpec((LANES, TM), lambda i: (0, i))
    const = lambda a: pl.BlockSpec(a.shape, lambda i: (0,) * a.ndim)
    out_shapes = (
        jax.ShapeDtypeStruct((n, 256), BF16), jax.ShapeDtypeStruct((LANES, n), BF16),
        jax.ShapeDtypeStruct((n, LANES), BF16),
        jax.ShapeDtypeStruct((n, 256), BF16), jax.ShapeDtypeStruct((LANES, n), BF16),
        jax.ShapeDtypeStruct((n, LANES), BF16),
        jax.ShapeDtypeStruct((n, 256), F32), jax.ShapeDtypeStruct((n, 512), F32),
        jax.ShapeDtypeStruct((n, LANES), F32), jax.ShapeDtypeStruct((n, 256), F32),
        jax.ShapeDtypeStruct((n, 256), F32))
    return pl.pallas_call(
        functools.partial(_inproj_kernel, n_lat=n_lat),
        grid=(nt,),
        in_specs=[row(d),
                  pl.BlockSpec((1, 1, d), _mod_spec(0, n_lat, seq, batch, TM)),
                  pl.BlockSpec((1, 1, d), _mod_spec(1, n_lat, seq, batch, TM)),
                  const(gpre), const(w_pad),
                  pl.BlockSpec((TM, 256), lambda i: (i % spt, 0)),
                  pl.BlockSpec((TM, 256), lambda i: (i % spt, 0)),
                  const(gq), const(gk), const(hm)],
        out_specs=(row(256), colT, row(LANES), row(256), colT, row(LANES),
                   row(256), row(512), row(LANES), row(256), row(256)),
        out_shape=out_shapes,
        compiler_params=_params(("parallel",)),
        name="inproj",
    )(xu, mod, mod, gpre, w_pad, cos_t, sin_t, gq, gk, hm)


def _conv_kernel(xs_ref, xsp_ref, xsn_ref, xl_ref, xlp_ref, xln_ref, ws_ref, bs_ref, wl_ref, bl_ref,
                 os_ref, ol_ref, *, n_lat, seq, ctx_len):
    i = pl.program_id(0)
    row0 = i * T_CONV
    pos = jnp.where(row0 < n_lat, row0 % seq, (row0 - n_lat) % ctx_len)
    slen = jnp.where(row0 < n_lat, seq, ctx_len)
    first = pos == 0
    last = pos + T_CONV == slen
    row = lax.broadcasted_iota(jnp.int32, (T_CONV, 1), 0)

    def conv(x, prev, nxt, w, b):
        pm = jnp.where(first, 0.0, prev)
        nx = jnp.where(last, 0.0, nxt)
        xm1 = jnp.where(row == 0, pm[7:8, :], pltpu.roll(x, 1, axis=0))
        xm2 = jnp.where(row == 0, pm[6:7, :], jnp.where(row == 1, pm[7:8, :], pltpu.roll(x, 2, axis=0)))
        xp1 = jnp.where(row == T_CONV - 1, nx[0:1, :], pltpu.roll(x, T_CONV - 1, axis=0))
        return w[0:1, :] * xm2 + w[1:2, :] * xm1 + w[2:3, :] * x + w[3:4, :] * xp1 + b

    os_ref[...] = _silu(conv(xs_ref[...], xsp_ref[...], xsn_ref[...], ws_ref[...], bs_ref[...]))
    ol_ref[...] = conv(xl_ref[...], xlp_ref[...], xln_ref[...], wl_ref[...], bl_ref[...])


def _conv(xbc_raw, lx_raw, ws, bs, wl, bl, *, n_lat, seq, ctx_len):
    n = xbc_raw.shape[0]
    nt = n // T_CONV
    r8 = T_CONV // SUBLANES
    n8 = n // SUBLANES
    main = lambda w: pl.BlockSpec((T_CONV, w), lambda i: (i, 0))
    prev = lambda w: pl.BlockSpec((SUBLANES, w), lambda i: (jnp.maximum(i * r8 - 1, 0), 0))
    nxt = lambda w: pl.BlockSpec((SUBLANES, w), lambda i: (jnp.minimum((i + 1) * r8, n8 - 1), 0))
    const = lambda a: pl.BlockSpec(a.shape, lambda i: (0,) * a.ndim)
    return pl.pallas_call(
        functools.partial(_conv_kernel, n_lat=n_lat, seq=seq, ctx_len=ctx_len),
        grid=(nt,),
        in_specs=[main(512), prev(512), nxt(512), main(256), prev(256), nxt(256),
                  const(ws), const(bs), const(wl), const(bl)],
        out_specs=(main(512), main(256)),
        out_shape=(jax.ShapeDtypeStruct((n, 512), F32), jax.ShapeDtypeStruct((n, 256), F32)),
        compiler_params=_params(("parallel",)),
        name="conv",
    )(xbc_raw, xbc_raw, xbc_raw, lx_raw, lx_raw, lx_raw, ws, bs, wl, bl)


def _chunk_maps(batch, seq, ctx_len):
    ncx = ctx_len // CHUNK
    nl = seq // CHUNK
    lat_blocks = batch * nl

    def block(b, c):
        return jnp.where(c < ncx, lat_blocks + b * ncx + c, b * nl + (c - ncx))

    def fwd(b, k):
        return (block(b, k), 0)

    def bwd(b, k):
        c = jnp.where(k < ncx, ncx - 1 - k, ncx + (nl - 1 - (k - ncx)))
        return (block(b, c), 0)

    return fwd, bwd, ncx + nl


def _ssd_kernel(xf_ref, dtf_ref, xb_ref, dtb_ref, dtbias_ref, alog_ref, yf_ref, yb_ref, state_ref):
    k = pl.program_id(1)

    @pl.when(k == 0)
    def _():
        state_ref[...] = jnp.zeros_like(state_ref)

    ri = lax.broadcasted_iota(jnp.int32, (CHUNK, CHUNK), 0)
    ci = lax.broadcasted_iota(jnp.int32, (CHUNK, CHUNK), 1)
    lane_lo = ci < HEAD_DIM
    aneg = -jnp.exp(alog_ref[...])
    dtbias = dtbias_ref[...]

    for d, (x_ref, dt_ref, y_ref) in enumerate(((xf_ref, dtf_ref, yf_ref), (xb_ref, dtb_ref, yb_ref))):
        causal = (ri >= ci) if d == 0 else (ci >= ri)
        tmat = jnp.where(causal, 1.0, 0.0).astype(BF16)
        xs = x_ref[:, 0:256]
        bm = x_ref[:, 256:384]
        cm = x_ref[:, 384:512]
        dtp = _softplus(dt_ref[...] + dtbias)
        acum = _dot3_left(tmat, dtp * aneg)
        acum_t = acum.T
        bt = bm.T.astype(BF16)
        cmb = cm.astype(BF16)
        bmb = bm.astype(BF16)
        tot_row = CHUNK - 1 if d == 0 else 0
        for p in range(2):
            cmask = jnp.where(lane_lo if p == 0 else jnp.logical_not(lane_lo), cmb, jnp.zeros_like(cmb))
            cb = _dot_nt(cmask, bmb)
            cols, dts, ys = [], [], []
            x_pair = xs[:, p * LANES:(p + 1) * LANES]
            for j in range(2):
                col = 4 * d + 2 * p + j
                colb = jnp.broadcast_to(acum[:, col:col + 1], (CHUNK, CHUNK))
                rowb = jnp.broadcast_to(acum_t[col:col + 1, :], (CHUNK, CHUNK))
                cols.append(colb)
                dts.append(jnp.broadcast_to(dtp[:, col:col + 1], (CHUNK, CHUNK)))
            col_pair = jnp.where(lane_lo, cols[0], cols[1])
            dt_pair = jnp.where(lane_lo, dts[0], dts[1])
            xdt = x_pair * dt_pair
            xdt_b = xdt.astype(BF16)
            for j in range(2):
                col = 4 * d + 2 * p + j
                rowb = jnp.broadcast_to(acum_t[col:col + 1, :], (CHUNK, CHUNK))
                decay = jnp.exp(jnp.where(causal, cols[j] - rowb, NEG_INF))
                ys.append(_dot((cb * decay).astype(BF16), xdt_b))
            y_intra = jnp.where(lane_lo, ys[0], ys[1])
            s_old = state_ref[d, p]
            y_inter = _dot(cmask, s_old.astype(BF16)) * jnp.exp(col_pair)
            y_ref[:, p * LANES:(p + 1) * LANES] = y_intra + y_inter
            tot_pair = col_pair[tot_row:tot_row + 1, :]
            to_end = jnp.exp(tot_pair - col_pair)
            state_ref[d, p] = s_old * jnp.exp(tot_pair) + _dot(bt, (xdt * to_end).astype(BF16))


def _ssd(xbc, dt, dtbias_row, alog_row, *, batch, seq, ctx_len):
    n = xbc.shape[0]
    fwd, bwd, steps = _chunk_maps(batch, seq, ctx_len)
    const = lambda a: pl.BlockSpec(a.shape, lambda b, k: (0,) * a.ndim)
    return pl.pallas_call(
        _ssd_kernel,
        grid=(batch, steps),
        in_specs=[pl.BlockSpec((CHUNK, 512), fwd), pl.BlockSpec((CHUNK, LANES), fwd),
                  pl.BlockSpec((CHUNK, 512), bwd), pl.BlockSpec((CHUNK, LANES), bwd),
                  const(dtbias_row), const(alog_row)],
        out_specs=(pl.BlockSpec((CHUNK, 256), fwd), pl.BlockSpec((CHUNK, 256), bwd)),
        out_shape=(jax.ShapeDtypeStruct((n, 256), F32), jax.ShapeDtypeStruct((n, 256), F32)),
        scratch_shapes=[pltpu.VMEM((2, 2, CHUNK, LANES), F32)],
        compiler_params=_params(("parallel", "arbitrary")),
        name="ssd_scan",
    )(xbc, dt, xbc, dt, dtbias_row, alog_row)


def _linear_scan(a, b, reverse):
    n = a.shape[0]
    row = lax.broadcasted_iota(jnp.int32, (n, 1), 0)
    s = 1
    while s < n:
        if reverse:
            ok = row < n - s
            a_sh = jnp.where(ok, pltpu.roll(a, n - s, axis=0), 1.0)
            b_sh = jnp.where(ok, pltpu.roll(b, n - s, axis=0), 0.0)
        else:
            ok = row >= s
            a_sh = jnp.where(ok, pltpu.roll(a, s, axis=0), 1.0)
            b_sh = jnp.where(ok, pltpu.roll(b, s, axis=0), 0.0)
        b = b + a * b_sh
        a = a * a_sh
        s *= 2
    return a, b


def _lru_kernel(uf_ref, ub_ref, wg_ref, bg_ref, lam_ref, hf_ref, hb_ref, carry_ref):
    k = pl.program_id(1)

    @pl.when(k == 0)
    def _():
        carry_ref[...] = jnp.zeros_like(carry_ref)

    for d, (u_ref, h_ref) in enumerate(((uf_ref, hf_ref), (ub_ref, hb_ref))):
        u = u_ref[...]
        gates = _dot(u.astype(BF16), wg_ref[d]) + bg_ref[d]
        r = jax.nn.sigmoid(gates[:, :LRU_WIDTH])
        ig = jax.nn.sigmoid(gates[:, LRU_WIDTH:])
        log_a = -LRU_C * r * _softplus(-lam_ref[d])
        a = jnp.exp(log_a)
        inp = jnp.sqrt(-jnp.tanh(log_a) * (1.0 + a * a)) * (ig * u)
        a_cum, b_cum = _linear_scan(a, inp, reverse=(d == 1))
        h = b_cum + a_cum * carry_ref[d, 0:1, :]
        h_ref[...] = h
        last = 0 if d == 1 else CHUNK - 1
        carry_ref[d, 0:1, :] = h[last:last + 1, :]


def _lru(u, wg, bg, lam, *, batch, seq, ctx_len):
    n = u.shape[0]
    fwd, bwd, steps = _chunk_maps(batch, seq, ctx_len)
    const = lambda a: pl.BlockSpec(a.shape, lambda b, k: (0,) * a.ndim)
    return pl.pallas_call(
        _lru_kernel,
        grid=(batch, steps),
        in_specs=[pl.BlockSpec((CHUNK, LRU_WIDTH), fwd), pl.BlockSpec((CHUNK, LRU_WIDTH), bwd),
                  const(wg), const(bg), const(lam)],
        out_specs=(pl.BlockSpec((CHUNK, LRU_WIDTH), fwd), pl.BlockSpec((CHUNK, LRU_WIDTH), bwd)),
        out_shape=(jax.ShapeDtypeStruct((n, LRU_WIDTH), F32), jax.ShapeDtypeStruct((n, LRU_WIDTH), F32)),
        scratch_shapes=[pltpu.VMEM((2, SUBLANES, LRU_WIDTH), F32)],
        compiler_params=_params(("parallel", "arbitrary")),
        name="lru_scan",
    )(u, u, wg, bg, lam)


def _stack_heads(q, g):
    qf = q.astype(F32)
    lo = g * LANES
    return jnp.concatenate([qf[:, lo:lo + HEAD_DIM], qf[:, lo + HEAD_DIM:lo + LANES]], axis=0).astype(BF16)


def _value_lanes(g):
    lane = lax.broadcasted_iota(jnp.int32, (1, LANES), 1)
    return (lane < HEAD_DIM) if g == 0 else (lane >= HEAD_DIM)


def _aug_values(v, g):
    return jnp.where(_value_lanes(g), v, jnp.ones_like(v))


def _flash_init(rows, g, sink_pair):
    if sink_pair is None:
        return jnp.full((rows, 1), NEG_INF, F32), jnp.zeros((rows, LANES), F32)
    half = lax.broadcasted_iota(jnp.int32, (rows, 1), 0) < rows // 2
    m = jnp.where(half, sink_pair[0], sink_pair[1]).astype(F32)
    acc = jnp.broadcast_to(jnp.where(_value_lanes(g), 0.0, 1.0), (rows, LANES))
    return m, acc


def _flash_update(state, q2, kt, v_aug, mask=None):
    m, acc = state
    s = _dot(q2, kt)
    if mask is not None:
        s = jnp.where(mask, s, NEG_INF)
    m_new = jnp.maximum(m, jnp.max(s, axis=-1, keepdims=True))
    p = jnp.exp(s - m_new).astype(BF16)
    acc = jnp.exp(m - m_new) * acc + _dot(p, v_aug)
    return m_new, acc


def _flash_finish(states, tq):
    pieces = []
    for g, (_, acc) in enumerate(states):
        den = (1 - g) * HEAD_DIM
        o = acc[:, g * HEAD_DIM:(g + 1) * HEAD_DIM] / acc[:, den:den + 1]
        pieces += [o[:tq], o[tq:]]
    return jnp.concatenate(pieces, axis=1)


def _group_rows(g):
    return slice(g * HEAD_DIM, (g + 1) * HEAD_DIM)


def _dense_attn_kernel(*refs, tq, seg_lens, has_sink):
    refs = list(refs)
    sink_ref = refs.pop(0) if has_sink else None
    q_ref = refs.pop(0)
    o_ref = refs.pop()
    segs = [(refs[2 * i], refs[2 * i + 1], n) for i, n in enumerate(seg_lens)]
    q = q_ref[...]
    q2 = [_stack_heads(q, g) for g in range(2)]
    states = tuple(_flash_init(2 * tq, g, (sink_ref[2 * g], sink_ref[2 * g + 1]) if has_sink else None)
                   for g in range(2))
    for kt_ref, v_ref, n_keys in segs:
        if n_keys <= KV_CHUNK:
            v = v_ref[...]
            states = tuple(_flash_update(states[g], q2[g], kt_ref[_group_rows(g), :], _aug_values(v, g))
                           for g in range(2))
        else:
            def body(c, sts, kt_ref=kt_ref, v_ref=v_ref):
                off = pl.multiple_of(c * KV_CHUNK, KV_CHUNK)
                v = v_ref[pl.ds(off, KV_CHUNK), :]
                return tuple(_flash_update(sts[g], q2[g], kt_ref[_group_rows(g), pl.ds(off, KV_CHUNK)],
                                           _aug_values(v, g)) for g in range(2))
            states = lax.fori_loop(0, n_keys // KV_CHUNK, body, states, unroll=2)
    o_ref[...] = _flash_finish(states, tq).astype(o_ref.dtype)


def _dense_attn(q, kt, v, sink, *, q_row0, q_len, tq, segs, batch):
    n = q.shape[0]
    qpb = q_len // tq
    q0 = q_row0 // tq
    in_specs, args = [], []
    if sink is not None:
        in_specs.append(pl.BlockSpec(memory_space=pltpu.SMEM))
        args.append(sink)
    in_specs.append(pl.BlockSpec((tq, 256), lambda b, i: (q0 + b * qpb + i, 0)))
    args.append(q)
    for row0, klen in segs:
        k0 = row0 // klen
        in_specs.append(pl.BlockSpec((LANES, klen), lambda b, i, k0=k0: (0, k0 + b)))
        in_specs.append(pl.BlockSpec((klen, LANES), lambda b, i, k0=k0: (k0 + b, 0)))
        args += [kt, v]
    return pl.pallas_call(
        functools.partial(_dense_attn_kernel, tq=tq, seg_lens=tuple(s[1] for s in segs), has_sink=sink is not None),
        grid=(batch, qpb),
        in_specs=in_specs,
        out_specs=pl.BlockSpec((tq, 256), lambda b, i: (b * qpb + i, 0)),
        out_shape=jax.ShapeDtypeStruct((batch * q_len, 256), BF16),
        compiler_params=_params(("parallel", "parallel")),
        name="dense_attn",
    )(*args)


def _window_attn_kernel(sink_ref, q_ref, ktc_ref, vc_ref, ktp_ref, vp_ref, ktm_ref, vm_ref, ktn_ref, vn_ref, o_ref,
                        *, nb):
    n = pl.program_id(1)
    tq = CHUNK
    iq = lax.broadcasted_iota(jnp.int32, (2 * tq, CHUNK), 0) & (tq - 1)
    jk = lax.broadcasted_iota(jnp.int32, (2 * tq, CHUNK), 1)
    mask_prev = jnp.logical_and(jk >= iq, n > 0)
    mask_next = jnp.logical_and(jk <= iq, n < nb - 1)
    q = q_ref[...]
    states = []
    for g in range(2):
        q2 = _stack_heads(q, g)
        rows = _group_rows(g)
        state = _flash_init(2 * tq, g, (sink_ref[2 * g], sink_ref[2 * g + 1]))
        state = _flash_update(state, q2, ktc_ref[rows, :], _aug_values(vc_ref[...], g))
        state = _flash_update(state, q2, ktm_ref[rows, :], _aug_values(vm_ref[...], g))
        state = _flash_update(state, q2, ktp_ref[rows, :], _aug_values(vp_ref[...], g), mask_prev)
        state = _flash_update(state, q2, ktn_ref[rows, :], _aug_values(vn_ref[...], g), mask_next)
        states.append(state)
    o_ref[...] = _flash_finish(states, tq).astype(o_ref.dtype)


def _window_attn(q, kt, v, sink, *, batch, seq, ctx_len):
    nb = seq // CHUNK
    ctx0 = (batch * seq) // ctx_len

    def kspec(fn):
        return pl.BlockSpec((LANES, CHUNK), lambda b, n: (0, b * nb + fn(n)))

    def vspec(fn):
        return pl.BlockSpec((CHUNK, LANES), lambda b, n: (b * nb + fn(n), 0))

    prev = lambda n: jnp.maximum(n - 1, 0)
    cur = lambda n: n
    nxt = lambda n: jnp.minimum(n + 1, nb - 1)
    return pl.pallas_call(
        functools.partial(_window_attn_kernel, nb=nb),
        grid=(batch, nb),
        in_specs=[pl.BlockSpec(memory_space=pltpu.SMEM),
                  pl.BlockSpec((CHUNK, 256), lambda b, n: (b * nb + n, 0)),
                  pl.BlockSpec((LANES, ctx_len), lambda b, n: (0, ctx0 + b)),
                  pl.BlockSpec((ctx_len, LANES), lambda b, n: (ctx0 + b, 0)),
                  kspec(prev), vspec(prev), kspec(cur), vspec(cur), kspec(nxt), vspec(nxt)],
        out_specs=pl.BlockSpec((CHUNK, 256), lambda b, n: (b * nb + n, 0)),
        out_shape=jax.ShapeDtypeStruct((batch * seq, 256), BF16),
        compiler_params=_params(("parallel", "parallel")),
        name="window_attn",
    )(sink, q, kt, v, kt, v, kt, v, kt, v)


def _gelu_tanh(x):
    return 0.5 * x * (1.0 + jnp.tanh(math.sqrt(2.0 / math.pi) * (x + 0.044715 * (x * x * x))))


def _outproj_kernel(x_ref, gate_ref, gpost_ref, oa_ref, od_ref, yf_ref, yb_ref, xs_ref, z_ref, dsk_ref, gn_ref,
                    hf_ref, hb_ref, lg_ref, w_ref, o_ref):
    y_ssd = (yf_ref[...] + yb_ref[...] + xs_ref[...] * dsk_ref[...]) * _silu(z_ref[...])
    ob = _rms(y_ssd, gn_ref[...])
    oc = (hf_ref[...] + hb_ref[...]) * _gelu_tanh(lg_ref[...])
    y = (_dot(oa_ref[...], w_ref[0:256, :]) + _dot(ob.astype(BF16), w_ref[256:512, :])
         + _dot(oc.astype(BF16), w_ref[512:768, :]) + _dot(od_ref[...], w_ref[768:1024, :]))
    o_ref[...] = x_ref[...] + gate_ref[0] * _rms(y, gpost_ref[...])


def _outproj(xu, mod, gpost, oa, od, yf, yb, xbc, z, dsk, gn, hf, hb, lg, w_out, *, n_rows, n_lat, seq, batch):
    d = xu.shape[1]
    row = lambda w: pl.BlockSpec((TM, w), lambda i: (i, 0))
    const = lambda a: pl.BlockSpec(a.shape, lambda i: (0,) * a.ndim)
    return pl.pallas_call(
        _outproj_kernel,
        grid=(n_rows // TM,),
        in_specs=[row(d), pl.BlockSpec((1, 1, d), _mod_spec(2, n_lat, seq, batch, TM)), const(gpost),
                  row(256), row(256), row(256), row(256), row(256), row(256), const(dsk), const(gn),
                  row(256), row(256), row(256), const(w_out)],
        out_specs=row(d),
        out_shape=jax.ShapeDtypeStruct((n_rows, d), F32),
        compiler_params=_params(("parallel",)),
        name="outproj",
    )(xu, mod, gpost, oa, od, yf, yb, xbc, z, dsk, gn, hf, hb, lg, w_out)


def _router_kernel(x_ref, shift_ref, scale_ref, gpre_ref, rwt_ref, rb_ref,
                   hp_ref, ek_ref, pk_ref, wk_ref, cnt_ref, carry_ref):
    i = pl.program_id(0)

    @pl.when(i == 0)
    def _():
        carry_ref[...] = jnp.zeros_like(carry_ref)

    h = _rms(x_ref[...], gpre_ref[...])
    h = h * (1.0 + scale_ref[0]) + shift_ref[0]
    hb = h.astype(BF16)
    half = h.shape[1] // 2
    bits = lax.bitcast_convert_type(hb.astype(F32), jnp.uint32)
    hp_ref[...] = (bits[:, :half] & jnp.uint32(0xFFFF0000)) | (bits[:, half:] >> 16)

    scores = jax.nn.sigmoid(_dot_nt(rwt_ref[...], hb))
    biased = scores + rb_ref[...]
    gsz = N_EXPERTS // N_EXPERT_GROUPS
    sub = lax.broadcasted_iota(jnp.int32, (gsz, TM), 0)
    blocks, gscore = [], []
    for g in range(N_EXPERT_GROUPS):
        blk = biased[g * gsz:(g + 1) * gsz, :]
        m1 = jnp.max(blk, axis=0, keepdims=True)
        first = jnp.min(jnp.where(blk == m1, sub, gsz), axis=0, keepdims=True)
        m2 = jnp.max(jnp.where(sub == first, -jnp.inf, blk), axis=0, keepdims=True)
        blocks.append(blk)
        gscore.append(m1 + m2)
    masked = []
    for g in range(N_EXPERT_GROUPS):
        rank = jnp.zeros((1, TM), F32)
        for g2 in range(N_EXPERT_GROUPS):
            if g2 == g:
                continue
            beats = (gscore[g2] > gscore[g]) | ((gscore[g2] == gscore[g]) if g2 < g else False)
            rank = rank + jnp.where(beats, 1.0, 0.0)
        masked.append(jnp.where(rank < TOPK_GROUPS, blocks[g], -jnp.inf))
    vals = jnp.concatenate(masked, axis=0)
    eidx = lax.broadcasted_iota(jnp.int32, (N_EXPERTS, TM), 0)
    rank = jnp.zeros((N_EXPERTS, TM), F32)
    for e2 in range(N_EXPERTS):
        rowv = vals[e2:e2 + 1, :]
        beats = (rowv > vals) | ((rowv == vals) & (eidx > e2))
        rank = rank + jnp.where(beats, 1.0, 0.0)
    sel = rank < TOP_K
    self32 = jnp.where(sel, 1.0, 0.0)
    picked = jnp.where(sel, scores, 0.0)
    wdense = picked / jnp.sum(picked, axis=0, keepdims=True) * ROUTED_SCALE

    tr = lax.broadcasted_iota(jnp.int32, (TM, TM), 0)
    tc = lax.broadcasted_iota(jnp.int32, (TM, TM), 1)
    before = jnp.where(tr < tc, 1.0, 0.0).astype(BF16)
    selb = self32.astype(BF16)
    pos = _dot(selb, before) + carry_ref[:, 0:1]
    er = lax.broadcasted_iota(jnp.int32, (N_EXPERTS, N_EXPERTS), 0)
    ec = lax.broadcasted_iota(jnp.int32, (N_EXPERTS, N_EXPERTS), 1)
    lower = jnp.where(ec < er, 1.0, 0.0).astype(BF16)
    ksel = _dot(lower, selb)
    cnt = carry_ref[...] + jnp.sum(self32, axis=1, keepdims=True)
    carry_ref[...] = cnt
    cnt_ref[...] = cnt

    r8 = lax.broadcasted_iota(jnp.int32, (TOP_K, TM), 0)
    ek = jnp.zeros((TOP_K, TM), F32)
    pk = jnp.zeros((TOP_K, TM), F32)
    wk = jnp.zeros((TOP_K, TM), F32)
    ef = eidx.astype(F32)
    for k in range(TOP_K):
        one = sel & (ksel == float(k))
        ek = jnp.where(r8 == k, jnp.sum(jnp.where(one, ef, 0.0), axis=0, keepdims=True), ek)
        pk = jnp.where(r8 == k, jnp.sum(jnp.where(one, pos, 0.0), axis=0, keepdims=True), pk)
        wk = jnp.where(r8 == k, jnp.sum(jnp.where(one, wdense, 0.0), axis=0, keepdims=True), wk)
    ek_ref[...] = ek.astype(jnp.int32)
    pk_ref[...] = pk.astype(jnp.int32)
    wk_ref[...] = wk


def _router(xu, mod, gpre, rwt, rb, *, n_rows, n_lat, seq, batch):
    d = xu.shape[1]
    row = lambda w: pl.BlockSpec((TM, w), lambda i: (i, 0))
    col = pl.BlockSpec((TOP_K, TM), lambda i: (0, i))
    const = lambda a: pl.BlockSpec(a.shape, lambda i: (0,) * a.ndim)
    return pl.pallas_call(
        _router_kernel,
        grid=(n_rows // TM,),
        in_specs=[row(d), pl.BlockSpec((1, 1, d), _mod_spec(3, n_lat, seq, batch, TM)),
                  pl.BlockSpec((1, 1, d), _mod_spec(4, n_lat, seq, batch, TM)),
                  const(gpre), const(rwt), const(rb)],
        out_specs=(row(d // 2), col, col, col, pl.BlockSpec((N_EXPERTS, LANES), lambda i: (0, 0))),
        out_shape=(jax.ShapeDtypeStruct((n_rows, d // 2), jnp.uint32),
                   jax.ShapeDtypeStruct((TOP_K, n_rows), jnp.int32),
                   jax.ShapeDtypeStruct((TOP_K, n_rows), jnp.int32),
                   jax.ShapeDtypeStruct((TOP_K, n_rows), F32),
                   jax.ShapeDtypeStruct((N_EXPERTS, LANES), F32)),
        scratch_shapes=[pltpu.VMEM((N_EXPERTS, LANES), F32)],
        compiler_params=_params(("arbitrary",)),
        name="router",
    )(xu, mod, mod, gpre, rwt, rb)


def _row_copy(src_ref, src_row, dst_ref, dst_row, sem):
    return pltpu.make_async_copy(src_ref.at[pl.ds(src_row, 1)], dst_ref.at[pl.ds(dst_row, 1)], sem)


def _pad_bits():
    bits, b = [], BM_EXPERT // 2
    while b >= SUBLANES:
        bits.append(b)
        b //= 2
    return bits


def _dispatch_kernel(dest_ref, pstart_ref, npad_ref, hp_ref, xs_ref, zbuf, sem, zsem):
    @pl.when(pl.program_id(0) == 0)
    def _():
        zbuf[...] = jnp.zeros_like(zbuf)

        def pads(wait):
            def body(e, c):
                start, n = pstart_ref[e], npad_ref[e]
                head = jnp.minimum(n, (-start) & (SUBLANES - 1))
                for r in range(SUBLANES - 1):
                    @pl.when(r < head)
                    def _():
                        cp = _row_copy(zbuf, 0, xs_ref, start + r, zsem)
                        cp.wait() if wait else cp.start()
                rest = n - head
                for bit in _pad_bits():
                    @pl.when((rest & bit) != 0)
                    def _():
                        off = pl.multiple_of(start + head + (rest & ~(2 * bit - 1)), SUBLANES)
                        cp = pltpu.make_async_copy(zbuf.at[pl.ds(0, bit)], xs_ref.at[pl.ds(off, bit)], zsem)
                        cp.wait() if wait else cp.start()
                return c
            lax.fori_loop(0, N_EXPERTS, body, 0)

        pads(False)
        pads(True)

    def issue(t, c):
        for k in range(TOP_K):
            _row_copy(hp_ref, t, xs_ref, dest_ref[k, t], sem).start()
        return c

    lax.fori_loop(0, T_DISPATCH, issue, 0)

    def drain(t, c):
        for k in range(TOP_K):
            _row_copy(hp_ref, 0, xs_ref, 0, sem).wait()
        return c

    lax.fori_loop(0, T_DISPATCH, drain, 0)


def _dispatch(dest, pad_start, n_pad, hp, n_slots):
    n, half = hp.shape
    smem = pl.BlockSpec(memory_space=pltpu.SMEM)
    return pl.pallas_call(
        _dispatch_kernel,
        grid=(n // T_DISPATCH,),
        in_specs=[pl.BlockSpec((TOP_K, T_DISPATCH), lambda i: (0, i), memory_space=pltpu.SMEM), smem, smem,
                  pl.BlockSpec((T_DISPATCH, half), lambda i: (i, 0))],
        out_specs=pl.BlockSpec(memory_space=pl.ANY),
        out_shape=jax.ShapeDtypeStruct((n_slots, half), jnp.uint32),
        scratch_shapes=[pltpu.VMEM((BM_EXPERT // 2, half), jnp.uint32), pltpu.SemaphoreType.DMA(()),
                        pltpu.SemaphoreType.DMA(())],
        compiler_params=_params(("arbitrary",)),
        name="moe_dispatch",
    )(dest, pad_start, n_pad, hp)


def _unpack_rows(words):
    hi = lax.bitcast_convert_type(words & jnp.uint32(0xFFFF0000), F32)
    lo = lax.bitcast_convert_type(words << 16, F32)
    return jnp.concatenate([hi, lo], axis=1).astype(BF16)


def _expert_kernel(be_ref, na_ref, xs_ref, wg_ref, wu_ref, wd_ref, ys_ref, wgb, wub, wdb):
    i = pl.program_id(0)

    @pl.when(i < na_ref[0])
    def _():
        @pl.when(jnp.logical_or(i == 0, be_ref[i] != be_ref[jnp.maximum(i - 1, 0)]))
        def _():
            wgb[...] = wg_ref[0, 0].astype(BF16)
            wub[...] = wu_ref[0, 0].astype(BF16)
            wdb[...] = wd_ref[0, 0].astype(BF16)

        xb = _unpack_rows(xs_ref[...])
        hid = _silu(_dot(xb, wgb[...])) * _dot(xb, wub[...])
        ys_ref[...] = _dot(hid.astype(BF16), wdb[...])


def _experts(block_e, n_active, xs, wg, wu, wd, layer):
    n_slots, half = xs.shape
    d = half * 2
    nb = n_slots // BM_EXPERT
    blk = lambda i, be, na: jnp.minimum(i, na[0] - 1)
    grid_spec = pltpu.PrefetchScalarGridSpec(
        num_scalar_prefetch=2,
        grid=(nb,),
        in_specs=[pl.BlockSpec((BM_EXPERT, half), lambda i, be, na: (blk(i, be, na), 0)),
                  pl.BlockSpec((1, 1, d, EXPERT_HIDDEN), lambda i, be, na: (layer, be[blk(i, be, na)], 0, 0)),
                  pl.BlockSpec((1, 1, d, EXPERT_HIDDEN), lambda i, be, na: (layer, be[blk(i, be, na)], 0, 0)),
                  pl.BlockSpec((1, 1, EXPERT_HIDDEN, d), lambda i, be, na: (layer, be[blk(i, be, na)], 0, 0))],
        out_specs=pl.BlockSpec((BM_EXPERT, d), lambda i, be, na: (blk(i, be, na), 0)),
        scratch_shapes=[pltpu.VMEM((d, EXPERT_HIDDEN), BF16), pltpu.VMEM((d, EXPERT_HIDDEN), BF16),
                        pltpu.VMEM((EXPERT_HIDDEN, d), BF16)],
    )
    return pl.pallas_call(
        _expert_kernel,
        grid_spec=grid_spec,
        out_shape=jax.ShapeDtypeStruct((n_slots, d), F32),
        compiler_params=_params(("arbitrary",)),
        name="moe_experts",
    )(block_e, n_active, xs, wg, wu, wd)


def _combine_kernel(dest_ref, ys_ref, wk_ref, hp_ref, x_ref, gate_ref, gpost_ref, sg_ref, su_ref, sd_ref,
                    o_ref, gbuf, sem):
    def issue(t, c):
        for k in range(TOP_K):
            pltpu.make_async_copy(ys_ref.at[pl.ds(dest_ref[k, t], 1)], gbuf.at[k, pl.ds(t, 1)], sem).start()
        return c

    lax.fori_loop(0, T_COMBINE, issue, 0)
    hb = _unpack_rows(hp_ref[...])
    shared = _dot((_silu(_dot(hb, sg_ref[...])) * _dot(hb, su_ref[...])).astype(BF16), sd_ref[...])

    def drain(t, c):
        for k in range(TOP_K):
            pltpu.make_async_copy(ys_ref.at[pl.ds(0, 1)], gbuf.at[k, pl.ds(0, 1)], sem).wait()
        return c

    lax.fori_loop(0, T_COMBINE, drain, 0)
    w = wk_ref[...]
    y = shared
    for k in range(TOP_K):
        y = y + gbuf[k] * w[:, k:k + 1]
    o_ref[...] = x_ref[...] + gate_ref[0] * _rms(y, gpost_ref[...])


def _combine(dest, ys, wkt, hp, xu, mod, gpost, sg, su, sd, *, n_rows, n_lat, seq, batch):
    d = xu.shape[1]
    row = lambda w: pl.BlockSpec((T_COMBINE, w), lambda i: (i, 0))
    const = lambda a: pl.BlockSpec(a.shape, lambda i: (0,) * a.ndim)
    return pl.pallas_call(
        _combine_kernel,
        grid=(n_rows // T_COMBINE,),
        in_specs=[pl.BlockSpec((TOP_K, T_COMBINE), lambda i: (0, i), memory_space=pltpu.SMEM),
                  pl.BlockSpec(memory_space=pl.ANY),
                  row(TOP_K), row(d // 2), row(d),
                  pl.BlockSpec((1, 1, d), _mod_spec(5, n_lat, seq, batch, T_COMBINE)),
                  const(gpost), const(sg), const(su), const(sd)],
        out_specs=row(d),
        out_shape=jax.ShapeDtypeStruct((n_rows, d), F32),
        scratch_shapes=[pltpu.VMEM((TOP_K, T_COMBINE, d), F32), pltpu.SemaphoreType.DMA(())],
        compiler_params=_params(("arbitrary",)),
        name="moe_combine",
    )(dest, ys, wkt, hp, xu, mod, gpost, sg, su, sd)


def _deinterleave(w):
    cols = w.shape[-1]
    perm = jnp.concatenate([jnp.arange(0, HEAD_DIM, 2), jnp.arange(1, HEAD_DIM, 2)])
    idx = (jnp.arange(cols // HEAD_DIM)[:, None] * HEAD_DIM + perm[None, :]).reshape(-1)
    return w[..., idx]


def _pad_in_proj(w_in):
    d = w_in.shape[0]
    o = 0
    parts = {}
    for name, width in (("qa", 256), ("ka", 128), ("va", 128), ("z", 256), ("xs", 256), ("bm", 128), ("cm", 128),
                        ("dtf", 4), ("dtb", 4), ("lx", 256), ("lg", 256), ("qd", 256), ("kd", 128), ("vd", 128)):
        parts[name] = w_in[:, o:o + width]
        o += width
    dt = jnp.concatenate([parts["dtf"], parts["dtb"], jnp.zeros((d, LANES - 8), w_in.dtype)], axis=1)
    cols = [_deinterleave(parts["qa"]), _deinterleave(parts["ka"]), parts["va"],
            _deinterleave(parts["qd"]), _deinterleave(parts["kd"]), parts["vd"],
            parts["z"], parts["xs"], parts["bm"], parts["cm"], dt, parts["lx"], parts["lg"]]
    return jnp.concatenate(cols, axis=1).astype(BF16)


def _rope_tables(seq):
    t = jnp.arange(seq)
    rowp = (t // GRID_W).astype(F32)
    colp = (t % GRID_W).astype(F32)
    axis_dim = HEAD_DIM // 2
    inv_freq = ROPE_THETA ** (-jnp.arange(0, axis_dim, 2, dtype=F32) / axis_dim)
    ang = jnp.concatenate([rowp[:, None] * inv_freq, colp[:, None] * inv_freq], axis=-1)
    cos, sin = jnp.cos(ang), jnp.sin(ang)
    cos_h = jnp.concatenate([cos, cos], axis=-1)
    sin_h = jnp.concatenate([-sin, sin], axis=-1)
    return jnp.tile(cos_h, (1, 4)), jnp.tile(sin_h, (1, 4))


def _block_diag(w):
    nb, bd, _ = w.shape
    eye = jnp.eye(nb, dtype=w.dtype)
    return (eye[:, None, :, None] * w[:, :, None, :]).reshape(nb * bd, nb * bd)


def _lane_row(fwd, bwd):
    return jnp.concatenate([fwd, bwd, jnp.zeros((LANES - 8,), F32)]).reshape(1, LANES)


def kernel(x, c, ctx, c_ctx, w_ada, b_ada, g_mix_pre, g_mix_post, g_ffn_pre, g_ffn_post, w_in, w_out, a_sink,
           ssd_conv_w, ssd_conv_b, ssd_dt_bias, ssd_a_log, ssd_d, ssd_norm, lru_conv_w, lru_conv_b, lru_w_a,
           lru_b_a, lru_w_i, lru_b_i, lru_lambda, d_q_norm, d_k_norm, router_w, router_bias, exp_w_gate,
           exp_w_up, exp_w_down, sh_w_gate, sh_w_up, sh_w_down):
    batch, seq, d = x.shape
    ctx_len = ctx.shape[1]
    depth = w_ada.shape[0]
    n_lat = batch * seq
    n_ctx = batch * ctx_len
    n_all = n_lat + n_ctx
    assert seq % TM == 0 and n_ctx % TM == 0 and seq % T_CONV == 0 and ctx_len % T_CONV == 0
    assert ctx_len <= KV_CHUNK and seq % KV_CHUNK == 0 and seq % TQ_GLOBAL == 0 and batch + 1 <= SUBLANES

    xu = jnp.concatenate([x.reshape(n_lat, d), ctx.reshape(n_ctx, d)], axis=0)
    cin = jnp.concatenate([c, c_ctx[None, :], jnp.zeros((SUBLANES - batch - 1, d), F32)], axis=0)
    mod_all = _adaln(cin, w_ada, b_ada)
    cos_t, sin_t = _rope_tables(seq)
    hm = jnp.kron(jnp.eye(4, dtype=F32), jnp.full((HEAD_DIM, HEAD_DIM), 1.0 / HEAD_DIM, F32)).astype(BF16)

    for l in range(depth):
        with_ctx = l < depth - 1
        mod = mod_all[l].reshape(SUBLANES * 6, 1, d)
        gq = jnp.tile(_deinterleave(d_q_norm[l]), 4).reshape(1, 256)
        gk = jnp.tile(_deinterleave(d_k_norm[l]), 2).reshape(1, LANES)
        qa, kat, va, qd, kdt, vd, z, xbc_raw, dt, lx_raw, lg = _inproj(
            xu, mod, g_mix_pre[l].reshape(1, d), _pad_in_proj(w_in[l]), cos_t, sin_t, gq, gk, hm,
            n_lat=n_lat, seq=seq, batch=batch)

        xbc, lu = _conv(xbc_raw, lx_raw, ssd_conv_w[l], ssd_conv_b[l].reshape(1, -1),
                        lru_conv_w[l], lru_conv_b[l].reshape(1, -1), n_lat=n_lat, seq=seq, ctx_len=ctx_len)
        yf, yb = _ssd(xbc, dt, _lane_row(ssd_dt_bias[l, 0], ssd_dt_bias[l, 1]),
                      _lane_row(ssd_a_log[l, 0], ssd_a_log[l, 1]), batch=batch, seq=seq, ctx_len=ctx_len)
        wg = jnp.stack([jnp.concatenate([_block_diag(lru_w_a[l, dd]), _block_diag(lru_w_i[l, dd])], axis=1)
                        for dd in range(2)]).astype(BF16)
        bg = jnp.concatenate([lru_b_a[l], lru_b_i[l]], axis=1).reshape(2, 1, 2 * LRU_WIDTH)
        hf, hb = _lru(lu, wg, bg, lru_lambda[l].reshape(2, 1, LRU_WIDTH), batch=batch, seq=seq, ctx_len=ctx_len)

        oa = _window_attn(qa, kat, va, a_sink[l], batch=batch, seq=seq, ctx_len=ctx_len)
        od = _dense_attn(qd, kdt, vd, None, q_row0=0, q_len=seq, tq=TQ_GLOBAL,
                         segs=[(n_lat, ctx_len), (0, seq)], batch=batch)
        if with_ctx:
            oa_c = _dense_attn(qa, kat, va, a_sink[l], q_row0=n_lat, q_len=ctx_len, tq=ctx_len,
                               segs=[(n_lat, ctx_len)], batch=batch)
            od_c = _dense_attn(qd, kdt, vd, None, q_row0=n_lat, q_len=ctx_len, tq=ctx_len,
                               segs=[(n_lat, ctx_len)], batch=batch)
            oa = jnp.concatenate([oa, oa_c], axis=0)
            od = jnp.concatenate([od, od_c], axis=0)
        n_rows = n_all if with_ctx else n_lat

        dsk = jnp.repeat(ssd_d[l], HEAD_DIM).reshape(1, 256)
        xu_mid = _outproj(xu, mod, g_mix_post[l].reshape(1, d), oa, od, yf, yb, xbc, z, dsk,
                          ssd_norm[l].reshape(1, 256), hf, hb, lg, w_out[l].astype(BF16),
                          n_rows=n_rows, n_lat=n_lat, seq=seq, batch=batch)

        hp, ek, pk, wk, cnt = _router(xu_mid, mod, g_ffn_pre[l].reshape(1, d), router_w[l].T.astype(BF16),
                                      router_bias[l].reshape(N_EXPERTS, 1), n_rows=n_rows, n_lat=n_lat,
                                      seq=seq, batch=batch)
        counts = cnt[:, 0].astype(jnp.int32)
        padded = (counts + BM_EXPERT - 1) // BM_EXPERT * BM_EXPERT
        padded_end = jnp.cumsum(padded)
        offs = padded_end - padded
        n_blocks = (n_rows * TOP_K) // BM_EXPERT + N_EXPERTS
        n_active = (padded_end[-1] // BM_EXPERT).astype(jnp.int32).reshape(1)
        block_start = jnp.arange(n_blocks, dtype=jnp.int32) * BM_EXPERT
        block_e = jnp.minimum(jnp.sum((padded_end[None, :] <= block_start[:, None]).astype(jnp.int32), axis=1),
                              N_EXPERTS - 1)
        expert_ids = jnp.arange(N_EXPERTS, dtype=jnp.int32)
        dest = pk + jnp.sum(jnp.where(ek[:, :, None] == expert_ids, offs, 0), axis=-1)
        xs = _dispatch(dest, offs + counts, padded - counts, hp, n_blocks * BM_EXPERT)
        ys = _experts(block_e, n_active, xs, exp_w_gate, exp_w_up, exp_w_down, l)
        xu = _combine(dest, ys, wk.T, hp, xu_mid, mod, g_ffn_post[l].reshape(1, d), sh_w_gate[l].astype(BF16),
                      sh_w_up[l].astype(BF16), sh_w_down[l].astype(BF16),
                      n_rows=n_rows, n_lat=n_lat, seq=seq, batch=batch)
    return xu[:n_lat].reshape(batch, seq, d)
```

```python
import functools
import math

import jax
import jax.numpy as jnp
from jax import lax
from jax.experimental import pallas as pl
from jax.experimental.pallas import tpu as pltpu

F32 = jnp.float32
BF16 = jnp.bfloat16

HEAD_DIM = 64
GRID_W = 64
ROPE_THETA = 10000.0
NORM_EPS = 1e-6
NEG_INF = -1e30
A_HEADS, A_KV_HEADS, WINDOW = 4, 2, 128
SSD_HEADS, SSD_GROUPS, SSD_STATE, SSD_CONV = 4, 2, 64, 4
LRU_WIDTH, LRU_BLOCKS, LRU_CONV, LRU_C = 256, 4, 4, 8.0
D_HEADS, D_KV_HEADS = 4, 2
N_EXPERTS, N_EXPERT_GROUPS, TOPK_GROUPS, TOP_K = 64, 8, 4, 8
EXPERT_HIDDEN, SHARED_HIDDEN = 256, 256
ROUTED_SCALE = 2.5

LANES = 128
SUBLANES = 8

TM = 512
T_CONV = 256
CHUNK = 128
TQ_GLOBAL = 256
KV_CHUNK = 512
T_DISPATCH = 256
T_COMBINE = 128
BM_EXPERT = 256
VMEM_LIMIT = 48 * 1024 * 1024

C_QA, C_KA, C_VA = 0, 256, 384
C_QD, C_KD, C_VD = 512, 768, 896
C_Z, C_XBC, C_DT = 1024, 1280, 1792
C_LX, C_LG = 1920, 2176
NP_IN = 2432


def _dot(a, b):
    return jnp.dot(a, b, preferred_element_type=F32)


def _dot_nt(a, b):
    return lax.dot_general(a, b, (((1,), (1,)), ((), ())), preferred_element_type=F32)


def _dot3(a, b):
    a1 = a.astype(BF16)
    r1 = a - a1.astype(F32)
    a2 = r1.astype(BF16)
    a3 = (r1 - a2.astype(F32)).astype(BF16)
    return _dot(a1, b) + _dot(a2, b) + _dot(a3, b)


def _dot3_left(a, b):
    b1 = b.astype(BF16)
    r1 = b - b1.astype(F32)
    b2 = r1.astype(BF16)
    b3 = (r1 - b2.astype(F32)).astype(BF16)
    return _dot(a, b1) + _dot(a, b2) + _dot(a, b3)


def _silu(x):
    return x * jax.nn.sigmoid(x)


def _softplus(x):
    return jnp.maximum(x, 0.0) + jnp.log1p(jnp.exp(-jnp.abs(x)))


def _rms(x, gain):
    return x * lax.rsqrt(jnp.mean(x * x, axis=-1, keepdims=True) + NORM_EPS) * gain


def _params(sem=None):
    return pltpu.CompilerParams(dimension_semantics=sem, vmem_limit_bytes=VMEM_LIMIT)


def _adaln_kernel(c_ref, w_ref, b_ref, o_ref):
    s = _silu(c_ref[...])
    o_ref[0] = _dot(s.astype(BF16), w_ref[0].astype(BF16)) + b_ref[0]


def _adaln(cin, w_ada, b_ada):
    depth, d, n6 = w_ada.shape
    tn = 1024
    return pl.pallas_call(
        _adaln_kernel,
        grid=(depth, n6 // tn),
        in_specs=[pl.BlockSpec((SUBLANES, d), lambda l, j: (0, 0)),
                  pl.BlockSpec((1, d, tn), lambda l, j: (l, 0, j)),
                  pl.BlockSpec((1, 1, tn), lambda l, j: (l, 0, j))],
        out_specs=pl.BlockSpec((1, SUBLANES, tn), lambda l, j: (l, 0, j)),
        out_shape=jax.ShapeDtypeStruct((depth, SUBLANES, n6), F32),
        compiler_params=_params(("parallel", "parallel")),
        name="adaln",
    )(cin, w_ada, b_ada.reshape(depth, 1, n6))


def _swap_halves(t):
    w = t.shape[1]
    lane = lax.broadcasted_iota(jnp.int32, (1, w), 1)
    first = (lane & 32) == 0
    return jnp.where(first, pltpu.roll(t, w - 32, axis=1), pltpu.roll(t, 32, axis=1))


def _inproj_kernel(x_ref, shift_ref, scale_ref, gpre_ref, w_ref, cos_ref, sin_ref, gq_ref, gk_ref, hm_ref,
                   qa_ref, kat_ref, va_ref, qd_ref, kdt_ref, vd_ref, z_ref, xbc_ref, dt_ref, lx_ref, lg_ref,
                   *, n_lat):
    i = pl.program_id(0)
    is_lat = i * TM < n_lat
    h = _rms(x_ref[...], gpre_ref[...])
    h = h * (1.0 + scale_ref[0]) + shift_ref[0]
    hb = h.astype(BF16)

    def sec(a, b):
        return _dot(hb, w_ref[:, a:b])

    cos = jnp.where(is_lat, cos_ref[...], 1.0)
    sin = jnp.where(is_lat, sin_ref[...], 0.0)

    def rope(t):
        w = t.shape[1]
        return t * cos[:, :w] + _swap_halves(t) * sin[:, :w]

    def head_norm(t, gain):
        w = t.shape[1]
        ms = _dot3(t * t, hm_ref[:w, :w])
        return t * lax.rsqrt(ms + NORM_EPS) * gain

    scale = HEAD_DIM ** -0.5
    qa_ref[...] = (rope(sec(C_QA, C_KA)) * scale).astype(BF16)
    kat_ref[...] = rope(sec(C_KA, C_VA)).T.astype(BF16)
    va_ref[...] = sec(C_VA, C_QD).astype(BF16)
    qd_ref[...] = (rope(head_norm(sec(C_QD, C_KD), gq_ref[...])) * scale).astype(BF16)
    kdt_ref[...] = rope(head_norm(sec(C_KD, C_VD), gk_ref[...])).T.astype(BF16)
    vd_ref[...] = sec(C_VD, C_Z).astype(BF16)
    z_ref[...] = sec(C_Z, C_XBC)
    xbc_ref[...] = sec(C_XBC, C_DT)
    dt_ref[...] = sec(C_DT, C_LX)
    lx_ref[...] = sec(C_LX, C_LG)
    lg_ref[...] = sec(C_LG, NP_IN)


def _mod_spec(chunk, n_lat, seq, batch, tile):
    def imap(i):
        row0 = i * tile
        seg = jnp.where(row0 < n_lat, row0 // seq, batch)
        return (seg * 6 + chunk, 0, 0)
    return imap


def _inproj(xu, mod, gpre, w_pad, cos_t, sin_t, gq, gk, hm, *, n_lat, seq, batch):
    n, d = xu.shape
    nt = n // TM
    spt = seq // TM
    row = lambda w: pl.BlockSpec((TM, w), lambda i: (i, 0))
    colT = pl.BlockSpec((LANES, TM), lambda i: (0, i))
    const = lambda a: pl.BlockSpec(a.shape, lambda i: (0,) * a.ndim)
    out_shapes = (
        jax.ShapeDtypeStruct((n, 256), BF16), jax.ShapeDtypeStruct((LANES, n), BF16),
        jax.ShapeDtypeStruct((n, LANES), BF16),
        jax.ShapeDtypeStruct((n, 256), BF16), jax.ShapeDtypeStruct((LANES, n), BF16),
        jax.ShapeDtypeStruct((n, LANES), BF16),
        jax.ShapeDtypeStruct((n, 256), F32), jax.ShapeDtypeStruct((n, 512), F32),
        jax.ShapeDtypeStruct((n, LANES), F32), jax.ShapeDtypeStruct((n, 256), F32),
        jax.ShapeDtypeStruct((n, 256), F32))
    return pl.pallas_call(
        functools.partial(_inproj_kernel, n_lat=n_lat),
        grid=(nt,),
        in_specs=[row(d),
                  pl.BlockSpec((1, 1, d), _mod_spec(0, n_lat, seq, batch, TM)),
                  pl.BlockSpec((1, 1, d), _mod_spec(1, n_lat, seq, batch, TM)),
                  const(gpre), const(w_pad),
                  pl.BlockSpec((TM, 256), lambda i: (i % spt, 0)),
                  pl.BlockSpec((TM, 256), lambda i: (i % spt, 0)),
                  const(gq), const(gk), const(hm)],
        out_specs=(row(256), colT, row(LANES), row(256), colT, row(LANES),
                   row(256), row(512), row(LANES), row(256), row(256)),
        out_shape=out_shapes,
        compiler_params=_params(("parallel",)),
        name="inproj",
    )(xu, mod, mod, gpre, w_pad, cos_t, sin_t, gq, gk, hm)


def _conv_kernel(xs_ref, xsp_ref, xsn_ref, xl_ref, xlp_ref, xln_ref, ws_ref, bs_ref, wl_ref, bl_ref,
                 os_ref, ol_ref, *, n_lat, seq, ctx_len):
    i = pl.program_id(0)
    row0 = i * T_CONV
    pos = jnp.where(row0 < n_lat, row0 % seq, (row0 - n_lat) % ctx_len)
    slen = jnp.where(row0 < n_lat, seq, ctx_len)
    first = pos == 0
    last = pos + T_CONV == slen
    row = lax.broadcasted_iota(jnp.int32, (T_CONV, 1), 0)

    def conv(x, prev, nxt, w, b):
        pm = jnp.where(first, 0.0, prev)
        nx = jnp.where(last, 0.0, nxt)
        xm1 = jnp.where(row == 0, pm[7:8, :], pltpu.roll(x, 1, axis=0))
        xm2 = jnp.where(row == 0, pm[6:7, :], jnp.where(row == 1, pm[7:8, :], pltpu.roll(x, 2, axis=0)))
        xp1 = jnp.where(row == T_CONV - 1, nx[0:1, :], pltpu.roll(x, T_CONV - 1, axis=0))
        return w[0:1, :] * xm2 + w[1:2, :] * xm1 + w[2:3, :] * x + w[3:4, :] * xp1 + b

    os_ref[...] = _silu(conv(xs_ref[...], xsp_ref[...], xsn_ref[...], ws_ref[...], bs_ref[...]))
    ol_ref[...] = conv(xl_ref[...], xlp_ref[...], xln_ref[...], wl_ref[...], bl_ref[...])


def _conv(xbc_raw, lx_raw, ws, bs, wl, bl, *, n_lat, seq, ctx_len):
    n = xbc_raw.shape[0]
    nt = n // T_CONV
    r8 = T_CONV // SUBLANES
    n8 = n // SUBLANES
    main = lambda w: pl.BlockSpec((T_CONV, w), lambda i: (i, 0))
    prev = lambda w: pl.BlockSpec((SUBLANES, w), lambda i: (jnp.maximum(i * r8 - 1, 0), 0))
    nxt = lambda w: pl.BlockSpec((SUBLANES, w), lambda i: (jnp.minimum((i + 1) * r8, n8 - 1), 0))
    const = lambda a: pl.BlockSpec(a.shape, lambda i: (0,) * a.ndim)
    return pl.pallas_call(
        functools.partial(_conv_kernel, n_lat=n_lat, seq=seq, ctx_len=ctx_len),
        grid=(nt,),
        in_specs=[main(512), prev(512), nxt(512), main(256), prev(256), nxt(256),
                  const(ws), const(bs), const(wl), const(bl)],
        out_specs=(main(512), main(256)),
        out_shape=(jax.ShapeDtypeStruct((n, 512), F32), jax.ShapeDtypeStruct((n, 256), F32)),
        compiler_params=_params(("parallel",)),
        name="conv",
    )(xbc_raw, xbc_raw, xbc_raw, lx_raw, lx_raw, lx_raw, ws, bs, wl, bl)


def _chunk_maps(batch, seq, ctx_len):
    ncx = ctx_len // CHUNK
    nl = seq // CHUNK
    lat_blocks = batch * nl

    def block(b, c):
        return jnp.where(c < ncx, lat_blocks + b * ncx + c, b * nl + (c - ncx))

    def fwd(b, k):
        return (block(b, k), 0)

    def bwd(b, k):
        c = jnp.where(k < ncx, ncx - 1 - k, ncx + (nl - 1 - (k - ncx)))
        return (block(b, c), 0)

    return fwd, bwd, ncx + nl


def _ssd_kernel(xf_ref, dtf_ref, xb_ref, dtb_ref, dtbias_ref, alog_ref, yf_ref, yb_ref, state_ref):
    k = pl.program_id(1)

    @pl.when(k == 0)
    def _():
        state_ref[...] = jnp.zeros_like(state_ref)

    ri = lax.broadcasted_iota(jnp.int32, (CHUNK, CHUNK), 0)
    ci = lax.broadcasted_iota(jnp.int32, (CHUNK, CHUNK), 1)
    lane_lo = ci < HEAD_DIM
    aneg = -jnp.exp(alog_ref[...])
    dtbias = dtbias_ref[...]

    for d, (x_ref, dt_ref, y_ref) in enumerate(((xf_ref, dtf_ref, yf_ref), (xb_ref, dtb_ref, yb_ref))):
        causal = (ri >= ci) if d == 0 else (ci >= ri)
        tmat = jnp.where(causal, 1.0, 0.0).astype(BF16)
        xs = x_ref[:, 0:256]
        bm = x_ref[:, 256:384]
        cm = x_ref[:, 384:512]
        dtp = _softplus(dt_ref[...] + dtbias)
        acum = _dot3_left(tmat, dtp * aneg)
        acum_t = acum.T
        bt = bm.T.astype(BF16)
        cmb = cm.astype(BF16)
        bmb = bm.astype(BF16)
        tot_row = CHUNK - 1 if d == 0 else 0
        for p in range(2):
            cmask = jnp.where(lane_lo if p == 0 else jnp.logical_not(lane_lo), cmb, jnp.zeros_like(cmb))
            cb = _dot_nt(cmask, bmb)
            cols, dts, ys = [], [], []
            x_pair = xs[:, p * LANES:(p + 1) * LANES]
            for j in range(2):
                col = 4 * d + 2 * p + j
                colb = jnp.broadcast_to(acum[:, col:col + 1], (CHUNK, CHUNK))
                rowb = jnp.broadcast_to(acum_t[col:col + 1, :], (CHUNK, CHUNK))
                cols.append(colb)
                dts.append(jnp.broadcast_to(dtp[:, col:col + 1], (CHUNK, CHUNK)))
            col_pair = jnp.where(lane_lo, cols[0], cols[1])
            dt_pair = jnp.where(lane_lo, dts[0], dts[1])
            xdt = x_pair * dt_pair
            xdt_b = xdt.astype(BF16)
            for j in range(2):
                col = 4 * d + 2 * p + j
                rowb = jnp.broadcast_to(acum_t[col:col + 1, :], (CHUNK, CHUNK))
                decay = jnp.exp(jnp.where(causal, cols[j] - rowb, NEG_INF))
                ys.append(_dot((cb * decay).astype(BF16), xdt_b))
            y_intra = jnp.where(lane_lo, ys[0], ys[1])
            s_old = state_ref[d, p]
            y_inter = _dot(cmask, s_old.astype(BF16)) * jnp.exp(col_pair)
            y_ref[:, p * LANES:(p + 1) * LANES] = y_intra + y_inter
            tot_pair = col_pair[tot_row:tot_row + 1, :]
            to_end = jnp.exp(tot_pair - col_pair)
            state_ref[d, p] = s_old * jnp.exp(tot_pair) + _dot(bt, (xdt * to_end).astype(BF16))


def _ssd(xbc, dt, dtbias_row, alog_row, *, batch, seq, ctx_len):
    n = xbc.shape[0]
    fwd, bwd, steps = _chunk_maps(batch, seq, ctx_len)
    const = lambda a: pl.BlockSpec(a.shape, lambda b, k: (0,) * a.ndim)
    return pl.pallas_call(
        _ssd_kernel,
        grid=(batch, steps),
        in_specs=[pl.BlockSpec((CHUNK, 512), fwd), pl.BlockSpec((CHUNK, LANES), fwd),
                  pl.BlockSpec((CHUNK, 512), bwd), pl.BlockSpec((CHUNK, LANES), bwd),
                  const(dtbias_row), const(alog_row)],
        out_specs=(pl.BlockSpec((CHUNK, 256), fwd), pl.BlockSpec((CHUNK, 256), bwd)),
        out_shape=(jax.ShapeDtypeStruct((n, 256), F32), jax.ShapeDtypeStruct((n, 256), F32)),
        scratch_shapes=[pltpu.VMEM((2, 2, CHUNK, LANES), F32)],
        compiler_params=_params(("parallel", "arbitrary")),
        name="ssd_scan",
    )(xbc, dt, xbc, dt, dtbias_row, alog_row)


def _linear_scan(a, b, reverse):
    n = a.shape[0]
    row = lax.broadcasted_iota(jnp.int32, (n, 1), 0)
    s = 1
    while s < n:
        if reverse:
            ok = row < n - s
            a_sh = jnp.where(ok, pltpu.roll(a, n - s, axis=0), 1.0)
            b_sh = jnp.where(ok, pltpu.roll(b, n - s, axis=0), 0.0)
        else:
            ok = row >= s
            a_sh = jnp.where(ok, pltpu.roll(a, s, axis=0), 1.0)
            b_sh = jnp.where(ok, pltpu.roll(b, s, axis=0), 0.0)
        b = b + a * b_sh
        a = a * a_sh
        s *= 2
    return a, b


def _lru_kernel(uf_ref, ub_ref, wg_ref, bg_ref, lam_ref, hf_ref, hb_ref, carry_ref):
    k = pl.program_id(1)

    @pl.when(k == 0)
    def _():
        carry_ref[...] = jnp.zeros_like(carry_ref)

    for d, (u_ref, h_ref) in enumerate(((uf_ref, hf_ref), (ub_ref, hb_ref))):
        u = u_ref[...]
        gates = _dot(u.astype(BF16), wg_ref[d]) + bg_ref[d]
        r = jax.nn.sigmoid(gates[:, :LRU_WIDTH])
        ig = jax.nn.sigmoid(gates[:, LRU_WIDTH:])
        log_a = -LRU_C * r * _softplus(-lam_ref[d])
        a = jnp.exp(log_a)
        inp = jnp.sqrt(-jnp.tanh(log_a) * (1.0 + a * a)) * (ig * u)
        a_cum, b_cum = _linear_scan(a, inp, reverse=(d == 1))
        h = b_cum + a_cum * carry_ref[d, 0:1, :]
        h_ref[...] = h
        last = 0 if d == 1 else CHUNK - 1
        carry_ref[d, 0:1, :] = h[last:last + 1, :]


def _lru(u, wg, bg, lam, *, batch, seq, ctx_len):
    n = u.shape[0]
    fwd, bwd, steps = _chunk_maps(batch, seq, ctx_len)
    const = lambda a: pl.BlockSpec(a.shape, lambda b, k: (0,) * a.ndim)
    return pl.pallas_call(
        _lru_kernel,
        grid=(batch, steps),
        in_specs=[pl.BlockSpec((CHUNK, LRU_WIDTH), fwd), pl.BlockSpec((CHUNK, LRU_WIDTH), bwd),
                  const(wg), const(bg), const(lam)],
        out_specs=(pl.BlockSpec((CHUNK, LRU_WIDTH), fwd), pl.BlockSpec((CHUNK, LRU_WIDTH), bwd)),
        out_shape=(jax.ShapeDtypeStruct((n, LRU_WIDTH), F32), jax.ShapeDtypeStruct((n, LRU_WIDTH), F32)),
        scratch_shapes=[pltpu.VMEM((2, SUBLANES, LRU_WIDTH), F32)],
        compiler_params=_params(("parallel", "arbitrary")),
        name="lru_scan",
    )(u, u, wg, bg, lam)


def _stack_heads(q, g):
    qf = q.astype(F32)
    lo = g * LANES
    return jnp.concatenate([qf[:, lo:lo + HEAD_DIM], qf[:, lo + HEAD_DIM:lo + LANES]], axis=0).astype(BF16)


def _value_lanes(g):
    lane = lax.broadcasted_iota(jnp.int32, (1, LANES), 1)
    return (lane < HEAD_DIM) if g == 0 else (lane >= HEAD_DIM)


def _aug_values(v, g):
    return jnp.where(_value_lanes(g), v, jnp.ones_like(v))


def _flash_init(rows, g, sink_pair):
    if sink_pair is None:
        return jnp.full((rows, 1), NEG_INF, F32), jnp.zeros((rows, LANES), F32)
    half = lax.broadcasted_iota(jnp.int32, (rows, 1), 0) < rows // 2
    m = jnp.where(half, sink_pair[0], sink_pair[1]).astype(F32)
    acc = jnp.broadcast_to(jnp.where(_value_lanes(g), 0.0, 1.0), (rows, LANES))
    return m, acc


def _flash_update(state, q2, kt, v_aug, mask=None):
    m, acc = state
    s = _dot(q2, kt)
    if mask is not None:
        s = jnp.where(mask, s, NEG_INF)
    m_new = jnp.maximum(m, jnp.max(s, axis=-1, keepdims=True))
    p = jnp.exp(s - m_new).astype(BF16)
    acc = jnp.exp(m - m_new) * acc + _dot(p, v_aug)
    return m_new, acc


def _flash_finish(states, tq):
    pieces = []
    for g, (_, acc) in enumerate(states):
        den = (1 - g) * HEAD_DIM
        o = acc[:, g * HEAD_DIM:(g + 1) * HEAD_DIM] / acc[:, den:den + 1]
        pieces += [o[:tq], o[tq:]]
    return jnp.concatenate(pieces, axis=1)


def _group_rows(g):
    return slice(g * HEAD_DIM, (g + 1) * HEAD_DIM)


def _dense_attn_kernel(*refs, tq, seg_lens, has_sink):
    refs = list(refs)
    sink_ref = refs.pop(0) if has_sink else None
    q_ref = refs.pop(0)
    o_ref = refs.pop()
    segs = [(refs[2 * i], refs[2 * i + 1], n) for i, n in enumerate(seg_lens)]
    q = q_ref[...]
    q2 = [_stack_heads(q, g) for g in range(2)]
    states = tuple(_flash_init(2 * tq, g, (sink_ref[2 * g], sink_ref[2 * g + 1]) if has_sink else None)
                   for g in range(2))
    for kt_ref, v_ref, n_keys in segs:
        if n_keys <= KV_CHUNK:
            v = v_ref[...]
            states = tuple(_flash_update(states[g], q2[g], kt_ref[_group_rows(g), :], _aug_values(v, g))
                           for g in range(2))
        else:
            def body(c, sts, kt_ref=kt_ref, v_ref=v_ref):
                off = pl.multiple_of(c * KV_CHUNK, KV_CHUNK)
                v = v_ref[pl.ds(off, KV_CHUNK), :]
                return tuple(_flash_update(sts[g], q2[g], kt_ref[_group_rows(g), pl.ds(off, KV_CHUNK)],
                                           _aug_values(v, g)) for g in range(2))
            states = lax.fori_loop(0, n_keys // KV_CHUNK, body, states, unroll=2)
    o_ref[...] = _flash_finish(states, tq).astype(o_ref.dtype)


def _dense_attn(q, kt, v, sink, *, q_row0, q_len, tq, segs, batch):
    n = q.shape[0]
    qpb = q_len // tq
    q0 = q_row0 // tq
    in_specs, args = [], []
    if sink is not None:
        in_specs.append(pl.BlockSpec(memory_space=pltpu.SMEM))
        args.append(sink)
    in_specs.append(pl.BlockSpec((tq, 256), lambda b, i: (q0 + b * qpb + i, 0)))
    args.append(q)
    for row0, klen in segs:
        k0 = row0 // klen
        in_specs.append(pl.BlockSpec((LANES, klen), lambda b, i, k0=k0: (0, k0 + b)))
        in_specs.append(pl.BlockSpec((klen, LANES), lambda b, i, k0=k0: (k0 + b, 0)))
        args += [kt, v]
    return pl.pallas_call(
        functools.partial(_dense_attn_kernel, tq=tq, seg_lens=tuple(s[1] for s in segs), has_sink=sink is not None),
        grid=(batch, qpb),
        in_specs=in_specs,
        out_specs=pl.BlockSpec((tq, 256), lambda b, i: (b * qpb + i, 0)),
        out_shape=jax.ShapeDtypeStruct((batch * q_len, 256), BF16),
        compiler_params=_params(("parallel", "parallel")),
        name="dense_attn",
    )(*args)


def _window_attn_kernel(sink_ref, q_ref, ktc_ref, vc_ref, ktp_ref, vp_ref, ktm_ref, vm_ref, ktn_ref, vn_ref, o_ref,
                        *, nb):
    n = pl.program_id(1)
    tq = CHUNK
    iq = lax.broadcasted_iota(jnp.int32, (2 * tq, CHUNK), 0) & (tq - 1)
    jk = lax.broadcasted_iota(jnp.int32, (2 * tq, CHUNK), 1)
    mask_prev = jnp.logical_and(jk >= iq, n > 0)
    mask_next = jnp.logical_and(jk <= iq, n < nb - 1)
    q = q_ref[...]
    states = []
    for g in range(2):
        q2 = _stack_heads(q, g)
        rows = _group_rows(g)
        state = _flash_init(2 * tq, g, (sink_ref[2 * g], sink_ref[2 * g + 1]))
        state = _flash_update(state, q2, ktc_ref[rows, :], _aug_values(vc_ref[...], g))
        state = _flash_update(state, q2, ktm_ref[rows, :], _aug_values(vm_ref[...], g))
        state = _flash_update(state, q2, ktp_ref[rows, :], _aug_values(vp_ref[...], g), mask_prev)
        state = _flash_update(state, q2, ktn_ref[rows, :], _aug_values(vn_ref[...], g), mask_next)
        states.append(state)
    o_ref[...] = _flash_finish(states, tq).astype(o_ref.dtype)


def _window_attn(q, kt, v, sink, *, batch, seq, ctx_len):
    nb = seq // CHUNK
    ctx0 = (batch * seq) // ctx_len

    def kspec(fn):
        return pl.BlockSpec((LANES, CHUNK), lambda b, n: (0, b * nb + fn(n)))

    def vspec(fn):
        return pl.BlockSpec((CHUNK, LANES), lambda b, n: (b * nb + fn(n), 0))

    prev = lambda n: jnp.maximum(n - 1, 0)
    cur = lambda n: n
    nxt = lambda n: jnp.minimum(n + 1, nb - 1)
    return pl.pallas_call(
        functools.partial(_window_attn_kernel, nb=nb),
        grid=(batch, nb),
        in_specs=[pl.BlockSpec(memory_space=pltpu.SMEM),
                  pl.BlockSpec((CHUNK, 256), lambda b, n: (b * nb + n, 0)),
                  pl.BlockSpec((LANES, ctx_len), lambda b, n: (0, ctx0 + b)),
                  pl.BlockSpec((ctx_len, LANES), lambda b, n: (ctx0 + b, 0)),
                  kspec(prev), vspec(prev), kspec(cur), vspec(cur), kspec(nxt), vspec(nxt)],
        out_specs=pl.BlockSpec((CHUNK, 256), lambda b, n: (b * nb + n, 0)),
        out_shape=jax.ShapeDtypeStruct((batch * seq, 256), BF16),
        compiler_params=_params(("parallel", "parallel")),
        name="window_attn",
    )(sink, q, kt, v, kt, v, kt, v, kt, v)


def _gelu_tanh(x):
    return 0.5 * x * (1.0 + jnp.tanh(math.sqrt(2.0 / math.pi) * (x + 0.044715 * (x * x * x))))


def _outproj_kernel(x_ref, gate_ref, gpost_ref, oa_ref, od_ref, yf_ref, yb_ref, xs_ref, z_ref, dsk_ref, gn_ref,
                    hf_ref, hb_ref, lg_ref, w_ref, o_ref):
    y_ssd = (yf_ref[...] + yb_ref[...] + xs_ref[...] * dsk_ref[...]) * _silu(z_ref[...])
    ob = _rms(y_ssd, gn_ref[...])
    oc = (hf_ref[...] + hb_ref[...]) * _gelu_tanh(lg_ref[...])
    y = (_dot(oa_ref[...], w_ref[0:256, :]) + _dot(ob.astype(BF16), w_ref[256:512, :])
         + _dot(oc.astype(BF16), w_ref[512:768, :]) + _dot(od_ref[...], w_ref[768:1024, :]))
    o_ref[...] = x_ref[...] + gate_ref[0] * _rms(y, gpost_ref[...])


def _outproj(xu, mod, gpost, oa, od, yf, yb, xbc, z, dsk, gn, hf, hb, lg, w_out, *, n_rows, n_lat, seq, batch):
    d = xu.shape[1]
    row = lambda w: pl.BlockSpec((TM, w), lambda i: (i, 0))
    const = lambda a: pl.BlockSpec(a.shape, lambda i: (0,) * a.ndim)
    return pl.pallas_call(
        _outproj_kernel,
        grid=(n_rows // TM,),
        in_specs=[row(d), pl.BlockSpec((1, 1, d), _mod_spec(2, n_lat, seq, batch, TM)), const(gpost),
                  row(256), row(256), row(256), row(256), row(256), row(256), const(dsk), const(gn),
                  row(256), row(256), row(256), const(w_out)],
        out_specs=row(d),
        out_shape=jax.ShapeDtypeStruct((n_rows, d), F32),
        compiler_params=_params(("parallel",)),
        name="outproj",
    )(xu, mod, gpost, oa, od, yf, yb, xbc, z, dsk, gn, hf, hb, lg, w_out)


def _router_kernel(x_ref, shift_ref, scale_ref, gpre_ref, rwt_ref, rb_ref,
                   hp_ref, ek_ref, pk_ref, wk_ref, cnt_ref, carry_ref):
    i = pl.program_id(0)

    @pl.when(i == 0)
    def _():
        carry_ref[...] = jnp.zeros_like(carry_ref)

    h = _rms(x_ref[...], gpre_ref[...])
    h = h * (1.0 + scale_ref[0]) + shift_ref[0]
    hb = h.astype(BF16)
    half = h.shape[1] // 2
    bits = lax.bitcast_convert_type(hb.astype(F32), jnp.uint32)
    hp_ref[...] = (bits[:, :half] & jnp.uint32(0xFFFF0000)) | (bits[:, half:] >> 16)

    scores = jax.nn.sigmoid(_dot_nt(rwt_ref[...], hb))
    biased = scores + rb_ref[...]
    gsz = N_EXPERTS // N_EXPERT_GROUPS
    sub = lax.broadcasted_iota(jnp.int32, (gsz, TM), 0)
    blocks, gscore = [], []
    for g in range(N_EXPERT_GROUPS):
        blk = biased[g * gsz:(g + 1) * gsz, :]
        m1 = jnp.max(blk, axis=0, keepdims=True)
        first = jnp.min(jnp.where(blk == m1, sub, gsz), axis=0, keepdims=True)
        m2 = jnp.max(jnp.where(sub == first, -jnp.inf, blk), axis=0, keepdims=True)
        blocks.append(blk)
        gscore.append(m1 + m2)
    masked = []
    for g in range(N_EXPERT_GROUPS):
        rank = jnp.zeros((1, TM), F32)
        for g2 in range(N_EXPERT_GROUPS):
            if g2 == g:
                continue
            beats = (gscore[g2] > gscore[g]) | ((gscore[g2] == gscore[g]) if g2 < g else False)
            rank = rank + jnp.where(beats, 1.0, 0.0)
        masked.append(jnp.where(rank < TOPK_GROUPS, blocks[g], -jnp.inf))
    vals = jnp.concatenate(masked, axis=0)
    eidx = lax.broadcasted_iota(jnp.int32, (N_EXPERTS, TM), 0)
    rank = jnp.zeros((N_EXPERTS, TM), F32)
    for e2 in range(N_EXPERTS):
        rowv = vals[e2:e2 + 1, :]
        beats = (rowv > vals) | ((rowv == vals) & (eidx > e2))
        rank = rank + jnp.where(beats, 1.0, 0.0)
    sel = rank < TOP_K
    self32 = jnp.where(sel, 1.0, 0.0)
    picked = jnp.where(sel, scores, 0.0)
    wdense = picked / jnp.sum(picked, axis=0, keepdims=True) * ROUTED_SCALE

    tr = lax.broadcasted_iota(jnp.int32, (TM, TM), 0)
    tc = lax.broadcasted_iota(jnp.int32, (TM, TM), 1)
    before = jnp.where(tr < tc, 1.0, 0.0).astype(BF16)
    selb = self32.astype(BF16)
    pos = _dot(selb, before) + carry_ref[:, 0:1]
    er = lax.broadcasted_iota(jnp.int32, (N_EXPERTS, N_EXPERTS), 0)
    ec = lax.broadcasted_iota(jnp.int32, (N_EXPERTS, N_EXPERTS), 1)
    lower = jnp.where(ec < er, 1.0, 0.0).astype(BF16)
    ksel = _dot(lower, selb)
    cnt = carry_ref[...] + jnp.sum(self32, axis=1, keepdims=True)
    carry_ref[...] = cnt
    cnt_ref[...] = cnt

    r8 = lax.broadcasted_iota(jnp.int32, (TOP_K, TM), 0)
    ek = jnp.zeros((TOP_K, TM), F32)
    pk = jnp.zeros((TOP_K, TM), F32)
    wk = jnp.zeros((TOP_K, TM), F32)
    ef = eidx.astype(F32)
    for k in range(TOP_K):
        one = sel & (ksel == float(k))
        ek = jnp.where(r8 == k, jnp.sum(jnp.where(one, ef, 0.0), axis=0, keepdims=True), ek)
        pk = jnp.where(r8 == k, jnp.sum(jnp.where(one, pos, 0.0), axis=0, keepdims=True), pk)
        wk = jnp.where(r8 == k, jnp.sum(jnp.where(one, wdense, 0.0), axis=0, keepdims=True), wk)
    ek_ref[...] = ek.astype(jnp.int32)
    pk_ref[...] = pk.astype(jnp.int32)
    wk_ref[...] = wk


def _router(xu, mod, gpre, rwt, rb, *, n_rows, n_lat, seq, batch):
    d = xu.shape[1]
    row = lambda w: pl.BlockSpec((TM, w), lambda i: (i, 0))
    col = pl.BlockSpec((TOP_K, TM), lambda i: (0, i))
    const = lambda a: pl.BlockSpec(a.shape, lambda i: (0,) * a.ndim)
    return pl.pallas_call(
        _router_kernel,
        grid=(n_rows // TM,),
        in_specs=[row(d), pl.BlockSpec((1, 1, d), _mod_spec(3, n_lat, seq, batch, TM)),
                  pl.BlockSpec((1, 1, d), _mod_spec(4, n_lat, seq, batch, TM)),
                  const(gpre), const(rwt), const(rb)],
        out_specs=(row(d // 2), col, col, col, pl.BlockSpec((N_EXPERTS, LANES), lambda i: (0, 0))),
        out_shape=(jax.ShapeDtypeStruct((n_rows, d // 2), jnp.uint32),
                   jax.ShapeDtypeStruct((TOP_K, n_rows), jnp.int32),
                   jax.ShapeDtypeStruct((TOP_K, n_rows), jnp.int32),
                   jax.ShapeDtypeStruct((TOP_K, n_rows), F32),
                   jax.ShapeDtypeStruct((N_EXPERTS, LANES), F32)),
        scratch_shapes=[pltpu.VMEM((N_EXPERTS, LANES), F32)],
        compiler_params=_params(("arbitrary",)),
        name="router",
    )(xu, mod, mod, gpre, rwt, rb)


def _row_copy(src_ref, src_row, dst_ref, dst_row, sem):
    return pltpu.make_async_copy(src_ref.at[pl.ds(src_row, 1)], dst_ref.at[pl.ds(dst_row, 1)], sem)


def _pad_bits():
    bits, b = [], BM_EXPERT // 2
    while b >= SUBLANES:
        bits.append(b)
        b //= 2
    return bits


def _dispatch_kernel(dest_ref, pstart_ref, npad_ref, hp_ref, xs_ref, zbuf, sem, zsem):
    @pl.when(pl.program_id(0) == 0)
    def _():
        zbuf[...] = jnp.zeros_like(zbuf)

        def pads(wait):
            def body(e, c):
                start, n = pstart_ref[e], npad_ref[e]
                head = jnp.minimum(n, (-start) & (SUBLANES - 1))
                for r in range(SUBLANES - 1):
                    @pl.when(r < head)
                    def _():
                        cp = _row_copy(zbuf, 0, xs_ref, start + r, zsem)
                        cp.wait() if wait else cp.start()
                rest = n - head
                for bit in _pad_bits():
                    @pl.when((rest & bit) != 0)
                    def _():
                        off = pl.multiple_of(start + head + (rest & ~(2 * bit - 1)), SUBLANES)
                        cp = pltpu.make_async_copy(zbuf.at[pl.ds(0, bit)], xs_ref.at[pl.ds(off, bit)], zsem)
                        cp.wait() if wait else cp.start()
                return c
            lax.fori_loop(0, N_EXPERTS, body, 0)

        pads(False)
        pads(True)

    def issue(t, c):
        for k in range(TOP_K):
            _row_copy(hp_ref, t, xs_ref, dest_ref[k, t], sem).start(priority=k % 2)
        return c

    lax.fori_loop(0, T_DISPATCH, issue, 0)

    def drain(t, c):
        for k in range(TOP_K):
            _row_copy(hp_ref, 0, xs_ref, 0, sem).wait()
        return c

    lax.fori_loop(0, T_DISPATCH, drain, 0)


def _dispatch(dest, pad_start, n_pad, hp, n_slots):
    n, half = hp.shape
    smem = pl.BlockSpec(memory_space=pltpu.SMEM)
    return pl.pallas_call(
        _dispatch_kernel,
        grid=(n // T_DISPATCH,),
        in_specs=[pl.BlockSpec((TOP_K, T_DISPATCH), lambda i: (0, i), memory_space=pltpu.SMEM), smem, smem,
                  pl.BlockSpec((T_DISPATCH, half), lambda i: (i, 0))],
        out_specs=pl.BlockSpec(memory_space=pl.ANY),
        out_shape=jax.ShapeDtypeStruct((n_slots, half), jnp.uint32),
        scratch_shapes=[pltpu.VMEM((BM_EXPERT // 2, half), jnp.uint32), pltpu.SemaphoreType.DMA(()),
                        pltpu.SemaphoreType.DMA(())],
        compiler_params=_params(("arbitrary",)),
        name="moe_dispatch",
    )(dest, pad_start, n_pad, hp)


def _unpack_rows(words):
    hi = lax.bitcast_convert_type(words & jnp.uint32(0xFFFF0000), F32)
    lo = lax.bitcast_convert_type(words << 16, F32)
    return jnp.concatenate([hi, lo], axis=1).astype(BF16)


def _expert_kernel(be_ref, na_ref, xs_ref, wg_ref, wu_ref, wd_ref, ys_ref, wgb, wub, wdb):
    i = pl.program_id(0)

    @pl.when(i < na_ref[0])
    def _():
        @pl.when(jnp.logical_or(i == 0, be_ref[i] != be_ref[jnp.maximum(i - 1, 0)]))
        def _():
            wgb[...] = wg_ref[0, 0].astype(BF16)
            wub[...] = wu_ref[0, 0].astype(BF16)
            wdb[...] = wd_ref[0, 0].astype(BF16)

        xb = _unpack_rows(xs_ref[...])
        hid = _silu(_dot(xb, wgb[...])) * _dot(xb, wub[...])
        ys_ref[...] = _dot(hid.astype(BF16), wdb[...])


def _experts(block_e, n_active, xs, wg, wu, wd, layer):
    n_slots, half = xs.shape
    d = half * 2
    nb = n_slots // BM_EXPERT
    blk = lambda i, be, na: jnp.minimum(i, na[0] - 1)
    grid_spec = pltpu.PrefetchScalarGridSpec(
        num_scalar_prefetch=2,
        grid=(nb,),
        in_specs=[pl.BlockSpec((BM_EXPERT, half), lambda i, be, na: (blk(i, be, na), 0)),
                  pl.BlockSpec((1, 1, d, EXPERT_HIDDEN), lambda i, be, na: (layer, be[blk(i, be, na)], 0, 0)),
                  pl.BlockSpec((1, 1, d, EXPERT_HIDDEN), lambda i, be, na: (layer, be[blk(i, be, na)], 0, 0)),
                  pl.BlockSpec((1, 1, EXPERT_HIDDEN, d), lambda i, be, na: (layer, be[blk(i, be, na)], 0, 0))],
        out_specs=pl.BlockSpec((BM_EXPERT, d), lambda i, be, na: (blk(i, be, na), 0)),
        scratch_shapes=[pltpu.VMEM((d, EXPERT_HIDDEN), BF16), pltpu.VMEM((d, EXPERT_HIDDEN), BF16),
                        pltpu.VMEM((EXPERT_HIDDEN, d), BF16)],
    )
    return pl.pallas_call(
        _expert_kernel,
        grid_spec=grid_spec,
        out_shape=jax.ShapeDtypeStruct((n_slots, d), F32),
        compiler_params=_params(("arbitrary",)),
        name="moe_experts",
    )(block_e, n_active, xs, wg, wu, wd)


def _combine_kernel(dest_ref, ys_ref, wk_ref, hp_ref, x_ref, gate_ref, gpost_ref, sg_ref, su_ref, sd_ref,
                    o_ref, gbuf, sem):
    def issue(t, c):
        for k in range(TOP_K):
            pltpu.make_async_copy(ys_ref.at[pl.ds(dest_ref[k, t], 1)], gbuf.at[k, pl.ds(t, 1)],
                                  sem).start(priority=k % 2)
        return c

    lax.fori_loop(0, T_COMBINE, issue, 0)
    hb = _unpack_rows(hp_ref[...])
    shared = _dot((_silu(_dot(hb, sg_ref[...])) * _dot(hb, su_ref[...])).astype(BF16), sd_ref[...])

    def drain(t, c):
        for k in range(TOP_K):
            pltpu.make_async_copy(ys_ref.at[pl.ds(0, 1)], gbuf.at[k, pl.ds(0, 1)], sem).wait()
        return c

    lax.fori_loop(0, T_COMBINE, drain, 0)
    w = wk_ref[...]
    y = shared
    for k in range(TOP_K):
        y = y + gbuf[k] * w[:, k:k + 1]
    o_ref[...] = x_ref[...] + gate_ref[0] * _rms(y, gpost_ref[...])


def _combine(dest, ys, wkt, hp, xu, mod, gpost, sg, su, sd, *, n_rows, n_lat, seq, batch):
    d = xu.shape[1]
    row = lambda w: pl.BlockSpec((T_COMBINE, w), lambda i: (i, 0))
    const = lambda a: pl.BlockSpec(a.shape, lambda i: (0,) * a.ndim)
    return pl.pallas_call(
        _combine_kernel,
        grid=(n_rows // T_COMBINE,),
        in_specs=[pl.BlockSpec((TOP_K, T_COMBINE), lambda i: (0, i), memory_space=pltpu.SMEM),
                  pl.BlockSpec(memory_space=pl.ANY),
                  row(TOP_K), row(d // 2), row(d),
                  pl.BlockSpec((1, 1, d), _mod_spec(5, n_lat, seq, batch, T_COMBINE)),
                  const(gpost), const(sg), const(su), const(sd)],
        out_specs=row(d),
        out_shape=jax.ShapeDtypeStruct((n_rows, d), F32),
        scratch_shapes=[pltpu.VMEM((TOP_K, T_COMBINE, d), F32), pltpu.SemaphoreType.DMA(())],
        compiler_params=_params(("arbitrary",)),
        name="moe_combine",
    )(dest, ys, wkt, hp, xu, mod, gpost, sg, su, sd)


def _deinterleave(w):
    cols = w.shape[-1]
    perm = jnp.concatenate([jnp.arange(0, HEAD_DIM, 2), jnp.arange(1, HEAD_DIM, 2)])
    idx = (jnp.arange(cols // HEAD_DIM)[:, None] * HEAD_DIM + perm[None, :]).reshape(-1)
    return w[..., idx]


def _pad_in_proj(w_in):
    d = w_in.shape[0]
    o = 0
    parts = {}
    for name, width in (("qa", 256), ("ka", 128), ("va", 128), ("z", 256), ("xs", 256), ("bm", 128), ("cm", 128),
                        ("dtf", 4), ("dtb", 4), ("lx", 256), ("lg", 256), ("qd", 256), ("kd", 128), ("vd", 128)):
        parts[name] = w_in[:, o:o + width]
        o += width
    dt = jnp.concatenate([parts["dtf"], parts["dtb"], jnp.zeros((d, LANES - 8), w_in.dtype)], axis=1)
    cols = [_deinterleave(parts["qa"]), _deinterleave(parts["ka"]), parts["va"],
            _deinterleave(parts["qd"]), _deinterleave(parts["kd"]), parts["vd"],
            parts["z"], parts["xs"], parts["bm"], parts["cm"], dt, parts["lx"], parts["lg"]]
    return jnp.concatenate(cols, axis=1).astype(BF16)


def _rope_tables(seq):
    t = jnp.arange(seq)
    rowp = (t // GRID_W).astype(F32)
    colp = (t % GRID_W).astype(F32)
    axis_dim = HEAD_DIM // 2
    inv_freq = ROPE_THETA ** (-jnp.arange(0, axis_dim, 2, dtype=F32) / axis_dim)
    ang = jnp.concatenate([rowp[:, None] * inv_freq, colp[:, None] * inv_freq], axis=-1)
    cos, sin = jnp.cos(ang), jnp.sin(ang)
    cos_h = jnp.concatenate([cos, cos], axis=-1)
    sin_h = jnp.concatenate([-sin, sin], axis=-1)
    return jnp.tile(cos_h, (1, 4)), jnp.tile(sin_h, (1, 4))


def _block_diag(w):
    nb, bd, _ = w.shape
    eye = jnp.eye(nb, dtype=w.dtype)
    return (eye[:, None, :, None] * w[:, :, None, :]).reshape(nb * bd, nb * bd)


def _lane_row(fwd, bwd):
    return jnp.concatenate([fwd, bwd, jnp.zeros((LANES - 8,), F32)]).reshape(1, LANES)


def kernel(x, c, ctx, c_ctx, w_ada, b_ada, g_mix_pre, g_mix_post, g_ffn_pre, g_ffn_post, w_in, w_out, a_sink,
           ssd_conv_w, ssd_conv_b, ssd_dt_bias, ssd_a_log, ssd_d, ssd_norm, lru_conv_w, lru_conv_b, lru_w_a,
           lru_b_a, lru_w_i, lru_b_i, lru_lambda, d_q_norm, d_k_norm, router_w, router_bias, exp_w_gate,
           exp_w_up, exp_w_down, sh_w_gate, sh_w_up, sh_w_down):
    batch, seq, d = x.shape
    ctx_len = ctx.shape[1]
    depth = w_ada.shape[0]
    n_lat = batch * seq
    n_ctx = batch * ctx_len
    n_all = n_lat + n_ctx
    assert seq % TM == 0 and n_ctx % TM == 0 and seq % T_CONV == 0 and ctx_len % T_CONV == 0
    assert ctx_len <= KV_CHUNK and seq % KV_CHUNK == 0 and seq % TQ_GLOBAL == 0 and batch + 1 <= SUBLANES

    xu = jnp.concatenate([x.reshape(n_lat, d), ctx.reshape(n_ctx, d)], axis=0)
    cin = jnp.concatenate([c, c_ctx[None, :], jnp.zeros((SUBLANES - batch - 1, d), F32)], axis=0)
    mod_all = _adaln(cin, w_ada, b_ada)
    cos_t, sin_t = _rope_tables(seq)
    hm = jnp.kron(jnp.eye(4, dtype=F32), jnp.full((HEAD_DIM, HEAD_DIM), 1.0 / HEAD_DIM, F32)).astype(BF16)

    for l in range(depth):
        with_ctx = l < depth - 1
        mod = mod_all[l].reshape(SUBLANES * 6, 1, d)
        gq = jnp.tile(_deinterleave(d_q_norm[l]), 4).reshape(1, 256)
        gk = jnp.tile(_deinterleave(d_k_norm[l]), 2).reshape(1, LANES)
        qa, kat, va, qd, kdt, vd, z, xbc_raw, dt, lx_raw, lg = _inproj(
            xu, mod, g_mix_pre[l].reshape(1, d), _pad_in_proj(w_in[l]), cos_t, sin_t, gq, gk, hm,
            n_lat=n_lat, seq=seq, batch=batch)

        xbc, lu = _conv(xbc_raw, lx_raw, ssd_conv_w[l], ssd_conv_b[l].reshape(1, -1),
                        lru_conv_w[l], lru_conv_b[l].reshape(1, -1), n_lat=n_lat, seq=seq, ctx_len=ctx_len)
        yf, yb = _ssd(xbc, dt, _lane_row(ssd_dt_bias[l, 0], ssd_dt_bias[l, 1]),
                      _lane_row(ssd_a_log[l, 0], ssd_a_log[l, 1]), batch=batch, seq=seq, ctx_len=ctx_len)
        wg = jnp.stack([jnp.concatenate([_block_diag(lru_w_a[l, dd]), _block_diag(lru_w_i[l, dd])], axis=1)
                        for dd in range(2)]).astype(BF16)
        bg = jnp.concatenate([lru_b_a[l], lru_b_i[l]], axis=1).reshape(2, 1, 2 * LRU_WIDTH)
        hf, hb = _lru(lu, wg, bg, lru_lambda[l].reshape(2, 1, LRU_WIDTH), batch=batch, seq=seq, ctx_len=ctx_len)

        oa = _window_attn(qa, kat, va, a_sink[l], batch=batch, seq=seq, ctx_len=ctx_len)
        od = _dense_attn(qd, kdt, vd, None, q_row0=0, q_len=seq, tq=TQ_GLOBAL,
                         segs=[(n_lat, ctx_len), (0, seq)], batch=batch)
        if with_ctx:
            oa_c = _dense_attn(qa, kat, va, a_sink[l], q_row0=n_lat, q_len=ctx_len, tq=ctx_len,
                               segs=[(n_lat, ctx_len)], batch=batch)
            od_c = _dense_attn(qd, kdt, vd, None, q_row0=n_lat, q_len=ctx_len, tq=ctx_len,
                               segs=[(n_lat, ctx_len)], batch=batch)
            oa = jnp.concatenate([oa, oa_c], axis=0)
            od = jnp.concatenate([od, od_c], axis=0)
        n_rows = n_all if with_ctx else n_lat

        dsk = jnp.repeat(ssd_d[l], HEAD_DIM).reshape(1, 256)
        xu_mid = _outproj(xu, mod, g_mix_post[l].reshape(1, d), oa, od, yf, yb, xbc, z, dsk,
                          ssd_norm[l].reshape(1, 256), hf, hb, lg, w_out[l].astype(BF16),
                          n_rows=n_rows, n_lat=n_lat, seq=seq, batch=batch)

        hp, ek, pk, wk, cnt = _router(xu_mid, mod, g_ffn_pre[l].reshape(1, d), router_w[l].T.astype(BF16),
                                      router_bias[l].reshape(N_EXPERTS, 1), n_rows=n_rows, n_lat=n_lat,
                                      seq=seq, batch=batch)
        counts = cnt[:, 0].astype(jnp.int32)
        padded = (counts + BM_EXPERT - 1) // BM_EXPERT * BM_EXPERT
        padded_end = jnp.cumsum(padded)
        offs = padded_end - padded
        n_blocks = (n_rows * TOP_K) // BM_EXPERT + N_EXPERTS
        n_active = (padded_end[-1] // BM_EXPERT).astype(jnp.int32).reshape(1)
        block_start = jnp.arange(n_blocks, dtype=jnp.int32) * BM_EXPERT
        block_e = jnp.minimum(jnp.sum((padded_end[None, :] <= block_start[:, None]).astype(jnp.int32), axis=1),
                              N_EXPERTS - 1)
        expert_ids = jnp.arange(N_EXPERTS, dtype=jnp.int32)
        dest = pk + jnp.sum(jnp.where(ek[:, :, None] == expert_ids, offs, 0), axis=-1)
        xs = _dispatch(dest, offs + counts, padded - counts, hp, n_blocks * BM_EXPERT)
        ys = _experts(block_e, n_active, xs, exp_w_gate, exp_w_up, exp_w_down, l)
        xu = _combine(dest, ys, wk.T, hp, xu_mid, mod, g_ffn_post[l].reshape(1, d), sh_w_gate[l].astype(BF16),
                      sh_w_up[l].astype(BF16), sh_w_down[l].astype(BF16),
                      n_rows=n_rows, n_lat=n_lat, seq=seq, batch=batch)
    return xu[:n_lat].reshape(batch, seq, d)
```

```python
import functools
import math

import jax
import jax.numpy as jnp
from jax import lax
from jax.experimental import pallas as pl
from jax.experimental.pallas import tpu as pltpu

F32 = jnp.float32
BF16 = jnp.bfloat16

HEAD_DIM = 64
GRID_W = 64
ROPE_THETA = 10000.0
NORM_EPS = 1e-6
NEG_INF = -1e30
A_HEADS, A_KV_HEADS, WINDOW = 4, 2, 128
SSD_HEADS, SSD_GROUPS, SSD_STATE, SSD_CONV = 4, 2, 64, 4
LRU_WIDTH, LRU_BLOCKS, LRU_CONV, LRU_C = 256, 4, 4, 8.0
D_HEADS, D_KV_HEADS = 4, 2
N_EXPERTS, N_EXPERT_GROUPS, TOPK_GROUPS, TOP_K = 64, 8, 4, 8
EXPERT_HIDDEN, SHARED_HIDDEN = 256, 256
ROUTED_SCALE = 2.5

LANES = 128
SUBLANES = 8

TM = 512
T_CONV = 256
CHUNK = 128
TQ_GLOBAL = 256
KV_CHUNK = 512
BM_EXPERT = 256
SEG_ALIGN = 16
STAGE_ROWS = TM * TOP_K + N_EXPERTS * SEG_ALIGN
VMEM_LIMIT = 48 * 1024 * 1024

C_QA, C_KA, C_VA = 0, 256, 384
C_QD, C_KD, C_VD = 512, 768, 896
C_Z, C_XBC, C_DT = 1024, 1280, 1792
C_LX, C_LG = 1920, 2176
NP_IN = 2432


def _dot(a, b):
    return jnp.dot(a, b, preferred_element_type=F32)


def _dot_nt(a, b):
    return lax.dot_general(a, b, (((1,), (1,)), ((), ())), preferred_element_type=F32)


def _dot3(a, b):
    a1 = a.astype(BF16)
    r1 = a - a1.astype(F32)
    a2 = r1.astype(BF16)
    a3 = (r1 - a2.astype(F32)).astype(BF16)
    return _dot(a1, b) + _dot(a2, b) + _dot(a3, b)


def _dot3_left(a, b):
    b1 = b.astype(BF16)
    r1 = b - b1.astype(F32)
    b2 = r1.astype(BF16)
    b3 = (r1 - b2.astype(F32)).astype(BF16)
    return _dot(a, b1) + _dot(a, b2) + _dot(a, b3)


def _silu(x):
    return x * jax.nn.sigmoid(x)


def _softplus(x):
    return jnp.maximum(x, 0.0) + jnp.log1p(jnp.exp(-jnp.abs(x)))


def _rms(x, gain):
    return x * lax.rsqrt(jnp.mean(x * x, axis=-1, keepdims=True) + NORM_EPS) * gain


def _params(sem=None):
    return pltpu.CompilerParams(dimension_semantics=sem, vmem_limit_bytes=VMEM_LIMIT)


def _adaln_kernel(c_ref, w_ref, b_ref, o_ref):
    s = _silu(c_ref[...])
    o_ref[0] = _dot(s.astype(BF16), w_ref[0].astype(BF16)) + b_ref[0]


def _adaln(cin, w_ada, b_ada):
    depth, d, n6 = w_ada.shape
    tn = 1024
    return pl.pallas_call(
        _adaln_kernel,
        grid=(depth, n6 // tn),
        in_specs=[pl.BlockSpec((SUBLANES, d), lambda l, j: (0, 0)),
                  pl.BlockSpec((1, d, tn), lambda l, j: (l, 0, j)),
                  pl.BlockSpec((1, 1, tn), lambda l, j: (l, 0, j))],
        out_specs=pl.BlockSpec((1, SUBLANES, tn), lambda l, j: (l, 0, j)),
        out_shape=jax.ShapeDtypeStruct((depth, SUBLANES, n6), F32),
        compiler_params=_params(("parallel", "parallel")),
        name="adaln",
    )(cin, w_ada, b_ada.reshape(depth, 1, n6))


def _swap_halves(t):
    w = t.shape[1]
    lane = lax.broadcasted_iota(jnp.int32, (1, w), 1)
    first = (lane & 32) == 0
    return jnp.where(first, pltpu.roll(t, w - 32, axis=1), pltpu.roll(t, 32, axis=1))


def _inproj_kernel(x_ref, shift_ref, scale_ref, gpre_ref, w_ref, cos_ref, sin_ref, gq_ref, gk_ref, hm_ref,
                   qa_ref, kat_ref, va_ref, qd_ref, kdt_ref, vd_ref, z_ref, xbc_ref, dt_ref, lx_ref, lg_ref,
                   *, n_lat):
    i = pl.program_id(0)
    is_lat = i * TM < n_lat
    h = _rms(x_ref[...], gpre_ref[...])
    h = h * (1.0 + scale_ref[0]) + shift_ref[0]
    hb = h.astype(BF16)

    def sec(a, b):
        return _dot(hb, w_ref[:, a:b])

    cos = jnp.where(is_lat, cos_ref[...], 1.0)
    sin = jnp.where(is_lat, sin_ref[...], 0.0)

    def rope(t):
        w = t.shape[1]
        return t * cos[:, :w] + _swap_halves(t) * sin[:, :w]

    def head_norm(t, gain):
        w = t.shape[1]
        ms = _dot3(t * t, hm_ref[:w, :w])
        return t * lax.rsqrt(ms + NORM_EPS) * gain

    scale = HEAD_DIM ** -0.5
    qa_ref[...] = (rope(sec(C_QA, C_KA)) * scale).astype(BF16)
    kat_ref[...] = rope(sec(C_KA, C_VA)).T.astype(BF16)
    va_ref[...] = sec(C_VA, C_QD).astype(BF16)
    qd_ref[...] = (rope(head_norm(sec(C_QD, C_KD), gq_ref[...])) * scale).astype(BF16)
    kdt_ref[...] = rope(head_norm(sec(C_KD, C_VD), gk_ref[...])).T.astype(BF16)
    vd_ref[...] = sec(C_VD, C_Z).astype(BF16)
    z_ref[...] = sec(C_Z, C_XBC)
    xbc_ref[...] = sec(C_XBC, C_DT)
    dt_ref[...] = sec(C_DT, C_LX)
    lx_ref[...] = sec(C_LX, C_LG)
    lg_ref[...] = sec(C_LG, NP_IN)


def _mod_spec(chunk, n_lat, seq, batch, tile):
    def imap(i):
        row0 = i * tile
        seg = jnp.where(row0 < n_lat, row0 // seq, batch)
        return (seg * 6 + chunk, 0, 0)
    return imap


def _inproj(xu, mod, gpre, w_pad, cos_t, sin_t, gq, gk, hm, *, n_lat, seq, batch):
    n, d = xu.shape
    nt = n // TM
    spt = seq // TM
    row = lambda w: pl.BlockSpec((TM, w), lambda i: (i, 0))
    colT = pl.BlockSpec((LANES, TM), lambda i: (0, i))
    const = lambda a: pl.BlockSpec(a.shape, lambda i: (0,) * a.ndim)
    out_shapes = (
        jax.ShapeDtypeStruct((n, 256), BF16), jax.ShapeDtypeStruct((LANES, n), BF16),
        jax.ShapeDtypeStruct((n, LANES), BF16),
        jax.ShapeDtypeStruct((n, 256), BF16), jax.ShapeDtypeStruct((LANES, n), BF16),
        jax.ShapeDtypeStruct((n, LANES), BF16),
        jax.ShapeDtypeStruct((n, 256), F32), jax.ShapeDtypeStruct((n, 512), F32),
        jax.ShapeDtypeStruct((n, LANES), F32), jax.ShapeDtypeStruct((n, 256), F32),
        jax.ShapeDtypeStruct((n, 256), F32))
    return pl.pallas_call(
        functools.partial(_inproj_kernel, n_lat=n_lat),
        grid=(nt,),
        in_specs=[row(d),
                  pl.BlockSpec((1, 1, d), _mod_spec(0, n_lat, seq, batch, TM)),
                  pl.BlockSpec((1, 1, d), _mod_spec(1, n_lat, seq, batch, TM)),
                  const(gpre), const(w_pad),
                  pl.BlockSpec((TM, 256), lambda i: (i % spt, 0)),
                  pl.BlockSpec((TM, 256), lambda i: (i % spt, 0)),
                  const(gq), const(gk), const(hm)],
        out_specs=(row(256), colT, row(LANES), row(256), colT, row(LANES),
                   row(256), row(512), row(LANES), row(256), row(256)),
        out_shape=out_shapes,
        compiler_params=_params(("parallel",)),
        name="inproj",
    )(xu, mod, mod, gpre, w_pad, cos_t, sin_t, gq, gk, hm)


def _conv_kernel(xs_ref, xsp_ref, xsn_ref, xl_ref, xlp_ref, xln_ref, ws_ref, bs_ref, wl_ref, bl_ref,
                 os_ref, ol_ref, *, n_lat, seq, ctx_len):
    i = pl.program_id(0)
    row0 = i * T_CONV
    pos = jnp.where(row0 < n_lat, row0 % seq, (row0 - n_lat) % ctx_len)
    slen = jnp.where(row0 < n_lat, seq, ctx_len)
    first = pos == 0
    last = pos + T_CONV == slen
    row = lax.broadcasted_iota(jnp.int32, (T_CONV, 1), 0)

    def conv(x, prev, nxt, w, b):
        pm = jnp.where(first, 0.0, prev)
        nx = jnp.where(last, 0.0, nxt)
        xm1 = jnp.where(row == 0, pm[7:8, :], pltpu.roll(x, 1, axis=0))
        xm2 = jnp.where(row == 0, pm[6:7, :], jnp.where(row == 1, pm[7:8, :], pltpu.roll(x, 2, axis=0)))
        xp1 = jnp.where(row == T_CONV - 1, nx[0:1, :], pltpu.roll(x, T_CONV - 1, axis=0))
        return w[0:1, :] * xm2 + w[1:2, :] * xm1 + w[2:3, :] * x + w[3:4, :] * xp1 + b

    os_ref[...] = _silu(conv(xs_ref[...], xsp_ref[...], xsn_ref[...], ws_ref[...], bs_ref[...]))
    ol_ref[...] = conv(xl_ref[...], xlp_ref[...], xln_ref[...], wl_ref[...], bl_ref[...])


def _conv(xbc_raw, lx_raw, ws, bs, wl, bl, *, n_lat, seq, ctx_len):
    n = xbc_raw.shape[0]
    nt = n // T_CONV
    r8 = T_CONV // SUBLANES
    n8 = n // SUBLANES
    main = lambda w: pl.BlockSpec((T_CONV, w), lambda i: (i, 0))
    prev = lambda w: pl.BlockSpec((SUBLANES, w), lambda i: (jnp.maximum(i * r8 - 1, 0), 0))
    nxt = lambda w: pl.BlockSpec((SUBLANES, w), lambda i: (jnp.minimum((i + 1) * r8, n8 - 1), 0))
    const = lambda a: pl.BlockSpec(a.shape, lambda i: (0,) * a.ndim)
    return pl.pallas_call(
        functools.partial(_conv_kernel, n_lat=n_lat, seq=seq, ctx_len=ctx_len),
        grid=(nt,),
        in_specs=[main(512), prev(512), nxt(512), main(256), prev(256), nxt(256),
                  const(ws), const(bs), const(wl), const(bl)],
        out_specs=(main(512), main(256)),
        out_shape=(jax.ShapeDtypeStruct((n, 512), F32), jax.ShapeDtypeStruct((n, 256), F32)),
        compiler_params=_params(("parallel",)),
        name="conv",
    )(xbc_raw, xbc_raw, xbc_raw, lx_raw, lx_raw, lx_raw, ws, bs, wl, bl)


def _chunk_maps(batch, seq, ctx_len):
    ncx = ctx_len // CHUNK
    nl = seq // CHUNK
    lat_blocks = batch * nl

    def block(b, c):
        return jnp.where(c < ncx, lat_blocks + b * ncx + c, b * nl + (c - ncx))

    def fwd(b, k):
        return (block(b, k), 0)

    def bwd(b, k):
        c = jnp.where(k < ncx, ncx - 1 - k, ncx + (nl - 1 - (k - ncx)))
        return (block(b, c), 0)

    return fwd, bwd, ncx + nl


def _ssd_kernel(xf_ref, dtf_ref, xb_ref, dtb_ref, dtbias_ref, alog_ref, yf_ref, yb_ref, state_ref):
    k = pl.program_id(1)

    @pl.when(k == 0)
    def _():
        state_ref[...] = jnp.zeros_like(state_ref)

    ri = lax.broadcasted_iota(jnp.int32, (CHUNK, CHUNK), 0)
    ci = lax.broadcasted_iota(jnp.int32, (CHUNK, CHUNK), 1)
    lane_lo = ci < HEAD_DIM
    aneg = -jnp.exp(alog_ref[...])
    dtbias = dtbias_ref[...]

    for d, (x_ref, dt_ref, y_ref) in enumerate(((xf_ref, dtf_ref, yf_ref), (xb_ref, dtb_ref, yb_ref))):
        causal = (ri >= ci) if d == 0 else (ci >= ri)
        tmat = jnp.where(causal, 1.0, 0.0).astype(BF16)
        xs = x_ref[:, 0:256]
        bm = x_ref[:, 256:384]
        cm = x_ref[:, 384:512]
        dtp = _softplus(dt_ref[...] + dtbias)
        acum = _dot3_left(tmat, dtp * aneg)
        acum_t = acum.T
        bt = bm.T.astype(BF16)
        cmb = cm.astype(BF16)
        bmb = bm.astype(BF16)
        tot_row = CHUNK - 1 if d == 0 else 0
        for p in range(2):
            cmask = jnp.where(lane_lo if p == 0 else jnp.logical_not(lane_lo), cmb, jnp.zeros_like(cmb))
            cb = _dot_nt(cmask, bmb)
            cols, dts, ys = [], [], []
            x_pair = xs[:, p * LANES:(p + 1) * LANES]
            for j in range(2):
                col = 4 * d + 2 * p + j
                colb = jnp.broadcast_to(acum[:, col:col + 1], (CHUNK, CHUNK))
                rowb = jnp.broadcast_to(acum_t[col:col + 1, :], (CHUNK, CHUNK))
                cols.append(colb)
                dts.append(jnp.broadcast_to(dtp[:, col:col + 1], (CHUNK, CHUNK)))
            col_pair = jnp.where(lane_lo, cols[0], cols[1])
            dt_pair = jnp.where(lane_lo, dts[0], dts[1])
            xdt = x_pair * dt_pair
            xdt_b = xdt.astype(BF16)
            for j in range(2):
                col = 4 * d + 2 * p + j
                rowb = jnp.broadcast_to(acum_t[col:col + 1, :], (CHUNK, CHUNK))
                decay = jnp.exp(jnp.where(causal, cols[j] - rowb, NEG_INF))
                ys.append(_dot((cb * decay).astype(BF16), xdt_b))
            y_intra = jnp.where(lane_lo, ys[0], ys[1])
            s_old = state_ref[d, p]
            y_inter = _dot(cmask, s_old.astype(BF16)) * jnp.exp(col_pair)
            y_ref[:, p * LANES:(p + 1) * LANES] = y_intra + y_inter
            tot_pair = col_pair[tot_row:tot_row + 1, :]
            to_end = jnp.exp(tot_pair - col_pair)
            state_ref[d, p] = s_old * jnp.exp(tot_pair) + _dot(bt, (xdt * to_end).astype(BF16))


def _ssd(xbc, dt, dtbias_row, alog_row, *, batch, seq, ctx_len):
    n = xbc.shape[0]
    fwd, bwd, steps = _chunk_maps(batch, seq, ctx_len)
    const = lambda a: pl.BlockSpec(a.shape, lambda b, k: (0,) * a.ndim)
    return pl.pallas_call(
        _ssd_kernel,
        grid=(batch, steps),
        in_specs=[pl.BlockSpec((CHUNK, 512), fwd), pl.BlockSpec((CHUNK, LANES), fwd),
                  pl.BlockSpec((CHUNK, 512), bwd), pl.BlockSpec((CHUNK, LANES), bwd),
                  const(dtbias_row), const(alog_row)],
        out_specs=(pl.BlockSpec((CHUNK, 256), fwd), pl.BlockSpec((CHUNK, 256), bwd)),
        out_shape=(jax.ShapeDtypeStruct((n, 256), F32), jax.ShapeDtypeStruct((n, 256), F32)),
        scratch_shapes=[pltpu.VMEM((2, 2, CHUNK, LANES), F32)],
        compiler_params=_params(("parallel", "arbitrary")),
        name="ssd_scan",
    )(xbc, dt, xbc, dt, dtbias_row, alog_row)


def _linear_scan(a, b, reverse):
    n = a.shape[0]
    row = lax.broadcasted_iota(jnp.int32, (n, 1), 0)
    s = 1
    while s < n:
        if reverse:
            ok = row < n - s
            a_sh = jnp.where(ok, pltpu.roll(a, n - s, axis=0), 1.0)
            b_sh = jnp.where(ok, pltpu.roll(b, n - s, axis=0), 0.0)
        else:
            ok = row >= s
            a_sh = jnp.where(ok, pltpu.roll(a, s, axis=0), 1.0)
            b_sh = jnp.where(ok, pltpu.roll(b, s, axis=0), 0.0)
        b = b + a * b_sh
        a = a * a_sh
        s *= 2
    return a, b


def _lru_kernel(uf_ref, ub_ref, wg_ref, bg_ref, lam_ref, hf_ref, hb_ref, carry_ref):
    k = pl.program_id(1)

    @pl.when(k == 0)
    def _():
        carry_ref[...] = jnp.zeros_like(carry_ref)

    for d, (u_ref, h_ref) in enumerate(((uf_ref, hf_ref), (ub_ref, hb_ref))):
        u = u_ref[...]
        gates = _dot(u.astype(BF16), wg_ref[d]) + bg_ref[d]
        r = jax.nn.sigmoid(gates[:, :LRU_WIDTH])
        ig = jax.nn.sigmoid(gates[:, LRU_WIDTH:])
        log_a = -LRU_C * r * _softplus(-lam_ref[d])
        a = jnp.exp(log_a)
        inp = jnp.sqrt(-jnp.tanh(log_a) * (1.0 + a * a)) * (ig * u)
        a_cum, b_cum = _linear_scan(a, inp, reverse=(d == 1))
        h = b_cum + a_cum * carry_ref[d, 0:1, :]
        h_ref[...] = h
        last = 0 if d == 1 else CHUNK - 1
        carry_ref[d, 0:1, :] = h[last:last + 1, :]


def _lru(u, wg, bg, lam, *, batch, seq, ctx_len):
    n = u.shape[0]
    fwd, bwd, steps = _chunk_maps(batch, seq, ctx_len)
    const = lambda a: pl.BlockSpec(a.shape, lambda b, k: (0,) * a.ndim)
    return pl.pallas_call(
        _lru_kernel,
        grid=(batch, steps),
        in_specs=[pl.BlockSpec((CHUNK, LRU_WIDTH), fwd), pl.BlockSpec((CHUNK, LRU_WIDTH), bwd),
                  const(wg), const(bg), const(lam)],
        out_specs=(pl.BlockSpec((CHUNK, LRU_WIDTH), fwd), pl.BlockSpec((CHUNK, LRU_WIDTH), bwd)),
        out_shape=(jax.ShapeDtypeStruct((n, LRU_WIDTH), F32), jax.ShapeDtypeStruct((n, LRU_WIDTH), F32)),
        scratch_shapes=[pltpu.VMEM((2, SUBLANES, LRU_WIDTH), F32)],
        compiler_params=_params(("parallel", "arbitrary")),
        name="lru_scan",
    )(u, u, wg, bg, lam)


def _stack_heads(q, g):
    qf = q.astype(F32)
    lo = g * LANES
    return jnp.concatenate([qf[:, lo:lo + HEAD_DIM], qf[:, lo + HEAD_DIM:lo + LANES]], axis=0).astype(BF16)


def _value_lanes(g):
    lane = lax.broadcasted_iota(jnp.int32, (1, LANES), 1)
    return (lane < HEAD_DIM) if g == 0 else (lane >= HEAD_DIM)


def _aug_values(v, g):
    return jnp.where(_value_lanes(g), v, jnp.ones_like(v))


def _flash_init(rows, g, sink_pair):
    if sink_pair is None:
        return jnp.full((rows, 1), NEG_INF, F32), jnp.zeros((rows, LANES), F32)
    half = lax.broadcasted_iota(jnp.int32, (rows, 1), 0) < rows // 2
    m = jnp.where(half, sink_pair[0], sink_pair[1]).astype(F32)
    acc = jnp.broadcast_to(jnp.where(_value_lanes(g), 0.0, 1.0), (rows, LANES))
    return m, acc


def _flash_update(state, q2, kt, v_aug, mask=None):
    m, acc = state
    s = _dot(q2, kt)
    if mask is not None:
        s = jnp.where(mask, s, NEG_INF)
    m_new = jnp.maximum(m, jnp.max(s, axis=-1, keepdims=True))
    p = jnp.exp(s - m_new).astype(BF16)
    acc = jnp.exp(m - m_new) * acc + _dot(p, v_aug)
    return m_new, acc


def _flash_finish(states, tq):
    pieces = []
    for g, (_, acc) in enumerate(states):
        den = (1 - g) * HEAD_DIM
        o = acc[:, g * HEAD_DIM:(g + 1) * HEAD_DIM] / acc[:, den:den + 1]
        pieces += [o[:tq], o[tq:]]
    return jnp.concatenate(pieces, axis=1)


def _group_rows(g):
    return slice(g * HEAD_DIM, (g + 1) * HEAD_DIM)


def _dense_attn_kernel(*refs, tq, seg_lens, has_sink):
    refs = list(refs)
    sink_ref = refs.pop(0) if has_sink else None
    q_ref = refs.pop(0)
    o_ref = refs.pop()
    segs = [(refs[2 * i], refs[2 * i + 1], n) for i, n in enumerate(seg_lens)]
    q = q_ref[...]
    q2 = [_stack_heads(q, g) for g in range(2)]
    states = tuple(_flash_init(2 * tq, g, (sink_ref[2 * g], sink_ref[2 * g + 1]) if has_sink else None)
                   for g in range(2))
    for kt_ref, v_ref, n_keys in segs:
        if n_keys <= KV_CHUNK:
            v = v_ref[...]
            states = tuple(_flash_update(states[g], q2[g], kt_ref[_group_rows(g), :], _aug_values(v, g))
                           for g in range(2))
        else:
            def body(c, sts, kt_ref=kt_ref, v_ref=v_ref):
                off = pl.multiple_of(c * KV_CHUNK, KV_CHUNK)
                v = v_ref[pl.ds(off, KV_CHUNK), :]
                return tuple(_flash_update(sts[g], q2[g], kt_ref[_group_rows(g), pl.ds(off, KV_CHUNK)],
                                           _aug_values(v, g)) for g in range(2))
            states = lax.fori_loop(0, n_keys // KV_CHUNK, body, states, unroll=2)
    o_ref[...] = _flash_finish(states, tq).astype(o_ref.dtype)


def _dense_attn(q, kt, v, sink, *, q_row0, q_len, tq, segs, batch):
    n = q.shape[0]
    qpb = q_len // tq
    q0 = q_row0 // tq
    in_specs, args = [], []
    if sink is not None:
        in_specs.append(pl.BlockSpec(memory_space=pltpu.SMEM))
        args.append(sink)
    in_specs.append(pl.BlockSpec((tq, 256), lambda b, i: (q0 + b * qpb + i, 0)))
    args.append(q)
    for row0, klen in segs:
        k0 = row0 // klen
        in_specs.append(pl.BlockSpec((LANES, klen), lambda b, i, k0=k0: (0, k0 + b)))
        in_specs.append(pl.BlockSpec((klen, LANES), lambda b, i, k0=k0: (k0 + b, 0)))
        args += [kt, v]
    return pl.pallas_call(
        functools.partial(_dense_attn_kernel, tq=tq, seg_lens=tuple(s[1] for s in segs), has_sink=sink is not None),
        grid=(batch, qpb),
        in_specs=in_specs,
        out_specs=pl.BlockSpec((tq, 256), lambda b, i: (b * qpb + i, 0)),
        out_shape=jax.ShapeDtypeStruct((batch * q_len, 256), BF16),
        compiler_params=_params(("parallel", "parallel")),
        name="dense_attn",
    )(*args)


def _window_attn_kernel(sink_ref, q_ref, ktc_ref, vc_ref, ktp_ref, vp_ref, ktm_ref, vm_ref, ktn_ref, vn_ref, o_ref,
                        *, nb):
    n = pl.program_id(1)
    tq = CHUNK
    iq = lax.broadcasted_iota(jnp.int32, (2 * tq, CHUNK), 0) & (tq - 1)
    jk = lax.broadcasted_iota(jnp.int32, (2 * tq, CHUNK), 1)
    mask_prev = jnp.logical_and(jk >= iq, n > 0)
    mask_next = jnp.logical_and(jk <= iq, n < nb - 1)
    q = q_ref[...]
    states = []
    for g in range(2):
        q2 = _stack_heads(q, g)
        rows = _group_rows(g)
        state = _flash_init(2 * tq, g, (sink_ref[2 * g], sink_ref[2 * g + 1]))
        state = _flash_update(state, q2, ktc_ref[rows, :], _aug_values(vc_ref[...], g))
        state = _flash_update(state, q2, ktm_ref[rows, :], _aug_values(vm_ref[...], g))
        state = _flash_update(state, q2, ktp_ref[rows, :], _aug_values(vp_ref[...], g), mask_prev)
        state = _flash_update(state, q2, ktn_ref[rows, :], _aug_values(vn_ref[...], g), mask_next)
        states.append(state)
    o_ref[...] = _flash_finish(states, tq).astype(o_ref.dtype)


def _window_attn(q, kt, v, sink, *, batch, seq, ctx_len):
    nb = seq // CHUNK
    ctx0 = (batch * seq) // ctx_len

    def kspec(fn):
        return pl.BlockSpec((LANES, CHUNK), lambda b, n: (0, b * nb + fn(n)))

    def vspec(fn):
        return pl.BlockSpec((CHUNK, LANES), lambda b, n: (b * nb + fn(n), 0))

    prev = lambda n: jnp.maximum(n - 1, 0)
    cur = lambda n: n
    nxt = lambda n: jnp.minimum(n + 1, nb - 1)
    return pl.pallas_call(
        functools.partial(_window_attn_kernel, nb=nb),
        grid=(batch, nb),
        in_specs=[pl.BlockSpec(memory_space=pltpu.SMEM),
                  pl.BlockSpec((CHUNK, 256), lambda b, n: (b * nb + n, 0)),
                  pl.BlockSpec((LANES, ctx_len), lambda b, n: (0, ctx0 + b)),
                  pl.BlockSpec((ctx_len, LANES), lambda b, n: (ctx0 + b, 0)),
                  kspec(prev), vspec(prev), kspec(cur), vspec(cur), kspec(nxt), vspec(nxt)],
        out_specs=pl.BlockSpec((CHUNK, 256), lambda b, n: (b * nb + n, 0)),
        out_shape=jax.ShapeDtypeStruct((batch * seq, 256), BF16),
        compiler_params=_params(("parallel", "parallel")),
        name="window_attn",
    )(sink, q, kt, v, kt, v, kt, v, kt, v)


def _gelu_tanh(x):
    return 0.5 * x * (1.0 + jnp.tanh(math.sqrt(2.0 / math.pi) * (x + 0.044715 * (x * x * x))))


def _outproj_kernel(x_ref, gate_ref, gpost_ref, oa_ref, od_ref, yf_ref, yb_ref, xs_ref, z_ref, dsk_ref, gn_ref,
                    hf_ref, hb_ref, lg_ref, w_ref, o_ref):
    y_ssd = (yf_ref[...] + yb_ref[...] + xs_ref[...] * dsk_ref[...]) * _silu(z_ref[...])
    ob = _rms(y_ssd, gn_ref[...])
    oc = (hf_ref[...] + hb_ref[...]) * _gelu_tanh(lg_ref[...])
    y = (_dot(oa_ref[...], w_ref[0:256, :]) + _dot(ob.astype(BF16), w_ref[256:512, :])
         + _dot(oc.astype(BF16), w_ref[512:768, :]) + _dot(od_ref[...], w_ref[768:1024, :]))
    o_ref[...] = x_ref[...] + gate_ref[0] * _rms(y, gpost_ref[...])


def _outproj(xu, mod, gpost, oa, od, yf, yb, xbc, z, dsk, gn, hf, hb, lg, w_out, *, n_rows, n_lat, seq, batch):
    d = xu.shape[1]
    row = lambda w: pl.BlockSpec((TM, w), lambda i: (i, 0))
    const = lambda a: pl.BlockSpec(a.shape, lambda i: (0,) * a.ndim)
    return pl.pallas_call(
        _outproj_kernel,
        grid=(n_rows // TM,),
        in_specs=[row(d), pl.BlockSpec((1, 1, d), _mod_spec(2, n_lat, seq, batch, TM)), const(gpost),
                  row(256), row(256), row(256), row(256), row(256), row(256), const(dsk), const(gn),
                  row(256), row(256), row(256), const(w_out)],
        out_specs=row(d),
        out_shape=jax.ShapeDtypeStruct((n_rows, d), F32),
        compiler_params=_params(("parallel",)),
        name="outproj",
    )(xu, mod, gpost, oa, od, yf, yb, xbc, z, dsk, gn, hf, hb, lg, w_out)


def _ceil_seg(c):
    return jnp.floor((c + (SEG_ALIGN - 1)) * (1.0 / SEG_ALIGN)) * SEG_ALIGN


def _router_kernel(x_ref, shift_ref, scale_ref, gpre_ref, rwt_ref, rb_ref, hb_ref, ld_ref, wk_ref, tab_ref):
    h = _rms(x_ref[...], gpre_ref[...])
    h = h * (1.0 + scale_ref[0]) + shift_ref[0]
    hb = h.astype(BF16)
    hb_ref[...] = hb

    scores = jax.nn.sigmoid(_dot_nt(rwt_ref[...], hb))
    biased = scores + rb_ref[...]
    gsz = N_EXPERTS // N_EXPERT_GROUPS
    sub = lax.broadcasted_iota(jnp.int32, (gsz, TM), 0)
    blocks, gscore = [], []
    for g in range(N_EXPERT_GROUPS):
        blk = biased[g * gsz:(g + 1) * gsz, :]
        m1 = jnp.max(blk, axis=0, keepdims=True)
        first = jnp.min(jnp.where(blk == m1, sub, gsz), axis=0, keepdims=True)
        m2 = jnp.max(jnp.where(sub == first, -jnp.inf, blk), axis=0, keepdims=True)
        blocks.append(blk)
        gscore.append(m1 + m2)
    masked = []
    for g in range(N_EXPERT_GROUPS):
        rank = jnp.zeros((1, TM), F32)
        for g2 in range(N_EXPERT_GROUPS):
            if g2 == g:
                continue
            beats = (gscore[g2] > gscore[g]) | ((gscore[g2] == gscore[g]) if g2 < g else False)
            rank = rank + jnp.where(beats, 1.0, 0.0)
        masked.append(jnp.where(rank < TOPK_GROUPS, blocks[g], -jnp.inf))
    vals = jnp.concatenate(masked, axis=0)
    eidx = lax.broadcasted_iota(jnp.int32, (N_EXPERTS, TM), 0)
    rank = jnp.zeros((N_EXPERTS, TM), F32)
    for e2 in range(N_EXPERTS):
        rowv = vals[e2:e2 + 1, :]
        beats = (rowv > vals) | ((rowv == vals) & (eidx > e2))
        rank = rank + jnp.where(beats, 1.0, 0.0)
    sel = rank < TOP_K
    self32 = jnp.where(sel, 1.0, 0.0)
    picked = jnp.where(sel, scores, 0.0)
    wdense = picked / jnp.sum(picked, axis=0, keepdims=True) * ROUTED_SCALE

    tr = lax.broadcasted_iota(jnp.int32, (TM, TM), 0)
    tc = lax.broadcasted_iota(jnp.int32, (TM, TM), 1)
    before = jnp.where(tr < tc, 1.0, 0.0).astype(BF16)
    selb = self32.astype(BF16)
    pos = _dot(selb, before)
    er = lax.broadcasted_iota(jnp.int32, (N_EXPERTS, N_EXPERTS), 0)
    ec = lax.broadcasted_iota(jnp.int32, (N_EXPERTS, N_EXPERTS), 1)
    lower = jnp.where(ec < er, 1.0, 0.0).astype(BF16)
    upper = jnp.where(er < ec, 1.0, 0.0).astype(BF16)
    ksel = _dot(lower, selb)
    cnt_col = _ceil_seg(jnp.sum(self32, axis=1, keepdims=True))
    loc_col = _dot3_left(lower, jnp.broadcast_to(cnt_col, (N_EXPERTS, LANES)))[:, 0:1]
    cnt_row = _ceil_seg(_dot_nt(jnp.ones((SUBLANES, TM), BF16), selb))
    loc_row = _dot3(cnt_row, upper)
    tab_ref[...] = jnp.concatenate([cnt_row, loc_row], axis=1).astype(jnp.int32)

    r8 = lax.broadcasted_iota(jnp.int32, (TOP_K, TM), 0)
    ld = jnp.zeros((TOP_K, TM), F32)
    wk = jnp.zeros((TOP_K, TM), F32)
    stage_row = pos + loc_col
    for k in range(TOP_K):
        one = sel & (ksel == float(k))
        ld = jnp.where(r8 == k, jnp.sum(jnp.where(one, stage_row, 0.0), axis=0, keepdims=True), ld)
        wk = jnp.where(r8 == k, jnp.sum(jnp.where(one, wdense, 0.0), axis=0, keepdims=True), wk)
    ld_ref[...] = ld.astype(jnp.int32)
    wk_ref[...] = wk


def _router(xu, mod, gpre, rwt, rb, *, n_rows, n_lat, seq, batch):
    d = xu.shape[1]
    row = lambda w: pl.BlockSpec((TM, w), lambda i: (i, 0))
    col = pl.BlockSpec((TOP_K, TM), lambda i: (0, i))
    const = lambda a: pl.BlockSpec(a.shape, lambda i: (0,) * a.ndim)
    return pl.pallas_call(
        _router_kernel,
        grid=(n_rows // TM,),
        in_specs=[row(d), pl.BlockSpec((1, 1, d), _mod_spec(3, n_lat, seq, batch, TM)),
                  pl.BlockSpec((1, 1, d), _mod_spec(4, n_lat, seq, batch, TM)),
                  const(gpre), const(rwt), const(rb)],
        out_specs=(row(d), col, col, pl.BlockSpec((SUBLANES, 2 * N_EXPERTS), lambda i: (i, 0))),
        out_shape=(jax.ShapeDtypeStruct((n_rows, d), BF16),
                   jax.ShapeDtypeStruct((TOP_K, n_rows), jnp.int32),
                   jax.ShapeDtypeStruct((TOP_K, n_rows), F32),
                   jax.ShapeDtypeStruct((n_rows // TM * SUBLANES, 2 * N_EXPERTS), jnp.int32)),
        compiler_params=_params(("parallel",)),
        name="router",
    )(xu, mod, mod, gpre, rwt, rb)


def _pow2_pieces(limit):
    bits, b = [], limit
    while b >= SEG_ALIGN:
        bits.append(b)
        b //= 2
    return bits


def _copy_pieces(n, src_ref, src0, dst_ref, dst0, sem, limit, wait, same_src=False):
    for bit in _pow2_pieces(limit):
        @pl.when((n & bit) != 0)
        def _():
            off = n & ~(2 * bit - 1)
            cp = pltpu.make_async_copy(src_ref.at[pl.ds(pl.multiple_of(src0 + (0 if same_src else off), SEG_ALIGN),
                                                          bit)],
                                       dst_ref.at[pl.ds(pl.multiple_of(dst0 + off, SEG_ALIGN), bit)], sem)
            cp.wait() if wait else cp.start()


def _segment_copies(tile, cnt_ref, loc_ref, seg_ref, stage_ref, slots_ref, sem, to_slots, wait):
    def body(e, c):
        idx = tile * N_EXPERTS + e
        if to_slots:
            _copy_pieces(cnt_ref[idx], stage_ref, loc_ref[idx], slots_ref, seg_ref[idx], sem, TM, wait)
        else:
            _copy_pieces(cnt_ref[idx], slots_ref, seg_ref[idx], stage_ref, loc_ref[idx], sem, TM, wait)
        return c
    lax.fori_loop(0, N_EXPERTS, body, 0)


def _used_blocks(tile, cnt_ref, loc_ref):
    last = tile * N_EXPERTS + N_EXPERTS - 1
    return (loc_ref[last] + cnt_ref[last] + TM - 1) // TM


def _dispatch_kernel(cnt_ref, loc_ref, seg_ref, pstart_ref, npad_ref, hb_ref, ld_ref, xs_ref, stage, zbuf, sem, zsem):
    i = pl.program_id(0)

    @pl.when(i == 0)
    def _():
        zbuf[...] = jnp.zeros_like(zbuf)
        for wait in (False, True):
            def body(e, c, wait=wait):
                _copy_pieces(npad_ref[e], zbuf, 0, xs_ref, pstart_ref[e], zsem, BM_EXPERT // 2, wait, same_src=True)
                return c
            lax.fori_loop(0, N_EXPERTS, body, 0)

    ld = ld_ref[...]
    hb = hb_ref[...]
    jrow = lax.broadcasted_iota(jnp.int32, (TM, TM), 0)

    def block(b, c):
        base = pl.multiple_of(b * TM, TM)
        rel = ld - base
        onehot = jnp.where(rel[0:1, :] == jrow, 1.0, 0.0)
        for k in range(1, TOP_K):
            onehot = jnp.where(rel[k:k + 1, :] == jrow, 1.0, onehot)
        stage[pl.ds(base, TM), :] = _dot(onehot.astype(BF16), hb).astype(BF16)
        return c

    lax.fori_loop(0, _used_blocks(i, cnt_ref, loc_ref), block, 0)
    _segment_copies(i, cnt_ref, loc_ref, seg_ref, stage, xs_ref, sem, True, False)
    _segment_copies(i, cnt_ref, loc_ref, seg_ref, stage, xs_ref, sem, True, True)


def _dispatch(tabs, pad_start, n_pad, hb, ld, n_slots):
    n, d = hb.shape
    grid_spec = pltpu.PrefetchScalarGridSpec(
        num_scalar_prefetch=5,
        grid=(n // TM,),
        in_specs=[pl.BlockSpec((TM, d), lambda i, *_: (i, 0)),
                  pl.BlockSpec((TOP_K, TM), lambda i, *_: (0, i))],
        out_specs=pl.BlockSpec(memory_space=pl.ANY),
        scratch_shapes=[pltpu.VMEM((STAGE_ROWS, d), BF16), pltpu.VMEM((BM_EXPERT // 2, d), BF16),
                        pltpu.SemaphoreType.DMA(()), pltpu.SemaphoreType.DMA(())],
    )
    return pl.pallas_call(
        _dispatch_kernel,
        grid_spec=grid_spec,
        out_shape=jax.ShapeDtypeStruct((n_slots, d), BF16),
        compiler_params=_params(("arbitrary",)),
        name="moe_dispatch",
    )(*tabs, pad_start, n_pad, hb, ld)


def _expert_kernel(be_ref, na_ref, xs_ref, wg_ref, wu_ref, wd_ref, ys_ref, wgb, wub, wdb):
    i = pl.program_id(0)

    @pl.when(i < na_ref[0])
    def _():
        @pl.when(jnp.logical_or(i == 0, be_ref[i] != be_ref[jnp.maximum(i - 1, 0)]))
        def _():
            wgb[...] = wg_ref[0, 0].astype(BF16)
            wub[...] = wu_ref[0, 0].astype(BF16)
            wdb[...] = wd_ref[0, 0].astype(BF16)

        xb = xs_ref[...]
        hid = _silu(_dot(xb, wgb[...])) * _dot(xb, wub[...])
        ys_ref[...] = _dot(hid.astype(BF16), wdb[...]).astype(ys_ref.dtype)


def _experts(block_e, n_active, xs, wg, wu, wd, layer):
    n_slots, d = xs.shape
    nb = n_slots // BM_EXPERT
    blk = lambda i, be, na: jnp.minimum(i, na[0] - 1)
    tiles = pl.BlockSpec((BM_EXPERT, d), lambda i, be, na: (blk(i, be, na), 0))
    grid_spec = pltpu.PrefetchScalarGridSpec(
        num_scalar_prefetch=2,
        grid=(nb,),
        in_specs=[tiles,
                  pl.BlockSpec((1, 1, d, EXPERT_HIDDEN), lambda i, be, na: (layer, be[blk(i, be, na)], 0, 0)),
                  pl.BlockSpec((1, 1, d, EXPERT_HIDDEN), lambda i, be, na: (layer, be[blk(i, be, na)], 0, 0)),
                  pl.BlockSpec((1, 1, EXPERT_HIDDEN, d), lambda i, be, na: (layer, be[blk(i, be, na)], 0, 0))],
        out_specs=tiles,
        scratch_shapes=[pltpu.VMEM((d, EXPERT_HIDDEN), BF16), pltpu.VMEM((d, EXPERT_HIDDEN), BF16),
                        pltpu.VMEM((EXPERT_HIDDEN, d), BF16)],
    )
    return pl.pallas_call(
        _expert_kernel,
        grid_spec=grid_spec,
        out_shape=jax.ShapeDtypeStruct((n_slots, d), BF16),
        compiler_params=_params(("arbitrary",)),
        name="moe_experts",
    )(block_e, n_active, xs, wg, wu, wd)


def _combine_kernel(cnt_ref, loc_ref, seg_ref, ys_ref, ldt_ref, wkt_ref, hb_ref, x_ref, gate_ref, gpost_ref,
                    sg_ref, su_ref, sd_ref, o_ref, stage, acc_ref, sem):
    i = pl.program_id(0)

    @pl.when(i == 0)
    def _():
        stage[...] = jnp.zeros_like(stage)

    _segment_copies(i, cnt_ref, loc_ref, seg_ref, stage, ys_ref, sem, False, False)
    hb = hb_ref[...]
    acc_ref[...] = _dot((_silu(_dot(hb, sg_ref[...])) * _dot(hb, su_ref[...])).astype(BF16), sd_ref[...])
    _segment_copies(i, cnt_ref, loc_ref, seg_ref, stage, ys_ref, sem, False, True)

    ldt = ldt_ref[...]
    wkt = wkt_ref[...]
    ld_b = [jnp.broadcast_to(ldt[:, k:k + 1], (TM, LANES)) for k in range(TOP_K)]
    w_b = [jnp.broadcast_to(wkt[:, k:k + 1], (TM, LANES)) for k in range(TOP_K)]
    lane = lax.broadcasted_iota(jnp.int32, (TM, LANES), 1)

    def block(b, c):
        base = pl.multiple_of(b * TM, TM)
        cols = []
        for j in range(TM // LANES):
            want = lane + (base + j * LANES)
            piece = jnp.where(ld_b[0] == want, w_b[0], 0.0)
            for k in range(1, TOP_K):
                piece = jnp.where(ld_b[k] == want, w_b[k], piece)
            cols.append(piece)
        weights = jnp.concatenate(cols, axis=1).astype(BF16)
        acc_ref[...] += _dot(weights, stage[pl.ds(base, TM), :])
        return c

    lax.fori_loop(0, _used_blocks(i, cnt_ref, loc_ref), block, 0)
    o_ref[...] = x_ref[...] + gate_ref[0] * _rms(acc_ref[...], gpost_ref[...])


def _combine(tabs, ys, ldt, wkt, hb, xu, mod, gpost, sg, su, sd, *, n_rows, n_lat, seq, batch):
    d = xu.shape[1]
    row = lambda w: pl.BlockSpec((TM, w), lambda i, *_: (i, 0))
    const = lambda a: pl.BlockSpec(a.shape, lambda i, *_: (0,) * a.ndim)
    mod_map = _mod_spec(5, n_lat, seq, batch, TM)
    grid_spec = pltpu.PrefetchScalarGridSpec(
        num_scalar_prefetch=3,
        grid=(n_rows // TM,),
        in_specs=[pl.BlockSpec(memory_space=pl.ANY), row(TOP_K), row(TOP_K), row(d), row(d),
                  pl.BlockSpec((1, 1, d), lambda i, *_: mod_map(i)),
                  const(gpost), const(sg), const(su), const(sd)],
        out_specs=row(d),
        scratch_shapes=[pltpu.VMEM((STAGE_ROWS, d), BF16), pltpu.VMEM((TM, d), F32), pltpu.SemaphoreType.DMA(())],
    )
    return pl.pallas_call(
        _combine_kernel,
        grid_spec=grid_spec,
        out_shape=jax.ShapeDtypeStruct((n_rows, d), F32),
        compiler_params=_params(("arbitrary",)),
        name="moe_combine",
    )(*tabs, ys, ldt, wkt, hb, xu, mod, gpost, sg, su, sd)


def _deinterleave(w):
    cols = w.shape[-1]
    perm = jnp.concatenate([jnp.arange(0, HEAD_DIM, 2), jnp.arange(1, HEAD_DIM, 2)])
    idx = (jnp.arange(cols // HEAD_DIM)[:, None] * HEAD_DIM + perm[None, :]).reshape(-1)
    return w[..., idx]


def _pad_in_proj(w_in):
    d = w_in.shape[0]
    o = 0
    parts = {}
    for name, width in (("qa", 256), ("ka", 128), ("va", 128), ("z", 256), ("xs", 256), ("bm", 128), ("cm", 128),
                        ("dtf", 4), ("dtb", 4), ("lx", 256), ("lg", 256), ("qd", 256), ("kd", 128), ("vd", 128)):
        parts[name] = w_in[:, o:o + width]
        o += width
    dt = jnp.concatenate([parts["dtf"], parts["dtb"], jnp.zeros((d, LANES - 8), w_in.dtype)], axis=1)
    cols = [_deinterleave(parts["qa"]), _deinterleave(parts["ka"]), parts["va"],
            _deinterleave(parts["qd"]), _deinterleave(parts["kd"]), parts["vd"],
            parts["z"], parts["xs"], parts["bm"], parts["cm"], dt, parts["lx"], parts["lg"]]
    return jnp.concatenate(cols, axis=1).astype(BF16)


def _rope_tables(seq):
    t = jnp.arange(seq)
    rowp = (t // GRID_W).astype(F32)
    colp = (t % GRID_W).astype(F32)
    axis_dim = HEAD_DIM // 2
    inv_freq = ROPE_THETA ** (-jnp.arange(0, axis_dim, 2, dtype=F32) / axis_dim)
    ang = jnp.concatenate([rowp[:, None] * inv_freq, colp[:, None] * inv_freq], axis=-1)
    cos, sin = jnp.cos(ang), jnp.sin(ang)
    cos_h = jnp.concatenate([cos, cos], axis=-1)
    sin_h = jnp.concatenate([-sin, sin], axis=-1)
    return jnp.tile(cos_h, (1, 4)), jnp.tile(sin_h, (1, 4))


def _block_diag(w):
    nb, bd, _ = w.shape
    eye = jnp.eye(nb, dtype=w.dtype)
    return (eye[:, None, :, None] * w[:, :, None, :]).reshape(nb * bd, nb * bd)


def _lane_row(fwd, bwd):
    return jnp.concatenate([fwd, bwd, jnp.zeros((LANES - 8,), F32)]).reshape(1, LANES)


def kernel(x, c, ctx, c_ctx, w_ada, b_ada, g_mix_pre, g_mix_post, g_ffn_pre, g_ffn_post, w_in, w_out, a_sink,
           ssd_conv_w, ssd_conv_b, ssd_dt_bias, ssd_a_log, ssd_d, ssd_norm, lru_conv_w, lru_conv_b, lru_w_a,
           lru_b_a, lru_w_i, lru_b_i, lru_lambda, d_q_norm, d_k_norm, router_w, router_bias, exp_w_gate,
           exp_w_up, exp_w_down, sh_w_gate, sh_w_up, sh_w_down):
    batch, seq, d = x.shape
    ctx_len = ctx.shape[1]
    depth = w_ada.shape[0]
    n_lat = batch * seq
    n_ctx = batch * ctx_len
    n_all = n_lat + n_ctx
    assert seq % TM == 0 and n_ctx % TM == 0 and seq % T_CONV == 0 and ctx_len % T_CONV == 0
    assert d == SUBLANES * LANES
    assert ctx_len <= KV_CHUNK and seq % KV_CHUNK == 0 and seq % TQ_GLOBAL == 0 and batch + 1 <= SUBLANES

    xu = jnp.concatenate([x.reshape(n_lat, d), ctx.reshape(n_ctx, d)], axis=0)
    cin = jnp.concatenate([c, c_ctx[None, :], jnp.zeros((SUBLANES - batch - 1, d), F32)], axis=0)
    mod_all = _adaln(cin, w_ada, b_ada)
    cos_t, sin_t = _rope_tables(seq)
    hm = jnp.kron(jnp.eye(4, dtype=F32), jnp.full((HEAD_DIM, HEAD_DIM), 1.0 / HEAD_DIM, F32)).astype(BF16)

    for l in range(depth):
        with_ctx = l < depth - 1
        mod = mod_all[l].reshape(SUBLANES * 6, 1, d)
        gq = jnp.tile(_deinterleave(d_q_norm[l]), 4).reshape(1, 256)
        gk = jnp.tile(_deinterleave(d_k_norm[l]), 2).reshape(1, LANES)
        qa, kat, va, qd, kdt, vd, z, xbc_raw, dt, lx_raw, lg = _inproj(
            xu, mod, g_mix_pre[l].reshape(1, d), _pad_in_proj(w_in[l]), cos_t, sin_t, gq, gk, hm,
            n_lat=n_lat, seq=seq, batch=batch)

        xbc, lu = _conv(xbc_raw, lx_raw, ssd_conv_w[l], ssd_conv_b[l].reshape(1, -1),
                        lru_conv_w[l], lru_conv_b[l].reshape(1, -1), n_lat=n_lat, seq=seq, ctx_len=ctx_len)
        yf, yb = _ssd(xbc, dt, _lane_row(ssd_dt_bias[l, 0], ssd_dt_bias[l, 1]),
                      _lane_row(ssd_a_log[l, 0], ssd_a_log[l, 1]), batch=batch, seq=seq, ctx_len=ctx_len)
        wg = jnp.stack([jnp.concatenate([_block_diag(lru_w_a[l, dd]), _block_diag(lru_w_i[l, dd])], axis=1)
                        for dd in range(2)]).astype(BF16)
        bg = jnp.concatenate([lru_b_a[l], lru_b_i[l]], axis=1).reshape(2, 1, 2 * LRU_WIDTH)
        hf, hb = _lru(lu, wg, bg, lru_lambda[l].reshape(2, 1, LRU_WIDTH), batch=batch, seq=seq, ctx_len=ctx_len)

        oa = _window_attn(qa, kat, va, a_sink[l], batch=batch, seq=seq, ctx_len=ctx_len)
        od = _dense_attn(qd, kdt, vd, None, q_row0=0, q_len=seq, tq=TQ_GLOBAL,
                         segs=[(n_lat, ctx_len), (0, seq)], batch=batch)
        if with_ctx:
            oa_c = _dense_attn(qa, kat, va, a_sink[l], q_row0=n_lat, q_len=ctx_len, tq=ctx_len,
                               segs=[(n_lat, ctx_len)], batch=batch)
            od_c = _dense_attn(qd, kdt, vd, None, q_row0=n_lat, q_len=ctx_len, tq=ctx_len,
                               segs=[(n_lat, ctx_len)], batch=batch)
            oa = jnp.concatenate([oa, oa_c], axis=0)
            od = jnp.concatenate([od, od_c], axis=0)
        n_rows = n_all if with_ctx else n_lat

        dsk = jnp.repeat(ssd_d[l], HEAD_DIM).reshape(1, 256)
        xu_mid = _outproj(xu, mod, g_mix_post[l].reshape(1, d), oa, od, yf, yb, xbc, z, dsk,
                          ssd_norm[l].reshape(1, 256), hf, hb, lg, w_out[l].astype(BF16),
                          n_rows=n_rows, n_lat=n_lat, seq=seq, batch=batch)

        hb_ffn, ld, wk, tab = _router(xu_mid, mod, g_ffn_pre[l].reshape(1, d), router_w[l].T.astype(BF16),
                                      router_bias[l].reshape(N_EXPERTS, 1), n_rows=n_rows, n_lat=n_lat,
                                      seq=seq, batch=batch)
        n_tiles = n_rows // TM
        tab = tab.reshape(n_tiles, SUBLANES, 2 * N_EXPERTS)[:, 0, :]
        seg_cnt, seg_loc = tab[:, :N_EXPERTS], tab[:, N_EXPERTS:]
        counts = jnp.sum(seg_cnt, axis=0)
        padded = (counts + BM_EXPERT - 1) // BM_EXPERT * BM_EXPERT
        padded_end = jnp.cumsum(padded)
        offs = padded_end - padded
        seg_off = offs[None, :] + jnp.cumsum(seg_cnt, axis=0) - seg_cnt
        n_blocks = (n_rows * TOP_K + n_tiles * N_EXPERTS * SEG_ALIGN) // BM_EXPERT + N_EXPERTS
        n_active = (padded_end[-1] // BM_EXPERT).astype(jnp.int32).reshape(1)
        block_start = jnp.arange(n_blocks, dtype=jnp.int32) * BM_EXPERT
        block_e = jnp.minimum(jnp.sum((padded_end[None, :] <= block_start[:, None]).astype(jnp.int32), axis=1),
                              N_EXPERTS - 1)
        tabs = (seg_cnt.reshape(-1), seg_loc.reshape(-1), seg_off.reshape(-1))
        xs = _dispatch(tabs, offs + counts, padded - counts, hb_ffn, ld, n_blocks * BM_EXPERT)
        ys = _experts(block_e, n_active, xs, exp_w_gate, exp_w_up, exp_w_down, l)
        xu = _combine(tabs, ys, ld.T, wk.T, hb_ffn, xu_mid, mod, g_ffn_post[l].reshape(1, d), sh_w_gate[l].astype(BF16),
                      sh_w_up[l].astype(BF16), sh_w_down[l].astype(BF16),
                      n_rows=n_rows, n_lat=n_lat, seq=seq, batch=batch)
    return xu[:n_lat].reshape(batch, seq, d)
```

```python
import functools
import math

import jax
import jax.numpy as jnp
from jax import lax
from jax.experimental import pallas as pl
from jax.experimental.pallas import tpu as pltpu

F32 = jnp.float32
BF16 = jnp.bfloat16

HEAD_DIM = 64
GRID_W = 64
ROPE_THETA = 10000.0
NORM_EPS = 1e-6
NEG_INF = -1e30
A_HEADS, A_KV_HEADS, WINDOW = 4, 2, 128
SSD_HEADS, SSD_GROUPS, SSD_STATE, SSD_CONV = 4, 2, 64, 4
LRU_WIDTH, LRU_BLOCKS, LRU_CONV, LRU_C = 256, 4, 4, 8.0
D_HEADS, D_KV_HEADS = 4, 2
N_EXPERTS, N_EXPERT_GROUPS, TOPK_GROUPS, TOP_K = 64, 8, 4, 8
EXPERT_HIDDEN, SHARED_HIDDEN = 256, 256
ROUTED_SCALE = 2.5

LANES = 128
SUBLANES = 8

TM = 512
T_CONV = 256
CHUNK = 128
TQ_GLOBAL = 256
KV_CHUNK = 512
BM_EXPERT = 512
SEG_ALIGN = 16
STAGE_ROWS = TM * TOP_K + N_EXPERTS * SEG_ALIGN
STAGE_PIECES = STAGE_ROWS // SEG_ALIGN
VMEM_LIMIT = 48 * 1024 * 1024

C_QA, C_KA, C_VA = 0, 256, 384
C_QD, C_KD, C_VD = 512, 768, 896
C_Z, C_XBC, C_DT = 1024, 1280, 1792
C_LX, C_LG = 1920, 2176
NP_IN = 2432


def _dot(a, b):
    return jnp.dot(a, b, preferred_element_type=F32)


def _dot_nt(a, b):
    return lax.dot_general(a, b, (((1,), (1,)), ((), ())), preferred_element_type=F32)


def _dot3(a, b):
    a1 = a.astype(BF16)
    r1 = a - a1.astype(F32)
    a2 = r1.astype(BF16)
    a3 = (r1 - a2.astype(F32)).astype(BF16)
    return _dot(a1, b) + _dot(a2, b) + _dot(a3, b)


def _dot3_left(a, b):
    b1 = b.astype(BF16)
    r1 = b - b1.astype(F32)
    b2 = r1.astype(BF16)
    b3 = (r1 - b2.astype(F32)).astype(BF16)
    return _dot(a, b1) + _dot(a, b2) + _dot(a, b3)


def _silu(x):
    return x * jax.nn.sigmoid(x)


def _softplus(x):
    return jnp.maximum(x, 0.0) + jnp.log1p(jnp.exp(-jnp.abs(x)))


def _rms(x, gain):
    return x * lax.rsqrt(jnp.mean(x * x, axis=-1, keepdims=True) + NORM_EPS) * gain


def _params(sem=None):
    return pltpu.CompilerParams(dimension_semantics=sem, vmem_limit_bytes=VMEM_LIMIT)


def _adaln_kernel(c_ref, w_ref, b_ref, o_ref):
    s = _silu(c_ref[...])
    o_ref[0] = _dot(s.astype(BF16), w_ref[0].astype(BF16)) + b_ref[0]


def _adaln(cin, w_ada, b_ada):
    depth, d, n6 = w_ada.shape
    tn = 1024
    return pl.pallas_call(
        _adaln_kernel,
        grid=(depth, n6 // tn),
        in_specs=[pl.BlockSpec((SUBLANES, d), lambda l, j: (0, 0)),
                  pl.BlockSpec((1, d, tn), lambda l, j: (l, 0, j)),
                  pl.BlockSpec((1, 1, tn), lambda l, j: (l, 0, j))],
        out_specs=pl.BlockSpec((1, SUBLANES, tn), lambda l, j: (l, 0, j)),
        out_shape=jax.ShapeDtypeStruct((depth, SUBLANES, n6), F32),
        compiler_params=_params(("parallel", "parallel")),
        name="adaln",
    )(cin, w_ada, b_ada.reshape(depth, 1, n6))


def _swap_halves(t):
    w = t.shape[1]
    lane = lax.broadcasted_iota(jnp.int32, (1, w), 1)
    first = (lane & 32) == 0
    return jnp.where(first, pltpu.roll(t, w - 32, axis=1), pltpu.roll(t, 32, axis=1))


def _inproj_kernel(x_ref, shift_ref, scale_ref, gpre_ref, w_ref, cos_ref, sin_ref, gq_ref, gk_ref, hm_ref,
                   qa_ref, kat_ref, va_ref, qd_ref, kdt_ref, vd_ref, z_ref, xbc_ref, dt_ref, lx_ref, lg_ref,
                   *, n_lat):
    i = pl.program_id(0)
    is_lat = i * TM < n_lat
    h = _rms(x_ref[...], gpre_ref[...])
    h = h * (1.0 + scale_ref[0]) + shift_ref[0]
    hb = h.astype(BF16)

    def sec(a, b):
        return _dot(hb, w_ref[:, a:b])

    cos = jnp.where(is_lat, cos_ref[...], 1.0)
    sin = jnp.where(is_lat, sin_ref[...], 0.0)

    def rope(t):
        w = t.shape[1]
        return t * cos[:, :w] + _swap_halves(t) * sin[:, :w]

    def head_norm(t, gain):
        w = t.shape[1]
        ms = _dot3(t * t, hm_ref[:w, :w])
        return t * lax.rsqrt(ms + NORM_EPS) * gain

    scale = HEAD_DIM ** -0.5
    qa_ref[...] = (rope(sec(C_QA, C_KA)) * scale).astype(BF16)
    kat_ref[...] = rope(sec(C_KA, C_VA)).T.astype(BF16)
    va_ref[...] = sec(C_VA, C_QD).astype(BF16)
    qd_ref[...] = (rope(head_norm(sec(C_QD, C_KD), gq_ref[...])) * scale).astype(BF16)
    kdt_ref[...] = rope(head_norm(sec(C_KD, C_VD), gk_ref[...])).T.astype(BF16)
    vd_ref[...] = sec(C_VD, C_Z).astype(BF16)
    z_ref[...] = sec(C_Z, C_XBC)
    xbc_ref[...] = sec(C_XBC, C_DT)
    dt_ref[...] = sec(C_DT, C_LX)
    lx_ref[...] = sec(C_LX, C_LG)
    lg_ref[...] = sec(C_LG, NP_IN)


def _mod_spec(chunk, n_lat, seq, batch, tile):
    def imap(i):
        row0 = i * tile
        seg = jnp.where(row0 < n_lat, row0 // seq, batch)
        return (seg * 6 + chunk, 0, 0)
    return imap


def _inproj(xu, mod, gpre, w_pad, cos_t, sin_t, gq, gk, hm, *, n_lat, seq, batch):
    n, d = xu.shape
    nt = n // TM
    spt = seq // TM
    row = lambda w: pl.BlockSpec((TM, w), lambda i: (i, 0))
    colT = pl.BlockSpec((LANES, TM), lambda i: (0, i))
    const = lambda a: pl.BlockSpec(a.shape, lambda i: (0,) * a.ndim)
    out_shapes = (
        jax.ShapeDtypeStruct((n, 256), BF16), jax.ShapeDtypeStruct((LANES, n), BF16),
        jax.ShapeDtypeStruct((n, LANES), BF16),
        jax.ShapeDtypeStruct((n, 256), BF16), jax.ShapeDtypeStruct((LANES, n), BF16),
        jax.ShapeDtypeStruct((n, LANES), BF16),
        jax.ShapeDtypeStruct((n, 256), F32), jax.ShapeDtypeStruct((n, 512), F32),
        jax.ShapeDtypeStruct((n, LANES), F32), jax.ShapeDtypeStruct((n, 256), F32),
        jax.ShapeDtypeStruct((n, 256), F32))
    return pl.pallas_call(
        functools.partial(_inproj_kernel, n_lat=n_lat),
        grid=(nt,),
        in_specs=[row(d),
                  pl.BlockSpec((1, 1, d), _mod_spec(0, n_lat, seq, batch, TM)),
                  pl.BlockSpec((1, 1, d), _mod_spec(1, n_lat, seq, batch, TM)),
                  const(gpre), const(w_pad),
                  pl.BlockSpec((TM, 256), lambda i: (i % spt, 0)),
                  pl.BlockSpec((TM, 256), lambda i: (i % spt, 0)),
                  const(gq), const(gk), const(hm)],
        out_specs=(row(256), colT, row(LANES), row(256), colT, row(LANES),
                   row(256), row(512), row(LANES), row(256), row(256)),
        out_shape=out_shapes,
        compiler_params=_params(("parallel",)),
        name="inproj",
    )(xu, mod, mod, gpre, w_pad, cos_t, sin_t, gq, gk, hm)


def _conv_kernel(xs_ref, xsp_ref, xsn_ref, xl_ref, xlp_ref, xln_ref, ws_ref, bs_ref, wl_ref, bl_ref,
                 os_ref, ol_ref, *, n_lat, seq, ctx_len):
    i = pl.program_id(0)
    row0 = i * T_CONV
    pos = jnp.where(row0 < n_lat, row0 % seq, (row0 - n_lat) % ctx_len)
    slen = jnp.where(row0 < n_lat, seq, ctx_len)
    first = pos == 0
    last = pos + T_CONV == slen
    row = lax.broadcasted_iota(jnp.int32, (T_CONV, 1), 0)

    def conv(x, prev, nxt, w, b):
        pm = jnp.where(first, 0.0, prev)
        nx = jnp.where(last, 0.0, nxt)
        xm1 = jnp.where(row == 0, pm[7:8, :], pltpu.roll(x, 1, axis=0))
        xm2 = jnp.where(row == 0, pm[6:7, :], jnp.where(row == 1, pm[7:8, :], pltpu.roll(x, 2, axis=0)))
        xp1 = jnp.where(row == T_CONV - 1, nx[0:1, :], pltpu.roll(x, T_CONV - 1, axis=0))
        return w[0:1, :] * xm2 + w[1:2, :] * xm1 + w[2:3, :] * x + w[3:4, :] * xp1 + b

    os_ref[...] = _silu(conv(xs_ref[...], xsp_ref[...], xsn_ref[...], ws_ref[...], bs_ref[...]))
    ol_ref[...] = conv(xl_ref[...], xlp_ref[...], xln_ref[...], wl_ref[...], bl_ref[...])


def _conv(xbc_raw, lx_raw, ws, bs, wl, bl, *, n_lat, seq, ctx_len):
    n = xbc_raw.shape[0]
    nt = n // T_CONV
    r8 = T_CONV // SUBLANES
    n8 = n // SUBLANES
    main = lambda w: pl.BlockSpec((T_CONV, w), lambda i: (i, 0))
    prev = lambda w: pl.BlockSpec((SUBLANES, w), lambda i: (jnp.maximum(i * r8 - 1, 0), 0))
    nxt = lambda w: pl.BlockSpec((SUBLANES, w), lambda i: (jnp.minimum((i + 1) * r8, n8 - 1), 0))
    const = lambda a: pl.BlockSpec(a.shape, lambda i: (0,) * a.ndim)
    return pl.pallas_call(
        functools.partial(_conv_kernel, n_lat=n_lat, seq=seq, ctx_len=ctx_len),
        grid=(nt,),
        in_specs=[main(512), prev(512), nxt(512), main(256), prev(256), nxt(256),
                  const(ws), const(bs), const(wl), const(bl)],
        out_specs=(main(512), main(256)),
        out_shape=(jax.ShapeDtypeStruct((n, 512), F32), jax.ShapeDtypeStruct((n, 256), F32)),
        compiler_params=_params(("parallel",)),
        name="conv",
    )(xbc_raw, xbc_raw, xbc_raw, lx_raw, lx_raw, lx_raw, ws, bs, wl, bl)


def _chunk_maps(batch, seq, ctx_len):
    ncx = ctx_len // CHUNK
    nl = seq // CHUNK
    lat_blocks = batch * nl

    def block(b, c):
        return jnp.where(c < ncx, lat_blocks + b * ncx + c, b * nl + (c - ncx))

    def fwd(b, k):
        return (block(b, k), 0)

    def bwd(b, k):
        c = jnp.where(k < ncx, ncx - 1 - k, ncx + (nl - 1 - (k - ncx)))
        return (block(b, c), 0)

    return fwd, bwd, ncx + nl


def _ssd_kernel(xf_ref, dtf_ref, xb_ref, dtb_ref, dtbias_ref, alog_ref, yf_ref, yb_ref, state_ref):
    k = pl.program_id(1)

    @pl.when(k == 0)
    def _():
        state_ref[...] = jnp.zeros_like(state_ref)

    ri = lax.broadcasted_iota(jnp.int32, (CHUNK, CHUNK), 0)
    ci = lax.broadcasted_iota(jnp.int32, (CHUNK, CHUNK), 1)
    lane_lo = ci < HEAD_DIM
    aneg = -jnp.exp(alog_ref[...])
    dtbias = dtbias_ref[...]

    for d, (x_ref, dt_ref, y_ref) in enumerate(((xf_ref, dtf_ref, yf_ref), (xb_ref, dtb_ref, yb_ref))):
        causal = (ri >= ci) if d == 0 else (ci >= ri)
        tmat = jnp.where(causal, 1.0, 0.0).astype(BF16)
        xs = x_ref[:, 0:256]
        bm = x_ref[:, 256:384]
        cm = x_ref[:, 384:512]
        dtp = _softplus(dt_ref[...] + dtbias)
        acum = _dot3_left(tmat, dtp * aneg)
        acum_t = acum.T
        bt = bm.T.astype(BF16)
        cmb = cm.astype(BF16)
        bmb = bm.astype(BF16)
        tot_row = CHUNK - 1 if d == 0 else 0
        for p in range(2):
            cmask = jnp.where(lane_lo if p == 0 else jnp.logical_not(lane_lo), cmb, jnp.zeros_like(cmb))
            cb = _dot_nt(cmask, bmb)
            cols, dts, ys = [], [], []
            x_pair = xs[:, p * LANES:(p + 1) * LANES]
            for j in range(2):
                col = 4 * d + 2 * p + j
                colb = jnp.broadcast_to(acum[:, col:col + 1], (CHUNK, CHUNK))
                rowb = jnp.broadcast_to(acum_t[col:col + 1, :], (CHUNK, CHUNK))
                cols.append(colb)
                dts.append(jnp.broadcast_to(dtp[:, col:col + 1], (CHUNK, CHUNK)))
            col_pair = jnp.where(lane_lo, cols[0], cols[1])
            dt_pair = jnp.where(lane_lo, dts[0], dts[1])
            xdt = x_pair * dt_pair
            xdt_b = xdt.astype(BF16)
            for j in range(2):
                col = 4 * d + 2 * p + j
                rowb = jnp.broadcast_to(acum_t[col:col + 1, :], (CHUNK, CHUNK))
                decay = jnp.exp(jnp.where(causal, cols[j] - rowb, NEG_INF))
                ys.append(_dot((cb * decay).astype(BF16), xdt_b))
            y_intra = jnp.where(lane_lo, ys[0], ys[1])
            s_old = state_ref[d, p]
            y_inter = _dot(cmask, s_old.astype(BF16)) * jnp.exp(col_pair)
            y_ref[:, p * LANES:(p + 1) * LANES] = y_intra + y_inter
            tot_pair = col_pair[tot_row:tot_row + 1, :]
            to_end = jnp.exp(tot_pair - col_pair)
            state_ref[d, p] = s_old * jnp.exp(tot_pair) + _dot(bt, (xdt * to_end).astype(BF16))


def _ssd(xbc, dt, dtbias_row, alog_row, *, batch, seq, ctx_len):
    n = xbc.shape[0]
    fwd, bwd, steps = _chunk_maps(batch, seq, ctx_len)
    const = lambda a: pl.BlockSpec(a.shape, lambda b, k: (0,) * a.ndim)
    return pl.pallas_call(
        _ssd_kernel,
        grid=(batch, steps),
        in_specs=[pl.BlockSpec((CHUNK, 512), fwd), pl.BlockSpec((CHUNK, LANES), fwd),
                  pl.BlockSpec((CHUNK, 512), bwd), pl.BlockSpec((CHUNK, LANES), bwd),
                  const(dtbias_row), const(alog_row)],
        out_specs=(pl.BlockSpec((CHUNK, 256), fwd), pl.BlockSpec((CHUNK, 256), bwd)),
        out_shape=(jax.ShapeDtypeStruct((n, 256), F32), jax.ShapeDtypeStruct((n, 256), F32)),
        scratch_shapes=[pltpu.VMEM((2, 2, CHUNK, LANES), F32)],
        compiler_params=_params(("parallel", "arbitrary")),
        name="ssd_scan",
    )(xbc, dt, xbc, dt, dtbias_row, alog_row)


def _linear_scan(a, b, reverse):
    n = a.shape[0]
    row = lax.broadcasted_iota(jnp.int32, (n, 1), 0)
    s = 1
    while s < n:
        if reverse:
            ok = row < n - s
            a_sh = jnp.where(ok, pltpu.roll(a, n - s, axis=0), 1.0)
            b_sh = jnp.where(ok, pltpu.roll(b, n - s, axis=0), 0.0)
        else:
            ok = row >= s
            a_sh = jnp.where(ok, pltpu.roll(a, s, axis=0), 1.0)
            b_sh = jnp.where(ok, pltpu.roll(b, s, axis=0), 0.0)
        b = b + a * b_sh
        a = a * a_sh
        s *= 2
    return a, b


def _lru_kernel(uf_ref, ub_ref, wg_ref, bg_ref, lam_ref, hf_ref, hb_ref, carry_ref):
    k = pl.program_id(1)

    @pl.when(k == 0)
    def _():
        carry_ref[...] = jnp.zeros_like(carry_ref)

    for d, (u_ref, h_ref) in enumerate(((uf_ref, hf_ref), (ub_ref, hb_ref))):
        u = u_ref[...]
        gates = _dot(u.astype(BF16), wg_ref[d]) + bg_ref[d]
        r = jax.nn.sigmoid(gates[:, :LRU_WIDTH])
        ig = jax.nn.sigmoid(gates[:, LRU_WIDTH:])
        log_a = -LRU_C * r * _softplus(-lam_ref[d])
        a = jnp.exp(log_a)
        inp = jnp.sqrt(-jnp.tanh(log_a) * (1.0 + a * a)) * (ig * u)
        a_cum, b_cum = _linear_scan(a, inp, reverse=(d == 1))
        h = b_cum + a_cum * carry_ref[d, 0:1, :]
        h_ref[...] = h
        last = 0 if d == 1 else CHUNK - 1
        carry_ref[d, 0:1, :] = h[last:last + 1, :]


def _lru(u, wg, bg, lam, *, batch, seq, ctx_len):
    n = u.shape[0]
    fwd, bwd, steps = _chunk_maps(batch, seq, ctx_len)
    const = lambda a: pl.BlockSpec(a.shape, lambda b, k: (0,) * a.ndim)
    return pl.pallas_call(
        _lru_kernel,
        grid=(batch, steps),
        in_specs=[pl.BlockSpec((CHUNK, LRU_WIDTH), fwd), pl.BlockSpec((CHUNK, LRU_WIDTH), bwd),
                  const(wg), const(bg), const(lam)],
        out_specs=(pl.BlockSpec((CHUNK, LRU_WIDTH), fwd), pl.BlockSpec((CHUNK, LRU_WIDTH), bwd)),
        out_shape=(jax.ShapeDtypeStruct((n, LRU_WIDTH), F32), jax.ShapeDtypeStruct((n, LRU_WIDTH), F32)),
        scratch_shapes=[pltpu.VMEM((2, SUBLANES, LRU_WIDTH), F32)],
        compiler_params=_params(("parallel", "arbitrary")),
        name="lru_scan",
    )(u, u, wg, bg, lam)


def _stack_heads(q, g):
    qf = q.astype(F32)
    lo = g * LANES
    return jnp.concatenate([qf[:, lo:lo + HEAD_DIM], qf[:, lo + HEAD_DIM:lo + LANES]], axis=0).astype(BF16)


def _value_lanes(g):
    lane = lax.broadcasted_iota(jnp.int32, (1, LANES), 1)
    return (lane < HEAD_DIM) if g == 0 else (lane >= HEAD_DIM)


def _aug_values(v, g):
    return jnp.where(_value_lanes(g), v, jnp.ones_like(v))


def _flash_init(rows, g, sink_pair):
    if sink_pair is None:
        return jnp.full((rows, 1), NEG_INF, F32), jnp.zeros((rows, LANES), F32)
    half = lax.broadcasted_iota(jnp.int32, (rows, 1), 0) < rows // 2
    m = jnp.where(half, sink_pair[0], sink_pair[1]).astype(F32)
    acc = jnp.broadcast_to(jnp.where(_value_lanes(g), 0.0, 1.0), (rows, LANES))
    return m, acc


def _flash_update(state, q2, kt, v_aug, mask=None):
    m, acc = state
    s = _dot(q2, kt)
    if mask is not None:
        s = jnp.where(mask, s, NEG_INF)
    m_new = jnp.maximum(m, jnp.max(s, axis=-1, keepdims=True))
    p = jnp.exp(s - m_new).astype(BF16)
    acc = jnp.exp(m - m_new) * acc + _dot(p, v_aug)
    return m_new, acc


def _flash_finish(states, tq):
    pieces = []
    for g, (_, acc) in enumerate(states):
        den = (1 - g) * HEAD_DIM
        o = acc[:, g * HEAD_DIM:(g + 1) * HEAD_DIM] / acc[:, den:den + 1]
        pieces += [o[:tq], o[tq:]]
    return jnp.concatenate(pieces, axis=1)


def _group_rows(g):
    return slice(g * HEAD_DIM, (g + 1) * HEAD_DIM)


def _dense_attn_kernel(*refs, tq, seg_lens, has_sink):
    refs = list(refs)
    sink_ref = refs.pop(0) if has_sink else None
    q_ref = refs.pop(0)
    o_ref = refs.pop()
    segs = [(refs[2 * i], refs[2 * i + 1], n) for i, n in enumerate(seg_lens)]
    q = q_ref[...]
    q2 = [_stack_heads(q, g) for g in range(2)]
    states = tuple(_flash_init(2 * tq, g, (sink_ref[2 * g], sink_ref[2 * g + 1]) if has_sink else None)
                   for g in range(2))
    for kt_ref, v_ref, n_keys in segs:
        if n_keys <= KV_CHUNK:
            v = v_ref[...]
            states = tuple(_flash_update(states[g], q2[g], kt_ref[_group_rows(g), :], _aug_values(v, g))
                           for g in range(2))
        else:
            def body(c, sts, kt_ref=kt_ref, v_ref=v_ref):
                off = pl.multiple_of(c * KV_CHUNK, KV_CHUNK)
                v = v_ref[pl.ds(off, KV_CHUNK), :]
                return tuple(_flash_update(sts[g], q2[g], kt_ref[_group_rows(g), pl.ds(off, KV_CHUNK)],
                                           _aug_values(v, g)) for g in range(2))
            states = lax.fori_loop(0, n_keys // KV_CHUNK, body, states, unroll=2)
    o_ref[...] = _flash_finish(states, tq).astype(o_ref.dtype)


def _dense_attn(q, kt, v, sink, *, q_row0, q_len, tq, segs, batch):
    n = q.shape[0]
    qpb = q_len // tq
    q0 = q_row0 // tq
    in_specs, args = [], []
    if sink is not None:
        in_specs.append(pl.BlockSpec(memory_space=pltpu.SMEM))
        args.append(sink)
    in_specs.append(pl.BlockSpec((tq, 256), lambda b, i: (q0 + b * qpb + i, 0)))
    args.append(q)
    for row0, klen in segs:
        k0 = row0 // klen
        in_specs.append(pl.BlockSpec((LANES, klen), lambda b, i, k0=k0: (0, k0 + b)))
        in_specs.append(pl.BlockSpec((klen, LANES), lambda b, i, k0=k0: (k0 + b, 0)))
        args += [kt, v]
    return pl.pallas_call(
        functools.partial(_dense_attn_kernel, tq=tq, seg_lens=tuple(s[1] for s in segs), has_sink=sink is not None),
        grid=(batch, qpb),
        in_specs=in_specs,
        out_specs=pl.BlockSpec((tq, 256), lambda b, i: (b * qpb + i, 0)),
        out_shape=jax.ShapeDtypeStruct((batch * q_len, 256), BF16),
        compiler_params=_params(("parallel", "parallel")),
        name="dense_attn",
    )(*args)


def _window_attn_kernel(sink_ref, q_ref, ktc_ref, vc_ref, ktp_ref, vp_ref, ktm_ref, vm_ref, ktn_ref, vn_ref, o_ref,
                        *, nb):
    n = pl.program_id(1)
    tq = CHUNK
    iq = lax.broadcasted_iota(jnp.int32, (2 * tq, CHUNK), 0) & (tq - 1)
    jk = lax.broadcasted_iota(jnp.int32, (2 * tq, CHUNK), 1)
    mask_prev = jnp.logical_and(jk >= iq, n > 0)
    mask_next = jnp.logical_and(jk <= iq, n < nb - 1)
    q = q_ref[...]
    states = []
    for g in range(2):
        q2 = _stack_heads(q, g)
        rows = _group_rows(g)
        state = _flash_init(2 * tq, g, (sink_ref[2 * g], sink_ref[2 * g + 1]))
        state = _flash_update(state, q2, ktc_ref[rows, :], _aug_values(vc_ref[...], g))
        state = _flash_update(state, q2, ktm_ref[rows, :], _aug_values(vm_ref[...], g))
        state = _flash_update(state, q2, ktp_ref[rows, :], _aug_values(vp_ref[...], g), mask_prev)
        state = _flash_update(state, q2, ktn_ref[rows, :], _aug_values(vn_ref[...], g), mask_next)
        states.append(state)
    o_ref[...] = _flash_finish(states, tq).astype(o_ref.dtype)


def _window_attn(q, kt, v, sink, *, batch, seq, ctx_len):
    nb = seq // CHUNK
    ctx0 = (batch * seq) // ctx_len

    def kspec(fn):
        return pl.BlockSpec((LANES, CHUNK), lambda b, n: (0, b * nb + fn(n)))

    def vspec(fn):
        return pl.BlockSpec((CHUNK, LANES), lambda b, n: (b * nb + fn(n), 0))

    prev = lambda n: jnp.maximum(n - 1, 0)
    cur = lambda n: n
    nxt = lambda n: jnp.minimum(n + 1, nb - 1)
    return pl.pallas_call(
        functools.partial(_window_attn_kernel, nb=nb),
        grid=(batch, nb),
        in_specs=[pl.BlockSpec(memory_space=pltpu.SMEM),
                  pl.BlockSpec((CHUNK, 256), lambda b, n: (b * nb + n, 0)),
                  pl.BlockSpec((LANES, ctx_len), lambda b, n: (0, ctx0 + b)),
                  pl.BlockSpec((ctx_len, LANES), lambda b, n: (ctx0 + b, 0)),
                  kspec(prev), vspec(prev), kspec(cur), vspec(cur), kspec(nxt), vspec(nxt)],
        out_specs=pl.BlockSpec((CHUNK, 256), lambda b, n: (b * nb + n, 0)),
        out_shape=jax.ShapeDtypeStruct((batch * seq, 256), BF16),
        compiler_params=_params(("parallel", "parallel")),
        name="window_attn",
    )(sink, q, kt, v, kt, v, kt, v, kt, v)


def _gelu_tanh(x):
    return 0.5 * x * (1.0 + jnp.tanh(math.sqrt(2.0 / math.pi) * (x + 0.044715 * (x * x * x))))


def _outproj_kernel(x_ref, gate_ref, gpost_ref, oa_ref, od_ref, yf_ref, yb_ref, xs_ref, z_ref, dsk_ref, gn_ref,
                    hf_ref, hb_ref, lg_ref, w_ref, o_ref):
    y_ssd = (yf_ref[...] + yb_ref[...] + xs_ref[...] * dsk_ref[...]) * _silu(z_ref[...])
    ob = _rms(y_ssd, gn_ref[...])
    oc = (hf_ref[...] + hb_ref[...]) * _gelu_tanh(lg_ref[...])
    y = (_dot(oa_ref[...], w_ref[0:256, :]) + _dot(ob.astype(BF16), w_ref[256:512, :])
         + _dot(oc.astype(BF16), w_ref[512:768, :]) + _dot(od_ref[...], w_ref[768:1024, :]))
    o_ref[...] = x_ref[...] + gate_ref[0] * _rms(y, gpost_ref[...])


def _outproj(xu, mod, gpost, oa, od, yf, yb, xbc, z, dsk, gn, hf, hb, lg, w_out, *, n_rows, n_lat, seq, batch):
    d = xu.shape[1]
    row = lambda w: pl.BlockSpec((TM, w), lambda i: (i, 0))
    const = lambda a: pl.BlockSpec(a.shape, lambda i: (0,) * a.ndim)
    return pl.pallas_call(
        _outproj_kernel,
        grid=(n_rows // TM,),
        in_specs=[row(d), pl.BlockSpec((1, 1, d), _mod_spec(2, n_lat, seq, batch, TM)), const(gpost),
                  row(256), row(256), row(256), row(256), row(256), row(256), const(dsk), const(gn),
                  row(256), row(256), row(256), const(w_out)],
        out_specs=row(d),
        out_shape=jax.ShapeDtypeStruct((n_rows, d), F32),
        compiler_params=_params(("parallel",)),
        name="outproj",
    )(xu, mod, gpost, oa, od, yf, yb, xbc, z, dsk, gn, hf, hb, lg, w_out)


def _ceil_seg(c):
    return jnp.floor((c + (SEG_ALIGN - 1)) * (1.0 / SEG_ALIGN)) * SEG_ALIGN


def _router_kernel(x_ref, shift_ref, scale_ref, gpre_ref, rwt_ref, rb_ref, hb_ref, ld_ref, wk_ref, tab_ref):
    h = _rms(x_ref[...], gpre_ref[...])
    h = h * (1.0 + scale_ref[0]) + shift_ref[0]
    hb = h.astype(BF16)
    hb_ref[...] = hb

    scores = jax.nn.sigmoid(_dot_nt(rwt_ref[...], hb))
    biased = scores + rb_ref[...]
    gsz = N_EXPERTS // N_EXPERT_GROUPS
    sub = lax.broadcasted_iota(jnp.int32, (gsz, TM), 0)
    blocks, gscore = [], []
    for g in range(N_EXPERT_GROUPS):
        blk = biased[g * gsz:(g + 1) * gsz, :]
        m1 = jnp.max(blk, axis=0, keepdims=True)
        first = jnp.min(jnp.where(blk == m1, sub, gsz), axis=0, keepdims=True)
        m2 = jnp.max(jnp.where(sub == first, -jnp.inf, blk), axis=0, keepdims=True)
        blocks.append(blk)
        gscore.append(m1 + m2)
    masked = []
    for g in range(N_EXPERT_GROUPS):
        rank = jnp.zeros((1, TM), F32)
        for g2 in range(N_EXPERT_GROUPS):
            if g2 == g:
                continue
            beats = (gscore[g2] > gscore[g]) | ((gscore[g2] == gscore[g]) if g2 < g else False)
            rank = rank + jnp.where(beats, 1.0, 0.0)
        masked.append(jnp.where(rank < TOPK_GROUPS, blocks[g], -jnp.inf))
    vals = jnp.concatenate(masked, axis=0)
    eidx = lax.broadcasted_iota(jnp.int32, (N_EXPERTS, TM), 0)
    rank = jnp.zeros((N_EXPERTS, TM), F32)
    for e2 in range(N_EXPERTS):
        rowv = vals[e2:e2 + 1, :]
        beats = (rowv > vals) | ((rowv == vals) & (eidx > e2))
        rank = rank + jnp.where(beats, 1.0, 0.0)
    sel = rank < TOP_K
    self32 = jnp.where(sel, 1.0, 0.0)
    picked = jnp.where(sel, scores, 0.0)
    wdense = picked / jnp.sum(picked, axis=0, keepdims=True) * ROUTED_SCALE

    tr = lax.broadcasted_iota(jnp.int32, (TM, TM), 0)
    tc = lax.broadcasted_iota(jnp.int32, (TM, TM), 1)
    before = jnp.where(tr < tc, 1.0, 0.0).astype(BF16)
    selb = self32.astype(BF16)
    pos = _dot(selb, before)
    er = lax.broadcasted_iota(jnp.int32, (N_EXPERTS, N_EXPERTS), 0)
    ec = lax.broadcasted_iota(jnp.int32, (N_EXPERTS, N_EXPERTS), 1)
    lower = jnp.where(ec < er, 1.0, 0.0).astype(BF16)
    upper = jnp.where(er < ec, 1.0, 0.0).astype(BF16)
    ksel = _dot(lower, selb)
    cnt_col = _ceil_seg(jnp.sum(self32, axis=1, keepdims=True))
    loc_col = _dot3_left(lower, jnp.broadcast_to(cnt_col, (N_EXPERTS, LANES)))[:, 0:1]
    cnt_row = _ceil_seg(_dot_nt(jnp.ones((SUBLANES, TM), BF16), selb))
    loc_row = _dot3(cnt_row, upper)
    tab_ref[...] = jnp.concatenate([cnt_row, loc_row], axis=1).astype(jnp.int32)

    r8 = lax.broadcasted_iota(jnp.int32, (TOP_K, TM), 0)
    ld = jnp.zeros((TOP_K, TM), F32)
    wk = jnp.zeros((TOP_K, TM), F32)
    stage_row = pos + loc_col
    for k in range(TOP_K):
        one = sel & (ksel == float(k))
        ld = jnp.where(r8 == k, jnp.sum(jnp.where(one, stage_row, 0.0), axis=0, keepdims=True), ld)
        wk = jnp.where(r8 == k, jnp.sum(jnp.where(one, wdense, 0.0), axis=0, keepdims=True), wk)
    ld_ref[...] = ld.astype(jnp.int32)
    wk_ref[...] = wk


def _router(xu, mod, gpre, rwt, rb, *, n_rows, n_lat, seq, batch):
    d = xu.shape[1]
    row = lambda w: pl.BlockSpec((TM, w), lambda i: (i, 0))
    col = pl.BlockSpec((TOP_K, TM), lambda i: (0, i))
    const = lambda a: pl.BlockSpec(a.shape, lambda i: (0,) * a.ndim)
    return pl.pallas_call(
        _router_kernel,
        grid=(n_rows // TM,),
        in_specs=[row(d), pl.BlockSpec((1, 1, d), _mod_spec(3, n_lat, seq, batch, TM)),
                  pl.BlockSpec((1, 1, d), _mod_spec(4, n_lat, seq, batch, TM)),
                  const(gpre), const(rwt), const(rb)],
        out_specs=(row(d), col, col, pl.BlockSpec((SUBLANES, 2 * N_EXPERTS), lambda i: (i, 0))),
        out_shape=(jax.ShapeDtypeStruct((n_rows, d), BF16),
                   jax.ShapeDtypeStruct((TOP_K, n_rows), jnp.int32),
                   jax.ShapeDtypeStruct((TOP_K, n_rows), F32),
                   jax.ShapeDtypeStruct((n_rows // TM * SUBLANES, 2 * N_EXPERTS), jnp.int32)),
        compiler_params=_params(("parallel",)),
        name="router",
    )(xu, mod, mod, gpre, rwt, rb)


def _pow2_pieces(limit):
    bits, b = [], limit
    while b >= SEG_ALIGN:
        bits.append(b)
        b //= 2
    return bits


def _copy_pieces(n, src_ref, src0, dst_ref, dst0, sem, limit, wait, same_src=False):
    for bit in _pow2_pieces(limit):
        @pl.when((n & bit) != 0)
        def _():
            off = n & ~(2 * bit - 1)
            cp = pltpu.make_async_copy(src_ref.at[pl.ds(pl.multiple_of(src0 + (0 if same_src else off), SEG_ALIGN),
                                                          bit)],
                                       dst_ref.at[pl.ds(pl.multiple_of(dst0 + off, SEG_ALIGN), bit)], sem)
            cp.wait() if wait else cp.start()


def _piece_copies(tile, np_ref, dst_ref, stage_ref, slots_ref, sem, to_slots, wait):
    def body(p, c):
        src = stage_ref.at[pl.ds(pl.multiple_of(p * SEG_ALIGN, SEG_ALIGN), SEG_ALIGN)]
        dst = slots_ref.at[pl.ds(pl.multiple_of(dst_ref[tile * STAGE_PIECES + p], SEG_ALIGN), SEG_ALIGN)]
        cp = pltpu.make_async_copy(src, dst, sem) if to_slots else pltpu.make_async_copy(dst, src, sem)
        cp.wait() if wait else cp.start()
        return c
    lax.fori_loop(0, np_ref[tile], body, 0)


def _used_blocks(tile, np_ref):
    return (np_ref[tile] * SEG_ALIGN + TM - 1) // TM


def _stage_rows_iota():
    return lax.broadcasted_iota(jnp.int32, (TM // 2, TM), 0).astype(F32).astype(BF16)


def _pick_matrix(ld, base, vals, jrow):
    rel = (ld - base).astype(F32)
    rel = jnp.where(jnp.logical_and(rel >= 0.0, rel < TM // 2), rel, -1.0).astype(BF16)
    out = jnp.zeros((TM // 2, TM), BF16)
    for k in range(TOP_K):
        out = jnp.where(rel[k:k + 1, :] == jrow, vals[k:k + 1, :], out)
    return out


def _dispatch_kernel(np_ref, dst_ref, pstart_ref, npad_ref, hb_ref, ld_ref, xs_ref, stage, zbuf, sem, zsem):
    i = pl.program_id(0)

    @pl.when(i == 0)
    def _():
        zbuf[...] = jnp.zeros_like(zbuf)
        for wait in (False, True):
            def body(e, c, wait=wait):
                _copy_pieces(npad_ref[e], zbuf, 0, xs_ref, pstart_ref[e], zsem, BM_EXPERT // 2, wait, same_src=True)
                return c
            lax.fori_loop(0, N_EXPERTS, body, 0)

    ld = ld_ref[...]
    hb = hb_ref[...]
    jrow = _stage_rows_iota()
    ones = jnp.ones((TOP_K, TM), BF16)

    def block(b, c):
        for half in range(2):
            base = pl.multiple_of(b * TM + half * (TM // 2), TM // 2)
            stage[pl.ds(base, TM // 2), :] = _dot(_pick_matrix(ld, base, ones, jrow), hb).astype(BF16)
        return c

    lax.fori_loop(0, _used_blocks(i, np_ref), block, 0)
    _piece_copies(i, np_ref, dst_ref, stage, xs_ref, sem, True, False)
    _piece_copies(i, np_ref, dst_ref, stage, xs_ref, sem, True, True)


def _dispatch(tabs, pad_start, n_pad, hb, ld, n_slots):
    n, d = hb.shape
    grid_spec = pltpu.PrefetchScalarGridSpec(
        num_scalar_prefetch=4,
        grid=(n // TM,),
        in_specs=[pl.BlockSpec((TM, d), lambda i, *_: (i, 0)),
                  pl.BlockSpec((TOP_K, TM), lambda i, *_: (0, i))],
        out_specs=pl.BlockSpec(memory_space=pl.ANY),
        scratch_shapes=[pltpu.VMEM((STAGE_ROWS, d), BF16), pltpu.VMEM((BM_EXPERT // 2, d), BF16),
                        pltpu.SemaphoreType.DMA(()), pltpu.SemaphoreType.DMA(())],
    )
    return pl.pallas_call(
        _dispatch_kernel,
        grid_spec=grid_spec,
        out_shape=jax.ShapeDtypeStruct((n_slots, d), BF16),
        compiler_params=_params(("arbitrary",)),
        name="moe_dispatch",
    )(*tabs, pad_start, n_pad, hb, ld)


def _expert_kernel(be_ref, na_ref, xs_ref, wg_ref, wu_ref, wd_ref, ys_ref, wgb, wub, wdb):
    i = pl.program_id(0)

    @pl.when(i < na_ref[0])
    def _():
        @pl.when(jnp.logical_or(i == 0, be_ref[i] != be_ref[jnp.maximum(i - 1, 0)]))
        def _():
            wgb[...] = wg_ref[0, 0].astype(BF16)
            wub[...] = wu_ref[0, 0].astype(BF16)
            wdb[...] = wd_ref[0, 0].astype(BF16)

        half = BM_EXPERT // 2
        for r in range(2):
            xb = xs_ref[r * half:(r + 1) * half, :]
            hid = _silu(_dot(xb, wgb[...])) * _dot(xb, wub[...])
            ys_ref[r * half:(r + 1) * half, :] = _dot(hid.astype(BF16), wdb[...]).astype(ys_ref.dtype)


def _experts(block_e, n_active, xs, wg, wu, wd, layer):
    n_slots, d = xs.shape
    nb = n_slots // BM_EXPERT
    blk = lambda i, be, na: jnp.minimum(i, na[0] - 1)
    tiles = pl.BlockSpec((BM_EXPERT, d), lambda i, be, na: (blk(i, be, na), 0))
    grid_spec = pltpu.PrefetchScalarGridSpec(
        num_scalar_prefetch=2,
        grid=(nb,),
        in_specs=[tiles,
                  pl.BlockSpec((1, 1, d, EXPERT_HIDDEN), lambda i, be, na: (layer, be[blk(i, be, na)], 0, 0)),
                  pl.BlockSpec((1, 1, d, EXPERT_HIDDEN), lambda i, be, na: (layer, be[blk(i, be, na)], 0, 0)),
                  pl.BlockSpec((1, 1, EXPERT_HIDDEN, d), lambda i, be, na: (layer, be[blk(i, be, na)], 0, 0))],
        out_specs=tiles,
        scratch_shapes=[pltpu.VMEM((d, EXPERT_HIDDEN), BF16), pltpu.VMEM((d, EXPERT_HIDDEN), BF16),
                        pltpu.VMEM((EXPERT_HIDDEN, d), BF16)],
    )
    return pl.pallas_call(
        _expert_kernel,
        grid_spec=grid_spec,
        out_shape=jax.ShapeDtypeStruct((n_slots, d), BF16),
        compiler_params=_params(("arbitrary",)),
        name="moe_experts",
    )(block_e, n_active, xs, wg, wu, wd)


def _combine_kernel(np_ref, dst_ref, ys_ref, ld_ref, wk_ref, hb_ref, x_ref, gate_ref, gpost_ref,
                    sg_ref, su_ref, sd_ref, o_ref, stage, acc_ref, sem):
    i = pl.program_id(0)

    @pl.when(i == 0)
    def _():
        stage[...] = jnp.zeros_like(stage)

    _piece_copies(i, np_ref, dst_ref, stage, ys_ref, sem, False, False)
    hb = hb_ref[...]
    acc_ref[...] = _dot((_silu(_dot(hb, sg_ref[...])) * _dot(hb, su_ref[...])).astype(BF16), sd_ref[...])
    _piece_copies(i, np_ref, dst_ref, stage, ys_ref, sem, False, True)

    ld = ld_ref[...]
    wkb = wk_ref[...].astype(BF16)
    jrow = _stage_rows_iota()

    def block(b, c):
        for half in range(2):
            base = pl.multiple_of(b * TM + half * (TM // 2), TM // 2)
            weights = _pick_matrix(ld, base, wkb, jrow)
            acc_ref[...] += lax.dot_general(weights, stage[pl.ds(base, TM // 2), :], (((0,), (0,)), ((), ())),
                                            preferred_element_type=F32)
        return c

    lax.fori_loop(0, _used_blocks(i, np_ref), block, 0)
    o_ref[...] = x_ref[...] + gate_ref[0] * _rms(acc_ref[...], gpost_ref[...])


def _combine(tabs, ys, ld, wk, hb, xu, mod, gpost, sg, su, sd, *, n_rows, n_lat, seq, batch):
    d = xu.shape[1]
    row = lambda w: pl.BlockSpec((TM, w), lambda i, *_: (i, 0))
    col = pl.BlockSpec((TOP_K, TM), lambda i, *_: (0, i))
    const = lambda a: pl.BlockSpec(a.shape, lambda i, *_: (0,) * a.ndim)
    mod_map = _mod_spec(5, n_lat, seq, batch, TM)
    grid_spec = pltpu.PrefetchScalarGridSpec(
        num_scalar_prefetch=2,
        grid=(n_rows // TM,),
        in_specs=[pl.BlockSpec(memory_space=pl.ANY), col, col, row(d), row(d),
                  pl.BlockSpec((1, 1, d), lambda i, *_: mod_map(i)),
                  const(gpost), const(sg), const(su), const(sd)],
        out_specs=row(d),
        scratch_shapes=[pltpu.VMEM((STAGE_ROWS, d), BF16), pltpu.VMEM((TM, d), F32), pltpu.SemaphoreType.DMA(())],
    )
    return pl.pallas_call(
        _combine_kernel,
        grid_spec=grid_spec,
        out_shape=jax.ShapeDtypeStruct((n_rows, d), F32),
        compiler_params=_params(("arbitrary",)),
        name="moe_combine",
    )(*tabs, ys, ld, wk, hb, xu, mod, gpost, sg, su, sd)


def _deinterleave(w):
    cols = w.shape[-1]
    perm = jnp.concatenate([jnp.arange(0, HEAD_DIM, 2), jnp.arange(1, HEAD_DIM, 2)])
    idx = (jnp.arange(cols // HEAD_DIM)[:, None] * HEAD_DIM + perm[None, :]).reshape(-1)
    return w[..., idx]


def _pad_in_proj(w_in):
    d = w_in.shape[0]
    o = 0
    parts = {}
    for name, width in (("qa", 256), ("ka", 128), ("va", 128), ("z", 256), ("xs", 256), ("bm", 128), ("cm", 128),
                        ("dtf", 4), ("dtb", 4), ("lx", 256), ("lg", 256), ("qd", 256), ("kd", 128), ("vd", 128)):
        parts[name] = w_in[:, o:o + width]
        o += width
    dt = jnp.concatenate([parts["dtf"], parts["dtb"], jnp.zeros((d, LANES - 8), w_in.dtype)], axis=1)
    cols = [_deinterleave(parts["qa"]), _deinterleave(parts["ka"]), parts["va"],
            _deinterleave(parts["qd"]), _deinterleave(parts["kd"]), parts["vd"],
            parts["z"], parts["xs"], parts["bm"], parts["cm"], dt, parts["lx"], parts["lg"]]
    return jnp.concatenate(cols, axis=1).astype(BF16)


def _rope_tables(seq):
    t = jnp.arange(seq)
    rowp = (t // GRID_W).astype(F32)
    colp = (t % GRID_W).astype(F32)
    axis_dim = HEAD_DIM // 2
    inv_freq = ROPE_THETA ** (-jnp.arange(0, axis_dim, 2, dtype=F32) / axis_dim)
    ang = jnp.concatenate([rowp[:, None] * inv_freq, colp[:, None] * inv_freq], axis=-1)
    cos, sin = jnp.cos(ang), jnp.sin(ang)
    cos_h = jnp.concatenate([cos, cos], axis=-1)
    sin_h = jnp.concatenate([-sin, sin], axis=-1)
    return jnp.tile(cos_h, (1, 4)), jnp.tile(sin_h, (1, 4))


def _block_diag(w):
    nb, bd, _ = w.shape
    eye = jnp.eye(nb, dtype=w.dtype)
    return (eye[:, None, :, None] * w[:, :, None, :]).reshape(nb * bd, nb * bd)


def _lane_row(fwd, bwd):
    return jnp.concatenate([fwd, bwd, jnp.zeros((LANES - 8,), F32)]).reshape(1, LANES)


def kernel(x, c, ctx, c_ctx, w_ada, b_ada, g_mix_pre, g_mix_post, g_ffn_pre, g_ffn_post, w_in, w_out, a_sink,
           ssd_conv_w, ssd_conv_b, ssd_dt_bias, ssd_a_log, ssd_d, ssd_norm, lru_conv_w, lru_conv_b, lru_w_a,
           lru_b_a, lru_w_i, lru_b_i, lru_lambda, d_q_norm, d_k_norm, router_w, router_bias, exp_w_gate,
           exp_w_up, exp_w_down, sh_w_gate, sh_w_up, sh_w_down):
    batch, seq, d = x.shape
    ctx_len = ctx.shape[1]
    depth = w_ada.shape[0]
    n_lat = batch * seq
    n_ctx = batch * ctx_len
    n_all = n_lat + n_ctx
    assert seq % TM == 0 and n_ctx % TM == 0 and seq % T_CONV == 0 and ctx_len % T_CONV == 0
    assert d == SUBLANES * LANES
    assert ctx_len <= KV_CHUNK and seq % KV_CHUNK == 0 and seq % TQ_GLOBAL == 0 and batch + 1 <= SUBLANES

    xu = jnp.concatenate([x.reshape(n_lat, d), ctx.reshape(n_ctx, d)], axis=0)
    cin = jnp.concatenate([c, c_ctx[None, :], jnp.zeros((SUBLANES - batch - 1, d), F32)], axis=0)
    mod_all = _adaln(cin, w_ada, b_ada)
    cos_t, sin_t = _rope_tables(seq)
    hm = jnp.kron(jnp.eye(4, dtype=F32), jnp.full((HEAD_DIM, HEAD_DIM), 1.0 / HEAD_DIM, F32)).astype(BF16)

    for l in range(depth):
        with_ctx = l < depth - 1
        mod = mod_all[l].reshape(SUBLANES * 6, 1, d)
        gq = jnp.tile(_deinterleave(d_q_norm[l]), 4).reshape(1, 256)
        gk = jnp.tile(_deinterleave(d_k_norm[l]), 2).reshape(1, LANES)
        qa, kat, va, qd, kdt, vd, z, xbc_raw, dt, lx_raw, lg = _inproj(
            xu, mod, g_mix_pre[l].reshape(1, d), _pad_in_proj(w_in[l]), cos_t, sin_t, gq, gk, hm,
            n_lat=n_lat, seq=seq, batch=batch)

        xbc, lu = _conv(xbc_raw, lx_raw, ssd_conv_w[l], ssd_conv_b[l].reshape(1, -1),
                        lru_conv_w[l], lru_conv_b[l].reshape(1, -1), n_lat=n_lat, seq=seq, ctx_len=ctx_len)
        yf, yb = _ssd(xbc, dt, _lane_row(ssd_dt_bias[l, 0], ssd_dt_bias[l, 1]),
                      _lane_row(ssd_a_log[l, 0], ssd_a_log[l, 1]), batch=batch, seq=seq, ctx_len=ctx_len)
        wg = jnp.stack([jnp.concatenate([_block_diag(lru_w_a[l, dd]), _block_diag(lru_w_i[l, dd])], axis=1)
                        for dd in range(2)]).astype(BF16)
        bg = jnp.concatenate([lru_b_a[l], lru_b_i[l]], axis=1).reshape(2, 1, 2 * LRU_WIDTH)
        hf, hb = _lru(lu, wg, bg, lru_lambda[l].reshape(2, 1, LRU_WIDTH), batch=batch, seq=seq, ctx_len=ctx_len)

        oa = _window_attn(qa, kat, va, a_sink[l], batch=batch, seq=seq, ctx_len=ctx_len)
        od = _dense_attn(qd, kdt, vd, None, q_row0=0, q_len=seq, tq=TQ_GLOBAL,
                         segs=[(n_lat, ctx_len), (0, seq)], batch=batch)
        if with_ctx:
            oa_c = _dense_attn(qa, kat, va, a_sink[l], q_row0=n_lat, q_len=ctx_len, tq=ctx_len,
                               segs=[(n_lat, ctx_len)], batch=batch)
            od_c = _dense_attn(qd, kdt, vd, None, q_row0=n_lat, q_len=ctx_len, tq=ctx_len,
                               segs=[(n_lat, ctx_len)], batch=batch)
            oa = jnp.concatenate([oa, oa_c], axis=0)
            od = jnp.concatenate([od, od_c], axis=0)
        n_rows = n_all if with_ctx else n_lat

        dsk = jnp.repeat(ssd_d[l], HEAD_DIM).reshape(1, 256)
        xu_mid = _outproj(xu, mod, g_mix_post[l].reshape(1, d), oa, od, yf, yb, xbc, z, dsk,
                          ssd_norm[l].reshape(1, 256), hf, hb, lg, w_out[l].astype(BF16),
                          n_rows=n_rows, n_lat=n_lat, seq=seq, batch=batch)

        hb_ffn, ld, wk, tab = _router(xu_mid, mod, g_ffn_pre[l].reshape(1, d), router_w[l].T.astype(BF16),
                                      router_bias[l].reshape(N_EXPERTS, 1), n_rows=n_rows, n_lat=n_lat,
                                      seq=seq, batch=batch)
        n_tiles = n_rows // TM
        tab = tab.reshape(n_tiles, SUBLANES, 2 * N_EXPERTS)[:, 0, :]
        seg_cnt, seg_loc = tab[:, :N_EXPERTS], tab[:, N_EXPERTS:]
        counts = jnp.sum(seg_cnt, axis=0)
        padded = (counts + BM_EXPERT - 1) // BM_EXPERT * BM_EXPERT
        padded_end = jnp.cumsum(padded)
        offs = padded_end - padded
        seg_off = offs[None, :] + jnp.cumsum(seg_cnt, axis=0) - seg_cnt
        n_blocks = (n_rows * TOP_K + n_tiles * N_EXPERTS * SEG_ALIGN) // BM_EXPERT + N_EXPERTS
        n_active = (padded_end[-1] // BM_EXPERT).astype(jnp.int32).reshape(1)
        block_start = jnp.arange(n_blocks, dtype=jnp.int32) * BM_EXPERT
        block_e = jnp.minimum(jnp.sum((padded_end[None, :] <= block_start[:, None]).astype(jnp.int32), axis=1),
                              N_EXPERTS - 1)
        piece_row = jnp.arange(STAGE_PIECES, dtype=jnp.int32) * SEG_ALIGN
        seg_end = seg_loc + seg_cnt
        piece_e = jnp.sum((seg_end[:, None, :] <= piece_row[None, :, None]).astype(jnp.int32), axis=-1)
        shift = jnp.sum(jnp.where(piece_e[:, :, None] == jnp.arange(N_EXPERTS, dtype=jnp.int32),
                                  (seg_off - seg_loc)[:, None, :], 0), axis=-1)
        tabs = (seg_end[:, -1] // SEG_ALIGN, (piece_row[None, :] + shift).reshape(-1))
        xs = _dispatch(tabs, offs + counts, padded - counts, hb_ffn, ld, n_blocks * BM_EXPERT)
        ys = _experts(block_e, n_active, xs, exp_w_gate, exp_w_up, exp_w_down, l)
        xu = _combine(tabs, ys, ld, wk, hb_ffn, xu_mid, mod, g_ffn_post[l].reshape(1, d), sh_w_gate[l].astype(BF16),
                      sh_w_up[l].astype(BF16), sh_w_down[l].astype(BF16),
                      n_rows=n_rows, n_lat=n_lat, seq=seq, batch=batch)
    return xu[:n_lat].reshape(batch, seq, d)
```

```python
import functools
import math

import jax
import jax.numpy as jnp
from jax import lax
from jax.experimental import pallas as pl
from jax.experimental.pallas import tpu as pltpu

F32 = jnp.float32
BF16 = jnp.bfloat16

HEAD_DIM = 64
GRID_W = 64
ROPE_THETA = 10000.0
NORM_EPS = 1e-6
NEG_INF = -1e30
A_HEADS, A_KV_HEADS, WINDOW = 4, 2, 128
SSD_HEADS, SSD_GROUPS, SSD_STATE, SSD_CONV = 4, 2, 64, 4
LRU_WIDTH, LRU_BLOCKS, LRU_CONV, LRU_C = 256, 4, 4, 8.0
D_HEADS, D_KV_HEADS = 4, 2
N_EXPERTS, N_EXPERT_GROUPS, TOPK_GROUPS, TOP_K = 64, 8, 4, 8
EXPERT_HIDDEN, SHARED_HIDDEN = 256, 256
ROUTED_SCALE = 2.5

LANES = 128
SUBLANES = 8

TM = 512
T_CONV = 256
CHUNK = 128
SCAN_CHUNKS = 2
TQ_GLOBAL = 256
TQ_WINDOW = 512
KV_CHUNK = 512
BM_EXPERT = 512
SEG_ALIGN = 16
STAGE_ROWS = TM * TOP_K + N_EXPERTS * SEG_ALIGN
STAGE_PIECES = STAGE_ROWS // SEG_ALIGN
VMEM_LIMIT = 48 * 1024 * 1024

C_QA, C_KA, C_VA = 0, 256, 384
C_QD, C_KD, C_VD = 512, 768, 896
C_Z, C_XBC, C_DT = 1024, 1280, 1792
C_LX, C_LG = 1920, 2176
NP_IN = 2432


def _dot(a, b):
    return jnp.dot(a, b, preferred_element_type=F32)


def _dot_nt(a, b):
    return lax.dot_general(a, b, (((1,), (1,)), ((), ())), preferred_element_type=F32)


def _dot3(a, b):
    a1 = a.astype(BF16)
    r1 = a - a1.astype(F32)
    a2 = r1.astype(BF16)
    a3 = (r1 - a2.astype(F32)).astype(BF16)
    return _dot(a1, b) + _dot(a2, b) + _dot(a3, b)


def _dot3_left(a, b):
    b1 = b.astype(BF16)
    r1 = b - b1.astype(F32)
    b2 = r1.astype(BF16)
    b3 = (r1 - b2.astype(F32)).astype(BF16)
    return _dot(a, b1) + _dot(a, b2) + _dot(a, b3)


def _silu(x):
    return x * jax.nn.sigmoid(x)


def _softplus(x):
    return jnp.maximum(x, 0.0) + jnp.log1p(jnp.exp(-jnp.abs(x)))


def _rms(x, gain):
    return x * lax.rsqrt(jnp.mean(x * x, axis=-1, keepdims=True) + NORM_EPS) * gain


def _params(sem=None):
    return pltpu.CompilerParams(dimension_semantics=sem, vmem_limit_bytes=VMEM_LIMIT)


def _adaln_kernel(c_ref, w_ref, b_ref, o_ref):
    s = _silu(c_ref[...])
    o_ref[0] = _dot(s.astype(BF16), w_ref[0].astype(BF16)) + b_ref[0]


def _adaln(cin, w_ada, b_ada):
    depth, d, n6 = w_ada.shape
    tn = 1024
    return pl.pallas_call(
        _adaln_kernel,
        grid=(depth, n6 // tn),
        in_specs=[pl.BlockSpec((SUBLANES, d), lambda l, j: (0, 0)),
                  pl.BlockSpec((1, d, tn), lambda l, j: (l, 0, j)),
                  pl.BlockSpec((1, 1, tn), lambda l, j: (l, 0, j))],
        out_specs=pl.BlockSpec((1, SUBLANES, tn), lambda l, j: (l, 0, j)),
        out_shape=jax.ShapeDtypeStruct((depth, SUBLANES, n6), F32),
        compiler_params=_params(("parallel", "parallel")),
        name="adaln",
    )(cin, w_ada, b_ada.reshape(depth, 1, n6))


def _swap_halves(t):
    w = t.shape[1]
    lane = lax.broadcasted_iota(jnp.int32, (1, w), 1)
    first = (lane & 32) == 0
    return jnp.where(first, pltpu.roll(t, w - 32, axis=1), pltpu.roll(t, 32, axis=1))


def _inproj_kernel(x_ref, shift_ref, scale_ref, gpre_ref, w_ref, cos_ref, sin_ref, gq_ref, gk_ref, hm_ref,
                   qa_ref, kat_ref, va_ref, qd_ref, kdt_ref, vd_ref, z_ref, xbc_ref, dt_ref, lx_ref, lg_ref,
                   *, n_lat):
    i = pl.program_id(0)
    is_lat = i * TM < n_lat
    h = _rms(x_ref[...], gpre_ref[...])
    h = h * (1.0 + scale_ref[0]) + shift_ref[0]
    hb = h.astype(BF16)

    def sec(a, b):
        return _dot(hb, w_ref[:, a:b])

    cos = jnp.where(is_lat, cos_ref[...], 1.0)
    sin = jnp.where(is_lat, sin_ref[...], 0.0)

    def rope(t):
        w = t.shape[1]
        return t * cos[:, :w] + _swap_halves(t) * sin[:, :w]

    def head_norm(t, gain):
        w = t.shape[1]
        ms = _dot3(t * t, hm_ref[:w, :w])
        return t * lax.rsqrt(ms + NORM_EPS) * gain

    scale = HEAD_DIM ** -0.5
    qa_ref[...] = (rope(sec(C_QA, C_KA)) * scale).astype(BF16)
    kat_ref[...] = rope(sec(C_KA, C_VA)).T.astype(BF16)
    va_ref[...] = sec(C_VA, C_QD).astype(BF16)
    qd_ref[...] = (rope(head_norm(sec(C_QD, C_KD), gq_ref[...])) * scale).astype(BF16)
    kdt_ref[...] = rope(head_norm(sec(C_KD, C_VD), gk_ref[...])).T.astype(BF16)
    vd_ref[...] = sec(C_VD, C_Z).astype(BF16)
    z_ref[...] = sec(C_Z, C_XBC)
    xbc_ref[...] = sec(C_XBC, C_DT)
    dt_ref[...] = sec(C_DT, C_LX)
    lx_ref[...] = sec(C_LX, C_LG)
    lg_ref[...] = sec(C_LG, NP_IN)


def _mod_spec(chunk, n_lat, seq, batch, tile):
    def imap(i):
        row0 = i * tile
        seg = jnp.where(row0 < n_lat, row0 // seq, batch)
        return (seg * 6 + chunk, 0, 0)
    return imap


def _inproj(xu, mod, gpre, w_pad, cos_t, sin_t, gq, gk, hm, *, n_lat, seq, batch):
    n, d = xu.shape
    nt = n // TM
    spt = seq // TM
    row = lambda w: pl.BlockSpec((TM, w), lambda i: (i, 0))
    colT = pl.BlockSpec((LANES, TM), lambda i: (0, i))
    const = lambda a: pl.BlockSpec(a.shape, lambda i: (0,) * a.ndim)
    out_shapes = (
        jax.ShapeDtypeStruct((n, 256), BF16), jax.ShapeDtypeStruct((LANES, n), BF16),
        jax.ShapeDtypeStruct((n, LANES), BF16),
        jax.ShapeDtypeStruct((n, 256), BF16), jax.ShapeDtypeStruct((LANES, n), BF16),
        jax.ShapeDtypeStruct((n, LANES), BF16),
        jax.ShapeDtypeStruct((n, 256), F32), jax.ShapeDtypeStruct((n, 512), F32),
        jax.ShapeDtypeStruct((n, LANES), F32), jax.ShapeDtypeStruct((n, 256), F32),
        jax.ShapeDtypeStruct((n, 256), F32))
    return pl.pallas_call(
        functools.partial(_inproj_kernel, n_lat=n_lat),
        grid=(nt,),
        in_specs=[row(d),
                  pl.BlockSpec((1, 1, d), _mod_spec(0, n_lat, seq, batch, TM)),
                  pl.BlockSpec((1, 1, d), _mod_spec(1, n_lat, seq, batch, TM)),
                  const(gpre), const(w_pad),
                  pl.BlockSpec((TM, 256), lambda i: (i % spt, 0)),
                  pl.BlockSpec((TM, 256), lambda i: (i % spt, 0)),
                  const(gq), const(gk), const(hm)],
        out_specs=(row(256), colT, row(LANES), row(256), colT, row(LANES),
                   row(256), row(512), row(LANES), row(256), row(256)),
        out_shape=out_shapes,
        compiler_params=_params(("parallel",)),
        name="inproj",
    )(xu, mod, mod, gpre, w_pad, cos_t, sin_t, gq, gk, hm)


def _conv_kernel(xs_ref, xsp_ref, xsn_ref, xl_ref, xlp_ref, xln_ref, ws_ref, bs_ref, wl_ref, bl_ref,
                 os_ref, ol_ref, *, n_lat, seq, ctx_len):
    i = pl.program_id(0)
    row0 = i * T_CONV
    pos = jnp.where(row0 < n_lat, row0 % seq, (row0 - n_lat) % ctx_len)
    slen = jnp.where(row0 < n_lat, seq, ctx_len)
    first = pos == 0
    last = pos + T_CONV == slen
    row = lax.broadcasted_iota(jnp.int32, (T_CONV, 1), 0)

    def conv(x, prev, nxt, w, b):
        pm = jnp.where(first, 0.0, prev)
        nx = jnp.where(last, 0.0, nxt)
        xm1 = jnp.where(row == 0, pm[7:8, :], pltpu.roll(x, 1, axis=0))
        xm2 = jnp.where(row == 0, pm[6:7, :], jnp.where(row == 1, pm[7:8, :], pltpu.roll(x, 2, axis=0)))
        xp1 = jnp.where(row == T_CONV - 1, nx[0:1, :], pltpu.roll(x, T_CONV - 1, axis=0))
        return w[0:1, :] * xm2 + w[1:2, :] * xm1 + w[2:3, :] * x + w[3:4, :] * xp1 + b

    os_ref[...] = _silu(conv(xs_ref[...], xsp_ref[...], xsn_ref[...], ws_ref[...], bs_ref[...]))
    ol_ref[...] = conv(xl_ref[...], xlp_ref[...], xln_ref[...], wl_ref[...], bl_ref[...])


def _conv(xbc_raw, lx_raw, ws, bs, wl, bl, *, n_lat, seq, ctx_len):
    n = xbc_raw.shape[0]
    nt = n // T_CONV
    r8 = T_CONV // SUBLANES
    n8 = n // SUBLANES
    main = lambda w: pl.BlockSpec((T_CONV, w), lambda i: (i, 0))
    prev = lambda w: pl.BlockSpec((SUBLANES, w), lambda i: (jnp.maximum(i * r8 - 1, 0), 0))
    nxt = lambda w: pl.BlockSpec((SUBLANES, w), lambda i: (jnp.minimum((i + 1) * r8, n8 - 1), 0))
    const = lambda a: pl.BlockSpec(a.shape, lambda i: (0,) * a.ndim)
    return pl.pallas_call(
        functools.partial(_conv_kernel, n_lat=n_lat, seq=seq, ctx_len=ctx_len),
        grid=(nt,),
        in_specs=[main(512), prev(512), nxt(512), main(256), prev(256), nxt(256),
                  const(ws), const(bs), const(wl), const(bl)],
        out_specs=(main(512), main(256)),
        out_shape=(jax.ShapeDtypeStruct((n, 512), F32), jax.ShapeDtypeStruct((n, 256), F32)),
        compiler_params=_params(("parallel",)),
        name="conv",
    )(xbc_raw, xbc_raw, xbc_raw, lx_raw, lx_raw, lx_raw, ws, bs, wl, bl)


def _chunk_maps(batch, seq, ctx_len):
    ncx = ctx_len // (SCAN_CHUNKS * CHUNK)
    nl = seq // (SCAN_CHUNKS * CHUNK)
    lat_blocks = batch * nl

    def block(b, c):
        return jnp.where(c < ncx, lat_blocks + b * ncx + c, b * nl + (c - ncx))

    def fwd(b, k):
        return (block(b, k), 0)

    def bwd(b, k):
        c = jnp.where(k < ncx, ncx - 1 - k, ncx + (nl - 1 - (k - ncx)))
        return (block(b, c), 0)

    return fwd, bwd, ncx + nl


def _ssd_kernel(xf_ref, dtf_ref, xb_ref, dtb_ref, dtbias_ref, alog_ref, yf_ref, yb_ref, state_ref):
    k = pl.program_id(1)

    @pl.when(k == 0)
    def _():
        state_ref[...] = jnp.zeros_like(state_ref)

    ri = lax.broadcasted_iota(jnp.int32, (CHUNK, CHUNK), 0)
    ci = lax.broadcasted_iota(jnp.int32, (CHUNK, CHUNK), 1)
    lane_lo = ci < HEAD_DIM
    aneg = -jnp.exp(alog_ref[...])
    dtbias = dtbias_ref[...]

    order = [(d, s if d == 0 else SCAN_CHUNKS - 1 - s) for s in range(SCAN_CHUNKS) for d in range(2)]
    for d, sub in order:
        x_ref, dt_ref, y_ref = ((xf_ref, dtf_ref, yf_ref), (xb_ref, dtb_ref, yb_ref))[d]
        rws = slice(sub * CHUNK, (sub + 1) * CHUNK)
        causal = (ri >= ci) if d == 0 else (ci >= ri)
        tmat = jnp.where(causal, 1.0, 0.0).astype(BF16)
        xs = x_ref[rws, 0:256]
        bm = x_ref[rws, 256:384]
        cm = x_ref[rws, 384:512]
        dtp = _softplus(dt_ref[rws, :] + dtbias)
        acum = _dot3_left(tmat, dtp * aneg)
        acum_t = acum.T
        bt = bm.T.astype(BF16)
        cmb = cm.astype(BF16)
        bmb = bm.astype(BF16)
        tot_row = CHUNK - 1 if d == 0 else 0
        for p in range(2):
            cmask = jnp.where(lane_lo if p == 0 else jnp.logical_not(lane_lo), cmb, jnp.zeros_like(cmb))
            cb = _dot_nt(cmask, bmb)
            cols, dts, ys = [], [], []
            x_pair = xs[:, p * LANES:(p + 1) * LANES]
            for j in range(2):
                col = 4 * d + 2 * p + j
                colb = jnp.broadcast_to(acum[:, col:col + 1], (CHUNK, CHUNK))
                rowb = jnp.broadcast_to(acum_t[col:col + 1, :], (CHUNK, CHUNK))
                cols.append(colb)
                dts.append(jnp.broadcast_to(dtp[:, col:col + 1], (CHUNK, CHUNK)))
            col_pair = jnp.where(lane_lo, cols[0], cols[1])
            dt_pair = jnp.where(lane_lo, dts[0], dts[1])
            xdt = x_pair * dt_pair
            xdt_b = xdt.astype(BF16)
            for j in range(2):
                col = 4 * d + 2 * p + j
                rowb = jnp.broadcast_to(acum_t[col:col + 1, :], (CHUNK, CHUNK))
                decay = jnp.exp(jnp.where(causal, cols[j] - rowb, NEG_INF))
                ys.append(_dot((cb * decay).astype(BF16), xdt_b))
            y_intra = jnp.where(lane_lo, ys[0], ys[1])
            s_old = state_ref[d, p]
            y_inter = _dot(cmask, s_old.astype(BF16)) * jnp.exp(col_pair)
            y_ref[rws, p * LANES:(p + 1) * LANES] = y_intra + y_inter
            tot_pair = col_pair[tot_row:tot_row + 1, :]
            to_end = jnp.exp(tot_pair - col_pair)
            state_ref[d, p] = s_old * jnp.exp(tot_pair) + _dot(bt, (xdt * to_end).astype(BF16))


def _ssd(xbc, dt, dtbias_row, alog_row, *, batch, seq, ctx_len):
    n = xbc.shape[0]
    fwd, bwd, steps = _chunk_maps(batch, seq, ctx_len)
    rows = SCAN_CHUNKS * CHUNK
    const = lambda a: pl.BlockSpec(a.shape, lambda b, k: (0,) * a.ndim)
    return pl.pallas_call(
        _ssd_kernel,
        grid=(batch, steps),
        in_specs=[pl.BlockSpec((rows, 512), fwd), pl.BlockSpec((rows, LANES), fwd),
                  pl.BlockSpec((rows, 512), bwd), pl.BlockSpec((rows, LANES), bwd),
                  const(dtbias_row), const(alog_row)],
        out_specs=(pl.BlockSpec((rows, 256), fwd), pl.BlockSpec((rows, 256), bwd)),
        out_shape=(jax.ShapeDtypeStruct((n, 256), F32), jax.ShapeDtypeStruct((n, 256), F32)),
        scratch_shapes=[pltpu.VMEM((2, 2, CHUNK, LANES), F32)],
        compiler_params=_params(("parallel", "arbitrary")),
        name="ssd_scan",
    )(xbc, dt, xbc, dt, dtbias_row, alog_row)


def _linear_scan(a, b, reverse):
    n = a.shape[0]
    row = lax.broadcasted_iota(jnp.int32, (n, 1), 0)
    s = 1
    while s < n:
        if reverse:
            ok = row < n - s
            a_sh = jnp.where(ok, pltpu.roll(a, n - s, axis=0), 1.0)
            b_sh = jnp.where(ok, pltpu.roll(b, n - s, axis=0), 0.0)
        else:
            ok = row >= s
            a_sh = jnp.where(ok, pltpu.roll(a, s, axis=0), 1.0)
            b_sh = jnp.where(ok, pltpu.roll(b, s, axis=0), 0.0)
        b = b + a * b_sh
        a = a * a_sh
        s *= 2
    return a, b


def _lru_kernel(uf_ref, ub_ref, wg_ref, bg_ref, lam_ref, hf_ref, hb_ref, carry_ref):
    k = pl.program_id(1)

    @pl.when(k == 0)
    def _():
        carry_ref[...] = jnp.zeros_like(carry_ref)

    for d, (u_ref, h_ref) in enumerate(((uf_ref, hf_ref), (ub_ref, hb_ref))):
        u = u_ref[...]
        gates = _dot(u.astype(BF16), wg_ref[d]) + bg_ref[d]
        r = jax.nn.sigmoid(gates[:, :LRU_WIDTH])
        ig = jax.nn.sigmoid(gates[:, LRU_WIDTH:])
        log_a = -LRU_C * r * _softplus(-lam_ref[d])
        a = jnp.exp(log_a)
        inp = jnp.sqrt(-jnp.tanh(log_a) * (1.0 + a * a)) * (ig * u)
        a_cum, b_cum = _linear_scan(a, inp, reverse=(d == 1))
        h = b_cum + a_cum * carry_ref[d, 0:1, :]
        h_ref[...] = h
        last = 0 if d == 1 else u.shape[0] - 1
        carry_ref[d, 0:1, :] = h[last:last + 1, :]


def _lru(u, wg, bg, lam, *, batch, seq, ctx_len):
    n = u.shape[0]
    fwd, bwd, steps = _chunk_maps(batch, seq, ctx_len)
    rows = SCAN_CHUNKS * CHUNK
    const = lambda a: pl.BlockSpec(a.shape, lambda b, k: (0,) * a.ndim)
    return pl.pallas_call(
        _lru_kernel,
        grid=(batch, steps),
        in_specs=[pl.BlockSpec((rows, LRU_WIDTH), fwd), pl.BlockSpec((rows, LRU_WIDTH), bwd),
                  const(wg), const(bg), const(lam)],
        out_specs=(pl.BlockSpec((rows, LRU_WIDTH), fwd), pl.BlockSpec((rows, LRU_WIDTH), bwd)),
        out_shape=(jax.ShapeDtypeStruct((n, LRU_WIDTH), F32), jax.ShapeDtypeStruct((n, LRU_WIDTH), F32)),
        scratch_shapes=[pltpu.VMEM((2, SUBLANES, LRU_WIDTH), F32)],
        compiler_params=_params(("parallel", "arbitrary")),
        name="lru_scan",
    )(u, u, wg, bg, lam)


def _stack_heads(q, g):
    qf = q.astype(F32)
    lo = g * LANES
    return jnp.concatenate([qf[:, lo:lo + HEAD_DIM], qf[:, lo + HEAD_DIM:lo + LANES]], axis=0).astype(BF16)


def _value_lanes(g):
    lane = lax.broadcasted_iota(jnp.int32, (1, LANES), 1)
    return (lane < HEAD_DIM) if g == 0 else (lane >= HEAD_DIM)


def _aug_values(v, g):
    return jnp.where(_value_lanes(g), v, jnp.ones_like(v))


def _flash_init(rows, g, sink_pair):
    if sink_pair is None:
        return jnp.full((rows, 1), NEG_INF, F32), jnp.zeros((rows, LANES), F32)
    half = lax.broadcasted_iota(jnp.int32, (rows, 1), 0) < rows // 2
    m = jnp.where(half, sink_pair[0], sink_pair[1]).astype(F32)
    acc = jnp.broadcast_to(jnp.where(_value_lanes(g), 0.0, 1.0), (rows, LANES))
    return m, acc


def _flash_update(state, q2, kt, v_aug, mask=None):
    m, acc = state
    s = _dot(q2, kt)
    if mask is not None:
        s = jnp.where(mask, s, NEG_INF)
    m_new = jnp.maximum(m, jnp.max(s, axis=-1, keepdims=True))
    p = jnp.exp(s - m_new).astype(BF16)
    acc = jnp.exp(m - m_new) * acc + _dot(p, v_aug)
    return m_new, acc


def _flash_finish(states, tq):
    pieces = []
    for g, (_, acc) in enumerate(states):
        den = (1 - g) * HEAD_DIM
        o = acc[:, g * HEAD_DIM:(g + 1) * HEAD_DIM] / acc[:, den:den + 1]
        pieces += [o[:tq], o[tq:]]
    return jnp.concatenate(pieces, axis=1)


def _group_rows(g):
    return slice(g * HEAD_DIM, (g + 1) * HEAD_DIM)


def _dense_attn_kernel(*refs, tq, seg_lens, has_sink):
    refs = list(refs)
    sink_ref = refs.pop(0) if has_sink else None
    q_ref = refs.pop(0)
    o_ref = refs.pop()
    segs = [(refs[2 * i], refs[2 * i + 1], n) for i, n in enumerate(seg_lens)]
    q = q_ref[...]
    q2 = [_stack_heads(q, g) for g in range(2)]
    states = tuple(_flash_init(2 * tq, g, (sink_ref[2 * g], sink_ref[2 * g + 1]) if has_sink else None)
                   for g in range(2))
    for kt_ref, v_ref, n_keys in segs:
        if n_keys <= KV_CHUNK:
            v = v_ref[...]
            states = tuple(_flash_update(states[g], q2[g], kt_ref[_group_rows(g), :], _aug_values(v, g))
                           for g in range(2))
        else:
            def body(c, sts, kt_ref=kt_ref, v_ref=v_ref):
                off = pl.multiple_of(c * KV_CHUNK, KV_CHUNK)
                v = v_ref[pl.ds(off, KV_CHUNK), :]
                return tuple(_flash_update(sts[g], q2[g], kt_ref[_group_rows(g), pl.ds(off, KV_CHUNK)],
                                           _aug_values(v, g)) for g in range(2))
            states = lax.fori_loop(0, n_keys // KV_CHUNK, body, states, unroll=4)
    o_ref[...] = _flash_finish(states, tq).astype(o_ref.dtype)


def _dense_attn(q, kt, v, sink, *, q_row0, q_len, tq, segs, batch):
    n = q.shape[0]
    qpb = q_len // tq
    q0 = q_row0 // tq
    in_specs, args = [], []
    if sink is not None:
        in_specs.append(pl.BlockSpec(memory_space=pltpu.SMEM))
        args.append(sink)
    in_specs.append(pl.BlockSpec((tq, 256), lambda b, i: (q0 + b * qpb + i, 0)))
    args.append(q)
    for row0, klen in segs:
        k0 = row0 // klen
        in_specs.append(pl.BlockSpec((LANES, klen), lambda b, i, k0=k0: (0, k0 + b)))
        in_specs.append(pl.BlockSpec((klen, LANES), lambda b, i, k0=k0: (k0 + b, 0)))
        args += [kt, v]
    return pl.pallas_call(
        functools.partial(_dense_attn_kernel, tq=tq, seg_lens=tuple(s[1] for s in segs), has_sink=sink is not None),
        grid=(batch, qpb),
        in_specs=in_specs,
        out_specs=pl.BlockSpec((tq, 256), lambda b, i: (b * qpb + i, 0)),
        out_shape=jax.ShapeDtypeStruct((batch * q_len, 256), BF16),
        compiler_params=_params(("parallel", "parallel")),
        name="dense_attn",
    )(*args)


def _window_attn_kernel(sink_ref, q_ref, ktc_ref, vc_ref, ktp_ref, vp_ref, ktm_ref, vm_ref, ktn_ref, vn_ref, o_ref,
                        *, n_tiles):
    n = pl.program_id(1)
    nsub = TQ_WINDOW // CHUNK
    iq = lax.broadcasted_iota(jnp.int32, (2 * CHUNK, CHUNK), 0) & (CHUNK - 1)
    jk = lax.broadcasted_iota(jnp.int32, (2 * CHUNK, CHUNK), 1)
    below = jk >= iq
    above = jk <= iq
    vctx = vc_ref[...]
    for j in range(nsub):
        cols = slice(j * CHUNK, (j + 1) * CHUNK)
        q = q_ref[cols, :]
        states = []
        for g in range(2):
            q2 = _stack_heads(q, g)
            rows = _group_rows(g)
            state = _flash_init(2 * CHUNK, g, (sink_ref[2 * g], sink_ref[2 * g + 1]))
            state = _flash_update(state, q2, ktc_ref[rows, :], _aug_values(vctx, g))
            state = _flash_update(state, q2, ktm_ref[rows, cols], _aug_values(vm_ref[cols, :], g))
            if j > 0:
                prev = slice((j - 1) * CHUNK, j * CHUNK)
                state = _flash_update(state, q2, ktm_ref[rows, prev], _aug_values(vm_ref[prev, :], g), below)
            else:
                state = _flash_update(state, q2, ktp_ref[rows, :], _aug_values(vp_ref[...], g),
                                      jnp.logical_and(below, n > 0))
            if j < nsub - 1:
                nxt = slice((j + 1) * CHUNK, (j + 2) * CHUNK)
                state = _flash_update(state, q2, ktm_ref[rows, nxt], _aug_values(vm_ref[nxt, :], g), above)
            else:
                state = _flash_update(state, q2, ktn_ref[rows, :], _aug_values(vn_ref[...], g),
                                      jnp.logical_and(above, n < n_tiles - 1))
            states.append(state)
        o_ref[cols, :] = _flash_finish(states, CHUNK).astype(o_ref.dtype)


def _window_attn(q, kt, v, sink, *, batch, seq, ctx_len):
    nt = seq // TQ_WINDOW
    nsub = TQ_WINDOW // CHUNK
    nb = seq // CHUNK
    ctx0 = (batch * seq) // ctx_len
    prev = lambda b, n: b * nb + jnp.maximum(n * nsub - 1, 0)
    nxt = lambda b, n: b * nb + jnp.minimum((n + 1) * nsub, nb - 1)
    return pl.pallas_call(
        functools.partial(_window_attn_kernel, n_tiles=nt),
        grid=(batch, nt),
        in_specs=[pl.BlockSpec(memory_space=pltpu.SMEM),
                  pl.BlockSpec((TQ_WINDOW, 256), lambda b, n: (b * nt + n, 0)),
                  pl.BlockSpec((LANES, ctx_len), lambda b, n: (0, ctx0 + b)),
                  pl.BlockSpec((ctx_len, LANES), lambda b, n: (ctx0 + b, 0)),
                  pl.BlockSpec((LANES, CHUNK), lambda b, n: (0, prev(b, n))),
                  pl.BlockSpec((CHUNK, LANES), lambda b, n: (prev(b, n), 0)),
                  pl.BlockSpec((LANES, TQ_WINDOW), lambda b, n: (0, b * nt + n)),
                  pl.BlockSpec((TQ_WINDOW, LANES), lambda b, n: (b * nt + n, 0)),
                  pl.BlockSpec((LANES, CHUNK), lambda b, n: (0, nxt(b, n))),
                  pl.BlockSpec((CHUNK, LANES), lambda b, n: (nxt(b, n), 0))],
        out_specs=pl.BlockSpec((TQ_WINDOW, 256), lambda b, n: (b * nt + n, 0)),
        out_shape=jax.ShapeDtypeStruct((batch * seq, 256), BF16),
        compiler_params=_params(("parallel", "parallel")),
        name="window_attn",
    )(sink, q, kt, v, kt, v, kt, v, kt, v)


def _gelu_tanh(x):
    return 0.5 * x * (1.0 + jnp.tanh(math.sqrt(2.0 / math.pi) * (x + 0.044715 * (x * x * x))))


def _outproj_kernel(x_ref, gate_ref, gpost_ref, oa_ref, od_ref, yf_ref, yb_ref, xs_ref, z_ref, dsk_ref, gn_ref,
                    hf_ref, hb_ref, lg_ref, w_ref, o_ref):
    y_ssd = (yf_ref[...] + yb_ref[...] + xs_ref[...] * dsk_ref[...]) * _silu(z_ref[...])
    ob = _rms(y_ssd, gn_ref[...])
    oc = (hf_ref[...] + hb_ref[...]) * _gelu_tanh(lg_ref[...])
    y = (_dot(oa_ref[...], w_ref[0:256, :]) + _dot(ob.astype(BF16), w_ref[256:512, :])
         + _dot(oc.astype(BF16), w_ref[512:768, :]) + _dot(od_ref[...], w_ref[768:1024, :]))
    o_ref[...] = x_ref[...] + gate_ref[0] * _rms(y, gpost_ref[...])


def _outproj(xu, mod, gpost, oa, od, yf, yb, xbc, z, dsk, gn, hf, hb, lg, w_out, *, n_rows, n_lat, seq, batch):
    d = xu.shape[1]
    row = lambda w: pl.BlockSpec((TM, w), lambda i: (i, 0))
    const = lambda a: pl.BlockSpec(a.shape, lambda i: (0,) * a.ndim)
    return pl.pallas_call(
        _outproj_kernel,
        grid=(n_rows // TM,),
        in_specs=[row(d), pl.BlockSpec((1, 1, d), _mod_spec(2, n_lat, seq, batch, TM)), const(gpost),
                  row(256), row(256), row(256), row(256), row(256), row(256), const(dsk), const(gn),
                  row(256), row(256), row(256), const(w_out)],
        out_specs=row(d),
        out_shape=jax.ShapeDtypeStruct((n_rows, d), F32),
        compiler_params=_params(("parallel",)),
        name="outproj",
    )(xu, mod, gpost, oa, od, yf, yb, xbc, z, dsk, gn, hf, hb, lg, w_out)


def _ceil_seg(c):
    return jnp.floor((c + (SEG_ALIGN - 1)) * (1.0 / SEG_ALIGN)) * SEG_ALIGN


def _router_kernel(x_ref, shift_ref, scale_ref, gpre_ref, rwt_ref, rb_ref, hb_ref, ld_ref, wk_ref, tab_ref):
    h = _rms(x_ref[...], gpre_ref[...])
    h = h * (1.0 + scale_ref[0]) + shift_ref[0]
    hb = h.astype(BF16)
    hb_ref[...] = hb

    scores = jax.nn.sigmoid(_dot_nt(rwt_ref[...], hb))
    biased = scores + rb_ref[...]
    gsz = N_EXPERTS // N_EXPERT_GROUPS
    sub = lax.broadcasted_iota(jnp.int32, (gsz, TM), 0)
    blocks, gscore = [], []
    for g in range(N_EXPERT_GROUPS):
        blk = biased[g * gsz:(g + 1) * gsz, :]
        m1 = jnp.max(blk, axis=0, keepdims=True)
        first = jnp.min(jnp.where(blk == m1, sub, gsz), axis=0, keepdims=True)
        m2 = jnp.max(jnp.where(sub == first, -jnp.inf, blk), axis=0, keepdims=True)
        blocks.append(blk)
        gscore.append(m1 + m2)
    masked = []
    for g in range(N_EXPERT_GROUPS):
        rank = jnp.zeros((1, TM), F32)
        for g2 in range(N_EXPERT_GROUPS):
            if g2 == g:
                continue
            beats = (gscore[g2] > gscore[g]) | ((gscore[g2] == gscore[g]) if g2 < g else False)
            rank = rank + jnp.where(beats, 1.0, 0.0)
        masked.append(jnp.where(rank < TOPK_GROUPS, blocks[g], -jnp.inf))
    vals = jnp.concatenate(masked, axis=0)
    eidx = lax.broadcasted_iota(jnp.int32, (N_EXPERTS, TM), 0)
    rank = jnp.zeros((N_EXPERTS, TM), F32)
    for e2 in range(N_EXPERTS):
        rowv = vals[e2:e2 + 1, :]
        beats = (rowv > vals) | ((rowv == vals) & (eidx > e2))
        rank = rank + jnp.where(beats, 1.0, 0.0)
    sel = rank < TOP_K
    self32 = jnp.where(sel, 1.0, 0.0)
    picked = jnp.where(sel, scores, 0.0)
    wdense = picked / jnp.sum(picked, axis=0, keepdims=True) * ROUTED_SCALE

    tr = lax.broadcasted_iota(jnp.int32, (TM, TM), 0)
    tc = lax.broadcasted_iota(jnp.int32, (TM, TM), 1)
    before = jnp.where(tr < tc, 1.0, 0.0).astype(BF16)
    selb = self32.astype(BF16)
    pos = _dot(selb, before)
    er = lax.broadcasted_iota(jnp.int32, (N_EXPERTS, N_EXPERTS), 0)
    ec = lax.broadcasted_iota(jnp.int32, (N_EXPERTS, N_EXPERTS), 1)
    lower = jnp.where(ec < er, 1.0, 0.0).astype(BF16)
    upper = jnp.where(er < ec, 1.0, 0.0).astype(BF16)
    ksel = _dot(lower, selb)
    cnt_col = _ceil_seg(jnp.sum(self32, axis=1, keepdims=True))
    loc_col = _dot3_left(lower, jnp.broadcast_to(cnt_col, (N_EXPERTS, LANES)))[:, 0:1]
    cnt_row = _ceil_seg(_dot_nt(jnp.ones((SUBLANES, TM), BF16), selb))
    loc_row = _dot3(cnt_row, upper)
    tab_ref[...] = jnp.concatenate([cnt_row, loc_row], axis=1).astype(jnp.int32)

    r8 = lax.broadcasted_iota(jnp.int32, (TOP_K, TM), 0)
    ld = jnp.zeros((TOP_K, TM), F32)
    wk = jnp.zeros((TOP_K, TM), F32)
    stage_row = pos + loc_col
    for k in range(TOP_K):
        one = sel & (ksel == float(k))
        ld = jnp.where(r8 == k, jnp.sum(jnp.where(one, stage_row, 0.0), axis=0, keepdims=True), ld)
        wk = jnp.where(r8 == k, jnp.sum(jnp.where(one, wdense, 0.0), axis=0, keepdims=True), wk)
    ld_ref[...] = ld.astype(jnp.int32)
    wk_ref[...] = wk


def _router(xu, mod, gpre, rwt, rb, *, n_rows, n_lat, seq, batch):
    d = xu.shape[1]
    row = lambda w: pl.BlockSpec((TM, w), lambda i: (i, 0))
    col = pl.BlockSpec((TOP_K, TM), lambda i: (0, i))
    const = lambda a: pl.BlockSpec(a.shape, lambda i: (0,) * a.ndim)
    return pl.pallas_call(
        _router_kernel,
        grid=(n_rows // TM,),
        in_specs=[row(d), pl.BlockSpec((1, 1, d), _mod_spec(3, n_lat, seq, batch, TM)),
                  pl.BlockSpec((1, 1, d), _mod_spec(4, n_lat, seq, batch, TM)),
                  const(gpre), const(rwt), const(rb)],
        out_specs=(row(d), col, col, pl.BlockSpec((SUBLANES, 2 * N_EXPERTS), lambda i: (i, 0))),
        out_shape=(jax.ShapeDtypeStruct((n_rows, d), BF16),
                   jax.ShapeDtypeStruct((TOP_K, n_rows), jnp.int32),
                   jax.ShapeDtypeStruct((TOP_K, n_rows), F32),
                   jax.ShapeDtypeStruct((n_rows // TM * SUBLANES, 2 * N_EXPERTS), jnp.int32)),
        compiler_params=_params(("parallel",)),
        name="router",
    )(xu, mod, mod, gpre, rwt, rb)


def _pow2_pieces(limit):
    bits, b = [], limit
    while b >= SEG_ALIGN:
        bits.append(b)
        b //= 2
    return bits


def _copy_pieces(n, src_ref, src0, dst_ref, dst0, sem, limit, wait, same_src=False):
    for bit in _pow2_pieces(limit):
        @pl.when((n & bit) != 0)
        def _():
            off = n & ~(2 * bit - 1)
            cp = pltpu.make_async_copy(src_ref.at[pl.ds(pl.multiple_of(src0 + (0 if same_src else off), SEG_ALIGN),
                                                          bit)],
                                       dst_ref.at[pl.ds(pl.multiple_of(dst0 + off, SEG_ALIGN), bit)], sem)
            cp.wait() if wait else cp.start()


def _piece_copies(tile, np_ref, dst_ref, stage_ref, slots_ref, sem, to_slots, wait):
    def body(p, c):
        src = stage_ref.at[pl.ds(pl.multiple_of(p * SEG_ALIGN, SEG_ALIGN), SEG_ALIGN)]
        dst = slots_ref.at[pl.ds(pl.multiple_of(dst_ref[tile * STAGE_PIECES + p], SEG_ALIGN), SEG_ALIGN)]
        cp = pltpu.make_async_copy(src, dst, sem) if to_slots else pltpu.make_async_copy(dst, src, sem)
        cp.wait() if wait else cp.start()
        return c
    lax.fori_loop(0, np_ref[tile], body, 0)


def _used_blocks(tile, np_ref):
    return (np_ref[tile] * SEG_ALIGN + TM - 1) // TM


def _stage_rows_iota():
    return lax.broadcasted_iota(jnp.int32, (TM // 2, TM), 0).astype(F32).astype(BF16)


def _pick_matrix(ld, base, vals, jrow):
    rel = (ld - base).astype(F32)
    rel = jnp.where(jnp.logical_and(rel >= 0.0, rel < TM // 2), rel, -1.0).astype(BF16)
    out = jnp.zeros((TM // 2, TM), BF16)
    for k in range(TOP_K):
        out = jnp.where(rel[k:k + 1, :] == jrow, vals[k:k + 1, :], out)
    return out


def _dispatch_kernel(np_ref, dst_ref, pstart_ref, npad_ref, hb_ref, ld_ref, xs_ref, stage, zbuf, sem, zsem):
    i = pl.program_id(0)

    @pl.when(i == 0)
    def _():
        zbuf[...] = jnp.zeros_like(zbuf)
        for wait in (False, True):
            def body(e, c, wait=wait):
                _copy_pieces(npad_ref[e], zbuf, 0, xs_ref, pstart_ref[e], zsem, BM_EXPERT // 2, wait, same_src=True)
                return c
            lax.fori_loop(0, N_EXPERTS, body, 0)

    ld = ld_ref[...]
    hb = hb_ref[...]
    jrow = _stage_rows_iota()
    ones = jnp.ones((TOP_K, TM), BF16)

    def block(b, c):
        for half in range(2):
            base = pl.multiple_of(b * TM + half * (TM // 2), TM // 2)
            stage[pl.ds(base, TM // 2), :] = _dot(_pick_matrix(ld, base, ones, jrow), hb).astype(BF16)
        return c

    lax.fori_loop(0, _used_blocks(i, np_ref), block, 0)
    _piece_copies(i, np_ref, dst_ref, stage, xs_ref, sem, True, False)
    _piece_copies(i, np_ref, dst_ref, stage, xs_ref, sem, True, True)


def _dispatch(tabs, pad_start, n_pad, hb, ld, n_slots):
    n, d = hb.shape
    grid_spec = pltpu.PrefetchScalarGridSpec(
        num_scalar_prefetch=4,
        grid=(n // TM,),
        in_specs=[pl.BlockSpec((TM, d), lambda i, *_: (i, 0)),
                  pl.BlockSpec((TOP_K, TM), lambda i, *_: (0, i))],
        out_specs=pl.BlockSpec(memory_space=pl.ANY),
        scratch_shapes=[pltpu.VMEM((STAGE_ROWS, d), BF16), pltpu.VMEM((BM_EXPERT // 2, d), BF16),
                        pltpu.SemaphoreType.DMA(()), pltpu.SemaphoreType.DMA(())],
    )
    return pl.pallas_call(
        _dispatch_kernel,
        grid_spec=grid_spec,
        out_shape=jax.ShapeDtypeStruct((n_slots, d), BF16),
        compiler_params=_params(("arbitrary",)),
        name="moe_dispatch",
    )(*tabs, pad_start, n_pad, hb, ld)


def _expert_kernel(be_ref, na_ref, xs_ref, wg_ref, wu_ref, wd_ref, ys_ref, wgb, wub, wdb):
    i = pl.program_id(0)

    @pl.when(i < na_ref[0])
    def _():
        @pl.when(jnp.logical_or(i == 0, be_ref[i] != be_ref[jnp.maximum(i - 1, 0)]))
        def _():
            wgb[...] = wg_ref[0, 0].astype(BF16)
            wub[...] = wu_ref[0, 0].astype(BF16)
            wdb[...] = wd_ref[0, 0].astype(BF16)

        half = BM_EXPERT // 2
        for r in range(2):
            xb = xs_ref[r * half:(r + 1) * half, :]
            hid = _silu(_dot(xb, wgb[...])) * _dot(xb, wub[...])
            ys_ref[r * half:(r + 1) * half, :] = _dot(hid.astype(BF16), wdb[...]).astype(ys_ref.dtype)


def _experts(block_e, n_active, xs, wg, wu, wd, layer):
    n_slots, d = xs.shape
    nb = n_slots // BM_EXPERT
    blk = lambda i, be, na: jnp.minimum(i, na[0] - 1)
    tiles = pl.BlockSpec((BM_EXPERT, d), lambda i, be, na: (blk(i, be, na), 0))
    grid_spec = pltpu.PrefetchScalarGridSpec(
        num_scalar_prefetch=2,
        grid=(nb,),
        in_specs=[tiles,
                  pl.BlockSpec((1, 1, d, EXPERT_HIDDEN), lambda i, be, na: (layer, be[blk(i, be, na)], 0, 0)),
                  pl.BlockSpec((1, 1, d, EXPERT_HIDDEN), lambda i, be, na: (layer, be[blk(i, be, na)], 0, 0)),
                  pl.BlockSpec((1, 1, EXPERT_HIDDEN, d), lambda i, be, na: (layer, be[blk(i, be, na)], 0, 0))],
        out_specs=tiles,
        scratch_shapes=[pltpu.VMEM((d, EXPERT_HIDDEN), BF16), pltpu.VMEM((d, EXPERT_HIDDEN), BF16),
                        pltpu.VMEM((EXPERT_HIDDEN, d), BF16)],
    )
    return pl.pallas_call(
        _expert_kernel,
        grid_spec=grid_spec,
        out_shape=jax.ShapeDtypeStruct((n_slots, d), BF16),
        compiler_params=_params(("arbitrary",)),
        name="moe_experts",
    )(block_e, n_active, xs, wg, wu, wd)


def _combine_kernel(np_ref, dst_ref, ys_ref, ld_ref, wk_ref, hb_ref, x_ref, gate_ref, gpost_ref,
                    sg_ref, su_ref, sd_ref, o_ref, stage, acc_ref, sem):
    i = pl.program_id(0)

    @pl.when(i == 0)
    def _():
        stage[...] = jnp.zeros_like(stage)

    _piece_copies(i, np_ref, dst_ref, stage, ys_ref, sem, False, False)
    hb = hb_ref[...]
    acc_ref[...] = _dot((_silu(_dot(hb, sg_ref[...])) * _dot(hb, su_ref[...])).astype(BF16), sd_ref[...])
    _piece_copies(i, np_ref, dst_ref, stage, ys_ref, sem, False, True)

    ld = ld_ref[...]
    wkb = wk_ref[...].astype(BF16)
    jrow = _stage_rows_iota()

    def block(b, c):
        for half in range(2):
            base = pl.multiple_of(b * TM + half * (TM // 2), TM // 2)
            weights = _pick_matrix(ld, base, wkb, jrow)
            acc_ref[...] += lax.dot_general(weights, stage[pl.ds(base, TM // 2), :], (((0,), (0,)), ((), ())),
                                            preferred_element_type=F32)
        return c

    lax.fori_loop(0, _used_blocks(i, np_ref), block, 0)
    o_ref[...] = x_ref[...] + gate_ref[0] * _rms(acc_ref[...], gpost_ref[...])


def _combine(tabs, ys, ld, wk, hb, xu, mod, gpost, sg, su, sd, *, n_rows, n_lat, seq, batch):
    d = xu.shape[1]
    row = lambda w: pl.BlockSpec((TM, w), lambda i, *_: (i, 0))
    col = pl.BlockSpec((TOP_K, TM), lambda i, *_: (0, i))
    const = lambda a: pl.BlockSpec(a.shape, lambda i, *_: (0,) * a.ndim)
    mod_map = _mod_spec(5, n_lat, seq, batch, TM)
    grid_spec = pltpu.PrefetchScalarGridSpec(
        num_scalar_prefetch=2,
        grid=(n_rows // TM,),
        in_specs=[pl.BlockSpec(memory_space=pl.ANY), col, col, row(d), row(d),
                  pl.BlockSpec((1, 1, d), lambda i, *_: mod_map(i)),
                  const(gpost), const(sg), const(su), const(sd)],
        out_specs=row(d),
        scratch_shapes=[pltpu.VMEM((STAGE_ROWS, d), BF16), pltpu.VMEM((TM, d), F32), pltpu.SemaphoreType.DMA(())],
    )
    return pl.pallas_call(
        _combine_kernel,
        grid_spec=grid_spec,
        out_shape=jax.ShapeDtypeStruct((n_rows, d), F32),
        compiler_params=_params(("arbitrary",)),
        name="moe_combine",
    )(*tabs, ys, ld, wk, hb, xu, mod, gpost, sg, su, sd)


def _deinterleave(w):
    cols = w.shape[-1]
    perm = jnp.concatenate([jnp.arange(0, HEAD_DIM, 2), jnp.arange(1, HEAD_DIM, 2)])
    idx = (jnp.arange(cols // HEAD_DIM)[:, None] * HEAD_DIM + perm[None, :]).reshape(-1)
    return w[..., idx]


def _pad_in_proj(w_in):
    d = w_in.shape[0]
    o = 0
    parts = {}
    for name, width in (("qa", 256), ("ka", 128), ("va", 128), ("z", 256), ("xs", 256), ("bm", 128), ("cm", 128),
                        ("dtf", 4), ("dtb", 4), ("lx", 256), ("lg", 256), ("qd", 256), ("kd", 128), ("vd", 128)):
        parts[name] = w_in[:, o:o + width]
        o += width
    dt = jnp.concatenate([parts["dtf"], parts["dtb"], jnp.zeros((d, LANES - 8), w_in.dtype)], axis=1)
    cols = [_deinterleave(parts["qa"]), _deinterleave(parts["ka"]), parts["va"],
            _deinterleave(parts["qd"]), _deinterleave(parts["kd"]), parts["vd"],
            parts["z"], parts["xs"], parts["bm"], parts["cm"], dt, parts["lx"], parts["lg"]]
    return jnp.concatenate(cols, axis=1).astype(BF16)


def _rope_tables(seq):
    t = jnp.arange(seq)
    rowp = (t // GRID_W).astype(F32)
    colp = (t % GRID_W).astype(F32)
    axis_dim = HEAD_DIM // 2
    inv_freq = ROPE_THETA ** (-jnp.arange(0, axis_dim, 2, dtype=F32) / axis_dim)
    ang = jnp.concatenate([rowp[:, None] * inv_freq, colp[:, None] * inv_freq], axis=-1)
    cos, sin = jnp.cos(ang), jnp.sin(ang)
    cos_h = jnp.concatenate([cos, cos], axis=-1)
    sin_h = jnp.concatenate([-sin, sin], axis=-1)
    return jnp.tile(cos_h, (1, 4)), jnp.tile(sin_h, (1, 4))


def _block_diag(w):
    nb, bd, _ = w.shape
    eye = jnp.eye(nb, dtype=w.dtype)
    return (eye[:, None, :, None] * w[:, :, None, :]).reshape(nb * bd, nb * bd)


def _lane_row(fwd, bwd):
    return jnp.concatenate([fwd, bwd, jnp.zeros((LANES - 8,), F32)]).reshape(1, LANES)


def kernel(x, c, ctx, c_ctx, w_ada, b_ada, g_mix_pre, g_mix_post, g_ffn_pre, g_ffn_post, w_in, w_out, a_sink,
           ssd_conv_w, ssd_conv_b, ssd_dt_bias, ssd_a_log, ssd_d, ssd_norm, lru_conv_w, lru_conv_b, lru_w_a,
           lru_b_a, lru_w_i, lru_b_i, lru_lambda, d_q_norm, d_k_norm, router_w, router_bias, exp_w_gate,
           exp_w_up, exp_w_down, sh_w_gate, sh_w_up, sh_w_down):
    batch, seq, d = x.shape
    ctx_len = ctx.shape[1]
    depth = w_ada.shape[0]
    n_lat = batch * seq
    n_ctx = batch * ctx_len
    n_all = n_lat + n_ctx
    assert seq % TM == 0 and n_ctx % TM == 0 and seq % T_CONV == 0 and ctx_len % T_CONV == 0
    assert seq % TQ_WINDOW == 0 and seq % (SCAN_CHUNKS * CHUNK) == 0 and ctx_len % (SCAN_CHUNKS * CHUNK) == 0
    assert ctx_len <= KV_CHUNK and seq % KV_CHUNK == 0 and seq % TQ_GLOBAL == 0 and batch + 1 <= SUBLANES

    xu = jnp.concatenate([x.reshape(n_lat, d), ctx.reshape(n_ctx, d)], axis=0)
    cin = jnp.concatenate([c, c_ctx[None, :], jnp.zeros((SUBLANES - batch - 1, d), F32)], axis=0)
    mod_all = _adaln(cin, w_ada, b_ada)
    cos_t, sin_t = _rope_tables(seq)
    hm = jnp.kron(jnp.eye(4, dtype=F32), jnp.full((HEAD_DIM, HEAD_DIM), 1.0 / HEAD_DIM, F32)).astype(BF16)

    for l in range(depth):
        with_ctx = l < depth - 1
        mod = mod_all[l].reshape(SUBLANES * 6, 1, d)
        gq = jnp.tile(_deinterleave(d_q_norm[l]), 4).reshape(1, 256)
        gk = jnp.tile(_deinterleave(d_k_norm[l]), 2).reshape(1, LANES)
        qa, kat, va, qd, kdt, vd, z, xbc_raw, dt, lx_raw, lg = _inproj(
            xu, mod, g_mix_pre[l].reshape(1, d), _pad_in_proj(w_in[l]), cos_t, sin_t, gq, gk, hm,
            n_lat=n_lat, seq=seq, batch=batch)

        xbc, lu = _conv(xbc_raw, lx_raw, ssd_conv_w[l], ssd_conv_b[l].reshape(1, -1),
                        lru_conv_w[l], lru_conv_b[l].reshape(1, -1), n_lat=n_lat, seq=seq, ctx_len=ctx_len)
        yf, yb = _ssd(xbc, dt, _lane_row(ssd_dt_bias[l, 0], ssd_dt_bias[l, 1]),
                      _lane_row(ssd_a_log[l, 0], ssd_a_log[l, 1]), batch=batch, seq=seq, ctx_len=ctx_len)
        wg = jnp.stack([jnp.concatenate([_block_diag(lru_w_a[l, dd]), _block_diag(lru_w_i[l, dd])], axis=1)
                        for dd in range(2)]).astype(BF16)
        bg = jnp.concatenate([lru_b_a[l], lru_b_i[l]], axis=1).reshape(2, 1, 2 * LRU_WIDTH)
        hf, hb = _lru(lu, wg, bg, lru_lambda[l].reshape(2, 1, LRU_WIDTH), batch=batch, seq=seq, ctx_len=ctx_len)

        oa = _window_attn(qa, kat, va, a_sink[l], batch=batch, seq=seq, ctx_len=ctx_len)
        od = _dense_attn(qd, kdt, vd, None, q_row0=0, q_len=seq, tq=TQ_GLOBAL,
                         segs=[(n_lat, ctx_len), (0, seq)], batch=batch)
        if with_ctx:
            oa_c = _dense_attn(qa, kat, va, a_sink[l], q_row0=n_lat, q_len=ctx_len, tq=ctx_len,
                               segs=[(n_lat, ctx_len)], batch=batch)
            od_c = _dense_attn(qd, kdt, vd, None, q_row0=n_lat, q_len=ctx_len, tq=ctx_len,
                               segs=[(n_lat, ctx_len)], batch=batch)
            oa = jnp.concatenate([oa, oa_c], axis=0)
            od = jnp.concatenate([od, od_c], axis=0)
        n_rows = n_all if with_ctx else n_lat

        dsk = jnp.repeat(ssd_d[l], HEAD_DIM).reshape(1, 256)
        xu_mid = _outproj(xu, mod, g_mix_post[l].reshape(1, d), oa, od, yf, yb, xbc, z, dsk,
                          ssd_norm[l].reshape(1, 256), hf, hb, lg, w_out[l].astype(BF16),
                          n_rows=n_rows, n_lat=n_lat, seq=seq, batch=batch)

        hb_ffn, ld, wk, tab = _router(xu_mid, mod, g_ffn_pre[l].reshape(1, d), router_w[l].T.astype(BF16),
                                      router_bias[l].reshape(N_EXPERTS, 1), n_rows=n_rows, n_lat=n_lat,
                                      seq=seq, batch=batch)
        n_tiles = n_rows // TM
        tab = tab.reshape(n_tiles, SUBLANES, 2 * N_EXPERTS)[:, 0, :]
        seg_cnt, seg_loc = tab[:, :N_EXPERTS], tab[:, N_EXPERTS:]
        counts = jnp.sum(seg_cnt, axis=0)
        padded = (counts + BM_EXPERT - 1) // BM_EXPERT * BM_EXPERT
        padded_end = jnp.cumsum(padded)
        offs = padded_end - padded
        seg_off = offs[None, :] + jnp.cumsum(seg_cnt, axis=0) - seg_cnt
        n_blocks = (n_rows * TOP_K + n_tiles * N_EXPERTS * SEG_ALIGN) // BM_EXPERT + N_EXPERTS
        n_active = (padded_end[-1] // BM_EXPERT).astype(jnp.int32).reshape(1)
        block_start = jnp.arange(n_blocks, dtype=jnp.int32) * BM_EXPERT
        block_e = jnp.minimum(jnp.sum((padded_end[None, :] <= block_start[:, None]).astype(jnp.int32), axis=1),
                              N_EXPERTS - 1)
        piece_row = jnp.arange(STAGE_PIECES, dtype=jnp.int32) * SEG_ALIGN
        seg_end = seg_loc + seg_cnt
        piece_e = jnp.sum((seg_end[:, None, :] <= piece_row[None, :, None]).astype(jnp.int32), axis=-1)
        shift = jnp.sum(jnp.where(piece_e[:, :, None] == jnp.arange(N_EXPERTS, dtype=jnp.int32),
                                  (seg_off - seg_loc)[:, None, :], 0), axis=-1)
        tabs = (seg_end[:, -1] // SEG_ALIGN, (piece_row[None, :] + shift).reshape(-1))
        xs = _dispatch(tabs, offs + counts, padded - counts, hb_ffn, ld, n_blocks * BM_EXPERT)
        ys = _experts(block_e, n_active, xs, exp_w_gate, exp_w_up, exp_w_down, l)
        xu = _combine(tabs, ys, ld, wk, hb_ffn, xu_mid, mod, g_ffn_post[l].reshape(1, d), sh_w_gate[l].astype(BF16),
                      sh_w_up[l].astype(BF16), sh_w_down[l].astype(BF16),
                      n_rows=n_rows, n_lat=n_lat, seq=seq, batch=batch)
    return xu[:n_lat].reshape(batch, seq, d)
```

```python
import functools
import math

import jax
import jax.numpy as jnp
from jax import lax
from jax.experimental import pallas as pl
from jax.experimental.pallas import tpu as pltpu

F32 = jnp.float32
BF16 = jnp.bfloat16

HEAD_DIM = 64
GRID_W = 64
ROPE_THETA = 10000.0
NORM_EPS = 1e-6
NEG_INF = -1e30
A_HEADS, A_KV_HEADS, WINDOW = 4, 2, 128
SSD_HEADS, SSD_GROUPS, SSD_STATE, SSD_CONV = 4, 2, 64, 4
LRU_WIDTH, LRU_BLOCKS, LRU_CONV, LRU_C = 256, 4, 4, 8.0
D_HEADS, D_KV_HEADS = 4, 2
N_EXPERTS, N_EXPERT_GROUPS, TOPK_GROUPS, TOP_K = 64, 8, 4, 8
EXPERT_HIDDEN, SHARED_HIDDEN = 256, 256
ROUTED_SCALE = 2.5

LANES = 128
SUBLANES = 8

TM = 512
T_CONV = 256
CHUNK = 128
SCAN_CHUNKS = 2
TQ_GLOBAL = 256
TQ_WINDOW = 512
KV_CHUNK = 512
BM_EXPERT = 512
SEG_ALIGN = 16
STAGE_ROWS = TM * TOP_K + N_EXPERTS * SEG_ALIGN
STAGE_PIECES = STAGE_ROWS // SEG_ALIGN
VMEM_LIMIT = 48 * 1024 * 1024

C_QA, C_KA, C_VA = 0, 256, 384
C_QD, C_KD, C_VD = 512, 768, 896
C_Z, C_XBC, C_DT = 1024, 1280, 1792
C_LX, C_LG = 1920, 2176
NP_IN = 2432


def _dot(a, b):
    return jnp.dot(a, b, preferred_element_type=F32)


def _dot_nt(a, b):
    return lax.dot_general(a, b, (((1,), (1,)), ((), ())), preferred_element_type=F32)


def _dot3(a, b):
    a1 = a.astype(BF16)
    r1 = a - a1.astype(F32)
    a2 = r1.astype(BF16)
    a3 = (r1 - a2.astype(F32)).astype(BF16)
    return _dot(a1, b) + _dot(a2, b) + _dot(a3, b)


def _dot3_left(a, b):
    b1 = b.astype(BF16)
    r1 = b - b1.astype(F32)
    b2 = r1.astype(BF16)
    b3 = (r1 - b2.astype(F32)).astype(BF16)
    return _dot(a, b1) + _dot(a, b2) + _dot(a, b3)


def _silu(x):
    return x * jax.nn.sigmoid(x)


def _softplus(x):
    return jnp.maximum(x, 0.0) + jnp.log1p(jnp.exp(-jnp.abs(x)))


def _rms(x, gain):
    return x * lax.rsqrt(jnp.mean(x * x, axis=-1, keepdims=True) + NORM_EPS) * gain


def _params(sem=None):
    return pltpu.CompilerParams(dimension_semantics=sem, vmem_limit_bytes=VMEM_LIMIT)


def _adaln_kernel(c_ref, w_ref, b_ref, o_ref):
    s = _silu(c_ref[...])
    o_ref[0] = _dot(s.astype(BF16), w_ref[0].astype(BF16)) + b_ref[0]


def _adaln(cin, w_ada, b_ada):
    depth, d, n6 = w_ada.shape
    tn = 1024
    return pl.pallas_call(
        _adaln_kernel,
        grid=(depth, n6 // tn),
        in_specs=[pl.BlockSpec((SUBLANES, d), lambda l, j: (0, 0)),
                  pl.BlockSpec((1, d, tn), lambda l, j: (l, 0, j)),
                  pl.BlockSpec((1, 1, tn), lambda l, j: (l, 0, j))],
        out_specs=pl.BlockSpec((1, SUBLANES, tn), lambda l, j: (l, 0, j)),
        out_shape=jax.ShapeDtypeStruct((depth, SUBLANES, n6), F32),
        compiler_params=_params(("parallel", "parallel")),
        name="adaln",
    )(cin, w_ada, b_ada.reshape(depth, 1, n6))


def _swap_halves(t):
    w = t.shape[1]
    lane = lax.broadcasted_iota(jnp.int32, (1, w), 1)
    first = (lane & 32) == 0
    return jnp.where(first, pltpu.roll(t, w - 32, axis=1), pltpu.roll(t, 32, axis=1))


def _inproj_kernel(x_ref, shift_ref, scale_ref, gpre_ref, w_ref, cos_ref, sin_ref, gq_ref, gk_ref, hm_ref,
                   qa_ref, kat_ref, va_ref, qd_ref, kdt_ref, vd_ref, z_ref, xbc_ref, dt_ref, lx_ref, lg_ref,
                   *, n_lat):
    i = pl.program_id(0)
    is_lat = i * TM < n_lat
    h = _rms(x_ref[...], gpre_ref[...])
    h = h * (1.0 + scale_ref[0]) + shift_ref[0]
    hb = h.astype(BF16)

    def sec(a, b):
        return _dot(hb, w_ref[:, a:b])

    cos = jnp.where(is_lat, cos_ref[...], 1.0)
    sin = jnp.where(is_lat, sin_ref[...], 0.0)

    def rope(t):
        w = t.shape[1]
        return t * cos[:, :w] + _swap_halves(t) * sin[:, :w]

    def head_norm(t, gain):
        w = t.shape[1]
        ms = _dot3(t * t, hm_ref[:w, :w])
        return t * lax.rsqrt(ms + NORM_EPS) * gain

    scale = HEAD_DIM ** -0.5
    qa_ref[...] = (rope(sec(C_QA, C_KA)) * scale).astype(BF16)
    kat_ref[...] = rope(sec(C_KA, C_VA)).T.astype(BF16)
    va_ref[...] = sec(C_VA, C_QD).astype(BF16)
    qd_ref[...] = (rope(head_norm(sec(C_QD, C_KD), gq_ref[...])) * scale).astype(BF16)
    kdt_ref[...] = rope(head_norm(sec(C_KD, C_VD), gk_ref[...])).T.astype(BF16)
    vd_ref[...] = sec(C_VD, C_Z).astype(BF16)
    z_ref[...] = sec(C_Z, C_XBC)
    xbc_ref[...] = sec(C_XBC, C_DT)
    dt_ref[...] = sec(C_DT, C_LX)
    lx_ref[...] = sec(C_LX, C_LG)
    lg_ref[...] = sec(C_LG, NP_IN)


def _mod_spec(chunk, n_lat, seq, batch, tile):
    def imap(i):
        row0 = i * tile
        seg = jnp.where(row0 < n_lat, row0 // seq, batch)
        return (seg * 6 + chunk, 0, 0)
    return imap


def _inproj(xu, mod, gpre, w_pad, cos_t, sin_t, gq, gk, hm, *, n_lat, seq, batch):
    n, d = xu.shape
    nt = n // TM
    spt = seq // TM
    row = lambda w: pl.BlockSpec((TM, w), lambda i: (i, 0))
    colT = pl.BlockSpec((LANES, TM), lambda i: (0, i))
    const = lambda a: pl.BlockSpec(a.shape, lambda i: (0,) * a.ndim)
    out_shapes = (
        jax.ShapeDtypeStruct((n, 256), BF16), jax.ShapeDtypeStruct((LANES, n), BF16),
        jax.ShapeDtypeStruct((n, LANES), BF16),
        jax.ShapeDtypeStruct((n, 256), BF16), jax.ShapeDtypeStruct((LANES, n), BF16),
        jax.ShapeDtypeStruct((n, LANES), BF16),
        jax.ShapeDtypeStruct((n, 256), F32), jax.ShapeDtypeStruct((n, 512), F32),
        jax.ShapeDtypeStruct((n, LANES), F32), jax.ShapeDtypeStruct((n, 256), F32),
        jax.ShapeDtypeStruct((n, 256), F32))
    return pl.pallas_call(
        functools.partial(_inproj_kernel, n_lat=n_lat),
        grid=(nt,),
        in_specs=[row(d),
                  pl.BlockSpec((1, 1, d), _mod_spec(0, n_lat, seq, batch, TM)),
                  pl.BlockSpec((1, 1, d), _mod_spec(1, n_lat, seq, batch, TM)),
                  const(gpre), const(w_pad),
                  pl.BlockSpec((TM, 256), lambda i: (i % spt, 0)),
                  pl.BlockSpec((TM, 256), lambda i: (i % spt, 0)),
                  const(gq), const(gk), const(hm)],
        out_specs=(row(256), colT, row(LANES), row(256), colT, row(LANES),
                   row(256), row(512), row(LANES), row(256), row(256)),
        out_shape=out_shapes,
        compiler_params=_params(("parallel",)),
        name="inproj",
    )(xu, mod, mod, gpre, w_pad, cos_t, sin_t, gq, gk, hm)


def _conv_kernel(xs_ref, xsp_ref, xsn_ref, xl_ref, xlp_ref, xln_ref, ws_ref, bs_ref, wl_ref, bl_ref,
                 os_ref, ol_ref, *, n_lat, seq, ctx_len):
    i = pl.program_id(0)
    row0 = i * T_CONV
    pos = jnp.where(row0 < n_lat, row0 % seq, (row0 - n_lat) % ctx_len)
    slen = jnp.where(row0 < n_lat, seq, ctx_len)
    first = pos == 0
    last = pos + T_CONV == slen
    row = lax.broadcasted_iota(jnp.int32, (T_CONV, 1), 0)

    def conv(x, prev, nxt, w, b):
        pm = jnp.where(first, 0.0, prev)
        nx = jnp.where(last, 0.0, nxt)
        xm1 = jnp.where(row == 0, pm[7:8, :], pltpu.roll(x, 1, axis=0))
        xm2 = jnp.where(row == 0, pm[6:7, :], jnp.where(row == 1, pm[7:8, :], pltpu.roll(x, 2, axis=0)))
        xp1 = jnp.where(row == T_CONV - 1, nx[0:1, :], pltpu.roll(x, T_CONV - 1, axis=0))
        return w[0:1, :] * xm2 + w[1:2, :] * xm1 + w[2:3, :] * x + w[3:4, :] * xp1 + b

    os_ref[...] = _silu(conv(xs_ref[...], xsp_ref[...], xsn_ref[...], ws_ref[...], bs_ref[...]))
    ol_ref[...] = conv(xl_ref[...], xlp_ref[...], xln_ref[...], wl_ref[...], bl_ref[...])


def _conv(xbc_raw, lx_raw, ws, bs, wl, bl, *, n_lat, seq, ctx_len):
    n = xbc_raw.shape[0]
    nt = n // T_CONV
    r8 = T_CONV // SUBLANES
    n8 = n // SUBLANES
    main = lambda w: pl.BlockSpec((T_CONV, w), lambda i: (i, 0))
    prev = lambda w: pl.BlockSpec((SUBLANES, w), lambda i: (jnp.maximum(i * r8 - 1, 0), 0))
    nxt = lambda w: pl.BlockSpec((SUBLANES, w), lambda i: (jnp.minimum((i + 1) * r8, n8 - 1), 0))
    const = lambda a: pl.BlockSpec(a.shape, lambda i: (0,) * a.ndim)
    return pl.pallas_call(
        functools.partial(_conv_kernel, n_lat=n_lat, seq=seq, ctx_len=ctx_len),
        grid=(nt,),
        in_specs=[main(512), prev(512), nxt(512), main(256), prev(256), nxt(256),
                  const(ws), const(bs), const(wl), const(bl)],
        out_specs=(main(512), main(256)),
        out_shape=(jax.ShapeDtypeStruct((n, 512), F32), jax.ShapeDtypeStruct((n, 256), F32)),
        compiler_params=_params(("parallel",)),
        name="conv",
    )(xbc_raw, xbc_raw, xbc_raw, lx_raw, lx_raw, lx_raw, ws, bs, wl, bl)


def _chunk_maps(batch, seq, ctx_len):
    ncx = ctx_len // (SCAN_CHUNKS * CHUNK)
    nl = seq // (SCAN_CHUNKS * CHUNK)
    lat_blocks = batch * nl

    def block(b, c):
        return jnp.where(c < ncx, lat_blocks + b * ncx + c, b * nl + (c - ncx))

    def fwd(b, k):
        return (block(b, k), 0)

    def bwd(b, k):
        c = jnp.where(k < ncx, ncx - 1 - k, ncx + (nl - 1 - (k - ncx)))
        return (block(b, c), 0)

    return fwd, bwd, ncx + nl


def _ssd_kernel(xf_ref, dtf_ref, xb_ref, dtb_ref, dtbias_ref, alog_ref, yf_ref, yb_ref, state_ref):
    k = pl.program_id(1)

    @pl.when(k == 0)
    def _():
        state_ref[...] = jnp.zeros_like(state_ref)

    ri = lax.broadcasted_iota(jnp.int32, (CHUNK, CHUNK), 0)
    ci = lax.broadcasted_iota(jnp.int32, (CHUNK, CHUNK), 1)
    lane_lo = ci < HEAD_DIM
    aneg = -jnp.exp(alog_ref[...])
    dtbias = dtbias_ref[...]

    order = [(d, s if d == 0 else SCAN_CHUNKS - 1 - s) for s in range(SCAN_CHUNKS) for d in range(2)]
    for d, sub in order:
        x_ref, dt_ref, y_ref = ((xf_ref, dtf_ref, yf_ref), (xb_ref, dtb_ref, yb_ref))[d]
        rws = slice(sub * CHUNK, (sub + 1) * CHUNK)
        causal = (ri >= ci) if d == 0 else (ci >= ri)
        tmat = jnp.where(causal, 1.0, 0.0).astype(BF16)
        xs = x_ref[rws, 0:256]
        bm = x_ref[rws, 256:384]
        cm = x_ref[rws, 384:512]
        dtp = _softplus(dt_ref[rws, :] + dtbias)
        acum = _dot3_left(tmat, dtp * aneg)
        acum_t = acum.T
        bt = bm.T.astype(BF16)
        cmb = cm.astype(BF16)
        bmb = bm.astype(BF16)
        tot_row = CHUNK - 1 if d == 0 else 0
        for p in range(2):
            cmask = jnp.where(lane_lo if p == 0 else jnp.logical_not(lane_lo), cmb, jnp.zeros_like(cmb))
            cb = _dot_nt(cmask, bmb)
            cols, dts, ys = [], [], []
            x_pair = xs[:, p * LANES:(p + 1) * LANES]
            for j in range(2):
                col = 4 * d + 2 * p + j
                colb = jnp.broadcast_to(acum[:, col:col + 1], (CHUNK, CHUNK))
                rowb = jnp.broadcast_to(acum_t[col:col + 1, :], (CHUNK, CHUNK))
                cols.append(colb)
                dts.append(jnp.broadcast_to(dtp[:, col:col + 1], (CHUNK, CHUNK)))
            col_pair = jnp.where(lane_lo, cols[0], cols[1])
            dt_pair = jnp.where(lane_lo, dts[0], dts[1])
            xdt = x_pair * dt_pair
            xdt_b = xdt.astype(BF16)
            for j in range(2):
                col = 4 * d + 2 * p + j
                rowb = jnp.broadcast_to(acum_t[col:col + 1, :], (CHUNK, CHUNK))
                decay = jnp.exp(jnp.where(causal, cols[j] - rowb, NEG_INF))
                ys.append(_dot((cb * decay).astype(BF16), xdt_b))
            y_intra = jnp.where(lane_lo, ys[0], ys[1])
            s_old = state_ref[d, p]
            y_inter = _dot(cmask, s_old.astype(BF16)) * jnp.exp(col_pair)
            y_ref[rws, p * LANES:(p + 1) * LANES] = y_intra + y_inter
            tot_pair = col_pair[tot_row:tot_row + 1, :]
            to_end = jnp.exp(tot_pair - col_pair)
            state_ref[d, p] = s_old * jnp.exp(tot_pair) + _dot(bt, (xdt * to_end).astype(BF16))


def _ssd(xbc, dt, dtbias_row, alog_row, *, batch, seq, ctx_len):
    n = xbc.shape[0]
    fwd, bwd, steps = _chunk_maps(batch, seq, ctx_len)
    rows = SCAN_CHUNKS * CHUNK
    const = lambda a: pl.BlockSpec(a.shape, lambda b, k: (0,) * a.ndim)
    return pl.pallas_call(
        _ssd_kernel,
        grid=(batch, steps),
        in_specs=[pl.BlockSpec((rows, 512), fwd), pl.BlockSpec((rows, LANES), fwd),
                  pl.BlockSpec((rows, 512), bwd), pl.BlockSpec((rows, LANES), bwd),
                  const(dtbias_row), const(alog_row)],
        out_specs=(pl.BlockSpec((rows, 256), fwd), pl.BlockSpec((rows, 256), bwd)),
        out_shape=(jax.ShapeDtypeStruct((n, 256), F32), jax.ShapeDtypeStruct((n, 256), F32)),
        scratch_shapes=[pltpu.VMEM((2, 2, CHUNK, LANES), F32)],
        compiler_params=_params(("parallel", "arbitrary")),
        name="ssd_scan",
    )(xbc, dt, xbc, dt, dtbias_row, alog_row)


def _linear_scan(a, b, reverse):
    n = a.shape[0]
    row = lax.broadcasted_iota(jnp.int32, (n, 1), 0)
    s = 1
    while s < n:
        if reverse:
            ok = row < n - s
            a_sh = jnp.where(ok, pltpu.roll(a, n - s, axis=0), 1.0)
            b_sh = jnp.where(ok, pltpu.roll(b, n - s, axis=0), 0.0)
        else:
            ok = row >= s
            a_sh = jnp.where(ok, pltpu.roll(a, s, axis=0), 1.0)
            b_sh = jnp.where(ok, pltpu.roll(b, s, axis=0), 0.0)
        b = b + a * b_sh
        a = a * a_sh
        s *= 2
    return a, b


def _lru_kernel(uf_ref, ub_ref, wg_ref, bg_ref, lam_ref, hf_ref, hb_ref, carry_ref):
    k = pl.program_id(1)

    @pl.when(k == 0)
    def _():
        carry_ref[...] = jnp.zeros_like(carry_ref)

    for d, (u_ref, h_ref) in enumerate(((uf_ref, hf_ref), (ub_ref, hb_ref))):
        u = u_ref[...]
        gates = _dot(u.astype(BF16), wg_ref[d]) + bg_ref[d]
        r = jax.nn.sigmoid(gates[:, :LRU_WIDTH])
        ig = jax.nn.sigmoid(gates[:, LRU_WIDTH:])
        log_a = -LRU_C * r * _softplus(-lam_ref[d])
        a = jnp.exp(log_a)
        inp = jnp.sqrt(-jnp.tanh(log_a) * (1.0 + a * a)) * (ig * u)
        a_cum, b_cum = _linear_scan(a, inp, reverse=(d == 1))
        h = b_cum + a_cum * carry_ref[d, 0:1, :]
        h_ref[...] = h
        last = 0 if d == 1 else u.shape[0] - 1
        carry_ref[d, 0:1, :] = h[last:last + 1, :]


def _lru(u, wg, bg, lam, *, batch, seq, ctx_len):
    n = u.shape[0]
    fwd, bwd, steps = _chunk_maps(batch, seq, ctx_len)
    rows = SCAN_CHUNKS * CHUNK
    const = lambda a: pl.BlockSpec(a.shape, lambda b, k: (0,) * a.ndim)
    return pl.pallas_call(
        _lru_kernel,
        grid=(batch, steps),
        in_specs=[pl.BlockSpec((rows, LRU_WIDTH), fwd), pl.BlockSpec((rows, LRU_WIDTH), bwd),
                  const(wg), const(bg), const(lam)],
        out_specs=(pl.BlockSpec((rows, LRU_WIDTH), fwd), pl.BlockSpec((rows, LRU_WIDTH), bwd)),
        out_shape=(jax.ShapeDtypeStruct((n, LRU_WIDTH), F32), jax.ShapeDtypeStruct((n, LRU_WIDTH), F32)),
        scratch_shapes=[pltpu.VMEM((2, SUBLANES, LRU_WIDTH), F32)],
        compiler_params=_params(("parallel", "arbitrary")),
        name="lru_scan",
    )(u, u, wg, bg, lam)


def _stack_heads(q, g):
    qf = q.astype(F32)
    lo = g * LANES
    return jnp.concatenate([qf[:, lo:lo + HEAD_DIM], qf[:, lo + HEAD_DIM:lo + LANES]], axis=0).astype(BF16)


def _value_lanes(g):
    lane = lax.broadcasted_iota(jnp.int32, (1, LANES), 1)
    return (lane < HEAD_DIM) if g == 0 else (lane >= HEAD_DIM)


def _aug_values(v, g):
    return jnp.where(_value_lanes(g), v, jnp.ones_like(v))


def _flash_init(rows, g, sink_pair):
    if sink_pair is None:
        return jnp.full((rows, 1), NEG_INF, F32), jnp.zeros((rows, LANES), F32)
    half = lax.broadcasted_iota(jnp.int32, (rows, 1), 0) < rows // 2
    m = jnp.where(half, sink_pair[0], sink_pair[1]).astype(F32)
    acc = jnp.broadcast_to(jnp.where(_value_lanes(g), 0.0, 1.0), (rows, LANES))
    return m, acc


def _flash_update(state, q2, kt, v_aug, mask=None):
    m, acc = state
    s = _dot(q2, kt)
    if mask is not None:
        s = jnp.where(mask, s, NEG_INF)
    m_new = jnp.maximum(m, jnp.max(s, axis=-1, keepdims=True))
    p = jnp.exp(s - m_new).astype(BF16)
    acc = jnp.exp(m - m_new) * acc + _dot(p, v_aug)
    return m_new, acc


def _flash_finish(states, tq):
    pieces = []
    for g, (_, acc) in enumerate(states):
        den = (1 - g) * HEAD_DIM
        o = acc[:, g * HEAD_DIM:(g + 1) * HEAD_DIM] / acc[:, den:den + 1]
        pieces += [o[:tq], o[tq:]]
    return jnp.concatenate(pieces, axis=1)


def _group_rows(g):
    return slice(g * HEAD_DIM, (g + 1) * HEAD_DIM)


def _dense_attn_kernel(*refs, tq, seg_lens, has_sink):
    refs = list(refs)
    sink_ref = refs.pop(0) if has_sink else None
    q_ref = refs.pop(0)
    o_ref = refs.pop()
    segs = [(refs[2 * i], refs[2 * i + 1], n) for i, n in enumerate(seg_lens)]
    q = q_ref[...]
    q2 = [_stack_heads(q, g) for g in range(2)]
    states = tuple(_flash_init(2 * tq, g, (sink_ref[2 * g], sink_ref[2 * g + 1]) if has_sink else None)
                   for g in range(2))
    for kt_ref, v_ref, n_keys in segs:
        if n_keys <= KV_CHUNK:
            v = v_ref[...]
            states = tuple(_flash_update(states[g], q2[g], kt_ref[_group_rows(g), :], _aug_values(v, g))
                           for g in range(2))
        else:
            def body(c, sts, kt_ref=kt_ref, v_ref=v_ref):
                off = pl.multiple_of(c * KV_CHUNK, KV_CHUNK)
                v = v_ref[pl.ds(off, KV_CHUNK), :]
                return tuple(_flash_update(sts[g], q2[g], kt_ref[_group_rows(g), pl.ds(off, KV_CHUNK)],
                                           _aug_values(v, g)) for g in range(2))
            states = lax.fori_loop(0, n_keys // KV_CHUNK, body, states, unroll=4)
    o_ref[...] = _flash_finish(states, tq).astype(o_ref.dtype)


def _dense_attn(q, kt, v, sink, *, q_row0, q_len, tq, segs, batch):
    n = q.shape[0]
    qpb = q_len // tq
    q0 = q_row0 // tq
    in_specs, args = [], []
    if sink is not None:
        in_specs.append(pl.BlockSpec(memory_space=pltpu.SMEM))
        args.append(sink)
    in_specs.append(pl.BlockSpec((tq, 256), lambda b, i: (q0 + b * qpb + i, 0)))
    args.append(q)
    for row0, klen in segs:
        k0 = row0 // klen
        in_specs.append(pl.BlockSpec((LANES, klen), lambda b, i, k0=k0: (0, k0 + b)))
        in_specs.append(pl.BlockSpec((klen, LANES), lambda b, i, k0=k0: (k0 + b, 0)))
        args += [kt, v]
    return pl.pallas_call(
        functools.partial(_dense_attn_kernel, tq=tq, seg_lens=tuple(s[1] for s in segs), has_sink=sink is not None),
        grid=(batch, qpb),
        in_specs=in_specs,
        out_specs=pl.BlockSpec((tq, 256), lambda b, i: (b * qpb + i, 0)),
        out_shape=jax.ShapeDtypeStruct((batch * q_len, 256), BF16),
        compiler_params=_params(("parallel", "parallel")),
        name="dense_attn",
    )(*args)


def _window_attn_kernel(sink_ref, q_ref, ktc_ref, vc_ref, ktp_ref, vp_ref, ktm_ref, vm_ref, ktn_ref, vn_ref, o_ref,
                        *, n_tiles):
    n = pl.program_id(1)
    nsub = TQ_WINDOW // CHUNK
    iq = lax.broadcasted_iota(jnp.int32, (2 * CHUNK, CHUNK), 0) & (CHUNK - 1)
    jk = lax.broadcasted_iota(jnp.int32, (2 * CHUNK, CHUNK), 1)
    below = jk >= iq
    above = jk <= iq
    vctx = vc_ref[...]
    for j in range(nsub):
        cols = slice(j * CHUNK, (j + 1) * CHUNK)
        q = q_ref[cols, :]
        states = []
        for g in range(2):
            q2 = _stack_heads(q, g)
            rows = _group_rows(g)
            state = _flash_init(2 * CHUNK, g, (sink_ref[2 * g], sink_ref[2 * g + 1]))
            state = _flash_update(state, q2, ktc_ref[rows, :], _aug_values(vctx, g))
            state = _flash_update(state, q2, ktm_ref[rows, cols], _aug_values(vm_ref[cols, :], g))
            if j > 0:
                prev = slice((j - 1) * CHUNK, j * CHUNK)
                state = _flash_update(state, q2, ktm_ref[rows, prev], _aug_values(vm_ref[prev, :], g), below)
            else:
                state = _flash_update(state, q2, ktp_ref[rows, :], _aug_values(vp_ref[...], g),
                                      jnp.logical_and(below, n > 0))
            if j < nsub - 1:
                nxt = slice((j + 1) * CHUNK, (j + 2) * CHUNK)
                state = _flash_update(state, q2, ktm_ref[rows, nxt], _aug_values(vm_ref[nxt, :], g), above)
            else:
                state = _flash_update(state, q2, ktn_ref[rows, :], _aug_values(vn_ref[...], g),
                                      jnp.logical_and(above, n < n_tiles - 1))
            states.append(state)
        o_ref[cols, :] = _flash_finish(states, CHUNK).astype(o_ref.dtype)


def _window_attn(q, kt, v, sink, *, batch, seq, ctx_len):
    nt = seq // TQ_WINDOW
    nsub = TQ_WINDOW // CHUNK
    nb = seq // CHUNK
    ctx0 = (batch * seq) // ctx_len
    prev = lambda b, n: b * nb + jnp.maximum(n * nsub - 1, 0)
    nxt = lambda b, n: b * nb + jnp.minimum((n + 1) * nsub, nb - 1)
    return pl.pallas_call(
        functools.partial(_window_attn_kernel, n_tiles=nt),
        grid=(batch, nt),
        in_specs=[pl.BlockSpec(memory_space=pltpu.SMEM),
                  pl.BlockSpec((TQ_WINDOW, 256), lambda b, n: (b * nt + n, 0)),
                  pl.BlockSpec((LANES, ctx_len), lambda b, n: (0, ctx0 + b)),
                  pl.BlockSpec((ctx_len, LANES), lambda b, n: (ctx0 + b, 0)),
                  pl.BlockSpec((LANES, CHUNK), lambda b, n: (0, prev(b, n))),
                  pl.BlockSpec((CHUNK, LANES), lambda b, n: (prev(b, n), 0)),
                  pl.BlockSpec((LANES, TQ_WINDOW), lambda b, n: (0, b * nt + n)),
                  pl.BlockSpec((TQ_WINDOW, LANES), lambda b, n: (b * nt + n, 0)),
                  pl.BlockSpec((LANES, CHUNK), lambda b, n: (0, nxt(b, n))),
                  pl.BlockSpec((CHUNK, LANES), lambda b, n: (nxt(b, n), 0))],
        out_specs=pl.BlockSpec((TQ_WINDOW, 256), lambda b, n: (b * nt + n, 0)),
        out_shape=jax.ShapeDtypeStruct((batch * seq, 256), BF16),
        compiler_params=_params(("parallel", "parallel")),
        name="window_attn",
    )(sink, q, kt, v, kt, v, kt, v, kt, v)


def _gelu_tanh(x):
    return 0.5 * x * (1.0 + jnp.tanh(math.sqrt(2.0 / math.pi) * (x + 0.044715 * (x * x * x))))


def _outproj_kernel(x_ref, gate_ref, gpost_ref, oa_ref, od_ref, yf_ref, yb_ref, xs_ref, z_ref, dsk_ref, gn_ref,
                    hf_ref, hb_ref, lg_ref, w_ref, o_ref):
    y_ssd = (yf_ref[...] + yb_ref[...] + xs_ref[...] * dsk_ref[...]) * _silu(z_ref[...])
    ob = _rms(y_ssd, gn_ref[...])
    oc = (hf_ref[...] + hb_ref[...]) * _gelu_tanh(lg_ref[...])
    y = (_dot(oa_ref[...], w_ref[0:256, :]) + _dot(ob.astype(BF16), w_ref[256:512, :])
         + _dot(oc.astype(BF16), w_ref[512:768, :]) + _dot(od_ref[...], w_ref[768:1024, :]))
    o_ref[...] = x_ref[...] + gate_ref[0] * _rms(y, gpost_ref[...])


def _outproj(xu, mod, gpost, oa, od, yf, yb, xbc, z, dsk, gn, hf, hb, lg, w_out, *, n_rows, n_lat, seq, batch):
    d = xu.shape[1]
    row = lambda w: pl.BlockSpec((TM, w), lambda i: (i, 0))
    const = lambda a: pl.BlockSpec(a.shape, lambda i: (0,) * a.ndim)
    return pl.pallas_call(
        _outproj_kernel,
        grid=(n_rows // TM,),
        in_specs=[row(d), pl.BlockSpec((1, 1, d), _mod_spec(2, n_lat, seq, batch, TM)), const(gpost),
                  row(256), row(256), row(256), row(256), row(256), row(256), const(dsk), const(gn),
                  row(256), row(256), row(256), const(w_out)],
        out_specs=row(d),
        out_shape=jax.ShapeDtypeStruct((n_rows, d), F32),
        compiler_params=_params(("parallel",)),
        name="outproj",
    )(xu, mod, gpost, oa, od, yf, yb, xbc, z, dsk, gn, hf, hb, lg, w_out)


def _ceil_seg(c):
    return jnp.floor((c + (SEG_ALIGN - 1)) * (1.0 / SEG_ALIGN)) * SEG_ALIGN


def _router_kernel(x_ref, shift_ref, scale_ref, gpre_ref, rwt_ref, rb_ref, hb_ref, ld_ref, wk_ref, tab_ref):
    h = _rms(x_ref[...], gpre_ref[...])
    h = h * (1.0 + scale_ref[0]) + shift_ref[0]
    hb = h.astype(BF16)
    hb_ref[...] = hb

    scores = jax.nn.sigmoid(_dot_nt(rwt_ref[...], hb))
    biased = scores + rb_ref[...]
    gsz = N_EXPERTS // N_EXPERT_GROUPS
    sub = lax.broadcasted_iota(jnp.int32, (gsz, TM), 0)
    blocks, gscore = [], []
    for g in range(N_EXPERT_GROUPS):
        blk = biased[g * gsz:(g + 1) * gsz, :]
        m1 = jnp.max(blk, axis=0, keepdims=True)
        first = jnp.min(jnp.where(blk == m1, sub, gsz), axis=0, keepdims=True)
        m2 = jnp.max(jnp.where(sub == first, -jnp.inf, blk), axis=0, keepdims=True)
        blocks.append(blk)
        gscore.append(m1 + m2)
    masked = []
    for g in range(N_EXPERT_GROUPS):
        rank = jnp.zeros((1, TM), F32)
        for g2 in range(N_EXPERT_GROUPS):
            if g2 == g:
                continue
            beats = (gscore[g2] > gscore[g]) | ((gscore[g2] == gscore[g]) if g2 < g else False)
            rank = rank + jnp.where(beats, 1.0, 0.0)
        masked.append(jnp.where(rank < TOPK_GROUPS, blocks[g], -jnp.inf))
    vals = jnp.concatenate(masked, axis=0)
    eidx = lax.broadcasted_iota(jnp.int32, (N_EXPERTS, TM), 0)
    rank = jnp.zeros((N_EXPERTS, TM), F32)
    for e2 in range(N_EXPERTS):
        rowv = vals[e2:e2 + 1, :]
        beats = (rowv > vals) | ((rowv == vals) & (eidx > e2))
        rank = rank + jnp.where(beats, 1.0, 0.0)
    sel = rank < TOP_K
    self32 = jnp.where(sel, 1.0, 0.0)
    picked = jnp.where(sel, scores, 0.0)
    wdense = picked / jnp.sum(picked, axis=0, keepdims=True) * ROUTED_SCALE

    tr = lax.broadcasted_iota(jnp.int32, (TM, TM), 0)
    tc = lax.broadcasted_iota(jnp.int32, (TM, TM), 1)
    before = jnp.where(tr < tc, 1.0, 0.0).astype(BF16)
    selb = self32.astype(BF16)
    pos = _dot(selb, before)
    er = lax.broadcasted_iota(jnp.int32, (N_EXPERTS, N_EXPERTS), 0)
    ec = lax.broadcasted_iota(jnp.int32, (N_EXPERTS, N_EXPERTS), 1)
    lower = jnp.where(ec < er, 1.0, 0.0).astype(BF16)
    upper = jnp.where(er < ec, 1.0, 0.0).astype(BF16)
    ksel = _dot(lower, selb)
    cnt_col = _ceil_seg(jnp.sum(self32, axis=1, keepdims=True))
    loc_col = _dot3_left(lower, jnp.broadcast_to(cnt_col, (N_EXPERTS, LANES)))[:, 0:1]
    cnt_row = _ceil_seg(_dot_nt(jnp.ones((SUBLANES, TM), BF16), selb))
    loc_row = _dot3(cnt_row, upper)
    tab_ref[...] = jnp.concatenate([cnt_row, loc_row], axis=1).astype(jnp.int32)

    r8 = lax.broadcasted_iota(jnp.int32, (TOP_K, TM), 0)
    ld = jnp.zeros((TOP_K, TM), F32)
    wk = jnp.zeros((TOP_K, TM), F32)
    stage_row = pos + loc_col
    for k in range(TOP_K):
        one = sel & (ksel == float(k))
        ld = jnp.where(r8 == k, jnp.sum(jnp.where(one, stage_row, 0.0), axis=0, keepdims=True), ld)
        wk = jnp.where(r8 == k, jnp.sum(jnp.where(one, wdense, 0.0), axis=0, keepdims=True), wk)
    ld_ref[...] = ld.astype(jnp.int32)
    wk_ref[...] = wk


def _router(xu, mod, gpre, rwt, rb, *, n_rows, n_lat, seq, batch):
    d = xu.shape[1]
    row = lambda w: pl.BlockSpec((TM, w), lambda i: (i, 0))
    col = pl.BlockSpec((TOP_K, TM), lambda i: (0, i))
    const = lambda a: pl.BlockSpec(a.shape, lambda i: (0,) * a.ndim)
    return pl.pallas_call(
        _router_kernel,
        grid=(n_rows // TM,),
        in_specs=[row(d), pl.BlockSpec((1, 1, d), _mod_spec(3, n_lat, seq, batch, TM)),
                  pl.BlockSpec((1, 1, d), _mod_spec(4, n_lat, seq, batch, TM)),
                  const(gpre), const(rwt), const(rb)],
        out_specs=(row(d), col, col, pl.BlockSpec((SUBLANES, 2 * N_EXPERTS), lambda i: (i, 0))),
        out_shape=(jax.ShapeDtypeStruct((n_rows, d), BF16),
                   jax.ShapeDtypeStruct((TOP_K, n_rows), jnp.int32),
                   jax.ShapeDtypeStruct((TOP_K, n_rows), F32),
                   jax.ShapeDtypeStruct((n_rows // TM * SUBLANES, 2 * N_EXPERTS), jnp.int32)),
        compiler_params=_params(("parallel",)),
        name="router",
    )(xu, mod, mod, gpre, rwt, rb)


def _pow2_pieces(limit):
    bits, b = [], limit
    while b >= SEG_ALIGN:
        bits.append(b)
        b //= 2
    return bits


def _copy_pieces(n, src_ref, src0, dst_ref, dst0, sem, limit, wait, same_src=False):
    for bit in _pow2_pieces(limit):
        @pl.when((n & bit) != 0)
        def _():
            off = n & ~(2 * bit - 1)
            cp = pltpu.make_async_copy(src_ref.at[pl.ds(pl.multiple_of(src0 + (0 if same_src else off), SEG_ALIGN),
                                                          bit)],
                                       dst_ref.at[pl.ds(pl.multiple_of(dst0 + off, SEG_ALIGN), bit)], sem)
            cp.wait() if wait else cp.start()


def _piece_copies(tile, np_ref, dst_ref, stage_ref, slots_ref, sem, to_slots, wait):
    def body(p, c):
        src = stage_ref.at[pl.ds(pl.multiple_of(p * SEG_ALIGN, SEG_ALIGN), SEG_ALIGN)]
        dst = slots_ref.at[pl.ds(pl.multiple_of(dst_ref[tile * STAGE_PIECES + p], SEG_ALIGN), SEG_ALIGN)]
        cp = pltpu.make_async_copy(src, dst, sem) if to_slots else pltpu.make_async_copy(dst, src, sem)
        cp.wait() if wait else cp.start()
        return c
    lax.fori_loop(0, np_ref[tile], body, 0)


def _used_blocks(tile, np_ref):
    return (np_ref[tile] * SEG_ALIGN + TM - 1) // TM


def _stage_rows_iota():
    return lax.broadcasted_iota(jnp.int32, (TM // 2, TM), 0).astype(F32).astype(BF16)


def _pick_matrix(ld, base, vals, jrow):
    rel = (ld - base).astype(F32)
    rel = jnp.where(jnp.logical_and(rel >= 0.0, rel < TM // 2), rel, -1.0).astype(BF16)
    out = jnp.zeros((TM // 2, TM), BF16)
    for k in range(TOP_K):
        out = jnp.where(rel[k:k + 1, :] == jrow, vals[k:k + 1, :], out)
    return out


def _dispatch_kernel(np_ref, dst_ref, pstart_ref, npad_ref, hb_ref, ld_ref, xs_ref, stage, zbuf, sem, zsem):
    i = pl.program_id(0)

    @pl.when(i == 0)
    def _():
        zbuf[...] = jnp.zeros_like(zbuf)
        for wait in (False, True):
            def body(e, c, wait=wait):
                _copy_pieces(npad_ref[e], zbuf, 0, xs_ref, pstart_ref[e], zsem, BM_EXPERT // 2, wait, same_src=True)
                return c
            lax.fori_loop(0, N_EXPERTS, body, 0)

    ld = ld_ref[...]
    hb = hb_ref[...]
    jrow = _stage_rows_iota()
    ones = jnp.ones((TOP_K, TM), BF16)

    def block(b, c):
        for half in range(2):
            base = pl.multiple_of(b * TM + half * (TM // 2), TM // 2)
            stage[pl.ds(base, TM // 2), :] = _dot(_pick_matrix(ld, base, ones, jrow), hb).astype(BF16)
        return c

    lax.fori_loop(0, _used_blocks(i, np_ref), block, 0)
    _piece_copies(i, np_ref, dst_ref, stage, xs_ref, sem, True, False)
    _piece_copies(i, np_ref, dst_ref, stage, xs_ref, sem, True, True)


def _dispatch(tabs, pad_start, n_pad, hb, ld, n_slots):
    n, d = hb.shape
    grid_spec = pltpu.PrefetchScalarGridSpec(
        num_scalar_prefetch=4,
        grid=(n // TM,),
        in_specs=[pl.BlockSpec((TM, d), lambda i, *_: (i, 0)),
                  pl.BlockSpec((TOP_K, TM), lambda i, *_: (0, i))],
        out_specs=pl.BlockSpec(memory_space=pl.ANY),
        scratch_shapes=[pltpu.VMEM((STAGE_ROWS, d), BF16), pltpu.VMEM((BM_EXPERT // 2, d), BF16),
                        pltpu.SemaphoreType.DMA(()), pltpu.SemaphoreType.DMA(())],
    )
    return pl.pallas_call(
        _dispatch_kernel,
        grid_spec=grid_spec,
        out_shape=jax.ShapeDtypeStruct((n_slots, d), BF16),
        compiler_params=_params(("arbitrary",)),
        name="moe_dispatch",
    )(*tabs, pad_start, n_pad, hb, ld)


def _expert_kernel(be_ref, na_ref, nxt_ref, slot_ref, xs_ref, wg_hbm, wu_hbm, wd_hbm, ys_ref,
                   wg_raw, wu_raw, wd_raw, wgb, wub, wdb, wsem, *, layer):
    i = pl.program_id(0)

    def weight_copies(e, slot):
        return [pltpu.make_async_copy(src.at[layer, e], dst.at[slot], wsem.at[slot, j])
                for j, (src, dst) in enumerate(((wg_hbm, wg_raw), (wu_hbm, wu_raw), (wd_hbm, wd_raw)))]

    @pl.when(i < na_ref[0])
    def _():
        e, slot = be_ref[i], slot_ref[i]

        @pl.when(i == 0)
        def _():
            for cp in weight_copies(e, slot):
                cp.start()

        @pl.when(jnp.logical_or(i == 0, e != be_ref[jnp.maximum(i - 1, 0)]))
        def _():
            for cp in weight_copies(e, slot):
                cp.wait()
            wgb[...] = wg_raw[slot].astype(BF16)
            wub[...] = wu_raw[slot].astype(BF16)
            wdb[...] = wd_raw[slot].astype(BF16)

            @pl.when(nxt_ref[i] != e)
            def _():
                for cp in weight_copies(nxt_ref[i], 1 - slot):
                    cp.start()

        half = BM_EXPERT // 2
        for r in range(2):
            xb = xs_ref[r * half:(r + 1) * half, :]
            hid = _silu(_dot(xb, wgb[...])) * _dot(xb, wub[...])
            ys_ref[r * half:(r + 1) * half, :] = _dot(hid.astype(BF16), wdb[...]).astype(ys_ref.dtype)


def _experts(block_e, n_active, next_e, w_slot, xs, wg, wu, wd, layer):
    n_slots, d = xs.shape
    nb = n_slots // BM_EXPERT
    tiles = pl.BlockSpec((BM_EXPERT, d), lambda i, be, na, *_: (jnp.minimum(i, na[0] - 1), 0))
    anywhere = pl.BlockSpec(memory_space=pl.ANY)
    grid_spec = pltpu.PrefetchScalarGridSpec(
        num_scalar_prefetch=4,
        grid=(nb,),
        in_specs=[tiles, anywhere, anywhere, anywhere],
        out_specs=tiles,
        scratch_shapes=[pltpu.VMEM((2, d, EXPERT_HIDDEN), F32), pltpu.VMEM((2, d, EXPERT_HIDDEN), F32),
                        pltpu.VMEM((2, EXPERT_HIDDEN, d), F32),
                        pltpu.VMEM((d, EXPERT_HIDDEN), BF16), pltpu.VMEM((d, EXPERT_HIDDEN), BF16),
                        pltpu.VMEM((EXPERT_HIDDEN, d), BF16), pltpu.SemaphoreType.DMA((2, 3))],
    )
    return pl.pallas_call(
        functools.partial(_expert_kernel, layer=layer),
        grid_spec=grid_spec,
        out_shape=jax.ShapeDtypeStruct((n_slots, d), BF16),
        compiler_params=_params(("arbitrary",)),
        name="moe_experts",
    )(block_e, n_active, next_e, w_slot, xs, wg, wu, wd)


def _combine_kernel(np_ref, dst_ref, ys_ref, ld_ref, wk_ref, hb_ref, x_ref, gate_ref, gpost_ref,
                    sg_ref, su_ref, sd_ref, o_ref, stage, acc_ref, sem):
    i = pl.program_id(0)

    @pl.when(i == 0)
    def _():
        stage[...] = jnp.zeros_like(stage)

    _piece_copies(i, np_ref, dst_ref, stage, ys_ref, sem, False, False)
    hb = hb_ref[...]
    acc_ref[...] = _dot((_silu(_dot(hb, sg_ref[...])) * _dot(hb, su_ref[...])).astype(BF16), sd_ref[...])
    _piece_copies(i, np_ref, dst_ref, stage, ys_ref, sem, False, True)

    ld = ld_ref[...]
    wkb = wk_ref[...].astype(BF16)
    jrow = _stage_rows_iota()

    def block(b, c):
        for half in range(2):
            base = pl.multiple_of(b * TM + half * (TM // 2), TM // 2)
            weights = _pick_matrix(ld, base, wkb, jrow)
            acc_ref[...] += lax.dot_general(weights, stage[pl.ds(base, TM // 2), :], (((0,), (0,)), ((), ())),
                                            preferred_element_type=F32)
        return c

    lax.fori_loop(0, _used_blocks(i, np_ref), block, 0)
    o_ref[...] = x_ref[...] + gate_ref[0] * _rms(acc_ref[...], gpost_ref[...])


def _combine(tabs, ys, ld, wk, hb, xu, mod, gpost, sg, su, sd, *, n_rows, n_lat, seq, batch):
    d = xu.shape[1]
    row = lambda w: pl.BlockSpec((TM, w), lambda i, *_: (i, 0))
    col = pl.BlockSpec((TOP_K, TM), lambda i, *_: (0, i))
    const = lambda a: pl.BlockSpec(a.shape, lambda i, *_: (0,) * a.ndim)
    mod_map = _mod_spec(5, n_lat, seq, batch, TM)
    grid_spec = pltpu.PrefetchScalarGridSpec(
        num_scalar_prefetch=2,
        grid=(n_rows // TM,),
        in_specs=[pl.BlockSpec(memory_space=pl.ANY), col, col, row(d), row(d),
                  pl.BlockSpec((1, 1, d), lambda i, *_: mod_map(i)),
                  const(gpost), const(sg), const(su), const(sd)],
        out_specs=row(d),
        scratch_shapes=[pltpu.VMEM((STAGE_ROWS, d), BF16), pltpu.VMEM((TM, d), F32), pltpu.SemaphoreType.DMA(())],
    )
    return pl.pallas_call(
        _combine_kernel,
        grid_spec=grid_spec,
        out_shape=jax.ShapeDtypeStruct((n_rows, d), F32),
        compiler_params=_params(("arbitrary",)),
        name="moe_combine",
    )(*tabs, ys, ld, wk, hb, xu, mod, gpost, sg, su, sd)


def _deinterleave(w):
    cols = w.shape[-1]
    perm = jnp.concatenate([jnp.arange(0, HEAD_DIM, 2), jnp.arange(1, HEAD_DIM, 2)])
    idx = (jnp.arange(cols // HEAD_DIM)[:, None] * HEAD_DIM + perm[None, :]).reshape(-1)
    return w[..., idx]


def _pad_in_proj(w_in):
    d = w_in.shape[0]
    o = 0
    parts = {}
    for name, width in (("qa", 256), ("ka", 128), ("va", 128), ("z", 256), ("xs", 256), ("bm", 128), ("cm", 128),
                        ("dtf", 4), ("dtb", 4), ("lx", 256), ("lg", 256), ("qd", 256), ("kd", 128), ("vd", 128)):
        parts[name] = w_in[:, o:o + width]
        o += width
    dt = jnp.concatenate([parts["dtf"], parts["dtb"], jnp.zeros((d, LANES - 8), w_in.dtype)], axis=1)
    cols = [_deinterleave(parts["qa"]), _deinterleave(parts["ka"]), parts["va"],
            _deinterleave(parts["qd"]), _deinterleave(parts["kd"]), parts["vd"],
            parts["z"], parts["xs"], parts["bm"], parts["cm"], dt, parts["lx"], parts["lg"]]
    return jnp.concatenate(cols, axis=1).astype(BF16)


def _rope_tables(seq):
    t = jnp.arange(seq)
    rowp = (t // GRID_W).astype(F32)
    colp = (t % GRID_W).astype(F32)
    axis_dim = HEAD_DIM // 2
    inv_freq = ROPE_THETA ** (-jnp.arange(0, axis_dim, 2, dtype=F32) / axis_dim)
    ang = jnp.concatenate([rowp[:, None] * inv_freq, colp[:, None] * inv_freq], axis=-1)
    cos, sin = jnp.cos(ang), jnp.sin(ang)
    cos_h = jnp.concatenate([cos, cos], axis=-1)
    sin_h = jnp.concatenate([-sin, sin], axis=-1)
    return jnp.tile(cos_h, (1, 4)), jnp.tile(sin_h, (1, 4))


def _block_diag(w):
    nb, bd, _ = w.shape
    eye = jnp.eye(nb, dtype=w.dtype)
    return (eye[:, None, :, None] * w[:, :, None, :]).reshape(nb * bd, nb * bd)


def _lane_row(fwd, bwd):
    return jnp.concatenate([fwd, bwd, jnp.zeros((LANES - 8,), F32)]).reshape(1, LANES)


def kernel(x, c, ctx, c_ctx, w_ada, b_ada, g_mix_pre, g_mix_post, g_ffn_pre, g_ffn_post, w_in, w_out, a_sink,
           ssd_conv_w, ssd_conv_b, ssd_dt_bias, ssd_a_log, ssd_d, ssd_norm, lru_conv_w, lru_conv_b, lru_w_a,
           lru_b_a, lru_w_i, lru_b_i, lru_lambda, d_q_norm, d_k_norm, router_w, router_bias, exp_w_gate,
           exp_w_up, exp_w_down, sh_w_gate, sh_w_up, sh_w_down):
    batch, seq, d = x.shape
    ctx_len = ctx.shape[1]
    depth = w_ada.shape[0]
    n_lat = batch * seq
    n_ctx = batch * ctx_len
    n_all = n_lat + n_ctx
    assert seq % TM == 0 and n_ctx % TM == 0 and seq % T_CONV == 0 and ctx_len % T_CONV == 0
    assert seq % TQ_WINDOW == 0 and seq % (SCAN_CHUNKS * CHUNK) == 0 and ctx_len % (SCAN_CHUNKS * CHUNK) == 0
    assert ctx_len <= KV_CHUNK and seq % KV_CHUNK == 0 and seq % TQ_GLOBAL == 0 and batch + 1 <= SUBLANES

    xu = jnp.concatenate([x.reshape(n_lat, d), ctx.reshape(n_ctx, d)], axis=0)
    cin = jnp.concatenate([c, c_ctx[None, :], jnp.zeros((SUBLANES - batch - 1, d), F32)], axis=0)
    mod_all = _adaln(cin, w_ada, b_ada)
    cos_t, sin_t = _rope_tables(seq)
    hm = jnp.kron(jnp.eye(4, dtype=F32), jnp.full((HEAD_DIM, HEAD_DIM), 1.0 / HEAD_DIM, F32)).astype(BF16)

    for l in range(depth):
        with_ctx = l < depth - 1
        mod = mod_all[l].reshape(SUBLANES * 6, 1, d)
        gq = jnp.tile(_deinterleave(d_q_norm[l]), 4).reshape(1, 256)
        gk = jnp.tile(_deinterleave(d_k_norm[l]), 2).reshape(1, LANES)
        qa, kat, va, qd, kdt, vd, z, xbc_raw, dt, lx_raw, lg = _inproj(
            xu, mod, g_mix_pre[l].reshape(1, d), _pad_in_proj(w_in[l]), cos_t, sin_t, gq, gk, hm,
            n_lat=n_lat, seq=seq, batch=batch)

        xbc, lu = _conv(xbc_raw, lx_raw, ssd_conv_w[l], ssd_conv_b[l].reshape(1, -1),
                        lru_conv_w[l], lru_conv_b[l].reshape(1, -1), n_lat=n_lat, seq=seq, ctx_len=ctx_len)
        yf, yb = _ssd(xbc, dt, _lane_row(ssd_dt_bias[l, 0], ssd_dt_bias[l, 1]),
                      _lane_row(ssd_a_log[l, 0], ssd_a_log[l, 1]), batch=batch, seq=seq, ctx_len=ctx_len)
        wg = jnp.stack([jnp.concatenate([_block_diag(lru_w_a[l, dd]), _block_diag(lru_w_i[l, dd])], axis=1)
                        for dd in range(2)]).astype(BF16)
        bg = jnp.concatenate([lru_b_a[l], lru_b_i[l]], axis=1).reshape(2, 1, 2 * LRU_WIDTH)
        hf, hb = _lru(lu, wg, bg, lru_lambda[l].reshape(2, 1, LRU_WIDTH), batch=batch, seq=seq, ctx_len=ctx_len)

        oa = _window_attn(qa, kat, va, a_sink[l], batch=batch, seq=seq, ctx_len=ctx_len)
        od = _dense_attn(qd, kdt, vd, None, q_row0=0, q_len=seq, tq=TQ_GLOBAL,
                         segs=[(n_lat, ctx_len), (0, seq)], batch=batch)
        if with_ctx:
            oa_c = _dense_attn(qa, kat, va, a_sink[l], q_row0=n_lat, q_len=ctx_len, tq=ctx_len,
                               segs=[(n_lat, ctx_len)], batch=batch)
            od_c = _dense_attn(qd, kdt, vd, None, q_row0=n_lat, q_len=ctx_len, tq=ctx_len,
                               segs=[(n_lat, ctx_len)], batch=batch)
            oa = jnp.concatenate([oa, oa_c], axis=0)
            od = jnp.concatenate([od, od_c], axis=0)
        n_rows = n_all if with_ctx else n_lat

        dsk = jnp.repeat(ssd_d[l], HEAD_DIM).reshape(1, 256)
        xu_mid = _outproj(xu, mod, g_mix_post[l].reshape(1, d), oa, od, yf, yb, xbc, z, dsk,
                          ssd_norm[l].reshape(1, 256), hf, hb, lg, w_out[l].astype(BF16),
                          n_rows=n_rows, n_lat=n_lat, seq=seq, batch=batch)

        hb_ffn, ld, wk, tab = _router(xu_mid, mod, g_ffn_pre[l].reshape(1, d), router_w[l].T.astype(BF16),
                                      router_bias[l].reshape(N_EXPERTS, 1), n_rows=n_rows, n_lat=n_lat,
                                      seq=seq, batch=batch)
        n_tiles = n_rows // TM
        tab = tab.reshape(n_tiles, SUBLANES, 2 * N_EXPERTS)[:, 0, :]
        seg_cnt, seg_loc = tab[:, :N_EXPERTS], tab[:, N_EXPERTS:]
        counts = jnp.sum(seg_cnt, axis=0)
        padded = (counts + BM_EXPERT - 1) // BM_EXPERT * BM_EXPERT
        padded_end = jnp.cumsum(padded)
        offs = padded_end - padded
        seg_off = offs[None, :] + jnp.cumsum(seg_cnt, axis=0) - seg_cnt
        n_blocks = (n_rows * TOP_K + n_tiles * N_EXPERTS * SEG_ALIGN) // BM_EXPERT + N_EXPERTS
        n_active = (padded_end[-1] // BM_EXPERT).astype(jnp.int32).reshape(1)
        block_start = jnp.arange(n_blocks, dtype=jnp.int32) * BM_EXPERT
        block_e = jnp.minimum(jnp.sum((padded_end[None, :] <= block_start[:, None]).astype(jnp.int32), axis=1),
                              N_EXPERTS - 1)
        piece_row = jnp.arange(STAGE_PIECES, dtype=jnp.int32) * SEG_ALIGN
        seg_end = seg_loc + seg_cnt
        piece_e = jnp.sum((seg_end[:, None, :] <= piece_row[None, :, None]).astype(jnp.int32), axis=-1)
        shift = jnp.sum(jnp.where(piece_e[:, :, None] == jnp.arange(N_EXPERTS, dtype=jnp.int32),
                                  (seg_off - seg_loc)[:, None, :], 0), axis=-1)
        tabs = (seg_end[:, -1] // SEG_ALIGN, (piece_row[None, :] + shift).reshape(-1))
        xs = _dispatch(tabs, offs + counts, padded - counts, hb_ffn, ld, n_blocks * BM_EXPERT)
        ids = jnp.arange(N_EXPERTS, dtype=jnp.int32)
        has_rows = padded > 0
        later = jnp.logical_and(ids[None, :] > ids[:, None], has_rows[None, :])
        nxt_of = jnp.min(jnp.where(later, ids[None, :], N_EXPERTS), axis=1)
        nxt_of = jnp.where(nxt_of == N_EXPERTS, ids, nxt_of)
        slot_of = (jnp.cumsum(has_rows.astype(jnp.int32)) - 1) & 1
        own = block_e[:, None] == ids[None, :]
        next_e = jnp.sum(jnp.where(own, nxt_of[None, :], 0), axis=1)
        w_slot = jnp.sum(jnp.where(own, slot_of[None, :], 0), axis=1)
        ys = _experts(block_e, n_active, next_e, w_slot, xs, exp_w_gate, exp_w_up, exp_w_down, l)
        xu = _combine(tabs, ys, ld, wk, hb_ffn, xu_mid, mod, g_ffn_post[l].reshape(1, d), sh_w_gate[l].astype(BF16),
                      sh_w_up[l].astype(BF16), sh_w_down[l].astype(BF16),
                      n_rows=n_rows, n_lat=n_lat, seq=seq, batch=batch)
    return xu[:n_lat].reshape(batch, seq, d)
```

```python
import functools
import math

import jax
import jax.numpy as jnp
from jax import lax
from jax.experimental import pallas as pl
from jax.experimental.pallas import tpu as pltpu

F32 = jnp.float32
BF16 = jnp.bfloat16

HEAD_DIM = 64
GRID_W = 64
ROPE_THETA = 10000.0
NORM_EPS = 1e-6
NEG_INF = -1e30
A_HEADS, A_KV_HEADS, WINDOW = 4, 2, 128
SSD_HEADS, SSD_GROUPS, SSD_STATE, SSD_CONV = 4, 2, 64, 4
LRU_WIDTH, LRU_BLOCKS, LRU_CONV, LRU_C = 256, 4, 4, 8.0
D_HEADS, D_KV_HEADS = 4, 2
N_EXPERTS, N_EXPERT_GROUPS, TOPK_GROUPS, TOP_K = 64, 8, 4, 8
EXPERT_HIDDEN, SHARED_HIDDEN = 256, 256
ROUTED_SCALE = 2.5

LANES = 128
SUBLANES = 8

TM = 512
T_CONV = 256
CHUNK = 128
SCAN_CHUNKS = 2
TQ_GLOBAL = 256
TQ_WINDOW = 512
KV_CHUNK = 512
BM_EXPERT = 512
SEG_ALIGN = 16
STAGE_ROWS = TM * TOP_K + N_EXPERTS * SEG_ALIGN
BIG_PIECE = 64
BIG_MAX = STAGE_ROWS // BIG_PIECE
SMALL_MAX = N_EXPERTS * (BIG_PIECE // SEG_ALIGN - 1)
VMEM_LIMIT = 48 * 1024 * 1024

C_QA, C_KA, C_VA = 0, 256, 384
C_QD, C_KD, C_VD = 512, 768, 896
C_Z, C_XBC, C_DT = 1024, 1280, 1792
C_LX, C_LG = 1920, 2176
NP_IN = 2432


def _dot(a, b):
    return jnp.dot(a, b, preferred_element_type=F32)


def _dot_nt(a, b):
    return lax.dot_general(a, b, (((1,), (1,)), ((), ())), preferred_element_type=F32)


def _dot3(a, b):
    a1 = a.astype(BF16)
    r1 = a - a1.astype(F32)
    a2 = r1.astype(BF16)
    a3 = (r1 - a2.astype(F32)).astype(BF16)
    return _dot(a1, b) + _dot(a2, b) + _dot(a3, b)


def _dot3_left(a, b):
    b1 = b.astype(BF16)
    r1 = b - b1.astype(F32)
    b2 = r1.astype(BF16)
    b3 = (r1 - b2.astype(F32)).astype(BF16)
    return _dot(a, b1) + _dot(a, b2) + _dot(a, b3)


def _silu(x):
    return x * jax.nn.sigmoid(x)


def _softplus(x):
    return jnp.maximum(x, 0.0) + jnp.log1p(jnp.exp(-jnp.abs(x)))


def _rms(x, gain):
    return x * lax.rsqrt(jnp.mean(x * x, axis=-1, keepdims=True) + NORM_EPS) * gain


def _params(sem=None):
    return pltpu.CompilerParams(dimension_semantics=sem, vmem_limit_bytes=VMEM_LIMIT)


def _adaln_kernel(c_ref, w_ref, b_ref, o_ref):
    s = _silu(c_ref[...])
    o_ref[0] = _dot(s.astype(BF16), w_ref[0].astype(BF16)) + b_ref[0]


def _adaln(cin, w_ada, b_ada):
    depth, d, n6 = w_ada.shape
    tn = 1024
    return pl.pallas_call(
        _adaln_kernel,
        grid=(depth, n6 // tn),
        in_specs=[pl.BlockSpec((SUBLANES, d), lambda l, j: (0, 0)),
                  pl.BlockSpec((1, d, tn), lambda l, j: (l, 0, j)),
                  pl.BlockSpec((1, 1, tn), lambda l, j: (l, 0, j))],
        out_specs=pl.BlockSpec((1, SUBLANES, tn), lambda l, j: (l, 0, j)),
        out_shape=jax.ShapeDtypeStruct((depth, SUBLANES, n6), F32),
        compiler_params=_params(("parallel", "parallel")),
        name="adaln",
    )(cin, w_ada, b_ada.reshape(depth, 1, n6))


def _swap_halves(t):
    w = t.shape[1]
    lane = lax.broadcasted_iota(jnp.int32, (1, w), 1)
    first = (lane & 32) == 0
    return jnp.where(first, pltpu.roll(t, w - 32, axis=1), pltpu.roll(t, 32, axis=1))


def _inproj_kernel(x_ref, shift_ref, scale_ref, gpre_ref, w_ref, cos_ref, sin_ref, gq_ref, gk_ref, hm_ref,
                   qa_ref, kat_ref, va_ref, qd_ref, kdt_ref, vd_ref, z_ref, xbc_ref, dt_ref, lx_ref, lg_ref,
                   *, n_lat):
    i = pl.program_id(0)
    is_lat = i * TM < n_lat
    h = _rms(x_ref[...], gpre_ref[...])
    h = h * (1.0 + scale_ref[0]) + shift_ref[0]
    hb = h.astype(BF16)

    def sec(a, b):
        return _dot(hb, w_ref[:, a:b])

    cos = jnp.where(is_lat, cos_ref[...], 1.0)
    sin = jnp.where(is_lat, sin_ref[...], 0.0)

    def rope(t):
        w = t.shape[1]
        return t * cos[:, :w] + _swap_halves(t) * sin[:, :w]

    def head_norm(t, gain):
        w = t.shape[1]
        ms = _dot3(t * t, hm_ref[:w, :w])
        return t * lax.rsqrt(ms + NORM_EPS) * gain

    scale = HEAD_DIM ** -0.5
    qa_ref[...] = (rope(sec(C_QA, C_KA)) * scale).astype(BF16)
    kat_ref[...] = rope(sec(C_KA, C_VA)).T.astype(BF16)
    va_ref[...] = sec(C_VA, C_QD).astype(BF16)
    qd_ref[...] = (rope(head_norm(sec(C_QD, C_KD), gq_ref[...])) * scale).astype(BF16)
    kdt_ref[...] = rope(head_norm(sec(C_KD, C_VD), gk_ref[...])).T.astype(BF16)
    vd_ref[...] = sec(C_VD, C_Z).astype(BF16)
    z_ref[...] = sec(C_Z, C_XBC)
    xbc_ref[...] = sec(C_XBC, C_DT)
    dt_ref[...] = sec(C_DT, C_LX)
    lx_ref[...] = sec(C_LX, C_LG)
    lg_ref[...] = sec(C_LG, NP_IN)


def _mod_spec(chunk, n_lat, seq, batch, tile):
    def imap(i):
        row0 = i * tile
        seg = jnp.where(row0 < n_lat, row0 // seq, batch)
        return (seg * 6 + chunk, 0, 0)
    return imap


def _inproj(xu, mod, gpre, w_pad, cos_t, sin_t, gq, gk, hm, *, n_lat, seq, batch):
    n, d = xu.shape
    nt = n // TM
    spt = seq // TM
    row = lambda w: pl.BlockSpec((TM, w), lambda i: (i, 0))
    colT = pl.BlockSpec((LANES, TM), lambda i: (0, i))
    const = lambda a: pl.BlockSpec(a.shape, lambda i: (0,) * a.ndim)
    out_shapes = (
        jax.ShapeDtypeStruct((n, 256), BF16), jax.ShapeDtypeStruct((LANES, n), BF16),
        jax.ShapeDtypeStruct((n, LANES), BF16),
        jax.ShapeDtypeStruct((n, 256), BF16), jax.ShapeDtypeStruct((LANES, n), BF16),
        jax.ShapeDtypeStruct((n, LANES), BF16),
        jax.ShapeDtypeStruct((n, 256), F32), jax.ShapeDtypeStruct((n, 512), F32),
        jax.ShapeDtypeStruct((n, LANES), F32), jax.ShapeDtypeStruct((n, 256), F32),
        jax.ShapeDtypeStruct((n, 256), F32))
    return pl.pallas_call(
        functools.partial(_inproj_kernel, n_lat=n_lat),
        grid=(nt,),
        in_specs=[row(d),
                  pl.BlockSpec((1, 1, d), _mod_spec(0, n_lat, seq, batch, TM)),
                  pl.BlockSpec((1, 1, d), _mod_spec(1, n_lat, seq, batch, TM)),
                  const(gpre), const(w_pad),
                  pl.BlockSpec((TM, 256), lambda i: (i % spt, 0)),
                  pl.BlockSpec((TM, 256), lambda i: (i % spt, 0)),
                  const(gq), const(gk), const(hm)],
        out_specs=(row(256), colT, row(LANES), row(256), colT, row(LANES),
                   row(256), row(512), row(LANES), row(256), row(256)),
        out_shape=out_shapes,
        compiler_params=_params(("parallel",)),
        name="inproj",
    )(xu, mod, mod, gpre, w_pad, cos_t, sin_t, gq, gk, hm)


def _conv_kernel(xs_ref, xsp_ref, xsn_ref, xl_ref, xlp_ref, xln_ref, ws_ref, bs_ref, wl_ref, bl_ref,
                 os_ref, ol_ref, *, n_lat, seq, ctx_len):
    i = pl.program_id(0)
    row0 = i * T_CONV
    pos = jnp.where(row0 < n_lat, row0 % seq, (row0 - n_lat) % ctx_len)
    slen = jnp.where(row0 < n_lat, seq, ctx_len)
    first = pos == 0
    last = pos + T_CONV == slen
    row = lax.broadcasted_iota(jnp.int32, (T_CONV, 1), 0)

    def conv(x, prev, nxt, w, b):
        pm = jnp.where(first, 0.0, prev)
        nx = jnp.where(last, 0.0, nxt)
        xm1 = jnp.where(row == 0, pm[7:8, :], pltpu.roll(x, 1, axis=0))
        xm2 = jnp.where(row == 0, pm[6:7, :], jnp.where(row == 1, pm[7:8, :], pltpu.roll(x, 2, axis=0)))
        xp1 = jnp.where(row == T_CONV - 1, nx[0:1, :], pltpu.roll(x, T_CONV - 1, axis=0))
        return w[0:1, :] * xm2 + w[1:2, :] * xm1 + w[2:3, :] * x + w[3:4, :] * xp1 + b

    os_ref[...] = _silu(conv(xs_ref[...], xsp_ref[...], xsn_ref[...], ws_ref[...], bs_ref[...]))
    ol_ref[...] = conv(xl_ref[...], xlp_ref[...], xln_ref[...], wl_ref[...], bl_ref[...])


def _conv(xbc_raw, lx_raw, ws, bs, wl, bl, *, n_lat, seq, ctx_len):
    n = xbc_raw.shape[0]
    nt = n // T_CONV
    r8 = T_CONV // SUBLANES
    n8 = n // SUBLANES
    main = lambda w: pl.BlockSpec((T_CONV, w), lambda i: (i, 0))
    prev = lambda w: pl.BlockSpec((SUBLANES, w), lambda i: (jnp.maximum(i * r8 - 1, 0), 0))
    nxt = lambda w: pl.BlockSpec((SUBLANES, w), lambda i: (jnp.minimum((i + 1) * r8, n8 - 1), 0))
    const = lambda a: pl.BlockSpec(a.shape, lambda i: (0,) * a.ndim)
    return pl.pallas_call(
        functools.partial(_conv_kernel, n_lat=n_lat, seq=seq, ctx_len=ctx_len),
        grid=(nt,),
        in_specs=[main(512), prev(512), nxt(512), main(256), prev(256), nxt(256),
                  const(ws), const(bs), const(wl), const(bl)],
        out_specs=(main(512), main(256)),
        out_shape=(jax.ShapeDtypeStruct((n, 512), F32), jax.ShapeDtypeStruct((n, 256), F32)),
        compiler_params=_params(("parallel",)),
        name="conv",
    )(xbc_raw, xbc_raw, xbc_raw, lx_raw, lx_raw, lx_raw, ws, bs, wl, bl)


def _chunk_maps(batch, seq, ctx_len):
    ncx = ctx_len // (SCAN_CHUNKS * CHUNK)
    nl = seq // (SCAN_CHUNKS * CHUNK)
    lat_blocks = batch * nl

    def block(b, c):
        return jnp.where(c < ncx, lat_blocks + b * ncx + c, b * nl + (c - ncx))

    def fwd(b, k):
        return (block(b, k), 0)

    def bwd(b, k):
        c = jnp.where(k < ncx, ncx - 1 - k, ncx + (nl - 1 - (k - ncx)))
        return (block(b, c), 0)

    return fwd, bwd, ncx + nl


def _ssd_kernel(xf_ref, dtf_ref, xb_ref, dtb_ref, dtbias_ref, alog_ref, yf_ref, yb_ref, state_ref):
    k = pl.program_id(1)

    @pl.when(k == 0)
    def _():
        state_ref[...] = jnp.zeros_like(state_ref)

    ri = lax.broadcasted_iota(jnp.int32, (CHUNK, CHUNK), 0)
    ci = lax.broadcasted_iota(jnp.int32, (CHUNK, CHUNK), 1)
    lane_lo = ci < HEAD_DIM
    aneg = -jnp.exp(alog_ref[...])
    dtbias = dtbias_ref[...]

    order = [(d, s if d == 0 else SCAN_CHUNKS - 1 - s) for s in range(SCAN_CHUNKS) for d in range(2)]
    for d, sub in order:
        x_ref, dt_ref, y_ref = ((xf_ref, dtf_ref, yf_ref), (xb_ref, dtb_ref, yb_ref))[d]
        rws = slice(sub * CHUNK, (sub + 1) * CHUNK)
        causal = (ri >= ci) if d == 0 else (ci >= ri)
        tmat = jnp.where(causal, 1.0, 0.0).astype(BF16)
        xs = x_ref[rws, 0:256]
        bm = x_ref[rws, 256:384]
        cm = x_ref[rws, 384:512]
        dtp = _softplus(dt_ref[rws, :] + dtbias)
        acum = _dot3_left(tmat, dtp * aneg)
        acum_t = acum.T
        bt = bm.T.astype(BF16)
        cmb = cm.astype(BF16)
        bmb = bm.astype(BF16)
        tot_row = CHUNK - 1 if d == 0 else 0
        for p in range(2):
            cmask = jnp.where(lane_lo if p == 0 else jnp.logical_not(lane_lo), cmb, jnp.zeros_like(cmb))
            cb = _dot_nt(cmask, bmb)
            cols, dts, ys = [], [], []
            x_pair = xs[:, p * LANES:(p + 1) * LANES]
            for j in range(2):
                col = 4 * d + 2 * p + j
                colb = jnp.broadcast_to(acum[:, col:col + 1], (CHUNK, CHUNK))
                rowb = jnp.broadcast_to(acum_t[col:col + 1, :], (CHUNK, CHUNK))
                cols.append(colb)
                dts.append(jnp.broadcast_to(dtp[:, col:col + 1], (CHUNK, CHUNK)))
            col_pair = jnp.where(lane_lo, cols[0], cols[1])
            dt_pair = jnp.where(lane_lo, dts[0], dts[1])
            xdt = x_pair * dt_pair
            xdt_b = xdt.astype(BF16)
            for j in range(2):
                col = 4 * d + 2 * p + j
                rowb = jnp.broadcast_to(acum_t[col:col + 1, :], (CHUNK, CHUNK))
                decay = jnp.exp(jnp.where(causal, cols[j] - rowb, NEG_INF))
                ys.append(_dot((cb * decay).astype(BF16), xdt_b))
            y_intra = jnp.where(lane_lo, ys[0], ys[1])
            s_old = state_ref[d, p]
            y_inter = _dot(cmask, s_old.astype(BF16)) * jnp.exp(col_pair)
            y_ref[rws, p * LANES:(p + 1) * LANES] = y_intra + y_inter
            tot_pair = col_pair[tot_row:tot_row + 1, :]
            to_end = jnp.exp(tot_pair - col_pair)
            state_ref[d, p] = s_old * jnp.exp(tot_pair) + _dot(bt, (xdt * to_end).astype(BF16))


def _ssd(xbc, dt, dtbias_row, alog_row, *, batch, seq, ctx_len):
    n = xbc.shape[0]
    fwd, bwd, steps = _chunk_maps(batch, seq, ctx_len)
    rows = SCAN_CHUNKS * CHUNK
    const = lambda a: pl.BlockSpec(a.shape, lambda b, k: (0,) * a.ndim)
    return pl.pallas_call(
        _ssd_kernel,
        grid=(batch, steps),
        in_specs=[pl.BlockSpec((rows, 512), fwd), pl.BlockSpec((rows, LANES), fwd),
                  pl.BlockSpec((rows, 512), bwd), pl.BlockSpec((rows, LANES), bwd),
                  const(dtbias_row), const(alog_row)],
        out_specs=(pl.BlockSpec((rows, 256), fwd), pl.BlockSpec((rows, 256), bwd)),
        out_shape=(jax.ShapeDtypeStruct((n, 256), F32), jax.ShapeDtypeStruct((n, 256), F32)),
        scratch_shapes=[pltpu.VMEM((2, 2, CHUNK, LANES), F32)],
        compiler_params=_params(("parallel", "arbitrary")),
        name="ssd_scan",
    )(xbc, dt, xbc, dt, dtbias_row, alog_row)


def _linear_scan(a, b, reverse):
    n = a.shape[0]
    row = lax.broadcasted_iota(jnp.int32, (n, 1), 0)
    s = 1
    while s < n:
        if reverse:
            ok = row < n - s
            a_sh = jnp.where(ok, pltpu.roll(a, n - s, axis=0), 1.0)
            b_sh = jnp.where(ok, pltpu.roll(b, n - s, axis=0), 0.0)
        else:
            ok = row >= s
            a_sh = jnp.where(ok, pltpu.roll(a, s, axis=0), 1.0)
            b_sh = jnp.where(ok, pltpu.roll(b, s, axis=0), 0.0)
        b = b + a * b_sh
        a = a * a_sh
        s *= 2
    return a, b


def _lru_kernel(uf_ref, ub_ref, wg_ref, bg_ref, lam_ref, hf_ref, hb_ref, carry_ref):
    k = pl.program_id(1)

    @pl.when(k == 0)
    def _():
        carry_ref[...] = jnp.zeros_like(carry_ref)

    for d, (u_ref, h_ref) in enumerate(((uf_ref, hf_ref), (ub_ref, hb_ref))):
        u = u_ref[...]
        gates = _dot(u.astype(BF16), wg_ref[d]) + bg_ref[d]
        r = jax.nn.sigmoid(gates[:, :LRU_WIDTH])
        ig = jax.nn.sigmoid(gates[:, LRU_WIDTH:])
        log_a = -LRU_C * r * _softplus(-lam_ref[d])
        a = jnp.exp(log_a)
        inp = jnp.sqrt(-jnp.tanh(log_a) * (1.0 + a * a)) * (ig * u)
        a_cum, b_cum = _linear_scan(a, inp, reverse=(d == 1))
        h = b_cum + a_cum * carry_ref[d, 0:1, :]
        h_ref[...] = h
        last = 0 if d == 1 else u.shape[0] - 1
        carry_ref[d, 0:1, :] = h[last:last + 1, :]


def _lru(u, wg, bg, lam, *, batch, seq, ctx_len):
    n = u.shape[0]
    fwd, bwd, steps = _chunk_maps(batch, seq, ctx_len)
    rows = SCAN_CHUNKS * CHUNK
    const = lambda a: pl.BlockSpec(a.shape, lambda b, k: (0,) * a.ndim)
    return pl.pallas_call(
        _lru_kernel,
        grid=(batch, steps),
        in_specs=[pl.BlockSpec((rows, LRU_WIDTH), fwd), pl.BlockSpec((rows, LRU_WIDTH), bwd),
                  const(wg), const(bg), const(lam)],
        out_specs=(pl.BlockSpec((rows, LRU_WIDTH), fwd), pl.BlockSpec((rows, LRU_WIDTH), bwd)),
        out_shape=(jax.ShapeDtypeStruct((n, LRU_WIDTH), F32), jax.ShapeDtypeStruct((n, LRU_WIDTH), F32)),
        scratch_shapes=[pltpu.VMEM((2, SUBLANES, LRU_WIDTH), F32)],
        compiler_params=_params(("parallel", "arbitrary")),
        name="lru_scan",
    )(u, u, wg, bg, lam)


def _stack_heads(q, g):
    qf = q.astype(F32)
    lo = g * LANES
    return jnp.concatenate([qf[:, lo:lo + HEAD_DIM], qf[:, lo + HEAD_DIM:lo + LANES]], axis=0).astype(BF16)


def _value_lanes(g):
    lane = lax.broadcasted_iota(jnp.int32, (1, LANES), 1)
    return (lane < HEAD_DIM) if g == 0 else (lane >= HEAD_DIM)


def _aug_values(v, g):
    return jnp.where(_value_lanes(g), v, jnp.ones_like(v))


def _flash_init(rows, g, sink_pair):
    if sink_pair is None:
        return jnp.full((rows, 1), NEG_INF, F32), jnp.zeros((rows, LANES), F32)
    half = lax.broadcasted_iota(jnp.int32, (rows, 1), 0) < rows // 2
    m = jnp.where(half, sink_pair[0], sink_pair[1]).astype(F32)
    acc = jnp.broadcast_to(jnp.where(_value_lanes(g), 0.0, 1.0), (rows, LANES))
    return m, acc


def _flash_update(state, q2, kt, v_aug, mask=None):
    m, acc = state
    s = _dot(q2, kt)
    if mask is not None:
        s = jnp.where(mask, s, NEG_INF)
    m_new = jnp.maximum(m, jnp.max(s, axis=-1, keepdims=True))
    p = jnp.exp(s - m_new).astype(BF16)
    acc = jnp.exp(m - m_new) * acc + _dot(p, v_aug)
    return m_new, acc


def _flash_finish(states, tq):
    pieces = []
    for g, (_, acc) in enumerate(states):
        den = (1 - g) * HEAD_DIM
        o = acc[:, g * HEAD_DIM:(g + 1) * HEAD_DIM] / acc[:, den:den + 1]
        pieces += [o[:tq], o[tq:]]
    return jnp.concatenate(pieces, axis=1)


def _group_rows(g):
    return slice(g * HEAD_DIM, (g + 1) * HEAD_DIM)


def _dense_attn_kernel(*refs, tq, seg_lens, has_sink):
    refs = list(refs)
    sink_ref = refs.pop(0) if has_sink else None
    q_ref = refs.pop(0)
    o_ref = refs.pop()
    segs = [(refs[2 * i], refs[2 * i + 1], n) for i, n in enumerate(seg_lens)]
    q = q_ref[...]
    q2 = [_stack_heads(q, g) for g in range(2)]
    states = tuple(_flash_init(2 * tq, g, (sink_ref[2 * g], sink_ref[2 * g + 1]) if has_sink else None)
                   for g in range(2))
    for kt_ref, v_ref, n_keys in segs:
        if n_keys <= KV_CHUNK:
            v = v_ref[...]
            states = tuple(_flash_update(states[g], q2[g], kt_ref[_group_rows(g), :], _aug_values(v, g))
                           for g in range(2))
        else:
            def body(c, sts, kt_ref=kt_ref, v_ref=v_ref):
                off = pl.multiple_of(c * KV_CHUNK, KV_CHUNK)
                v = v_ref[pl.ds(off, KV_CHUNK), :]
                return tuple(_flash_update(sts[g], q2[g], kt_ref[_group_rows(g), pl.ds(off, KV_CHUNK)],
                                           _aug_values(v, g)) for g in range(2))
            states = lax.fori_loop(0, n_keys // KV_CHUNK, body, states, unroll=4)
    o_ref[...] = _flash_finish(states, tq).astype(o_ref.dtype)


def _dense_attn(q, kt, v, sink, *, q_row0, q_len, tq, segs, batch):
    n = q.shape[0]
    qpb = q_len // tq
    q0 = q_row0 // tq
    in_specs, args = [], []
    if sink is not None:
        in_specs.append(pl.BlockSpec(memory_space=pltpu.SMEM))
        args.append(sink)
    in_specs.append(pl.BlockSpec((tq, 256), lambda b, i: (q0 + b * qpb + i, 0)))
    args.append(q)
    for row0, klen in segs:
        k0 = row0 // klen
        in_specs.append(pl.BlockSpec((LANES, klen), lambda b, i, k0=k0: (0, k0 + b)))
        in_specs.append(pl.BlockSpec((klen, LANES), lambda b, i, k0=k0: (k0 + b, 0)))
        args += [kt, v]
    return pl.pallas_call(
        functools.partial(_dense_attn_kernel, tq=tq, seg_lens=tuple(s[1] for s in segs), has_sink=sink is not None),
        grid=(batch, qpb),
        in_specs=in_specs,
        out_specs=pl.BlockSpec((tq, 256), lambda b, i: (b * qpb + i, 0)),
        out_shape=jax.ShapeDtypeStruct((batch * q_len, 256), BF16),
        compiler_params=_params(("parallel", "parallel")),
        name="dense_attn",
    )(*args)


def _window_attn_kernel(sink_ref, q_ref, ktc_ref, vc_ref, ktp_ref, vp_ref, ktm_ref, vm_ref, ktn_ref, vn_ref, o_ref,
                        *, n_tiles):
    n = pl.program_id(1)
    nsub = TQ_WINDOW // CHUNK
    iq = lax.broadcasted_iota(jnp.int32, (2 * CHUNK, CHUNK), 0) & (CHUNK - 1)
    jk = lax.broadcasted_iota(jnp.int32, (2 * CHUNK, CHUNK), 1)
    below = jk >= iq
    above = jk <= iq
    vctx = vc_ref[...]
    for j in range(nsub):
        cols = slice(j * CHUNK, (j + 1) * CHUNK)
        q = q_ref[cols, :]
        states = []
        for g in range(2):
            q2 = _stack_heads(q, g)
            rows = _group_rows(g)
            state = _flash_init(2 * CHUNK, g, (sink_ref[2 * g], sink_ref[2 * g + 1]))
            state = _flash_update(state, q2, ktc_ref[rows, :], _aug_values(vctx, g))
            state = _flash_update(state, q2, ktm_ref[rows, cols], _aug_values(vm_ref[cols, :], g))
            if j > 0:
                prev = slice((j - 1) * CHUNK, j * CHUNK)
                state = _flash_update(state, q2, ktm_ref[rows, prev], _aug_values(vm_ref[prev, :], g), below)
            else:
                state = _flash_update(state, q2, ktp_ref[rows, :], _aug_values(vp_ref[...], g),
                                      jnp.logical_and(below, n > 0))
            if j < nsub - 1:
                nxt = slice((j + 1) * CHUNK, (j + 2) * CHUNK)
                state = _flash_update(state, q2, ktm_ref[rows, nxt], _aug_values(vm_ref[nxt, :], g), above)
            else:
                state = _flash_update(state, q2, ktn_ref[rows, :], _aug_values(vn_ref[...], g),
                                      jnp.logical_and(above, n < n_tiles - 1))
            states.append(state)
        o_ref[cols, :] = _flash_finish(states, CHUNK).astype(o_ref.dtype)


def _window_attn(q, kt, v, sink, *, batch, seq, ctx_len):
    nt = seq // TQ_WINDOW
    nsub = TQ_WINDOW // CHUNK
    nb = seq // CHUNK
    ctx0 = (batch * seq) // ctx_len
    prev = lambda b, n: b * nb + jnp.maximum(n * nsub - 1, 0)
    nxt = lambda b, n: b * nb + jnp.minimum((n + 1) * nsub, nb - 1)
    return pl.pallas_call(
        functools.partial(_window_attn_kernel, n_tiles=nt),
        grid=(batch, nt),
        in_specs=[pl.BlockSpec(memory_space=pltpu.SMEM),
                  pl.BlockSpec((TQ_WINDOW, 256), lambda b, n: (b * nt + n, 0)),
                  pl.BlockSpec((LANES, ctx_len), lambda b, n: (0, ctx0 + b)),
                  pl.BlockSpec((ctx_len, LANES), lambda b, n: (ctx0 + b, 0)),
                  pl.BlockSpec((LANES, CHUNK), lambda b, n: (0, prev(b, n))),
                  pl.BlockSpec((CHUNK, LANES), lambda b, n: (prev(b, n), 0)),
                  pl.BlockSpec((LANES, TQ_WINDOW), lambda b, n: (0, b * nt + n)),
                  pl.BlockSpec((TQ_WINDOW, LANES), lambda b, n: (b * nt + n, 0)),
                  pl.BlockSpec((LANES, CHUNK), lambda b, n: (0, nxt(b, n))),
                  pl.BlockSpec((CHUNK, LANES), lambda b, n: (nxt(b, n), 0))],
        out_specs=pl.BlockSpec((TQ_WINDOW, 256), lambda b, n: (b * nt + n, 0)),
        out_shape=jax.ShapeDtypeStruct((batch * seq, 256), BF16),
        compiler_params=_params(("parallel", "parallel")),
        name="window_attn",
    )(sink, q, kt, v, kt, v, kt, v, kt, v)


def _gelu_tanh(x):
    return 0.5 * x * (1.0 + jnp.tanh(math.sqrt(2.0 / math.pi) * (x + 0.044715 * (x * x * x))))


def _outproj_kernel(x_ref, gate_ref, gpost_ref, oa_ref, od_ref, yf_ref, yb_ref, xs_ref, z_ref, dsk_ref, gn_ref,
                    hf_ref, hb_ref, lg_ref, w_ref, o_ref):
    y_ssd = (yf_ref[...] + yb_ref[...] + xs_ref[...] * dsk_ref[...]) * _silu(z_ref[...])
    ob = _rms(y_ssd, gn_ref[...])
    oc = (hf_ref[...] + hb_ref[...]) * _gelu_tanh(lg_ref[...])
    y = (_dot(oa_ref[...], w_ref[0:256, :]) + _dot(ob.astype(BF16), w_ref[256:512, :])
         + _dot(oc.astype(BF16), w_ref[512:768, :]) + _dot(od_ref[...], w_ref[768:1024, :]))
    o_ref[...] = x_ref[...] + gate_ref[0] * _rms(y, gpost_ref[...])


def _outproj(xu, mod, gpost, oa, od, yf, yb, xbc, z, dsk, gn, hf, hb, lg, w_out, *, n_rows, n_lat, seq, batch):
    d = xu.shape[1]
    row = lambda w: pl.BlockSpec((TM, w), lambda i: (i, 0))
    const = lambda a: pl.BlockSpec(a.shape, lambda i: (0,) * a.ndim)
    return pl.pallas_call(
        _outproj_kernel,
        grid=(n_rows // TM,),
        in_specs=[row(d), pl.BlockSpec((1, 1, d), _mod_spec(2, n_lat, seq, batch, TM)), const(gpost),
                  row(256), row(256), row(256), row(256), row(256), row(256), const(dsk), const(gn),
                  row(256), row(256), row(256), const(w_out)],
        out_specs=row(d),
        out_shape=jax.ShapeDtypeStruct((n_rows, d), F32),
        compiler_params=_params(("parallel",)),
        name="outproj",
    )(xu, mod, gpost, oa, od, yf, yb, xbc, z, dsk, gn, hf, hb, lg, w_out)


def _ceil_seg(c):
    return jnp.floor((c + (SEG_ALIGN - 1)) * (1.0 / SEG_ALIGN)) * SEG_ALIGN


def _router_kernel(x_ref, shift_ref, scale_ref, gpre_ref, rwt_ref, rb_ref, hb_ref, ld_ref, wk_ref, tab_ref):
    h = _rms(x_ref[...], gpre_ref[...])
    h = h * (1.0 + scale_ref[0]) + shift_ref[0]
    hb = h.astype(BF16)
    hb_ref[...] = hb

    scores = jax.nn.sigmoid(_dot_nt(rwt_ref[...], hb))
    biased = scores + rb_ref[...]
    gsz = N_EXPERTS // N_EXPERT_GROUPS
    sub = lax.broadcasted_iota(jnp.int32, (gsz, TM), 0)
    blocks, gscore = [], []
    for g in range(N_EXPERT_GROUPS):
        blk = biased[g * gsz:(g + 1) * gsz, :]
        m1 = jnp.max(blk, axis=0, keepdims=True)
        first = jnp.min(jnp.where(blk == m1, sub, gsz), axis=0, keepdims=True)
        m2 = jnp.max(jnp.where(sub == first, -jnp.inf, blk), axis=0, keepdims=True)
        blocks.append(blk)
        gscore.append(m1 + m2)
    masked = []
    for g in range(N_EXPERT_GROUPS):
        rank = jnp.zeros((1, TM), F32)
        for g2 in range(N_EXPERT_GROUPS):
            if g2 == g:
                continue
            beats = (gscore[g2] > gscore[g]) | ((gscore[g2] == gscore[g]) if g2 < g else False)
            rank = rank + jnp.where(beats, 1.0, 0.0)
        masked.append(jnp.where(rank < TOPK_GROUPS, blocks[g], -jnp.inf))
    vals = jnp.concatenate(masked, axis=0)
    eidx = lax.broadcasted_iota(jnp.int32, (N_EXPERTS, TM), 0)
    rank = jnp.zeros((N_EXPERTS, TM), F32)
    for e2 in range(N_EXPERTS):
        rowv = vals[e2:e2 + 1, :]
        beats = (rowv > vals) | ((rowv == vals) & (eidx > e2))
        rank = rank + jnp.where(beats, 1.0, 0.0)
    sel = rank < TOP_K
    self32 = jnp.where(sel, 1.0, 0.0)
    picked = jnp.where(sel, scores, 0.0)
    wdense = picked / jnp.sum(picked, axis=0, keepdims=True) * ROUTED_SCALE

    tr = lax.broadcasted_iota(jnp.int32, (TM, TM), 0)
    tc = lax.broadcasted_iota(jnp.int32, (TM, TM), 1)
    before = jnp.where(tr < tc, 1.0, 0.0).astype(BF16)
    selb = self32.astype(BF16)
    pos = _dot(selb, before)
    er = lax.broadcasted_iota(jnp.int32, (N_EXPERTS, N_EXPERTS), 0)
    ec = lax.broadcasted_iota(jnp.int32, (N_EXPERTS, N_EXPERTS), 1)
    lower = jnp.where(ec < er, 1.0, 0.0).astype(BF16)
    upper = jnp.where(er < ec, 1.0, 0.0).astype(BF16)
    ksel = _dot(lower, selb)
    cnt_col = _ceil_seg(jnp.sum(self32, axis=1, keepdims=True))
    loc_col = _dot3_left(lower, jnp.broadcast_to(cnt_col, (N_EXPERTS, LANES)))[:, 0:1]
    cnt_row = _ceil_seg(_dot_nt(jnp.ones((SUBLANES, TM), BF16), selb))
    loc_row = _dot3(cnt_row, upper)
    tab_ref[...] = jnp.concatenate([cnt_row, loc_row], axis=1).astype(jnp.int32)

    r8 = lax.broadcasted_iota(jnp.int32, (TOP_K, TM), 0)
    ld = jnp.zeros((TOP_K, TM), F32)
    wk = jnp.zeros((TOP_K, TM), F32)
    stage_row = pos + loc_col
    for k in range(TOP_K):
        one = sel & (ksel == float(k))
        ld = jnp.where(r8 == k, jnp.sum(jnp.where(one, stage_row, 0.0), axis=0, keepdims=True), ld)
        wk = jnp.where(r8 == k, jnp.sum(jnp.where(one, wdense, 0.0), axis=0, keepdims=True), wk)
    ld_ref[...] = ld.astype(jnp.int32)
    wk_ref[...] = wk


def _router(xu, mod, gpre, rwt, rb, *, n_rows, n_lat, seq, batch):
    d = xu.shape[1]
    row = lambda w: pl.BlockSpec((TM, w), lambda i: (i, 0))
    col = pl.BlockSpec((TOP_K, TM), lambda i: (0, i))
    const = lambda a: pl.BlockSpec(a.shape, lambda i: (0,) * a.ndim)
    return pl.pallas_call(
        _router_kernel,
        grid=(n_rows // TM,),
        in_specs=[row(d), pl.BlockSpec((1, 1, d), _mod_spec(3, n_lat, seq, batch, TM)),
                  pl.BlockSpec((1, 1, d), _mod_spec(4, n_lat, seq, batch, TM)),
                  const(gpre), const(rwt), const(rb)],
        out_specs=(row(d), col, col, pl.BlockSpec((SUBLANES, 2 * N_EXPERTS), lambda i: (i, 0))),
        out_shape=(jax.ShapeDtypeStruct((n_rows, d), BF16),
                   jax.ShapeDtypeStruct((TOP_K, n_rows), jnp.int32),
                   jax.ShapeDtypeStruct((TOP_K, n_rows), F32),
                   jax.ShapeDtypeStruct((n_rows // TM * SUBLANES, 2 * N_EXPERTS), jnp.int32)),
        compiler_params=_params(("parallel",)),
        name="router",
    )(xu, mod, mod, gpre, rwt, rb)


def _pow2_pieces(limit):
    bits, b = [], limit
    while b >= SEG_ALIGN:
        bits.append(b)
        b //= 2
    return bits


def _copy_pieces(n, src_ref, src0, dst_ref, dst0, sem, limit, wait, same_src=False):
    for bit in _pow2_pieces(limit):
        @pl.when((n & bit) != 0)
        def _():
            off = n & ~(2 * bit - 1)
            cp = pltpu.make_async_copy(src_ref.at[pl.ds(pl.multiple_of(src0 + (0 if same_src else off), SEG_ALIGN),
                                                          bit)],
                                       dst_ref.at[pl.ds(pl.multiple_of(dst0 + off, SEG_ALIGN), bit)], sem)
            cp.wait() if wait else cp.start()


N_PIECE_TABS = 7


def _piece_copies(tile, tabs, stage_ref, slots_ref, sem, to_slots, wait):
    nbig_ref, nsmall_ref, _, bsrc_ref, bdst_ref, ssrc_ref, sdst_ref = tabs
    for rows, n_ref, a_ref, b_ref, cap in ((BIG_PIECE, nbig_ref, bsrc_ref, bdst_ref, BIG_MAX),
                                           (SEG_ALIGN, nsmall_ref, ssrc_ref, sdst_ref, SMALL_MAX)):
        def body(p, c, rows=rows, a_ref=a_ref, b_ref=b_ref, cap=cap):
            src = stage_ref.at[pl.ds(pl.multiple_of(a_ref[tile * cap + p], SEG_ALIGN), rows)]
            dst = slots_ref.at[pl.ds(pl.multiple_of(b_ref[tile * cap + p], SEG_ALIGN), rows)]
            cp = pltpu.make_async_copy(src, dst, sem) if to_slots else pltpu.make_async_copy(dst, src, sem)
            cp.wait() if wait else cp.start()
            return c
        lax.fori_loop(0, n_ref[tile], body, 0)


def _used_blocks(tile, tabs):
    return (tabs[2][tile] + TM - 1) // TM


def _stage_rows_iota():
    return lax.broadcasted_iota(jnp.int32, (TM // 2, TM), 0).astype(F32).astype(BF16)


def _pick_matrix(ld, base, vals, jrow):
    rel = (ld - base).astype(F32)
    rel = jnp.where(jnp.logical_and(rel >= 0.0, rel < TM // 2), rel, -1.0).astype(BF16)
    out = jnp.zeros((TM // 2, TM), BF16)
    for k in range(TOP_K):
        out = jnp.where(rel[k:k + 1, :] == jrow, vals[k:k + 1, :], out)
    return out


def _dispatch_kernel(*refs):
    tabs = refs[:N_PIECE_TABS]
    pstart_ref, npad_ref, hb_ref, ld_ref, xs_ref, stage, zbuf, sem, zsem = refs[N_PIECE_TABS:]
    i = pl.program_id(0)

    @pl.when(i == 0)
    def _():
        zbuf[...] = jnp.zeros_like(zbuf)
        for wait in (False, True):
            def body(e, c, wait=wait):
                _copy_pieces(npad_ref[e], zbuf, 0, xs_ref, pstart_ref[e], zsem, BM_EXPERT // 2, wait, same_src=True)
                return c
            lax.fori_loop(0, N_EXPERTS, body, 0)

    ld = ld_ref[...]
    hb = hb_ref[...]
    jrow = _stage_rows_iota()
    ones = jnp.ones((TOP_K, TM), BF16)

    def block(b, c):
        for half in range(2):
            base = pl.multiple_of(b * TM + half * (TM // 2), TM // 2)
            stage[pl.ds(base, TM // 2), :] = _dot(_pick_matrix(ld, base, ones, jrow), hb).astype(BF16)
        return c

    lax.fori_loop(0, _used_blocks(i, tabs), block, 0)
    _piece_copies(i, tabs, stage, xs_ref, sem, True, False)
    _piece_copies(i, tabs, stage, xs_ref, sem, True, True)


def _dispatch(tabs, pad_start, n_pad, hb, ld, n_slots):
    n, d = hb.shape
    grid_spec = pltpu.PrefetchScalarGridSpec(
        num_scalar_prefetch=N_PIECE_TABS + 2,
        grid=(n // TM,),
        in_specs=[pl.BlockSpec((TM, d), lambda i, *_: (i, 0)),
                  pl.BlockSpec((TOP_K, TM), lambda i, *_: (0, i))],
        out_specs=pl.BlockSpec(memory_space=pl.ANY),
        scratch_shapes=[pltpu.VMEM((STAGE_ROWS, d), BF16), pltpu.VMEM((BM_EXPERT // 2, d), BF16),
                        pltpu.SemaphoreType.DMA(()), pltpu.SemaphoreType.DMA(())],
    )
    return pl.pallas_call(
        _dispatch_kernel,
        grid_spec=grid_spec,
        out_shape=jax.ShapeDtypeStruct((n_slots, d), BF16),
        compiler_params=_params(("arbitrary",)),
        name="moe_dispatch",
    )(*tabs, pad_start, n_pad, hb, ld)


def _expert_kernel(be_ref, na_ref, nxt_ref, slot_ref, xs_ref, wg_hbm, wu_hbm, wd_hbm, ys_ref,
                   wg_raw, wu_raw, wd_raw, wgb, wub, wdb, wsem, *, layer):
    i = pl.program_id(0)

    def weight_copies(e, slot):
        return [pltpu.make_async_copy(src.at[layer, e], dst.at[slot], wsem.at[slot, j])
                for j, (src, dst) in enumerate(((wg_hbm, wg_raw), (wu_hbm, wu_raw), (wd_hbm, wd_raw)))]

    @pl.when(i < na_ref[0])
    def _():
        e, slot = be_ref[i], slot_ref[i]

        @pl.when(i == 0)
        def _():
            for cp in weight_copies(e, slot):
                cp.start()

        @pl.when(jnp.logical_or(i == 0, e != be_ref[jnp.maximum(i - 1, 0)]))
        def _():
            for cp in weight_copies(e, slot):
                cp.wait()
            wgb[...] = wg_raw[slot].astype(BF16)
            wub[...] = wu_raw[slot].astype(BF16)
            wdb[...] = wd_raw[slot].astype(BF16)

            @pl.when(nxt_ref[i] != e)
            def _():
                for cp in weight_copies(nxt_ref[i], 1 - slot):
                    cp.start()

        half = BM_EXPERT // 2
        for r in range(2):
            xb = xs_ref[r * half:(r + 1) * half, :]
            hid = _silu(_dot(xb, wgb[...])) * _dot(xb, wub[...])
            ys_ref[r * half:(r + 1) * half, :] = _dot(hid.astype(BF16), wdb[...]).astype(ys_ref.dtype)


def _experts(block_e, n_active, next_e, w_slot, xs, wg, wu, wd, layer):
    n_slots, d = xs.shape
    nb = n_slots // BM_EXPERT
    tiles = pl.BlockSpec((BM_EXPERT, d), lambda i, be, na, *_: (jnp.minimum(i, na[0] - 1), 0))
    anywhere = pl.BlockSpec(memory_space=pl.ANY)
    grid_spec = pltpu.PrefetchScalarGridSpec(
        num_scalar_prefetch=4,
        grid=(nb,),
        in_specs=[tiles, anywhere, anywhere, anywhere],
        out_specs=tiles,
        scratch_shapes=[pltpu.VMEM((2, d, EXPERT_HIDDEN), F32), pltpu.VMEM((2, d, EXPERT_HIDDEN), F32),
                        pltpu.VMEM((2, EXPERT_HIDDEN, d), F32),
                        pltpu.VMEM((d, EXPERT_HIDDEN), BF16), pltpu.VMEM((d, EXPERT_HIDDEN), BF16),
                        pltpu.VMEM((EXPERT_HIDDEN, d), BF16), pltpu.SemaphoreType.DMA((2, 3))],
    )
    return pl.pallas_call(
        functools.partial(_expert_kernel, layer=layer),
        grid_spec=grid_spec,
        out_shape=jax.ShapeDtypeStruct((n_slots, d), BF16),
        compiler_params=_params(("arbitrary",)),
        name="moe_experts",
    )(block_e, n_active, next_e, w_slot, xs, wg, wu, wd)


def _combine_kernel(*refs):
    tabs = refs[:N_PIECE_TABS]
    (ys_ref, ld_ref, wk_ref, hb_ref, x_ref, gate_ref, gpost_ref, sg_ref, su_ref, sd_ref, o_ref,
     stage, acc_ref, sem) = refs[N_PIECE_TABS:]
    i = pl.program_id(0)

    @pl.when(i == 0)
    def _():
        stage[...] = jnp.zeros_like(stage)

    _piece_copies(i, tabs, stage, ys_ref, sem, False, False)
    hb = hb_ref[...]
    acc_ref[...] = _dot((_silu(_dot(hb, sg_ref[...])) * _dot(hb, su_ref[...])).astype(BF16), sd_ref[...])
    _piece_copies(i, tabs, stage, ys_ref, sem, False, True)

    ld = ld_ref[...]
    wkb = wk_ref[...].astype(BF16)
    jrow = _stage_rows_iota()

    def block(b, c):
        for half in range(2):
            base = pl.multiple_of(b * TM + half * (TM // 2), TM // 2)
            weights = _pick_matrix(ld, base, wkb, jrow)
            acc_ref[...] += lax.dot_general(weights, stage[pl.ds(base, TM // 2), :], (((0,), (0,)), ((), ())),
                                            preferred_element_type=F32)
        return c

    lax.fori_loop(0, _used_blocks(i, tabs), block, 0)
    o_ref[...] = x_ref[...] + gate_ref[0] * _rms(acc_ref[...], gpost_ref[...])


def _combine(tabs, ys, ld, wk, hb, xu, mod, gpost, sg, su, sd, *, n_rows, n_lat, seq, batch):
    d = xu.shape[1]
    row = lambda w: pl.BlockSpec((TM, w), lambda i, *_: (i, 0))
    col = pl.BlockSpec((TOP_K, TM), lambda i, *_: (0, i))
    const = lambda a: pl.BlockSpec(a.shape, lambda i, *_: (0,) * a.ndim)
    mod_map = _mod_spec(5, n_lat, seq, batch, TM)
    grid_spec = pltpu.PrefetchScalarGridSpec(
        num_scalar_prefetch=N_PIECE_TABS,
        grid=(n_rows // TM,),
        in_specs=[pl.BlockSpec(memory_space=pl.ANY), col, col, row(d), row(d),
                  pl.BlockSpec((1, 1, d), lambda i, *_: mod_map(i)),
                  const(gpost), const(sg), const(su), const(sd)],
        out_specs=row(d),
        scratch_shapes=[pltpu.VMEM((STAGE_ROWS, d), BF16), pltpu.VMEM((TM, d), F32), pltpu.SemaphoreType.DMA(())],
    )
    return pl.pallas_call(
        _combine_kernel,
        grid_spec=grid_spec,
        out_shape=jax.ShapeDtypeStruct((n_rows, d), F32),
        compiler_params=_params(("arbitrary",)),
        name="moe_combine",
    )(*tabs, ys, ld, wk, hb, xu, mod, gpost, sg, su, sd)


def _deinterleave(w):
    cols = w.shape[-1]
    perm = jnp.concatenate([jnp.arange(0, HEAD_DIM, 2), jnp.arange(1, HEAD_DIM, 2)])
    idx = (jnp.arange(cols // HEAD_DIM)[:, None] * HEAD_DIM + perm[None, :]).reshape(-1)
    return w[..., idx]


def _pad_in_proj(w_in):
    d = w_in.shape[0]
    o = 0
    parts = {}
    for name, width in (("qa", 256), ("ka", 128), ("va", 128), ("z", 256), ("xs", 256), ("bm", 128), ("cm", 128),
                        ("dtf", 4), ("dtb", 4), ("lx", 256), ("lg", 256), ("qd", 256), ("kd", 128), ("vd", 128)):
        parts[name] = w_in[:, o:o + width]
        o += width
    dt = jnp.concatenate([parts["dtf"], parts["dtb"], jnp.zeros((d, LANES - 8), w_in.dtype)], axis=1)
    cols = [_deinterleave(parts["qa"]), _deinterleave(parts["ka"]), parts["va"],
            _deinterleave(parts["qd"]), _deinterleave(parts["kd"]), parts["vd"],
            parts["z"], parts["xs"], parts["bm"], parts["cm"], dt, parts["lx"], parts["lg"]]
    return jnp.concatenate(cols, axis=1).astype(BF16)


def _rope_tables(seq):
    t = jnp.arange(seq)
    rowp = (t // GRID_W).astype(F32)
    colp = (t % GRID_W).astype(F32)
    axis_dim = HEAD_DIM // 2
    inv_freq = ROPE_THETA ** (-jnp.arange(0, axis_dim, 2, dtype=F32) / axis_dim)
    ang = jnp.concatenate([rowp[:, None] * inv_freq, colp[:, None] * inv_freq], axis=-1)
    cos, sin = jnp.cos(ang), jnp.sin(ang)
    cos_h = jnp.concatenate([cos, cos], axis=-1)
    sin_h = jnp.concatenate([-sin, sin], axis=-1)
    return jnp.tile(cos_h, (1, 4)), jnp.tile(sin_h, (1, 4))


def _block_diag(w):
    nb, bd, _ = w.shape
    eye = jnp.eye(nb, dtype=w.dtype)
    return (eye[:, None, :, None] * w[:, :, None, :]).reshape(nb * bd, nb * bd)


def _piece_table(counts, cap, stage0, slot0, rows, ids):
    ends = jnp.cumsum(counts, axis=1)
    q = jnp.arange(cap, dtype=jnp.int32)
    owner = jnp.sum((ends[:, None, :] <= q[None, :, None]).astype(jnp.int32), axis=-1)
    mine = owner[:, :, None] == ids
    pick = lambda v: jnp.sum(jnp.where(mine, v[:, None, :], 0), axis=-1)
    step = rows * (q[None, :] - pick(ends - counts))
    return (pick(stage0) + step).reshape(-1), (pick(slot0) + step).reshape(-1)


def _lane_row(fwd, bwd):
    return jnp.concatenate([fwd, bwd, jnp.zeros((LANES - 8,), F32)]).reshape(1, LANES)


def kernel(x, c, ctx, c_ctx, w_ada, b_ada, g_mix_pre, g_mix_post, g_ffn_pre, g_ffn_post, w_in, w_out, a_sink,
           ssd_conv_w, ssd_conv_b, ssd_dt_bias, ssd_a_log, ssd_d, ssd_norm, lru_conv_w, lru_conv_b, lru_w_a,
           lru_b_a, lru_w_i, lru_b_i, lru_lambda, d_q_norm, d_k_norm, router_w, router_bias, exp_w_gate,
           exp_w_up, exp_w_down, sh_w_gate, sh_w_up, sh_w_down):
    batch, seq, d = x.shape
    ctx_len = ctx.shape[1]
    depth = w_ada.shape[0]
    n_lat = batch * seq
    n_ctx = batch * ctx_len
    n_all = n_lat + n_ctx
    assert seq % TM == 0 and n_ctx % TM == 0 and seq % T_CONV == 0 and ctx_len % T_CONV == 0
    assert seq % TQ_WINDOW == 0 and seq % (SCAN_CHUNKS * CHUNK) == 0 and ctx_len % (SCAN_CHUNKS * CHUNK) == 0
    assert ctx_len <= KV_CHUNK and seq % KV_CHUNK == 0 and seq % TQ_GLOBAL == 0 and batch + 1 <= SUBLANES

    xu = jnp.concatenate([x.reshape(n_lat, d), ctx.reshape(n_ctx, d)], axis=0)
    cin = jnp.concatenate([c, c_ctx[None, :], jnp.zeros((SUBLANES - batch - 1, d), F32)], axis=0)
    mod_all = _adaln(cin, w_ada, b_ada)
    cos_t, sin_t = _rope_tables(seq)
    hm = jnp.kron(jnp.eye(4, dtype=F32), jnp.full((HEAD_DIM, HEAD_DIM), 1.0 / HEAD_DIM, F32)).astype(BF16)

    for l in range(depth):
        with_ctx = l < depth - 1
        mod = mod_all[l].reshape(SUBLANES * 6, 1, d)
        gq = jnp.tile(_deinterleave(d_q_norm[l]), 4).reshape(1, 256)
        gk = jnp.tile(_deinterleave(d_k_norm[l]), 2).reshape(1, LANES)
        qa, kat, va, qd, kdt, vd, z, xbc_raw, dt, lx_raw, lg = _inproj(
            xu, mod, g_mix_pre[l].reshape(1, d), _pad_in_proj(w_in[l]), cos_t, sin_t, gq, gk, hm,
            n_lat=n_lat, seq=seq, batch=batch)

        xbc, lu = _conv(xbc_raw, lx_raw, ssd_conv_w[l], ssd_conv_b[l].reshape(1, -1),
                        lru_conv_w[l], lru_conv_b[l].reshape(1, -1), n_lat=n_lat, seq=seq, ctx_len=ctx_len)
        yf, yb = _ssd(xbc, dt, _lane_row(ssd_dt_bias[l, 0], ssd_dt_bias[l, 1]),
                      _lane_row(ssd_a_log[l, 0], ssd_a_log[l, 1]), batch=batch, seq=seq, ctx_len=ctx_len)
        wg = jnp.stack([jnp.concatenate([_block_diag(lru_w_a[l, dd]), _block_diag(lru_w_i[l, dd])], axis=1)
                        for dd in range(2)]).astype(BF16)
        bg = jnp.concatenate([lru_b_a[l], lru_b_i[l]], axis=1).reshape(2, 1, 2 * LRU_WIDTH)
        hf, hb = _lru(lu, wg, bg, lru_lambda[l].reshape(2, 1, LRU_WIDTH), batch=batch, seq=seq, ctx_len=ctx_len)

        oa = _window_attn(qa, kat, va, a_sink[l], batch=batch, seq=seq, ctx_len=ctx_len)
        od = _dense_attn(qd, kdt, vd, None, q_row0=0, q_len=seq, tq=TQ_GLOBAL,
                         segs=[(n_lat, ctx_len), (0, seq)], batch=batch)
        if with_ctx:
            oa_c = _dense_attn(qa, kat, va, a_sink[l], q_row0=n_lat, q_len=ctx_len, tq=ctx_len,
                               segs=[(n_lat, ctx_len)], batch=batch)
            od_c = _dense_attn(qd, kdt, vd, None, q_row0=n_lat, q_len=ctx_len, tq=ctx_len,
                               segs=[(n_lat, ctx_len)], batch=batch)
            oa = jnp.concatenate([oa, oa_c], axis=0)
            od = jnp.concatenate([od, od_c], axis=0)
        n_rows = n_all if with_ctx else n_lat

        dsk = jnp.repeat(ssd_d[l], HEAD_DIM).reshape(1, 256)
        xu_mid = _outproj(xu, mod, g_mix_post[l].reshape(1, d), oa, od, yf, yb, xbc, z, dsk,
                          ssd_norm[l].reshape(1, 256), hf, hb, lg, w_out[l].astype(BF16),
                          n_rows=n_rows, n_lat=n_lat, seq=seq, batch=batch)

        hb_ffn, ld, wk, tab = _router(xu_mid, mod, g_ffn_pre[l].reshape(1, d), router_w[l].T.astype(BF16),
                                      router_bias[l].reshape(N_EXPERTS, 1), n_rows=n_rows, n_lat=n_lat,
                                      seq=seq, batch=batch)
        n_tiles = n_rows // TM
        tab = tab.reshape(n_tiles, SUBLANES, 2 * N_EXPERTS)[:, 0, :]
        seg_cnt, seg_loc = tab[:, :N_EXPERTS], tab[:, N_EXPERTS:]
        counts = jnp.sum(seg_cnt, axis=0)
        padded = (counts + BM_EXPERT - 1) // BM_EXPERT * BM_EXPERT
        padded_end = jnp.cumsum(padded)
        offs = padded_end - padded
        seg_off = offs[None, :] + jnp.cumsum(seg_cnt, axis=0) - seg_cnt
        n_blocks = (n_rows * TOP_K + n_tiles * N_EXPERTS * SEG_ALIGN) // BM_EXPERT + N_EXPERTS
        n_active = (padded_end[-1] // BM_EXPERT).astype(jnp.int32).reshape(1)
        block_start = jnp.arange(n_blocks, dtype=jnp.int32) * BM_EXPERT
        block_e = jnp.minimum(jnp.sum((padded_end[None, :] <= block_start[:, None]).astype(jnp.int32), axis=1),
                              N_EXPERTS - 1)
        ids = jnp.arange(N_EXPERTS, dtype=jnp.int32)
        n_big = seg_cnt // BIG_PIECE
        n_small = (seg_cnt % BIG_PIECE) // SEG_ALIGN
        tabs = (jnp.sum(n_big, axis=1), jnp.sum(n_small, axis=1), seg_loc[:, -1] + seg_cnt[:, -1],
                *_piece_table(n_big, BIG_MAX, seg_loc, seg_off, BIG_PIECE, ids),
                *_piece_table(n_small, SMALL_MAX, seg_loc + n_big * BIG_PIECE, seg_off + n_big * BIG_PIECE,
                              SEG_ALIGN, ids))
        xs = _dispatch(tabs, offs + counts, padded - counts, hb_ffn, ld, n_blocks * BM_EXPERT)
        has_rows = padded > 0
        later = jnp.logical_and(ids[None, :] > ids[:, None], has_rows[None, :])
        nxt_of = jnp.min(jnp.where(later, ids[None, :], N_EXPERTS), axis=1)
        nxt_of = jnp.where(nxt_of == N_EXPERTS, ids, nxt_of)
        slot_of = (jnp.cumsum(has_rows.astype(jnp.int32)) - 1) & 1
        own = block_e[:, None] == ids[None, :]
        next_e = jnp.sum(jnp.where(own, nxt_of[None, :], 0), axis=1)
        w_slot = jnp.sum(jnp.where(own, slot_of[None, :], 0), axis=1)
        ys = _experts(block_e, n_active, next_e, w_slot, xs, exp_w_gate, exp_w_up, exp_w_down, l)
        xu = _combine(tabs, ys, ld, wk, hb_ffn, xu_mid, mod, g_ffn_post[l].reshape(1, d), sh_w_gate[l].astype(BF16),
                      sh_w_up[l].astype(BF16), sh_w_down[l].astype(BF16),
                      n_rows=n_rows, n_lat=n_lat, seq=seq, batch=batch)
    return xu[:n_lat].reshape(batch, seq, d)
```

```python
import functools
import math

import jax
import jax.numpy as jnp
from jax import lax
from jax.experimental import pallas as pl
from jax.experimental.pallas import tpu as pltpu

F32 = jnp.float32
BF16 = jnp.bfloat16

HEAD_DIM = 64
GRID_W = 64
ROPE_THETA = 10000.0
NORM_EPS = 1e-6
NEG_INF = -1e30
A_HEADS, A_KV_HEADS, WINDOW = 4, 2, 128
SSD_HEADS, SSD_GROUPS, SSD_STATE, SSD_CONV = 4, 2, 64, 4
LRU_WIDTH, LRU_BLOCKS, LRU_CONV, LRU_C = 256, 4, 4, 8.0
D_HEADS, D_KV_HEADS = 4, 2
N_EXPERTS, N_EXPERT_GROUPS, TOPK_GROUPS, TOP_K = 64, 8, 4, 8
EXPERT_HIDDEN, SHARED_HIDDEN = 256, 256
ROUTED_SCALE = 2.5

LANES = 128
SUBLANES = 8

TM = 512
T_CONV = 256
CHUNK = 128
SCAN_CHUNKS = 2
TQ_GLOBAL = 256
TQ_WINDOW = 512
KV_CHUNK = 512
BM_EXPERT = 512
SEG_ALIGN = 16
STAGE_ROWS = TM * TOP_K + N_EXPERTS * SEG_ALIGN
BIG_PIECE = 64
BIG_MAX = STAGE_ROWS // BIG_PIECE
SMALL_MAX = N_EXPERTS * (BIG_PIECE // SEG_ALIGN - 1)
VMEM_LIMIT = 48 * 1024 * 1024

C_QA, C_KA, C_VA = 0, 256, 384
C_QD, C_KD, C_VD = 512, 768, 896
C_Z, C_XBC, C_DT = 1024, 1280, 1792
C_LX, C_LG = 1920, 2176
NP_IN = 2432


def _dot(a, b):
    return jnp.dot(a, b, preferred_element_type=F32)


def _dot_nt(a, b):
    return lax.dot_general(a, b, (((1,), (1,)), ((), ())), preferred_element_type=F32)


def _dot3(a, b):
    a1 = a.astype(BF16)
    r1 = a - a1.astype(F32)
    a2 = r1.astype(BF16)
    a3 = (r1 - a2.astype(F32)).astype(BF16)
    return _dot(a1, b) + _dot(a2, b) + _dot(a3, b)


def _dot3_left(a, b):
    b1 = b.astype(BF16)
    r1 = b - b1.astype(F32)
    b2 = r1.astype(BF16)
    b3 = (r1 - b2.astype(F32)).astype(BF16)
    return _dot(a, b1) + _dot(a, b2) + _dot(a, b3)


def _silu(x):
    return x * jax.nn.sigmoid(x)


def _softplus(x):
    return jnp.maximum(x, 0.0) + jnp.log1p(jnp.exp(-jnp.abs(x)))


def _rms(x, gain):
    return x * lax.rsqrt(jnp.mean(x * x, axis=-1, keepdims=True) + NORM_EPS) * gain


def _params(sem=None):
    return pltpu.CompilerParams(dimension_semantics=sem, vmem_limit_bytes=VMEM_LIMIT)


def _adaln_kernel(c_ref, w_ref, b_ref, o_ref):
    s = _silu(c_ref[...])
    o_ref[0] = _dot(s.astype(BF16), w_ref[0].astype(BF16)) + b_ref[0]


def _adaln(cin, w_ada, b_ada):
    depth, d, n6 = w_ada.shape
    tn = 1024
    return pl.pallas_call(
        _adaln_kernel,
        grid=(depth, n6 // tn),
        in_specs=[pl.BlockSpec((SUBLANES, d), lambda l, j: (0, 0)),
                  pl.BlockSpec((1, d, tn), lambda l, j: (l, 0, j)),
                  pl.BlockSpec((1, 1, tn), lambda l, j: (l, 0, j))],
        out_specs=pl.BlockSpec((1, SUBLANES, tn), lambda l, j: (l, 0, j)),
        out_shape=jax.ShapeDtypeStruct((depth, SUBLANES, n6), F32),
        compiler_params=_params(("parallel", "parallel")),
        name="adaln",
    )(cin, w_ada, b_ada.reshape(depth, 1, n6))


def _swap_halves(t):
    w = t.shape[1]
    lane = lax.broadcasted_iota(jnp.int32, (1, w), 1)
    first = (lane & 32) == 0
    return jnp.where(first, pltpu.roll(t, w - 32, axis=1), pltpu.roll(t, 32, axis=1))


def _inproj_kernel(x_ref, shift_ref, scale_ref, gpre_ref, w_ref, cos_ref, sin_ref, gq_ref, gk_ref, hm_ref,
                   qa_ref, kat_ref, va_ref, qd_ref, kdt_ref, vd_ref, z_ref, xbc_ref, dt_ref, lx_ref, lg_ref,
                   *, n_lat):
    i = pl.program_id(0)
    is_lat = i * TM < n_lat
    h = _rms(x_ref[...], gpre_ref[...])
    h = h * (1.0 + scale_ref[0]) + shift_ref[0]
    hb = h.astype(BF16)

    def sec(a, b):
        return _dot(hb, w_ref[:, a:b])

    cos = jnp.where(is_lat, cos_ref[...], 1.0)
    sin = jnp.where(is_lat, sin_ref[...], 0.0)

    def rope(t):
        w = t.shape[1]
        return t * cos[:, :w] + _swap_halves(t) * sin[:, :w]

    def head_norm(t, gain):
        w = t.shape[1]
        ms = _dot3(t * t, hm_ref[:w, :w])
        return t * lax.rsqrt(ms + NORM_EPS) * gain

    scale = HEAD_DIM ** -0.5
    qa_ref[...] = (rope(sec(C_QA, C_KA)) * scale).astype(BF16)
    kat_ref[...] = rope(sec(C_KA, C_VA)).T.astype(BF16)
    va_ref[...] = sec(C_VA, C_QD).astype(BF16)
    qd_ref[...] = (rope(head_norm(sec(C_QD, C_KD), gq_ref[...])) * scale).astype(BF16)
    kdt_ref[...] = rope(head_norm(sec(C_KD, C_VD), gk_ref[...])).T.astype(BF16)
    vd_ref[...] = sec(C_VD, C_Z).astype(BF16)
    z_ref[...] = sec(C_Z, C_XBC)
    xbc_ref[...] = sec(C_XBC, C_DT)
    dt_ref[...] = sec(C_DT, C_LX)
    lx_ref[...] = sec(C_LX, C_LG)
    lg_ref[...] = sec(C_LG, NP_IN)


def _mod_spec(chunk, n_lat, seq, batch, tile):
    def imap(i):
        row0 = i * tile
        seg = jnp.where(row0 < n_lat, row0 // seq, batch)
        return (seg * 6 + chunk, 0, 0)
    return imap


def _inproj(xu, mod, gpre, w_pad, cos_t, sin_t, gq, gk, hm, *, n_lat, seq, batch):
    n, d = xu.shape
    nt = n // TM
    spt = seq // TM
    row = lambda w: pl.BlockSpec((TM, w), lambda i: (i, 0))
    colT = pl.BlockSpec((LANES, TM), lambda i: (0, i))
    const = lambda a: pl.BlockSpec(a.shape, lambda i: (0,) * a.ndim)
    out_shapes = (
        jax.ShapeDtypeStruct((n, 256), BF16), jax.ShapeDtypeStruct((LANES, n), BF16),
        jax.ShapeDtypeStruct((n, LANES), BF16),
        jax.ShapeDtypeStruct((n, 256), BF16), jax.ShapeDtypeStruct((LANES, n), BF16),
        jax.ShapeDtypeStruct((n, LANES), BF16),
        jax.ShapeDtypeStruct((n, 256), F32), jax.ShapeDtypeStruct((n, 512), F32),
        jax.ShapeDtypeStruct((n, LANES), F32), jax.ShapeDtypeStruct((n, 256), F32),
        jax.ShapeDtypeStruct((n, 256), F32))
    return pl.pallas_call(
        functools.partial(_inproj_kernel, n_lat=n_lat),
        grid=(nt,),
        in_specs=[row(d),
                  pl.BlockSpec((1, 1, d), _mod_spec(0, n_lat, seq, batch, TM)),
                  pl.BlockSpec((1, 1, d), _mod_spec(1, n_lat, seq, batch, TM)),
                  const(gpre), const(w_pad),
                  pl.BlockSpec((TM, 256), lambda i: (i % spt, 0)),
                  pl.BlockSpec((TM, 256), lambda i: (i % spt, 0)),
                  const(gq), const(gk), const(hm)],
        out_specs=(row(256), colT, row(LANES), row(256), colT, row(LANES),
                   row(256), row(512), row(LANES), row(256), row(256)),
        out_shape=out_shapes,
        compiler_params=_params(("parallel",)),
        name="inproj",
    )(xu, mod, mod, gpre, w_pad, cos_t, sin_t, gq, gk, hm)


def _conv_kernel(xs_ref, xsp_ref, xsn_ref, xl_ref, xlp_ref, xln_ref, ws_ref, bs_ref, wl_ref, bl_ref,
                 os_ref, ol_ref, *, n_lat, seq, ctx_len):
    i = pl.program_id(0)
    row0 = i * T_CONV
    pos = jnp.where(row0 < n_lat, row0 % seq, (row0 - n_lat) % ctx_len)
    slen = jnp.where(row0 < n_lat, seq, ctx_len)
    first = pos == 0
    last = pos + T_CONV == slen
    row = lax.broadcasted_iota(jnp.int32, (T_CONV, 1), 0)

    def conv(x, prev, nxt, w, b):
        pm = jnp.where(first, 0.0, prev)
        nx = jnp.where(last, 0.0, nxt)
        xm1 = jnp.where(row == 0, pm[7:8, :], pltpu.roll(x, 1, axis=0))
        xm2 = jnp.where(row == 0, pm[6:7, :], jnp.where(row == 1, pm[7:8, :], pltpu.roll(x, 2, axis=0)))
        xp1 = jnp.where(row == T_CONV - 1, nx[0:1, :], pltpu.roll(x, T_CONV - 1, axis=0))
        return w[0:1, :] * xm2 + w[1:2, :] * xm1 + w[2:3, :] * x + w[3:4, :] * xp1 + b

    os_ref[...] = _silu(conv(xs_ref[...], xsp_ref[...], xsn_ref[...], ws_ref[...], bs_ref[...]))
    ol_ref[...] = conv(xl_ref[...], xlp_ref[...], xln_ref[...], wl_ref[...], bl_ref[...])


def _conv(xbc_raw, lx_raw, ws, bs, wl, bl, *, n_lat, seq, ctx_len):
    n = xbc_raw.shape[0]
    nt = n // T_CONV
    r8 = T_CONV // SUBLANES
    n8 = n // SUBLANES
    main = lambda w: pl.BlockSpec((T_CONV, w), lambda i: (i, 0))
    prev = lambda w: pl.BlockSpec((SUBLANES, w), lambda i: (jnp.maximum(i * r8 - 1, 0), 0))
    nxt = lambda w: pl.BlockSpec((SUBLANES, w), lambda i: (jnp.minimum((i + 1) * r8, n8 - 1), 0))
    const = lambda a: pl.BlockSpec(a.shape, lambda i: (0,) * a.ndim)
    return pl.pallas_call(
        functools.partial(_conv_kernel, n_lat=n_lat, seq=seq, ctx_len=ctx_len),
        grid=(nt,),
        in_specs=[main(512), prev(512), nxt(512), main(256), prev(256), nxt(256),
                  const(ws), const(bs), const(wl), const(bl)],
        out_specs=(main(512), main(256)),
        out_shape=(jax.ShapeDtypeStruct((n, 512), F32), jax.ShapeDtypeStruct((n, 256), F32)),
        compiler_params=_params(("parallel",)),
        name="conv",
    )(xbc_raw, xbc_raw, xbc_raw, lx_raw, lx_raw, lx_raw, ws, bs, wl, bl)


def _chunk_maps(batch, seq, ctx_len):
    ncx = ctx_len // (SCAN_CHUNKS * CHUNK)
    nl = seq // (SCAN_CHUNKS * CHUNK)
    lat_blocks = batch * nl

    def block(b, c):
        return jnp.where(c < ncx, lat_blocks + b * ncx + c, b * nl + (c - ncx))

    def fwd(b, k):
        return (block(b, k), 0)

    def bwd(b, k):
        c = jnp.where(k < ncx, ncx - 1 - k, ncx + (nl - 1 - (k - ncx)))
        return (block(b, c), 0)

    return fwd, bwd, ncx + nl


def _ssd_kernel(xf_ref, dtf_ref, xb_ref, dtb_ref, dtbias_ref, alog_ref, yf_ref, yb_ref, state_ref):
    k = pl.program_id(1)

    @pl.when(k == 0)
    def _():
        state_ref[...] = jnp.zeros_like(state_ref)

    ri = lax.broadcasted_iota(jnp.int32, (CHUNK, CHUNK), 0)
    ci = lax.broadcasted_iota(jnp.int32, (CHUNK, CHUNK), 1)
    lane_lo = ci < HEAD_DIM
    aneg = -jnp.exp(alog_ref[...])
    dtbias = dtbias_ref[...]

    order = [(d, s if d == 0 else SCAN_CHUNKS - 1 - s) for s in range(SCAN_CHUNKS) for d in range(2)]
    for d, sub in order:
        x_ref, dt_ref, y_ref = ((xf_ref, dtf_ref, yf_ref), (xb_ref, dtb_ref, yb_ref))[d]
        rws = slice(sub * CHUNK, (sub + 1) * CHUNK)
        causal = (ri >= ci) if d == 0 else (ci >= ri)
        tmat = jnp.where(causal, 1.0, 0.0).astype(BF16)
        xs = x_ref[rws, 0:256]
        bm = x_ref[rws, 256:384]
        cm = x_ref[rws, 384:512]
        dtp = _softplus(dt_ref[rws, :] + dtbias)
        acum = _dot3_left(tmat, dtp * aneg)
        acum_t = acum.T
        bt = bm.T.astype(BF16)
        cmb = cm.astype(BF16)
        bmb = bm.astype(BF16)
        tot_row = CHUNK - 1 if d == 0 else 0
        for p in range(2):
            cmask = jnp.where(lane_lo if p == 0 else jnp.logical_not(lane_lo), cmb, jnp.zeros_like(cmb))
            cb = _dot_nt(cmask, bmb)
            cols, dts, ys = [], [], []
            x_pair = xs[:, p * LANES:(p + 1) * LANES]
            for j in range(2):
                col = 4 * d + 2 * p + j
                colb = jnp.broadcast_to(acum[:, col:col + 1], (CHUNK, CHUNK))
                rowb = jnp.broadcast_to(acum_t[col:col + 1, :], (CHUNK, CHUNK))
                cols.append(colb)
                dts.append(jnp.broadcast_to(dtp[:, col:col + 1], (CHUNK, CHUNK)))
            col_pair = jnp.where(lane_lo, cols[0], cols[1])
            dt_pair = jnp.where(lane_lo, dts[0], dts[1])
            xdt = x_pair * dt_pair
            xdt_b = xdt.astype(BF16)
            for j in range(2):
                col = 4 * d + 2 * p + j
                rowb = jnp.broadcast_to(acum_t[col:col + 1, :], (CHUNK, CHUNK))
                decay = jnp.exp(jnp.where(causal, cols[j] - rowb, NEG_INF))
                ys.append(_dot((cb * decay).astype(BF16), xdt_b))
            y_intra = jnp.where(lane_lo, ys[0], ys[1])
            s_old = state_ref[d, p]
            y_inter = _dot(cmask, s_old.astype(BF16)) * jnp.exp(col_pair)
            y_ref[rws, p * LANES:(p + 1) * LANES] = y_intra + y_inter
            tot_pair = col_pair[tot_row:tot_row + 1, :]
            to_end = jnp.exp(tot_pair - col_pair)
            state_ref[d, p] = s_old * jnp.exp(tot_pair) + _dot(bt, (xdt * to_end).astype(BF16))


def _ssd(xbc, dt, dtbias_row, alog_row, *, batch, seq, ctx_len):
    n = xbc.shape[0]
    fwd, bwd, steps = _chunk_maps(batch, seq, ctx_len)
    rows = SCAN_CHUNKS * CHUNK
    const = lambda a: pl.BlockSpec(a.shape, lambda b, k: (0,) * a.ndim)
    return pl.pallas_call(
        _ssd_kernel,
        grid=(batch, steps),
        in_specs=[pl.BlockSpec((rows, 512), fwd), pl.BlockSpec((rows, LANES), fwd),
                  pl.BlockSpec((rows, 512), bwd), pl.BlockSpec((rows, LANES), bwd),
                  const(dtbias_row), const(alog_row)],
        out_specs=(pl.BlockSpec((rows, 256), fwd), pl.BlockSpec((rows, 256), bwd)),
        out_shape=(jax.ShapeDtypeStruct((n, 256), F32), jax.ShapeDtypeStruct((n, 256), F32)),
        scratch_shapes=[pltpu.VMEM((2, 2, CHUNK, LANES), F32)],
        compiler_params=_params(("parallel", "arbitrary")),
        name="ssd_scan",
    )(xbc, dt, xbc, dt, dtbias_row, alog_row)


def _linear_scan(a, b, reverse):
    n = a.shape[0]
    row = lax.broadcasted_iota(jnp.int32, (n, 1), 0)
    s = 1
    while s < n:
        if reverse:
            ok = row < n - s
            a_sh = jnp.where(ok, pltpu.roll(a, n - s, axis=0), 1.0)
            b_sh = jnp.where(ok, pltpu.roll(b, n - s, axis=0), 0.0)
        else:
            ok = row >= s
            a_sh = jnp.where(ok, pltpu.roll(a, s, axis=0), 1.0)
            b_sh = jnp.where(ok, pltpu.roll(b, s, axis=0), 0.0)
        b = b + a * b_sh
        a = a * a_sh
        s *= 2
    return a, b


def _lru_kernel(uf_ref, ub_ref, wg_ref, bg_ref, lam_ref, hf_ref, hb_ref, carry_ref):
    k = pl.program_id(1)

    @pl.when(k == 0)
    def _():
        carry_ref[...] = jnp.zeros_like(carry_ref)

    for d, (u_ref, h_ref) in enumerate(((uf_ref, hf_ref), (ub_ref, hb_ref))):
        u = u_ref[...]
        gates = _dot(u.astype(BF16), wg_ref[d]) + bg_ref[d]
        r = jax.nn.sigmoid(gates[:, :LRU_WIDTH])
        ig = jax.nn.sigmoid(gates[:, LRU_WIDTH:])
        log_a = -LRU_C * r * _softplus(-lam_ref[d])
        a = jnp.exp(log_a)
        inp = jnp.sqrt(-jnp.tanh(log_a) * (1.0 + a * a)) * (ig * u)
        a_cum, b_cum = _linear_scan(a, inp, reverse=(d == 1))
        h = b_cum + a_cum * carry_ref[d, 0:1, :]
        h_ref[...] = h
        last = 0 if d == 1 else u.shape[0] - 1
        carry_ref[d, 0:1, :] = h[last:last + 1, :]


def _lru(u, wg, bg, lam, *, batch, seq, ctx_len):
    n = u.shape[0]
    fwd, bwd, steps = _chunk_maps(batch, seq, ctx_len)
    rows = SCAN_CHUNKS * CHUNK
    const = lambda a: pl.BlockSpec(a.shape, lambda b, k: (0,) * a.ndim)
    return pl.pallas_call(
        _lru_kernel,
        grid=(batch, steps),
        in_specs=[pl.BlockSpec((rows, LRU_WIDTH), fwd), pl.BlockSpec((rows, LRU_WIDTH), bwd),
                  const(wg), const(bg), const(lam)],
        out_specs=(pl.BlockSpec((rows, LRU_WIDTH), fwd), pl.BlockSpec((rows, LRU_WIDTH), bwd)),
        out_shape=(jax.ShapeDtypeStruct((n, LRU_WIDTH), F32), jax.ShapeDtypeStruct((n, LRU_WIDTH), F32)),
        scratch_shapes=[pltpu.VMEM((2, SUBLANES, LRU_WIDTH), F32)],
        compiler_params=_params(("parallel", "arbitrary")),
        name="lru_scan",
    )(u, u, wg, bg, lam)


def _stack_heads(q, g):
    qf = q.astype(F32)
    lo = g * LANES
    return jnp.concatenate([qf[:, lo:lo + HEAD_DIM], qf[:, lo + HEAD_DIM:lo + LANES]], axis=0).astype(BF16)


def _value_lanes(g):
    lane = lax.broadcasted_iota(jnp.int32, (1, LANES), 1)
    return (lane < HEAD_DIM) if g == 0 else (lane >= HEAD_DIM)


def _aug_values(v, g):
    return jnp.where(_value_lanes(g), v, jnp.ones_like(v))


def _flash_init(rows, g, sink_pair):
    if sink_pair is None:
        return jnp.full((rows, 1), NEG_INF, F32), jnp.zeros((rows, LANES), F32)
    half = lax.broadcasted_iota(jnp.int32, (rows, 1), 0) < rows // 2
    m = jnp.where(half, sink_pair[0], sink_pair[1]).astype(F32)
    acc = jnp.broadcast_to(jnp.where(_value_lanes(g), 0.0, 1.0), (rows, LANES))
    return m, acc


def _flash_update(state, q2, kt, v_aug, mask=None):
    m, acc = state
    s = _dot(q2, kt)
    if mask is not None:
        s = jnp.where(mask, s, NEG_INF)
    m_new = jnp.maximum(m, jnp.max(s, axis=-1, keepdims=True))
    p = jnp.exp(s - m_new).astype(BF16)
    acc = jnp.exp(m - m_new) * acc + _dot(p, v_aug)
    return m_new, acc


def _flash_finish(states, tq):
    pieces = []
    for g, (_, acc) in enumerate(states):
        den = (1 - g) * HEAD_DIM
        o = acc[:, g * HEAD_DIM:(g + 1) * HEAD_DIM] / acc[:, den:den + 1]
        pieces += [o[:tq], o[tq:]]
    return jnp.concatenate(pieces, axis=1)


def _group_rows(g):
    return slice(g * HEAD_DIM, (g + 1) * HEAD_DIM)


def _dense_attn_kernel(*refs, tq, seg_lens, has_sink):
    refs = list(refs)
    sink_ref = refs.pop(0) if has_sink else None
    q_ref = refs.pop(0)
    o_ref = refs.pop()
    segs = [(refs[2 * i], refs[2 * i + 1], n) for i, n in enumerate(seg_lens)]
    q = q_ref[...]
    q2 = [_stack_heads(q, g) for g in range(2)]
    states = tuple(_flash_init(2 * tq, g, (sink_ref[2 * g], sink_ref[2 * g + 1]) if has_sink else None)
                   for g in range(2))
    for kt_ref, v_ref, n_keys in segs:
        if n_keys <= KV_CHUNK:
            v = v_ref[...]
            states = tuple(_flash_update(states[g], q2[g], kt_ref[_group_rows(g), :], _aug_values(v, g))
                           for g in range(2))
        else:
            def body(c, sts, kt_ref=kt_ref, v_ref=v_ref):
                off = pl.multiple_of(c * KV_CHUNK, KV_CHUNK)
                v = v_ref[pl.ds(off, KV_CHUNK), :]
                return tuple(_flash_update(sts[g], q2[g], kt_ref[_group_rows(g), pl.ds(off, KV_CHUNK)],
                                           _aug_values(v, g)) for g in range(2))
            states = lax.fori_loop(0, n_keys // KV_CHUNK, body, states, unroll=4)
    o_ref[...] = _flash_finish(states, tq).astype(o_ref.dtype)


def _dense_attn(q, kt, v, sink, *, q_row0, q_len, tq, segs, batch):
    n = q.shape[0]
    qpb = q_len // tq
    q0 = q_row0 // tq
    in_specs, args = [], []
    if sink is not None:
        in_specs.append(pl.BlockSpec(memory_space=pltpu.SMEM))
        args.append(sink)
    in_specs.append(pl.BlockSpec((tq, 256), lambda b, i: (q0 + b * qpb + i, 0)))
    args.append(q)
    for row0, klen in segs:
        k0 = row0 // klen
        in_specs.append(pl.BlockSpec((LANES, klen), lambda b, i, k0=k0: (0, k0 + b)))
        in_specs.append(pl.BlockSpec((klen, LANES), lambda b, i, k0=k0: (k0 + b, 0)))
        args += [kt, v]
    return pl.pallas_call(
        functools.partial(_dense_attn_kernel, tq=tq, seg_lens=tuple(s[1] for s in segs), has_sink=sink is not None),
        grid=(batch, qpb),
        in_specs=in_specs,
        out_specs=pl.BlockSpec((tq, 256), lambda b, i: (b * qpb + i, 0)),
        out_shape=jax.ShapeDtypeStruct((batch * q_len, 256), BF16),
        compiler_params=_params(("parallel", "parallel")),
        name="dense_attn",
    )(*args)


def _window_attn_kernel(sink_ref, q_ref, ktc_ref, vc_ref, ktp_ref, vp_ref, ktm_ref, vm_ref, ktn_ref, vn_ref, o_ref,
                        *, n_tiles):
    n = pl.program_id(1)
    nsub = TQ_WINDOW // CHUNK
    iq = lax.broadcasted_iota(jnp.int32, (2 * CHUNK, CHUNK), 0) & (CHUNK - 1)
    jk = lax.broadcasted_iota(jnp.int32, (2 * CHUNK, CHUNK), 1)
    below = jk >= iq
    above = jk <= iq
    vctx = vc_ref[...]
    for j in range(nsub):
        cols = slice(j * CHUNK, (j + 1) * CHUNK)
        q = q_ref[cols, :]
        states = []
        for g in range(2):
            q2 = _stack_heads(q, g)
            rows = _group_rows(g)
            state = _flash_init(2 * CHUNK, g, (sink_ref[2 * g], sink_ref[2 * g + 1]))
            state = _flash_update(state, q2, ktc_ref[rows, :], _aug_values(vctx, g))
            state = _flash_update(state, q2, ktm_ref[rows, cols], _aug_values(vm_ref[cols, :], g))
            if j > 0:
                prev = slice((j - 1) * CHUNK, j * CHUNK)
                state = _flash_update(state, q2, ktm_ref[rows, prev], _aug_values(vm_ref[prev, :], g), below)
            else:
                state = _flash_update(state, q2, ktp_ref[rows, :], _aug_values(vp_ref[...], g),
                                      jnp.logical_and(below, n > 0))
            if j < nsub - 1:
                nxt = slice((j + 1) * CHUNK, (j + 2) * CHUNK)
                state = _flash_update(state, q2, ktm_ref[rows, nxt], _aug_values(vm_ref[nxt, :], g), above)
            else:
                state = _flash_update(state, q2, ktn_ref[rows, :], _aug_values(vn_ref[...], g),
                                      jnp.logical_and(above, n < n_tiles - 1))
            states.append(state)
        o_ref[cols, :] = _flash_finish(states, CHUNK).astype(o_ref.dtype)


def _window_attn(q, kt, v, sink, *, batch, seq, ctx_len):
    nt = seq // TQ_WINDOW
    nsub = TQ_WINDOW // CHUNK
    nb = seq // CHUNK
    ctx0 = (batch * seq) // ctx_len
    prev = lambda b, n: b * nb + jnp.maximum(n * nsub - 1, 0)
    nxt = lambda b, n: b * nb + jnp.minimum((n + 1) * nsub, nb - 1)
    return pl.pallas_call(
        functools.partial(_window_attn_kernel, n_tiles=nt),
        grid=(batch, nt),
        in_specs=[pl.BlockSpec(memory_space=pltpu.SMEM),
                  pl.BlockSpec((TQ_WINDOW, 256), lambda b, n: (b * nt + n, 0)),
                  pl.BlockSpec((LANES, ctx_len), lambda b, n: (0, ctx0 + b)),
                  pl.BlockSpec((ctx_len, LANES), lambda b, n: (ctx0 + b, 0)),
                  pl.BlockSpec((LANES, CHUNK), lambda b, n: (0, prev(b, n))),
                  pl.BlockSpec((CHUNK, LANES), lambda b, n: (prev(b, n), 0)),
                  pl.BlockSpec((LANES, TQ_WINDOW), lambda b, n: (0, b * nt + n)),
                  pl.BlockSpec((TQ_WINDOW, LANES), lambda b, n: (b * nt + n, 0)),
                  pl.BlockSpec((LANES, CHUNK), lambda b, n: (0, nxt(b, n))),
                  pl.BlockSpec((CHUNK, LANES), lambda b, n: (nxt(b, n), 0))],
        out_specs=pl.BlockSpec((TQ_WINDOW, 256), lambda b, n: (b * nt + n, 0)),
        out_shape=jax.ShapeDtypeStruct((batch * seq, 256), BF16),
        compiler_params=_params(("parallel", "parallel")),
        name="window_attn",
    )(sink, q, kt, v, kt, v, kt, v, kt, v)


def _gelu_tanh(x):
    return 0.5 * x * (1.0 + jnp.tanh(math.sqrt(2.0 / math.pi) * (x + 0.044715 * (x * x * x))))


def _outproj_kernel(x_ref, gate_ref, gpost_ref, oa_ref, od_ref, yf_ref, yb_ref, xs_ref, z_ref, dsk_ref, gn_ref,
                    hf_ref, hb_ref, lg_ref, w_ref, o_ref):
    y_ssd = (yf_ref[...] + yb_ref[...] + xs_ref[...] * dsk_ref[...]) * _silu(z_ref[...])
    ob = _rms(y_ssd, gn_ref[...])
    oc = (hf_ref[...] + hb_ref[...]) * _gelu_tanh(lg_ref[...])
    y = (_dot(oa_ref[...], w_ref[0:256, :]) + _dot(ob.astype(BF16), w_ref[256:512, :])
         + _dot(oc.astype(BF16), w_ref[512:768, :]) + _dot(od_ref[...], w_ref[768:1024, :]))
    o_ref[...] = x_ref[...] + gate_ref[0] * _rms(y, gpost_ref[...])


def _outproj(xu, mod, gpost, oa, od, yf, yb, xbc, z, dsk, gn, hf, hb, lg, w_out, *, n_rows, n_lat, seq, batch):
    d = xu.shape[1]
    row = lambda w: pl.BlockSpec((TM, w), lambda i: (i, 0))
    const = lambda a: pl.BlockSpec(a.shape, lambda i: (0,) * a.ndim)
    return pl.pallas_call(
        _outproj_kernel,
        grid=(n_rows // TM,),
        in_specs=[row(d), pl.BlockSpec((1, 1, d), _mod_spec(2, n_lat, seq, batch, TM)), const(gpost),
                  row(256), row(256), row(256), row(256), row(256), row(256), const(dsk), const(gn),
                  row(256), row(256), row(256), const(w_out)],
        out_specs=row(d),
        out_shape=jax.ShapeDtypeStruct((n_rows, d), F32),
        compiler_params=_params(("parallel",)),
        name="outproj",
    )(xu, mod, gpost, oa, od, yf, yb, xbc, z, dsk, gn, hf, hb, lg, w_out)


def _ceil_seg(c):
    return jnp.floor((c + (SEG_ALIGN - 1)) * (1.0 / SEG_ALIGN)) * SEG_ALIGN


def _router_kernel(x_ref, shift_ref, scale_ref, gpre_ref, rwt_ref, rb_ref, hb_ref, ld_ref, wk_ref, tab_ref):
    h = _rms(x_ref[...], gpre_ref[...])
    h = h * (1.0 + scale_ref[0]) + shift_ref[0]
    hb = h.astype(BF16)
    hb_ref[...] = hb

    scores = jax.nn.sigmoid(_dot_nt(rwt_ref[...], hb))
    biased = scores + rb_ref[...]
    gsz = N_EXPERTS // N_EXPERT_GROUPS
    sub = lax.broadcasted_iota(jnp.int32, (gsz, TM), 0)
    blocks, gscore = [], []
    for g in range(N_EXPERT_GROUPS):
        blk = biased[g * gsz:(g + 1) * gsz, :]
        m1 = jnp.max(blk, axis=0, keepdims=True)
        first = jnp.min(jnp.where(blk == m1, sub, gsz), axis=0, keepdims=True)
        m2 = jnp.max(jnp.where(sub == first, -jnp.inf, blk), axis=0, keepdims=True)
        blocks.append(blk)
        gscore.append(m1 + m2)
    masked = []
    for g in range(N_EXPERT_GROUPS):
        rank = jnp.zeros((1, TM), F32)
        for g2 in range(N_EXPERT_GROUPS):
            if g2 == g:
                continue
            beats = (gscore[g2] > gscore[g]) | ((gscore[g2] == gscore[g]) if g2 < g else False)
            rank = rank + jnp.where(beats, 1.0, 0.0)
        masked.append(jnp.where(rank < TOPK_GROUPS, blocks[g], -jnp.inf))
    vals = jnp.concatenate(masked, axis=0)
    eidx = lax.broadcasted_iota(jnp.int32, (N_EXPERTS, TM), 0)
    rank = jnp.zeros((N_EXPERTS, TM), F32)
    for e2 in range(N_EXPERTS):
        rowv = vals[e2:e2 + 1, :]
        beats = (rowv > vals) | ((rowv == vals) & (eidx > e2))
        rank = rank + jnp.where(beats, 1.0, 0.0)
    sel = rank < TOP_K
    self32 = jnp.where(sel, 1.0, 0.0)
    picked = jnp.where(sel, scores, 0.0)
    wdense = picked / jnp.sum(picked, axis=0, keepdims=True) * ROUTED_SCALE

    tr = lax.broadcasted_iota(jnp.int32, (TM, TM), 0)
    tc = lax.broadcasted_iota(jnp.int32, (TM, TM), 1)
    before = jnp.where(tr < tc, 1.0, 0.0).astype(BF16)
    selb = self32.astype(BF16)
    pos = _dot(selb, before)
    er = lax.broadcasted_iota(jnp.int32, (N_EXPERTS, N_EXPERTS), 0)
    ec = lax.broadcasted_iota(jnp.int32, (N_EXPERTS, N_EXPERTS), 1)
    lower = jnp.where(ec < er, 1.0, 0.0).astype(BF16)
    upper = jnp.where(er < ec, 1.0, 0.0).astype(BF16)
    ksel = _dot(lower, selb)
    cnt_col = _ceil_seg(jnp.sum(self32, axis=1, keepdims=True))
    loc_col = _dot3_left(lower, jnp.broadcast_to(cnt_col, (N_EXPERTS, LANES)))[:, 0:1]
    cnt_row = _ceil_seg(_dot_nt(jnp.ones((SUBLANES, TM), BF16), selb))
    loc_row = _dot3(cnt_row, upper)
    tab_ref[...] = jnp.concatenate([cnt_row, loc_row], axis=1).astype(jnp.int32)

    r8 = lax.broadcasted_iota(jnp.int32, (TOP_K, TM), 0)
    ld = jnp.zeros((TOP_K, TM), F32)
    wk = jnp.zeros((TOP_K, TM), F32)
    stage_row = pos + loc_col
    for k in range(TOP_K):
        one = sel & (ksel == float(k))
        ld = jnp.where(r8 == k, jnp.sum(jnp.where(one, stage_row, 0.0), axis=0, keepdims=True), ld)
        wk = jnp.where(r8 == k, jnp.sum(jnp.where(one, wdense, 0.0), axis=0, keepdims=True), wk)
    ld_ref[...] = ld.astype(jnp.int32)
    wk_ref[...] = wk


def _router(xu, mod, gpre, rwt, rb, *, n_rows, n_lat, seq, batch):
    d = xu.shape[1]
    row = lambda w: pl.BlockSpec((TM, w), lambda i: (i, 0))
    col = pl.BlockSpec((TOP_K, TM), lambda i: (0, i))
    const = lambda a: pl.BlockSpec(a.shape, lambda i: (0,) * a.ndim)
    return pl.pallas_call(
        _router_kernel,
        grid=(n_rows // TM,),
        in_specs=[row(d), pl.BlockSpec((1, 1, d), _mod_spec(3, n_lat, seq, batch, TM)),
                  pl.BlockSpec((1, 1, d), _mod_spec(4, n_lat, seq, batch, TM)),
                  const(gpre), const(rwt), const(rb)],
        out_specs=(row(d), col, col, pl.BlockSpec((SUBLANES, 2 * N_EXPERTS), lambda i: (i, 0))),
        out_shape=(jax.ShapeDtypeStruct((n_rows, d), BF16),
                   jax.ShapeDtypeStruct((TOP_K, n_rows), jnp.int32),
                   jax.ShapeDtypeStruct((TOP_K, n_rows), F32),
                   jax.ShapeDtypeStruct((n_rows // TM * SUBLANES, 2 * N_EXPERTS), jnp.int32)),
        compiler_params=_params(("parallel",)),
        name="router",
    )(xu, mod, mod, gpre, rwt, rb)


def _pow2_pieces(limit):
    bits, b = [], limit
    while b >= SEG_ALIGN:
        bits.append(b)
        b //= 2
    return bits


def _copy_pieces(n, src_ref, src0, dst_ref, dst0, sem, limit, wait, same_src=False):
    for bit in _pow2_pieces(limit):
        @pl.when((n & bit) != 0)
        def _():
            off = n & ~(2 * bit - 1)
            cp = pltpu.make_async_copy(src_ref.at[pl.ds(pl.multiple_of(src0 + (0 if same_src else off), SEG_ALIGN),
                                                          bit)],
                                       dst_ref.at[pl.ds(pl.multiple_of(dst0 + off, SEG_ALIGN), bit)], sem)
            cp.wait() if wait else cp.start()


N_PIECE_TABS = 7


def _piece_copies(tile, tabs, stage_ref, slots_ref, sem, to_slots, wait):
    nbig_ref, nsmall_ref, _, bsrc_ref, bdst_ref, ssrc_ref, sdst_ref = tabs
    for rows, n_ref, a_ref, b_ref, cap in ((BIG_PIECE, nbig_ref, bsrc_ref, bdst_ref, BIG_MAX),
                                           (SEG_ALIGN, nsmall_ref, ssrc_ref, sdst_ref, SMALL_MAX)):
        def body(p, c, rows=rows, a_ref=a_ref, b_ref=b_ref, cap=cap):
            src = stage_ref.at[pl.ds(pl.multiple_of(a_ref[tile * cap + p], SEG_ALIGN), rows)]
            dst = slots_ref.at[pl.ds(pl.multiple_of(b_ref[tile * cap + p], SEG_ALIGN), rows)]
            cp = pltpu.make_async_copy(src, dst, sem) if to_slots else pltpu.make_async_copy(dst, src, sem)
            cp.wait() if wait else cp.start()
            return c
        lax.fori_loop(0, n_ref[tile], body, 0)


def _used_blocks(tile, tabs):
    return (tabs[2][tile] + TM - 1) // TM


def _stage_rows_iota():
    return lax.broadcasted_iota(jnp.int32, (TM // 2, TM), 0).astype(F32).astype(BF16)


def _pick_matrix(ld, base, vals, jrow):
    rel = (ld - base).astype(F32)
    rel = jnp.where(jnp.logical_and(rel >= 0.0, rel < TM // 2), rel, -1.0).astype(BF16)
    out = jnp.zeros((TM // 2, TM), BF16)
    for k in range(TOP_K):
        out = jnp.where(rel[k:k + 1, :] == jrow, vals[k:k + 1, :], out)
    return out


def _dispatch_kernel(*refs):
    tabs = refs[:N_PIECE_TABS]
    pstart_ref, npad_ref, hb_ref, ld_ref, xs_ref, stage, zbuf, sem, zsem = refs[N_PIECE_TABS:]
    i = pl.program_id(0)

    @pl.when(i == 0)
    def _():
        zbuf[...] = jnp.zeros_like(zbuf)
        for wait in (False, True):
            def body(e, c, wait=wait):
                _copy_pieces(npad_ref[e], zbuf, 0, xs_ref, pstart_ref[e], zsem, BM_EXPERT // 2, wait, same_src=True)
                return c
            lax.fori_loop(0, N_EXPERTS, body, 0)

    ld = ld_ref[...]
    hb = hb_ref[...]
    jrow = _stage_rows_iota()
    ones = jnp.ones((TOP_K, TM), BF16)

    cur = stage.at[i & 1]

    def block(b, c):
        for half in range(2):
            base = pl.multiple_of(b * TM + half * (TM // 2), TM // 2)
            cur[pl.ds(base, TM // 2), :] = _dot(_pick_matrix(ld, base, ones, jrow), hb).astype(BF16)
        return c

    lax.fori_loop(0, _used_blocks(i, tabs), block, 0)

    @pl.when(i > 0)
    def _():
        _piece_copies(i - 1, tabs, stage.at[(i - 1) & 1], xs_ref, sem, True, True)

    _piece_copies(i, tabs, cur, xs_ref, sem, True, False)

    @pl.when(i == pl.num_programs(0) - 1)
    def _():
        _piece_copies(i, tabs, cur, xs_ref, sem, True, True)


def _dispatch(tabs, pad_start, n_pad, hb, ld, n_slots):
    n, d = hb.shape
    grid_spec = pltpu.PrefetchScalarGridSpec(
        num_scalar_prefetch=N_PIECE_TABS + 2,
        grid=(n // TM,),
        in_specs=[pl.BlockSpec((TM, d), lambda i, *_: (i, 0)),
                  pl.BlockSpec((TOP_K, TM), lambda i, *_: (0, i))],
        out_specs=pl.BlockSpec(memory_space=pl.ANY),
        scratch_shapes=[pltpu.VMEM((2, STAGE_ROWS, d), BF16), pltpu.VMEM((BM_EXPERT // 2, d), BF16),
                        pltpu.SemaphoreType.DMA(()), pltpu.SemaphoreType.DMA(())],
    )
    return pl.pallas_call(
        _dispatch_kernel,
        grid_spec=grid_spec,
        out_shape=jax.ShapeDtypeStruct((n_slots, d), BF16),
        compiler_params=_params(("arbitrary",)),
        name="moe_dispatch",
    )(*tabs, pad_start, n_pad, hb, ld)


def _expert_kernel(be_ref, na_ref, nxt_ref, slot_ref, xs_ref, wg_hbm, wu_hbm, wd_hbm, ys_ref,
                   wg_raw, wu_raw, wd_raw, wgb, wub, wdb, wsem, *, layer):
    i = pl.program_id(0)

    def weight_copies(e, slot):
        return [pltpu.make_async_copy(src.at[layer, e], dst.at[slot], wsem.at[slot, j])
                for j, (src, dst) in enumerate(((wg_hbm, wg_raw), (wu_hbm, wu_raw), (wd_hbm, wd_raw)))]

    @pl.when(i < na_ref[0])
    def _():
        e, slot = be_ref[i], slot_ref[i]

        @pl.when(i == 0)
        def _():
            for cp in weight_copies(e, slot):
                cp.start()

        @pl.when(jnp.logical_or(i == 0, e != be_ref[jnp.maximum(i - 1, 0)]))
        def _():
            for cp in weight_copies(e, slot):
                cp.wait()
            wgb[...] = wg_raw[slot].astype(BF16)
            wub[...] = wu_raw[slot].astype(BF16)
            wdb[...] = wd_raw[slot].astype(BF16)

            @pl.when(nxt_ref[i] != e)
            def _():
                for cp in weight_copies(nxt_ref[i], 1 - slot):
                    cp.start()

        half = BM_EXPERT // 2
        for r in range(2):
            xb = xs_ref[r * half:(r + 1) * half, :]
            hid = _silu(_dot(xb, wgb[...])) * _dot(xb, wub[...])
            ys_ref[r * half:(r + 1) * half, :] = _dot(hid.astype(BF16), wdb[...]).astype(ys_ref.dtype)


def _experts(block_e, n_active, next_e, w_slot, xs, wg, wu, wd, layer):
    n_slots, d = xs.shape
    nb = n_slots // BM_EXPERT
    tiles = pl.BlockSpec((BM_EXPERT, d), lambda i, be, na, *_: (jnp.minimum(i, na[0] - 1), 0))
    anywhere = pl.BlockSpec(memory_space=pl.ANY)
    grid_spec = pltpu.PrefetchScalarGridSpec(
        num_scalar_prefetch=4,
        grid=(nb,),
        in_specs=[tiles, anywhere, anywhere, anywhere],
        out_specs=tiles,
        scratch_shapes=[pltpu.VMEM((2, d, EXPERT_HIDDEN), F32), pltpu.VMEM((2, d, EXPERT_HIDDEN), F32),
                        pltpu.VMEM((2, EXPERT_HIDDEN, d), F32),
                        pltpu.VMEM((d, EXPERT_HIDDEN), BF16), pltpu.VMEM((d, EXPERT_HIDDEN), BF16),
                        pltpu.VMEM((EXPERT_HIDDEN, d), BF16), pltpu.SemaphoreType.DMA((2, 3))],
    )
    return pl.pallas_call(
        functools.partial(_expert_kernel, layer=layer),
        grid_spec=grid_spec,
        out_shape=jax.ShapeDtypeStruct((n_slots, d), BF16),
        compiler_params=_params(("arbitrary",)),
        name="moe_experts",
    )(block_e, n_active, next_e, w_slot, xs, wg, wu, wd)


def _combine_kernel(*refs):
    tabs = refs[:N_PIECE_TABS]
    (ys_ref, ld_ref, wk_ref, hb_ref, x_ref, gate_ref, gpost_ref, sg_ref, su_ref, sd_ref, o_ref,
     stage, acc_ref, sem) = refs[N_PIECE_TABS:]
    i = pl.program_id(0)

    @pl.when(i == 0)
    def _():
        stage[...] = jnp.zeros_like(stage)
        _piece_copies(0, tabs, stage.at[0], ys_ref, sem, False, False)

    cur = stage.at[i & 1]
    _piece_copies(i, tabs, cur, ys_ref, sem, False, True)

    @pl.when(i + 1 < pl.num_programs(0))
    def _():
        _piece_copies(i + 1, tabs, stage.at[(i + 1) & 1], ys_ref, sem, False, False)

    hb = hb_ref[...]
    acc_ref[...] = _dot((_silu(_dot(hb, sg_ref[...])) * _dot(hb, su_ref[...])).astype(BF16), sd_ref[...])

    ld = ld_ref[...]
    wkb = wk_ref[...].astype(BF16)
    jrow = _stage_rows_iota()

    def block(b, c):
        for half in range(2):
            base = pl.multiple_of(b * TM + half * (TM // 2), TM // 2)
            weights = _pick_matrix(ld, base, wkb, jrow)
            acc_ref[...] += lax.dot_general(weights, cur[pl.ds(base, TM // 2), :], (((0,), (0,)), ((), ())),
                                            preferred_element_type=F32)
        return c

    lax.fori_loop(0, _used_blocks(i, tabs), block, 0)
    o_ref[...] = x_ref[...] + gate_ref[0] * _rms(acc_ref[...], gpost_ref[...])


def _combine(tabs, ys, ld, wk, hb, xu, mod, gpost, sg, su, sd, *, n_rows, n_lat, seq, batch):
    d = xu.shape[1]
    row = lambda w: pl.BlockSpec((TM, w), lambda i, *_: (i, 0))
    col = pl.BlockSpec((TOP_K, TM), lambda i, *_: (0, i))
    const = lambda a: pl.BlockSpec(a.shape, lambda i, *_: (0,) * a.ndim)
    mod_map = _mod_spec(5, n_lat, seq, batch, TM)
    grid_spec = pltpu.PrefetchScalarGridSpec(
        num_scalar_prefetch=N_PIECE_TABS,
        grid=(n_rows // TM,),
        in_specs=[pl.BlockSpec(memory_space=pl.ANY), col, col, row(d), row(d),
                  pl.BlockSpec((1, 1, d), lambda i, *_: mod_map(i)),
                  const(gpost), const(sg), const(su), const(sd)],
        out_specs=row(d),
        scratch_shapes=[pltpu.VMEM((2, STAGE_ROWS, d), BF16), pltpu.VMEM((TM, d), F32), pltpu.SemaphoreType.DMA(())],
    )
    return pl.pallas_call(
        _combine_kernel,
        grid_spec=grid_spec,
        out_shape=jax.ShapeDtypeStruct((n_rows, d), F32),
        compiler_params=_params(("arbitrary",)),
        name="moe_combine",
    )(*tabs, ys, ld, wk, hb, xu, mod, gpost, sg, su, sd)


def _deinterleave(w):
    cols = w.shape[-1]
    perm = jnp.concatenate([jnp.arange(0, HEAD_DIM, 2), jnp.arange(1, HEAD_DIM, 2)])
    idx = (jnp.arange(cols // HEAD_DIM)[:, None] * HEAD_DIM + perm[None, :]).reshape(-1)
    return w[..., idx]


def _pad_in_proj(w_in):
    d = w_in.shape[0]
    o = 0
    parts = {}
    for name, width in (("qa", 256), ("ka", 128), ("va", 128), ("z", 256), ("xs", 256), ("bm", 128), ("cm", 128),
                        ("dtf", 4), ("dtb", 4), ("lx", 256), ("lg", 256), ("qd", 256), ("kd", 128), ("vd", 128)):
        parts[name] = w_in[:, o:o + width]
        o += width
    dt = jnp.concatenate([parts["dtf"], parts["dtb"], jnp.zeros((d, LANES - 8), w_in.dtype)], axis=1)
    cols = [_deinterleave(parts["qa"]), _deinterleave(parts["ka"]), parts["va"],
            _deinterleave(parts["qd"]), _deinterleave(parts["kd"]), parts["vd"],
            parts["z"], parts["xs"], parts["bm"], parts["cm"], dt, parts["lx"], parts["lg"]]
    return jnp.concatenate(cols, axis=1).astype(BF16)


def _rope_tables(seq):
    t = jnp.arange(seq)
    rowp = (t // GRID_W).astype(F32)
    colp = (t % GRID_W).astype(F32)
    axis_dim = HEAD_DIM // 2
    inv_freq = ROPE_THETA ** (-jnp.arange(0, axis_dim, 2, dtype=F32) / axis_dim)
    ang = jnp.concatenate([rowp[:, None] * inv_freq, colp[:, None] * inv_freq], axis=-1)
    cos, sin = jnp.cos(ang), jnp.sin(ang)
    cos_h = jnp.concatenate([cos, cos], axis=-1)
    sin_h = jnp.concatenate([-sin, sin], axis=-1)
    return jnp.tile(cos_h, (1, 4)), jnp.tile(sin_h, (1, 4))


def _block_diag(w):
    nb, bd, _ = w.shape
    eye = jnp.eye(nb, dtype=w.dtype)
    return (eye[:, None, :, None] * w[:, :, None, :]).reshape(nb * bd, nb * bd)


def _piece_table(counts, cap, stage0, slot0, rows, ids):
    ends = jnp.cumsum(counts, axis=1)
    q = jnp.arange(cap, dtype=jnp.int32)
    owner = jnp.sum((ends[:, None, :] <= q[None, :, None]).astype(jnp.int32), axis=-1)
    mine = owner[:, :, None] == ids
    pick = lambda v: jnp.sum(jnp.where(mine, v[:, None, :], 0), axis=-1)
    step = rows * (q[None, :] - pick(ends - counts))
    return (pick(stage0) + step).reshape(-1), (pick(slot0) + step).reshape(-1)


def _lane_row(fwd, bwd):
    return jnp.concatenate([fwd, bwd, jnp.zeros((LANES - 8,), F32)]).reshape(1, LANES)


def kernel(x, c, ctx, c_ctx, w_ada, b_ada, g_mix_pre, g_mix_post, g_ffn_pre, g_ffn_post, w_in, w_out, a_sink,
           ssd_conv_w, ssd_conv_b, ssd_dt_bias, ssd_a_log, ssd_d, ssd_norm, lru_conv_w, lru_conv_b, lru_w_a,
           lru_b_a, lru_w_i, lru_b_i, lru_lambda, d_q_norm, d_k_norm, router_w, router_bias, exp_w_gate,
           exp_w_up, exp_w_down, sh_w_gate, sh_w_up, sh_w_down):
    batch, seq, d = x.shape
    ctx_len = ctx.shape[1]
    depth = w_ada.shape[0]
    n_lat = batch * seq
    n_ctx = batch * ctx_len
    n_all = n_lat + n_ctx
    assert seq % TM == 0 and n_ctx % TM == 0 and seq % T_CONV == 0 and ctx_len % T_CONV == 0
    assert seq % TQ_WINDOW == 0 and seq % (SCAN_CHUNKS * CHUNK) == 0 and ctx_len % (SCAN_CHUNKS * CHUNK) == 0
    assert ctx_len <= KV_CHUNK and seq % KV_CHUNK == 0 and seq % TQ_GLOBAL == 0 and batch + 1 <= SUBLANES

    xu = jnp.concatenate([x.reshape(n_lat, d), ctx.reshape(n_ctx, d)], axis=0)
    cin = jnp.concatenate([c, c_ctx[None, :], jnp.zeros((SUBLANES - batch - 1, d), F32)], axis=0)
    mod_all = _adaln(cin, w_ada, b_ada)
    cos_t, sin_t = _rope_tables(seq)
    hm = jnp.kron(jnp.eye(4, dtype=F32), jnp.full((HEAD_DIM, HEAD_DIM), 1.0 / HEAD_DIM, F32)).astype(BF16)

    for l in range(depth):
        with_ctx = l < depth - 1
        mod = mod_all[l].reshape(SUBLANES * 6, 1, d)
        gq = jnp.tile(_deinterleave(d_q_norm[l]), 4).reshape(1, 256)
        gk = jnp.tile(_deinterleave(d_k_norm[l]), 2).reshape(1, LANES)
        qa, kat, va, qd, kdt, vd, z, xbc_raw, dt, lx_raw, lg = _inproj(
            xu, mod, g_mix_pre[l].reshape(1, d), _pad_in_proj(w_in[l]), cos_t, sin_t, gq, gk, hm,
            n_lat=n_lat, seq=seq, batch=batch)

        xbc, lu = _conv(xbc_raw, lx_raw, ssd_conv_w[l], ssd_conv_b[l].reshape(1, -1),
                        lru_conv_w[l], lru_conv_b[l].reshape(1, -1), n_lat=n_lat, seq=seq, ctx_len=ctx_len)
        yf, yb = _ssd(xbc, dt, _lane_row(ssd_dt_bias[l, 0], ssd_dt_bias[l, 1]),
                      _lane_row(ssd_a_log[l, 0], ssd_a_log[l, 1]), batch=batch, seq=seq, ctx_len=ctx_len)
        wg = jnp.stack([jnp.concatenate([_block_diag(lru_w_a[l, dd]), _block_diag(lru_w_i[l, dd])], axis=1)
                        for dd in range(2)]).astype(BF16)
        bg = jnp.concatenate([lru_b_a[l], lru_b_i[l]], axis=1).reshape(2, 1, 2 * LRU_WIDTH)
        hf, hb = _lru(lu, wg, bg, lru_lambda[l].reshape(2, 1, LRU_WIDTH), batch=batch, seq=seq, ctx_len=ctx_len)

        oa = _window_attn(qa, kat, va, a_sink[l], batch=batch, seq=seq, ctx_len=ctx_len)
        od = _dense_attn(qd, kdt, vd, None, q_row0=0, q_len=seq, tq=TQ_GLOBAL,
                         segs=[(n_lat, ctx_len), (0, seq)], batch=batch)
        if with_ctx:
            oa_c = _dense_attn(qa, kat, va, a_sink[l], q_row0=n_lat, q_len=ctx_len, tq=ctx_len,
                               segs=[(n_lat, ctx_len)], batch=batch)
            od_c = _dense_attn(qd, kdt, vd, None, q_row0=n_lat, q_len=ctx_len, tq=ctx_len,
                               segs=[(n_lat, ctx_len)], batch=batch)
            oa = jnp.concatenate([oa, oa_c], axis=0)
            od = jnp.concatenate([od, od_c], axis=0)
        n_rows = n_all if with_ctx else n_lat

        dsk = jnp.repeat(ssd_d[l], HEAD_DIM).reshape(1, 256)
        xu_mid = _outproj(xu, mod, g_mix_post[l].reshape(1, d), oa, od, yf, yb, xbc, z, dsk,
                          ssd_norm[l].reshape(1, 256), hf, hb, lg, w_out[l].astype(BF16),
                          n_rows=n_rows, n_lat=n_lat, seq=seq, batch=batch)

        hb_ffn, ld, wk, tab = _router(xu_mid, mod, g_ffn_pre[l].reshape(1, d), router_w[l].T.astype(BF16),
                                      router_bias[l].reshape(N_EXPERTS, 1), n_rows=n_rows, n_lat=n_lat,
                                      seq=seq, batch=batch)
        n_tiles = n_rows // TM
        tab = tab.reshape(n_tiles, SUBLANES, 2 * N_EXPERTS)[:, 0, :]
        seg_cnt, seg_loc = tab[:, :N_EXPERTS], tab[:, N_EXPERTS:]
        counts = jnp.sum(seg_cnt, axis=0)
        padded = (counts + BM_EXPERT - 1) // BM_EXPERT * BM_EXPERT
        padded_end = jnp.cumsum(padded)
        offs = padded_end - padded
        seg_off = offs[None, :] + jnp.cumsum(seg_cnt, axis=0) - seg_cnt
        n_blocks = (n_rows * TOP_K + n_tiles * N_EXPERTS * SEG_ALIGN) // BM_EXPERT + N_EXPERTS
        n_active = (padded_end[-1] // BM_EXPERT).astype(jnp.int32).reshape(1)
        block_start = jnp.arange(n_blocks, dtype=jnp.int32) * BM_EXPERT
        block_e = jnp.minimum(jnp.sum((padded_end[None, :] <= block_start[:, None]).astype(jnp.int32), axis=1),
                              N_EXPERTS - 1)
        ids = jnp.arange(N_EXPERTS, dtype=jnp.int32)
        n_big = seg_cnt // BIG_PIECE
        n_small = (seg_cnt % BIG_PIECE) // SEG_ALIGN
        tabs = (jnp.sum(n_big, axis=1), jnp.sum(n_small, axis=1), seg_loc[:, -1] + seg_cnt[:, -1],
                *_piece_table(n_big, BIG_MAX, seg_loc, seg_off, BIG_PIECE, ids),
                *_piece_table(n_small, SMALL_MAX, seg_loc + n_big * BIG_PIECE, seg_off + n_big * BIG_PIECE,
                              SEG_ALIGN, ids))
        xs = _dispatch(tabs, offs + counts, padded - counts, hb_ffn, ld, n_blocks * BM_EXPERT)
        has_rows = padded > 0
        later = jnp.logical_and(ids[None, :] > ids[:, None], has_rows[None, :])
        nxt_of = jnp.min(jnp.where(later, ids[None, :], N_EXPERTS), axis=1)
        nxt_of = jnp.where(nxt_of == N_EXPERTS, ids, nxt_of)
        slot_of = (jnp.cumsum(has_rows.astype(jnp.int32)) - 1) & 1
        own = block_e[:, None] == ids[None, :]
        next_e = jnp.sum(jnp.where(own, nxt_of[None, :], 0), axis=1)
        w_slot = jnp.sum(jnp.where(own, slot_of[None, :], 0), axis=1)
        ys = _experts(block_e, n_active, next_e, w_slot, xs, exp_w_gate, exp_w_up, exp_w_down, l)
        xu = _combine(tabs, ys, ld, wk, hb_ffn, xu_mid, mod, g_ffn_post[l].reshape(1, d), sh_w_gate[l].astype(BF16),
                      sh_w_up[l].astype(BF16), sh_w_down[l].astype(BF16),
                      n_rows=n_rows, n_lat=n_lat, seq=seq, batch=batch)
    return xu[:n_lat].reshape(batch, seq, d)
```

```python
import functools
import math

import jax
import jax.numpy as jnp
from jax import lax
from jax.experimental import pallas as pl
from jax.experimental.pallas import tpu as pltpu

F32 = jnp.float32
BF16 = jnp.bfloat16

HEAD_DIM = 64
GRID_W = 64
ROPE_THETA = 10000.0
NORM_EPS = 1e-6
NEG_INF = -1e30
A_HEADS, A_KV_HEADS, WINDOW = 4, 2, 128
SSD_HEADS, SSD_GROUPS, SSD_STATE, SSD_CONV = 4, 2, 64, 4
LRU_WIDTH, LRU_BLOCKS, LRU_CONV, LRU_C = 256, 4, 4, 8.0
D_HEADS, D_KV_HEADS = 4, 2
N_EXPERTS, N_EXPERT_GROUPS, TOPK_GROUPS, TOP_K = 64, 8, 4, 8
EXPERT_HIDDEN, SHARED_HIDDEN = 256, 256
ROUTED_SCALE = 2.5

LANES = 128
SUBLANES = 8

TM = 512
T_CONV = 256
CHUNK = 128
SCAN_CHUNKS = 2
TQ_GLOBAL = 256
TQ_WINDOW = 512
KV_CHUNK = 256
KV_UNROLL = 16
BM_EXPERT = 512
SEG_ALIGN = 16
STAGE_ROWS = TM * TOP_K + N_EXPERTS * SEG_ALIGN
BIG_PIECE = 64
BIG_MAX = STAGE_ROWS // BIG_PIECE
SMALL_MAX = N_EXPERTS * (BIG_PIECE // SEG_ALIGN - 1)
VMEM_LIMIT = 48 * 1024 * 1024

C_QA, C_KA, C_VA = 0, 256, 384
C_QD, C_KD, C_VD = 512, 768, 896
C_Z, C_XBC, C_DT = 1024, 1280, 1792
C_LX, C_LG = 1920, 2176
NP_IN = 2432


def _dot(a, b):
    return jnp.dot(a, b, preferred_element_type=F32)


def _dot_nt(a, b):
    return lax.dot_general(a, b, (((1,), (1,)), ((), ())), preferred_element_type=F32)


def _dot3(a, b):
    a1 = a.astype(BF16)
    r1 = a - a1.astype(F32)
    a2 = r1.astype(BF16)
    a3 = (r1 - a2.astype(F32)).astype(BF16)
    return _dot(a1, b) + _dot(a2, b) + _dot(a3, b)


def _dot3_left(a, b):
    b1 = b.astype(BF16)
    r1 = b - b1.astype(F32)
    b2 = r1.astype(BF16)
    b3 = (r1 - b2.astype(F32)).astype(BF16)
    return _dot(a, b1) + _dot(a, b2) + _dot(a, b3)


def _silu(x):
    return x * jax.nn.sigmoid(x)


def _softplus(x):
    return jnp.maximum(x, 0.0) + jnp.log1p(jnp.exp(-jnp.abs(x)))


def _rms(x, gain):
    return x * lax.rsqrt(jnp.mean(x * x, axis=-1, keepdims=True) + NORM_EPS) * gain


def _params(sem=None):
    return pltpu.CompilerParams(dimension_semantics=sem, vmem_limit_bytes=VMEM_LIMIT)


def _adaln_kernel(c_ref, w_ref, b_ref, o_ref):
    s = _silu(c_ref[...])
    o_ref[0] = _dot(s.astype(BF16), w_ref[0].astype(BF16)) + b_ref[0]


def _adaln(cin, w_ada, b_ada):
    depth, d, n6 = w_ada.shape
    tn = 1024
    return pl.pallas_call(
        _adaln_kernel,
        grid=(depth, n6 // tn),
        in_specs=[pl.BlockSpec((SUBLANES, d), lambda l, j: (0, 0)),
                  pl.BlockSpec((1, d, tn), lambda l, j: (l, 0, j)),
                  pl.BlockSpec((1, 1, tn), lambda l, j: (l, 0, j))],
        out_specs=pl.BlockSpec((1, SUBLANES, tn), lambda l, j: (l, 0, j)),
        out_shape=jax.ShapeDtypeStruct((depth, SUBLANES, n6), F32),
        compiler_params=_params(("parallel", "parallel")),
        name="adaln",
    )(cin, w_ada, b_ada.reshape(depth, 1, n6))


def _swap_halves(t):
    w = t.shape[1]
    lane = lax.broadcasted_iota(jnp.int32, (1, w), 1)
    first = (lane & 32) == 0
    return jnp.where(first, pltpu.roll(t, w - 32, axis=1), pltpu.roll(t, 32, axis=1))


def _inproj_kernel(x_ref, shift_ref, scale_ref, gpre_ref, w_ref, cos_ref, sin_ref, gq_ref, gk_ref, hm_ref,
                   qa_ref, kat_ref, va_ref, qd_ref, kdt_ref, vd_ref, z_ref, xbc_ref, dt_ref, lx_ref, lg_ref,
                   *, n_lat):
    i = pl.program_id(0)
    is_lat = i * TM < n_lat
    h = _rms(x_ref[...], gpre_ref[...])
    h = h * (1.0 + scale_ref[0]) + shift_ref[0]
    hb = h.astype(BF16)

    def sec(a, b):
        return _dot(hb, w_ref[:, a:b])

    cos = jnp.where(is_lat, cos_ref[...], 1.0)
    sin = jnp.where(is_lat, sin_ref[...], 0.0)

    def rope(t):
        w = t.shape[1]
        return t * cos[:, :w] + _swap_halves(t) * sin[:, :w]

    def head_norm(t, gain):
        w = t.shape[1]
        ms = _dot3(t * t, hm_ref[:w, :w])
        return t * lax.rsqrt(ms + NORM_EPS) * gain

    scale = HEAD_DIM ** -0.5
    qa_ref[...] = (rope(sec(C_QA, C_KA)) * scale).astype(BF16)
    kat_ref[...] = rope(sec(C_KA, C_VA)).T.astype(BF16)
    va_ref[...] = sec(C_VA, C_QD).astype(BF16)
    qd_ref[...] = (rope(head_norm(sec(C_QD, C_KD), gq_ref[...])) * scale).astype(BF16)
    kdt_ref[...] = rope(head_norm(sec(C_KD, C_VD), gk_ref[...])).T.astype(BF16)
    vd_ref[...] = sec(C_VD, C_Z).astype(BF16)
    z_ref[...] = sec(C_Z, C_XBC)
    xbc_ref[...] = sec(C_XBC, C_DT)
    dt_ref[...] = sec(C_DT, C_LX)
    lx_ref[...] = sec(C_LX, C_LG)
    lg_ref[...] = sec(C_LG, NP_IN)


def _mod_spec(chunk, n_lat, seq, batch, tile):
    def imap(i):
        row0 = i * tile
        seg = jnp.where(row0 < n_lat, row0 // seq, batch)
        return (seg * 6 + chunk, 0, 0)
    return imap


def _inproj(xu, mod, gpre, w_pad, cos_t, sin_t, gq, gk, hm, *, n_lat, seq, batch):
    n, d = xu.shape
    nt = n // TM
    spt = seq // TM
    row = lambda w: pl.BlockSpec((TM, w), lambda i: (i, 0))
    colT = pl.BlockSpec((LANES, TM), lambda i: (0, i))
    const = lambda a: pl.BlockSpec(a.shape, lambda i: (0,) * a.ndim)
    out_shapes = (
        jax.ShapeDtypeStruct((n, 256), BF16), jax.ShapeDtypeStruct((LANES, n), BF16),
        jax.ShapeDtypeStruct((n, LANES), BF16),
        jax.ShapeDtypeStruct((n, 256), BF16), jax.ShapeDtypeStruct((LANES, n), BF16),
        jax.ShapeDtypeStruct((n, LANES), BF16),
        jax.ShapeDtypeStruct((n, 256), F32), jax.ShapeDtypeStruct((n, 512), F32),
        jax.ShapeDtypeStruct((n, LANES), F32), jax.ShapeDtypeStruct((n, 256), F32),
        jax.ShapeDtypeStruct((n, 256), F32))
    return pl.pallas_call(
        functools.partial(_inproj_kernel, n_lat=n_lat),
        grid=(nt,),
        in_specs=[row(d),
                  pl.BlockSpec((1, 1, d), _mod_spec(0, n_lat, seq, batch, TM)),
                  pl.BlockSpec((1, 1, d), _mod_spec(1, n_lat, seq, batch, TM)),
                  const(gpre), const(w_pad),
                  pl.BlockSpec((TM, 256), lambda i: (i % spt, 0)),
                  pl.BlockSpec((TM, 256), lambda i: (i % spt, 0)),
                  const(gq), const(gk), const(hm)],
        out_specs=(row(256), colT, row(LANES), row(256), colT, row(LANES),
                   row(256), row(512), row(LANES), row(256), row(256)),
        out_shape=out_shapes,
        compiler_params=_params(("parallel",)),
        name="inproj",
    )(xu, mod, mod, gpre, w_pad, cos_t, sin_t, gq, gk, hm)


def _conv_kernel(xs_ref, xsp_ref, xsn_ref, xl_ref, xlp_ref, xln_ref, ws_ref, bs_ref, wl_ref, bl_ref,
                 os_ref, ol_ref, *, n_lat, seq, ctx_len):
    i = pl.program_id(0)
    row0 = i * T_CONV
    pos = jnp.where(row0 < n_lat, row0 % seq, (row0 - n_lat) % ctx_len)
    slen = jnp.where(row0 < n_lat, seq, ctx_len)
    first = pos == 0
    last = pos + T_CONV == slen
    row = lax.broadcasted_iota(jnp.int32, (T_CONV, 1), 0)

    def conv(x, prev, nxt, w, b):
        pm = jnp.where(first, 0.0, prev)
        nx = jnp.where(last, 0.0, nxt)
        xm1 = jnp.where(row == 0, pm[7:8, :], pltpu.roll(x, 1, axis=0))
        xm2 = jnp.where(row == 0, pm[6:7, :], jnp.where(row == 1, pm[7:8, :], pltpu.roll(x, 2, axis=0)))
        xp1 = jnp.where(row == T_CONV - 1, nx[0:1, :], pltpu.roll(x, T_CONV - 1, axis=0))
        return w[0:1, :] * xm2 + w[1:2, :] * xm1 + w[2:3, :] * x + w[3:4, :] * xp1 + b

    os_ref[...] = _silu(conv(xs_ref[...], xsp_ref[...], xsn_ref[...], ws_ref[...], bs_ref[...]))
    ol_ref[...] = conv(xl_ref[...], xlp_ref[...], xln_ref[...], wl_ref[...], bl_ref[...])


def _conv(xbc_raw, lx_raw, ws, bs, wl, bl, *, n_lat, seq, ctx_len):
    n = xbc_raw.shape[0]
    nt = n // T_CONV
    r8 = T_CONV // SUBLANES
    n8 = n // SUBLANES
    main = lambda w: pl.BlockSpec((T_CONV, w), lambda i: (i, 0))
    prev = lambda w: pl.BlockSpec((SUBLANES, w), lambda i: (jnp.maximum(i * r8 - 1, 0), 0))
    nxt = lambda w: pl.BlockSpec((SUBLANES, w), lambda i: (jnp.minimum((i + 1) * r8, n8 - 1), 0))
    const = lambda a: pl.BlockSpec(a.shape, lambda i: (0,) * a.ndim)
    return pl.pallas_call(
        functools.partial(_conv_kernel, n_lat=n_lat, seq=seq, ctx_len=ctx_len),
        grid=(nt,),
        in_specs=[main(512), prev(512), nxt(512), main(256), prev(256), nxt(256),
                  const(ws), const(bs), const(wl), const(bl)],
        out_specs=(main(512), main(256)),
        out_shape=(jax.ShapeDtypeStruct((n, 512), F32), jax.ShapeDtypeStruct((n, 256), F32)),
        compiler_params=_params(("parallel",)),
        name="conv",
    )(xbc_raw, xbc_raw, xbc_raw, lx_raw, lx_raw, lx_raw, ws, bs, wl, bl)


def _chunk_maps(batch, seq, ctx_len):
    ncx = ctx_len // (SCAN_CHUNKS * CHUNK)
    nl = seq // (SCAN_CHUNKS * CHUNK)
    lat_blocks = batch * nl

    def block(b, c):
        return jnp.where(c < ncx, lat_blocks + b * ncx + c, b * nl + (c - ncx))

    def fwd(b, k):
        return (block(b, k), 0)

    def bwd(b, k):
        c = jnp.where(k < ncx, ncx - 1 - k, ncx + (nl - 1 - (k - ncx)))
        return (block(b, c), 0)

    return fwd, bwd, ncx + nl


def _ssd_kernel(xf_ref, dtf_ref, xb_ref, dtb_ref, dtbias_ref, alog_ref, yf_ref, yb_ref, state_ref):
    k = pl.program_id(1)

    @pl.when(k == 0)
    def _():
        state_ref[...] = jnp.zeros_like(state_ref)

    ri = lax.broadcasted_iota(jnp.int32, (CHUNK, CHUNK), 0)
    ci = lax.broadcasted_iota(jnp.int32, (CHUNK, CHUNK), 1)
    lane_lo = ci < HEAD_DIM
    aneg = -jnp.exp(alog_ref[...])
    dtbias = dtbias_ref[...]

    order = [(d, s if d == 0 else SCAN_CHUNKS - 1 - s) for s in range(SCAN_CHUNKS) for d in range(2)]
    for d, sub in order:
        x_ref, dt_ref, y_ref = ((xf_ref, dtf_ref, yf_ref), (xb_ref, dtb_ref, yb_ref))[d]
        rws = slice(sub * CHUNK, (sub + 1) * CHUNK)
        causal = (ri >= ci) if d == 0 else (ci >= ri)
        tmat = jnp.where(causal, 1.0, 0.0).astype(BF16)
        xs = x_ref[rws, 0:256]
        bm = x_ref[rws, 256:384]
        cm = x_ref[rws, 384:512]
        dtp = _softplus(dt_ref[rws, :] + dtbias)
        acum = _dot3_left(tmat, dtp * aneg)
        acum_t = acum.T
        bt = bm.T.astype(BF16)
        cmb = cm.astype(BF16)
        bmb = bm.astype(BF16)
        tot_row = CHUNK - 1 if d == 0 else 0
        for p in range(2):
            cmask = jnp.where(lane_lo if p == 0 else jnp.logical_not(lane_lo), cmb, jnp.zeros_like(cmb))
            cb = _dot_nt(cmask, bmb)
            cols, dts, ys = [], [], []
            x_pair = xs[:, p * LANES:(p + 1) * LANES]
            for j in range(2):
                col = 4 * d + 2 * p + j
                colb = jnp.broadcast_to(acum[:, col:col + 1], (CHUNK, CHUNK))
                rowb = jnp.broadcast_to(acum_t[col:col + 1, :], (CHUNK, CHUNK))
                cols.append(colb)
                dts.append(jnp.broadcast_to(dtp[:, col:col + 1], (CHUNK, CHUNK)))
            col_pair = jnp.where(lane_lo, cols[0], cols[1])
            dt_pair = jnp.where(lane_lo, dts[0], dts[1])
            xdt = x_pair * dt_pair
            xdt_b = xdt.astype(BF16)
            for j in range(2):
                col = 4 * d + 2 * p + j
                rowb = jnp.broadcast_to(acum_t[col:col + 1, :], (CHUNK, CHUNK))
                decay = jnp.exp(jnp.where(causal, cols[j] - rowb, NEG_INF))
                ys.append(_dot((cb * decay).astype(BF16), xdt_b))
            y_intra = jnp.where(lane_lo, ys[0], ys[1])
            s_old = state_ref[d, p]
            y_inter = _dot(cmask, s_old.astype(BF16)) * jnp.exp(col_pair)
            y_ref[rws, p * LANES:(p + 1) * LANES] = y_intra + y_inter
            tot_pair = col_pair[tot_row:tot_row + 1, :]
            to_end = jnp.exp(tot_pair - col_pair)
            state_ref[d, p] = s_old * jnp.exp(tot_pair) + _dot(bt, (xdt * to_end).astype(BF16))


def _ssd(xbc, dt, dtbias_row, alog_row, *, batch, seq, ctx_len):
    n = xbc.shape[0]
    fwd, bwd, steps = _chunk_maps(batch, seq, ctx_len)
    rows = SCAN_CHUNKS * CHUNK
    const = lambda a: pl.BlockSpec(a.shape, lambda b, k: (0,) * a.ndim)
    return pl.pallas_call(
        _ssd_kernel,
        grid=(batch, steps),
        in_specs=[pl.BlockSpec((rows, 512), fwd), pl.BlockSpec((rows, LANES), fwd),
                  pl.BlockSpec((rows, 512), bwd), pl.BlockSpec((rows, LANES), bwd),
                  const(dtbias_row), const(alog_row)],
        out_specs=(pl.BlockSpec((rows, 256), fwd), pl.BlockSpec((rows, 256), bwd)),
        out_shape=(jax.ShapeDtypeStruct((n, 256), F32), jax.ShapeDtypeStruct((n, 256), F32)),
        scratch_shapes=[pltpu.VMEM((2, 2, CHUNK, LANES), F32)],
        compiler_params=_params(("parallel", "arbitrary")),
        name="ssd_scan",
    )(xbc, dt, xbc, dt, dtbias_row, alog_row)


def _linear_scan(a, b, reverse):
    n = a.shape[0]
    row = lax.broadcasted_iota(jnp.int32, (n, 1), 0)
    s = 1
    while s < n:
        if reverse:
            ok = row < n - s
            a_sh = jnp.where(ok, pltpu.roll(a, n - s, axis=0), 1.0)
            b_sh = jnp.where(ok, pltpu.roll(b, n - s, axis=0), 0.0)
        else:
            ok = row >= s
            a_sh = jnp.where(ok, pltpu.roll(a, s, axis=0), 1.0)
            b_sh = jnp.where(ok, pltpu.roll(b, s, axis=0), 0.0)
        b = b + a * b_sh
        a = a * a_sh
        s *= 2
    return a, b


def _lru_kernel(uf_ref, ub_ref, wg_ref, bg_ref, lam_ref, hf_ref, hb_ref, carry_ref):
    k = pl.program_id(1)

    @pl.when(k == 0)
    def _():
        carry_ref[...] = jnp.zeros_like(carry_ref)

    for d, (u_ref, h_ref) in enumerate(((uf_ref, hf_ref), (ub_ref, hb_ref))):
        u = u_ref[...]
        gates = _dot(u.astype(BF16), wg_ref[d]) + bg_ref[d]
        r = jax.nn.sigmoid(gates[:, :LRU_WIDTH])
        ig = jax.nn.sigmoid(gates[:, LRU_WIDTH:])
        log_a = -LRU_C * r * _softplus(-lam_ref[d])
        a = jnp.exp(log_a)
        inp = jnp.sqrt(-jnp.tanh(log_a) * (1.0 + a * a)) * (ig * u)
        a_cum, b_cum = _linear_scan(a, inp, reverse=(d == 1))
        h = b_cum + a_cum * carry_ref[d, 0:1, :]
        h_ref[...] = h
        last = 0 if d == 1 else u.shape[0] - 1
        carry_ref[d, 0:1, :] = h[last:last + 1, :]


def _lru(u, wg, bg, lam, *, batch, seq, ctx_len):
    n = u.shape[0]
    fwd, bwd, steps = _chunk_maps(batch, seq, ctx_len)
    rows = SCAN_CHUNKS * CHUNK
    const = lambda a: pl.BlockSpec(a.shape, lambda b, k: (0,) * a.ndim)
    return pl.pallas_call(
        _lru_kernel,
        grid=(batch, steps),
        in_specs=[pl.BlockSpec((rows, LRU_WIDTH), fwd), pl.BlockSpec((rows, LRU_WIDTH), bwd),
                  const(wg), const(bg), const(lam)],
        out_specs=(pl.BlockSpec((rows, LRU_WIDTH), fwd), pl.BlockSpec((rows, LRU_WIDTH), bwd)),
        out_shape=(jax.ShapeDtypeStruct((n, LRU_WIDTH), F32), jax.ShapeDtypeStruct((n, LRU_WIDTH), F32)),
        scratch_shapes=[pltpu.VMEM((2, SUBLANES, LRU_WIDTH), F32)],
        compiler_params=_params(("parallel", "arbitrary")),
        name="lru_scan",
    )(u, u, wg, bg, lam)


def _stack_heads(q, g):
    qf = q.astype(F32)
    lo = g * LANES
    return jnp.concatenate([qf[:, lo:lo + HEAD_DIM], qf[:, lo + HEAD_DIM:lo + LANES]], axis=0).astype(BF16)


def _value_lanes(g):
    lane = lax.broadcasted_iota(jnp.int32, (1, LANES), 1)
    return (lane < HEAD_DIM) if g == 0 else (lane >= HEAD_DIM)


def _aug_values(v, g):
    return jnp.where(_value_lanes(g), v, jnp.ones_like(v))


def _flash_init(rows, g, sink_pair):
    if sink_pair is None:
        return jnp.full((rows, 1), NEG_INF, F32), jnp.zeros((rows, LANES), F32)
    half = lax.broadcasted_iota(jnp.int32, (rows, 1), 0) < rows // 2
    m = jnp.where(half, sink_pair[0], sink_pair[1]).astype(F32)
    acc = jnp.broadcast_to(jnp.where(_value_lanes(g), 0.0, 1.0), (rows, LANES))
    return m, acc


def _flash_update(state, q2, kt, v_aug, mask=None):
    m, acc = state
    s = _dot(q2, kt)
    if mask is not None:
        s = jnp.where(mask, s, NEG_INF)
    m_new = jnp.maximum(m, jnp.max(s, axis=-1, keepdims=True))
    p = jnp.exp(s - m_new).astype(BF16)
    acc = jnp.exp(m - m_new) * acc + _dot(p, v_aug)
    return m_new, acc


def _flash_finish(states, tq):
    pieces = []
    for g, (_, acc) in enumerate(states):
        den = (1 - g) * HEAD_DIM
        o = acc[:, g * HEAD_DIM:(g + 1) * HEAD_DIM] / acc[:, den:den + 1]
        pieces += [o[:tq], o[tq:]]
    return jnp.concatenate(pieces, axis=1)


def _group_rows(g):
    return slice(g * HEAD_DIM, (g + 1) * HEAD_DIM)


def _dense_attn_kernel(*refs, tq, seg_lens, has_sink):
    refs = list(refs)
    sink_ref = refs.pop(0) if has_sink else None
    q_ref = refs.pop(0)
    o_ref = refs.pop()
    segs = [(refs[2 * i], refs[2 * i + 1], n) for i, n in enumerate(seg_lens)]
    q = q_ref[...]
    q2 = [_stack_heads(q, g) for g in range(2)]
    states = tuple(_flash_init(2 * tq, g, (sink_ref[2 * g], sink_ref[2 * g + 1]) if has_sink else None)
                   for g in range(2))
    for kt_ref, v_ref, n_keys in segs:
        if n_keys <= KV_CHUNK:
            v = v_ref[...]
            states = tuple(_flash_update(states[g], q2[g], kt_ref[_group_rows(g), :], _aug_values(v, g))
                           for g in range(2))
        else:
            def body(c, sts, kt_ref=kt_ref, v_ref=v_ref):
                off = pl.multiple_of(c * KV_CHUNK, KV_CHUNK)
                v = v_ref[pl.ds(off, KV_CHUNK), :]
                return tuple(_flash_update(sts[g], q2[g], kt_ref[_group_rows(g), pl.ds(off, KV_CHUNK)],
                                           _aug_values(v, g)) for g in range(2))
            states = lax.fori_loop(0, n_keys // KV_CHUNK, body, states, unroll=KV_UNROLL)
    o_ref[...] = _flash_finish(states, tq).astype(o_ref.dtype)


def _dense_attn(q, kt, v, sink, *, q_row0, q_len, tq, segs, batch):
    n = q.shape[0]
    qpb = q_len // tq
    q0 = q_row0 // tq
    in_specs, args = [], []
    if sink is not None:
        in_specs.append(pl.BlockSpec(memory_space=pltpu.SMEM))
        args.append(sink)
    in_specs.append(pl.BlockSpec((tq, 256), lambda b, i: (q0 + b * qpb + i, 0)))
    args.append(q)
    for row0, klen in segs:
        k0 = row0 // klen
        in_specs.append(pl.BlockSpec((LANES, klen), lambda b, i, k0=k0: (0, k0 + b)))
        in_specs.append(pl.BlockSpec((klen, LANES), lambda b, i, k0=k0: (k0 + b, 0)))
        args += [kt, v]
    return pl.pallas_call(
        functools.partial(_dense_attn_kernel, tq=tq, seg_lens=tuple(s[1] for s in segs), has_sink=sink is not None),
        grid=(batch, qpb),
        in_specs=in_specs,
        out_specs=pl.BlockSpec((tq, 256), lambda b, i: (b * qpb + i, 0)),
        out_shape=jax.ShapeDtypeStruct((batch * q_len, 256), BF16),
        compiler_params=_params(("parallel", "parallel")),
        name="dense_attn",
    )(*args)


def _window_attn_kernel(sink_ref, q_ref, ktc_ref, vc_ref, ktp_ref, vp_ref, ktm_ref, vm_ref, ktn_ref, vn_ref, o_ref,
                        *, n_tiles):
    n = pl.program_id(1)
    nsub = TQ_WINDOW // CHUNK
    iq = lax.broadcasted_iota(jnp.int32, (2 * CHUNK, CHUNK), 0) & (CHUNK - 1)
    jk = lax.broadcasted_iota(jnp.int32, (2 * CHUNK, CHUNK), 1)
    below = jk >= iq
    above = jk <= iq
    vctx = vc_ref[...]
    for j in range(nsub):
        cols = slice(j * CHUNK, (j + 1) * CHUNK)
        q = q_ref[cols, :]
        states = []
        for g in range(2):
            q2 = _stack_heads(q, g)
            rows = _group_rows(g)
            state = _flash_init(2 * CHUNK, g, (sink_ref[2 * g], sink_ref[2 * g + 1]))
            state = _flash_update(state, q2, ktc_ref[rows, :], _aug_values(vctx, g))
            state = _flash_update(state, q2, ktm_ref[rows, cols], _aug_values(vm_ref[cols, :], g))
            if j > 0:
                prev = slice((j - 1) * CHUNK, j * CHUNK)
                state = _flash_update(state, q2, ktm_ref[rows, prev], _aug_values(vm_ref[prev, :], g), below)
            else:
                state = _flash_update(state, q2, ktp_ref[rows, :], _aug_values(vp_ref[...], g),
                                      jnp.logical_and(below, n > 0))
            if j < nsub - 1:
                nxt = slice((j + 1) * CHUNK, (j + 2) * CHUNK)
                state = _flash_update(state, q2, ktm_ref[rows, nxt], _aug_values(vm_ref[nxt, :], g), above)
            else:
                state = _flash_update(state, q2, ktn_ref[rows, :], _aug_values(vn_ref[...], g),
                                      jnp.logical_and(above, n < n_tiles - 1))
            states.append(state)
        o_ref[cols, :] = _flash_finish(states, CHUNK).astype(o_ref.dtype)


def _window_attn(q, kt, v, sink, *, batch, seq, ctx_len):
    nt = seq // TQ_WINDOW
    nsub = TQ_WINDOW // CHUNK
    nb = seq // CHUNK
    ctx0 = (batch * seq) // ctx_len
    prev = lambda b, n: b * nb + jnp.maximum(n * nsub - 1, 0)
    nxt = lambda b, n: b * nb + jnp.minimum((n + 1) * nsub, nb - 1)
    return pl.pallas_call(
        functools.partial(_window_attn_kernel, n_tiles=nt),
        grid=(batch, nt),
        in_specs=[pl.BlockSpec(memory_space=pltpu.SMEM),
                  pl.BlockSpec((TQ_WINDOW, 256), lambda b, n: (b * nt + n, 0)),
                  pl.BlockSpec((LANES, ctx_len), lambda b, n: (0, ctx0 + b)),
                  pl.BlockSpec((ctx_len, LANES), lambda b, n: (ctx0 + b, 0)),
                  pl.BlockSpec((LANES, CHUNK), lambda b, n: (0, prev(b, n))),
                  pl.BlockSpec((CHUNK, LANES), lambda b, n: (prev(b, n), 0)),
                  pl.BlockSpec((LANES, TQ_WINDOW), lambda b, n: (0, b * nt + n)),
                  pl.BlockSpec((TQ_WINDOW, LANES), lambda b, n: (b * nt + n, 0)),
                  pl.BlockSpec((LANES, CHUNK), lambda b, n: (0, nxt(b, n))),
                  pl.BlockSpec((CHUNK, LANES), lambda b, n: (nxt(b, n), 0))],
        out_specs=pl.BlockSpec((TQ_WINDOW, 256), lambda b, n: (b * nt + n, 0)),
        out_shape=jax.ShapeDtypeStruct((batch * seq, 256), BF16),
        compiler_params=_params(("parallel", "parallel")),
        name="window_attn",
    )(sink, q, kt, v, kt, v, kt, v, kt, v)


def _gelu_tanh(x):
    return 0.5 * x * (1.0 + jnp.tanh(math.sqrt(2.0 / math.pi) * (x + 0.044715 * (x * x * x))))


def _outproj_kernel(x_ref, gate_ref, gpost_ref, oa_ref, od_ref, yf_ref, yb_ref, xs_ref, z_ref, dsk_ref, gn_ref,
                    hf_ref, hb_ref, lg_ref, w_ref, o_ref):
    y_ssd = (yf_ref[...] + yb_ref[...] + xs_ref[...] * dsk_ref[...]) * _silu(z_ref[...])
    ob = _rms(y_ssd, gn_ref[...])
    oc = (hf_ref[...] + hb_ref[...]) * _gelu_tanh(lg_ref[...])
    y = (_dot(oa_ref[...], w_ref[0:256, :]) + _dot(ob.astype(BF16), w_ref[256:512, :])
         + _dot(oc.astype(BF16), w_ref[512:768, :]) + _dot(od_ref[...], w_ref[768:1024, :]))
    o_ref[...] = x_ref[...] + gate_ref[0] * _rms(y, gpost_ref[...])


def _outproj(xu, mod, gpost, oa, od, yf, yb, xbc, z, dsk, gn, hf, hb, lg, w_out, *, n_rows, n_lat, seq, batch):
    d = xu.shape[1]
    row = lambda w: pl.BlockSpec((TM, w), lambda i: (i, 0))
    const = lambda a: pl.BlockSpec(a.shape, lambda i: (0,) * a.ndim)
    return pl.pallas_call(
        _outproj_kernel,
        grid=(n_rows // TM,),
        in_specs=[row(d), pl.BlockSpec((1, 1, d), _mod_spec(2, n_lat, seq, batch, TM)), const(gpost),
                  row(256), row(256), row(256), row(256), row(256), row(256), const(dsk), const(gn),
                  row(256), row(256), row(256), const(w_out)],
        out_specs=row(d),
        out_shape=jax.ShapeDtypeStruct((n_rows, d), F32),
        compiler_params=_params(("parallel",)),
        name="outproj",
    )(xu, mod, gpost, oa, od, yf, yb, xbc, z, dsk, gn, hf, hb, lg, w_out)


def _ceil_seg(c):
    return jnp.floor((c + (SEG_ALIGN - 1)) * (1.0 / SEG_ALIGN)) * SEG_ALIGN


def _router_kernel(x_ref, shift_ref, scale_ref, gpre_ref, rwt_ref, rb_ref, hb_ref, ld_ref, wk_ref, tab_ref):
    h = _rms(x_ref[...], gpre_ref[...])
    h = h * (1.0 + scale_ref[0]) + shift_ref[0]
    hb = h.astype(BF16)
    hb_ref[...] = hb

    scores = jax.nn.sigmoid(_dot_nt(rwt_ref[...], hb))
    biased = scores + rb_ref[...]
    gsz = N_EXPERTS // N_EXPERT_GROUPS
    sub = lax.broadcasted_iota(jnp.int32, (gsz, TM), 0)
    blocks, gscore = [], []
    for g in range(N_EXPERT_GROUPS):
        blk = biased[g * gsz:(g + 1) * gsz, :]
        m1 = jnp.max(blk, axis=0, keepdims=True)
        first = jnp.min(jnp.where(blk == m1, sub, gsz), axis=0, keepdims=True)
        m2 = jnp.max(jnp.where(sub == first, -jnp.inf, blk), axis=0, keepdims=True)
        blocks.append(blk)
        gscore.append(m1 + m2)
    masked = []
    for g in range(N_EXPERT_GROUPS):
        rank = jnp.zeros((1, TM), F32)
        for g2 in range(N_EXPERT_GROUPS):
            if g2 == g:
                continue
            beats = (gscore[g2] > gscore[g]) | ((gscore[g2] == gscore[g]) if g2 < g else False)
            rank = rank + jnp.where(beats, 1.0, 0.0)
        masked.append(jnp.where(rank < TOPK_GROUPS, blocks[g], -jnp.inf))
    vals = jnp.concatenate(masked, axis=0)
    eidx = lax.broadcasted_iota(jnp.int32, (N_EXPERTS, TM), 0)
    rank = jnp.zeros((N_EXPERTS, TM), F32)
    for e2 in range(N_EXPERTS):
        rowv = vals[e2:e2 + 1, :]
        beats = (rowv > vals) | ((rowv == vals) & (eidx > e2))
        rank = rank + jnp.where(beats, 1.0, 0.0)
    sel = rank < TOP_K
    self32 = jnp.where(sel, 1.0, 0.0)
    picked = jnp.where(sel, scores, 0.0)
    wdense = picked / jnp.sum(picked, axis=0, keepdims=True) * ROUTED_SCALE

    tr = lax.broadcasted_iota(jnp.int32, (TM, TM), 0)
    tc = lax.broadcasted_iota(jnp.int32, (TM, TM), 1)
    before = jnp.where(tr < tc, 1.0, 0.0).astype(BF16)
    selb = self32.astype(BF16)
    pos = _dot(selb, before)
    er = lax.broadcasted_iota(jnp.int32, (N_EXPERTS, N_EXPERTS), 0)
    ec = lax.broadcasted_iota(jnp.int32, (N_EXPERTS, N_EXPERTS), 1)
    lower = jnp.where(ec < er, 1.0, 0.0).astype(BF16)
    upper = jnp.where(er < ec, 1.0, 0.0).astype(BF16)
    ksel = _dot(lower, selb)
    cnt_col = _ceil_seg(jnp.sum(self32, axis=1, keepdims=True))
    loc_col = _dot3_left(lower, jnp.broadcast_to(cnt_col, (N_EXPERTS, LANES)))[:, 0:1]
    cnt_row = _ceil_seg(_dot_nt(jnp.ones((SUBLANES, TM), BF16), selb))
    loc_row = _dot3(cnt_row, upper)
    tab_ref[...] = jnp.concatenate([cnt_row, loc_row], axis=1).astype(jnp.int32)

    r8 = lax.broadcasted_iota(jnp.int32, (TOP_K, TM), 0)
    ld = jnp.zeros((TOP_K, TM), F32)
    wk = jnp.zeros((TOP_K, TM), F32)
    stage_row = pos + loc_col
    for k in range(TOP_K):
        one = sel & (ksel == float(k))
        ld = jnp.where(r8 == k, jnp.sum(jnp.where(one, stage_row, 0.0), axis=0, keepdims=True), ld)
        wk = jnp.where(r8 == k, jnp.sum(jnp.where(one, wdense, 0.0), axis=0, keepdims=True), wk)
    ld_ref[...] = ld.astype(jnp.int32)
    wk_ref[...] = wk


def _router(xu, mod, gpre, rwt, rb, *, n_rows, n_lat, seq, batch):
    d = xu.shape[1]
    row = lambda w: pl.BlockSpec((TM, w), lambda i: (i, 0))
    col = pl.BlockSpec((TOP_K, TM), lambda i: (0, i))
    const = lambda a: pl.BlockSpec(a.shape, lambda i: (0,) * a.ndim)
    return pl.pallas_call(
        _router_kernel,
        grid=(n_rows // TM,),
        in_specs=[row(d), pl.BlockSpec((1, 1, d), _mod_spec(3, n_lat, seq, batch, TM)),
                  pl.BlockSpec((1, 1, d), _mod_spec(4, n_lat, seq, batch, TM)),
                  const(gpre), const(rwt), const(rb)],
        out_specs=(row(d), col, col, pl.BlockSpec((SUBLANES, 2 * N_EXPERTS), lambda i: (i, 0))),
        out_shape=(jax.ShapeDtypeStruct((n_rows, d), BF16),
                   jax.ShapeDtypeStruct((TOP_K, n_rows), jnp.int32),
                   jax.ShapeDtypeStruct((TOP_K, n_rows), F32),
                   jax.ShapeDtypeStruct((n_rows // TM * SUBLANES, 2 * N_EXPERTS), jnp.int32)),
        compiler_params=_params(("parallel",)),
        name="router",
    )(xu, mod, mod, gpre, rwt, rb)


def _pow2_pieces(limit):
    bits, b = [], limit
    while b >= SEG_ALIGN:
        bits.append(b)
        b //= 2
    return bits


def _copy_pieces(n, src_ref, src0, dst_ref, dst0, sem, limit, wait, same_src=False):
    for bit in _pow2_pieces(limit):
        @pl.when((n & bit) != 0)
        def _():
            off = n & ~(2 * bit - 1)
            cp = pltpu.make_async_copy(src_ref.at[pl.ds(pl.multiple_of(src0 + (0 if same_src else off), SEG_ALIGN),
                                                          bit)],
                                       dst_ref.at[pl.ds(pl.multiple_of(dst0 + off, SEG_ALIGN), bit)], sem)
            cp.wait() if wait else cp.start()


N_PIECE_TABS = 7


def _piece_copies(tile, tabs, stage_ref, slots_ref, sem, to_slots, wait):
    nbig_ref, nsmall_ref, _, bsrc_ref, bdst_ref, ssrc_ref, sdst_ref = tabs
    for rows, n_ref, a_ref, b_ref, cap in ((BIG_PIECE, nbig_ref, bsrc_ref, bdst_ref, BIG_MAX),
                                           (SEG_ALIGN, nsmall_ref, ssrc_ref, sdst_ref, SMALL_MAX)):
        def body(p, c, rows=rows, a_ref=a_ref, b_ref=b_ref, cap=cap):
            src = stage_ref.at[pl.ds(pl.multiple_of(a_ref[tile * cap + p], SEG_ALIGN), rows)]
            dst = slots_ref.at[pl.ds(pl.multiple_of(b_ref[tile * cap + p], SEG_ALIGN), rows)]
            cp = pltpu.make_async_copy(src, dst, sem) if to_slots else pltpu.make_async_copy(dst, src, sem)
            cp.wait() if wait else cp.start()
            return c
        lax.fori_loop(0, n_ref[tile], body, 0)


def _used_blocks(tile, tabs):
    return (tabs[2][tile] + TM - 1) // TM


def _stage_rows_iota():
    return lax.broadcasted_iota(jnp.int32, (TM // 2, TM), 0).astype(F32).astype(BF16)


def _pick_matrix(ld, base, vals, jrow):
    rel = (ld - base).astype(F32)
    rel = jnp.where(jnp.logical_and(rel >= 0.0, rel < TM // 2), rel, -1.0).astype(BF16)
    out = jnp.zeros((TM // 2, TM), BF16)
    for k in range(TOP_K):
        out = jnp.where(rel[k:k + 1, :] == jrow, vals[k:k + 1, :], out)
    return out


def _dispatch_kernel(*refs):
    tabs = refs[:N_PIECE_TABS]
    pstart_ref, npad_ref, hb_ref, ld_ref, xs_ref, stage, zbuf, sem, zsem = refs[N_PIECE_TABS:]
    i = pl.program_id(0)

    @pl.when(i == 0)
    def _():
        zbuf[...] = jnp.zeros_like(zbuf)
        for wait in (False, True):
            def body(e, c, wait=wait):
                _copy_pieces(npad_ref[e], zbuf, 0, xs_ref, pstart_ref[e], zsem, BM_EXPERT // 2, wait, same_src=True)
                return c
            lax.fori_loop(0, N_EXPERTS, body, 0)

    ld = ld_ref[...]
    hb = hb_ref[...]
    jrow = _stage_rows_iota()
    ones = jnp.ones((TOP_K, TM), BF16)

    cur = stage.at[i & 1]

    def block(b, c):
        for half in range(2):
            base = pl.multiple_of(b * TM + half * (TM // 2), TM // 2)
            cur[pl.ds(base, TM // 2), :] = _dot(_pick_matrix(ld, base, ones, jrow), hb).astype(BF16)
        return c

    lax.fori_loop(0, _used_blocks(i, tabs), block, 0)

    @pl.when(i > 0)
    def _():
        _piece_copies(i - 1, tabs, stage.at[(i - 1) & 1], xs_ref, sem, True, True)

    _piece_copies(i, tabs, cur, xs_ref, sem, True, False)

    @pl.when(i == pl.num_programs(0) - 1)
    def _():
        _piece_copies(i, tabs, cur, xs_ref, sem, True, True)


def _dispatch(tabs, pad_start, n_pad, hb, ld, n_slots):
    n, d = hb.shape
    grid_spec = pltpu.PrefetchScalarGridSpec(
        num_scalar_prefetch=N_PIECE_TABS + 2,
        grid=(n // TM,),
        in_specs=[pl.BlockSpec((TM, d), lambda i, *_: (i, 0)),
                  pl.BlockSpec((TOP_K, TM), lambda i, *_: (0, i))],
        out_specs=pl.BlockSpec(memory_space=pl.ANY),
        scratch_shapes=[pltpu.VMEM((2, STAGE_ROWS, d), BF16), pltpu.VMEM((BM_EXPERT // 2, d), BF16),
                        pltpu.SemaphoreType.DMA(()), pltpu.SemaphoreType.DMA(())],
    )
    return pl.pallas_call(
        _dispatch_kernel,
        grid_spec=grid_spec,
        out_shape=jax.ShapeDtypeStruct((n_slots, d), BF16),
        compiler_params=_params(("arbitrary",)),
        name="moe_dispatch",
    )(*tabs, pad_start, n_pad, hb, ld)


def _expert_kernel(be_ref, na_ref, nxt_ref, slot_ref, xs_ref, wg_hbm, wu_hbm, wd_hbm, ys_ref,
                   wg_raw, wu_raw, wd_raw, wgb, wub, wdb, wsem, *, layer):
    i = pl.program_id(0)

    def weight_copies(e, slot):
        return [pltpu.make_async_copy(src.at[layer, e], dst.at[slot], wsem.at[slot, j])
                for j, (src, dst) in enumerate(((wg_hbm, wg_raw), (wu_hbm, wu_raw), (wd_hbm, wd_raw)))]

    @pl.when(i < na_ref[0])
    def _():
        e, slot = be_ref[i], slot_ref[i]

        @pl.when(i == 0)
        def _():
            for cp in weight_copies(e, slot):
                cp.start()

        @pl.when(jnp.logical_or(i == 0, e != be_ref[jnp.maximum(i - 1, 0)]))
        def _():
            for cp in weight_copies(e, slot):
                cp.wait()
            wgb[...] = wg_raw[slot].astype(BF16)
            wub[...] = wu_raw[slot].astype(BF16)
            wdb[...] = wd_raw[slot].astype(BF16)

            @pl.when(nxt_ref[i] != e)
            def _():
                for cp in weight_copies(nxt_ref[i], 1 - slot):
                    cp.start()

        half = BM_EXPERT // 2
        for r in range(2):
            xb = xs_ref[r * half:(r + 1) * half, :]
            hid = _silu(_dot(xb, wgb[...])) * _dot(xb, wub[...])
            ys_ref[r * half:(r + 1) * half, :] = _dot(hid.astype(BF16), wdb[...]).astype(ys_ref.dtype)


def _experts(block_e, n_active, next_e, w_slot, xs, wg, wu, wd, layer):
    n_slots, d = xs.shape
    nb = n_slots // BM_EXPERT
    tiles = pl.BlockSpec((BM_EXPERT, d), lambda i, be, na, *_: (jnp.minimum(i, na[0] - 1), 0))
    anywhere = pl.BlockSpec(memory_space=pl.ANY)
    grid_spec = pltpu.PrefetchScalarGridSpec(
        num_scalar_prefetch=4,
        grid=(nb,),
        in_specs=[tiles, anywhere, anywhere, anywhere],
        out_specs=tiles,
        scratch_shapes=[pltpu.VMEM((2, d, EXPERT_HIDDEN), F32), pltpu.VMEM((2, d, EXPERT_HIDDEN), F32),
                        pltpu.VMEM((2, EXPERT_HIDDEN, d), F32),
                        pltpu.VMEM((d, EXPERT_HIDDEN), BF16), pltpu.VMEM((d, EXPERT_HIDDEN), BF16),
                        pltpu.VMEM((EXPERT_HIDDEN, d), BF16), pltpu.SemaphoreType.DMA((2, 3))],
    )
    return pl.pallas_call(
        functools.partial(_expert_kernel, layer=layer),
        grid_spec=grid_spec,
        out_shape=jax.ShapeDtypeStruct((n_slots, d), BF16),
        compiler_params=_params(("arbitrary",)),
        name="moe_experts",
    )(block_e, n_active, next_e, w_slot, xs, wg, wu, wd)


def _combine_kernel(*refs):
    tabs = refs[:N_PIECE_TABS]
    (ys_ref, ld_ref, wk_ref, hb_ref, x_ref, gate_ref, gpost_ref, sg_ref, su_ref, sd_ref, o_ref,
     stage, acc_ref, sem) = refs[N_PIECE_TABS:]
    i = pl.program_id(0)

    @pl.when(i == 0)
    def _():
        stage[...] = jnp.zeros_like(stage)
        _piece_copies(0, tabs, stage.at[0], ys_ref, sem, False, False)

    cur = stage.at[i & 1]
    _piece_copies(i, tabs, cur, ys_ref, sem, False, True)

    @pl.when(i + 1 < pl.num_programs(0))
    def _():
        _piece_copies(i + 1, tabs, stage.at[(i + 1) & 1], ys_ref, sem, False, False)

    hb = hb_ref[...]
    acc_ref[...] = _dot((_silu(_dot(hb, sg_ref[...])) * _dot(hb, su_ref[...])).astype(BF16), sd_ref[...])

    ld = ld_ref[...]
    wkb = wk_ref[...].astype(BF16)
    jrow = _stage_rows_iota()

    def block(b, c):
        for half in range(2):
            base = pl.multiple_of(b * TM + half * (TM // 2), TM // 2)
            weights = _pick_matrix(ld, base, wkb, jrow)
            acc_ref[...] += lax.dot_general(weights, cur[pl.ds(base, TM // 2), :], (((0,), (0,)), ((), ())),
                                            preferred_element_type=F32)
        return c

    lax.fori_loop(0, _used_blocks(i, tabs), block, 0)
    o_ref[...] = x_ref[...] + gate_ref[0] * _rms(acc_ref[...], gpost_ref[...])


def _combine(tabs, ys, ld, wk, hb, xu, mod, gpost, sg, su, sd, *, n_rows, n_lat, seq, batch):
    d = xu.shape[1]
    row = lambda w: pl.BlockSpec((TM, w), lambda i, *_: (i, 0))
    col = pl.BlockSpec((TOP_K, TM), lambda i, *_: (0, i))
    const = lambda a: pl.BlockSpec(a.shape, lambda i, *_: (0,) * a.ndim)
    mod_map = _mod_spec(5, n_lat, seq, batch, TM)
    grid_spec = pltpu.PrefetchScalarGridSpec(
        num_scalar_prefetch=N_PIECE_TABS,
        grid=(n_rows // TM,),
        in_specs=[pl.BlockSpec(memory_space=pl.ANY), col, col, row(d), row(d),
                  pl.BlockSpec((1, 1, d), lambda i, *_: mod_map(i)),
                  const(gpost), const(sg), const(su), const(sd)],
        out_specs=row(d),
        scratch_shapes=[pltpu.VMEM((2, STAGE_ROWS, d), BF16), pltpu.VMEM((TM, d), F32), pltpu.SemaphoreType.DMA(())],
    )
    return pl.pallas_call(
        _combine_kernel,
        grid_spec=grid_spec,
        out_shape=jax.ShapeDtypeStruct((n_rows, d), F32),
        compiler_params=_params(("arbitrary",)),
        name="moe_combine",
    )(*tabs, ys, ld, wk, hb, xu, mod, gpost, sg, su, sd)


def _deinterleave(w):
    cols = w.shape[-1]
    perm = jnp.concatenate([jnp.arange(0, HEAD_DIM, 2), jnp.arange(1, HEAD_DIM, 2)])
    idx = (jnp.arange(cols // HEAD_DIM)[:, None] * HEAD_DIM + perm[None, :]).reshape(-1)
    return w[..., idx]


def _pad_in_proj(w_in):
    d = w_in.shape[0]
    o = 0
    parts = {}
    for name, width in (("qa", 256), ("ka", 128), ("va", 128), ("z", 256), ("xs", 256), ("bm", 128), ("cm", 128),
                        ("dtf", 4), ("dtb", 4), ("lx", 256), ("lg", 256), ("qd", 256), ("kd", 128), ("vd", 128)):
        parts[name] = w_in[:, o:o + width]
        o += width
    dt = jnp.concatenate([parts["dtf"], parts["dtb"], jnp.zeros((d, LANES - 8), w_in.dtype)], axis=1)
    cols = [_deinterleave(parts["qa"]), _deinterleave(parts["ka"]), parts["va"],
            _deinterleave(parts["qd"]), _deinterleave(parts["kd"]), parts["vd"],
            parts["z"], parts["xs"], parts["bm"], parts["cm"], dt, parts["lx"], parts["lg"]]
    return jnp.concatenate(cols, axis=1).astype(BF16)


def _rope_tables(seq):
    t = jnp.arange(seq)
    rowp = (t // GRID_W).astype(F32)
    colp = (t % GRID_W).astype(F32)
    axis_dim = HEAD_DIM // 2
    inv_freq = ROPE_THETA ** (-jnp.arange(0, axis_dim, 2, dtype=F32) / axis_dim)
    ang = jnp.concatenate([rowp[:, None] * inv_freq, colp[:, None] * inv_freq], axis=-1)
    cos, sin = jnp.cos(ang), jnp.sin(ang)
    cos_h = jnp.concatenate([cos, cos], axis=-1)
    sin_h = jnp.concatenate([-sin, sin], axis=-1)
    return jnp.tile(cos_h, (1, 4)), jnp.tile(sin_h, (1, 4))


def _block_diag(w):
    nb, bd, _ = w.shape
    eye = jnp.eye(nb, dtype=w.dtype)
    return (eye[:, None, :, None] * w[:, :, None, :]).reshape(nb * bd, nb * bd)


def _piece_table(counts, cap, stage0, slot0, rows, ids):
    ends = jnp.cumsum(counts, axis=1)
    q = jnp.arange(cap, dtype=jnp.int32)
    owner = jnp.sum((ends[:, None, :] <= q[None, :, None]).astype(jnp.int32), axis=-1)
    mine = owner[:, :, None] == ids
    pick = lambda v: jnp.sum(jnp.where(mine, v[:, None, :], 0), axis=-1)
    step = rows * (q[None, :] - pick(ends - counts))
    return (pick(stage0) + step).reshape(-1), (pick(slot0) + step).reshape(-1)


def _lane_row(fwd, bwd):
    return jnp.concatenate([fwd, bwd, jnp.zeros((LANES - 8,), F32)]).reshape(1, LANES)


def kernel(x, c, ctx, c_ctx, w_ada, b_ada, g_mix_pre, g_mix_post, g_ffn_pre, g_ffn_post, w_in, w_out, a_sink,
           ssd_conv_w, ssd_conv_b, ssd_dt_bias, ssd_a_log, ssd_d, ssd_norm, lru_conv_w, lru_conv_b, lru_w_a,
           lru_b_a, lru_w_i, lru_b_i, lru_lambda, d_q_norm, d_k_norm, router_w, router_bias, exp_w_gate,
           exp_w_up, exp_w_down, sh_w_gate, sh_w_up, sh_w_down):
    batch, seq, d = x.shape
    ctx_len = ctx.shape[1]
    depth = w_ada.shape[0]
    n_lat = batch * seq
    n_ctx = batch * ctx_len
    n_all = n_lat + n_ctx
    assert seq % TM == 0 and n_ctx % TM == 0 and seq % T_CONV == 0 and ctx_len % T_CONV == 0
    assert seq % TQ_WINDOW == 0 and seq % (SCAN_CHUNKS * CHUNK) == 0 and ctx_len % (SCAN_CHUNKS * CHUNK) == 0
    assert ctx_len <= KV_CHUNK and seq % KV_CHUNK == 0 and seq % TQ_GLOBAL == 0 and batch + 1 <= SUBLANES

    xu = jnp.concatenate([x.reshape(n_lat, d), ctx.reshape(n_ctx, d)], axis=0)
    cin = jnp.concatenate([c, c_ctx[None, :], jnp.zeros((SUBLANES - batch - 1, d), F32)], axis=0)
    mod_all = _adaln(cin, w_ada, b_ada)
    cos_t, sin_t = _rope_tables(seq)
    hm = jnp.kron(jnp.eye(4, dtype=F32), jnp.full((HEAD_DIM, HEAD_DIM), 1.0 / HEAD_DIM, F32)).astype(BF16)

    for l in range(depth):
        with_ctx = l < depth - 1
        mod = mod_all[l].reshape(SUBLANES * 6, 1, d)
        gq = jnp.tile(_deinterleave(d_q_norm[l]), 4).reshape(1, 256)
        gk = jnp.tile(_deinterleave(d_k_norm[l]), 2).reshape(1, LANES)
        qa, kat, va, qd, kdt, vd, z, xbc_raw, dt, lx_raw, lg = _inproj(
            xu, mod, g_mix_pre[l].reshape(1, d), _pad_in_proj(w_in[l]), cos_t, sin_t, gq, gk, hm,
            n_lat=n_lat, seq=seq, batch=batch)

        xbc, lu = _conv(xbc_raw, lx_raw, ssd_conv_w[l], ssd_conv_b[l].reshape(1, -1),
                        lru_conv_w[l], lru_conv_b[l].reshape(1, -1), n_lat=n_lat, seq=seq, ctx_len=ctx_len)
        yf, yb = _ssd(xbc, dt, _lane_row(ssd_dt_bias[l, 0], ssd_dt_bias[l, 1]),
                      _lane_row(ssd_a_log[l, 0], ssd_a_log[l, 1]), batch=batch, seq=seq, ctx_len=ctx_len)
        wg = jnp.stack([jnp.concatenate([_block_diag(lru_w_a[l, dd]), _block_diag(lru_w_i[l, dd])], axis=1)
                        for dd in range(2)]).astype(BF16)
        bg = jnp.concatenate([lru_b_a[l], lru_b_i[l]], axis=1).reshape(2, 1, 2 * LRU_WIDTH)
        hf, hb = _lru(lu, wg, bg, lru_lambda[l].reshape(2, 1, LRU_WIDTH), batch=batch, seq=seq, ctx_len=ctx_len)

        oa = _window_attn(qa, kat, va, a_sink[l], batch=batch, seq=seq, ctx_len=ctx_len)
        od = _dense_attn(qd, kdt, vd, None, q_row0=0, q_len=seq, tq=TQ_GLOBAL,
                         segs=[(n_lat, ctx_len), (0, seq)], batch=batch)
        if with_ctx:
            oa_c = _dense_attn(qa, kat, va, a_sink[l], q_row0=n_lat, q_len=ctx_len, tq=ctx_len,
                               segs=[(n_lat, ctx_len)], batch=batch)
            od_c = _dense_attn(qd, kdt, vd, None, q_row0=n_lat, q_len=ctx_len, tq=ctx_len,
                               segs=[(n_lat, ctx_len)], batch=batch)
            oa = jnp.concatenate([oa, oa_c], axis=0)
            od = jnp.concatenate([od, od_c], axis=0)
        n_rows = n_all if with_ctx else n_lat

        dsk = jnp.repeat(ssd_d[l], HEAD_DIM).reshape(1, 256)
        xu_mid = _outproj(xu, mod, g_mix_post[l].reshape(1, d), oa, od, yf, yb, xbc, z, dsk,
                          ssd_norm[l].reshape(1, 256), hf, hb, lg, w_out[l].astype(BF16),
                          n_rows=n_rows, n_lat=n_lat, seq=seq, batch=batch)

        hb_ffn, ld, wk, tab = _router(xu_mid, mod, g_ffn_pre[l].reshape(1, d), router_w[l].T.astype(BF16),
                                      router_bias[l].reshape(N_EXPERTS, 1), n_rows=n_rows, n_lat=n_lat,
                                      seq=seq, batch=batch)
        n_tiles = n_rows // TM
        tab = tab.reshape(n_tiles, SUBLANES, 2 * N_EXPERTS)[:, 0, :]
        seg_cnt, seg_loc = tab[:, :N_EXPERTS], tab[:, N_EXPERTS:]
        counts = jnp.sum(seg_cnt, axis=0)
        padded = (counts + BM_EXPERT - 1) // BM_EXPERT * BM_EXPERT
        padded_end = jnp.cumsum(padded)
        offs = padded_end - padded
        seg_off = offs[None, :] + jnp.cumsum(seg_cnt, axis=0) - seg_cnt
        n_blocks = (n_rows * TOP_K + n_tiles * N_EXPERTS * SEG_ALIGN) // BM_EXPERT + N_EXPERTS
        n_active = (padded_end[-1] // BM_EXPERT).astype(jnp.int32).reshape(1)
        block_start = jnp.arange(n_blocks, dtype=jnp.int32) * BM_EXPERT
        block_e = jnp.minimum(jnp.sum((padded_end[None, :] <= block_start[:, None]).astype(jnp.int32), axis=1),
                              N_EXPERTS - 1)
        ids = jnp.arange(N_EXPERTS, dtype=jnp.int32)
        n_big = seg_cnt // BIG_PIECE
        n_small = (seg_cnt % BIG_PIECE) // SEG_ALIGN
        tabs = (jnp.sum(n_big, axis=1), jnp.sum(n_small, axis=1), seg_loc[:, -1] + seg_cnt[:, -1],
                *_piece_table(n_big, BIG_MAX, seg_loc, seg_off, BIG_PIECE, ids),
                *_piece_table(n_small, SMALL_MAX, seg_loc + n_big * BIG_PIECE, seg_off + n_big * BIG_PIECE,
                              SEG_ALIGN, ids))
        xs = _dispatch(tabs, offs + counts, padded - counts, hb_ffn, ld, n_blocks * BM_EXPERT)
        has_rows = padded > 0
        later = jnp.logical_and(ids[None, :] > ids[:, None], has_rows[None, :])
        nxt_of = jnp.min(jnp.where(later, ids[None, :], N_EXPERTS), axis=1)
        nxt_of = jnp.where(nxt_of == N_EXPERTS, ids, nxt_of)
        slot_of = (jnp.cumsum(has_rows.astype(jnp.int32)) - 1) & 1
        own = block_e[:, None] == ids[None, :]
        next_e = jnp.sum(jnp.where(own, nxt_of[None, :], 0), axis=1)
        w_slot = jnp.sum(jnp.where(own, slot_of[None, :], 0), axis=1)
        ys = _experts(block_e, n_active, next_e, w_slot, xs, exp_w_gate, exp_w_up, exp_w_down, l)
        xu = _combine(tabs, ys, ld, wk, hb_ffn, xu_mid, mod, g_ffn_post[l].reshape(1, d), sh_w_gate[l].astype(BF16),
                      sh_w_up[l].astype(BF16), sh_w_down[l].astype(BF16),
                      n_rows=n_rows, n_lat=n_lat, seq=seq, batch=batch)
    return xu[:n_lat].reshape(batch, seq, d)
```

```python
import functools
import math

import jax
import jax.numpy as jnp
from jax import lax
from jax.experimental import pallas as pl
from jax.experimental.pallas import tpu as pltpu

F32 = jnp.float32
BF16 = jnp.bfloat16

HEAD_DIM = 64
GRID_W = 64
ROPE_THETA = 10000.0
NORM_EPS = 1e-6
NEG_INF = -1e30
A_HEADS, A_KV_HEADS, WINDOW = 4, 2, 128
SSD_HEADS, SSD_GROUPS, SSD_STATE, SSD_CONV = 4, 2, 64, 4
LRU_WIDTH, LRU_BLOCKS, LRU_CONV, LRU_C = 256, 4, 4, 8.0
D_HEADS, D_KV_HEADS = 4, 2
N_EXPERTS, N_EXPERT_GROUPS, TOPK_GROUPS, TOP_K = 64, 8, 4, 8
EXPERT_HIDDEN, SHARED_HIDDEN = 256, 256
ROUTED_SCALE = 2.5

LANES = 128
SUBLANES = 8

TM = 512
T_CONV = 256
CHUNK = 128
SCAN_CHUNKS = 2
TQ_GLOBAL = 256
TQ_WINDOW = 512
KV_CHUNK = 256
KV_UNROLL = 16
BM_EXPERT = 512
SEG_ALIGN = 16
STAGE_ROWS = TM * TOP_K + N_EXPERTS * SEG_ALIGN
BIG_PIECE = 64
BIG_MAX = STAGE_ROWS // BIG_PIECE
SMALL_MAX = N_EXPERTS * (BIG_PIECE // SEG_ALIGN - 1)
VMEM_LIMIT = 48 * 1024 * 1024

C_QA, C_KA, C_VA = 0, 256, 384
C_QD, C_KD, C_VD = 512, 768, 896
C_Z, C_XBC, C_DT = 1024, 1280, 1792
C_LX, C_LG = 1920, 2176
NP_IN = 2432


def _dot(a, b):
    return jnp.dot(a, b, preferred_element_type=F32)


def _dot_nt(a, b):
    return lax.dot_general(a, b, (((1,), (1,)), ((), ())), preferred_element_type=F32)


def _dot3(a, b):
    a1 = a.astype(BF16)
    r1 = a - a1.astype(F32)
    a2 = r1.astype(BF16)
    a3 = (r1 - a2.astype(F32)).astype(BF16)
    return _dot(a1, b) + _dot(a2, b) + _dot(a3, b)


def _dot3_left(a, b):
    b1 = b.astype(BF16)
    r1 = b - b1.astype(F32)
    b2 = r1.astype(BF16)
    b3 = (r1 - b2.astype(F32)).astype(BF16)
    return _dot(a, b1) + _dot(a, b2) + _dot(a, b3)


def _silu(x):
    return x * jax.nn.sigmoid(x)


def _softplus(x):
    return jnp.maximum(x, 0.0) + jnp.log1p(jnp.exp(-jnp.abs(x)))


def _rms(x, gain):
    return x * lax.rsqrt(jnp.mean(x * x, axis=-1, keepdims=True) + NORM_EPS) * gain


def _params(sem=None):
    return pltpu.CompilerParams(dimension_semantics=sem, vmem_limit_bytes=VMEM_LIMIT)


def _adaln_kernel(c_ref, w_ref, b_ref, o_ref):
    s = _silu(c_ref[...])
    o_ref[0] = _dot(s.astype(BF16), w_ref[0].astype(BF16)) + b_ref[0]


def _adaln(cin, w_ada, b_ada):
    depth, d, n6 = w_ada.shape
    tn = 1024
    return pl.pallas_call(
        _adaln_kernel,
        grid=(depth, n6 // tn),
        in_specs=[pl.BlockSpec((SUBLANES, d), lambda l, j: (0, 0)),
                  pl.BlockSpec((1, d, tn), lambda l, j: (l, 0, j)),
                  pl.BlockSpec((1, 1, tn), lambda l, j: (l, 0, j))],
        out_specs=pl.BlockSpec((1, SUBLANES, tn), lambda l, j: (l, 0, j)),
        out_shape=jax.ShapeDtypeStruct((depth, SUBLANES, n6), F32),
        compiler_params=_params(("parallel", "parallel")),
        name="adaln",
    )(cin, w_ada, b_ada.reshape(depth, 1, n6))


def _swap_halves(t):
    w = t.shape[1]
    lane = lax.broadcasted_iota(jnp.int32, (1, w), 1)
    first = (lane & 32) == 0
    return jnp.where(first, pltpu.roll(t, w - 32, axis=1), pltpu.roll(t, 32, axis=1))


def _inproj_kernel(x_ref, shift_ref, scale_ref, gpre_ref, w_ref, cos_ref, sin_ref, gq_ref, gk_ref, hm_ref,
                   qa_ref, kat_ref, va_ref, qd_ref, kdt_ref, vd_ref, z_ref, xbc_ref, dt_ref, lx_ref, lg_ref,
                   *, n_lat):
    i = pl.program_id(0)
    is_lat = i * TM < n_lat
    h = _rms(x_ref[...], gpre_ref[...])
    h = h * (1.0 + scale_ref[0]) + shift_ref[0]
    hb = h.astype(BF16)

    def sec(a, b):
        return _dot(hb, w_ref[:, a:b])

    cos = jnp.where(is_lat, cos_ref[...], 1.0)
    sin = jnp.where(is_lat, sin_ref[...], 0.0)

    def rope(t):
        w = t.shape[1]
        return t * cos[:, :w] + _swap_halves(t) * sin[:, :w]

    def head_norm(t, gain):
        w = t.shape[1]
        ms = _dot3(t * t, hm_ref[:w, :w])
        return t * lax.rsqrt(ms + NORM_EPS) * gain

    scale = HEAD_DIM ** -0.5
    qa_ref[...] = (rope(sec(C_QA, C_KA)) * scale).astype(BF16)
    kat_ref[...] = rope(sec(C_KA, C_VA)).T.astype(BF16)
    va_ref[...] = sec(C_VA, C_QD).astype(BF16)
    qd_ref[...] = (rope(head_norm(sec(C_QD, C_KD), gq_ref[...])) * scale).astype(BF16)
    kdt_ref[...] = rope(head_norm(sec(C_KD, C_VD), gk_ref[...])).T.astype(BF16)
    vd_ref[...] = sec(C_VD, C_Z).astype(BF16)
    z_ref[...] = sec(C_Z, C_XBC)
    xbc_ref[...] = sec(C_XBC, C_DT)
    dt_ref[...] = sec(C_DT, C_LX)
    lx_ref[...] = sec(C_LX, C_LG)
    lg_ref[...] = sec(C_LG, NP_IN)


def _mod_spec(chunk, n_lat, seq, batch, tile):
    def imap(i):
        row0 = i * tile
        seg = jnp.where(row0 < n_lat, row0 // seq, batch)
        return (seg * 6 + chunk, 0, 0)
    return imap


def _inproj(xu, mod, gpre, w_pad, cos_t, sin_t, gq, gk, hm, *, n_lat, seq, batch):
    n, d = xu.shape
    nt = n // TM
    spt = seq // TM
    row = lambda w: pl.BlockSpec((TM, w), lambda i: (i, 0))
    colT = pl.BlockSpec((LANES, TM), lambda i: (0, i))
    const = lambda a: pl.BlockSpec(a.shape, lambda i: (0,) * a.ndim)
    out_shapes = (
        jax.ShapeDtypeStruct((n, 256), BF16), jax.ShapeDtypeStruct((LANES, n), BF16),
        jax.ShapeDtypeStruct((n, LANES), BF16),
        jax.ShapeDtypeStruct((n, 256), BF16), jax.ShapeDtypeStruct((LANES, n), BF16),
        jax.ShapeDtypeStruct((n, LANES), BF16),
        jax.ShapeDtypeStruct((n, 256), F32), jax.ShapeDtypeStruct((n, 512), F32),
        jax.ShapeDtypeStruct((n, LANES), F32), jax.ShapeDtypeStruct((n, 256), F32),
        jax.ShapeDtypeStruct((n, 256), F32))
    return pl.pallas_call(
        functools.partial(_inproj_kernel, n_lat=n_lat),
        grid=(nt,),
        in_specs=[row(d),
                  pl.BlockSpec((1, 1, d), _mod_spec(0, n_lat, seq, batch, TM)),
                  pl.BlockSpec((1, 1, d), _mod_spec(1, n_lat, seq, batch, TM)),
                  const(gpre), const(w_pad),
                  pl.BlockSpec((TM, 256), lambda i: (i % spt, 0)),
                  pl.BlockSpec((TM, 256), lambda i: (i % spt, 0)),
                  const(gq), const(gk), const(hm)],
        out_specs=(row(256), colT, row(LANES), row(256), colT, row(LANES),
                   row(256), row(512), row(LANES), row(256), row(256)),
        out_shape=out_shapes,
        compiler_params=_params(("parallel",)),
        name="inproj",
    )(xu, mod, mod, gpre, w_pad, cos_t, sin_t, gq, gk, hm)


def _conv_kernel(xs_ref, xsp_ref, xsn_ref, xl_ref, xlp_ref, xln_ref, ws_ref, bs_ref, wl_ref, bl_ref,
                 os_ref, ol_ref, *, n_lat, seq, ctx_len):
    i = pl.program_id(0)
    row0 = i * T_CONV
    pos = jnp.where(row0 < n_lat, row0 % seq, (row0 - n_lat) % ctx_len)
    slen = jnp.where(row0 < n_lat, seq, ctx_len)
    first = pos == 0
    last = pos + T_CONV == slen
    row = lax.broadcasted_iota(jnp.int32, (T_CONV, 1), 0)

    def conv(x, prev, nxt, w, b):
        pm = jnp.where(first, 0.0, prev)
        nx = jnp.where(last, 0.0, nxt)
        xm1 = jnp.where(row == 0, pm[7:8, :], pltpu.roll(x, 1, axis=0))
        xm2 = jnp.where(row == 0, pm[6:7, :], jnp.where(row == 1, pm[7:8, :], pltpu.roll(x, 2, axis=0)))
        xp1 = jnp.where(row == T_CONV - 1, nx[0:1, :], pltpu.roll(x, T_CONV - 1, axis=0))
        return w[0:1, :] * xm2 + w[1:2, :] * xm1 + w[2:3, :] * x + w[3:4, :] * xp1 + b

    os_ref[...] = _silu(conv(xs_ref[...], xsp_ref[...], xsn_ref[...], ws_ref[...], bs_ref[...]))
    ol_ref[...] = conv(xl_ref[...], xlp_ref[...], xln_ref[...], wl_ref[...], bl_ref[...])


def _conv(xbc_raw, lx_raw, ws, bs, wl, bl, *, n_lat, seq, ctx_len):
    n = xbc_raw.shape[0]
    nt = n // T_CONV
    r8 = T_CONV // SUBLANES
    n8 = n // SUBLANES
    main = lambda w: pl.BlockSpec((T_CONV, w), lambda i: (i, 0))
    prev = lambda w: pl.BlockSpec((SUBLANES, w), lambda i: (jnp.maximum(i * r8 - 1, 0), 0))
    nxt = lambda w: pl.BlockSpec((SUBLANES, w), lambda i: (jnp.minimum((i + 1) * r8, n8 - 1), 0))
    const = lambda a: pl.BlockSpec(a.shape, lambda i: (0,) * a.ndim)
    return pl.pallas_call(
        functools.partial(_conv_kernel, n_lat=n_lat, seq=seq, ctx_len=ctx_len),
        grid=(nt,),
        in_specs=[main(512), prev(512), nxt(512), main(256), prev(256), nxt(256),
                  const(ws), const(bs), const(wl), const(bl)],
        out_specs=(main(512), main(256)),
        out_shape=(jax.ShapeDtypeStruct((n, 512), F32), jax.ShapeDtypeStruct((n, 256), F32)),
        compiler_params=_params(("parallel",)),
        name="conv",
    )(xbc_raw, xbc_raw, xbc_raw, lx_raw, lx_raw, lx_raw, ws, bs, wl, bl)


def _chunk_maps(batch, seq, ctx_len):
    ncx = ctx_len // (SCAN_CHUNKS * CHUNK)
    nl = seq // (SCAN_CHUNKS * CHUNK)
    lat_blocks = batch * nl

    def block(b, c):
        return jnp.where(c < ncx, lat_blocks + b * ncx + c, b * nl + (c - ncx))

    def fwd(b, k):
        return (block(b, k), 0)

    def bwd(b, k):
        c = jnp.where(k < ncx, ncx - 1 - k, ncx + (nl - 1 - (k - ncx)))
        return (block(b, c), 0)

    return fwd, bwd, ncx + nl


def _ssd_kernel(xf_ref, dtf_ref, xb_ref, dtb_ref, dtbias_ref, alog_ref, yf_ref, yb_ref, state_ref):
    k = pl.program_id(1)

    @pl.when(k == 0)
    def _():
        state_ref[...] = jnp.zeros_like(state_ref)

    ri = lax.broadcasted_iota(jnp.int32, (CHUNK, CHUNK), 0)
    ci = lax.broadcasted_iota(jnp.int32, (CHUNK, CHUNK), 1)
    lane_lo = ci < HEAD_DIM
    aneg = -jnp.exp(alog_ref[...])
    dtbias = dtbias_ref[...]

    order = [(d, s if d == 0 else SCAN_CHUNKS - 1 - s) for s in range(SCAN_CHUNKS) for d in range(2)]
    for d, sub in order:
        x_ref, dt_ref, y_ref = ((xf_ref, dtf_ref, yf_ref), (xb_ref, dtb_ref, yb_ref))[d]
        rws = slice(sub * CHUNK, (sub + 1) * CHUNK)
        causal = (ri >= ci) if d == 0 else (ci >= ri)
        tmat = jnp.where(causal, 1.0, 0.0).astype(BF16)
        xs = x_ref[rws, 0:256]
        bm = x_ref[rws, 256:384]
        cm = x_ref[rws, 384:512]
        dtp = _softplus(dt_ref[rws, :] + dtbias)
        acum = _dot3_left(tmat, dtp * aneg)
        acum_t = acum.T
        bt = bm.T.astype(BF16)
        cmb = cm.astype(BF16)
        bmb = bm.astype(BF16)
        tot_row = CHUNK - 1 if d == 0 else 0
        for p in range(2):
            cmask = jnp.where(lane_lo if p == 0 else jnp.logical_not(lane_lo), cmb, jnp.zeros_like(cmb))
            cb = _dot_nt(cmask, bmb)
            cols, dts, ys = [], [], []
            x_pair = xs[:, p * LANES:(p + 1) * LANES]
            for j in range(2):
                col = 4 * d + 2 * p + j
                colb = jnp.broadcast_to(acum[:, col:col + 1], (CHUNK, CHUNK))
                rowb = jnp.broadcast_to(acum_t[col:col + 1, :], (CHUNK, CHUNK))
                cols.append(colb)
                dts.append(jnp.broadcast_to(dtp[:, col:col + 1], (CHUNK, CHUNK)))
            col_pair = jnp.where(lane_lo, cols[0], cols[1])
            dt_pair = jnp.where(lane_lo, dts[0], dts[1])
            xdt = x_pair * dt_pair
            xdt_b = xdt.astype(BF16)
            for j in range(2):
                col = 4 * d + 2 * p + j
                rowb = jnp.broadcast_to(acum_t[col:col + 1, :], (CHUNK, CHUNK))
                decay = jnp.exp(jnp.where(causal, cols[j] - rowb, NEG_INF))
                ys.append(_dot((cb * decay).astype(BF16), xdt_b))
            y_intra = jnp.where(lane_lo, ys[0], ys[1])
            s_old = state_ref[d, p]
            y_inter = _dot(cmask, s_old.astype(BF16)) * jnp.exp(col_pair)
            y_ref[rws, p * LANES:(p + 1) * LANES] = y_intra + y_inter
            tot_pair = col_pair[tot_row:tot_row + 1, :]
            to_end = jnp.exp(tot_pair - col_pair)
            state_ref[d, p] = s_old * jnp.exp(tot_pair) + _dot(bt, (xdt * to_end).astype(BF16))


def _ssd(xbc, dt, dtbias_row, alog_row, *, batch, seq, ctx_len):
    n = xbc.shape[0]
    fwd, bwd, steps = _chunk_maps(batch, seq, ctx_len)
    rows = SCAN_CHUNKS * CHUNK
    const = lambda a: pl.BlockSpec(a.shape, lambda b, k: (0,) * a.ndim)
    return pl.pallas_call(
        _ssd_kernel,
        grid=(batch, steps),
        in_specs=[pl.BlockSpec((rows, 512), fwd), pl.BlockSpec((rows, LANES), fwd),
                  pl.BlockSpec((rows, 512), bwd), pl.BlockSpec((rows, LANES), bwd),
                  const(dtbias_row), const(alog_row)],
        out_specs=(pl.BlockSpec((rows, 256), fwd), pl.BlockSpec((rows, 256), bwd)),
        out_shape=(jax.ShapeDtypeStruct((n, 256), F32), jax.ShapeDtypeStruct((n, 256), F32)),
        scratch_shapes=[pltpu.VMEM((2, 2, CHUNK, LANES), F32)],
        compiler_params=_params(("parallel", "arbitrary")),
        name="ssd_scan",
    )(xbc, dt, xbc, dt, dtbias_row, alog_row)


def _linear_scan(a, b, reverse):
    n = a.shape[0]
    row = lax.broadcasted_iota(jnp.int32, (n, 1), 0)
    s = 1
    while s < n:
        if reverse:
            ok = row < n - s
            a_sh = jnp.where(ok, pltpu.roll(a, n - s, axis=0), 1.0)
            b_sh = jnp.where(ok, pltpu.roll(b, n - s, axis=0), 0.0)
        else:
            ok = row >= s
            a_sh = jnp.where(ok, pltpu.roll(a, s, axis=0), 1.0)
            b_sh = jnp.where(ok, pltpu.roll(b, s, axis=0), 0.0)
        b = b + a * b_sh
        a = a * a_sh
        s *= 2
    return a, b


def _lru_kernel(uf_ref, ub_ref, wg_ref, bg_ref, lam_ref, hf_ref, hb_ref, carry_ref):
    k = pl.program_id(1)

    @pl.when(k == 0)
    def _():
        carry_ref[...] = jnp.zeros_like(carry_ref)

    for d, (u_ref, h_ref) in enumerate(((uf_ref, hf_ref), (ub_ref, hb_ref))):
        u = u_ref[...]
        gates = _dot(u.astype(BF16), wg_ref[d]) + bg_ref[d]
        r = jax.nn.sigmoid(gates[:, :LRU_WIDTH])
        ig = jax.nn.sigmoid(gates[:, LRU_WIDTH:])
        log_a = -LRU_C * r * _softplus(-lam_ref[d])
        a = jnp.exp(log_a)
        inp = jnp.sqrt(-jnp.tanh(log_a) * (1.0 + a * a)) * (ig * u)
        a_cum, b_cum = _linear_scan(a, inp, reverse=(d == 1))
        h = b_cum + a_cum * carry_ref[d, 0:1, :]
        h_ref[...] = h
        last = 0 if d == 1 else u.shape[0] - 1
        carry_ref[d, 0:1, :] = h[last:last + 1, :]


def _lru(u, wg, bg, lam, *, batch, seq, ctx_len):
    n = u.shape[0]
    fwd, bwd, steps = _chunk_maps(batch, seq, ctx_len)
    rows = SCAN_CHUNKS * CHUNK
    const = lambda a: pl.BlockSpec(a.shape, lambda b, k: (0,) * a.ndim)
    return pl.pallas_call(
        _lru_kernel,
        grid=(batch, steps),
        in_specs=[pl.BlockSpec((rows, LRU_WIDTH), fwd), pl.BlockSpec((rows, LRU_WIDTH), bwd),
                  const(wg), const(bg), const(lam)],
        out_specs=(pl.BlockSpec((rows, LRU_WIDTH), fwd), pl.BlockSpec((rows, LRU_WIDTH), bwd)),
        out_shape=(jax.ShapeDtypeStruct((n, LRU_WIDTH), F32), jax.ShapeDtypeStruct((n, LRU_WIDTH), F32)),
        scratch_shapes=[pltpu.VMEM((2, SUBLANES, LRU_WIDTH), F32)],
        compiler_params=_params(("parallel", "arbitrary")),
        name="lru_scan",
    )(u, u, wg, bg, lam)


def _stack_heads(q, g):
    qf = q.astype(F32)
    lo = g * LANES
    return jnp.concatenate([qf[:, lo:lo + HEAD_DIM], qf[:, lo + HEAD_DIM:lo + LANES]], axis=0).astype(BF16)


def _value_lanes(g):
    lane = lax.broadcasted_iota(jnp.int32, (1, LANES), 1)
    return (lane < HEAD_DIM) if g == 0 else (lane >= HEAD_DIM)


def _aug_values(v, g):
    return jnp.where(_value_lanes(g), v, jnp.ones_like(v))


def _flash_init(rows, g, sink_pair):
    if sink_pair is None:
        return jnp.full((rows, 1), NEG_INF, F32), jnp.zeros((rows, LANES), F32)
    half = lax.broadcasted_iota(jnp.int32, (rows, 1), 0) < rows // 2
    m = jnp.where(half, sink_pair[0], sink_pair[1]).astype(F32)
    acc = jnp.broadcast_to(jnp.where(_value_lanes(g), 0.0, 1.0), (rows, LANES))
    return m, acc


def _flash_update(state, q2, kt, v_aug, mask=None):
    m, acc = state
    s = _dot(q2, kt)
    if mask is not None:
        s = jnp.where(mask, s, NEG_INF)
    m_new = jnp.maximum(m, jnp.max(s, axis=-1, keepdims=True))
    p = jnp.exp(s - m_new).astype(BF16)
    acc = jnp.exp(m - m_new) * acc + _dot(p, v_aug)
    return m_new, acc


def _flash_finish(states, tq):
    pieces = []
    for g, (_, acc) in enumerate(states):
        den = (1 - g) * HEAD_DIM
        o = acc[:, g * HEAD_DIM:(g + 1) * HEAD_DIM] / acc[:, den:den + 1]
        pieces += [o[:tq], o[tq:]]
    return jnp.concatenate(pieces, axis=1)


def _group_rows(g):
    return slice(g * HEAD_DIM, (g + 1) * HEAD_DIM)


def _dense_attn_kernel(*refs, tq, seg_lens, has_sink):
    refs = list(refs)
    sink_ref = refs.pop(0) if has_sink else None
    q_ref = refs.pop(0)
    o_ref = refs.pop()
    segs = [(refs[2 * i], refs[2 * i + 1], n) for i, n in enumerate(seg_lens)]
    q = q_ref[...]
    q2 = [_stack_heads(q, g) for g in range(2)]
    states = tuple(_flash_init(2 * tq, g, (sink_ref[2 * g], sink_ref[2 * g + 1]) if has_sink else None)
                   for g in range(2))
    for kt_ref, v_ref, n_keys in segs:
        if n_keys <= KV_CHUNK:
            v = v_ref[...]
            states = tuple(_flash_update(states[g], q2[g], kt_ref[_group_rows(g), :], _aug_values(v, g))
                           for g in range(2))
        else:
            def body(c, sts, kt_ref=kt_ref, v_ref=v_ref):
                off = pl.multiple_of(c * KV_CHUNK, KV_CHUNK)
                v = v_ref[pl.ds(off, KV_CHUNK), :]
                return tuple(_flash_update(sts[g], q2[g], kt_ref[_group_rows(g), pl.ds(off, KV_CHUNK)],
                                           _aug_values(v, g)) for g in range(2))
            states = lax.fori_loop(0, n_keys // KV_CHUNK, body, states, unroll=KV_UNROLL)
    o_ref[...] = _flash_finish(states, tq).astype(o_ref.dtype)


def _dense_attn(q, kt, v, sink, *, q_row0, q_len, tq, segs, batch):
    n = q.shape[0]
    qpb = q_len // tq
    q0 = q_row0 // tq
    in_specs, args = [], []
    if sink is not None:
        in_specs.append(pl.BlockSpec(memory_space=pltpu.SMEM))
        args.append(sink)
    in_specs.append(pl.BlockSpec((tq, 256), lambda b, i: (q0 + b * qpb + i, 0)))
    args.append(q)
    for row0, klen in segs:
        k0 = row0 // klen
        in_specs.append(pl.BlockSpec((LANES, klen), lambda b, i, k0=k0: (0, k0 + b)))
        in_specs.append(pl.BlockSpec((klen, LANES), lambda b, i, k0=k0: (k0 + b, 0)))
        args += [kt, v]
    return pl.pallas_call(
        functools.partial(_dense_attn_kernel, tq=tq, seg_lens=tuple(s[1] for s in segs), has_sink=sink is not None),
        grid=(batch, qpb),
        in_specs=in_specs,
        out_specs=pl.BlockSpec((tq, 256), lambda b, i: (b * qpb + i, 0)),
        out_shape=jax.ShapeDtypeStruct((batch * q_len, 256), BF16),
        compiler_params=_params(("parallel", "parallel")),
        name="dense_attn",
    )(*args)


def _window_attn_kernel(sink_ref, q_ref, ktc_ref, vc_ref, ktp_ref, vp_ref, ktm_ref, vm_ref, ktn_ref, vn_ref, o_ref,
                        *, n_tiles):
    n = pl.program_id(1)
    nsub = TQ_WINDOW // CHUNK
    iq = lax.broadcasted_iota(jnp.int32, (2 * CHUNK, CHUNK), 0) & (CHUNK - 1)
    jk = lax.broadcasted_iota(jnp.int32, (2 * CHUNK, CHUNK), 1)
    below = jk >= iq
    above = jk <= iq
    vctx = vc_ref[...]
    for j in range(nsub):
        cols = slice(j * CHUNK, (j + 1) * CHUNK)
        q = q_ref[cols, :]
        states = []
        for g in range(2):
            q2 = _stack_heads(q, g)
            rows = _group_rows(g)
            state = _flash_init(2 * CHUNK, g, (sink_ref[2 * g], sink_ref[2 * g + 1]))
            state = _flash_update(state, q2, ktc_ref[rows, :], _aug_values(vctx, g))
            state = _flash_update(state, q2, ktm_ref[rows, cols], _aug_values(vm_ref[cols, :], g))
            if j > 0:
                prev = slice((j - 1) * CHUNK, j * CHUNK)
                state = _flash_update(state, q2, ktm_ref[rows, prev], _aug_values(vm_ref[prev, :], g), below)
            else:
                state = _flash_update(state, q2, ktp_ref[rows, :], _aug_values(vp_ref[...], g),
                                      jnp.logical_and(below, n > 0))
            if j < nsub - 1:
                nxt = slice((j + 1) * CHUNK, (j + 2) * CHUNK)
                state = _flash_update(state, q2, ktm_ref[rows, nxt], _aug_values(vm_ref[nxt, :], g), above)
            else:
                state = _flash_update(state, q2, ktn_ref[rows, :], _aug_values(vn_ref[...], g),
                                      jnp.logical_and(above, n < n_tiles - 1))
            states.append(state)
        o_ref[cols, :] = _flash_finish(states, CHUNK).astype(o_ref.dtype)


def _window_attn(q, kt, v, sink, *, batch, seq, ctx_len):
    nt = seq // TQ_WINDOW
    nsub = TQ_WINDOW // CHUNK
    nb = seq // CHUNK
    ctx0 = (batch * seq) // ctx_len
    prev = lambda b, n: b * nb + jnp.maximum(n * nsub - 1, 0)
    nxt = lambda b, n: b * nb + jnp.minimum((n + 1) * nsub, nb - 1)
    return pl.pallas_call(
        functools.partial(_window_attn_kernel, n_tiles=nt),
        grid=(batch, nt),
        in_specs=[pl.BlockSpec(memory_space=pltpu.SMEM),
                  pl.BlockSpec((TQ_WINDOW, 256), lambda b, n: (b * nt + n, 0)),
                  pl.BlockSpec((LANES, ctx_len), lambda b, n: (0, ctx0 + b)),
                  pl.BlockSpec((ctx_len, LANES), lambda b, n: (ctx0 + b, 0)),
                  pl.BlockSpec((LANES, CHUNK), lambda b, n: (0, prev(b, n))),
                  pl.BlockSpec((CHUNK, LANES), lambda b, n: (prev(b, n), 0)),
                  pl.BlockSpec((LANES, TQ_WINDOW), lambda b, n: (0, b * nt + n)),
                  pl.BlockSpec((TQ_WINDOW, LANES), lambda b, n: (b * nt + n, 0)),
                  pl.BlockSpec((LANES, CHUNK), lambda b, n: (0, nxt(b, n))),
                  pl.BlockSpec((CHUNK, LANES), lambda b, n: (nxt(b, n), 0))],
        out_specs=pl.BlockSpec((TQ_WINDOW, 256), lambda b, n: (b * nt + n, 0)),
        out_shape=jax.ShapeDtypeStruct((batch * seq, 256), BF16),
        compiler_params=_params(("parallel", "parallel")),
        name="window_attn",
    )(sink, q, kt, v, kt, v, kt, v, kt, v)


def _gelu_tanh(x):
    return 0.5 * x * (1.0 + jnp.tanh(math.sqrt(2.0 / math.pi) * (x + 0.044715 * (x * x * x))))


def _outproj_kernel(x_ref, gate_ref, gpost_ref, oa_ref, od_ref, yf_ref, yb_ref, xs_ref, z_ref, dsk_ref, gn_ref,
                    hf_ref, hb_ref, lg_ref, w_ref, o_ref):
    y_ssd = (yf_ref[...] + yb_ref[...] + xs_ref[...] * dsk_ref[...]) * _silu(z_ref[...])
    ob = _rms(y_ssd, gn_ref[...])
    oc = (hf_ref[...] + hb_ref[...]) * _gelu_tanh(lg_ref[...])
    y = (_dot(oa_ref[...], w_ref[0:256, :]) + _dot(ob.astype(BF16), w_ref[256:512, :])
         + _dot(oc.astype(BF16), w_ref[512:768, :]) + _dot(od_ref[...], w_ref[768:1024, :]))
    o_ref[...] = x_ref[...] + gate_ref[0] * _rms(y, gpost_ref[...])


def _outproj(xu, mod, gpost, oa, od, yf, yb, xbc, z, dsk, gn, hf, hb, lg, w_out, *, n_rows, n_lat, seq, batch):
    d = xu.shape[1]
    row = lambda w: pl.BlockSpec((TM, w), lambda i: (i, 0))
    const = lambda a: pl.BlockSpec(a.shape, lambda i: (0,) * a.ndim)
    return pl.pallas_call(
        _outproj_kernel,
        grid=(n_rows // TM,),
        in_specs=[row(d), pl.BlockSpec((1, 1, d), _mod_spec(2, n_lat, seq, batch, TM)), const(gpost),
                  row(256), row(256), row(256), row(256), row(256), row(256), const(dsk), const(gn),
                  row(256), row(256), row(256), const(w_out)],
        out_specs=row(d),
        out_shape=jax.ShapeDtypeStruct((n_rows, d), F32),
        compiler_params=_params(("parallel",)),
        name="outproj",
    )(xu, mod, gpost, oa, od, yf, yb, xbc, z, dsk, gn, hf, hb, lg, w_out)


def _ceil_seg(c):
    return jnp.floor((c + (SEG_ALIGN - 1)) * (1.0 / SEG_ALIGN)) * SEG_ALIGN


def _router_kernel(x_ref, shift_ref, scale_ref, gpre_ref, rwt_ref, rb_ref, hb_ref, ld_ref, wk_ref, tab_ref):
    h = _rms(x_ref[...], gpre_ref[...])
    h = h * (1.0 + scale_ref[0]) + shift_ref[0]
    hb = h.astype(BF16)
    hb_ref[...] = hb

    scores = jax.nn.sigmoid(_dot_nt(rwt_ref[...], hb))
    biased = scores + rb_ref[...]
    gsz = N_EXPERTS // N_EXPERT_GROUPS
    sub = lax.broadcasted_iota(jnp.int32, (gsz, TM), 0)
    blocks, gscore = [], []
    for g in range(N_EXPERT_GROUPS):
        blk = biased[g * gsz:(g + 1) * gsz, :]
        m1 = jnp.max(blk, axis=0, keepdims=True)
        first = jnp.min(jnp.where(blk == m1, sub, gsz), axis=0, keepdims=True)
        m2 = jnp.max(jnp.where(sub == first, -jnp.inf, blk), axis=0, keepdims=True)
        blocks.append(blk)
        gscore.append(m1 + m2)
    masked = []
    for g in range(N_EXPERT_GROUPS):
        rank = jnp.zeros((1, TM), F32)
        for g2 in range(N_EXPERT_GROUPS):
            if g2 == g:
                continue
            beats = (gscore[g2] > gscore[g]) | ((gscore[g2] == gscore[g]) if g2 < g else False)
            rank = rank + jnp.where(beats, 1.0, 0.0)
        masked.append(jnp.where(rank < TOPK_GROUPS, blocks[g], -jnp.inf))
    vals = jnp.concatenate(masked, axis=0)
    eidx = lax.broadcasted_iota(jnp.int32, (N_EXPERTS, TM), 0)
    self32 = jnp.zeros((N_EXPERTS, TM), F32)
    rest = vals
    for _ in range(TOP_K):
        top = jnp.max(rest, axis=0, keepdims=True)
        first = jnp.min(jnp.where(rest == top, eidx, N_EXPERTS), axis=0, keepdims=True)
        hit = eidx == first
        self32 = jnp.where(hit, 1.0, self32)
        rest = jnp.where(hit, -jnp.inf, rest)
    sel = self32 > 0.5
    picked = jnp.where(sel, scores, 0.0)
    wdense = picked / jnp.sum(picked, axis=0, keepdims=True) * ROUTED_SCALE

    tr = lax.broadcasted_iota(jnp.int32, (TM, TM), 0)
    tc = lax.broadcasted_iota(jnp.int32, (TM, TM), 1)
    before = jnp.where(tr < tc, 1.0, 0.0).astype(BF16)
    selb = self32.astype(BF16)
    pos = _dot(selb, before)
    er = lax.broadcasted_iota(jnp.int32, (N_EXPERTS, N_EXPERTS), 0)
    ec = lax.broadcasted_iota(jnp.int32, (N_EXPERTS, N_EXPERTS), 1)
    lower = jnp.where(ec < er, 1.0, 0.0).astype(BF16)
    upper = jnp.where(er < ec, 1.0, 0.0).astype(BF16)
    ksel = _dot(lower, selb)
    cnt_col = _ceil_seg(jnp.sum(self32, axis=1, keepdims=True))
    loc_col = _dot3_left(lower, jnp.broadcast_to(cnt_col, (N_EXPERTS, LANES)))[:, 0:1]
    cnt_row = _ceil_seg(_dot_nt(jnp.ones((SUBLANES, TM), BF16), selb))
    loc_row = _dot3(cnt_row, upper)
    tab_ref[...] = jnp.concatenate([cnt_row, loc_row], axis=1).astype(jnp.int32)

    r8 = lax.broadcasted_iota(jnp.int32, (TOP_K, TM), 0)
    ld = jnp.zeros((TOP_K, TM), F32)
    wk = jnp.zeros((TOP_K, TM), F32)
    stage_row = pos + loc_col
    for k in range(TOP_K):
        one = sel & (ksel == float(k))
        ld = jnp.where(r8 == k, jnp.sum(jnp.where(one, stage_row, 0.0), axis=0, keepdims=True), ld)
        wk = jnp.where(r8 == k, jnp.sum(jnp.where(one, wdense, 0.0), axis=0, keepdims=True), wk)
    ld_ref[...] = ld.astype(jnp.int32)
    wk_ref[...] = wk


def _router(xu, mod, gpre, rwt, rb, *, n_rows, n_lat, seq, batch):
    d = xu.shape[1]
    row = lambda w: pl.BlockSpec((TM, w), lambda i: (i, 0))
    col = pl.BlockSpec((TOP_K, TM), lambda i: (0, i))
    const = lambda a: pl.BlockSpec(a.shape, lambda i: (0,) * a.ndim)
    return pl.pallas_call(
        _router_kernel,
        grid=(n_rows // TM,),
        in_specs=[row(d), pl.BlockSpec((1, 1, d), _mod_spec(3, n_lat, seq, batch, TM)),
                  pl.BlockSpec((1, 1, d), _mod_spec(4, n_lat, seq, batch, TM)),
                  const(gpre), const(rwt), const(rb)],
        out_specs=(row(d), col, col, pl.BlockSpec((SUBLANES, 2 * N_EXPERTS), lambda i: (i, 0))),
        out_shape=(jax.ShapeDtypeStruct((n_rows, d), BF16),
                   jax.ShapeDtypeStruct((TOP_K, n_rows), jnp.int32),
                   jax.ShapeDtypeStruct((TOP_K, n_rows), F32),
                   jax.ShapeDtypeStruct((n_rows // TM * SUBLANES, 2 * N_EXPERTS), jnp.int32)),
        compiler_params=_params(("parallel",)),
        name="router",
    )(xu, mod, mod, gpre, rwt, rb)


def _pow2_pieces(limit):
    bits, b = [], limit
    while b >= SEG_ALIGN:
        bits.append(b)
        b //= 2
    return bits


def _copy_pieces(n, src_ref, src0, dst_ref, dst0, sem, limit, wait, same_src=False):
    for bit in _pow2_pieces(limit):
        @pl.when((n & bit) != 0)
        def _():
            off = n & ~(2 * bit - 1)
            cp = pltpu.make_async_copy(src_ref.at[pl.ds(pl.multiple_of(src0 + (0 if same_src else off), SEG_ALIGN),
                                                          bit)],
                                       dst_ref.at[pl.ds(pl.multiple_of(dst0 + off, SEG_ALIGN), bit)], sem)
            cp.wait() if wait else cp.start()


N_PIECE_TABS = 7


def _piece_copies(tile, tabs, stage_ref, slots_ref, sem, to_slots, wait):
    nbig_ref, nsmall_ref, _, bsrc_ref, bdst_ref, ssrc_ref, sdst_ref = tabs
    for rows, n_ref, a_ref, b_ref, cap in ((BIG_PIECE, nbig_ref, bsrc_ref, bdst_ref, BIG_MAX),
                                           (SEG_ALIGN, nsmall_ref, ssrc_ref, sdst_ref, SMALL_MAX)):
        def body(p, c, rows=rows, a_ref=a_ref, b_ref=b_ref, cap=cap):
            src = stage_ref.at[pl.ds(pl.multiple_of(a_ref[tile * cap + p], SEG_ALIGN), rows)]
            dst = slots_ref.at[pl.ds(pl.multiple_of(b_ref[tile * cap + p], SEG_ALIGN), rows)]
            cp = pltpu.make_async_copy(src, dst, sem) if to_slots else pltpu.make_async_copy(dst, src, sem)
            cp.wait() if wait else cp.start()
            return c
        lax.fori_loop(0, n_ref[tile], body, 0)


def _used_blocks(tile, tabs):
    return (tabs[2][tile] + TM - 1) // TM


def _for_used_blocks(used, body):
    for b in range(TOP_K):
        body(b)
    for b in range(TOP_K, STAGE_ROWS // TM):
        @pl.when(b < used)
        def _():
            body(b)


def _stage_rows_iota():
    return lax.broadcasted_iota(jnp.int32, (TM // 2, TM), 0).astype(F32).astype(BF16)


def _pick_matrix(ld, base, vals, jrow):
    rel = (ld - base).astype(F32)
    rel = jnp.where(jnp.logical_and(rel >= 0.0, rel < TM // 2), rel, -1.0).astype(BF16)
    out = jnp.zeros((TM // 2, TM), BF16)
    for k in range(TOP_K):
        out = jnp.where(rel[k:k + 1, :] == jrow, vals[k:k + 1, :], out)
    return out


def _dispatch_kernel(*refs):
    tabs = refs[:N_PIECE_TABS]
    pstart_ref, npad_ref, hb_ref, ld_ref, xs_ref, stage, zbuf, sem, zsem = refs[N_PIECE_TABS:]
    i = pl.program_id(0)

    @pl.when(i == 0)
    def _():
        zbuf[...] = jnp.zeros_like(zbuf)
        for wait in (False, True):
            def body(e, c, wait=wait):
                _copy_pieces(npad_ref[e], zbuf, 0, xs_ref, pstart_ref[e], zsem, BM_EXPERT // 2, wait, same_src=True)
                return c
            lax.fori_loop(0, N_EXPERTS, body, 0)

    ld = ld_ref[...]
    hb = hb_ref[...]
    jrow = _stage_rows_iota()
    ones = jnp.ones((TOP_K, TM), BF16)

    cur = stage.at[i & 1]

    def block(b):
        for half in range(2):
            base = b * TM + half * (TM // 2)
            cur[base:base + TM // 2, :] = _dot(_pick_matrix(ld, base, ones, jrow), hb).astype(BF16)

    _for_used_blocks(_used_blocks(i, tabs), block)

    @pl.when(i > 0)
    def _():
        _piece_copies(i - 1, tabs, stage.at[(i - 1) & 1], xs_ref, sem, True, True)

    _piece_copies(i, tabs, cur, xs_ref, sem, True, False)

    @pl.when(i == pl.num_programs(0) - 1)
    def _():
        _piece_copies(i, tabs, cur, xs_ref, sem, True, True)


def _dispatch(tabs, pad_start, n_pad, hb, ld, n_slots):
    n, d = hb.shape
    grid_spec = pltpu.PrefetchScalarGridSpec(
        num_scalar_prefetch=N_PIECE_TABS + 2,
        grid=(n // TM,),
        in_specs=[pl.BlockSpec((TM, d), lambda i, *_: (i, 0)),
                  pl.BlockSpec((TOP_K, TM), lambda i, *_: (0, i))],
        out_specs=pl.BlockSpec(memory_space=pl.ANY),
        scratch_shapes=[pltpu.VMEM((2, STAGE_ROWS, d), BF16), pltpu.VMEM((BM_EXPERT // 2, d), BF16),
                        pltpu.SemaphoreType.DMA(()), pltpu.SemaphoreType.DMA(())],
    )
    return pl.pallas_call(
        _dispatch_kernel,
        grid_spec=grid_spec,
        out_shape=jax.ShapeDtypeStruct((n_slots, d), BF16),
        compiler_params=_params(("arbitrary",)),
        name="moe_dispatch",
    )(*tabs, pad_start, n_pad, hb, ld)


def _expert_kernel(be_ref, na_ref, nxt_ref, slot_ref, xs_ref, wg_hbm, wu_hbm, wd_hbm, ys_ref,
                   wg_raw, wu_raw, wd_raw, wgb, wub, wdb, wsem, *, layer):
    i = pl.program_id(0)

    def weight_copies(e, slot):
        return [pltpu.make_async_copy(src.at[layer, e], dst.at[slot], wsem.at[slot, j])
                for j, (src, dst) in enumerate(((wg_hbm, wg_raw), (wu_hbm, wu_raw), (wd_hbm, wd_raw)))]

    @pl.when(i < na_ref[0])
    def _():
        e, slot = be_ref[i], slot_ref[i]

        @pl.when(i == 0)
        def _():
            for cp in weight_copies(e, slot):
                cp.start()

        @pl.when(jnp.logical_or(i == 0, e != be_ref[jnp.maximum(i - 1, 0)]))
        def _():
            for cp in weight_copies(e, slot):
                cp.wait()
            wgb[...] = wg_raw[slot].astype(BF16)
            wub[...] = wu_raw[slot].astype(BF16)
            wdb[...] = wd_raw[slot].astype(BF16)

            @pl.when(nxt_ref[i] != e)
            def _():
                for cp in weight_copies(nxt_ref[i], 1 - slot):
                    cp.start()

        half = BM_EXPERT // 2
        for r in range(2):
            xb = xs_ref[r * half:(r + 1) * half, :]
            hid = _silu(_dot(xb, wgb[...])) * _dot(xb, wub[...])
            ys_ref[r * half:(r + 1) * half, :] = _dot(hid.astype(BF16), wdb[...]).astype(ys_ref.dtype)


def _experts(block_e, n_active, next_e, w_slot, xs, wg, wu, wd, layer):
    n_slots, d = xs.shape
    nb = n_slots // BM_EXPERT
    tiles = pl.BlockSpec((BM_EXPERT, d), lambda i, be, na, *_: (jnp.minimum(i, na[0] - 1), 0))
    anywhere = pl.BlockSpec(memory_space=pl.ANY)
    grid_spec = pltpu.PrefetchScalarGridSpec(
        num_scalar_prefetch=4,
        grid=(nb,),
        in_specs=[tiles, anywhere, anywhere, anywhere],
        out_specs=tiles,
        scratch_shapes=[pltpu.VMEM((2, d, EXPERT_HIDDEN), F32), pltpu.VMEM((2, d, EXPERT_HIDDEN), F32),
                        pltpu.VMEM((2, EXPERT_HIDDEN, d), F32),
                        pltpu.VMEM((d, EXPERT_HIDDEN), BF16), pltpu.VMEM((d, EXPERT_HIDDEN), BF16),
                        pltpu.VMEM((EXPERT_HIDDEN, d), BF16), pltpu.SemaphoreType.DMA((2, 3))],
    )
    return pl.pallas_call(
        functools.partial(_expert_kernel, layer=layer),
        grid_spec=grid_spec,
        out_shape=jax.ShapeDtypeStruct((n_slots, d), BF16),
        compiler_params=_params(("arbitrary",)),
        name="moe_experts",
    )(block_e, n_active, next_e, w_slot, xs, wg, wu, wd)


def _combine_kernel(*refs):
    tabs = refs[:N_PIECE_TABS]
    (ys_ref, ld_ref, wk_ref, hb_ref, x_ref, gate_ref, gpost_ref, sg_ref, su_ref, sd_ref, o_ref,
     stage, acc_ref, sem) = refs[N_PIECE_TABS:]
    i = pl.program_id(0)

    @pl.when(i == 0)
    def _():
        stage[...] = jnp.zeros_like(stage)
        _piece_copies(0, tabs, stage.at[0], ys_ref, sem, False, False)

    cur = stage.at[i & 1]
    _piece_copies(i, tabs, cur, ys_ref, sem, False, True)

    @pl.when(i + 1 < pl.num_programs(0))
    def _():
        _piece_copies(i + 1, tabs, stage.at[(i + 1) & 1], ys_ref, sem, False, False)

    hb = hb_ref[...]
    acc_ref[...] = _dot((_silu(_dot(hb, sg_ref[...])) * _dot(hb, su_ref[...])).astype(BF16), sd_ref[...])

    ld = ld_ref[...]
    wkb = wk_ref[...].astype(BF16)
    jrow = _stage_rows_iota()

    def block(b):
        for half in range(2):
            base = b * TM + half * (TM // 2)
            weights = _pick_matrix(ld, base, wkb, jrow)
            acc_ref[...] += lax.dot_general(weights, cur[base:base + TM // 2, :], (((0,), (0,)), ((), ())),
                                            preferred_element_type=F32)

    _for_used_blocks(_used_blocks(i, tabs), block)
    o_ref[...] = x_ref[...] + gate_ref[0] * _rms(acc_ref[...], gpost_ref[...])


def _combine(tabs, ys, ld, wk, hb, xu, mod, gpost, sg, su, sd, *, n_rows, n_lat, seq, batch):
    d = xu.shape[1]
    row = lambda w: pl.BlockSpec((TM, w), lambda i, *_: (i, 0))
    col = pl.BlockSpec((TOP_K, TM), lambda i, *_: (0, i))
    const = lambda a: pl.BlockSpec(a.shape, lambda i, *_: (0,) * a.ndim)
    mod_map = _mod_spec(5, n_lat, seq, batch, TM)
    grid_spec = pltpu.PrefetchScalarGridSpec(
        num_scalar_prefetch=N_PIECE_TABS,
        grid=(n_rows // TM,),
        in_specs=[pl.BlockSpec(memory_space=pl.ANY), col, col, row(d), row(d),
                  pl.BlockSpec((1, 1, d), lambda i, *_: mod_map(i)),
                  const(gpost), const(sg), const(su), const(sd)],
        out_specs=row(d),
        scratch_shapes=[pltpu.VMEM((2, STAGE_ROWS, d), BF16), pltpu.VMEM((TM, d), F32), pltpu.SemaphoreType.DMA(())],
    )
    return pl.pallas_call(
        _combine_kernel,
        grid_spec=grid_spec,
        out_shape=jax.ShapeDtypeStruct((n_rows, d), F32),
        compiler_params=_params(("arbitrary",)),
        name="moe_combine",
    )(*tabs, ys, ld, wk, hb, xu, mod, gpost, sg, su, sd)


def _deinterleave(w):
    cols = w.shape[-1]
    perm = jnp.concatenate([jnp.arange(0, HEAD_DIM, 2), jnp.arange(1, HEAD_DIM, 2)])
    idx = (jnp.arange(cols // HEAD_DIM)[:, None] * HEAD_DIM + perm[None, :]).reshape(-1)
    return w[..., idx]


def _pad_in_proj(w_in):
    d = w_in.shape[0]
    o = 0
    parts = {}
    for name, width in (("qa", 256), ("ka", 128), ("va", 128), ("z", 256), ("xs", 256), ("bm", 128), ("cm", 128),
                        ("dtf", 4), ("dtb", 4), ("lx", 256), ("lg", 256), ("qd", 256), ("kd", 128), ("vd", 128)):
        parts[name] = w_in[:, o:o + width]
        o += width
    dt = jnp.concatenate([parts["dtf"], parts["dtb"], jnp.zeros((d, LANES - 8), w_in.dtype)], axis=1)
    cols = [_deinterleave(parts["qa"]), _deinterleave(parts["ka"]), parts["va"],
            _deinterleave(parts["qd"]), _deinterleave(parts["kd"]), parts["vd"],
            parts["z"], parts["xs"], parts["bm"], parts["cm"], dt, parts["lx"], parts["lg"]]
    return jnp.concatenate(cols, axis=1).astype(BF16)


def _rope_tables(seq):
    t = jnp.arange(seq)
    rowp = (t // GRID_W).astype(F32)
    colp = (t % GRID_W).astype(F32)
    axis_dim = HEAD_DIM // 2
    inv_freq = ROPE_THETA ** (-jnp.arange(0, axis_dim, 2, dtype=F32) / axis_dim)
    ang = jnp.concatenate([rowp[:, None] * inv_freq, colp[:, None] * inv_freq], axis=-1)
    cos, sin = jnp.cos(ang), jnp.sin(ang)
    cos_h = jnp.concatenate([cos, cos], axis=-1)
    sin_h = jnp.concatenate([-sin, sin], axis=-1)
    return jnp.tile(cos_h, (1, 4)), jnp.tile(sin_h, (1, 4))


def _block_diag(w):
    nb, bd, _ = w.shape
    eye = jnp.eye(nb, dtype=w.dtype)
    return (eye[:, None, :, None] * w[:, :, None, :]).reshape(nb * bd, nb * bd)


def _piece_table(counts, cap, stage0, slot0, rows, ids):
    ends = jnp.cumsum(counts, axis=1)
    q = jnp.arange(cap, dtype=jnp.int32)
    owner = jnp.sum((ends[:, None, :] <= q[None, :, None]).astype(jnp.int32), axis=-1)
    mine = owner[:, :, None] == ids
    pick = lambda v: jnp.sum(jnp.where(mine, v[:, None, :], 0), axis=-1)
    step = rows * (q[None, :] - pick(ends - counts))
    return (pick(stage0) + step).reshape(-1), (pick(slot0) + step).reshape(-1)


def _lane_row(fwd, bwd):
    return jnp.concatenate([fwd, bwd, jnp.zeros((LANES - 8,), F32)]).reshape(1, LANES)


def kernel(x, c, ctx, c_ctx, w_ada, b_ada, g_mix_pre, g_mix_post, g_ffn_pre, g_ffn_post, w_in, w_out, a_sink,
           ssd_conv_w, ssd_conv_b, ssd_dt_bias, ssd_a_log, ssd_d, ssd_norm, lru_conv_w, lru_conv_b, lru_w_a,
           lru_b_a, lru_w_i, lru_b_i, lru_lambda, d_q_norm, d_k_norm, router_w, router_bias, exp_w_gate,
           exp_w_up, exp_w_down, sh_w_gate, sh_w_up, sh_w_down):
    batch, seq, d = x.shape
    ctx_len = ctx.shape[1]
    depth = w_ada.shape[0]
    n_lat = batch * seq
    n_ctx = batch * ctx_len
    n_all = n_lat + n_ctx
    assert seq % TM == 0 and n_ctx % TM == 0 and seq % T_CONV == 0 and ctx_len % T_CONV == 0
    assert seq % TQ_WINDOW == 0 and seq % (SCAN_CHUNKS * CHUNK) == 0 and ctx_len % (SCAN_CHUNKS * CHUNK) == 0
    assert ctx_len <= KV_CHUNK and seq % KV_CHUNK == 0 and seq % TQ_GLOBAL == 0 and batch + 1 <= SUBLANES

    xu = jnp.concatenate([x.reshape(n_lat, d), ctx.reshape(n_ctx, d)], axis=0)
    cin = jnp.concatenate([c, c_ctx[None, :], jnp.zeros((SUBLANES - batch - 1, d), F32)], axis=0)
    mod_all = _adaln(cin, w_ada, b_ada)
    cos_t, sin_t = _rope_tables(seq)
    hm = jnp.kron(jnp.eye(4, dtype=F32), jnp.full((HEAD_DIM, HEAD_DIM), 1.0 / HEAD_DIM, F32)).astype(BF16)

    for l in range(depth):
        with_ctx = l < depth - 1
        mod = mod_all[l].reshape(SUBLANES * 6, 1, d)
        gq = jnp.tile(_deinterleave(d_q_norm[l]), 4).reshape(1, 256)
        gk = jnp.tile(_deinterleave(d_k_norm[l]), 2).reshape(1, LANES)
        qa, kat, va, qd, kdt, vd, z, xbc_raw, dt, lx_raw, lg = _inproj(
            xu, mod, g_mix_pre[l].reshape(1, d), _pad_in_proj(w_in[l]), cos_t, sin_t, gq, gk, hm,
            n_lat=n_lat, seq=seq, batch=batch)

        xbc, lu = _conv(xbc_raw, lx_raw, ssd_conv_w[l], ssd_conv_b[l].reshape(1, -1),
                        lru_conv_w[l], lru_conv_b[l].reshape(1, -1), n_lat=n_lat, seq=seq, ctx_len=ctx_len)
        yf, yb = _ssd(xbc, dt, _lane_row(ssd_dt_bias[l, 0], ssd_dt_bias[l, 1]),
                      _lane_row(ssd_a_log[l, 0], ssd_a_log[l, 1]), batch=batch, seq=seq, ctx_len=ctx_len)
        wg = jnp.stack([jnp.concatenate([_block_diag(lru_w_a[l, dd]), _block_diag(lru_w_i[l, dd])], axis=1)
                        for dd in range(2)]).astype(BF16)
        bg = jnp.concatenate([lru_b_a[l], lru_b_i[l]], axis=1).reshape(2, 1, 2 * LRU_WIDTH)
        hf, hb = _lru(lu, wg, bg, lru_lambda[l].reshape(2, 1, LRU_WIDTH), batch=batch, seq=seq, ctx_len=ctx_len)

        oa = _window_attn(qa, kat, va, a_sink[l], batch=batch, seq=seq, ctx_len=ctx_len)
        od = _dense_attn(qd, kdt, vd, None, q_row0=0, q_len=seq, tq=TQ_GLOBAL,
                         segs=[(n_lat, ctx_len), (0, seq)], batch=batch)
        if with_ctx:
            oa_c = _dense_attn(qa, kat, va, a_sink[l], q_row0=n_lat, q_len=ctx_len, tq=ctx_len,
                               segs=[(n_lat, ctx_len)], batch=batch)
            od_c = _dense_attn(qd, kdt, vd, None, q_row0=n_lat, q_len=ctx_len, tq=ctx_len,
                               segs=[(n_lat, ctx_len)], batch=batch)
            oa = jnp.concatenate([oa, oa_c], axis=0)
            od = jnp.concatenate([od, od_c], axis=0)
        n_rows = n_all if with_ctx else n_lat

        dsk = jnp.repeat(ssd_d[l], HEAD_DIM).reshape(1, 256)
        xu_mid = _outproj(xu, mod, g_mix_post[l].reshape(1, d), oa, od, yf, yb, xbc, z, dsk,
                          ssd_norm[l].reshape(1, 256), hf, hb, lg, w_out[l].astype(BF16),
                          n_rows=n_rows, n_lat=n_lat, seq=seq, batch=batch)

        hb_ffn, ld, wk, tab = _router(xu_mid, mod, g_ffn_pre[l].reshape(1, d), router_w[l].T.astype(BF16),
                                      router_bias[l].reshape(N_EXPERTS, 1), n_rows=n_rows, n_lat=n_lat,
                                      seq=seq, batch=batch)
        n_tiles = n_rows // TM
        tab = tab.reshape(n_tiles, SUBLANES, 2 * N_EXPERTS)[:, 0, :]
        seg_cnt, seg_loc = tab[:, :N_EXPERTS], tab[:, N_EXPERTS:]
        counts = jnp.sum(seg_cnt, axis=0)
        padded = (counts + BM_EXPERT - 1) // BM_EXPERT * BM_EXPERT
        padded_end = jnp.cumsum(padded)
        offs = padded_end - padded
        seg_off = offs[None, :] + jnp.cumsum(seg_cnt, axis=0) - seg_cnt
        n_blocks = (n_rows * TOP_K + n_tiles * N_EXPERTS * SEG_ALIGN) // BM_EXPERT + N_EXPERTS
        n_active = (padded_end[-1] // BM_EXPERT).astype(jnp.int32).reshape(1)
        block_start = jnp.arange(n_blocks, dtype=jnp.int32) * BM_EXPERT
        block_e = jnp.minimum(jnp.sum((padded_end[None, :] <= block_start[:, None]).astype(jnp.int32), axis=1),
                              N_EXPERTS - 1)
        ids = jnp.arange(N_EXPERTS, dtype=jnp.int32)
        n_big = seg_cnt // BIG_PIECE
        n_small = (seg_cnt % BIG_PIECE) // SEG_ALIGN
        tabs = (jnp.sum(n_big, axis=1), jnp.sum(n_small, axis=1), seg_loc[:, -1] + seg_cnt[:, -1],
                *_piece_table(n_big, BIG_MAX, seg_loc, seg_off, BIG_PIECE, ids),
                *_piece_table(n_small, SMALL_MAX, seg_loc + n_big * BIG_PIECE, seg_off + n_big * BIG_PIECE,
                              SEG_ALIGN, ids))
        xs = _dispatch(tabs, offs + counts, padded - counts, hb_ffn, ld, n_blocks * BM_EXPERT)
        has_rows = padded > 0
        later = jnp.logical_and(ids[None, :] > ids[:, None], has_rows[None, :])
        nxt_of = jnp.min(jnp.where(later, ids[None, :], N_EXPERTS), axis=1)
        nxt_of = jnp.where(nxt_of == N_EXPERTS, ids, nxt_of)
        slot_of = (jnp.cumsum(has_rows.astype(jnp.int32)) - 1) & 1
        own = block_e[:, None] == ids[None, :]
        next_e = jnp.sum(jnp.where(own, nxt_of[None, :], 0), axis=1)
        w_slot = jnp.sum(jnp.where(own, slot_of[None, :], 0), axis=1)
        ys = _experts(block_e, n_active, next_e, w_slot, xs, exp_w_gate, exp_w_up, exp_w_down, l)
        xu = _combine(tabs, ys, ld, wk, hb_ffn, xu_mid, mod, g_ffn_post[l].reshape(1, d), sh_w_gate[l].astype(BF16),
                      sh_w_up[l].astype(BF16), sh_w_down[l].astype(BF16),
                      n_rows=n_rows, n_lat=n_lat, seq=seq, batch=batch)
    return xu[:n_lat].reshape(batch, seq, d)
```

```python
import functools
import math

import jax
import jax.numpy as jnp
from jax import lax
from jax.experimental import pallas as pl
from jax.experimental.pallas import tpu as pltpu

F32 = jnp.float32
BF16 = jnp.bfloat16

HEAD_DIM = 64
GRID_W = 64
ROPE_THETA = 10000.0
NORM_EPS = 1e-6
NEG_INF = -1e30
A_HEADS, A_KV_HEADS, WINDOW = 4, 2, 128
SSD_HEADS, SSD_GROUPS, SSD_STATE, SSD_CONV = 4, 2, 64, 4
LRU_WIDTH, LRU_BLOCKS, LRU_CONV, LRU_C = 256, 4, 4, 8.0
D_HEADS, D_KV_HEADS = 4, 2
N_EXPERTS, N_EXPERT_GROUPS, TOPK_GROUPS, TOP_K = 64, 8, 4, 8
EXPERT_HIDDEN, SHARED_HIDDEN = 256, 256
ROUTED_SCALE = 2.5

LANES = 128
SUBLANES = 8

TM = 512
T_CONV = 256
CHUNK = 128
SCAN_CHUNKS = 2
TQ_GLOBAL = 256
TQ_WINDOW = 512
KV_CHUNK = 256
KV_UNROLL = 16
BM_EXPERT = 512
SEG_ALIGN = 16
STAGE_ROWS = TM * TOP_K + N_EXPERTS * SEG_ALIGN
BIG_PIECE = 64
BIG_MAX = STAGE_ROWS // BIG_PIECE
SMALL_MAX = N_EXPERTS * (BIG_PIECE // SEG_ALIGN - 1)
VMEM_LIMIT = 48 * 1024 * 1024

C_QA, C_KA, C_VA = 0, 256, 384
C_QD, C_KD, C_VD = 512, 768, 896
C_Z, C_XBC, C_DT = 1024, 1280, 1792
C_LX, C_LG = 1920, 2176
NP_IN = 2432


def _dot(a, b):
    return jnp.dot(a, b, preferred_element_type=F32)


def _dot_nt(a, b):
    return lax.dot_general(a, b, (((1,), (1,)), ((), ())), preferred_element_type=F32)


def _dot3(a, b):
    a1 = a.astype(BF16)
    r1 = a - a1.astype(F32)
    a2 = r1.astype(BF16)
    a3 = (r1 - a2.astype(F32)).astype(BF16)
    return _dot(a1, b) + _dot(a2, b) + _dot(a3, b)


def _dot3_left(a, b):
    b1 = b.astype(BF16)
    r1 = b - b1.astype(F32)
    b2 = r1.astype(BF16)
    b3 = (r1 - b2.astype(F32)).astype(BF16)
    return _dot(a, b1) + _dot(a, b2) + _dot(a, b3)


def _silu(x):
    return x * jax.nn.sigmoid(x)


def _softplus(x):
    return jnp.maximum(x, 0.0) + jnp.log1p(jnp.exp(-jnp.abs(x)))


def _rms(x, gain):
    return x * lax.rsqrt(jnp.mean(x * x, axis=-1, keepdims=True) + NORM_EPS) * gain


def _params(sem=None):
    return pltpu.CompilerParams(dimension_semantics=sem, vmem_limit_bytes=VMEM_LIMIT)


def _adaln_kernel(c_ref, w_ref, b_ref, o_ref):
    s = _silu(c_ref[...])
    o_ref[0] = _dot(s.astype(BF16), w_ref[0].astype(BF16)) + b_ref[0]


def _adaln(cin, w_ada, b_ada):
    depth, d, n6 = w_ada.shape
    tn = 1024
    return pl.pallas_call(
        _adaln_kernel,
        grid=(depth, n6 // tn),
        in_specs=[pl.BlockSpec((SUBLANES, d), lambda l, j: (0, 0)),
                  pl.BlockSpec((1, d, tn), lambda l, j: (l, 0, j)),
                  pl.BlockSpec((1, 1, tn), lambda l, j: (l, 0, j))],
        out_specs=pl.BlockSpec((1, SUBLANES, tn), lambda l, j: (l, 0, j)),
        out_shape=jax.ShapeDtypeStruct((depth, SUBLANES, n6), F32),
        compiler_params=_params(("parallel", "parallel")),
        name="adaln",
    )(cin, w_ada, b_ada.reshape(depth, 1, n6))


def _swap_halves(t):
    w = t.shape[1]
    lane = lax.broadcasted_iota(jnp.int32, (1, w), 1)
    first = (lane & 32) == 0
    return jnp.where(first, pltpu.roll(t, w - 32, axis=1), pltpu.roll(t, 32, axis=1))


def _inproj_kernel(x_ref, shift_ref, scale_ref, gpre_ref, w_ref, cos_ref, sin_ref, gq_ref, gk_ref, hm_ref,
                   qa_ref, kat_ref, va_ref, qd_ref, kdt_ref, vd_ref, z_ref, xbc_ref, dt_ref, lx_ref, lg_ref,
                   *, n_lat):
    i = pl.program_id(0)
    is_lat = i * TM < n_lat
    h = _rms(x_ref[...], gpre_ref[...])
    h = h * (1.0 + scale_ref[0]) + shift_ref[0]
    hb = h.astype(BF16)

    def sec(a, b):
        return _dot(hb, w_ref[:, a:b])

    cos = jnp.where(is_lat, cos_ref[...], 1.0)
    sin = jnp.where(is_lat, sin_ref[...], 0.0)

    def rope(t):
        w = t.shape[1]
        return t * cos[:, :w] + _swap_halves(t) * sin[:, :w]

    def head_norm(t, gain):
        w = t.shape[1]
        ms = _dot3(t * t, hm_ref[:w, :w])
        return t * lax.rsqrt(ms + NORM_EPS) * gain

    scale = HEAD_DIM ** -0.5
    qa_ref[...] = (rope(sec(C_QA, C_KA)) * scale).astype(BF16)
    kat_ref[...] = rope(sec(C_KA, C_VA)).T.astype(BF16)
    va_ref[...] = sec(C_VA, C_QD).astype(BF16)
    qd_ref[...] = (rope(head_norm(sec(C_QD, C_KD), gq_ref[...])) * scale).astype(BF16)
    kdt_ref[...] = rope(head_norm(sec(C_KD, C_VD), gk_ref[...])).T.astype(BF16)
    vd_ref[...] = sec(C_VD, C_Z).astype(BF16)
    z_ref[...] = sec(C_Z, C_XBC)
    xbc_ref[...] = sec(C_XBC, C_DT)
    dt_ref[...] = sec(C_DT, C_LX)
    lx_ref[...] = sec(C_LX, C_LG)
    lg_ref[...] = sec(C_LG, NP_IN)


def _mod_spec(chunk, n_lat, seq, batch, tile):
    def imap(i):
        row0 = i * tile
        seg = jnp.where(row0 < n_lat, row0 // seq, batch)
        return (seg * 6 + chunk, 0, 0)
    return imap


def _inproj(xu, mod, gpre, w_pad, cos_t, sin_t, gq, gk, hm, *, n_lat, seq, batch):
    n, d = xu.shape
    nt = n // TM
    spt = seq // TM
    row = lambda w: pl.BlockSpec((TM, w), lambda i: (i, 0))
    colT = pl.BlockSpec((LANES, TM), lambda i: (0, i))
    const = lambda a: pl.BlockSpec(a.shape, lambda i: (0,) * a.ndim)
    out_shapes = (
        jax.ShapeDtypeStruct((n, 256), BF16), jax.ShapeDtypeStruct((LANES, n), BF16),
        jax.ShapeDtypeStruct((n, LANES), BF16),
        jax.ShapeDtypeStruct((n, 256), BF16), jax.ShapeDtypeStruct((LANES, n), BF16),
        jax.ShapeDtypeStruct((n, LANES), BF16),
        jax.ShapeDtypeStruct((n, 256), F32), jax.ShapeDtypeStruct((n, 512), F32),
        jax.ShapeDtypeStruct((n, LANES), F32), jax.ShapeDtypeStruct((n, 256), F32),
        jax.ShapeDtypeStruct((n, 256), F32))
    return pl.pallas_call(
        functools.partial(_inproj_kernel, n_lat=n_lat),
        grid=(nt,),
        in_specs=[row(d),
                  pl.BlockSpec((1, 1, d), _mod_spec(0, n_lat, seq, batch, TM)),
                  pl.BlockSpec((1, 1, d), _mod_spec(1, n_lat, seq, batch, TM)),
                  const(gpre), const(w_pad),
                  pl.BlockSpec((TM, 256), lambda i: (i % spt, 0)),
                  pl.BlockSpec((TM, 256), lambda i: (i % spt, 0)),
                  const(gq), const(gk), const(hm)],
        out_specs=(row(256), colT, row(LANES), row(256), colT, row(LANES),
                   row(256), row(512), row(LANES), row(256), row(256)),
        out_shape=out_shapes,
        compiler_params=_params(("parallel",)),
        name="inproj",
    )(xu, mod, mod, gpre, w_pad, cos_t, sin_t, gq, gk, hm)


def _conv_kernel(xs_ref, xsp_ref, xsn_ref, xl_ref, xlp_ref, xln_ref, ws_ref, bs_ref, wl_ref, bl_ref,
                 os_ref, ol_ref, *, n_lat, seq, ctx_len):
    i = pl.program_id(0)
    row0 = i * T_CONV
    pos = jnp.where(row0 < n_lat, row0 % seq, (row0 - n_lat) % ctx_len)
    slen = jnp.where(row0 < n_lat, seq, ctx_len)
    first = pos == 0
    last = pos + T_CONV == slen
    row = lax.broadcasted_iota(jnp.int32, (T_CONV, 1), 0)

    def conv(x, prev, nxt, w, b):
        pm = jnp.where(first, 0.0, prev)
        nx = jnp.where(last, 0.0, nxt)
        xm1 = jnp.where(row == 0, pm[7:8, :], pltpu.roll(x, 1, axis=0))
        xm2 = jnp.where(row == 0, pm[6:7, :], jnp.where(row == 1, pm[7:8, :], pltpu.roll(x, 2, axis=0)))
        xp1 = jnp.where(row == T_CONV - 1, nx[0:1, :], pltpu.roll(x, T_CONV - 1, axis=0))
        return w[0:1, :] * xm2 + w[1:2, :] * xm1 + w[2:3, :] * x + w[3:4, :] * xp1 + b

    os_ref[...] = _silu(conv(xs_ref[...], xsp_ref[...], xsn_ref[...], ws_ref[...], bs_ref[...]))
    ol_ref[...] = conv(xl_ref[...], xlp_ref[...], xln_ref[...], wl_ref[...], bl_ref[...])


def _conv(xbc_raw, lx_raw, ws, bs, wl, bl, *, n_lat, seq, ctx_len):
    n = xbc_raw.shape[0]
    nt = n // T_CONV
    r8 = T_CONV // SUBLANES
    n8 = n // SUBLANES
    main = lambda w: pl.BlockSpec((T_CONV, w), lambda i: (i, 0))
    prev = lambda w: pl.BlockSpec((SUBLANES, w), lambda i: (jnp.maximum(i * r8 - 1, 0), 0))
    nxt = lambda w: pl.BlockSpec((SUBLANES, w), lambda i: (jnp.minimum((i + 1) * r8, n8 - 1), 0))
    const = lambda a: pl.BlockSpec(a.shape, lambda i: (0,) * a.ndim)
    return pl.pallas_call(
        functools.partial(_conv_kernel, n_lat=n_lat, seq=seq, ctx_len=ctx_len),
        grid=(nt,),
        in_specs=[main(512), prev(512), nxt(512), main(256), prev(256), nxt(256),
                  const(ws), const(bs), const(wl), const(bl)],
        out_specs=(main(512), main(256)),
        out_shape=(jax.ShapeDtypeStruct((n, 512), F32), jax.ShapeDtypeStruct((n, 256), F32)),
        compiler_params=_params(("parallel",)),
        name="conv",
    )(xbc_raw, xbc_raw, xbc_raw, lx_raw, lx_raw, lx_raw, ws, bs, wl, bl)


def _chunk_maps(batch, seq, ctx_len):
    ncx = ctx_len // (SCAN_CHUNKS * CHUNK)
    nl = seq // (SCAN_CHUNKS * CHUNK)
    lat_blocks = batch * nl

    def block(b, c):
        return jnp.where(c < ncx, lat_blocks + b * ncx + c, b * nl + (c - ncx))

    def fwd(b, k):
        return (block(b, k), 0)

    def bwd(b, k):
        c = jnp.where(k < ncx, ncx - 1 - k, ncx + (nl - 1 - (k - ncx)))
        return (block(b, c), 0)

    return fwd, bwd, ncx + nl


def _ssd_kernel(xf_ref, dtf_ref, xb_ref, dtb_ref, dtbias_ref, alog_ref, yf_ref, yb_ref, state_ref):
    k = pl.program_id(1)

    @pl.when(k == 0)
    def _():
        state_ref[...] = jnp.zeros_like(state_ref)

    ri = lax.broadcasted_iota(jnp.int32, (CHUNK, CHUNK), 0)
    ci = lax.broadcasted_iota(jnp.int32, (CHUNK, CHUNK), 1)
    lane_lo = ci < HEAD_DIM
    aneg = -jnp.exp(alog_ref[...])
    dtbias = dtbias_ref[...]

    order = [(d, s if d == 0 else SCAN_CHUNKS - 1 - s) for s in range(SCAN_CHUNKS) for d in range(2)]
    for d, sub in order:
        x_ref, dt_ref, y_ref = ((xf_ref, dtf_ref, yf_ref), (xb_ref, dtb_ref, yb_ref))[d]
        rws = slice(sub * CHUNK, (sub + 1) * CHUNK)
        causal = (ri >= ci) if d == 0 else (ci >= ri)
        tmat = jnp.where(causal, 1.0, 0.0).astype(BF16)
        xs = x_ref[rws, 0:256]
        bm = x_ref[rws, 256:384]
        cm = x_ref[rws, 384:512]
        dtp = _softplus(dt_ref[rws, :] + dtbias)
        acum = _dot3_left(tmat, dtp * aneg)
        acum_t = acum.T
        bt = bm.T.astype(BF16)
        cmb = cm.astype(BF16)
        bmb = bm.astype(BF16)
        tot_row = CHUNK - 1 if d == 0 else 0
        for p in range(2):
            cmask = jnp.where(lane_lo if p == 0 else jnp.logical_not(lane_lo), cmb, jnp.zeros_like(cmb))
            cb = _dot_nt(cmask, bmb)
            cols, dts, ys = [], [], []
            x_pair = xs[:, p * LANES:(p + 1) * LANES]
            for j in range(2):
                col = 4 * d + 2 * p + j
                colb = jnp.broadcast_to(acum[:, col:col + 1], (CHUNK, CHUNK))
                rowb = jnp.broadcast_to(acum_t[col:col + 1, :], (CHUNK, CHUNK))
                cols.append(colb)
                dts.append(jnp.broadcast_to(dtp[:, col:col + 1], (CHUNK, CHUNK)))
            col_pair = jnp.where(lane_lo, cols[0], cols[1])
            dt_pair = jnp.where(lane_lo, dts[0], dts[1])
            xdt = x_pair * dt_pair
            xdt_b = xdt.astype(BF16)
            for j in range(2):
                col = 4 * d + 2 * p + j
                rowb = jnp.broadcast_to(acum_t[col:col + 1, :], (CHUNK, CHUNK))
                decay = jnp.exp(jnp.where(causal, cols[j] - rowb, NEG_INF))
                ys.append(_dot((cb * decay).astype(BF16), xdt_b))
            y_intra = jnp.where(lane_lo, ys[0], ys[1])
            s_old = state_ref[d, p]
            y_inter = _dot(cmask, s_old.astype(BF16)) * jnp.exp(col_pair)
            y_ref[rws, p * LANES:(p + 1) * LANES] = y_intra + y_inter
            tot_pair = col_pair[tot_row:tot_row + 1, :]
            to_end = jnp.exp(tot_pair - col_pair)
            state_ref[d, p] = s_old * jnp.exp(tot_pair) + _dot(bt, (xdt * to_end).astype(BF16))


def _ssd(xbc, dt, dtbias_row, alog_row, *, batch, seq, ctx_len):
    n = xbc.shape[0]
    fwd, bwd, steps = _chunk_maps(batch, seq, ctx_len)
    rows = SCAN_CHUNKS * CHUNK
    const = lambda a: pl.BlockSpec(a.shape, lambda b, k: (0,) * a.ndim)
    return pl.pallas_call(
        _ssd_kernel,
        grid=(batch, steps),
        in_specs=[pl.BlockSpec((rows, 512), fwd), pl.BlockSpec((rows, LANES), fwd),
                  pl.BlockSpec((rows, 512), bwd), pl.BlockSpec((rows, LANES), bwd),
                  const(dtbias_row), const(alog_row)],
        out_specs=(pl.BlockSpec((rows, 256), fwd), pl.BlockSpec((rows, 256), bwd)),
        out_shape=(jax.ShapeDtypeStruct((n, 256), F32), jax.ShapeDtypeStruct((n, 256), F32)),
        scratch_shapes=[pltpu.VMEM((2, 2, CHUNK, LANES), F32)],
        compiler_params=_params(("parallel", "arbitrary")),
        name="ssd_scan",
    )(xbc, dt, xbc, dt, dtbias_row, alog_row)


def _linear_scan(a, b, reverse):
    n = a.shape[0]
    row = lax.broadcasted_iota(jnp.int32, (n, 1), 0)
    s = 1
    while s < n:
        if reverse:
            ok = row < n - s
            a_sh = jnp.where(ok, pltpu.roll(a, n - s, axis=0), 1.0)
            b_sh = jnp.where(ok, pltpu.roll(b, n - s, axis=0), 0.0)
        else:
            ok = row >= s
            a_sh = jnp.where(ok, pltpu.roll(a, s, axis=0), 1.0)
            b_sh = jnp.where(ok, pltpu.roll(b, s, axis=0), 0.0)
        b = b + a * b_sh
        a = a * a_sh
        s *= 2
    return a, b


def _lru_kernel(uf_ref, ub_ref, wg_ref, bg_ref, lam_ref, hf_ref, hb_ref, carry_ref):
    k = pl.program_id(1)

    @pl.when(k == 0)
    def _():
        carry_ref[...] = jnp.zeros_like(carry_ref)

    for d, (u_ref, h_ref) in enumerate(((uf_ref, hf_ref), (ub_ref, hb_ref))):
        u = u_ref[...]
        gates = _dot(u.astype(BF16), wg_ref[d]) + bg_ref[d]
        r = jax.nn.sigmoid(gates[:, :LRU_WIDTH])
        ig = jax.nn.sigmoid(gates[:, LRU_WIDTH:])
        log_a = -LRU_C * r * _softplus(-lam_ref[d])
        a = jnp.exp(log_a)
        inp = jnp.sqrt(-jnp.tanh(log_a) * (1.0 + a * a)) * (ig * u)
        a_cum, b_cum = _linear_scan(a, inp, reverse=(d == 1))
        h = b_cum + a_cum * carry_ref[d, 0:1, :]
        h_ref[...] = h
        last = 0 if d == 1 else u.shape[0] - 1
        carry_ref[d, 0:1, :] = h[last:last + 1, :]


def _lru(u, wg, bg, lam, *, batch, seq, ctx_len):
    n = u.shape[0]
    fwd, bwd, steps = _chunk_maps(batch, seq, ctx_len)
    rows = SCAN_CHUNKS * CHUNK
    const = lambda a: pl.BlockSpec(a.shape, lambda b, k: (0,) * a.ndim)
    return pl.pallas_call(
        _lru_kernel,
        grid=(batch, steps),
        in_specs=[pl.BlockSpec((rows, LRU_WIDTH), fwd), pl.BlockSpec((rows, LRU_WIDTH), bwd),
                  const(wg), const(bg), const(lam)],
        out_specs=(pl.BlockSpec((rows, LRU_WIDTH), fwd), pl.BlockSpec((rows, LRU_WIDTH), bwd)),
        out_shape=(jax.ShapeDtypeStruct((n, LRU_WIDTH), F32), jax.ShapeDtypeStruct((n, LRU_WIDTH), F32)),
        scratch_shapes=[pltpu.VMEM((2, SUBLANES, LRU_WIDTH), F32)],
        compiler_params=_params(("parallel", "arbitrary")),
        name="lru_scan",
    )(u, u, wg, bg, lam)


def _stack_heads(q, g):
    qf = q.astype(F32)
    lo = g * LANES
    return jnp.concatenate([qf[:, lo:lo + HEAD_DIM], qf[:, lo + HEAD_DIM:lo + LANES]], axis=0).astype(BF16)


def _value_lanes(g):
    lane = lax.broadcasted_iota(jnp.int32, (1, LANES), 1)
    return (lane < HEAD_DIM) if g == 0 else (lane >= HEAD_DIM)


def _aug_values(v, g):
    return jnp.where(_value_lanes(g), v, jnp.ones_like(v))


def _flash_init(rows, g, sink_pair):
    if sink_pair is None:
        return jnp.full((rows, 1), NEG_INF, F32), jnp.zeros((rows, LANES), F32)
    half = lax.broadcasted_iota(jnp.int32, (rows, 1), 0) < rows // 2
    m = jnp.where(half, sink_pair[0], sink_pair[1]).astype(F32)
    acc = jnp.broadcast_to(jnp.where(_value_lanes(g), 0.0, 1.0), (rows, LANES))
    return m, acc


def _flash_update(state, q2, kt, v_aug, mask=None):
    m, acc = state
    s = _dot(q2, kt)
    if mask is not None:
        s = jnp.where(mask, s, NEG_INF)
    m_new = jnp.maximum(m, jnp.max(s, axis=-1, keepdims=True))
    p = jnp.exp(s - m_new).astype(BF16)
    acc = jnp.exp(m - m_new) * acc + _dot(p, v_aug)
    return m_new, acc


def _flash_finish(states, tq):
    pieces = []
    for g, (_, acc) in enumerate(states):
        den = (1 - g) * HEAD_DIM
        o = acc[:, g * HEAD_DIM:(g + 1) * HEAD_DIM] / acc[:, den:den + 1]
        pieces += [o[:tq], o[tq:]]
    return jnp.concatenate(pieces, axis=1)


def _group_rows(g):
    return slice(g * HEAD_DIM, (g + 1) * HEAD_DIM)


def _dense_attn_kernel(*refs, tq, seg_lens, has_sink):
    refs = list(refs)
    sink_ref = refs.pop(0) if has_sink else None
    q_ref = refs.pop(0)
    o_ref = refs.pop()
    segs = [(refs[2 * i], refs[2 * i + 1], n) for i, n in enumerate(seg_lens)]
    q = q_ref[...]
    q2 = [_stack_heads(q, g) for g in range(2)]
    states = tuple(_flash_init(2 * tq, g, (sink_ref[2 * g], sink_ref[2 * g + 1]) if has_sink else None)
                   for g in range(2))
    for kt_ref, v_ref, n_keys in segs:
        if n_keys <= KV_CHUNK:
            v = v_ref[...]
            states = tuple(_flash_update(states[g], q2[g], kt_ref[_group_rows(g), :], _aug_values(v, g))
                           for g in range(2))
        else:
            def body(c, sts, kt_ref=kt_ref, v_ref=v_ref):
                off = pl.multiple_of(c * KV_CHUNK, KV_CHUNK)
                v = v_ref[pl.ds(off, KV_CHUNK), :]
                return tuple(_flash_update(sts[g], q2[g], kt_ref[_group_rows(g), pl.ds(off, KV_CHUNK)],
                                           _aug_values(v, g)) for g in range(2))
            states = lax.fori_loop(0, n_keys // KV_CHUNK, body, states, unroll=KV_UNROLL)
    o_ref[...] = _flash_finish(states, tq).astype(o_ref.dtype)


def _dense_attn(q, kt, v, sink, *, q_row0, q_len, tq, segs, batch):
    n = q.shape[0]
    qpb = q_len // tq
    q0 = q_row0 // tq
    in_specs, args = [], []
    if sink is not None:
        in_specs.append(pl.BlockSpec(memory_space=pltpu.SMEM))
        args.append(sink)
    in_specs.append(pl.BlockSpec((tq, 256), lambda b, i: (q0 + b * qpb + i, 0)))
    args.append(q)
    for row0, klen in segs:
        k0 = row0 // klen
        in_specs.append(pl.BlockSpec((LANES, klen), lambda b, i, k0=k0: (0, k0 + b)))
        in_specs.append(pl.BlockSpec((klen, LANES), lambda b, i, k0=k0: (k0 + b, 0)))
        args += [kt, v]
    return pl.pallas_call(
        functools.partial(_dense_attn_kernel, tq=tq, seg_lens=tuple(s[1] for s in segs), has_sink=sink is not None),
        grid=(batch, qpb),
        in_specs=in_specs,
        out_specs=pl.BlockSpec((tq, 256), lambda b, i: (b * qpb + i, 0)),
        out_shape=jax.ShapeDtypeStruct((batch * q_len, 256), BF16),
        compiler_params=_params(("parallel", "parallel")),
        name="dense_attn",
    )(*args)


def _window_attn_kernel(sink_ref, q_ref, ktc_ref, vc_ref, ktp_ref, vp_ref, ktm_ref, vm_ref, ktn_ref, vn_ref, o_ref,
                        *, n_tiles):
    n = pl.program_id(1)
    nsub = TQ_WINDOW // CHUNK
    iq = lax.broadcasted_iota(jnp.int32, (2 * CHUNK, CHUNK), 0) & (CHUNK - 1)
    jk = lax.broadcasted_iota(jnp.int32, (2 * CHUNK, CHUNK), 1)
    below = jk >= iq
    above = jk <= iq
    vctx = vc_ref[...]
    for j in range(nsub):
        cols = slice(j * CHUNK, (j + 1) * CHUNK)
        q = q_ref[cols, :]
        states = []
        for g in range(2):
            q2 = _stack_heads(q, g)
            rows = _group_rows(g)
            state = _flash_init(2 * CHUNK, g, (sink_ref[2 * g], sink_ref[2 * g + 1]))
            state = _flash_update(state, q2, ktc_ref[rows, :], _aug_values(vctx, g))
            state = _flash_update(state, q2, ktm_ref[rows, cols], _aug_values(vm_ref[cols, :], g))
            if j > 0:
                prev = slice((j - 1) * CHUNK, j * CHUNK)
                state = _flash_update(state, q2, ktm_ref[rows, prev], _aug_values(vm_ref[prev, :], g), below)
            else:
                state = _flash_update(state, q2, ktp_ref[rows, :], _aug_values(vp_ref[...], g),
                                      jnp.logical_and(below, n > 0))
            if j < nsub - 1:
                nxt = slice((j + 1) * CHUNK, (j + 2) * CHUNK)
                state = _flash_update(state, q2, ktm_ref[rows, nxt], _aug_values(vm_ref[nxt, :], g), above)
            else:
                state = _flash_update(state, q2, ktn_ref[rows, :], _aug_values(vn_ref[...], g),
                                      jnp.logical_and(above, n < n_tiles - 1))
            states.append(state)
        o_ref[cols, :] = _flash_finish(states, CHUNK).astype(o_ref.dtype)


def _window_attn(q, kt, v, sink, *, batch, seq, ctx_len):
    nt = seq // TQ_WINDOW
    nsub = TQ_WINDOW // CHUNK
    nb = seq // CHUNK
    ctx0 = (batch * seq) // ctx_len
    prev = lambda b, n: b * nb + jnp.maximum(n * nsub - 1, 0)
    nxt = lambda b, n: b * nb + jnp.minimum((n + 1) * nsub, nb - 1)
    return pl.pallas_call(
        functools.partial(_window_attn_kernel, n_tiles=nt),
        grid=(batch, nt),
        in_specs=[pl.BlockSpec(memory_space=pltpu.SMEM),
                  pl.BlockSpec((TQ_WINDOW, 256), lambda b, n: (b * nt + n, 0)),
                  pl.BlockSpec((LANES, ctx_len), lambda b, n: (0, ctx0 + b)),
                  pl.BlockSpec((ctx_len, LANES), lambda b, n: (ctx0 + b, 0)),
                  pl.BlockSpec((LANES, CHUNK), lambda b, n: (0, prev(b, n))),
                  pl.BlockSpec((CHUNK, LANES), lambda b, n: (prev(b, n), 0)),
                  pl.BlockSpec((LANES, TQ_WINDOW), lambda b, n: (0, b * nt + n)),
                  pl.BlockSpec((TQ_WINDOW, LANES), lambda b, n: (b * nt + n, 0)),
                  pl.BlockSpec((LANES, CHUNK), lambda b, n: (0, nxt(b, n))),
                  pl.BlockSpec((CHUNK, LANES), lambda b, n: (nxt(b, n), 0))],
        out_specs=pl.BlockSpec((TQ_WINDOW, 256), lambda b, n: (b * nt + n, 0)),
        out_shape=jax.ShapeDtypeStruct((batch * seq, 256), BF16),
        compiler_params=_params(("parallel", "parallel")),
        name="window_attn",
    )(sink, q, kt, v, kt, v, kt, v, kt, v)


def _gelu_tanh(x):
    return 0.5 * x * (1.0 + jnp.tanh(math.sqrt(2.0 / math.pi) * (x + 0.044715 * (x * x * x))))


def _outproj_kernel(x_ref, gate_ref, gpost_ref, oa_ref, od_ref, yf_ref, yb_ref, xs_ref, z_ref, dsk_ref, gn_ref,
                    hf_ref, hb_ref, lg_ref, w_ref, o_ref):
    y_ssd = (yf_ref[...] + yb_ref[...] + xs_ref[...] * dsk_ref[...]) * _silu(z_ref[...])
    ob = _rms(y_ssd, gn_ref[...])
    oc = (hf_ref[...] + hb_ref[...]) * _gelu_tanh(lg_ref[...])
    y = (_dot(oa_ref[...], w_ref[0:256, :]) + _dot(ob.astype(BF16), w_ref[256:512, :])
         + _dot(oc.astype(BF16), w_ref[512:768, :]) + _dot(od_ref[...], w_ref[768:1024, :]))
    o_ref[...] = x_ref[...] + gate_ref[0] * _rms(y, gpost_ref[...])


def _outproj(xu, mod, gpost, oa, od, yf, yb, xbc, z, dsk, gn, hf, hb, lg, w_out, *, n_rows, n_lat, seq, batch):
    d = xu.shape[1]
    row = lambda w: pl.BlockSpec((TM, w), lambda i: (i, 0))
    const = lambda a: pl.BlockSpec(a.shape, lambda i: (0,) * a.ndim)
    return pl.pallas_call(
        _outproj_kernel,
        grid=(n_rows // TM,),
        in_specs=[row(d), pl.BlockSpec((1, 1, d), _mod_spec(2, n_lat, seq, batch, TM)), const(gpost),
                  row(256), row(256), row(256), row(256), row(256), row(256), const(dsk), const(gn),
                  row(256), row(256), row(256), const(w_out)],
        out_specs=row(d),
        out_shape=jax.ShapeDtypeStruct((n_rows, d), F32),
        compiler_params=_params(("parallel",)),
        name="outproj",
    )(xu, mod, gpost, oa, od, yf, yb, xbc, z, dsk, gn, hf, hb, lg, w_out)


def _ceil_seg(c):
    return jnp.floor((c + (SEG_ALIGN - 1)) * (1.0 / SEG_ALIGN)) * SEG_ALIGN


def _router_kernel(x_ref, shift_ref, scale_ref, gpre_ref, rwt_ref, rb_ref, hb_ref, ld_ref, wk_ref, tab_ref):
    h = _rms(x_ref[...], gpre_ref[...])
    h = h * (1.0 + scale_ref[0]) + shift_ref[0]
    hb = h.astype(BF16)
    hb_ref[...] = hb

    scores = jax.nn.sigmoid(_dot_nt(rwt_ref[...], hb))
    biased = scores + rb_ref[...]
    gsz = N_EXPERTS // N_EXPERT_GROUPS
    sub = lax.broadcasted_iota(jnp.int32, (gsz, TM), 0)
    blocks, gscore = [], []
    for g in range(N_EXPERT_GROUPS):
        blk = biased[g * gsz:(g + 1) * gsz, :]
        m1 = jnp.max(blk, axis=0, keepdims=True)
        first = jnp.min(jnp.where(blk == m1, sub, gsz), axis=0, keepdims=True)
        m2 = jnp.max(jnp.where(sub == first, -jnp.inf, blk), axis=0, keepdims=True)
        blocks.append(blk)
        gscore.append(m1 + m2)
    masked = []
    for g in range(N_EXPERT_GROUPS):
        rank = jnp.zeros((1, TM), F32)
        for g2 in range(N_EXPERT_GROUPS):
            if g2 == g:
                continue
            beats = (gscore[g2] > gscore[g]) | ((gscore[g2] == gscore[g]) if g2 < g else False)
            rank = rank + jnp.where(beats, 1.0, 0.0)
        masked.append(jnp.where(rank < TOPK_GROUPS, blocks[g], -jnp.inf))
    vals = jnp.concatenate(masked, axis=0)
    eidx = lax.broadcasted_iota(jnp.int32, (N_EXPERTS, TM), 0)
    self32 = jnp.zeros((N_EXPERTS, TM), F32)
    rest = vals
    for _ in range(TOP_K):
        top = jnp.max(rest, axis=0, keepdims=True)
        first = jnp.min(jnp.where(rest == top, eidx, N_EXPERTS), axis=0, keepdims=True)
        hit = eidx == first
        self32 = jnp.where(hit, 1.0, self32)
        rest = jnp.where(hit, -jnp.inf, rest)
    sel = self32 > 0.5
    picked = jnp.where(sel, scores, 0.0)
    wdense = picked / jnp.sum(picked, axis=0, keepdims=True) * ROUTED_SCALE

    tr = lax.broadcasted_iota(jnp.int32, (TM, TM), 0)
    tc = lax.broadcasted_iota(jnp.int32, (TM, TM), 1)
    before = jnp.where(tr < tc, 1.0, 0.0).astype(BF16)
    selb = self32.astype(BF16)
    pos = _dot(selb, before)
    er = lax.broadcasted_iota(jnp.int32, (N_EXPERTS, N_EXPERTS), 0)
    ec = lax.broadcasted_iota(jnp.int32, (N_EXPERTS, N_EXPERTS), 1)
    lower = jnp.where(ec < er, 1.0, 0.0).astype(BF16)
    upper = jnp.where(er < ec, 1.0, 0.0).astype(BF16)
    ksel = _dot(lower, selb)
    cnt_col = _ceil_seg(jnp.sum(self32, axis=1, keepdims=True))
    loc_col = _dot3_left(lower, jnp.broadcast_to(cnt_col, (N_EXPERTS, LANES)))[:, 0:1]
    cnt_row = _ceil_seg(_dot_nt(jnp.ones((SUBLANES, TM), BF16), selb))
    loc_row = _dot3(cnt_row, upper)
    tab_ref[...] = jnp.concatenate([cnt_row, loc_row], axis=1).astype(jnp.int32)

    r8 = lax.broadcasted_iota(jnp.int32, (TOP_K, TM), 0)
    ld = jnp.zeros((TOP_K, TM), F32)
    wk = jnp.zeros((TOP_K, TM), F32)
    stage_row = pos + loc_col
    for k in range(TOP_K):
        one = sel & (ksel == float(k))
        ld = jnp.where(r8 == k, jnp.sum(jnp.where(one, stage_row, 0.0), axis=0, keepdims=True), ld)
        wk = jnp.where(r8 == k, jnp.sum(jnp.where(one, wdense, 0.0), axis=0, keepdims=True), wk)
    ld_ref[...] = ld.astype(jnp.int32)
    wk_ref[...] = wk


def _router(xu, mod, gpre, rwt, rb, *, n_rows, n_lat, seq, batch):
    d = xu.shape[1]
    row = lambda w: pl.BlockSpec((TM, w), lambda i: (i, 0))
    col = pl.BlockSpec((TOP_K, TM), lambda i: (0, i))
    const = lambda a: pl.BlockSpec(a.shape, lambda i: (0,) * a.ndim)
    return pl.pallas_call(
        _router_kernel,
        grid=(n_rows // TM,),
        in_specs=[row(d), pl.BlockSpec((1, 1, d), _mod_spec(3, n_lat, seq, batch, TM)),
                  pl.BlockSpec((1, 1, d), _mod_spec(4, n_lat, seq, batch, TM)),
                  const(gpre), const(rwt), const(rb)],
        out_specs=(row(d), col, col, pl.BlockSpec((SUBLANES, 2 * N_EXPERTS), lambda i: (i, 0))),
        out_shape=(jax.ShapeDtypeStruct((n_rows, d), BF16),
                   jax.ShapeDtypeStruct((TOP_K, n_rows), jnp.int32),
                   jax.ShapeDtypeStruct((TOP_K, n_rows), F32),
                   jax.ShapeDtypeStruct((n_rows // TM * SUBLANES, 2 * N_EXPERTS), jnp.int32)),
        compiler_params=_params(("parallel",)),
        name="router",
    )(xu, mod, mod, gpre, rwt, rb)


def _pow2_pieces(limit):
    bits, b = [], limit
    while b >= SEG_ALIGN:
        bits.append(b)
        b //= 2
    return bits


def _copy_pieces(n, src_ref, src0, dst_ref, dst0, sem, limit, wait, same_src=False):
    for bit in _pow2_pieces(limit):
        @pl.when((n & bit) != 0)
        def _():
            off = n & ~(2 * bit - 1)
            cp = pltpu.make_async_copy(src_ref.at[pl.ds(pl.multiple_of(src0 + (0 if same_src else off), SEG_ALIGN),
                                                          bit)],
                                       dst_ref.at[pl.ds(pl.multiple_of(dst0 + off, SEG_ALIGN), bit)], sem)
            cp.wait() if wait else cp.start()


N_PIECE_TABS = 7


def _piece_copies(tile, tabs, stage_ref, slots_ref, sem, to_slots, wait):
    nbig_ref, nsmall_ref, _, bsrc_ref, bdst_ref, ssrc_ref, sdst_ref = tabs
    for rows, n_ref, a_ref, b_ref, cap in ((BIG_PIECE, nbig_ref, bsrc_ref, bdst_ref, BIG_MAX),
                                           (SEG_ALIGN, nsmall_ref, ssrc_ref, sdst_ref, SMALL_MAX)):
        def body(p, c, rows=rows, a_ref=a_ref, b_ref=b_ref, cap=cap):
            src = stage_ref.at[pl.ds(pl.multiple_of(a_ref[tile * cap + p], SEG_ALIGN), rows)]
            dst = slots_ref.at[pl.ds(pl.multiple_of(b_ref[tile * cap + p], SEG_ALIGN), rows)]
            cp = pltpu.make_async_copy(src, dst, sem) if to_slots else pltpu.make_async_copy(dst, src, sem)
            cp.wait() if wait else cp.start()
            return c
        lax.fori_loop(0, n_ref[tile], body, 0)


def _used_blocks(tile, tabs):
    return (tabs[2][tile] + TM - 1) // TM


def _for_used_blocks(used, body):
    for b in range(TOP_K):
        body(b)
    for b in range(TOP_K, STAGE_ROWS // TM):
        @pl.when(b < used)
        def _():
            body(b)


def _stage_rows_iota():
    return lax.broadcasted_iota(jnp.int32, (TM // 2, TM), 0).astype(F32).astype(BF16)


def _pick_matrix(ld, base, vals, jrow):
    rel = (ld - base).astype(F32)
    rel = jnp.where(jnp.logical_and(rel >= 0.0, rel < TM // 2), rel, -1.0).astype(BF16)
    out = jnp.zeros((TM // 2, TM), BF16)
    for k in range(TOP_K):
        out = jnp.where(rel[k:k + 1, :] == jrow, vals[k:k + 1, :], out)
    return out


def _dispatch_kernel(*refs):
    tabs = refs[:N_PIECE_TABS]
    pstart_ref, npad_ref, hb_ref, ld_ref, xs_ref, stage, zbuf, sem, zsem = refs[N_PIECE_TABS:]
    i = pl.program_id(0)

    @pl.when(i == 0)
    def _():
        zbuf[...] = jnp.zeros_like(zbuf)
        for wait in (False, True):
            def body(e, c, wait=wait):
                _copy_pieces(npad_ref[e], zbuf, 0, xs_ref, pstart_ref[e], zsem, BM_EXPERT // 2, wait, same_src=True)
                return c
            lax.fori_loop(0, N_EXPERTS, body, 0)

    ld = ld_ref[...]
    hb = hb_ref[...]
    jrow = _stage_rows_iota()
    ones = jnp.ones((TOP_K, TM), BF16)

    cur = stage.at[i & 1]

    def block(b):
        for half in range(2):
            base = b * TM + half * (TM // 2)
            cur[base:base + TM // 2, :] = _dot(_pick_matrix(ld, base, ones, jrow), hb).astype(BF16)

    _for_used_blocks(_used_blocks(i, tabs), block)

    @pl.when(i > 0)
    def _():
        _piece_copies(i - 1, tabs, stage.at[(i - 1) & 1], xs_ref, sem, True, True)

    _piece_copies(i, tabs, cur, xs_ref, sem, True, False)

    @pl.when(i == pl.num_programs(0) - 1)
    def _():
        _piece_copies(i, tabs, cur, xs_ref, sem, True, True)


def _dispatch(tabs, pad_start, n_pad, hb, ld, n_slots):
    n, d = hb.shape
    grid_spec = pltpu.PrefetchScalarGridSpec(
        num_scalar_prefetch=N_PIECE_TABS + 2,
        grid=(n // TM,),
        in_specs=[pl.BlockSpec((TM, d), lambda i, *_: (i, 0)),
                  pl.BlockSpec((TOP_K, TM), lambda i, *_: (0, i))],
        out_specs=pl.BlockSpec(memory_space=pl.ANY),
        scratch_shapes=[pltpu.VMEM((2, STAGE_ROWS, d), BF16), pltpu.VMEM((BM_EXPERT // 2, d), BF16),
                        pltpu.SemaphoreType.DMA(()), pltpu.SemaphoreType.DMA(())],
    )
    return pl.pallas_call(
        _dispatch_kernel,
        grid_spec=grid_spec,
        out_shape=jax.ShapeDtypeStruct((n_slots, d), BF16),
        compiler_params=_params(("arbitrary",)),
        name="moe_dispatch",
    )(*tabs, pad_start, n_pad, hb, ld)


def _expert_kernel(be_ref, na_ref, nxt_ref, slot_ref, xs_ref, wg_hbm, wu_hbm, wd_hbm, ys_ref,
                   wg_raw, wu_raw, wd_raw, wgub, wdb, wsem, *, layer):
    i = pl.program_id(0)

    def weight_copies(e, slot):
        return [pltpu.make_async_copy(src.at[layer, e], dst.at[slot], wsem.at[slot, j])
                for j, (src, dst) in enumerate(((wg_hbm, wg_raw), (wu_hbm, wu_raw), (wd_hbm, wd_raw)))]

    @pl.when(i < na_ref[0])
    def _():
        e, slot = be_ref[i], slot_ref[i]

        @pl.when(i == 0)
        def _():
            for cp in weight_copies(e, slot):
                cp.start()

        @pl.when(jnp.logical_or(i == 0, e != be_ref[jnp.maximum(i - 1, 0)]))
        def _():
            for cp in weight_copies(e, slot):
                cp.wait()
            wgub[:, :EXPERT_HIDDEN] = wg_raw[slot].astype(BF16)
            wgub[:, EXPERT_HIDDEN:] = wu_raw[slot].astype(BF16)
            wdb[...] = wd_raw[slot].astype(BF16)

            @pl.when(nxt_ref[i] != e)
            def _():
                for cp in weight_copies(nxt_ref[i], 1 - slot):
                    cp.start()

        gu = _dot(xs_ref[...], wgub[...])
        hid = _silu(gu[:, :EXPERT_HIDDEN]) * gu[:, EXPERT_HIDDEN:]
        ys_ref[...] = _dot(hid.astype(BF16), wdb[...]).astype(ys_ref.dtype)


def _experts(block_e, n_active, next_e, w_slot, xs, wg, wu, wd, layer):
    n_slots, d = xs.shape
    nb = n_slots // BM_EXPERT
    tiles = pl.BlockSpec((BM_EXPERT, d), lambda i, be, na, *_: (jnp.minimum(i, na[0] - 1), 0))
    anywhere = pl.BlockSpec(memory_space=pl.ANY)
    grid_spec = pltpu.PrefetchScalarGridSpec(
        num_scalar_prefetch=4,
        grid=(nb,),
        in_specs=[tiles, anywhere, anywhere, anywhere],
        out_specs=tiles,
        scratch_shapes=[pltpu.VMEM((2, d, EXPERT_HIDDEN), F32), pltpu.VMEM((2, d, EXPERT_HIDDEN), F32),
                        pltpu.VMEM((2, EXPERT_HIDDEN, d), F32),
                        pltpu.VMEM((d, 2 * EXPERT_HIDDEN), BF16),
                        pltpu.VMEM((EXPERT_HIDDEN, d), BF16), pltpu.SemaphoreType.DMA((2, 3))],
    )
    return pl.pallas_call(
        functools.partial(_expert_kernel, layer=layer),
        grid_spec=grid_spec,
        out_shape=jax.ShapeDtypeStruct((n_slots, d), BF16),
        compiler_params=_params(("arbitrary",)),
        name="moe_experts",
    )(block_e, n_active, next_e, w_slot, xs, wg, wu, wd)


def _combine_kernel(*refs):
    tabs = refs[:N_PIECE_TABS]
    (ys_ref, ld_ref, wk_ref, hb_ref, x_ref, gate_ref, gpost_ref, sg_ref, su_ref, sd_ref, o_ref,
     stage, acc_ref, sem) = refs[N_PIECE_TABS:]
    i = pl.program_id(0)

    @pl.when(i == 0)
    def _():
        stage[...] = jnp.zeros_like(stage)
        _piece_copies(0, tabs, stage.at[0], ys_ref, sem, False, False)

    cur = stage.at[i & 1]
    _piece_copies(i, tabs, cur, ys_ref, sem, False, True)

    @pl.when(i + 1 < pl.num_programs(0))
    def _():
        _piece_copies(i + 1, tabs, stage.at[(i + 1) & 1], ys_ref, sem, False, False)

    hb = hb_ref[...]
    acc_ref[...] = _dot((_silu(_dot(hb, sg_ref[...])) * _dot(hb, su_ref[...])).astype(BF16), sd_ref[...])

    ld = ld_ref[...]
    wkb = wk_ref[...].astype(BF16)
    jrow = _stage_rows_iota()

    def block(b):
        for half in range(2):
            base = b * TM + half * (TM // 2)
            weights = _pick_matrix(ld, base, wkb, jrow)
            acc_ref[...] += lax.dot_general(weights, cur[base:base + TM // 2, :], (((0,), (0,)), ((), ())),
                                            preferred_element_type=F32)

    _for_used_blocks(_used_blocks(i, tabs), block)
    o_ref[...] = x_ref[...] + gate_ref[0] * _rms(acc_ref[...], gpost_ref[...])


def _combine(tabs, ys, ld, wk, hb, xu, mod, gpost, sg, su, sd, *, n_rows, n_lat, seq, batch):
    d = xu.shape[1]
    row = lambda w: pl.BlockSpec((TM, w), lambda i, *_: (i, 0))
    col = pl.BlockSpec((TOP_K, TM), lambda i, *_: (0, i))
    const = lambda a: pl.BlockSpec(a.shape, lambda i, *_: (0,) * a.ndim)
    mod_map = _mod_spec(5, n_lat, seq, batch, TM)
    grid_spec = pltpu.PrefetchScalarGridSpec(
        num_scalar_prefetch=N_PIECE_TABS,
        grid=(n_rows // TM,),
        in_specs=[pl.BlockSpec(memory_space=pl.ANY), col, col, row(d), row(d),
                  pl.BlockSpec((1, 1, d), lambda i, *_: mod_map(i)),
                  const(gpost), const(sg), const(su), const(sd)],
        out_specs=row(d),
        scratch_shapes=[pltpu.VMEM((2, STAGE_ROWS, d), BF16), pltpu.VMEM((TM, d), F32), pltpu.SemaphoreType.DMA(())],
    )
    return pl.pallas_call(
        _combine_kernel,
        grid_spec=grid_spec,
        out_shape=jax.ShapeDtypeStruct((n_rows, d), F32),
        compiler_params=_params(("arbitrary",)),
        name="moe_combine",
    )(*tabs, ys, ld, wk, hb, xu, mod, gpost, sg, su, sd)


def _deinterleave(w):
    cols = w.shape[-1]
    perm = jnp.concatenate([jnp.arange(0, HEAD_DIM, 2), jnp.arange(1, HEAD_DIM, 2)])
    idx = (jnp.arange(cols // HEAD_DIM)[:, None] * HEAD_DIM + perm[None, :]).reshape(-1)
    return w[..., idx]


def _pad_in_proj(w_in):
    d = w_in.shape[0]
    o = 0
    parts = {}
    for name, width in (("qa", 256), ("ka", 128), ("va", 128), ("z", 256), ("xs", 256), ("bm", 128), ("cm", 128),
                        ("dtf", 4), ("dtb", 4), ("lx", 256), ("lg", 256), ("qd", 256), ("kd", 128), ("vd", 128)):
        parts[name] = w_in[:, o:o + width]
        o += width
    dt = jnp.concatenate([parts["dtf"], parts["dtb"], jnp.zeros((d, LANES - 8), w_in.dtype)], axis=1)
    cols = [_deinterleave(parts["qa"]), _deinterleave(parts["ka"]), parts["va"],
            _deinterleave(parts["qd"]), _deinterleave(parts["kd"]), parts["vd"],
            parts["z"], parts["xs"], parts["bm"], parts["cm"], dt, parts["lx"], parts["lg"]]
    return jnp.concatenate(cols, axis=1).astype(BF16)


def _rope_tables(seq):
    t = jnp.arange(seq)
    rowp = (t // GRID_W).astype(F32)
    colp = (t % GRID_W).astype(F32)
    axis_dim = HEAD_DIM // 2
    inv_freq = ROPE_THETA ** (-jnp.arange(0, axis_dim, 2, dtype=F32) / axis_dim)
    ang = jnp.concatenate([rowp[:, None] * inv_freq, colp[:, None] * inv_freq], axis=-1)
    cos, sin = jnp.cos(ang), jnp.sin(ang)
    cos_h = jnp.concatenate([cos, cos], axis=-1)
    sin_h = jnp.concatenate([-sin, sin], axis=-1)
    return jnp.tile(cos_h, (1, 4)), jnp.tile(sin_h, (1, 4))


def _block_diag(w):
    nb, bd, _ = w.shape
    eye = jnp.eye(nb, dtype=w.dtype)
    return (eye[:, None, :, None] * w[:, :, None, :]).reshape(nb * bd, nb * bd)


def _piece_table(counts, cap, stage0, slot0, rows, ids):
    ends = jnp.cumsum(counts, axis=1)
    q = jnp.arange(cap, dtype=jnp.int32)
    owner = jnp.sum((ends[:, None, :] <= q[None, :, None]).astype(jnp.int32), axis=-1)
    mine = owner[:, :, None] == ids
    pick = lambda v: jnp.sum(jnp.where(mine, v[:, None, :], 0), axis=-1)
    step = rows * (q[None, :] - pick(ends - counts))
    return (pick(stage0) + step).reshape(-1), (pick(slot0) + step).reshape(-1)


def _lane_row(fwd, bwd):
    return jnp.concatenate([fwd, bwd, jnp.zeros((LANES - 8,), F32)]).reshape(1, LANES)


def kernel(x, c, ctx, c_ctx, w_ada, b_ada, g_mix_pre, g_mix_post, g_ffn_pre, g_ffn_post, w_in, w_out, a_sink,
           ssd_conv_w, ssd_conv_b, ssd_dt_bias, ssd_a_log, ssd_d, ssd_norm, lru_conv_w, lru_conv_b, lru_w_a,
           lru_b_a, lru_w_i, lru_b_i, lru_lambda, d_q_norm, d_k_norm, router_w, router_bias, exp_w_gate,
           exp_w_up, exp_w_down, sh_w_gate, sh_w_up, sh_w_down):
    batch, seq, d = x.shape
    ctx_len = ctx.shape[1]
    depth = w_ada.shape[0]
    n_lat = batch * seq
    n_ctx = batch * ctx_len
    n_all = n_lat + n_ctx
    assert seq % TM == 0 and n_ctx % TM == 0 and seq % T_CONV == 0 and ctx_len % T_CONV == 0
    assert seq % TQ_WINDOW == 0 and seq % (SCAN_CHUNKS * CHUNK) == 0 and ctx_len % (SCAN_CHUNKS * CHUNK) == 0
    assert ctx_len <= KV_CHUNK and seq % KV_CHUNK == 0 and seq % TQ_GLOBAL == 0 and batch + 1 <= SUBLANES

    xu = jnp.concatenate([x.reshape(n_lat, d), ctx.reshape(n_ctx, d)], axis=0)
    cin = jnp.concatenate([c, c_ctx[None, :], jnp.zeros((SUBLANES - batch - 1, d), F32)], axis=0)
    mod_all = _adaln(cin, w_ada, b_ada)
    cos_t, sin_t = _rope_tables(seq)
    hm = jnp.kron(jnp.eye(4, dtype=F32), jnp.full((HEAD_DIM, HEAD_DIM), 1.0 / HEAD_DIM, F32)).astype(BF16)

    for l in range(depth):
        with_ctx = l < depth - 1
        mod = mod_all[l].reshape(SUBLANES * 6, 1, d)
        gq = jnp.tile(_deinterleave(d_q_norm[l]), 4).reshape(1, 256)
        gk = jnp.tile(_deinterleave(d_k_norm[l]), 2).reshape(1, LANES)
        qa, kat, va, qd, kdt, vd, z, xbc_raw, dt, lx_raw, lg = _inproj(
            xu, mod, g_mix_pre[l].reshape(1, d), _pad_in_proj(w_in[l]), cos_t, sin_t, gq, gk, hm,
            n_lat=n_lat, seq=seq, batch=batch)

        xbc, lu = _conv(xbc_raw, lx_raw, ssd_conv_w[l], ssd_conv_b[l].reshape(1, -1),
                        lru_conv_w[l], lru_conv_b[l].reshape(1, -1), n_lat=n_lat, seq=seq, ctx_len=ctx_len)
        yf, yb = _ssd(xbc, dt, _lane_row(ssd_dt_bias[l, 0], ssd_dt_bias[l, 1]),
                      _lane_row(ssd_a_log[l, 0], ssd_a_log[l, 1]), batch=batch, seq=seq, ctx_len=ctx_len)
        wg = jnp.stack([jnp.concatenate([_block_diag(lru_w_a[l, dd]), _block_diag(lru_w_i[l, dd])], axis=1)
                        for dd in range(2)]).astype(BF16)
        bg = jnp.concatenate([lru_b_a[l], lru_b_i[l]], axis=1).reshape(2, 1, 2 * LRU_WIDTH)
        hf, hb = _lru(lu, wg, bg, lru_lambda[l].reshape(2, 1, LRU_WIDTH), batch=batch, seq=seq, ctx_len=ctx_len)

        oa = _window_attn(qa, kat, va, a_sink[l], batch=batch, seq=seq, ctx_len=ctx_len)
        od = _dense_attn(qd, kdt, vd, None, q_row0=0, q_len=seq, tq=TQ_GLOBAL,
                         segs=[(n_lat, ctx_len), (0, seq)], batch=batch)
        if with_ctx:
            oa_c = _dense_attn(qa, kat, va, a_sink[l], q_row0=n_lat, q_len=ctx_len, tq=ctx_len,
                               segs=[(n_lat, ctx_len)], batch=batch)
            od_c = _dense_attn(qd, kdt, vd, None, q_row0=n_lat, q_len=ctx_len, tq=ctx_len,
                               segs=[(n_lat, ctx_len)], batch=batch)
            oa = jnp.concatenate([oa, oa_c], axis=0)
            od = jnp.concatenate([od, od_c], axis=0)
        n_rows = n_all if with_ctx else n_lat

        dsk = jnp.repeat(ssd_d[l], HEAD_DIM).reshape(1, 256)
        xu_mid = _outproj(xu, mod, g_mix_post[l].reshape(1, d), oa, od, yf, yb, xbc, z, dsk,
                          ssd_norm[l].reshape(1, 256), hf, hb, lg, w_out[l].astype(BF16),
                          n_rows=n_rows, n_lat=n_lat, seq=seq, batch=batch)

        hb_ffn, ld, wk, tab = _router(xu_mid, mod, g_ffn_pre[l].reshape(1, d), router_w[l].T.astype(BF16),
                                      router_bias[l].reshape(N_EXPERTS, 1), n_rows=n_rows, n_lat=n_lat,
                                      seq=seq, batch=batch)
        n_tiles = n_rows // TM
        tab = tab.reshape(n_tiles, SUBLANES, 2 * N_EXPERTS)[:, 0, :]
        seg_cnt, seg_loc = tab[:, :N_EXPERTS], tab[:, N_EXPERTS:]
        counts = jnp.sum(seg_cnt, axis=0)
        padded = (counts + BM_EXPERT - 1) // BM_EXPERT * BM_EXPERT
        padded_end = jnp.cumsum(padded)
        offs = padded_end - padded
        seg_off = offs[None, :] + jnp.cumsum(seg_cnt, axis=0) - seg_cnt
        n_blocks = (n_rows * TOP_K + n_tiles * N_EXPERTS * SEG_ALIGN) // BM_EXPERT + N_EXPERTS
        n_active = (padded_end[-1] // BM_EXPERT).astype(jnp.int32).reshape(1)
        block_start = jnp.arange(n_blocks, dtype=jnp.int32) * BM_EXPERT
        block_e = jnp.minimum(jnp.sum((padded_end[None, :] <= block_start[:, None]).astype(jnp.int32), axis=1),
                              N_EXPERTS - 1)
        ids = jnp.arange(N_EXPERTS, dtype=jnp.int32)
        n_big = seg_cnt // BIG_PIECE
        n_small = (seg_cnt % BIG_PIECE) // SEG_ALIGN
        tabs = (jnp.sum(n_big, axis=1), jnp.sum(n_small, axis=1), seg_loc[:, -1] + seg_cnt[:, -1],
                *_piece_table(n_big, BIG_MAX, seg_loc, seg_off, BIG_PIECE, ids),
                *_piece_table(n_small, SMALL_MAX, seg_loc + n_big * BIG_PIECE, seg_off + n_big * BIG_PIECE,
                              SEG_ALIGN, ids))
        xs = _dispatch(tabs, offs + counts, padded - counts, hb_ffn, ld, n_blocks * BM_EXPERT)
        has_rows = padded > 0
        later = jnp.logical_and(ids[None, :] > ids[:, None], has_rows[None, :])
        nxt_of = jnp.min(jnp.where(later, ids[None, :], N_EXPERTS), axis=1)
        nxt_of = jnp.where(nxt_of == N_EXPERTS, ids, nxt_of)
        slot_of = (jnp.cumsum(has_rows.astype(jnp.int32)) - 1) & 1
        own = block_e[:, None] == ids[None, :]
        next_e = jnp.sum(jnp.where(own, nxt_of[None, :], 0), axis=1)
        w_slot = jnp.sum(jnp.where(own, slot_of[None, :], 0), axis=1)
        ys = _experts(block_e, n_active, next_e, w_slot, xs, exp_w_gate, exp_w_up, exp_w_down, l)
        xu = _combine(tabs, ys, ld, wk, hb_ffn, xu_mid, mod, g_ffn_post[l].reshape(1, d), sh_w_gate[l].astype(BF16),
                      sh_w_up[l].astype(BF16), sh_w_down[l].astype(BF16),
                      n_rows=n_rows, n_lat=n_lat, seq=seq, batch=batch)
    return xu[:n_lat].reshape(batch, seq, d)
```

```python
import functools
import math

import jax
import jax.numpy as jnp
from jax import lax
from jax.experimental import pallas as pl
from jax.experimental.pallas import tpu as pltpu

F32 = jnp.float32
BF16 = jnp.bfloat16

HEAD_DIM = 64
GRID_W = 64
ROPE_THETA = 10000.0
NORM_EPS = 1e-6
NEG_INF = -1e30
A_HEADS, A_KV_HEADS, WINDOW = 4, 2, 128
SSD_HEADS, SSD_GROUPS, SSD_STATE, SSD_CONV = 4, 2, 64, 4
LRU_WIDTH, LRU_BLOCKS, LRU_CONV, LRU_C = 256, 4, 4, 8.0
D_HEADS, D_KV_HEADS = 4, 2
N_EXPERTS, N_EXPERT_GROUPS, TOPK_GROUPS, TOP_K = 64, 8, 4, 8
EXPERT_HIDDEN, SHARED_HIDDEN = 256, 256
ROUTED_SCALE = 2.5

LANES = 128
SUBLANES = 8

TM = 512
T_CONV = 256
CHUNK = 128
SCAN_CHUNKS = 2
TQ_GLOBAL = 256
TQ_WINDOW = 512
KV_CHUNK = 256
KV_UNROLL = 16
BM_EXPERT = 512
X_BUFFERS = 3
SEG_ALIGN = 16
STAGE_ROWS = TM * TOP_K + N_EXPERTS * SEG_ALIGN
BIG_PIECE = 64
BIG_MAX = STAGE_ROWS // BIG_PIECE
SMALL_MAX = N_EXPERTS * (BIG_PIECE // SEG_ALIGN - 1)
VMEM_LIMIT = 48 * 1024 * 1024

C_QA, C_KA, C_VA = 0, 256, 384
C_QD, C_KD, C_VD = 512, 768, 896
C_Z, C_XBC, C_DT = 1024, 1280, 1792
C_LX, C_LG = 1920, 2176
NP_IN = 2432


def _dot(a, b):
    return jnp.dot(a, b, preferred_element_type=F32)


def _dot_nt(a, b):
    return lax.dot_general(a, b, (((1,), (1,)), ((), ())), preferred_element_type=F32)


def _dot3(a, b):
    a1 = a.astype(BF16)
    r1 = a - a1.astype(F32)
    a2 = r1.astype(BF16)
    a3 = (r1 - a2.astype(F32)).astype(BF16)
    return _dot(a1, b) + _dot(a2, b) + _dot(a3, b)


def _dot3_left(a, b):
    b1 = b.astype(BF16)
    r1 = b - b1.astype(F32)
    b2 = r1.astype(BF16)
    b3 = (r1 - b2.astype(F32)).astype(BF16)
    return _dot(a, b1) + _dot(a, b2) + _dot(a, b3)


def _silu(x):
    return x * jax.nn.sigmoid(x)


def _softplus(x):
    return jnp.maximum(x, 0.0) + jnp.log1p(jnp.exp(-jnp.abs(x)))


def _rms(x, gain):
    return x * lax.rsqrt(jnp.mean(x * x, axis=-1, keepdims=True) + NORM_EPS) * gain


def _params(sem=None):
    return pltpu.CompilerParams(dimension_semantics=sem, vmem_limit_bytes=VMEM_LIMIT)


def _adaln_kernel(c_ref, w_ref, b_ref, o_ref):
    s = _silu(c_ref[...])
    o_ref[0] = _dot(s.astype(BF16), w_ref[0].astype(BF16)) + b_ref[0]


def _adaln(cin, w_ada, b_ada):
    depth, d, n6 = w_ada.shape
    tn = 1024
    return pl.pallas_call(
        _adaln_kernel,
        grid=(depth, n6 // tn),
        in_specs=[pl.BlockSpec((SUBLANES, d), lambda l, j: (0, 0)),
                  pl.BlockSpec((1, d, tn), lambda l, j: (l, 0, j)),
                  pl.BlockSpec((1, 1, tn), lambda l, j: (l, 0, j))],
        out_specs=pl.BlockSpec((1, SUBLANES, tn), lambda l, j: (l, 0, j)),
        out_shape=jax.ShapeDtypeStruct((depth, SUBLANES, n6), F32),
        compiler_params=_params(("parallel", "parallel")),
        name="adaln",
    )(cin, w_ada, b_ada.reshape(depth, 1, n6))


def _swap_halves(t):
    w = t.shape[1]
    lane = lax.broadcasted_iota(jnp.int32, (1, w), 1)
    first = (lane & 32) == 0
    return jnp.where(first, pltpu.roll(t, w - 32, axis=1), pltpu.roll(t, 32, axis=1))


def _inproj_kernel(x_ref, shift_ref, scale_ref, gpre_ref, w_ref, cos_ref, sin_ref, gq_ref, gk_ref, hm_ref,
                   qa_ref, kat_ref, va_ref, qd_ref, kdt_ref, vd_ref, z_ref, xbc_ref, dt_ref, lx_ref, lg_ref,
                   *, n_lat):
    i = pl.program_id(0)
    is_lat = i * TM < n_lat
    h = _rms(x_ref[...], gpre_ref[...])
    h = h * (1.0 + scale_ref[0]) + shift_ref[0]
    hb = h.astype(BF16)

    def sec(a, b):
        return _dot(hb, w_ref[:, a:b])

    cos = jnp.where(is_lat, cos_ref[...], 1.0)
    sin = jnp.where(is_lat, sin_ref[...], 0.0)

    def rope(t):
        w = t.shape[1]
        return t * cos[:, :w] + _swap_halves(t) * sin[:, :w]

    def head_norm(t, gain):
        w = t.shape[1]
        ms = _dot3(t * t, hm_ref[:w, :w])
        return t * lax.rsqrt(ms + NORM_EPS) * gain

    scale = HEAD_DIM ** -0.5
    qa_ref[...] = (rope(sec(C_QA, C_KA)) * scale).astype(BF16)
    kat_ref[...] = rope(sec(C_KA, C_VA)).T.astype(BF16)
    va_ref[...] = sec(C_VA, C_QD).astype(BF16)
    qd_ref[...] = (rope(head_norm(sec(C_QD, C_KD), gq_ref[...])) * scale).astype(BF16)
    kdt_ref[...] = rope(head_norm(sec(C_KD, C_VD), gk_ref[...])).T.astype(BF16)
    vd_ref[...] = sec(C_VD, C_Z).astype(BF16)
    z_ref[...] = sec(C_Z, C_XBC)
    xbc_ref[...] = sec(C_XBC, C_DT)
    dt_ref[...] = sec(C_DT, C_LX)
    lx_ref[...] = sec(C_LX, C_LG)
    lg_ref[...] = sec(C_LG, NP_IN)


def _mod_spec(chunk, n_lat, seq, batch, tile):
    def imap(i):
        row0 = i * tile
        seg = jnp.where(row0 < n_lat, row0 // seq, batch)
        return (seg * 6 + chunk, 0, 0)
    return imap


def _inproj(xu, mod, gpre, w_pad, cos_t, sin_t, gq, gk, hm, *, n_lat, seq, batch):
    n, d = xu.shape
    nt = n // TM
    spt = seq // TM
    row = lambda w: pl.BlockSpec((TM, w), lambda i: (i, 0))
    colT = pl.BlockSpec((LANES, TM), lambda i: (0, i))
    const = lambda a: pl.BlockSpec(a.shape, lambda i: (0,) * a.ndim)
    out_shapes = (
        jax.ShapeDtypeStruct((n, 256), BF16), jax.ShapeDtypeStruct((LANES, n), BF16),
        jax.ShapeDtypeStruct((n, LANES), BF16),
        jax.ShapeDtypeStruct((n, 256), BF16), jax.ShapeDtypeStruct((LANES, n), BF16),
        jax.ShapeDtypeStruct((n, LANES), BF16),
        jax.ShapeDtypeStruct((n, 256), F32), jax.ShapeDtypeStruct((n, 512), F32),
        jax.ShapeDtypeStruct((n, LANES), F32), jax.ShapeDtypeStruct((n, 256), F32),
        jax.ShapeDtypeStruct((n, 256), F32))
    return pl.pallas_call(
        functools.partial(_inproj_kernel, n_lat=n_lat),
        grid=(nt,),
        in_specs=[row(d),
                  pl.BlockSpec((1, 1, d), _mod_spec(0, n_lat, seq, batch, TM)),
                  pl.BlockSpec((1, 1, d), _mod_spec(1, n_lat, seq, batch, TM)),
                  const(gpre), const(w_pad),
                  pl.BlockSpec((TM, 256), lambda i: (i % spt, 0)),
                  pl.BlockSpec((TM, 256), lambda i: (i % spt, 0)),
                  const(gq), const(gk), const(hm)],
        out_specs=(row(256), colT, row(LANES), row(256), colT, row(LANES),
                   row(256), row(512), row(LANES), row(256), row(256)),
        out_shape=out_shapes,
        compiler_params=_params(("parallel",)),
        name="inproj",
    )(xu, mod, mod, gpre, w_pad, cos_t, sin_t, gq, gk, hm)


def _conv_kernel(xs_ref, xsp_ref, xsn_ref, xl_ref, xlp_ref, xln_ref, ws_ref, bs_ref, wl_ref, bl_ref,
                 os_ref, ol_ref, *, n_lat, seq, ctx_len):
    i = pl.program_id(0)
    row0 = i * T_CONV
    pos = jnp.where(row0 < n_lat, row0 % seq, (row0 - n_lat) % ctx_len)
    slen = jnp.where(row0 < n_lat, seq, ctx_len)
    first = pos == 0
    last = pos + T_CONV == slen
    row = lax.broadcasted_iota(jnp.int32, (T_CONV, 1), 0)

    def conv(x, prev, nxt, w, b):
        pm = jnp.where(first, 0.0, prev)
        nx = jnp.where(last, 0.0, nxt)
        xm1 = jnp.where(row == 0, pm[7:8, :], pltpu.roll(x, 1, axis=0))
        xm2 = jnp.where(row == 0, pm[6:7, :], jnp.where(row == 1, pm[7:8, :], pltpu.roll(x, 2, axis=0)))
        xp1 = jnp.where(row == T_CONV - 1, nx[0:1, :], pltpu.roll(x, T_CONV - 1, axis=0))
        return w[0:1, :] * xm2 + w[1:2, :] * xm1 + w[2:3, :] * x + w[3:4, :] * xp1 + b

    os_ref[...] = _silu(conv(xs_ref[...], xsp_ref[...], xsn_ref[...], ws_ref[...], bs_ref[...]))
    ol_ref[...] = conv(xl_ref[...], xlp_ref[...], xln_ref[...], wl_ref[...], bl_ref[...])


def _conv(xbc_raw, lx_raw, ws, bs, wl, bl, *, n_lat, seq, ctx_len):
    n = xbc_raw.shape[0]
    nt = n // T_CONV
    r8 = T_CONV // SUBLANES
    n8 = n // SUBLANES
    main = lambda w: pl.BlockSpec((T_CONV, w), lambda i: (i, 0))
    prev = lambda w: pl.BlockSpec((SUBLANES, w), lambda i: (jnp.maximum(i * r8 - 1, 0), 0))
    nxt = lambda w: pl.BlockSpec((SUBLANES, w), lambda i: (jnp.minimum((i + 1) * r8, n8 - 1), 0))
    const = lambda a: pl.BlockSpec(a.shape, lambda i: (0,) * a.ndim)
    return pl.pallas_call(
        functools.partial(_conv_kernel, n_lat=n_lat, seq=seq, ctx_len=ctx_len),
        grid=(nt,),
        in_specs=[main(512), prev(512), nxt(512), main(256), prev(256), nxt(256),
                  const(ws), const(bs), const(wl), const(bl)],
        out_specs=(main(512), main(256)),
        out_shape=(jax.ShapeDtypeStruct((n, 512), F32), jax.ShapeDtypeStruct((n, 256), F32)),
        compiler_params=_params(("parallel",)),
        name="conv",
    )(xbc_raw, xbc_raw, xbc_raw, lx_raw, lx_raw, lx_raw, ws, bs, wl, bl)


def _chunk_maps(batch, seq, ctx_len):
    ncx = ctx_len // (SCAN_CHUNKS * CHUNK)
    nl = seq // (SCAN_CHUNKS * CHUNK)
    lat_blocks = batch * nl

    def block(b, c):
        return jnp.where(c < ncx, lat_blocks + b * ncx + c, b * nl + (c - ncx))

    def fwd(b, k):
        return (block(b, k), 0)

    def bwd(b, k):
        c = jnp.where(k < ncx, ncx - 1 - k, ncx + (nl - 1 - (k - ncx)))
        return (block(b, c), 0)

    return fwd, bwd, ncx + nl


def _ssd_kernel(xf_ref, dtf_ref, xb_ref, dtb_ref, dtbias_ref, alog_ref, yf_ref, yb_ref, state_ref):
    k = pl.program_id(1)

    @pl.when(k == 0)
    def _():
        state_ref[...] = jnp.zeros_like(state_ref)

    ri = lax.broadcasted_iota(jnp.int32, (CHUNK, CHUNK), 0)
    ci = lax.broadcasted_iota(jnp.int32, (CHUNK, CHUNK), 1)
    lane_lo = ci < HEAD_DIM
    aneg = -jnp.exp(alog_ref[...])
    dtbias = dtbias_ref[...]

    order = [(d, s if d == 0 else SCAN_CHUNKS - 1 - s) for s in range(SCAN_CHUNKS) for d in range(2)]
    for d, sub in order:
        x_ref, dt_ref, y_ref = ((xf_ref, dtf_ref, yf_ref), (xb_ref, dtb_ref, yb_ref))[d]
        rws = slice(sub * CHUNK, (sub + 1) * CHUNK)
        causal = (ri >= ci) if d == 0 else (ci >= ri)
        tmat = jnp.where(causal, 1.0, 0.0).astype(BF16)
        xs = x_ref[rws, 0:256]
        bm = x_ref[rws, 256:384]
        cm = x_ref[rws, 384:512]
        dtp = _softplus(dt_ref[rws, :] + dtbias)
        acum = _dot3_left(tmat, dtp * aneg)
        acum_t = acum.T
        bt = bm.T.astype(BF16)
        cmb = cm.astype(BF16)
        bmb = bm.astype(BF16)
        tot_row = CHUNK - 1 if d == 0 else 0
        for p in range(2):
            cmask = jnp.where(lane_lo if p == 0 else jnp.logical_not(lane_lo), cmb, jnp.zeros_like(cmb))
            cb = _dot_nt(cmask, bmb)
            cols, dts, ys = [], [], []
            x_pair = xs[:, p * LANES:(p + 1) * LANES]
            for j in range(2):
                col = 4 * d + 2 * p + j
                colb = jnp.broadcast_to(acum[:, col:col + 1], (CHUNK, CHUNK))
                rowb = jnp.broadcast_to(acum_t[col:col + 1, :], (CHUNK, CHUNK))
                cols.append(colb)
                dts.append(jnp.broadcast_to(dtp[:, col:col + 1], (CHUNK, CHUNK)))
            col_pair = jnp.where(lane_lo, cols[0], cols[1])
            dt_pair = jnp.where(lane_lo, dts[0], dts[1])
            xdt = x_pair * dt_pair
            xdt_b = xdt.astype(BF16)
            for j in range(2):
                col = 4 * d + 2 * p + j
                rowb = jnp.broadcast_to(acum_t[col:col + 1, :], (CHUNK, CHUNK))
                decay = jnp.exp(jnp.where(causal, cols[j] - rowb, NEG_INF))
                ys.append(_dot((cb * decay).astype(BF16), xdt_b))
            y_intra = jnp.where(lane_lo, ys[0], ys[1])
            s_old = state_ref[d, p]
            y_inter = _dot(cmask, s_old.astype(BF16)) * jnp.exp(col_pair)
            y_ref[rws, p * LANES:(p + 1) * LANES] = y_intra + y_inter
            tot_pair = col_pair[tot_row:tot_row + 1, :]
            to_end = jnp.exp(tot_pair - col_pair)
            state_ref[d, p] = s_old * jnp.exp(tot_pair) + _dot(bt, (xdt * to_end).astype(BF16))


def _ssd(xbc, dt, dtbias_row, alog_row, *, batch, seq, ctx_len):
    n = xbc.shape[0]
    fwd, bwd, steps = _chunk_maps(batch, seq, ctx_len)
    rows = SCAN_CHUNKS * CHUNK
    const = lambda a: pl.BlockSpec(a.shape, lambda b, k: (0,) * a.ndim)
    return pl.pallas_call(
        _ssd_kernel,
        grid=(batch, steps),
        in_specs=[pl.BlockSpec((rows, 512), fwd), pl.BlockSpec((rows, LANES), fwd),
                  pl.BlockSpec((rows, 512), bwd), pl.BlockSpec((rows, LANES), bwd),
                  const(dtbias_row), const(alog_row)],
        out_specs=(pl.BlockSpec((rows, 256), fwd), pl.BlockSpec((rows, 256), bwd)),
        out_shape=(jax.ShapeDtypeStruct((n, 256), F32), jax.ShapeDtypeStruct((n, 256), F32)),
        scratch_shapes=[pltpu.VMEM((2, 2, CHUNK, LANES), F32)],
        compiler_params=_params(("parallel", "arbitrary")),
        name="ssd_scan",
    )(xbc, dt, xbc, dt, dtbias_row, alog_row)


def _linear_scan(a, b, reverse):
    n = a.shape[0]
    row = lax.broadcasted_iota(jnp.int32, (n, 1), 0)
    s = 1
    while s < n:
        if reverse:
            ok = row < n - s
            a_sh = jnp.where(ok, pltpu.roll(a, n - s, axis=0), 1.0)
            b_sh = jnp.where(ok, pltpu.roll(b, n - s, axis=0), 0.0)
        else:
            ok = row >= s
            a_sh = jnp.where(ok, pltpu.roll(a, s, axis=0), 1.0)
            b_sh = jnp.where(ok, pltpu.roll(b, s, axis=0), 0.0)
        b = b + a * b_sh
        a = a * a_sh
        s *= 2
    return a, b


def _lru_kernel(uf_ref, ub_ref, wg_ref, bg_ref, lam_ref, hf_ref, hb_ref, carry_ref):
    k = pl.program_id(1)

    @pl.when(k == 0)
    def _():
        carry_ref[...] = jnp.zeros_like(carry_ref)

    for d, (u_ref, h_ref) in enumerate(((uf_ref, hf_ref), (ub_ref, hb_ref))):
        u = u_ref[...]
        gates = _dot(u.astype(BF16), wg_ref[d]) + bg_ref[d]
        r = jax.nn.sigmoid(gates[:, :LRU_WIDTH])
        ig = jax.nn.sigmoid(gates[:, LRU_WIDTH:])
        log_a = -LRU_C * r * _softplus(-lam_ref[d])
        a = jnp.exp(log_a)
        inp = jnp.sqrt(-jnp.tanh(log_a) * (1.0 + a * a)) * (ig * u)
        a_cum, b_cum = _linear_scan(a, inp, reverse=(d == 1))
        h = b_cum + a_cum * carry_ref[d, 0:1, :]
        h_ref[...] = h
        last = 0 if d == 1 else u.shape[0] - 1
        carry_ref[d, 0:1, :] = h[last:last + 1, :]


def _lru(u, wg, bg, lam, *, batch, seq, ctx_len):
    n = u.shape[0]
    fwd, bwd, steps = _chunk_maps(batch, seq, ctx_len)
    rows = SCAN_CHUNKS * CHUNK
    const = lambda a: pl.BlockSpec(a.shape, lambda b, k: (0,) * a.ndim)
    return pl.pallas_call(
        _lru_kernel,
        grid=(batch, steps),
        in_specs=[pl.BlockSpec((rows, LRU_WIDTH), fwd), pl.BlockSpec((rows, LRU_WIDTH), bwd),
                  const(wg), const(bg), const(lam)],
        out_specs=(pl.BlockSpec((rows, LRU_WIDTH), fwd), pl.BlockSpec((rows, LRU_WIDTH), bwd)),
        out_shape=(jax.ShapeDtypeStruct((n, LRU_WIDTH), F32), jax.ShapeDtypeStruct((n, LRU_WIDTH), F32)),
        scratch_shapes=[pltpu.VMEM((2, SUBLANES, LRU_WIDTH), F32)],
        compiler_params=_params(("parallel", "arbitrary")),
        name="lru_scan",
    )(u, u, wg, bg, lam)


def _stack_heads(q, g):
    qf = q.astype(F32)
    lo = g * LANES
    return jnp.concatenate([qf[:, lo:lo + HEAD_DIM], qf[:, lo + HEAD_DIM:lo + LANES]], axis=0).astype(BF16)


def _value_lanes(g):
    lane = lax.broadcasted_iota(jnp.int32, (1, LANES), 1)
    return (lane < HEAD_DIM) if g == 0 else (lane >= HEAD_DIM)


def _aug_values(v, g):
    return jnp.where(_value_lanes(g), v, jnp.ones_like(v))


def _flash_init(rows, g, sink_pair):
    if sink_pair is None:
        return jnp.full((rows, 1), NEG_INF, F32), jnp.zeros((rows, LANES), F32)
    half = lax.broadcasted_iota(jnp.int32, (rows, 1), 0) < rows // 2
    m = jnp.where(half, sink_pair[0], sink_pair[1]).astype(F32)
    acc = jnp.broadcast_to(jnp.where(_value_lanes(g), 0.0, 1.0), (rows, LANES))
    return m, acc


def _flash_update(state, q2, kt, v_aug, mask=None):
    m, acc = state
    s = _dot(q2, kt)
    if mask is not None:
        s = jnp.where(mask, s, NEG_INF)
    m_new = jnp.maximum(m, jnp.max(s, axis=-1, keepdims=True))
    p = jnp.exp(s - m_new).astype(BF16)
    acc = jnp.exp(m - m_new) * acc + _dot(p, v_aug)
    return m_new, acc


def _flash_finish(states, tq):
    pieces = []
    for g, (_, acc) in enumerate(states):
        den = (1 - g) * HEAD_DIM
        o = acc[:, g * HEAD_DIM:(g + 1) * HEAD_DIM] / acc[:, den:den + 1]
        pieces += [o[:tq], o[tq:]]
    return jnp.concatenate(pieces, axis=1)


def _group_rows(g):
    return slice(g * HEAD_DIM, (g + 1) * HEAD_DIM)


def _dense_attn_kernel(*refs, tq, seg_lens, has_sink):
    refs = list(refs)
    sink_ref = refs.pop(0) if has_sink else None
    q_ref = refs.pop(0)
    o_ref = refs.pop()
    segs = [(refs[2 * i], refs[2 * i + 1], n) for i, n in enumerate(seg_lens)]
    q = q_ref[...]
    q2 = [_stack_heads(q, g) for g in range(2)]
    states = tuple(_flash_init(2 * tq, g, (sink_ref[2 * g], sink_ref[2 * g + 1]) if has_sink else None)
                   for g in range(2))
    for kt_ref, v_ref, n_keys in segs:
        if n_keys <= KV_CHUNK:
            v = v_ref[...]
            states = tuple(_flash_update(states[g], q2[g], kt_ref[_group_rows(g), :], _aug_values(v, g))
                           for g in range(2))
        else:
            def body(c, sts, kt_ref=kt_ref, v_ref=v_ref):
                off = pl.multiple_of(c * KV_CHUNK, KV_CHUNK)
                v = v_ref[pl.ds(off, KV_CHUNK), :]
                return tuple(_flash_update(sts[g], q2[g], kt_ref[_group_rows(g), pl.ds(off, KV_CHUNK)],
                                           _aug_values(v, g)) for g in range(2))
            states = lax.fori_loop(0, n_keys // KV_CHUNK, body, states, unroll=KV_UNROLL)
    o_ref[...] = _flash_finish(states, tq).astype(o_ref.dtype)


def _dense_attn(q, kt, v, sink, *, q_row0, q_len, tq, segs, batch):
    n = q.shape[0]
    qpb = q_len // tq
    q0 = q_row0 // tq
    in_specs, args = [], []
    if sink is not None:
        in_specs.append(pl.BlockSpec(memory_space=pltpu.SMEM))
        args.append(sink)
    in_specs.append(pl.BlockSpec((tq, 256), lambda b, i: (q0 + b * qpb + i, 0)))
    args.append(q)
    for row0, klen in segs:
        k0 = row0 // klen
        in_specs.append(pl.BlockSpec((LANES, klen), lambda b, i, k0=k0: (0, k0 + b)))
        in_specs.append(pl.BlockSpec((klen, LANES), lambda b, i, k0=k0: (k0 + b, 0)))
        args += [kt, v]
    return pl.pallas_call(
        functools.partial(_dense_attn_kernel, tq=tq, seg_lens=tuple(s[1] for s in segs), has_sink=sink is not None),
        grid=(batch, qpb),
        in_specs=in_specs,
        out_specs=pl.BlockSpec((tq, 256), lambda b, i: (b * qpb + i, 0)),
        out_shape=jax.ShapeDtypeStruct((batch * q_len, 256), BF16),
        compiler_params=_params(("parallel", "parallel")),
        name="dense_attn",
    )(*args)


def _window_attn_kernel(sink_ref, q_ref, ktc_ref, vc_ref, ktp_ref, vp_ref, ktm_ref, vm_ref, ktn_ref, vn_ref, o_ref,
                        *, n_tiles):
    n = pl.program_id(1)
    nsub = TQ_WINDOW // CHUNK
    iq = lax.broadcasted_iota(jnp.int32, (2 * CHUNK, CHUNK), 0) & (CHUNK - 1)
    jk = lax.broadcasted_iota(jnp.int32, (2 * CHUNK, CHUNK), 1)
    below = jk >= iq
    above = jk <= iq
    vctx = vc_ref[...]
    for j in range(nsub):
        cols = slice(j * CHUNK, (j + 1) * CHUNK)
        q = q_ref[cols, :]
        states = []
        for g in range(2):
            q2 = _stack_heads(q, g)
            rows = _group_rows(g)
            state = _flash_init(2 * CHUNK, g, (sink_ref[2 * g], sink_ref[2 * g + 1]))
            state = _flash_update(state, q2, ktc_ref[rows, :], _aug_values(vctx, g))
            state = _flash_update(state, q2, ktm_ref[rows, cols], _aug_values(vm_ref[cols, :], g))
            if j > 0:
                prev = slice((j - 1) * CHUNK, j * CHUNK)
                state = _flash_update(state, q2, ktm_ref[rows, prev], _aug_values(vm_ref[prev, :], g), below)
            else:
                state = _flash_update(state, q2, ktp_ref[rows, :], _aug_values(vp_ref[...], g),
                                      jnp.logical_and(below, n > 0))
            if j < nsub - 1:
                nxt = slice((j + 1) * CHUNK, (j + 2) * CHUNK)
                state = _flash_update(state, q2, ktm_ref[rows, nxt], _aug_values(vm_ref[nxt, :], g), above)
            else:
                state = _flash_update(state, q2, ktn_ref[rows, :], _aug_values(vn_ref[...], g),
                                      jnp.logical_and(above, n < n_tiles - 1))
            states.append(state)
        o_ref[cols, :] = _flash_finish(states, CHUNK).astype(o_ref.dtype)


def _window_attn(q, kt, v, sink, *, batch, seq, ctx_len):
    nt = seq // TQ_WINDOW
    nsub = TQ_WINDOW // CHUNK
    nb = seq // CHUNK
    ctx0 = (batch * seq) // ctx_len
    prev = lambda b, n: b * nb + jnp.maximum(n * nsub - 1, 0)
    nxt = lambda b, n: b * nb + jnp.minimum((n + 1) * nsub, nb - 1)
    return pl.pallas_call(
        functools.partial(_window_attn_kernel, n_tiles=nt),
        grid=(batch, nt),
        in_specs=[pl.BlockSpec(memory_space=pltpu.SMEM),
                  pl.BlockSpec((TQ_WINDOW, 256), lambda b, n: (b * nt + n, 0)),
                  pl.BlockSpec((LANES, ctx_len), lambda b, n: (0, ctx0 + b)),
                  pl.BlockSpec((ctx_len, LANES), lambda b, n: (ctx0 + b, 0)),
                  pl.BlockSpec((LANES, CHUNK), lambda b, n: (0, prev(b, n))),
                  pl.BlockSpec((CHUNK, LANES), lambda b, n: (prev(b, n), 0)),
                  pl.BlockSpec((LANES, TQ_WINDOW), lambda b, n: (0, b * nt + n)),
                  pl.BlockSpec((TQ_WINDOW, LANES), lambda b, n: (b * nt + n, 0)),
                  pl.BlockSpec((LANES, CHUNK), lambda b, n: (0, nxt(b, n))),
                  pl.BlockSpec((CHUNK, LANES), lambda b, n: (nxt(b, n), 0))],
        out_specs=pl.BlockSpec((TQ_WINDOW, 256), lambda b, n: (b * nt + n, 0)),
        out_shape=jax.ShapeDtypeStruct((batch * seq, 256), BF16),
        compiler_params=_params(("parallel", "parallel")),
        name="window_attn",
    )(sink, q, kt, v, kt, v, kt, v, kt, v)


def _gelu_tanh(x):
    return 0.5 * x * (1.0 + jnp.tanh(math.sqrt(2.0 / math.pi) * (x + 0.044715 * (x * x * x))))


def _outproj_kernel(x_ref, gate_ref, gpost_ref, oa_ref, od_ref, yf_ref, yb_ref, xs_ref, z_ref, dsk_ref, gn_ref,
                    hf_ref, hb_ref, lg_ref, w_ref, o_ref):
    y_ssd = (yf_ref[...] + yb_ref[...] + xs_ref[...] * dsk_ref[...]) * _silu(z_ref[...])
    ob = _rms(y_ssd, gn_ref[...])
    oc = (hf_ref[...] + hb_ref[...]) * _gelu_tanh(lg_ref[...])
    y = (_dot(oa_ref[...], w_ref[0:256, :]) + _dot(ob.astype(BF16), w_ref[256:512, :])
         + _dot(oc.astype(BF16), w_ref[512:768, :]) + _dot(od_ref[...], w_ref[768:1024, :]))
    o_ref[...] = x_ref[...] + gate_ref[0] * _rms(y, gpost_ref[...])


def _outproj(xu, mod, gpost, oa, od, yf, yb, xbc, z, dsk, gn, hf, hb, lg, w_out, *, n_rows, n_lat, seq, batch):
    d = xu.shape[1]
    row = lambda w: pl.BlockSpec((TM, w), lambda i: (i, 0))
    const = lambda a: pl.BlockSpec(a.shape, lambda i: (0,) * a.ndim)
    return pl.pallas_call(
        _outproj_kernel,
        grid=(n_rows // TM,),
        in_specs=[row(d), pl.BlockSpec((1, 1, d), _mod_spec(2, n_lat, seq, batch, TM)), const(gpost),
                  row(256), row(256), row(256), row(256), row(256), row(256), const(dsk), const(gn),
                  row(256), row(256), row(256), const(w_out)],
        out_specs=row(d),
        out_shape=jax.ShapeDtypeStruct((n_rows, d), F32),
        compiler_params=_params(("parallel",)),
        name="outproj",
    )(xu, mod, gpost, oa, od, yf, yb, xbc, z, dsk, gn, hf, hb, lg, w_out)


def _ceil_seg(c):
    return jnp.floor((c + (SEG_ALIGN - 1)) * (1.0 / SEG_ALIGN)) * SEG_ALIGN


def _router_kernel(x_ref, shift_ref, scale_ref, gpre_ref, rwt_ref, rb_ref, hb_ref, ld_ref, wk_ref, tab_ref):
    h = _rms(x_ref[...], gpre_ref[...])
    h = h * (1.0 + scale_ref[0]) + shift_ref[0]
    hb = h.astype(BF16)
    hb_ref[...] = hb

    scores = jax.nn.sigmoid(_dot_nt(rwt_ref[...], hb))
    biased = scores + rb_ref[...]
    gsz = N_EXPERTS // N_EXPERT_GROUPS
    sub = lax.broadcasted_iota(jnp.int32, (gsz, TM), 0)
    blocks, gscore = [], []
    for g in range(N_EXPERT_GROUPS):
        blk = biased[g * gsz:(g + 1) * gsz, :]
        m1 = jnp.max(blk, axis=0, keepdims=True)
        first = jnp.min(jnp.where(blk == m1, sub, gsz), axis=0, keepdims=True)
        m2 = jnp.max(jnp.where(sub == first, -jnp.inf, blk), axis=0, keepdims=True)
        blocks.append(blk)
        gscore.append(m1 + m2)
    masked = []
    for g in range(N_EXPERT_GROUPS):
        rank = jnp.zeros((1, TM), F32)
        for g2 in range(N_EXPERT_GROUPS):
            if g2 == g:
                continue
            beats = (gscore[g2] > gscore[g]) | ((gscore[g2] == gscore[g]) if g2 < g else False)
            rank = rank + jnp.where(beats, 1.0, 0.0)
        masked.append(jnp.where(rank < TOPK_GROUPS, blocks[g], -jnp.inf))
    vals = jnp.concatenate(masked, axis=0)
    eidx = lax.broadcasted_iota(jnp.int32, (N_EXPERTS, TM), 0)
    self32 = jnp.zeros((N_EXPERTS, TM), F32)
    rest = vals
    for _ in range(TOP_K):
        top = jnp.max(rest, axis=0, keepdims=True)
        first = jnp.min(jnp.where(rest == top, eidx, N_EXPERTS), axis=0, keepdims=True)
        hit = eidx == first
        self32 = jnp.where(hit, 1.0, self32)
        rest = jnp.where(hit, -jnp.inf, rest)
    sel = self32 > 0.5
    picked = jnp.where(sel, scores, 0.0)
    wdense = picked / jnp.sum(picked, axis=0, keepdims=True) * ROUTED_SCALE

    tr = lax.broadcasted_iota(jnp.int32, (TM, TM), 0)
    tc = lax.broadcasted_iota(jnp.int32, (TM, TM), 1)
    before = jnp.where(tr < tc, 1.0, 0.0).astype(BF16)
    selb = self32.astype(BF16)
    pos = _dot(selb, before)
    er = lax.broadcasted_iota(jnp.int32, (N_EXPERTS, N_EXPERTS), 0)
    ec = lax.broadcasted_iota(jnp.int32, (N_EXPERTS, N_EXPERTS), 1)
    lower = jnp.where(ec < er, 1.0, 0.0).astype(BF16)
    upper = jnp.where(er < ec, 1.0, 0.0).astype(BF16)
    ksel = _dot(lower, selb)
    cnt_col = _ceil_seg(jnp.sum(self32, axis=1, keepdims=True))
    loc_col = _dot3_left(lower, jnp.broadcast_to(cnt_col, (N_EXPERTS, LANES)))[:, 0:1]
    cnt_row = _ceil_seg(_dot_nt(jnp.ones((SUBLANES, TM), BF16), selb))
    loc_row = _dot3(cnt_row, upper)
    tab_ref[...] = jnp.concatenate([cnt_row, loc_row], axis=1).astype(jnp.int32)

    r8 = lax.broadcasted_iota(jnp.int32, (TOP_K, TM), 0)
    ld = jnp.zeros((TOP_K, TM), F32)
    wk = jnp.zeros((TOP_K, TM), F32)
    stage_row = pos + loc_col
    for k in range(TOP_K):
        one = sel & (ksel == float(k))
        ld = jnp.where(r8 == k, jnp.sum(jnp.where(one, stage_row, 0.0), axis=0, keepdims=True), ld)
        wk = jnp.where(r8 == k, jnp.sum(jnp.where(one, wdense, 0.0), axis=0, keepdims=True), wk)
    ld_ref[...] = ld.astype(jnp.int32)
    wk_ref[...] = wk


def _router(xu, mod, gpre, rwt, rb, *, n_rows, n_lat, seq, batch):
    d = xu.shape[1]
    row = lambda w: pl.BlockSpec((TM, w), lambda i: (i, 0))
    col = pl.BlockSpec((TOP_K, TM), lambda i: (0, i))
    const = lambda a: pl.BlockSpec(a.shape, lambda i: (0,) * a.ndim)
    return pl.pallas_call(
        _router_kernel,
        grid=(n_rows // TM,),
        in_specs=[row(d), pl.BlockSpec((1, 1, d), _mod_spec(3, n_lat, seq, batch, TM)),
                  pl.BlockSpec((1, 1, d), _mod_spec(4, n_lat, seq, batch, TM)),
                  const(gpre), const(rwt), const(rb)],
        out_specs=(row(d), col, col, pl.BlockSpec((SUBLANES, 2 * N_EXPERTS), lambda i: (i, 0))),
        out_shape=(jax.ShapeDtypeStruct((n_rows, d), BF16),
                   jax.ShapeDtypeStruct((TOP_K, n_rows), jnp.int32),
                   jax.ShapeDtypeStruct((TOP_K, n_rows), F32),
                   jax.ShapeDtypeStruct((n_rows // TM * SUBLANES, 2 * N_EXPERTS), jnp.int32)),
        compiler_params=_params(("parallel",)),
        name="router",
    )(xu, mod, mod, gpre, rwt, rb)


def _pow2_pieces(limit):
    bits, b = [], limit
    while b >= SEG_ALIGN:
        bits.append(b)
        b //= 2
    return bits


def _copy_pieces(n, src_ref, src0, dst_ref, dst0, sem, limit, wait, same_src=False):
    for bit in _pow2_pieces(limit):
        @pl.when((n & bit) != 0)
        def _():
            off = n & ~(2 * bit - 1)
            cp = pltpu.make_async_copy(src_ref.at[pl.ds(pl.multiple_of(src0 + (0 if same_src else off), SEG_ALIGN),
                                                          bit)],
                                       dst_ref.at[pl.ds(pl.multiple_of(dst0 + off, SEG_ALIGN), bit)], sem)
            cp.wait() if wait else cp.start()


N_PIECE_TABS = 7


def _piece_copies(tile, tabs, stage_ref, slots_ref, sem, to_slots, wait):
    nbig_ref, nsmall_ref, _, bsrc_ref, bdst_ref, ssrc_ref, sdst_ref = tabs
    for rows, n_ref, a_ref, b_ref, cap in ((BIG_PIECE, nbig_ref, bsrc_ref, bdst_ref, BIG_MAX),
                                           (SEG_ALIGN, nsmall_ref, ssrc_ref, sdst_ref, SMALL_MAX)):
        def body(p, c, rows=rows, a_ref=a_ref, b_ref=b_ref, cap=cap):
            src = stage_ref.at[pl.ds(pl.multiple_of(a_ref[tile * cap + p], SEG_ALIGN), rows)]
            dst = slots_ref.at[pl.ds(pl.multiple_of(b_ref[tile * cap + p], SEG_ALIGN), rows)]
            cp = pltpu.make_async_copy(src, dst, sem) if to_slots else pltpu.make_async_copy(dst, src, sem)
            cp.wait() if wait else cp.start()
            return c
        lax.fori_loop(0, n_ref[tile], body, 0)


def _used_blocks(tile, tabs):
    return (tabs[2][tile] + TM - 1) // TM


def _for_used_blocks(used, body):
    for b in range(TOP_K):
        body(b)
    for b in range(TOP_K, STAGE_ROWS // TM):
        @pl.when(b < used)
        def _():
            body(b)


def _stage_rows_iota():
    return lax.broadcasted_iota(jnp.int32, (TM // 2, TM), 0).astype(F32).astype(BF16)


def _pick_matrix(ld, base, vals, jrow):
    rel = (ld - base).astype(F32)
    rel = jnp.where(jnp.logical_and(rel >= 0.0, rel < TM // 2), rel, -1.0).astype(BF16)
    out = jnp.zeros((TM // 2, TM), BF16)
    for k in range(TOP_K):
        out = jnp.where(rel[k:k + 1, :] == jrow, vals[k:k + 1, :], out)
    return out


def _dispatch_kernel(*refs):
    tabs = refs[:N_PIECE_TABS]
    pstart_ref, npad_ref, hb_ref, ld_ref, xs_ref, stage, zbuf, sem, zsem = refs[N_PIECE_TABS:]
    i = pl.program_id(0)

    @pl.when(i == 0)
    def _():
        zbuf[...] = jnp.zeros_like(zbuf)
        for wait in (False, True):
            def body(e, c, wait=wait):
                _copy_pieces(npad_ref[e], zbuf, 0, xs_ref, pstart_ref[e], zsem, BM_EXPERT // 2, wait, same_src=True)
                return c
            lax.fori_loop(0, N_EXPERTS, body, 0)

    ld = ld_ref[...]
    hb = hb_ref[...]
    jrow = _stage_rows_iota()
    ones = jnp.ones((TOP_K, TM), BF16)

    cur = stage.at[i & 1]

    def block(b):
        for half in range(2):
            base = b * TM + half * (TM // 2)
            cur[base:base + TM // 2, :] = _dot(_pick_matrix(ld, base, ones, jrow), hb).astype(BF16)

    _for_used_blocks(_used_blocks(i, tabs), block)

    @pl.when(i > 0)
    def _():
        _piece_copies(i - 1, tabs, stage.at[(i - 1) & 1], xs_ref, sem, True, True)

    _piece_copies(i, tabs, cur, xs_ref, sem, True, False)

    @pl.when(i == pl.num_programs(0) - 1)
    def _():
        _piece_copies(i, tabs, cur, xs_ref, sem, True, True)


def _dispatch(tabs, pad_start, n_pad, hb, ld, n_slots):
    n, d = hb.shape
    grid_spec = pltpu.PrefetchScalarGridSpec(
        num_scalar_prefetch=N_PIECE_TABS + 2,
        grid=(n // TM,),
        in_specs=[pl.BlockSpec((TM, d), lambda i, *_: (i, 0)),
                  pl.BlockSpec((TOP_K, TM), lambda i, *_: (0, i))],
        out_specs=pl.BlockSpec(memory_space=pl.ANY),
        scratch_shapes=[pltpu.VMEM((2, STAGE_ROWS, d), BF16), pltpu.VMEM((BM_EXPERT // 2, d), BF16),
                        pltpu.SemaphoreType.DMA(()), pltpu.SemaphoreType.DMA(())],
    )
    return pl.pallas_call(
        _dispatch_kernel,
        grid_spec=grid_spec,
        out_shape=jax.ShapeDtypeStruct((n_slots, d), BF16),
        compiler_params=_params(("arbitrary",)),
        name="moe_dispatch",
    )(*tabs, pad_start, n_pad, hb, ld)


def _expert_kernel(be_ref, na_ref, nxt_ref, slot_ref, xs_hbm, wg_hbm, wu_hbm, wd_hbm, ys_ref,
                   xbuf, wg_raw, wu_raw, wd_raw, wgub, wdb, xsem, wsem, *, layer):
    i = pl.program_id(0)
    n_act = na_ref[0]

    def weight_copies(e, slot):
        return [pltpu.make_async_copy(src.at[layer, e], dst.at[slot], wsem.at[slot, j])
                for j, (src, dst) in enumerate(((wg_hbm, wg_raw), (wu_hbm, wu_raw), (wd_hbm, wd_raw)))]

    def rows_copy(blk):
        buf = lax.rem(blk, X_BUFFERS)
        return pltpu.make_async_copy(xs_hbm.at[pl.ds(pl.multiple_of(blk * BM_EXPERT, BM_EXPERT), BM_EXPERT)],
                                     xbuf.at[buf], xsem.at[buf])

    @pl.when(i < n_act)
    def _():
        e, slot = be_ref[i], slot_ref[i]

        @pl.when(i == 0)
        def _():
            for cp in weight_copies(e, slot):
                cp.start()
            for j in range(X_BUFFERS - 1):
                @pl.when(j < n_act)
                def _():
                    rows_copy(j).start()

        @pl.when(i + (X_BUFFERS - 1) < n_act)
        def _():
            rows_copy(i + (X_BUFFERS - 1)).start()

        @pl.when(jnp.logical_or(i == 0, e != be_ref[jnp.maximum(i - 1, 0)]))
        def _():
            for cp in weight_copies(e, slot):
                cp.wait()
            wgub[:, :EXPERT_HIDDEN] = wg_raw[slot].astype(BF16)
            wgub[:, EXPERT_HIDDEN:] = wu_raw[slot].astype(BF16)
            wdb[...] = wd_raw[slot].astype(BF16)

            @pl.when(nxt_ref[i] != e)
            def _():
                for cp in weight_copies(nxt_ref[i], 1 - slot):
                    cp.start()

        rows_copy(i).wait()
        gu = _dot(xbuf[lax.rem(i, X_BUFFERS)], wgub[...])
        hid = _silu(gu[:, :EXPERT_HIDDEN]) * gu[:, EXPERT_HIDDEN:]
        ys_ref[...] = _dot(hid.astype(BF16), wdb[...]).astype(ys_ref.dtype)


def _experts(block_e, n_active, next_e, w_slot, xs, wg, wu, wd, layer):
    n_slots, d = xs.shape
    nb = n_slots // BM_EXPERT
    tiles = pl.BlockSpec((BM_EXPERT, d), lambda i, be, na, *_: (jnp.minimum(i, na[0] - 1), 0))
    anywhere = pl.BlockSpec(memory_space=pl.ANY)
    grid_spec = pltpu.PrefetchScalarGridSpec(
        num_scalar_prefetch=4,
        grid=(nb,),
        in_specs=[anywhere, anywhere, anywhere, anywhere],
        out_specs=tiles,
        scratch_shapes=[pltpu.VMEM((X_BUFFERS, BM_EXPERT, d), BF16),
                        pltpu.VMEM((2, d, EXPERT_HIDDEN), F32), pltpu.VMEM((2, d, EXPERT_HIDDEN), F32),
                        pltpu.VMEM((2, EXPERT_HIDDEN, d), F32),
                        pltpu.VMEM((d, 2 * EXPERT_HIDDEN), BF16),
                        pltpu.VMEM((EXPERT_HIDDEN, d), BF16),
                        pltpu.SemaphoreType.DMA((X_BUFFERS,)), pltpu.SemaphoreType.DMA((2, 3))],
    )
    return pl.pallas_call(
        functools.partial(_expert_kernel, layer=layer),
        grid_spec=grid_spec,
        out_shape=jax.ShapeDtypeStruct((n_slots, d), BF16),
        compiler_params=_params(("arbitrary",)),
        name="moe_experts",
    )(block_e, n_active, next_e, w_slot, xs, wg, wu, wd)


def _combine_kernel(*refs):
    tabs = refs[:N_PIECE_TABS]
    (ys_ref, ld_ref, wk_ref, hb_ref, x_ref, gate_ref, gpost_ref, sg_ref, su_ref, sd_ref, o_ref,
     stage, acc_ref, sem) = refs[N_PIECE_TABS:]
    i = pl.program_id(0)

    @pl.when(i == 0)
    def _():
        stage[...] = jnp.zeros_like(stage)
        _piece_copies(0, tabs, stage.at[0], ys_ref, sem, False, False)

    cur = stage.at[i & 1]
    _piece_copies(i, tabs, cur, ys_ref, sem, False, True)

    @pl.when(i + 1 < pl.num_programs(0))
    def _():
        _piece_copies(i + 1, tabs, stage.at[(i + 1) & 1], ys_ref, sem, False, False)

    hb = hb_ref[...]
    acc_ref[...] = _dot((_silu(_dot(hb, sg_ref[...])) * _dot(hb, su_ref[...])).astype(BF16), sd_ref[...])

    ld = ld_ref[...]
    wkb = wk_ref[...].astype(BF16)
    jrow = _stage_rows_iota()

    def block(b):
        for half in range(2):
            base = b * TM + half * (TM // 2)
            weights = _pick_matrix(ld, base, wkb, jrow)
            acc_ref[...] += lax.dot_general(weights, cur[base:base + TM // 2, :], (((0,), (0,)), ((), ())),
                                            preferred_element_type=F32)

    _for_used_blocks(_used_blocks(i, tabs), block)
    o_ref[...] = x_ref[...] + gate_ref[0] * _rms(acc_ref[...], gpost_ref[...])


def _combine(tabs, ys, ld, wk, hb, xu, mod, gpost, sg, su, sd, *, n_rows, n_lat, seq, batch):
    d = xu.shape[1]
    row = lambda w: pl.BlockSpec((TM, w), lambda i, *_: (i, 0))
    col = pl.BlockSpec((TOP_K, TM), lambda i, *_: (0, i))
    const = lambda a: pl.BlockSpec(a.shape, lambda i, *_: (0,) * a.ndim)
    mod_map = _mod_spec(5, n_lat, seq, batch, TM)
    grid_spec = pltpu.PrefetchScalarGridSpec(
        num_scalar_prefetch=N_PIECE_TABS,
        grid=(n_rows // TM,),
        in_specs=[pl.BlockSpec(memory_space=pl.ANY), col, col, row(d), row(d),
                  pl.BlockSpec((1, 1, d), lambda i, *_: mod_map(i)),
                  const(gpost), const(sg), const(su), const(sd)],
        out_specs=row(d),
        scratch_shapes=[pltpu.VMEM((2, STAGE_ROWS, d), BF16), pltpu.VMEM((TM, d), F32), pltpu.SemaphoreType.DMA(())],
    )
    return pl.pallas_call(
        _combine_kernel,
        grid_spec=grid_spec,
        out_shape=jax.ShapeDtypeStruct((n_rows, d), F32),
        compiler_params=_params(("arbitrary",)),
        name="moe_combine",
    )(*tabs, ys, ld, wk, hb, xu, mod, gpost, sg, su, sd)


def _deinterleave(w):
    cols = w.shape[-1]
    perm = jnp.concatenate([jnp.arange(0, HEAD_DIM, 2), jnp.arange(1, HEAD_DIM, 2)])
    idx = (jnp.arange(cols // HEAD_DIM)[:, None] * HEAD_DIM + perm[None, :]).reshape(-1)
    return w[..., idx]


def _pad_in_proj(w_in):
    d = w_in.shape[0]
    o = 0
    parts = {}
    for name, width in (("qa", 256), ("ka", 128), ("va", 128), ("z", 256), ("xs", 256), ("bm", 128), ("cm", 128),
                        ("dtf", 4), ("dtb", 4), ("lx", 256), ("lg", 256), ("qd", 256), ("kd", 128), ("vd", 128)):
        parts[name] = w_in[:, o:o + width]
        o += width
    dt = jnp.concatenate([parts["dtf"], parts["dtb"], jnp.zeros((d, LANES - 8), w_in.dtype)], axis=1)
    cols = [_deinterleave(parts["qa"]), _deinterleave(parts["ka"]), parts["va"],
            _deinterleave(parts["qd"]), _deinterleave(parts["kd"]), parts["vd"],
            parts["z"], parts["xs"], parts["bm"], parts["cm"], dt, parts["lx"], parts["lg"]]
    return jnp.concatenate(cols, axis=1).astype(BF16)


def _rope_tables(seq):
    t = jnp.arange(seq)
    rowp = (t // GRID_W).astype(F32)
    colp = (t % GRID_W).astype(F32)
    axis_dim = HEAD_DIM // 2
    inv_freq = ROPE_THETA ** (-jnp.arange(0, axis_dim, 2, dtype=F32) / axis_dim)
    ang = jnp.concatenate([rowp[:, None] * inv_freq, colp[:, None] * inv_freq], axis=-1)
    cos, sin = jnp.cos(ang), jnp.sin(ang)
    cos_h = jnp.concatenate([cos, cos], axis=-1)
    sin_h = jnp.concatenate([-sin, sin], axis=-1)
    return jnp.tile(cos_h, (1, 4)), jnp.tile(sin_h, (1, 4))


def _block_diag(w):
    nb, bd, _ = w.shape
    eye = jnp.eye(nb, dtype=w.dtype)
    return (eye[:, None, :, None] * w[:, :, None, :]).reshape(nb * bd, nb * bd)


def _piece_table(counts, cap, stage0, slot0, rows, ids):
    ends = jnp.cumsum(counts, axis=1)
    q = jnp.arange(cap, dtype=jnp.int32)
    owner = jnp.sum((ends[:, None, :] <= q[None, :, None]).astype(jnp.int32), axis=-1)
    mine = owner[:, :, None] == ids
    pick = lambda v: jnp.sum(jnp.where(mine, v[:, None, :], 0), axis=-1)
    step = rows * (q[None, :] - pick(ends - counts))
    return (pick(stage0) + step).reshape(-1), (pick(slot0) + step).reshape(-1)


def _lane_row(fwd, bwd):
    return jnp.concatenate([fwd, bwd, jnp.zeros((LANES - 8,), F32)]).reshape(1, LANES)


def kernel(x, c, ctx, c_ctx, w_ada, b_ada, g_mix_pre, g_mix_post, g_ffn_pre, g_ffn_post, w_in, w_out, a_sink,
           ssd_conv_w, ssd_conv_b, ssd_dt_bias, ssd_a_log, ssd_d, ssd_norm, lru_conv_w, lru_conv_b, lru_w_a,
           lru_b_a, lru_w_i, lru_b_i, lru_lambda, d_q_norm, d_k_norm, router_w, router_bias, exp_w_gate,
           exp_w_up, exp_w_down, sh_w_gate, sh_w_up, sh_w_down):
    batch, seq, d = x.shape
    ctx_len = ctx.shape[1]
    depth = w_ada.shape[0]
    n_lat = batch * seq
    n_ctx = batch * ctx_len
    n_all = n_lat + n_ctx
    assert seq % TM == 0 and n_ctx % TM == 0 and seq % T_CONV == 0 and ctx_len % T_CONV == 0
    assert seq % TQ_WINDOW == 0 and seq % (SCAN_CHUNKS * CHUNK) == 0 and ctx_len % (SCAN_CHUNKS * CHUNK) == 0
    assert ctx_len <= KV_CHUNK and seq % KV_CHUNK == 0 and seq % TQ_GLOBAL == 0 and batch + 1 <= SUBLANES

    xu = jnp.concatenate([x.reshape(n_lat, d), ctx.reshape(n_ctx, d)], axis=0)
    cin = jnp.concatenate([c, c_ctx[None, :], jnp.zeros((SUBLANES - batch - 1, d), F32)], axis=0)
    mod_all = _adaln(cin, w_ada, b_ada)
    cos_t, sin_t = _rope_tables(seq)
    hm = jnp.kron(jnp.eye(4, dtype=F32), jnp.full((HEAD_DIM, HEAD_DIM), 1.0 / HEAD_DIM, F32)).astype(BF16)

    for l in range(depth):
        with_ctx = l < depth - 1
        mod = mod_all[l].reshape(SUBLANES * 6, 1, d)
        gq = jnp.tile(_deinterleave(d_q_norm[l]), 4).reshape(1, 256)
        gk = jnp.tile(_deinterleave(d_k_norm[l]), 2).reshape(1, LANES)
        qa, kat, va, qd, kdt, vd, z, xbc_raw, dt, lx_raw, lg = _inproj(
            xu, mod, g_mix_pre[l].reshape(1, d), _pad_in_proj(w_in[l]), cos_t, sin_t, gq, gk, hm,
            n_lat=n_lat, seq=seq, batch=batch)

        xbc, lu = _conv(xbc_raw, lx_raw, ssd_conv_w[l], ssd_conv_b[l].reshape(1, -1),
                        lru_conv_w[l], lru_conv_b[l].reshape(1, -1), n_lat=n_lat, seq=seq, ctx_len=ctx_len)
        yf, yb = _ssd(xbc, dt, _lane_row(ssd_dt_bias[l, 0], ssd_dt_bias[l, 1]),
                      _lane_row(ssd_a_log[l, 0], ssd_a_log[l, 1]), batch=batch, seq=seq, ctx_len=ctx_len)
        wg = jnp.stack([jnp.concatenate([_block_diag(lru_w_a[l, dd]), _block_diag(lru_w_i[l, dd])], axis=1)
                        for dd in range(2)]).astype(BF16)
        bg = jnp.concatenate([lru_b_a[l], lru_b_i[l]], axis=1).reshape(2, 1, 2 * LRU_WIDTH)
        hf, hb = _lru(lu, wg, bg, lru_lambda[l].reshape(2, 1, LRU_WIDTH), batch=batch, seq=seq, ctx_len=ctx_len)

        oa = _window_attn(qa, kat, va, a_sink[l], batch=batch, seq=seq, ctx_len=ctx_len)
        od = _dense_attn(qd, kdt, vd, None, q_row0=0, q_len=seq, tq=TQ_GLOBAL,
                         segs=[(n_lat, ctx_len), (0, seq)], batch=batch)
        if with_ctx:
            oa_c = _dense_attn(qa, kat, va, a_sink[l], q_row0=n_lat, q_len=ctx_len, tq=ctx_len,
                               segs=[(n_lat, ctx_len)], batch=batch)
            od_c = _dense_attn(qd, kdt, vd, None, q_row0=n_lat, q_len=ctx_len, tq=ctx_len,
                               segs=[(n_lat, ctx_len)], batch=batch)
            oa = jnp.concatenate([oa, oa_c], axis=0)
            od = jnp.concatenate([od, od_c], axis=0)
        n_rows = n_all if with_ctx else n_lat

        dsk = jnp.repeat(ssd_d[l], HEAD_DIM).reshape(1, 256)
        xu_mid = _outproj(xu, mod, g_mix_post[l].reshape(1, d), oa, od, yf, yb, xbc, z, dsk,
                          ssd_norm[l].reshape(1, 256), hf, hb, lg, w_out[l].astype(BF16),
                          n_rows=n_rows, n_lat=n_lat, seq=seq, batch=batch)

        hb_ffn, ld, wk, tab = _router(xu_mid, mod, g_ffn_pre[l].reshape(1, d), router_w[l].T.astype(BF16),
                                      router_bias[l].reshape(N_EXPERTS, 1), n_rows=n_rows, n_lat=n_lat,
                                      seq=seq, batch=batch)
        n_tiles = n_rows // TM
        tab = tab.reshape(n_tiles, SUBLANES, 2 * N_EXPERTS)[:, 0, :]
        seg_cnt, seg_loc = tab[:, :N_EXPERTS], tab[:, N_EXPERTS:]
        counts = jnp.sum(seg_cnt, axis=0)
        padded = (counts + BM_EXPERT - 1) // BM_EXPERT * BM_EXPERT
        padded_end = jnp.cumsum(padded)
        offs = padded_end - padded
        seg_off = offs[None, :] + jnp.cumsum(seg_cnt, axis=0) - seg_cnt
        n_blocks = (n_rows * TOP_K + n_tiles * N_EXPERTS * SEG_ALIGN) // BM_EXPERT + N_EXPERTS
        n_active = (padded_end[-1] // BM_EXPERT).astype(jnp.int32).reshape(1)
        block_start = jnp.arange(n_blocks, dtype=jnp.int32) * BM_EXPERT
        block_e = jnp.minimum(jnp.sum((padded_end[None, :] <= block_start[:, None]).astype(jnp.int32), axis=1),
                              N_EXPERTS - 1)
        ids = jnp.arange(N_EXPERTS, dtype=jnp.int32)
        n_big = seg_cnt // BIG_PIECE
        n_small = (seg_cnt % BIG_PIECE) // SEG_ALIGN
        tabs = (jnp.sum(n_big, axis=1), jnp.sum(n_small, axis=1), seg_loc[:, -1] + seg_cnt[:, -1],
                *_piece_table(n_big, BIG_MAX, seg_loc, seg_off, BIG_PIECE, ids),
                *_piece_table(n_small, SMALL_MAX, seg_loc + n_big * BIG_PIECE, seg_off + n_big * BIG_PIECE,
                              SEG_ALIGN, ids))
        xs = _dispatch(tabs, offs + counts, padded - counts, hb_ffn, ld, n_blocks * BM_EXPERT)
        has_rows = padded > 0
        later = jnp.logical_and(ids[None, :] > ids[:, None], has_rows[None, :])
        nxt_of = jnp.min(jnp.where(later, ids[None, :], N_EXPERTS), axis=1)
        nxt_of = jnp.where(nxt_of == N_EXPERTS, ids, nxt_of)
        slot_of = (jnp.cumsum(has_rows.astype(jnp.int32)) - 1) & 1
        own = block_e[:, None] == ids[None, :]
        next_e = jnp.sum(jnp.where(own, nxt_of[None, :], 0), axis=1)
        w_slot = jnp.sum(jnp.where(own, slot_of[None, :], 0), axis=1)
        ys = _experts(block_e, n_active, next_e, w_slot, xs, exp_w_gate, exp_w_up, exp_w_down, l)
        xu = _combine(tabs, ys, ld, wk, hb_ffn, xu_mid, mod, g_ffn_post[l].reshape(1, d), sh_w_gate[l].astype(BF16),
                      sh_w_up[l].astype(BF16), sh_w_down[l].astype(BF16),
                      n_rows=n_rows, n_lat=n_lat, seq=seq, batch=batch)
    return xu[:n_lat].reshape(batch, seq, d)
```

```python
import functools
import math

import jax
import jax.numpy as jnp
from jax import lax
from jax.experimental import pallas as pl
from jax.experimental.pallas import tpu as pltpu

F32 = jnp.float32
BF16 = jnp.bfloat16

HEAD_DIM = 64
GRID_W = 64
ROPE_THETA = 10000.0
NORM_EPS = 1e-6
NEG_INF = -1e30
A_HEADS, A_KV_HEADS, WINDOW = 4, 2, 128
SSD_HEADS, SSD_GROUPS, SSD_STATE, SSD_CONV = 4, 2, 64, 4
LRU_WIDTH, LRU_BLOCKS, LRU_CONV, LRU_C = 256, 4, 4, 8.0
D_HEADS, D_KV_HEADS = 4, 2
N_EXPERTS, N_EXPERT_GROUPS, TOPK_GROUPS, TOP_K = 64, 8, 4, 8
EXPERT_HIDDEN, SHARED_HIDDEN = 256, 256
ROUTED_SCALE = 2.5

LANES = 128
SUBLANES = 8

TM = 512
CHUNK = 128
SCAN_CHUNKS = 2
TQ_GLOBAL = 256
TQ_WINDOW = 512
KV_CHUNK = 256
KV_UNROLL = 16
BM_EXPERT = 512
X_BUFFERS = 3
SEG_ALIGN = 16
STAGE_ROWS = TM * TOP_K + N_EXPERTS * SEG_ALIGN
BIG_PIECE = 64
BIG_MAX = STAGE_ROWS // BIG_PIECE
SMALL_MAX = N_EXPERTS * (BIG_PIECE // SEG_ALIGN - 1)
VMEM_LIMIT = 48 * 1024 * 1024

C_QA, C_KA, C_VA = 0, 256, 384
C_QD, C_KD, C_VD = 512, 768, 896
C_Z, C_XBC, C_DT = 1024, 1280, 1792
C_LX, C_LG = 1920, 2176
NP_IN = 2432


def _dot(a, b):
    return jnp.dot(a, b, preferred_element_type=F32)


def _dot_nt(a, b):
    return lax.dot_general(a, b, (((1,), (1,)), ((), ())), preferred_element_type=F32)


def _dot3(a, b):
    a1 = a.astype(BF16)
    r1 = a - a1.astype(F32)
    a2 = r1.astype(BF16)
    a3 = (r1 - a2.astype(F32)).astype(BF16)
    return _dot(a1, b) + _dot(a2, b) + _dot(a3, b)


def _dot3_left(a, b):
    b1 = b.astype(BF16)
    r1 = b - b1.astype(F32)
    b2 = r1.astype(BF16)
    b3 = (r1 - b2.astype(F32)).astype(BF16)
    return _dot(a, b1) + _dot(a, b2) + _dot(a, b3)


def _silu(x):
    return x * jax.nn.sigmoid(x)


def _softplus(x):
    return jnp.maximum(x, 0.0) + jnp.log1p(jnp.exp(-jnp.abs(x)))


def _rms(x, gain):
    return x * lax.rsqrt(jnp.mean(x * x, axis=-1, keepdims=True) + NORM_EPS) * gain


def _params(sem=None):
    return pltpu.CompilerParams(dimension_semantics=sem, vmem_limit_bytes=VMEM_LIMIT)


def _adaln_kernel(c_ref, w_ref, b_ref, o_ref):
    s = _silu(c_ref[...])
    o_ref[0] = _dot(s.astype(BF16), w_ref[0].astype(BF16)) + b_ref[0]


def _adaln(cin, w_ada, b_ada):
    depth, d, n6 = w_ada.shape
    tn = 1024
    return pl.pallas_call(
        _adaln_kernel,
        grid=(depth, n6 // tn),
        in_specs=[pl.BlockSpec((SUBLANES, d), lambda l, j: (0, 0)),
                  pl.BlockSpec((1, d, tn), lambda l, j: (l, 0, j)),
                  pl.BlockSpec((1, 1, tn), lambda l, j: (l, 0, j))],
        out_specs=pl.BlockSpec((1, SUBLANES, tn), lambda l, j: (l, 0, j)),
        out_shape=jax.ShapeDtypeStruct((depth, SUBLANES, n6), F32),
        compiler_params=_params(("parallel", "parallel")),
        name="adaln",
    )(cin, w_ada, b_ada.reshape(depth, 1, n6))


def _swap_halves(t):
    w = t.shape[1]
    lane = lax.broadcasted_iota(jnp.int32, (1, w), 1)
    first = (lane & 32) == 0
    return jnp.where(first, pltpu.roll(t, w - 32, axis=1), pltpu.roll(t, 32, axis=1))


def _inproj_kernel(x_ref, xp_ref, xn_ref, shift_ref, scale_ref, gpre_ref, w_ref, cos_ref, sin_ref, gq_ref, gk_ref,
                   hm_ref, ws_ref, bs_ref, wl_ref, bl_ref,
                   qa_ref, kat_ref, va_ref, qd_ref, kdt_ref, vd_ref, z_ref, xbc_ref, dt_ref, lu_ref, lg_ref,
                   *, n_lat, seq, ctx_len):
    i = pl.program_id(0)
    is_lat = i * TM < n_lat

    def pre(x):
        return _rms(x, gpre_ref[...]) * (1.0 + scale_ref[0]) + shift_ref[0]

    h = pre(x_ref[...])
    hb = h.astype(BF16)
    hb_ext = jnp.concatenate([pre(xp_ref[...]), h, pre(xn_ref[...])], axis=0).astype(BF16)

    def sec(a, b):
        return _dot(hb, w_ref[:, a:b])

    row = lax.broadcasted_iota(jnp.int32, (TM, 1), 0)
    pos = jnp.where(is_lat, lax.rem(i * TM + row, seq), lax.rem(i * TM - n_lat + row, ctx_len))
    slen = jnp.where(is_lat, seq, ctx_len)

    def conv(a, b, w, bias):
        ext = _dot(hb_ext, w_ref[:, a:b])
        prev, x, nxt = ext[0:SUBLANES], ext[SUBLANES:SUBLANES + TM], ext[SUBLANES + TM:]
        xm1 = jnp.where(row == 0, prev[7:8, :], pltpu.roll(x, 1, axis=0))
        xm2 = jnp.where(row == 0, prev[6:7, :], jnp.where(row == 1, prev[7:8, :], pltpu.roll(x, 2, axis=0)))
        xp1 = jnp.where(row == TM - 1, nxt[0:1, :], pltpu.roll(x, TM - 1, axis=0))
        xm1 = jnp.where(pos >= 1, xm1, 0.0)
        xm2 = jnp.where(pos >= 2, xm2, 0.0)
        xp1 = jnp.where(pos <= slen - 2, xp1, 0.0)
        return w[0:1, :] * xm2 + w[1:2, :] * xm1 + w[2:3, :] * x + w[3:4, :] * xp1 + bias

    cos = jnp.where(is_lat, cos_ref[...], 1.0)
    sin = jnp.where(is_lat, sin_ref[...], 0.0)

    def rope(t):
        w = t.shape[1]
        return t * cos[:, :w] + _swap_halves(t) * sin[:, :w]

    def head_norm(t, gain):
        w = t.shape[1]
        ms = _dot3(t * t, hm_ref[:w, :w])
        return t * lax.rsqrt(ms + NORM_EPS) * gain

    scale = HEAD_DIM ** -0.5
    qa_ref[...] = (rope(sec(C_QA, C_KA)) * scale).astype(BF16)
    kat_ref[...] = rope(sec(C_KA, C_VA)).T.astype(BF16)
    va_ref[...] = sec(C_VA, C_QD).astype(BF16)
    qd_ref[...] = (rope(head_norm(sec(C_QD, C_KD), gq_ref[...])) * scale).astype(BF16)
    kdt_ref[...] = rope(head_norm(sec(C_KD, C_VD), gk_ref[...])).T.astype(BF16)
    vd_ref[...] = sec(C_VD, C_Z).astype(BF16)
    z_ref[...] = sec(C_Z, C_XBC)
    xbc_ref[...] = _silu(conv(C_XBC, C_DT, ws_ref[...], bs_ref[...]))
    dt_ref[...] = sec(C_DT, C_LX)
    lu_ref[...] = conv(C_LX, C_LG, wl_ref[...], bl_ref[...])
    lg_ref[...] = sec(C_LG, NP_IN)


def _mod_spec(chunk, n_lat, seq, batch, tile):
    def imap(i):
        row0 = i * tile
        seg = jnp.where(row0 < n_lat, row0 // seq, batch)
        return (seg * 6 + chunk, 0, 0)
    return imap


def _inproj(xu, mod, gpre, w_pad, cos_t, sin_t, gq, gk, hm, ws, bs, wl, bl, *, n_lat, seq, ctx_len, batch):
    n, d = xu.shape
    nt = n // TM
    spt = seq // TM
    r8 = TM // SUBLANES
    n8 = n // SUBLANES
    row = lambda w: pl.BlockSpec((TM, w), lambda i: (i, 0))
    colT = pl.BlockSpec((LANES, TM), lambda i: (0, i))
    const = lambda a: pl.BlockSpec(a.shape, lambda i: (0,) * a.ndim)
    out_shapes = (
        jax.ShapeDtypeStruct((n, 256), BF16), jax.ShapeDtypeStruct((LANES, n), BF16),
        jax.ShapeDtypeStruct((n, LANES), BF16),
        jax.ShapeDtypeStruct((n, 256), BF16), jax.ShapeDtypeStruct((LANES, n), BF16),
        jax.ShapeDtypeStruct((n, LANES), BF16),
        jax.ShapeDtypeStruct((n, 256), F32), jax.ShapeDtypeStruct((n, 512), F32),
        jax.ShapeDtypeStruct((n, LANES), F32), jax.ShapeDtypeStruct((n, 256), F32),
        jax.ShapeDtypeStruct((n, 256), F32))
    return pl.pallas_call(
        functools.partial(_inproj_kernel, n_lat=n_lat, seq=seq, ctx_len=ctx_len),
        grid=(nt,),
        in_specs=[row(d),
                  pl.BlockSpec((SUBLANES, d), lambda i: (jnp.maximum(i * r8 - 1, 0), 0)),
                  pl.BlockSpec((SUBLANES, d), lambda i: (jnp.minimum((i + 1) * r8, n8 - 1), 0)),
                  pl.BlockSpec((1, 1, d), _mod_spec(0, n_lat, seq, batch, TM)),
                  pl.BlockSpec((1, 1, d), _mod_spec(1, n_lat, seq, batch, TM)),
                  const(gpre), const(w_pad),
                  pl.BlockSpec((TM, 256), lambda i: (i % spt, 0)),
                  pl.BlockSpec((TM, 256), lambda i: (i % spt, 0)),
                  const(gq), const(gk), const(hm), const(ws), const(bs), const(wl), const(bl)],
        out_specs=(row(256), colT, row(LANES), row(256), colT, row(LANES),
                   row(256), row(512), row(LANES), row(256), row(256)),
        out_shape=out_shapes,
        compiler_params=_params(("parallel",)),
        name="inproj",
    )(xu, xu, xu, mod, mod, gpre, w_pad, cos_t, sin_t, gq, gk, hm, ws, bs, wl, bl)


def _chunk_maps(batch, seq, ctx_len):
    ncx = ctx_len // (SCAN_CHUNKS * CHUNK)
    nl = seq // (SCAN_CHUNKS * CHUNK)
    lat_blocks = batch * nl

    def block(b, c):
        return jnp.where(c < ncx, lat_blocks + b * ncx + c, b * nl + (c - ncx))

    def fwd(b, k):
        return (block(b, k), 0)

    def bwd(b, k):
        c = jnp.where(k < ncx, ncx - 1 - k, ncx + (nl - 1 - (k - ncx)))
        return (block(b, c), 0)

    return fwd, bwd, ncx + nl


def _ssd_kernel(xf_ref, dtf_ref, xb_ref, dtb_ref, dtbias_ref, alog_ref, yf_ref, yb_ref, state_ref):
    k = pl.program_id(1)

    @pl.when(k == 0)
    def _():
        state_ref[...] = jnp.zeros_like(state_ref)

    ri = lax.broadcasted_iota(jnp.int32, (CHUNK, CHUNK), 0)
    ci = lax.broadcasted_iota(jnp.int32, (CHUNK, CHUNK), 1)
    lane_lo = ci < HEAD_DIM
    aneg = -jnp.exp(alog_ref[...])
    dtbias = dtbias_ref[...]

    order = [(d, s if d == 0 else SCAN_CHUNKS - 1 - s) for s in range(SCAN_CHUNKS) for d in range(2)]
    for d, sub in order:
        x_ref, dt_ref, y_ref = ((xf_ref, dtf_ref, yf_ref), (xb_ref, dtb_ref, yb_ref))[d]
        rws = slice(sub * CHUNK, (sub + 1) * CHUNK)
        causal = (ri >= ci) if d == 0 else (ci >= ri)
        tmat = jnp.where(causal, 1.0, 0.0).astype(BF16)
        xs = x_ref[rws, 0:256]
        bm = x_ref[rws, 256:384]
        cm = x_ref[rws, 384:512]
        dtp = _softplus(dt_ref[rws, :] + dtbias)
        acum = _dot3_left(tmat, dtp * aneg)
        acum_t = acum.T
        bt = bm.T.astype(BF16)
        cmb = cm.astype(BF16)
        bmb = bm.astype(BF16)
        tot_row = CHUNK - 1 if d == 0 else 0
        for p in range(2):
            cmask = jnp.where(lane_lo if p == 0 else jnp.logical_not(lane_lo), cmb, jnp.zeros_like(cmb))
            cb = _dot_nt(cmask, bmb)
            cols, dts, ys = [], [], []
            x_pair = xs[:, p * LANES:(p + 1) * LANES]
            for j in range(2):
                col = 4 * d + 2 * p + j
                colb = jnp.broadcast_to(acum[:, col:col + 1], (CHUNK, CHUNK))
                rowb = jnp.broadcast_to(acum_t[col:col + 1, :], (CHUNK, CHUNK))
                cols.append(colb)
                dts.append(jnp.broadcast_to(dtp[:, col:col + 1], (CHUNK, CHUNK)))
            col_pair = jnp.where(lane_lo, cols[0], cols[1])
            dt_pair = jnp.where(lane_lo, dts[0], dts[1])
            xdt = x_pair * dt_pair
            xdt_b = xdt.astype(BF16)
            for j in range(2):
                col = 4 * d + 2 * p + j
                rowb = jnp.broadcast_to(acum_t[col:col + 1, :], (CHUNK, CHUNK))
                decay = jnp.exp(jnp.where(causal, cols[j] - rowb, NEG_INF))
                ys.append(_dot((cb * decay).astype(BF16), xdt_b))
            y_intra = jnp.where(lane_lo, ys[0], ys[1])
            s_old = state_ref[d, p]
            y_inter = _dot(cmask, s_old.astype(BF16)) * jnp.exp(col_pair)
            y_ref[rws, p * LANES:(p + 1) * LANES] = y_intra + y_inter
            tot_pair = col_pair[tot_row:tot_row + 1, :]
            to_end = jnp.exp(tot_pair - col_pair)
            state_ref[d, p] = s_old * jnp.exp(tot_pair) + _dot(bt, (xdt * to_end).astype(BF16))


def _ssd(xbc, dt, dtbias_row, alog_row, *, batch, seq, ctx_len):
    n = xbc.shape[0]
    fwd, bwd, steps = _chunk_maps(batch, seq, ctx_len)
    rows = SCAN_CHUNKS * CHUNK
    const = lambda a: pl.BlockSpec(a.shape, lambda b, k: (0,) * a.ndim)
    return pl.pallas_call(
        _ssd_kernel,
        grid=(batch, steps),
        in_specs=[pl.BlockSpec((rows, 512), fwd), pl.BlockSpec((rows, LANES), fwd),
                  pl.BlockSpec((rows, 512), bwd), pl.BlockSpec((rows, LANES), bwd),
                  const(dtbias_row), const(alog_row)],
        out_specs=(pl.BlockSpec((rows, 256), fwd), pl.BlockSpec((rows, 256), bwd)),
        out_shape=(jax.ShapeDtypeStruct((n, 256), F32), jax.ShapeDtypeStruct((n, 256), F32)),
        scratch_shapes=[pltpu.VMEM((2, 2, CHUNK, LANES), F32)],
        compiler_params=_params(("parallel", "arbitrary")),
        name="ssd_scan",
    )(xbc, dt, xbc, dt, dtbias_row, alog_row)


def _linear_scan(a, b, reverse):
    n = a.shape[0]
    row = lax.broadcasted_iota(jnp.int32, (n, 1), 0)
    s = 1
    while s < n:
        if reverse:
            ok = row < n - s
            a_sh = jnp.where(ok, pltpu.roll(a, n - s, axis=0), 1.0)
            b_sh = jnp.where(ok, pltpu.roll(b, n - s, axis=0), 0.0)
        else:
            ok = row >= s
            a_sh = jnp.where(ok, pltpu.roll(a, s, axis=0), 1.0)
            b_sh = jnp.where(ok, pltpu.roll(b, s, axis=0), 0.0)
        b = b + a * b_sh
        a = a * a_sh
        s *= 2
    return a, b


def _lru_kernel(uf_ref, ub_ref, wg_ref, bg_ref, lam_ref, hf_ref, hb_ref, carry_ref):
    k = pl.program_id(1)

    @pl.when(k == 0)
    def _():
        carry_ref[...] = jnp.zeros_like(carry_ref)

    for d, (u_ref, h_ref) in enumerate(((uf_ref, hf_ref), (ub_ref, hb_ref))):
        u = u_ref[...]
        gates = _dot(u.astype(BF16), wg_ref[d]) + bg_ref[d]
        r = jax.nn.sigmoid(gates[:, :LRU_WIDTH])
        ig = jax.nn.sigmoid(gates[:, LRU_WIDTH:])
        log_a = -LRU_C * r * _softplus(-lam_ref[d])
        a = jnp.exp(log_a)
        inp = jnp.sqrt(-jnp.tanh(log_a) * (1.0 + a * a)) * (ig * u)
        a_cum, b_cum = _linear_scan(a, inp, reverse=(d == 1))
        h = b_cum + a_cum * carry_ref[d, 0:1, :]
        h_ref[...] = h
        last = 0 if d == 1 else u.shape[0] - 1
        carry_ref[d, 0:1, :] = h[last:last + 1, :]


def _lru(u, wg, bg, lam, *, batch, seq, ctx_len):
    n = u.shape[0]
    fwd, bwd, steps = _chunk_maps(batch, seq, ctx_len)
    rows = SCAN_CHUNKS * CHUNK
    const = lambda a: pl.BlockSpec(a.shape, lambda b, k: (0,) * a.ndim)
    return pl.pallas_call(
        _lru_kernel,
        grid=(batch, steps),
        in_specs=[pl.BlockSpec((rows, LRU_WIDTH), fwd), pl.BlockSpec((rows, LRU_WIDTH), bwd),
                  const(wg), const(bg), const(lam)],
        out_specs=(pl.BlockSpec((rows, LRU_WIDTH), fwd), pl.BlockSpec((rows, LRU_WIDTH), bwd)),
        out_shape=(jax.ShapeDtypeStruct((n, LRU_WIDTH), F32), jax.ShapeDtypeStruct((n, LRU_WIDTH), F32)),
        scratch_shapes=[pltpu.VMEM((2, SUBLANES, LRU_WIDTH), F32)],
        compiler_params=_params(("parallel", "arbitrary")),
        name="lru_scan",
    )(u, u, wg, bg, lam)


def _stack_heads(q, g):
    qf = q.astype(F32)
    lo = g * LANES
    return jnp.concatenate([qf[:, lo:lo + HEAD_DIM], qf[:, lo + HEAD_DIM:lo + LANES]], axis=0).astype(BF16)


def _value_lanes(g):
    lane = lax.broadcasted_iota(jnp.int32, (1, LANES), 1)
    return (lane < HEAD_DIM) if g == 0 else (lane >= HEAD_DIM)


def _aug_values(v, g):
    return jnp.where(_value_lanes(g), v, jnp.ones_like(v))


def _flash_init(rows, g, sink_pair):
    if sink_pair is None:
        return jnp.full((rows, 1), NEG_INF, F32), jnp.zeros((rows, LANES), F32)
    half = lax.broadcasted_iota(jnp.int32, (rows, 1), 0) < rows // 2
    m = jnp.where(half, sink_pair[0], sink_pair[1]).astype(F32)
    acc = jnp.broadcast_to(jnp.where(_value_lanes(g), 0.0, 1.0), (rows, LANES))
    return m, acc


def _flash_update(state, q2, kt, v_aug, mask=None):
    m, acc = state
    s = _dot(q2, kt)
    if mask is not None:
        s = jnp.where(mask, s, NEG_INF)
    m_new = jnp.maximum(m, jnp.max(s, axis=-1, keepdims=True))
    p = jnp.exp(s - m_new).astype(BF16)
    acc = jnp.exp(m - m_new) * acc + _dot(p, v_aug)
    return m_new, acc


def _flash_finish(states, tq):
    pieces = []
    for g, (_, acc) in enumerate(states):
        den = (1 - g) * HEAD_DIM
        o = acc[:, g * HEAD_DIM:(g + 1) * HEAD_DIM] / acc[:, den:den + 1]
        pieces += [o[:tq], o[tq:]]
    return jnp.concatenate(pieces, axis=1)


def _group_rows(g):
    return slice(g * HEAD_DIM, (g + 1) * HEAD_DIM)


def _dense_attn_kernel(*refs, tq, seg_lens, has_sink):
    refs = list(refs)
    sink_ref = refs.pop(0) if has_sink else None
    q_ref = refs.pop(0)
    o_ref = refs.pop()
    segs = [(refs[2 * i], refs[2 * i + 1], n) for i, n in enumerate(seg_lens)]
    q = q_ref[...]
    q2 = [_stack_heads(q, g) for g in range(2)]
    states = tuple(_flash_init(2 * tq, g, (sink_ref[2 * g], sink_ref[2 * g + 1]) if has_sink else None)
                   for g in range(2))
    for kt_ref, v_ref, n_keys in segs:
        if n_keys <= KV_CHUNK:
            v = v_ref[...]
            states = tuple(_flash_update(states[g], q2[g], kt_ref[_group_rows(g), :], _aug_values(v, g))
                           for g in range(2))
        else:
            def body(c, sts, kt_ref=kt_ref, v_ref=v_ref):
                off = pl.multiple_of(c * KV_CHUNK, KV_CHUNK)
                v = v_ref[pl.ds(off, KV_CHUNK), :]
                return tuple(_flash_update(sts[g], q2[g], kt_ref[_group_rows(g), pl.ds(off, KV_CHUNK)],
                                           _aug_values(v, g)) for g in range(2))
            states = lax.fori_loop(0, n_keys // KV_CHUNK, body, states, unroll=KV_UNROLL)
    o_ref[...] = _flash_finish(states, tq).astype(o_ref.dtype)


def _dense_attn(q, kt, v, sink, *, q_row0, q_len, tq, segs, batch):
    n = q.shape[0]
    qpb = q_len // tq
    q0 = q_row0 // tq
    in_specs, args = [], []
    if sink is not None:
        in_specs.append(pl.BlockSpec(memory_space=pltpu.SMEM))
        args.append(sink)
    in_specs.append(pl.BlockSpec((tq, 256), lambda b, i: (q0 + b * qpb + i, 0)))
    args.append(q)
    for row0, klen in segs:
        k0 = row0 // klen
        in_specs.append(pl.BlockSpec((LANES, klen), lambda b, i, k0=k0: (0, k0 + b)))
        in_specs.append(pl.BlockSpec((klen, LANES), lambda b, i, k0=k0: (k0 + b, 0)))
        args += [kt, v]
    return pl.pallas_call(
        functools.partial(_dense_attn_kernel, tq=tq, seg_lens=tuple(s[1] for s in segs), has_sink=sink is not None),
        grid=(batch, qpb),
        in_specs=in_specs,
        out_specs=pl.BlockSpec((tq, 256), lambda b, i: (b * qpb + i, 0)),
        out_shape=jax.ShapeDtypeStruct((batch * q_len, 256), BF16),
        compiler_params=_params(("parallel", "parallel")),
        name="dense_attn",
    )(*args)


def _window_attn_kernel(sink_ref, q_ref, ktc_ref, vc_ref, ktp_ref, vp_ref, ktm_ref, vm_ref, ktn_ref, vn_ref, o_ref,
                        *, n_tiles):
    n = pl.program_id(1)
    nsub = TQ_WINDOW // CHUNK
    iq = lax.broadcasted_iota(jnp.int32, (2 * CHUNK, CHUNK), 0) & (CHUNK - 1)
    jk = lax.broadcasted_iota(jnp.int32, (2 * CHUNK, CHUNK), 1)
    below = jk >= iq
    above = jk <= iq
    vctx = vc_ref[...]
    for j in range(nsub):
        cols = slice(j * CHUNK, (j + 1) * CHUNK)
        q = q_ref[cols, :]
        states = []
        for g in range(2):
            q2 = _stack_heads(q, g)
            rows = _group_rows(g)
            state = _flash_init(2 * CHUNK, g, (sink_ref[2 * g], sink_ref[2 * g + 1]))
            state = _flash_update(state, q2, ktc_ref[rows, :], _aug_values(vctx, g))
            state = _flash_update(state, q2, ktm_ref[rows, cols], _aug_values(vm_ref[cols, :], g))
            if j > 0:
                prev = slice((j - 1) * CHUNK, j * CHUNK)
                state = _flash_update(state, q2, ktm_ref[rows, prev], _aug_values(vm_ref[prev, :], g), below)
            else:
                state = _flash_update(state, q2, ktp_ref[rows, :], _aug_values(vp_ref[...], g),
                                      jnp.logical_and(below, n > 0))
            if j < nsub - 1:
                nxt = slice((j + 1) * CHUNK, (j + 2) * CHUNK)
                state = _flash_update(state, q2, ktm_ref[rows, nxt], _aug_values(vm_ref[nxt, :], g), above)
            else:
                state = _flash_update(state, q2, ktn_ref[rows, :], _aug_values(vn_ref[...], g),
                                      jnp.logical_and(above, n < n_tiles - 1))
            states.append(state)
        o_ref[cols, :] = _flash_finish(states, CHUNK).astype(o_ref.dtype)


def _window_attn(q, kt, v, sink, *, batch, seq, ctx_len):
    nt = seq // TQ_WINDOW
    nsub = TQ_WINDOW // CHUNK
    nb = seq // CHUNK
    ctx0 = (batch * seq) // ctx_len
    prev = lambda b, n: b * nb + jnp.maximum(n * nsub - 1, 0)
    nxt = lambda b, n: b * nb + jnp.minimum((n + 1) * nsub, nb - 1)
    return pl.pallas_call(
        functools.partial(_window_attn_kernel, n_tiles=nt),
        grid=(batch, nt),
        in_specs=[pl.BlockSpec(memory_space=pltpu.SMEM),
                  pl.BlockSpec((TQ_WINDOW, 256), lambda b, n: (b * nt + n, 0)),
                  pl.BlockSpec((LANES, ctx_len), lambda b, n: (0, ctx0 + b)),
                  pl.BlockSpec((ctx_len, LANES), lambda b, n: (ctx0 + b, 0)),
                  pl.BlockSpec((LANES, CHUNK), lambda b, n: (0, prev(b, n))),
                  pl.BlockSpec((CHUNK, LANES), lambda b, n: (prev(b, n), 0)),
                  pl.BlockSpec((LANES, TQ_WINDOW), lambda b, n: (0, b * nt + n)),
                  pl.BlockSpec((TQ_WINDOW, LANES), lambda b, n: (b * nt + n, 0)),
                  pl.BlockSpec((LANES, CHUNK), lambda b, n: (0, nxt(b, n))),
                  pl.BlockSpec((CHUNK, LANES), lambda b, n: (nxt(b, n), 0))],
        out_specs=pl.BlockSpec((TQ_WINDOW, 256), lambda b, n: (b * nt + n, 0)),
        out_shape=jax.ShapeDtypeStruct((batch * seq, 256), BF16),
        compiler_params=_params(("parallel", "parallel")),
        name="window_attn",
    )(sink, q, kt, v, kt, v, kt, v, kt, v)


def _gelu_tanh(x):
    return 0.5 * x * (1.0 + jnp.tanh(math.sqrt(2.0 / math.pi) * (x + 0.044715 * (x * x * x))))


def _outproj_kernel(x_ref, gate_ref, gpost_ref, oa_ref, od_ref, yf_ref, yb_ref, xs_ref, z_ref, dsk_ref, gn_ref,
                    hf_ref, hb_ref, lg_ref, w_ref, o_ref):
    y_ssd = (yf_ref[...] + yb_ref[...] + xs_ref[...] * dsk_ref[...]) * _silu(z_ref[...])
    ob = _rms(y_ssd, gn_ref[...])
    oc = (hf_ref[...] + hb_ref[...]) * _gelu_tanh(lg_ref[...])
    y = (_dot(oa_ref[...], w_ref[0:256, :]) + _dot(ob.astype(BF16), w_ref[256:512, :])
         + _dot(oc.astype(BF16), w_ref[512:768, :]) + _dot(od_ref[...], w_ref[768:1024, :]))
    o_ref[...] = x_ref[...] + gate_ref[0] * _rms(y, gpost_ref[...])


def _outproj(xu, mod, gpost, oa, od, yf, yb, xbc, z, dsk, gn, hf, hb, lg, w_out, *, n_rows, n_lat, seq, batch):
    d = xu.shape[1]
    row = lambda w: pl.BlockSpec((TM, w), lambda i: (i, 0))
    const = lambda a: pl.BlockSpec(a.shape, lambda i: (0,) * a.ndim)
    return pl.pallas_call(
        _outproj_kernel,
        grid=(n_rows // TM,),
        in_specs=[row(d), pl.BlockSpec((1, 1, d), _mod_spec(2, n_lat, seq, batch, TM)), const(gpost),
                  row(256), row(256), row(256), row(256), row(256), row(256), const(dsk), const(gn),
                  row(256), row(256), row(256), const(w_out)],
        out_specs=row(d),
        out_shape=jax.ShapeDtypeStruct((n_rows, d), F32),
        compiler_params=_params(("parallel",)),
        name="outproj",
    )(xu, mod, gpost, oa, od, yf, yb, xbc, z, dsk, gn, hf, hb, lg, w_out)


def _ceil_seg(c):
    return jnp.floor((c + (SEG_ALIGN - 1)) * (1.0 / SEG_ALIGN)) * SEG_ALIGN


def _router_kernel(x_ref, shift_ref, scale_ref, gpre_ref, rwt_ref, rb_ref, hb_ref, ld_ref, wk_ref, tab_ref):
    h = _rms(x_ref[...], gpre_ref[...])
    h = h * (1.0 + scale_ref[0]) + shift_ref[0]
    hb = h.astype(BF16)
    hb_ref[...] = hb

    scores = jax.nn.sigmoid(_dot_nt(rwt_ref[...], hb))
    biased = scores + rb_ref[...]
    gsz = N_EXPERTS // N_EXPERT_GROUPS
    sub = lax.broadcasted_iota(jnp.int32, (gsz, TM), 0)
    blocks, gscore = [], []
    for g in range(N_EXPERT_GROUPS):
        blk = biased[g * gsz:(g + 1) * gsz, :]
        m1 = jnp.max(blk, axis=0, keepdims=True)
        first = jnp.min(jnp.where(blk == m1, sub, gsz), axis=0, keepdims=True)
        m2 = jnp.max(jnp.where(sub == first, -jnp.inf, blk), axis=0, keepdims=True)
        blocks.append(blk)
        gscore.append(m1 + m2)
    masked = []
    for g in range(N_EXPERT_GROUPS):
        rank = jnp.zeros((1, TM), F32)
        for g2 in range(N_EXPERT_GROUPS):
            if g2 == g:
                continue
            beats = (gscore[g2] > gscore[g]) | ((gscore[g2] == gscore[g]) if g2 < g else False)
            rank = rank + jnp.where(beats, 1.0, 0.0)
        masked.append(jnp.where(rank < TOPK_GROUPS, blocks[g], -jnp.inf))
    vals = jnp.concatenate(masked, axis=0)
    eidx = lax.broadcasted_iota(jnp.int32, (N_EXPERTS, TM), 0)
    self32 = jnp.zeros((N_EXPERTS, TM), F32)
    rest = vals
    for _ in range(TOP_K):
        top = jnp.max(rest, axis=0, keepdims=True)
        first = jnp.min(jnp.where(rest == top, eidx, N_EXPERTS), axis=0, keepdims=True)
        hit = eidx == first
        self32 = jnp.where(hit, 1.0, self32)
        rest = jnp.where(hit, -jnp.inf, rest)
    sel = self32 > 0.5
    picked = jnp.where(sel, scores, 0.0)
    wdense = picked / jnp.sum(picked, axis=0, keepdims=True) * ROUTED_SCALE

    tr = lax.broadcasted_iota(jnp.int32, (TM, TM), 0)
    tc = lax.broadcasted_iota(jnp.int32, (TM, TM), 1)
    before = jnp.where(tr < tc, 1.0, 0.0).astype(BF16)
    selb = self32.astype(BF16)
    pos = _dot(selb, before)
    er = lax.broadcasted_iota(jnp.int32, (N_EXPERTS, N_EXPERTS), 0)
    ec = lax.broadcasted_iota(jnp.int32, (N_EXPERTS, N_EXPERTS), 1)
    lower = jnp.where(ec < er, 1.0, 0.0).astype(BF16)
    upper = jnp.where(er < ec, 1.0, 0.0).astype(BF16)
    ksel = _dot(lower, selb)
    cnt_col = _ceil_seg(jnp.sum(self32, axis=1, keepdims=True))
    loc_col = _dot3_left(lower, jnp.broadcast_to(cnt_col, (N_EXPERTS, LANES)))[:, 0:1]
    cnt_row = _ceil_seg(_dot_nt(jnp.ones((SUBLANES, TM), BF16), selb))
    loc_row = _dot3(cnt_row, upper)
    tab_ref[...] = jnp.concatenate([cnt_row, loc_row], axis=1).astype(jnp.int32)

    r8 = lax.broadcasted_iota(jnp.int32, (TOP_K, TM), 0)
    ld = jnp.zeros((TOP_K, TM), F32)
    wk = jnp.zeros((TOP_K, TM), F32)
    stage_row = pos + loc_col
    for k in range(TOP_K):
        one = sel & (ksel == float(k))
        ld = jnp.where(r8 == k, jnp.sum(jnp.where(one, stage_row, 0.0), axis=0, keepdims=True), ld)
        wk = jnp.where(r8 == k, jnp.sum(jnp.where(one, wdense, 0.0), axis=0, keepdims=True), wk)
    ld_ref[...] = ld.astype(jnp.int32)
    wk_ref[...] = wk


def _router(xu, mod, gpre, rwt, rb, *, n_rows, n_lat, seq, batch):
    d = xu.shape[1]
    row = lambda w: pl.BlockSpec((TM, w), lambda i: (i, 0))
    col = pl.BlockSpec((TOP_K, TM), lambda i: (0, i))
    const = lambda a: pl.BlockSpec(a.shape, lambda i: (0,) * a.ndim)
    return pl.pallas_call(
        _router_kernel,
        grid=(n_rows // TM,),
        in_specs=[row(d), pl.BlockSpec((1, 1, d), _mod_spec(3, n_lat, seq, batch, TM)),
                  pl.BlockSpec((1, 1, d), _mod_spec(4, n_lat, seq, batch, TM)),
                  const(gpre), const(rwt), const(rb)],
        out_specs=(row(d), col, col, pl.BlockSpec((SUBLANES, 2 * N_EXPERTS), lambda i: (i, 0))),
        out_shape=(jax.ShapeDtypeStruct((n_rows, d), BF16),
                   jax.ShapeDtypeStruct((TOP_K, n_rows), jnp.int32),
                   jax.ShapeDtypeStruct((TOP_K, n_rows), F32),
                   jax.ShapeDtypeStruct((n_rows // TM * SUBLANES, 2 * N_EXPERTS), jnp.int32)),
        compiler_params=_params(("parallel",)),
        name="router",
    )(xu, mod, mod, gpre, rwt, rb)


def _pow2_pieces(limit):
    bits, b = [], limit
    while b >= SEG_ALIGN:
        bits.append(b)
        b //= 2
    return bits


def _copy_pieces(n, src_ref, src0, dst_ref, dst0, sem, limit, wait, same_src=False):
    for bit in _pow2_pieces(limit):
        @pl.when((n & bit) != 0)
        def _():
            off = n & ~(2 * bit - 1)
            cp = pltpu.make_async_copy(src_ref.at[pl.ds(pl.multiple_of(src0 + (0 if same_src else off), SEG_ALIGN),
                                                          bit)],
                                       dst_ref.at[pl.ds(pl.multiple_of(dst0 + off, SEG_ALIGN), bit)], sem)
            cp.wait() if wait else cp.start()


N_PIECE_TABS = 7


def _piece_copies(tile, tabs, stage_ref, slots_ref, sem, to_slots, wait):
    nbig_ref, nsmall_ref, _, bsrc_ref, bdst_ref, ssrc_ref, sdst_ref = tabs
    for rows, n_ref, a_ref, b_ref, cap in ((BIG_PIECE, nbig_ref, bsrc_ref, bdst_ref, BIG_MAX),
                                           (SEG_ALIGN, nsmall_ref, ssrc_ref, sdst_ref, SMALL_MAX)):
        def body(p, c, rows=rows, a_ref=a_ref, b_ref=b_ref, cap=cap):
            src = stage_ref.at[pl.ds(pl.multiple_of(a_ref[tile * cap + p], SEG_ALIGN), rows)]
            dst = slots_ref.at[pl.ds(pl.multiple_of(b_ref[tile * cap + p], SEG_ALIGN), rows)]
            cp = pltpu.make_async_copy(src, dst, sem) if to_slots else pltpu.make_async_copy(dst, src, sem)
            cp.wait() if wait else cp.start()
            return c
        lax.fori_loop(0, n_ref[tile], body, 0)


def _used_blocks(tile, tabs):
    return (tabs[2][tile] + TM - 1) // TM


def _for_used_blocks(used, body):
    for b in range(TOP_K):
        body(b)
    for b in range(TOP_K, STAGE_ROWS // TM):
        @pl.when(b < used)
        def _():
            body(b)


def _stage_rows_iota():
    return lax.broadcasted_iota(jnp.int32, (TM // 2, TM), 0).astype(F32).astype(BF16)


def _pick_matrix(ld, base, vals, jrow):
    rel = (ld - base).astype(F32)
    rel = jnp.where(jnp.logical_and(rel >= 0.0, rel < TM // 2), rel, -1.0).astype(BF16)
    out = jnp.zeros((TM // 2, TM), BF16)
    for k in range(TOP_K):
        out = jnp.where(rel[k:k + 1, :] == jrow, vals[k:k + 1, :], out)
    return out


def _dispatch_kernel(*refs):
    tabs = refs[:N_PIECE_TABS]
    pstart_ref, npad_ref, hb_ref, ld_ref, xs_ref, stage, zbuf, sem, zsem = refs[N_PIECE_TABS:]
    i = pl.program_id(0)

    @pl.when(i == 0)
    def _():
        zbuf[...] = jnp.zeros_like(zbuf)
        for wait in (False, True):
            def body(e, c, wait=wait):
                _copy_pieces(npad_ref[e], zbuf, 0, xs_ref, pstart_ref[e], zsem, BM_EXPERT // 2, wait, same_src=True)
                return c
            lax.fori_loop(0, N_EXPERTS, body, 0)

    ld = ld_ref[...]
    hb = hb_ref[...]
    jrow = _stage_rows_iota()
    ones = jnp.ones((TOP_K, TM), BF16)

    cur = stage.at[i & 1]

    def block(b):
        for half in range(2):
            base = b * TM + half * (TM // 2)
            cur[base:base + TM // 2, :] = _dot(_pick_matrix(ld, base, ones, jrow), hb).astype(BF16)

    _for_used_blocks(_used_blocks(i, tabs), block)

    @pl.when(i > 0)
    def _():
        _piece_copies(i - 1, tabs, stage.at[(i - 1) & 1], xs_ref, sem, True, True)

    _piece_copies(i, tabs, cur, xs_ref, sem, True, False)

    @pl.when(i == pl.num_programs(0) - 1)
    def _():
        _piece_copies(i, tabs, cur, xs_ref, sem, True, True)


def _dispatch(tabs, pad_start, n_pad, hb, ld, n_slots):
    n, d = hb.shape
    grid_spec = pltpu.PrefetchScalarGridSpec(
        num_scalar_prefetch=N_PIECE_TABS + 2,
        grid=(n // TM,),
        in_specs=[pl.BlockSpec((TM, d), lambda i, *_: (i, 0)),
                  pl.BlockSpec((TOP_K, TM), lambda i, *_: (0, i))],
        out_specs=pl.BlockSpec(memory_space=pl.ANY),
        scratch_shapes=[pltpu.VMEM((2, STAGE_ROWS, d), BF16), pltpu.VMEM((BM_EXPERT // 2, d), BF16),
                        pltpu.SemaphoreType.DMA(()), pltpu.SemaphoreType.DMA(())],
    )
    return pl.pallas_call(
        _dispatch_kernel,
        grid_spec=grid_spec,
        out_shape=jax.ShapeDtypeStruct((n_slots, d), BF16),
        compiler_params=_params(("arbitrary",)),
        name="moe_dispatch",
    )(*tabs, pad_start, n_pad, hb, ld)


def _expert_kernel(be_ref, na_ref, nxt_ref, slot_ref, xs_hbm, wg_hbm, wu_hbm, wd_hbm, ys_ref,
                   xbuf, wg_raw, wu_raw, wd_raw, wgub, wdb, xsem, wsem, *, layer):
    i = pl.program_id(0)
    n_act = na_ref[0]

    def weight_copies(e, slot):
        return [pltpu.make_async_copy(src.at[layer, e], dst.at[slot], wsem.at[slot, j])
                for j, (src, dst) in enumerate(((wg_hbm, wg_raw), (wu_hbm, wu_raw), (wd_hbm, wd_raw)))]

    def rows_copy(blk):
        buf = lax.rem(blk, X_BUFFERS)
        return pltpu.make_async_copy(xs_hbm.at[pl.ds(pl.multiple_of(blk * BM_EXPERT, BM_EXPERT), BM_EXPERT)],
                                     xbuf.at[buf], xsem.at[buf])

    @pl.when(i < n_act)
    def _():
        e, slot = be_ref[i], slot_ref[i]

        @pl.when(i == 0)
        def _():
            for cp in weight_copies(e, slot):
                cp.start()
            for j in range(X_BUFFERS - 1):
                @pl.when(j < n_act)
                def _():
                    rows_copy(j).start()

        @pl.when(i + (X_BUFFERS - 1) < n_act)
        def _():
            rows_copy(i + (X_BUFFERS - 1)).start()

        @pl.when(jnp.logical_or(i == 0, e != be_ref[jnp.maximum(i - 1, 0)]))
        def _():
            for cp in weight_copies(e, slot):
                cp.wait()
            wgub[:, :EXPERT_HIDDEN] = wg_raw[slot].astype(BF16)
            wgub[:, EXPERT_HIDDEN:] = wu_raw[slot].astype(BF16)
            wdb[...] = wd_raw[slot].astype(BF16)

            @pl.when(nxt_ref[i] != e)
            def _():
                for cp in weight_copies(nxt_ref[i], 1 - slot):
                    cp.start()

        rows_copy(i).wait()
        gu = _dot(xbuf[lax.rem(i, X_BUFFERS)], wgub[...])
        hid = _silu(gu[:, :EXPERT_HIDDEN]) * gu[:, EXPERT_HIDDEN:]
        ys_ref[...] = _dot(hid.astype(BF16), wdb[...]).astype(ys_ref.dtype)


def _experts(block_e, n_active, next_e, w_slot, xs, wg, wu, wd, layer):
    n_slots, d = xs.shape
    nb = n_slots // BM_EXPERT
    tiles = pl.BlockSpec((BM_EXPERT, d), lambda i, be, na, *_: (jnp.minimum(i, na[0] - 1), 0))
    anywhere = pl.BlockSpec(memory_space=pl.ANY)
    grid_spec = pltpu.PrefetchScalarGridSpec(
        num_scalar_prefetch=4,
        grid=(nb,),
        in_specs=[anywhere, anywhere, anywhere, anywhere],
        out_specs=tiles,
        scratch_shapes=[pltpu.VMEM((X_BUFFERS, BM_EXPERT, d), BF16),
                        pltpu.VMEM((2, d, EXPERT_HIDDEN), F32), pltpu.VMEM((2, d, EXPERT_HIDDEN), F32),
                        pltpu.VMEM((2, EXPERT_HIDDEN, d), F32),
                        pltpu.VMEM((d, 2 * EXPERT_HIDDEN), BF16),
                        pltpu.VMEM((EXPERT_HIDDEN, d), BF16),
                        pltpu.SemaphoreType.DMA((X_BUFFERS,)), pltpu.SemaphoreType.DMA((2, 3))],
    )
    return pl.pallas_call(
        functools.partial(_expert_kernel, layer=layer),
        grid_spec=grid_spec,
        out_shape=jax.ShapeDtypeStruct((n_slots, d), BF16),
        compiler_params=_params(("arbitrary",)),
        name="moe_experts",
    )(block_e, n_active, next_e, w_slot, xs, wg, wu, wd)


def _combine_kernel(*refs):
    tabs = refs[:N_PIECE_TABS]
    (ys_ref, ld_ref, wk_ref, hb_ref, x_ref, gate_ref, gpost_ref, sg_ref, su_ref, sd_ref, o_ref,
     stage, acc_ref, sem) = refs[N_PIECE_TABS:]
    i = pl.program_id(0)

    @pl.when(i == 0)
    def _():
        stage[...] = jnp.zeros_like(stage)
        _piece_copies(0, tabs, stage.at[0], ys_ref, sem, False, False)

    cur = stage.at[i & 1]
    _piece_copies(i, tabs, cur, ys_ref, sem, False, True)

    @pl.when(i + 1 < pl.num_programs(0))
    def _():
        _piece_copies(i + 1, tabs, stage.at[(i + 1) & 1], ys_ref, sem, False, False)

    hb = hb_ref[...]
    acc_ref[...] = _dot((_silu(_dot(hb, sg_ref[...])) * _dot(hb, su_ref[...])).astype(BF16), sd_ref[...])

    ld = ld_ref[...]
    wkb = wk_ref[...].astype(BF16)
    jrow = _stage_rows_iota()

    def block(b):
        for half in range(2):
            base = b * TM + half * (TM // 2)
            weights = _pick_matrix(ld, base, wkb, jrow)
            acc_ref[...] += lax.dot_general(weights, cur[base:base + TM // 2, :], (((0,), (0,)), ((), ())),
                                            preferred_element_type=F32)

    _for_used_blocks(_used_blocks(i, tabs), block)
    o_ref[...] = x_ref[...] + gate_ref[0] * _rms(acc_ref[...], gpost_ref[...])


def _combine(tabs, ys, ld, wk, hb, xu, mod, gpost, sg, su, sd, *, n_rows, n_lat, seq, batch):
    d = xu.shape[1]
    row = lambda w: pl.BlockSpec((TM, w), lambda i, *_: (i, 0))
    col = pl.BlockSpec((TOP_K, TM), lambda i, *_: (0, i))
    const = lambda a: pl.BlockSpec(a.shape, lambda i, *_: (0,) * a.ndim)
    mod_map = _mod_spec(5, n_lat, seq, batch, TM)
    grid_spec = pltpu.PrefetchScalarGridSpec(
        num_scalar_prefetch=N_PIECE_TABS,
        grid=(n_rows // TM,),
        in_specs=[pl.BlockSpec(memory_space=pl.ANY), col, col, row(d), row(d),
                  pl.BlockSpec((1, 1, d), lambda i, *_: mod_map(i)),
                  const(gpost), const(sg), const(su), const(sd)],
        out_specs=row(d),
        scratch_shapes=[pltpu.VMEM((2, STAGE_ROWS, d), BF16), pltpu.VMEM((TM, d), F32), pltpu.SemaphoreType.DMA(())],
    )
    return pl.pallas_call(
        _combine_kernel,
        grid_spec=grid_spec,
        out_shape=jax.ShapeDtypeStruct((n_rows, d), F32),
        compiler_params=_params(("arbitrary",)),
        name="moe_combine",
    )(*tabs, ys, ld, wk, hb, xu, mod, gpost, sg, su, sd)


def _deinterleave(w):
    cols = w.shape[-1]
    perm = jnp.concatenate([jnp.arange(0, HEAD_DIM, 2), jnp.arange(1, HEAD_DIM, 2)])
    idx = (jnp.arange(cols // HEAD_DIM)[:, None] * HEAD_DIM + perm[None, :]).reshape(-1)
    return w[..., idx]


def _pad_in_proj(w_in):
    d = w_in.shape[0]
    o = 0
    parts = {}
    for name, width in (("qa", 256), ("ka", 128), ("va", 128), ("z", 256), ("xs", 256), ("bm", 128), ("cm", 128),
                        ("dtf", 4), ("dtb", 4), ("lx", 256), ("lg", 256), ("qd", 256), ("kd", 128), ("vd", 128)):
        parts[name] = w_in[:, o:o + width]
        o += width
    dt = jnp.concatenate([parts["dtf"], parts["dtb"], jnp.zeros((d, LANES - 8), w_in.dtype)], axis=1)
    cols = [_deinterleave(parts["qa"]), _deinterleave(parts["ka"]), parts["va"],
            _deinterleave(parts["qd"]), _deinterleave(parts["kd"]), parts["vd"],
            parts["z"], parts["xs"], parts["bm"], parts["cm"], dt, parts["lx"], parts["lg"]]
    return jnp.concatenate(cols, axis=1).astype(BF16)


def _rope_tables(seq):
    t = jnp.arange(seq)
    rowp = (t // GRID_W).astype(F32)
    colp = (t % GRID_W).astype(F32)
    axis_dim = HEAD_DIM // 2
    inv_freq = ROPE_THETA ** (-jnp.arange(0, axis_dim, 2, dtype=F32) / axis_dim)
    ang = jnp.concatenate([rowp[:, None] * inv_freq, colp[:, None] * inv_freq], axis=-1)
    cos, sin = jnp.cos(ang), jnp.sin(ang)
    cos_h = jnp.concatenate([cos, cos], axis=-1)
    sin_h = jnp.concatenate([-sin, sin], axis=-1)
    return jnp.tile(cos_h, (1, 4)), jnp.tile(sin_h, (1, 4))


def _block_diag(w):
    nb, bd, _ = w.shape
    eye = jnp.eye(nb, dtype=w.dtype)
    return (eye[:, None, :, None] * w[:, :, None, :]).reshape(nb * bd, nb * bd)


def _piece_table(counts, cap, stage0, slot0, rows, ids):
    ends = jnp.cumsum(counts, axis=1)
    q = jnp.arange(cap, dtype=jnp.int32)
    owner = jnp.sum((ends[:, None, :] <= q[None, :, None]).astype(jnp.int32), axis=-1)
    mine = owner[:, :, None] == ids
    pick = lambda v: jnp.sum(jnp.where(mine, v[:, None, :], 0), axis=-1)
    step = rows * (q[None, :] - pick(ends - counts))
    return (pick(stage0) + step).reshape(-1), (pick(slot0) + step).reshape(-1)


def _lane_row(fwd, bwd):
    return jnp.concatenate([fwd, bwd, jnp.zeros((LANES - 8,), F32)]).reshape(1, LANES)


def kernel(x, c, ctx, c_ctx, w_ada, b_ada, g_mix_pre, g_mix_post, g_ffn_pre, g_ffn_post, w_in, w_out, a_sink,
           ssd_conv_w, ssd_conv_b, ssd_dt_bias, ssd_a_log, ssd_d, ssd_norm, lru_conv_w, lru_conv_b, lru_w_a,
           lru_b_a, lru_w_i, lru_b_i, lru_lambda, d_q_norm, d_k_norm, router_w, router_bias, exp_w_gate,
           exp_w_up, exp_w_down, sh_w_gate, sh_w_up, sh_w_down):
    batch, seq, d = x.shape
    ctx_len = ctx.shape[1]
    depth = w_ada.shape[0]
    n_lat = batch * seq
    n_ctx = batch * ctx_len
    n_all = n_lat + n_ctx
    assert seq % TM == 0 and n_ctx % TM == 0 and ctx_len >= LRU_CONV
    assert seq % TQ_WINDOW == 0 and seq % (SCAN_CHUNKS * CHUNK) == 0 and ctx_len % (SCAN_CHUNKS * CHUNK) == 0
    assert ctx_len <= KV_CHUNK and seq % KV_CHUNK == 0 and seq % TQ_GLOBAL == 0 and batch + 1 <= SUBLANES

    xu = jnp.concatenate([x.reshape(n_lat, d), ctx.reshape(n_ctx, d)], axis=0)
    cin = jnp.concatenate([c, c_ctx[None, :], jnp.zeros((SUBLANES - batch - 1, d), F32)], axis=0)
    mod_all = _adaln(cin, w_ada, b_ada)
    cos_t, sin_t = _rope_tables(seq)
    hm = jnp.kron(jnp.eye(4, dtype=F32), jnp.full((HEAD_DIM, HEAD_DIM), 1.0 / HEAD_DIM, F32)).astype(BF16)

    for l in range(depth):
        with_ctx = l < depth - 1
        mod = mod_all[l].reshape(SUBLANES * 6, 1, d)
        gq = jnp.tile(_deinterleave(d_q_norm[l]), 4).reshape(1, 256)
        gk = jnp.tile(_deinterleave(d_k_norm[l]), 2).reshape(1, LANES)
        qa, kat, va, qd, kdt, vd, z, xbc, dt, lu, lg = _inproj(
            xu, mod, g_mix_pre[l].reshape(1, d), _pad_in_proj(w_in[l]), cos_t, sin_t, gq, gk, hm,
            ssd_conv_w[l], ssd_conv_b[l].reshape(1, -1), lru_conv_w[l], lru_conv_b[l].reshape(1, -1),
            n_lat=n_lat, seq=seq, ctx_len=ctx_len, batch=batch)
        yf, yb = _ssd(xbc, dt, _lane_row(ssd_dt_bias[l, 0], ssd_dt_bias[l, 1]),
                      _lane_row(ssd_a_log[l, 0], ssd_a_log[l, 1]), batch=batch, seq=seq, ctx_len=ctx_len)
        wg = jnp.stack([jnp.concatenate([_block_diag(lru_w_a[l, dd]), _block_diag(lru_w_i[l, dd])], axis=1)
                        for dd in range(2)]).astype(BF16)
        bg = jnp.concatenate([lru_b_a[l], lru_b_i[l]], axis=1).reshape(2, 1, 2 * LRU_WIDTH)
        hf, hb = _lru(lu, wg, bg, lru_lambda[l].reshape(2, 1, LRU_WIDTH), batch=batch, seq=seq, ctx_len=ctx_len)

        oa = _window_attn(qa, kat, va, a_sink[l], batch=batch, seq=seq, ctx_len=ctx_len)
        od = _dense_attn(qd, kdt, vd, None, q_row0=0, q_len=seq, tq=TQ_GLOBAL,
                         segs=[(n_lat, ctx_len), (0, seq)], batch=batch)
        if with_ctx:
            oa_c = _dense_attn(qa, kat, va, a_sink[l], q_row0=n_lat, q_len=ctx_len, tq=ctx_len,
                               segs=[(n_lat, ctx_len)], batch=batch)
            od_c = _dense_attn(qd, kdt, vd, None, q_row0=n_lat, q_len=ctx_len, tq=ctx_len,
                               segs=[(n_lat, ctx_len)], batch=batch)
            oa = jnp.concatenate([oa, oa_c], axis=0)
            od = jnp.concatenate([od, od_c], axis=0)
        n_rows = n_all if with_ctx else n_lat

        dsk = jnp.repeat(ssd_d[l], HEAD_DIM).reshape(1, 256)
        xu_mid = _outproj(xu, mod, g_mix_post[l].reshape(1, d), oa, od, yf, yb, xbc, z, dsk,
                          ssd_norm[l].reshape(1, 256), hf, hb, lg, w_out[l].astype(BF16),
                          n_rows=n_rows, n_lat=n_lat, seq=seq, batch=batch)

        hb_ffn, ld, wk, tab = _router(xu_mid, mod, g_ffn_pre[l].reshape(1, d), router_w[l].T.astype(BF16),
                                      router_bias[l].reshape(N_EXPERTS, 1), n_rows=n_rows, n_lat=n_lat,
                                      seq=seq, batch=batch)
        n_tiles = n_rows // TM
        tab = tab.reshape(n_tiles, SUBLANES, 2 * N_EXPERTS)[:, 0, :]
        seg_cnt, seg_loc = tab[:, :N_EXPERTS], tab[:, N_EXPERTS:]
        counts = jnp.sum(seg_cnt, axis=0)
        padded = (counts + BM_EXPERT - 1) // BM_EXPERT * BM_EXPERT
        padded_end = jnp.cumsum(padded)
        offs = padded_end - padded
        seg_off = offs[None, :] + jnp.cumsum(seg_cnt, axis=0) - seg_cnt
        n_blocks = (n_rows * TOP_K + n_tiles * N_EXPERTS * SEG_ALIGN) // BM_EXPERT + N_EXPERTS
        n_active = (padded_end[-1] // BM_EXPERT).astype(jnp.int32).reshape(1)
        block_start = jnp.arange(n_blocks, dtype=jnp.int32) * BM_EXPERT
        block_e = jnp.minimum(jnp.sum((padded_end[None, :] <= block_start[:, None]).astype(jnp.int32), axis=1),
                              N_EXPERTS - 1)
        ids = jnp.arange(N_EXPERTS, dtype=jnp.int32)
        n_big = seg_cnt // BIG_PIECE
        n_small = (seg_cnt % BIG_PIECE) // SEG_ALIGN
        tabs = (jnp.sum(n_big, axis=1), jnp.sum(n_small, axis=1), seg_loc[:, -1] + seg_cnt[:, -1],
                *_piece_table(n_big, BIG_MAX, seg_loc, seg_off, BIG_PIECE, ids),
                *_piece_table(n_small, SMALL_MAX, seg_loc + n_big * BIG_PIECE, seg_off + n_big * BIG_PIECE,
                              SEG_ALIGN, ids))
        xs = _dispatch(tabs, offs + counts, padded - counts, hb_ffn, ld, n_blocks * BM_EXPERT)
        has_rows = padded > 0
        later = jnp.logical_and(ids[None, :] > ids[:, None], has_rows[None, :])
        nxt_of = jnp.min(jnp.where(later, ids[None, :], N_EXPERTS), axis=1)
        nxt_of = jnp.where(nxt_of == N_EXPERTS, ids, nxt_of)
        slot_of = (jnp.cumsum(has_rows.astype(jnp.int32)) - 1) & 1
        own = block_e[:, None] == ids[None, :]
        next_e = jnp.sum(jnp.where(own, nxt_of[None, :], 0), axis=1)
        w_slot = jnp.sum(jnp.where(own, slot_of[None, :], 0), axis=1)
        ys = _experts(block_e, n_active, next_e, w_slot, xs, exp_w_gate, exp_w_up, exp_w_down, l)
        xu = _combine(tabs, ys, ld, wk, hb_ffn, xu_mid, mod, g_ffn_post[l].reshape(1, d), sh_w_gate[l].astype(BF16),
                      sh_w_up[l].astype(BF16), sh_w_down[l].astype(BF16),
                      n_rows=n_rows, n_lat=n_lat, seq=seq, batch=batch)
    return xu[:n_lat].reshape(batch, seq, d)
```

```python
import functools
import math

import jax
import jax.numpy as jnp
from jax import lax
from jax.experimental import pallas as pl
from jax.experimental.pallas import tpu as pltpu

F32 = jnp.float32
BF16 = jnp.bfloat16

HEAD_DIM = 64
GRID_W = 64
ROPE_THETA = 10000.0
NORM_EPS = 1e-6
NEG_INF = -1e30
A_HEADS, A_KV_HEADS, WINDOW = 4, 2, 128
SSD_HEADS, SSD_GROUPS, SSD_STATE, SSD_CONV = 4, 2, 64, 4
LRU_WIDTH, LRU_BLOCKS, LRU_CONV, LRU_C = 256, 4, 4, 8.0
D_HEADS, D_KV_HEADS = 4, 2
N_EXPERTS, N_EXPERT_GROUPS, TOPK_GROUPS, TOP_K = 64, 8, 4, 8
EXPERT_HIDDEN, SHARED_HIDDEN = 256, 256
ROUTED_SCALE = 2.5

LANES = 128
SUBLANES = 8

TM = 512
CHUNK = 128
SCAN_CHUNKS = 2
TQ_GLOBAL = 256
TQ_WINDOW = 512
KV_CHUNK = 256
KV_UNROLL = 16
BM_EXPERT = 512
X_BUFFERS = 3
SEG_ALIGN = 16
STAGE_ROWS = TM * TOP_K + N_EXPERTS * SEG_ALIGN
BIG_PIECE = 64
BIG_MAX = STAGE_ROWS // BIG_PIECE
SMALL_MAX = N_EXPERTS * (BIG_PIECE // SEG_ALIGN - 1)
VMEM_LIMIT = 48 * 1024 * 1024

C_QA, C_KA, C_VA = 0, 256, 384
C_QD, C_KD, C_VD = 512, 768, 896
C_Z, C_XBC, C_DT = 1024, 1280, 1792
C_LX, C_LG = 1920, 2176
NP_IN = 2432


def _dot(a, b):
    return jnp.dot(a, b, preferred_element_type=F32)


def _dot_nt(a, b):
    return lax.dot_general(a, b, (((1,), (1,)), ((), ())), preferred_element_type=F32)


def _dot3(a, b):
    a1 = a.astype(BF16)
    r1 = a - a1.astype(F32)
    a2 = r1.astype(BF16)
    a3 = (r1 - a2.astype(F32)).astype(BF16)
    return _dot(a1, b) + _dot(a2, b) + _dot(a3, b)


def _dot3_left(a, b):
    b1 = b.astype(BF16)
    r1 = b - b1.astype(F32)
    b2 = r1.astype(BF16)
    b3 = (r1 - b2.astype(F32)).astype(BF16)
    return _dot(a, b1) + _dot(a, b2) + _dot(a, b3)


def _silu(x):
    return x * jax.nn.sigmoid(x)


def _softplus(x):
    return jnp.maximum(x, 0.0) + jnp.log1p(jnp.exp(-jnp.abs(x)))


def _rms(x, gain):
    return x * lax.rsqrt(jnp.mean(x * x, axis=-1, keepdims=True) + NORM_EPS) * gain


def _params(sem=None):
    return pltpu.CompilerParams(dimension_semantics=sem, vmem_limit_bytes=VMEM_LIMIT)


def _adaln_kernel(c_ref, w_ref, b_ref, o_ref):
    s = _silu(c_ref[...])
    o_ref[0] = _dot(s.astype(BF16), w_ref[0].astype(BF16)) + b_ref[0]


def _adaln(cin, w_ada, b_ada):
    depth, d, n6 = w_ada.shape
    tn = 1024
    return pl.pallas_call(
        _adaln_kernel,
        grid=(depth, n6 // tn),
        in_specs=[pl.BlockSpec((SUBLANES, d), lambda l, j: (0, 0)),
                  pl.BlockSpec((1, d, tn), lambda l, j: (l, 0, j)),
                  pl.BlockSpec((1, 1, tn), lambda l, j: (l, 0, j))],
        out_specs=pl.BlockSpec((1, SUBLANES, tn), lambda l, j: (l, 0, j)),
        out_shape=jax.ShapeDtypeStruct((depth, SUBLANES, n6), F32),
        compiler_params=_params(("parallel", "parallel")),
        name="adaln",
    )(cin, w_ada, b_ada.reshape(depth, 1, n6))


def _swap_halves(t):
    w = t.shape[1]
    lane = lax.broadcasted_iota(jnp.int32, (1, w), 1)
    first = (lane & 32) == 0
    return jnp.where(first, pltpu.roll(t, w - 32, axis=1), pltpu.roll(t, 32, axis=1))


def _inproj_kernel(x_ref, xp_ref, xn_ref, shift_ref, scale_ref, gpre_ref, w_ref, cos_ref, sin_ref, gq_ref, gk_ref,
                   hm_ref, ws_ref, bs_ref, wl_ref, bl_ref,
                   qa_ref, kat_ref, va_ref, qd_ref, kdt_ref, vd_ref, z_ref, xbc_ref, dt_ref, lu_ref, lg_ref,
                   *, n_lat, seq, ctx_len):
    i = pl.program_id(0)
    is_lat = i * TM < n_lat

    def pre(x):
        return _rms(x, gpre_ref[...]) * (1.0 + scale_ref[0]) + shift_ref[0]

    h = pre(x_ref[...])
    hb = h.astype(BF16)
    hb_ext = jnp.concatenate([pre(xp_ref[...]), h, pre(xn_ref[...])], axis=0).astype(BF16)

    def sec(a, b):
        return _dot(hb, w_ref[:, a:b])

    row = lax.broadcasted_iota(jnp.int32, (TM, 1), 0)
    pos = jnp.where(is_lat, lax.rem(i * TM + row, seq), lax.rem(i * TM - n_lat + row, ctx_len))
    slen = jnp.where(is_lat, seq, ctx_len)

    def conv(a, b, w, bias):
        ext = _dot(hb_ext, w_ref[:, a:b])
        prev, x, nxt = ext[0:SUBLANES], ext[SUBLANES:SUBLANES + TM], ext[SUBLANES + TM:]
        xm1 = jnp.where(row == 0, prev[7:8, :], pltpu.roll(x, 1, axis=0))
        xm2 = jnp.where(row == 0, prev[6:7, :], jnp.where(row == 1, prev[7:8, :], pltpu.roll(x, 2, axis=0)))
        xp1 = jnp.where(row == TM - 1, nxt[0:1, :], pltpu.roll(x, TM - 1, axis=0))
        xm1 = jnp.where(pos >= 1, xm1, 0.0)
        xm2 = jnp.where(pos >= 2, xm2, 0.0)
        xp1 = jnp.where(pos <= slen - 2, xp1, 0.0)
        return w[0:1, :] * xm2 + w[1:2, :] * xm1 + w[2:3, :] * x + w[3:4, :] * xp1 + bias

    cos = jnp.where(is_lat, cos_ref[...], 1.0)
    sin = jnp.where(is_lat, sin_ref[...], 0.0)

    def rope(t):
        w = t.shape[1]
        return t * cos[:, :w] + _swap_halves(t) * sin[:, :w]

    def head_norm(t, gain):
        w = t.shape[1]
        ms = _dot3(t * t, hm_ref[:w, :w])
        return t * lax.rsqrt(ms + NORM_EPS) * gain

    scale = HEAD_DIM ** -0.5
    qa_ref[...] = (rope(sec(C_QA, C_KA)) * scale).astype(BF16)
    kat_ref[...] = rope(sec(C_KA, C_VA)).T.astype(BF16)
    va_ref[...] = sec(C_VA, C_QD).astype(BF16)
    qd_ref[...] = (rope(head_norm(sec(C_QD, C_KD), gq_ref[...])) * scale).astype(BF16)
    kdt_ref[...] = rope(head_norm(sec(C_KD, C_VD), gk_ref[...])).T.astype(BF16)
    vd_ref[...] = sec(C_VD, C_Z).astype(BF16)
    z_ref[...] = sec(C_Z, C_XBC)
    xbc_ref[...] = _silu(conv(C_XBC, C_DT, ws_ref[...], bs_ref[...]))
    dt_ref[...] = sec(C_DT, C_LX)
    lu_ref[...] = conv(C_LX, C_LG, wl_ref[...], bl_ref[...])
    lg_ref[...] = sec(C_LG, NP_IN)


def _mod_spec(chunk, n_lat, seq, batch, tile):
    def imap(i):
        row0 = i * tile
        seg = jnp.where(row0 < n_lat, row0 // seq, batch)
        return (seg * 6 + chunk, 0, 0)
    return imap


def _inproj(xu, mod, gpre, w_pad, cos_t, sin_t, gq, gk, hm, ws, bs, wl, bl, *, n_lat, seq, ctx_len, batch):
    n, d = xu.shape
    nt = n // TM
    spt = seq // TM
    r8 = TM // SUBLANES
    n8 = n // SUBLANES
    row = lambda w: pl.BlockSpec((TM, w), lambda i: (i, 0))
    colT = pl.BlockSpec((LANES, TM), lambda i: (0, i))
    const = lambda a: pl.BlockSpec(a.shape, lambda i: (0,) * a.ndim)
    out_shapes = (
        jax.ShapeDtypeStruct((n, 256), BF16), jax.ShapeDtypeStruct((LANES, n), BF16),
        jax.ShapeDtypeStruct((n, LANES), BF16),
        jax.ShapeDtypeStruct((n, 256), BF16), jax.ShapeDtypeStruct((LANES, n), BF16),
        jax.ShapeDtypeStruct((n, LANES), BF16),
        jax.ShapeDtypeStruct((n, 256), F32), jax.ShapeDtypeStruct((n, 512), F32),
        jax.ShapeDtypeStruct((n, LANES), F32), jax.ShapeDtypeStruct((n, 256), F32),
        jax.ShapeDtypeStruct((n, 256), F32))
    return pl.pallas_call(
        functools.partial(_inproj_kernel, n_lat=n_lat, seq=seq, ctx_len=ctx_len),
        grid=(nt,),
        in_specs=[row(d),
                  pl.BlockSpec((SUBLANES, d), lambda i: (jnp.maximum(i * r8 - 1, 0), 0)),
                  pl.BlockSpec((SUBLANES, d), lambda i: (jnp.minimum((i + 1) * r8, n8 - 1), 0)),
                  pl.BlockSpec((1, 1, d), _mod_spec(0, n_lat, seq, batch, TM)),
                  pl.BlockSpec((1, 1, d), _mod_spec(1, n_lat, seq, batch, TM)),
                  const(gpre), const(w_pad),
                  pl.BlockSpec((TM, 256), lambda i: (i % spt, 0)),
                  pl.BlockSpec((TM, 256), lambda i: (i % spt, 0)),
                  const(gq), const(gk), const(hm), const(ws), const(bs), const(wl), const(bl)],
        out_specs=(row(256), colT, row(LANES), row(256), colT, row(LANES),
                   row(256), row(512), row(LANES), row(256), row(256)),
        out_shape=out_shapes,
        compiler_params=_params(("parallel",)),
        name="inproj",
    )(xu, xu, xu, mod, mod, gpre, w_pad, cos_t, sin_t, gq, gk, hm, ws, bs, wl, bl)


def _chunk_maps(batch, seq, ctx_len):
    ncx = ctx_len // (SCAN_CHUNKS * CHUNK)
    nl = seq // (SCAN_CHUNKS * CHUNK)
    lat_blocks = batch * nl

    def block(b, c):
        return jnp.where(c < ncx, lat_blocks + b * ncx + c, b * nl + (c - ncx))

    def fwd(b, k):
        return (block(b, k), 0)

    def bwd(b, k):
        c = jnp.where(k < ncx, ncx - 1 - k, ncx + (nl - 1 - (k - ncx)))
        return (block(b, c), 0)

    return fwd, bwd, ncx + nl


def _ssd_kernel(xf_ref, dtf_ref, xb_ref, dtb_ref, dtbias_ref, alog_ref, yf_ref, yb_ref, state_ref):
    k = pl.program_id(1)

    @pl.when(k == 0)
    def _():
        state_ref[...] = jnp.zeros_like(state_ref)

    ri = lax.broadcasted_iota(jnp.int32, (CHUNK, CHUNK), 0)
    ci = lax.broadcasted_iota(jnp.int32, (CHUNK, CHUNK), 1)
    lane_lo = ci < HEAD_DIM
    aneg = -jnp.exp(alog_ref[...])
    dtbias = dtbias_ref[...]

    order = [(d, s if d == 0 else SCAN_CHUNKS - 1 - s) for s in range(SCAN_CHUNKS) for d in range(2)]
    for d, sub in order:
        x_ref, dt_ref, y_ref = ((xf_ref, dtf_ref, yf_ref), (xb_ref, dtb_ref, yb_ref))[d]
        rws = slice(sub * CHUNK, (sub + 1) * CHUNK)
        causal = (ri >= ci) if d == 0 else (ci >= ri)
        tmat = jnp.where(causal, 1.0, 0.0).astype(BF16)
        xs = x_ref[rws, 0:256]
        bm = x_ref[rws, 256:384]
        cm = x_ref[rws, 384:512]
        dtp = _softplus(dt_ref[rws, :] + dtbias)
        acum = _dot3_left(tmat, dtp * aneg)
        acum_t = acum.T
        bt = bm.T.astype(BF16)
        cmb = cm.astype(BF16)
        bmb = bm.astype(BF16)
        tot_row = CHUNK - 1 if d == 0 else 0
        for p in range(2):
            cmask = jnp.where(lane_lo if p == 0 else jnp.logical_not(lane_lo), cmb, jnp.zeros_like(cmb))
            cb = _dot_nt(cmask, bmb)
            cols, dts, ys = [], [], []
            x_pair = xs[:, p * LANES:(p + 1) * LANES]
            for j in range(2):
                col = 4 * d + 2 * p + j
                colb = jnp.broadcast_to(acum[:, col:col + 1], (CHUNK, CHUNK))
                rowb = jnp.broadcast_to(acum_t[col:col + 1, :], (CHUNK, CHUNK))
                cols.append(colb)
                dts.append(jnp.broadcast_to(dtp[:, col:col + 1], (CHUNK, CHUNK)))
            col_pair = jnp.where(lane_lo, cols[0], cols[1])
            dt_pair = jnp.where(lane_lo, dts[0], dts[1])
            xdt = x_pair * dt_pair
            xdt_b = xdt.astype(BF16)
            for j in range(2):
                col = 4 * d + 2 * p + j
                rowb = jnp.broadcast_to(acum_t[col:col + 1, :], (CHUNK, CHUNK))
                decay = jnp.exp(jnp.where(causal, cols[j] - rowb, NEG_INF))
                ys.append(_dot((cb * decay).astype(BF16), xdt_b))
            y_intra = jnp.where(lane_lo, ys[0], ys[1])
            s_old = state_ref[d, p]
            y_inter = _dot(cmask, s_old.astype(BF16)) * jnp.exp(col_pair)
            y_ref[rws, p * LANES:(p + 1) * LANES] = y_intra + y_inter
            tot_pair = col_pair[tot_row:tot_row + 1, :]
            to_end = jnp.exp(tot_pair - col_pair)
            state_ref[d, p] = s_old * jnp.exp(tot_pair) + _dot(bt, (xdt * to_end).astype(BF16))


def _ssd(xbc, dt, dtbias_row, alog_row, *, batch, seq, ctx_len):
    n = xbc.shape[0]
    fwd, bwd, steps = _chunk_maps(batch, seq, ctx_len)
    rows = SCAN_CHUNKS * CHUNK
    const = lambda a: pl.BlockSpec(a.shape, lambda b, k: (0,) * a.ndim)
    return pl.pallas_call(
        _ssd_kernel,
        grid=(batch, steps),
        in_specs=[pl.BlockSpec((rows, 512), fwd), pl.BlockSpec((rows, LANES), fwd),
                  pl.BlockSpec((rows, 512), bwd), pl.BlockSpec((rows, LANES), bwd),
                  const(dtbias_row), const(alog_row)],
        out_specs=(pl.BlockSpec((rows, 256), fwd), pl.BlockSpec((rows, 256), bwd)),
        out_shape=(jax.ShapeDtypeStruct((n, 256), F32), jax.ShapeDtypeStruct((n, 256), F32)),
        scratch_shapes=[pltpu.VMEM((2, 2, CHUNK, LANES), F32)],
        compiler_params=_params(("parallel", "arbitrary")),
        name="ssd_scan",
    )(xbc, dt, xbc, dt, dtbias_row, alog_row)


def _linear_scan(a, b, reverse):
    n = a.shape[0]
    row = lax.broadcasted_iota(jnp.int32, (n, 1), 0)
    s = 1
    while s < n:
        if s < SUBLANES:
            if reverse:
                ok = row < n - s
                a_sh = jnp.where(ok, pltpu.roll(a, n - s, axis=0), 1.0)
                b_sh = jnp.where(ok, pltpu.roll(b, n - s, axis=0), 0.0)
            else:
                ok = row >= s
                a_sh = jnp.where(ok, pltpu.roll(a, s, axis=0), 1.0)
                b_sh = jnp.where(ok, pltpu.roll(b, s, axis=0), 0.0)
            b = b + a * b_sh
            a = a * a_sh
        elif reverse:
            b = jnp.concatenate([b[:n - s] + a[:n - s] * b[s:], b[n - s:]], axis=0)
            a = jnp.concatenate([a[:n - s] * a[s:], a[n - s:]], axis=0)
        else:
            b = jnp.concatenate([b[:s], b[s:] + a[s:] * b[:n - s]], axis=0)
            a = jnp.concatenate([a[:s], a[s:] * a[:n - s]], axis=0)
        s *= 2
    return a, b


def _lru_kernel(uf_ref, ub_ref, wg_ref, bg_ref, lam_ref, hf_ref, hb_ref, carry_ref):
    k = pl.program_id(1)

    @pl.when(k == 0)
    def _():
        carry_ref[...] = jnp.zeros_like(carry_ref)

    for d, (u_ref, h_ref) in enumerate(((uf_ref, hf_ref), (ub_ref, hb_ref))):
        u = u_ref[...]
        gates = _dot(u.astype(BF16), wg_ref[d]) + bg_ref[d]
        r = jax.nn.sigmoid(gates[:, :LRU_WIDTH])
        ig = jax.nn.sigmoid(gates[:, LRU_WIDTH:])
        log_a = -LRU_C * r * _softplus(-lam_ref[d])
        a = jnp.exp(log_a)
        inp = jnp.sqrt(-jnp.tanh(log_a) * (1.0 + a * a)) * (ig * u)
        a_cum, b_cum = _linear_scan(a, inp, reverse=(d == 1))
        h = b_cum + a_cum * carry_ref[d, 0:1, :]
        h_ref[...] = h
        last = 0 if d == 1 else u.shape[0] - 1
        carry_ref[d, 0:1, :] = h[last:last + 1, :]


def _lru(u, wg, bg, lam, *, batch, seq, ctx_len):
    n = u.shape[0]
    fwd, bwd, steps = _chunk_maps(batch, seq, ctx_len)
    rows = SCAN_CHUNKS * CHUNK
    const = lambda a: pl.BlockSpec(a.shape, lambda b, k: (0,) * a.ndim)
    return pl.pallas_call(
        _lru_kernel,
        grid=(batch, steps),
        in_specs=[pl.BlockSpec((rows, LRU_WIDTH), fwd), pl.BlockSpec((rows, LRU_WIDTH), bwd),
                  const(wg), const(bg), const(lam)],
        out_specs=(pl.BlockSpec((rows, LRU_WIDTH), fwd), pl.BlockSpec((rows, LRU_WIDTH), bwd)),
        out_shape=(jax.ShapeDtypeStruct((n, LRU_WIDTH), F32), jax.ShapeDtypeStruct((n, LRU_WIDTH), F32)),
        scratch_shapes=[pltpu.VMEM((2, SUBLANES, LRU_WIDTH), F32)],
        compiler_params=_params(("parallel", "arbitrary")),
        name="lru_scan",
    )(u, u, wg, bg, lam)


def _stack_heads(q, g):
    qf = q.astype(F32)
    lo = g * LANES
    return jnp.concatenate([qf[:, lo:lo + HEAD_DIM], qf[:, lo + HEAD_DIM:lo + LANES]], axis=0).astype(BF16)


def _value_lanes(g):
    lane = lax.broadcasted_iota(jnp.int32, (1, LANES), 1)
    return (lane < HEAD_DIM) if g == 0 else (lane >= HEAD_DIM)


def _aug_values(v, g):
    return jnp.where(_value_lanes(g), v, jnp.ones_like(v))


def _flash_init(rows, g, sink_pair):
    if sink_pair is None:
        return jnp.full((rows, 1), NEG_INF, F32), jnp.zeros((rows, LANES), F32)
    half = lax.broadcasted_iota(jnp.int32, (rows, 1), 0) < rows // 2
    m = jnp.where(half, sink_pair[0], sink_pair[1]).astype(F32)
    acc = jnp.broadcast_to(jnp.where(_value_lanes(g), 0.0, 1.0), (rows, LANES))
    return m, acc


def _flash_update(state, q2, kt, v_aug, mask=None):
    m, acc = state
    s = _dot(q2, kt)
    if mask is not None:
        s = jnp.where(mask, s, NEG_INF)
    m_new = jnp.maximum(m, jnp.max(s, axis=-1, keepdims=True))
    p = jnp.exp(s - m_new).astype(BF16)
    acc = jnp.exp(m - m_new) * acc + _dot(p, v_aug)
    return m_new, acc


def _flash_finish(states, tq):
    pieces = []
    for g, (_, acc) in enumerate(states):
        den = (1 - g) * HEAD_DIM
        o = acc[:, g * HEAD_DIM:(g + 1) * HEAD_DIM] / acc[:, den:den + 1]
        pieces += [o[:tq], o[tq:]]
    return jnp.concatenate(pieces, axis=1)


def _group_rows(g):
    return slice(g * HEAD_DIM, (g + 1) * HEAD_DIM)


def _dense_attn_kernel(*refs, tq, seg_lens, has_sink):
    refs = list(refs)
    sink_ref = refs.pop(0) if has_sink else None
    q_ref = refs.pop(0)
    o_ref = refs.pop()
    segs = [(refs[2 * i], refs[2 * i + 1], n) for i, n in enumerate(seg_lens)]
    q = q_ref[...]
    q2 = [_stack_heads(q, g) for g in range(2)]
    states = tuple(_flash_init(2 * tq, g, (sink_ref[2 * g], sink_ref[2 * g + 1]) if has_sink else None)
                   for g in range(2))
    for kt_ref, v_ref, n_keys in segs:
        if n_keys <= KV_CHUNK:
            v = v_ref[...]
            states = tuple(_flash_update(states[g], q2[g], kt_ref[_group_rows(g), :], _aug_values(v, g))
                           for g in range(2))
        else:
            def body(c, sts, kt_ref=kt_ref, v_ref=v_ref):
                off = pl.multiple_of(c * KV_CHUNK, KV_CHUNK)
                v = v_ref[pl.ds(off, KV_CHUNK), :]
                return tuple(_flash_update(sts[g], q2[g], kt_ref[_group_rows(g), pl.ds(off, KV_CHUNK)],
                                           _aug_values(v, g)) for g in range(2))
            states = lax.fori_loop(0, n_keys // KV_CHUNK, body, states, unroll=KV_UNROLL)
    o_ref[...] = _flash_finish(states, tq).astype(o_ref.dtype)


def _dense_attn(q, kt, v, sink, *, q_row0, q_len, tq, segs, batch):
    n = q.shape[0]
    qpb = q_len // tq
    q0 = q_row0 // tq
    in_specs, args = [], []
    if sink is not None:
        in_specs.append(pl.BlockSpec(memory_space=pltpu.SMEM))
        args.append(sink)
    in_specs.append(pl.BlockSpec((tq, 256), lambda b, i: (q0 + b * qpb + i, 0)))
    args.append(q)
    for row0, klen in segs:
        k0 = row0 // klen
        in_specs.append(pl.BlockSpec((LANES, klen), lambda b, i, k0=k0: (0, k0 + b)))
        in_specs.append(pl.BlockSpec((klen, LANES), lambda b, i, k0=k0: (k0 + b, 0)))
        args += [kt, v]
    return pl.pallas_call(
        functools.partial(_dense_attn_kernel, tq=tq, seg_lens=tuple(s[1] for s in segs), has_sink=sink is not None),
        grid=(batch, qpb),
        in_specs=in_specs,
        out_specs=pl.BlockSpec((tq, 256), lambda b, i: (b * qpb + i, 0)),
        out_shape=jax.ShapeDtypeStruct((batch * q_len, 256), BF16),
        compiler_params=_params(("parallel", "parallel")),
        name="dense_attn",
    )(*args)


def _window_attn_kernel(sink_ref, q_ref, ktc_ref, vc_ref, ktp_ref, vp_ref, ktm_ref, vm_ref, ktn_ref, vn_ref, o_ref,
                        *, n_tiles):
    n = pl.program_id(1)
    nsub = TQ_WINDOW // CHUNK
    iq = lax.broadcasted_iota(jnp.int32, (2 * CHUNK, CHUNK), 0) & (CHUNK - 1)
    jk = lax.broadcasted_iota(jnp.int32, (2 * CHUNK, CHUNK), 1)
    below = jk >= iq
    above = jk <= iq
    vctx = vc_ref[...]
    for j in range(nsub):
        cols = slice(j * CHUNK, (j + 1) * CHUNK)
        q = q_ref[cols, :]
        states = []
        for g in range(2):
            q2 = _stack_heads(q, g)
            rows = _group_rows(g)
            state = _flash_init(2 * CHUNK, g, (sink_ref[2 * g], sink_ref[2 * g + 1]))
            state = _flash_update(state, q2, ktc_ref[rows, :], _aug_values(vctx, g))
            state = _flash_update(state, q2, ktm_ref[rows, cols], _aug_values(vm_ref[cols, :], g))
            if j > 0:
                prev = slice((j - 1) * CHUNK, j * CHUNK)
                state = _flash_update(state, q2, ktm_ref[rows, prev], _aug_values(vm_ref[prev, :], g), below)
            else:
                state = _flash_update(state, q2, ktp_ref[rows, :], _aug_values(vp_ref[...], g),
                                      jnp.logical_and(below, n > 0))
            if j < nsub - 1:
                nxt = slice((j + 1) * CHUNK, (j + 2) * CHUNK)
                state = _flash_update(state, q2, ktm_ref[rows, nxt], _aug_values(vm_ref[nxt, :], g), above)
            else:
                state = _flash_update(state, q2, ktn_ref[rows, :], _aug_values(vn_ref[...], g),
                                      jnp.logical_and(above, n < n_tiles - 1))
            states.append(state)
        o_ref[cols, :] = _flash_finish(states, CHUNK).astype(o_ref.dtype)


def _window_attn(q, kt, v, sink, *, batch, seq, ctx_len):
    nt = seq // TQ_WINDOW
    nsub = TQ_WINDOW // CHUNK
    nb = seq // CHUNK
    ctx0 = (batch * seq) // ctx_len
    prev = lambda b, n: b * nb + jnp.maximum(n * nsub - 1, 0)
    nxt = lambda b, n: b * nb + jnp.minimum((n + 1) * nsub, nb - 1)
    return pl.pallas_call(
        functools.partial(_window_attn_kernel, n_tiles=nt),
        grid=(batch, nt),
        in_specs=[pl.BlockSpec(memory_space=pltpu.SMEM),
                  pl.BlockSpec((TQ_WINDOW, 256), lambda b, n: (b * nt + n, 0)),
                  pl.BlockSpec((LANES, ctx_len), lambda b, n: (0, ctx0 + b)),
                  pl.BlockSpec((ctx_len, LANES), lambda b, n: (ctx0 + b, 0)),
                  pl.BlockSpec((LANES, CHUNK), lambda b, n: (0, prev(b, n))),
                  pl.BlockSpec((CHUNK, LANES), lambda b, n: (prev(b, n), 0)),
                  pl.BlockSpec((LANES, TQ_WINDOW), lambda b, n: (0, b * nt + n)),
                  pl.BlockSpec((TQ_WINDOW, LANES), lambda b, n: (b * nt + n, 0)),
                  pl.BlockSpec((LANES, CHUNK), lambda b, n: (0, nxt(b, n))),
                  pl.BlockSpec((CHUNK, LANES), lambda b, n: (nxt(b, n), 0))],
        out_specs=pl.BlockSpec((TQ_WINDOW, 256), lambda b, n: (b * nt + n, 0)),
        out_shape=jax.ShapeDtypeStruct((batch * seq, 256), BF16),
        compiler_params=_params(("parallel", "parallel")),
        name="window_attn",
    )(sink, q, kt, v, kt, v, kt, v, kt, v)


def _gelu_tanh(x):
    return 0.5 * x * (1.0 + jnp.tanh(math.sqrt(2.0 / math.pi) * (x + 0.044715 * (x * x * x))))


def _outproj_kernel(x_ref, gate_ref, gpost_ref, oa_ref, od_ref, oac_ref, odc_ref, yf_ref, yb_ref, xs_ref, z_ref,
                    dsk_ref, gn_ref, hf_ref, hb_ref, lg_ref, w_ref, o_ref, *, n_lat):
    is_lat = pl.program_id(0) * TM < n_lat
    oa = jnp.where(is_lat, oa_ref[...], oac_ref[...])
    od = jnp.where(is_lat, od_ref[...], odc_ref[...])
    y_ssd = (yf_ref[...] + yb_ref[...] + xs_ref[...] * dsk_ref[...]) * _silu(z_ref[...])
    ob = _rms(y_ssd, gn_ref[...])
    oc = (hf_ref[...] + hb_ref[...]) * _gelu_tanh(lg_ref[...])
    y = (_dot(oa, w_ref[0:256, :]) + _dot(ob.astype(BF16), w_ref[256:512, :])
         + _dot(oc.astype(BF16), w_ref[512:768, :]) + _dot(od, w_ref[768:1024, :]))
    o_ref[...] = x_ref[...] + gate_ref[0] * _rms(y, gpost_ref[...])


def _outproj(xu, mod, gpost, oa, od, oa_c, od_c, yf, yb, xbc, z, dsk, gn, hf, hb, lg, w_out,
             *, n_rows, n_lat, seq, batch):
    d = xu.shape[1]
    lat_tiles = n_lat // TM
    row = lambda w: pl.BlockSpec((TM, w), lambda i: (i, 0))
    lat = pl.BlockSpec((TM, 256), lambda i: (jnp.minimum(i, lat_tiles - 1), 0))
    ctx = pl.BlockSpec((TM, 256), lambda i: (jnp.maximum(i - lat_tiles, 0), 0))
    const = lambda a: pl.BlockSpec(a.shape, lambda i: (0,) * a.ndim)
    return pl.pallas_call(
        functools.partial(_outproj_kernel, n_lat=n_lat),
        grid=(n_rows // TM,),
        in_specs=[row(d), pl.BlockSpec((1, 1, d), _mod_spec(2, n_lat, seq, batch, TM)), const(gpost),
                  lat, lat, ctx, ctx, row(256), row(256), row(256), row(256), const(dsk), const(gn),
                  row(256), row(256), row(256), const(w_out)],
        out_specs=row(d),
        out_shape=jax.ShapeDtypeStruct((n_rows, d), F32),
        compiler_params=_params(("parallel",)),
        name="outproj",
    )(xu, mod, gpost, oa, od, oa_c, od_c, yf, yb, xbc, z, dsk, gn, hf, hb, lg, w_out)


def _ceil_seg(c):
    return jnp.floor((c + (SEG_ALIGN - 1)) * (1.0 / SEG_ALIGN)) * SEG_ALIGN


def _router_kernel(x_ref, shift_ref, scale_ref, gpre_ref, rwt_ref, rb_ref, hb_ref, ld_ref, wk_ref, tab_ref):
    h = _rms(x_ref[...], gpre_ref[...])
    h = h * (1.0 + scale_ref[0]) + shift_ref[0]
    hb = h.astype(BF16)
    hb_ref[...] = hb

    scores = jax.nn.sigmoid(_dot_nt(rwt_ref[...], hb))
    biased = scores + rb_ref[...]
    gsz = N_EXPERTS // N_EXPERT_GROUPS
    sub = lax.broadcasted_iota(jnp.int32, (gsz, TM), 0)
    blocks, gscore = [], []
    for g in range(N_EXPERT_GROUPS):
        blk = biased[g * gsz:(g + 1) * gsz, :]
        m1 = jnp.max(blk, axis=0, keepdims=True)
        first = jnp.min(jnp.where(blk == m1, sub, gsz), axis=0, keepdims=True)
        m2 = jnp.max(jnp.where(sub == first, -jnp.inf, blk), axis=0, keepdims=True)
        blocks.append(blk)
        gscore.append(m1 + m2)
    masked = []
    for g in range(N_EXPERT_GROUPS):
        rank = jnp.zeros((1, TM), F32)
        for g2 in range(N_EXPERT_GROUPS):
            if g2 == g:
                continue
            beats = (gscore[g2] > gscore[g]) | ((gscore[g2] == gscore[g]) if g2 < g else False)
            rank = rank + jnp.where(beats, 1.0, 0.0)
        masked.append(jnp.where(rank < TOPK_GROUPS, blocks[g], -jnp.inf))
    vals = jnp.concatenate(masked, axis=0)
    eidx = lax.broadcasted_iota(jnp.int32, (N_EXPERTS, TM), 0)
    self32 = jnp.zeros((N_EXPERTS, TM), F32)
    rest = vals
    for _ in range(TOP_K):
        top = jnp.max(rest, axis=0, keepdims=True)
        first = jnp.min(jnp.where(rest == top, eidx, N_EXPERTS), axis=0, keepdims=True)
        hit = eidx == first
        self32 = jnp.where(hit, 1.0, self32)
        rest = jnp.where(hit, -jnp.inf, rest)
    sel = self32 > 0.5
    picked = jnp.where(sel, scores, 0.0)
    wdense = picked / jnp.sum(picked, axis=0, keepdims=True) * ROUTED_SCALE

    tr = lax.broadcasted_iota(jnp.int32, (TM, TM), 0)
    tc = lax.broadcasted_iota(jnp.int32, (TM, TM), 1)
    before = jnp.where(tr < tc, 1.0, 0.0).astype(BF16)
    selb = self32.astype(BF16)
    pos = _dot(selb, before)
    er = lax.broadcasted_iota(jnp.int32, (N_EXPERTS, N_EXPERTS), 0)
    ec = lax.broadcasted_iota(jnp.int32, (N_EXPERTS, N_EXPERTS), 1)
    lower = jnp.where(ec < er, 1.0, 0.0).astype(BF16)
    upper = jnp.where(er < ec, 1.0, 0.0).astype(BF16)
    ksel = _dot(lower, selb)
    cnt_col = _ceil_seg(jnp.sum(self32, axis=1, keepdims=True))
    loc_col = _dot3_left(lower, jnp.broadcast_to(cnt_col, (N_EXPERTS, LANES)))[:, 0:1]
    cnt_row = _ceil_seg(_dot_nt(jnp.ones((SUBLANES, TM), BF16), selb))
    loc_row = _dot3(cnt_row, upper)
    tab_ref[...] = jnp.concatenate([cnt_row, loc_row], axis=1).astype(jnp.int32)

    r8 = lax.broadcasted_iota(jnp.int32, (TOP_K, TM), 0)
    ld = jnp.zeros((TOP_K, TM), F32)
    wk = jnp.zeros((TOP_K, TM), F32)
    stage_row = pos + loc_col
    for k in range(TOP_K):
        one = sel & (ksel == float(k))
        ld = jnp.where(r8 == k, jnp.sum(jnp.where(one, stage_row, 0.0), axis=0, keepdims=True), ld)
        wk = jnp.where(r8 == k, jnp.sum(jnp.where(one, wdense, 0.0), axis=0, keepdims=True), wk)
    ld_ref[...] = ld.astype(jnp.int32)
    wk_ref[...] = wk


def _router(xu, mod, gpre, rwt, rb, *, n_rows, n_lat, seq, batch):
    d = xu.shape[1]
    row = lambda w: pl.BlockSpec((TM, w), lambda i: (i, 0))
    col = pl.BlockSpec((TOP_K, TM), lambda i: (0, i))
    const = lambda a: pl.BlockSpec(a.shape, lambda i: (0,) * a.ndim)
    return pl.pallas_call(
        _router_kernel,
        grid=(n_rows // TM,),
        in_specs=[row(d), pl.BlockSpec((1, 1, d), _mod_spec(3, n_lat, seq, batch, TM)),
                  pl.BlockSpec((1, 1, d), _mod_spec(4, n_lat, seq, batch, TM)),
                  const(gpre), const(rwt), const(rb)],
        out_specs=(row(d), col, col, pl.BlockSpec((SUBLANES, 2 * N_EXPERTS), lambda i: (i, 0))),
        out_shape=(jax.ShapeDtypeStruct((n_rows, d), BF16),
                   jax.ShapeDtypeStruct((TOP_K, n_rows), jnp.int32),
                   jax.ShapeDtypeStruct((TOP_K, n_rows), F32),
                   jax.ShapeDtypeStruct((n_rows // TM * SUBLANES, 2 * N_EXPERTS), jnp.int32)),
        compiler_params=_params(("parallel",)),
        name="router",
    )(xu, mod, mod, gpre, rwt, rb)


def _pow2_pieces(limit):
    bits, b = [], limit
    while b >= SEG_ALIGN:
        bits.append(b)
        b //= 2
    return bits


def _copy_pieces(n, src_ref, src0, dst_ref, dst0, sem, limit, wait, same_src=False):
    for bit in _pow2_pieces(limit):
        @pl.when((n & bit) != 0)
        def _():
            off = n & ~(2 * bit - 1)
            cp = pltpu.make_async_copy(src_ref.at[pl.ds(pl.multiple_of(src0 + (0 if same_src else off), SEG_ALIGN),
                                                          bit)],
                                       dst_ref.at[pl.ds(pl.multiple_of(dst0 + off, SEG_ALIGN), bit)], sem)
            cp.wait() if wait else cp.start()


N_PIECE_TABS = 7


def _piece_copies(tile, tabs, stage_ref, slots_ref, sem, to_slots, wait):
    nbig_ref, nsmall_ref, _, bsrc_ref, bdst_ref, ssrc_ref, sdst_ref = tabs
    for rows, n_ref, a_ref, b_ref, cap in ((BIG_PIECE, nbig_ref, bsrc_ref, bdst_ref, BIG_MAX),
                                           (SEG_ALIGN, nsmall_ref, ssrc_ref, sdst_ref, SMALL_MAX)):
        def body(p, c, rows=rows, a_ref=a_ref, b_ref=b_ref, cap=cap):
            src = stage_ref.at[pl.ds(pl.multiple_of(a_ref[tile * cap + p], SEG_ALIGN), rows)]
            dst = slots_ref.at[pl.ds(pl.multiple_of(b_ref[tile * cap + p], SEG_ALIGN), rows)]
            cp = pltpu.make_async_copy(src, dst, sem) if to_slots else pltpu.make_async_copy(dst, src, sem)
            cp.wait() if wait else cp.start()
            return c
        lax.fori_loop(0, n_ref[tile], body, 0)


def _used_blocks(tile, tabs):
    return (tabs[2][tile] + TM - 1) // TM


def _for_used_blocks(used, body):
    for b in range(TOP_K):
        body(b)
    for b in range(TOP_K, STAGE_ROWS // TM):
        @pl.when(b < used)
        def _():
            body(b)


def _stage_rows_iota():
    return lax.broadcasted_iota(jnp.int32, (TM // 2, TM), 0).astype(F32).astype(BF16)


def _pick_matrix(ld, base, vals, jrow):
    rel = (ld - base).astype(F32)
    rel = jnp.where(jnp.logical_and(rel >= 0.0, rel < TM // 2), rel, -1.0).astype(BF16)
    out = jnp.zeros((TM // 2, TM), BF16)
    for k in range(TOP_K):
        out = jnp.where(rel[k:k + 1, :] == jrow, vals[k:k + 1, :], out)
    return out


def _dispatch_kernel(*refs):
    tabs = refs[:N_PIECE_TABS]
    pstart_ref, npad_ref, hb_ref, ld_ref, xs_ref, stage, zbuf, sem, zsem = refs[N_PIECE_TABS:]
    i = pl.program_id(0)

    @pl.when(i == 0)
    def _():
        zbuf[...] = jnp.zeros_like(zbuf)
        for wait in (False, True):
            def body(e, c, wait=wait):
                _copy_pieces(npad_ref[e], zbuf, 0, xs_ref, pstart_ref[e], zsem, BM_EXPERT // 2, wait, same_src=True)
                return c
            lax.fori_loop(0, N_EXPERTS, body, 0)

    ld = ld_ref[...]
    hb = hb_ref[...]
    jrow = _stage_rows_iota()
    ones = jnp.ones((TOP_K, TM), BF16)

    cur = stage.at[i & 1]

    def block(b):
        for half in range(2):
            base = b * TM + half * (TM // 2)
            cur[base:base + TM // 2, :] = _dot(_pick_matrix(ld, base, ones, jrow), hb).astype(BF16)

    _for_used_blocks(_used_blocks(i, tabs), block)

    @pl.when(i > 0)
    def _():
        _piece_copies(i - 1, tabs, stage.at[(i - 1) & 1], xs_ref, sem, True, True)

    _piece_copies(i, tabs, cur, xs_ref, sem, True, False)

    @pl.when(i == pl.num_programs(0) - 1)
    def _():
        _piece_copies(i, tabs, cur, xs_ref, sem, True, True)


def _dispatch(tabs, pad_start, n_pad, hb, ld, n_slots):
    n, d = hb.shape
    grid_spec = pltpu.PrefetchScalarGridSpec(
        num_scalar_prefetch=N_PIECE_TABS + 2,
        grid=(n // TM,),
        in_specs=[pl.BlockSpec((TM, d), lambda i, *_: (i, 0)),
                  pl.BlockSpec((TOP_K, TM), lambda i, *_: (0, i))],
        out_specs=pl.BlockSpec(memory_space=pl.ANY),
        scratch_shapes=[pltpu.VMEM((2, STAGE_ROWS, d), BF16), pltpu.VMEM((BM_EXPERT // 2, d), BF16),
                        pltpu.SemaphoreType.DMA(()), pltpu.SemaphoreType.DMA(())],
    )
    return pl.pallas_call(
        _dispatch_kernel,
        grid_spec=grid_spec,
        out_shape=jax.ShapeDtypeStruct((n_slots, d), BF16),
        compiler_params=_params(("arbitrary",)),
        name="moe_dispatch",
    )(*tabs, pad_start, n_pad, hb, ld)


def _expert_kernel(be_ref, na_ref, nxt_ref, slot_ref, xs_hbm, wg_hbm, wu_hbm, wd_hbm, ys_ref,
                   xbuf, wg_raw, wu_raw, wd_raw, wgub, wdb, xsem, wsem, *, layer):
    i = pl.program_id(0)
    n_act = na_ref[0]

    def weight_copies(e, slot):
        return [pltpu.make_async_copy(src.at[layer, e], dst.at[slot], wsem.at[slot, j])
                for j, (src, dst) in enumerate(((wg_hbm, wg_raw), (wu_hbm, wu_raw), (wd_hbm, wd_raw)))]

    def rows_copy(blk):
        buf = lax.rem(blk, X_BUFFERS)
        return pltpu.make_async_copy(xs_hbm.at[pl.ds(pl.multiple_of(blk * BM_EXPERT, BM_EXPERT), BM_EXPERT)],
                                     xbuf.at[buf], xsem.at[buf])

    @pl.when(i < n_act)
    def _():
        e, slot = be_ref[i], slot_ref[i]

        @pl.when(i == 0)
        def _():
            for cp in weight_copies(e, slot):
                cp.start()
            for j in range(X_BUFFERS - 1):
                @pl.when(j < n_act)
                def _():
                    rows_copy(j).start()

        @pl.when(i + (X_BUFFERS - 1) < n_act)
        def _():
            rows_copy(i + (X_BUFFERS - 1)).start()

        @pl.when(jnp.logical_or(i == 0, e != be_ref[jnp.maximum(i - 1, 0)]))
        def _():
            for cp in weight_copies(e, slot):
                cp.wait()
            wgub[:, :EXPERT_HIDDEN] = wg_raw[slot].astype(BF16)
            wgub[:, EXPERT_HIDDEN:] = wu_raw[slot].astype(BF16)
            wdb[...] = wd_raw[slot].astype(BF16)

            @pl.when(nxt_ref[i] != e)
            def _():
                for cp in weight_copies(nxt_ref[i], 1 - slot):
                    cp.start()

        rows_copy(i).wait()
        gu = _dot(xbuf[lax.rem(i, X_BUFFERS)], wgub[...])
        hid = _silu(gu[:, :EXPERT_HIDDEN]) * gu[:, EXPERT_HIDDEN:]
        ys_ref[...] = _dot(hid.astype(BF16), wdb[...]).astype(ys_ref.dtype)


def _experts(block_e, n_active, next_e, w_slot, xs, wg, wu, wd, layer):
    n_slots, d = xs.shape
    nb = n_slots // BM_EXPERT
    tiles = pl.BlockSpec((BM_EXPERT, d), lambda i, be, na, *_: (jnp.minimum(i, na[0] - 1), 0))
    anywhere = pl.BlockSpec(memory_space=pl.ANY)
    grid_spec = pltpu.PrefetchScalarGridSpec(
        num_scalar_prefetch=4,
        grid=(nb,),
        in_specs=[anywhere, anywhere, anywhere, anywhere],
        out_specs=tiles,
        scratch_shapes=[pltpu.VMEM((X_BUFFERS, BM_EXPERT, d), BF16),
                        pltpu.VMEM((2, d, EXPERT_HIDDEN), F32), pltpu.VMEM((2, d, EXPERT_HIDDEN), F32),
                        pltpu.VMEM((2, EXPERT_HIDDEN, d), F32),
                        pltpu.VMEM((d, 2 * EXPERT_HIDDEN), BF16),
                        pltpu.VMEM((EXPERT_HIDDEN, d), BF16),
                        pltpu.SemaphoreType.DMA((X_BUFFERS,)), pltpu.SemaphoreType.DMA((2, 3))],
    )
    return pl.pallas_call(
        functools.partial(_expert_kernel, layer=layer),
        grid_spec=grid_spec,
        out_shape=jax.ShapeDtypeStruct((n_slots, d), BF16),
        compiler_params=_params(("arbitrary",)),
        name="moe_experts",
    )(block_e, n_active, next_e, w_slot, xs, wg, wu, wd)


def _combine_kernel(*refs):
    tabs = refs[:N_PIECE_TABS]
    (ys_ref, ld_ref, wk_ref, hb_ref, x_ref, gate_ref, gpost_ref, sg_ref, su_ref, sd_ref, o_ref,
     stage, acc_ref, sem) = refs[N_PIECE_TABS:]
    i = pl.program_id(0)

    @pl.when(i == 0)
    def _():
        stage[...] = jnp.zeros_like(stage)
        _piece_copies(0, tabs, stage.at[0], ys_ref, sem, False, False)

    cur = stage.at[i & 1]
    _piece_copies(i, tabs, cur, ys_ref, sem, False, True)

    @pl.when(i + 1 < pl.num_programs(0))
    def _():
        _piece_copies(i + 1, tabs, stage.at[(i + 1) & 1], ys_ref, sem, False, False)

    hb = hb_ref[...]
    acc_ref[...] = _dot((_silu(_dot(hb, sg_ref[...])) * _dot(hb, su_ref[...])).astype(BF16), sd_ref[...])

    ld = ld_ref[...]
    wkb = wk_ref[...].astype(BF16)
    jrow = _stage_rows_iota()

    def block(b):
        for half in range(2):
            base = b * TM + half * (TM // 2)
            weights = _pick_matrix(ld, base, wkb, jrow)
            acc_ref[...] += lax.dot_general(weights, cur[base:base + TM // 2, :], (((0,), (0,)), ((), ())),
                                            preferred_element_type=F32)

    _for_used_blocks(_used_blocks(i, tabs), block)
    o_ref[...] = x_ref[...] + gate_ref[0] * _rms(acc_ref[...], gpost_ref[...])


def _combine(tabs, ys, ld, wk, hb, xu, mod, gpost, sg, su, sd, *, n_rows, n_lat, seq, batch):
    d = xu.shape[1]
    row = lambda w: pl.BlockSpec((TM, w), lambda i, *_: (i, 0))
    col = pl.BlockSpec((TOP_K, TM), lambda i, *_: (0, i))
    const = lambda a: pl.BlockSpec(a.shape, lambda i, *_: (0,) * a.ndim)
    mod_map = _mod_spec(5, n_lat, seq, batch, TM)
    grid_spec = pltpu.PrefetchScalarGridSpec(
        num_scalar_prefetch=N_PIECE_TABS,
        grid=(n_rows // TM,),
        in_specs=[pl.BlockSpec(memory_space=pl.ANY), col, col, row(d), row(d),
                  pl.BlockSpec((1, 1, d), lambda i, *_: mod_map(i)),
                  const(gpost), const(sg), const(su), const(sd)],
        out_specs=row(d),
        scratch_shapes=[pltpu.VMEM((2, STAGE_ROWS, d), BF16), pltpu.VMEM((TM, d), F32), pltpu.SemaphoreType.DMA(())],
    )
    return pl.pallas_call(
        _combine_kernel,
        grid_spec=grid_spec,
        out_shape=jax.ShapeDtypeStruct((n_rows, d), F32),
        compiler_params=_params(("arbitrary",)),
        name="moe_combine",
    )(*tabs, ys, ld, wk, hb, xu, mod, gpost, sg, su, sd)


def _deinterleave(w):
    cols = w.shape[-1]
    perm = jnp.concatenate([jnp.arange(0, HEAD_DIM, 2), jnp.arange(1, HEAD_DIM, 2)])
    idx = (jnp.arange(cols // HEAD_DIM)[:, None] * HEAD_DIM + perm[None, :]).reshape(-1)
    return w[..., idx]


def _pad_in_proj(w_in):
    d = w_in.shape[0]
    o = 0
    parts = {}
    for name, width in (("qa", 256), ("ka", 128), ("va", 128), ("z", 256), ("xs", 256), ("bm", 128), ("cm", 128),
                        ("dtf", 4), ("dtb", 4), ("lx", 256), ("lg", 256), ("qd", 256), ("kd", 128), ("vd", 128)):
        parts[name] = w_in[:, o:o + width]
        o += width
    dt = jnp.concatenate([parts["dtf"], parts["dtb"], jnp.zeros((d, LANES - 8), w_in.dtype)], axis=1)
    cols = [_deinterleave(parts["qa"]), _deinterleave(parts["ka"]), parts["va"],
            _deinterleave(parts["qd"]), _deinterleave(parts["kd"]), parts["vd"],
            parts["z"], parts["xs"], parts["bm"], parts["cm"], dt, parts["lx"], parts["lg"]]
    return jnp.concatenate(cols, axis=1).astype(BF16)


def _rope_tables(seq):
    t = jnp.arange(seq)
    rowp = (t // GRID_W).astype(F32)
    colp = (t % GRID_W).astype(F32)
    axis_dim = HEAD_DIM // 2
    inv_freq = ROPE_THETA ** (-jnp.arange(0, axis_dim, 2, dtype=F32) / axis_dim)
    ang = jnp.concatenate([rowp[:, None] * inv_freq, colp[:, None] * inv_freq], axis=-1)
    cos, sin = jnp.cos(ang), jnp.sin(ang)
    cos_h = jnp.concatenate([cos, cos], axis=-1)
    sin_h = jnp.concatenate([-sin, sin], axis=-1)
    return jnp.tile(cos_h, (1, 4)), jnp.tile(sin_h, (1, 4))


def _block_diag(w):
    nb, bd, _ = w.shape
    eye = jnp.eye(nb, dtype=w.dtype)
    return (eye[:, None, :, None] * w[:, :, None, :]).reshape(nb * bd, nb * bd)


def _piece_table(counts, cap, stage0, slot0, rows, ids):
    ends = jnp.cumsum(counts, axis=1)
    q = jnp.arange(cap, dtype=jnp.int32)
    owner = jnp.sum((ends[:, None, :] <= q[None, :, None]).astype(jnp.int32), axis=-1)
    mine = owner[:, :, None] == ids
    pick = lambda v: jnp.sum(jnp.where(mine, v[:, None, :], 0), axis=-1)
    step = rows * (q[None, :] - pick(ends - counts))
    return (pick(stage0) + step).reshape(-1), (pick(slot0) + step).reshape(-1)


def _lane_row(fwd, bwd):
    return jnp.concatenate([fwd, bwd, jnp.zeros((LANES - 8,), F32)]).reshape(1, LANES)


def kernel(x, c, ctx, c_ctx, w_ada, b_ada, g_mix_pre, g_mix_post, g_ffn_pre, g_ffn_post, w_in, w_out, a_sink,
           ssd_conv_w, ssd_conv_b, ssd_dt_bias, ssd_a_log, ssd_d, ssd_norm, lru_conv_w, lru_conv_b, lru_w_a,
           lru_b_a, lru_w_i, lru_b_i, lru_lambda, d_q_norm, d_k_norm, router_w, router_bias, exp_w_gate,
           exp_w_up, exp_w_down, sh_w_gate, sh_w_up, sh_w_down):
    batch, seq, d = x.shape
    ctx_len = ctx.shape[1]
    depth = w_ada.shape[0]
    n_lat = batch * seq
    n_ctx = batch * ctx_len
    n_all = n_lat + n_ctx
    assert seq % TM == 0 and n_ctx % TM == 0 and ctx_len >= LRU_CONV
    assert seq % TQ_WINDOW == 0 and seq % (SCAN_CHUNKS * CHUNK) == 0 and ctx_len % (SCAN_CHUNKS * CHUNK) == 0
    assert ctx_len <= KV_CHUNK and seq % KV_CHUNK == 0 and seq % TQ_GLOBAL == 0 and batch + 1 <= SUBLANES

    xu = jnp.concatenate([x.reshape(n_lat, d), ctx.reshape(n_ctx, d)], axis=0)
    cin = jnp.concatenate([c, c_ctx[None, :], jnp.zeros((SUBLANES - batch - 1, d), F32)], axis=0)
    mod_all = _adaln(cin, w_ada, b_ada)
    cos_t, sin_t = _rope_tables(seq)
    hm = jnp.kron(jnp.eye(4, dtype=F32), jnp.full((HEAD_DIM, HEAD_DIM), 1.0 / HEAD_DIM, F32)).astype(BF16)

    for l in range(depth):
        with_ctx = l < depth - 1
        mod = mod_all[l].reshape(SUBLANES * 6, 1, d)
        gq = jnp.tile(_deinterleave(d_q_norm[l]), 4).reshape(1, 256)
        gk = jnp.tile(_deinterleave(d_k_norm[l]), 2).reshape(1, LANES)
        qa, kat, va, qd, kdt, vd, z, xbc, dt, lu, lg = _inproj(
            xu, mod, g_mix_pre[l].reshape(1, d), _pad_in_proj(w_in[l]), cos_t, sin_t, gq, gk, hm,
            ssd_conv_w[l], ssd_conv_b[l].reshape(1, -1), lru_conv_w[l], lru_conv_b[l].reshape(1, -1),
            n_lat=n_lat, seq=seq, ctx_len=ctx_len, batch=batch)
        yf, yb = _ssd(xbc, dt, _lane_row(ssd_dt_bias[l, 0], ssd_dt_bias[l, 1]),
                      _lane_row(ssd_a_log[l, 0], ssd_a_log[l, 1]), batch=batch, seq=seq, ctx_len=ctx_len)
        wg = jnp.stack([jnp.concatenate([_block_diag(lru_w_a[l, dd]), _block_diag(lru_w_i[l, dd])], axis=1)
                        for dd in range(2)]).astype(BF16)
        bg = jnp.concatenate([lru_b_a[l], lru_b_i[l]], axis=1).reshape(2, 1, 2 * LRU_WIDTH)
        hf, hb = _lru(lu, wg, bg, lru_lambda[l].reshape(2, 1, LRU_WIDTH), batch=batch, seq=seq, ctx_len=ctx_len)

        oa = _window_attn(qa, kat, va, a_sink[l], batch=batch, seq=seq, ctx_len=ctx_len)
        od = _dense_attn(qd, kdt, vd, None, q_row0=0, q_len=seq, tq=TQ_GLOBAL,
                         segs=[(n_lat, ctx_len), (0, seq)], batch=batch)
        if with_ctx:
            oa_c = _dense_attn(qa, kat, va, a_sink[l], q_row0=n_lat, q_len=ctx_len, tq=ctx_len,
                               segs=[(n_lat, ctx_len)], batch=batch)
            od_c = _dense_attn(qd, kdt, vd, None, q_row0=n_lat, q_len=ctx_len, tq=ctx_len,
                               segs=[(n_lat, ctx_len)], batch=batch)
        else:
            oa_c, od_c = oa, od
        n_rows = n_all if with_ctx else n_lat

        dsk = jnp.repeat(ssd_d[l], HEAD_DIM).reshape(1, 256)
        xu_mid = _outproj(xu, mod, g_mix_post[l].reshape(1, d), oa, od, oa_c, od_c, yf, yb, xbc, z, dsk,
                          ssd_norm[l].reshape(1, 256), hf, hb, lg, w_out[l].astype(BF16),
                          n_rows=n_rows, n_lat=n_lat, seq=seq, batch=batch)

        hb_ffn, ld, wk, tab = _router(xu_mid, mod, g_ffn_pre[l].reshape(1, d), router_w[l].T.astype(BF16),
                                      router_bias[l].reshape(N_EXPERTS, 1), n_rows=n_rows, n_lat=n_lat,
                                      seq=seq, batch=batch)
        n_tiles = n_rows // TM
        tab = tab.reshape(n_tiles, SUBLANES, 2 * N_EXPERTS)[:, 0, :]
        seg_cnt, seg_loc = tab[:, :N_EXPERTS], tab[:, N_EXPERTS:]
        counts = jnp.sum(seg_cnt, axis=0)
        padded = (counts + BM_EXPERT - 1) // BM_EXPERT * BM_EXPERT
        padded_end = jnp.cumsum(padded)
        offs = padded_end - padded
        seg_off = offs[None, :] + jnp.cumsum(seg_cnt, axis=0) - seg_cnt
        n_blocks = (n_rows * TOP_K + n_tiles * N_EXPERTS * SEG_ALIGN) // BM_EXPERT + N_EXPERTS
        n_active = (padded_end[-1] // BM_EXPERT).astype(jnp.int32).reshape(1)
        block_start = jnp.arange(n_blocks, dtype=jnp.int32) * BM_EXPERT
        block_e = jnp.minimum(jnp.sum((padded_end[None, :] <= block_start[:, None]).astype(jnp.int32), axis=1),
                              N_EXPERTS - 1)
        ids = jnp.arange(N_EXPERTS, dtype=jnp.int32)
        n_big = seg_cnt // BIG_PIECE
        n_small = (seg_cnt % BIG_PIECE) // SEG_ALIGN
        tabs = (jnp.sum(n_big, axis=1), jnp.sum(n_small, axis=1), seg_loc[:, -1] + seg_cnt[:, -1],
                *_piece_table(n_big, BIG_MAX, seg_loc, seg_off, BIG_PIECE, ids),
                *_piece_table(n_small, SMALL_MAX, seg_loc + n_big * BIG_PIECE, seg_off + n_big * BIG_PIECE,
                              SEG_ALIGN, ids))
        xs = _dispatch(tabs, offs + counts, padded - counts, hb_ffn, ld, n_blocks * BM_EXPERT)
        has_rows = padded > 0
        later = jnp.logical_and(ids[None, :] > ids[:, None], has_rows[None, :])
        nxt_of = jnp.min(jnp.where(later, ids[None, :], N_EXPERTS), axis=1)
        nxt_of = jnp.where(nxt_of == N_EXPERTS, ids, nxt_of)
        slot_of = (jnp.cumsum(has_rows.astype(jnp.int32)) - 1) & 1
        own = block_e[:, None] == ids[None, :]
        next_e = jnp.sum(jnp.where(own, nxt_of[None, :], 0), axis=1)
        w_slot = jnp.sum(jnp.where(own, slot_of[None, :], 0), axis=1)
        ys = _experts(block_e, n_active, next_e, w_slot, xs, exp_w_gate, exp_w_up, exp_w_down, l)
        xu = _combine(tabs, ys, ld, wk, hb_ffn, xu_mid, mod, g_ffn_post[l].reshape(1, d), sh_w_gate[l].astype(BF16),
                      sh_w_up[l].astype(BF16), sh_w_down[l].astype(BF16),
                      n_rows=n_rows, n_lat=n_lat, seq=seq, batch=batch)
    return xu[:n_lat].reshape(batch, seq, d)
```

```python
import functools
import math

import jax
import jax.numpy as jnp
from jax import lax
from jax.experimental import pallas as pl
from jax.experimental.pallas import tpu as pltpu

F32 = jnp.float32
BF16 = jnp.bfloat16

HEAD_DIM = 64
GRID_W = 64
ROPE_THETA = 10000.0
NORM_EPS = 1e-6
NEG_INF = -1e30
A_HEADS, A_KV_HEADS, WINDOW = 4, 2, 128
SSD_HEADS, SSD_GROUPS, SSD_STATE, SSD_CONV = 4, 2, 64, 4
LRU_WIDTH, LRU_BLOCKS, LRU_CONV, LRU_C = 256, 4, 4, 8.0
D_HEADS, D_KV_HEADS = 4, 2
N_EXPERTS, N_EXPERT_GROUPS, TOPK_GROUPS, TOP_K = 64, 8, 4, 8
EXPERT_HIDDEN, SHARED_HIDDEN = 256, 256
ROUTED_SCALE = 2.5

LANES = 128
SUBLANES = 8

TM = 512
CHUNK = 128
SCAN_CHUNKS = 2
TQ_GLOBAL = 256
TQ_WINDOW = 512
KV_CHUNK = 256
KV_UNROLL = 16
BM_EXPERT = 512
X_BUFFERS = 3
SEG_ALIGN = 16
STAGE_ROWS = TM * TOP_K + N_EXPERTS * SEG_ALIGN
BIG_PIECE = 64
BIG_MAX = STAGE_ROWS // BIG_PIECE
SMALL_MAX = N_EXPERTS * (BIG_PIECE // SEG_ALIGN - 1)
VMEM_LIMIT = 48 * 1024 * 1024

C_QA, C_KA, C_VA = 0, 256, 384
C_QD, C_KD, C_VD = 512, 768, 896
C_Z, C_XBC, C_DT = 1024, 1280, 1792
C_LX, C_LG = 1920, 2176
NP_IN = 2432


def _dot(a, b):
    return jnp.dot(a, b, preferred_element_type=F32)


def _dot_nt(a, b):
    return lax.dot_general(a, b, (((1,), (1,)), ((), ())), preferred_element_type=F32)


def _dot3(a, b):
    a1 = a.astype(BF16)
    r1 = a - a1.astype(F32)
    a2 = r1.astype(BF16)
    a3 = (r1 - a2.astype(F32)).astype(BF16)
    return _dot(a1, b) + _dot(a2, b) + _dot(a3, b)


def _dot3_left(a, b):
    b1 = b.astype(BF16)
    r1 = b - b1.astype(F32)
    b2 = r1.astype(BF16)
    b3 = (r1 - b2.astype(F32)).astype(BF16)
    return _dot(a, b1) + _dot(a, b2) + _dot(a, b3)


def _silu(x):
    return x * jax.nn.sigmoid(x)


def _softplus(x):
    return jnp.maximum(x, 0.0) + jnp.log1p(jnp.exp(-jnp.abs(x)))


def _rms(x, gain):
    return x * lax.rsqrt(jnp.mean(x * x, axis=-1, keepdims=True) + NORM_EPS) * gain


def _params(sem=None):
    return pltpu.CompilerParams(dimension_semantics=sem, vmem_limit_bytes=VMEM_LIMIT)


def _adaln_kernel(c_ref, w_ref, b_ref, o_ref):
    s = _silu(c_ref[...])
    o_ref[0] = _dot(s.astype(BF16), w_ref[0].astype(BF16)) + b_ref[0]


def _adaln(cin, w_ada, b_ada):
    depth, d, n6 = w_ada.shape
    tn = 1024
    return pl.pallas_call(
        _adaln_kernel,
        grid=(depth, n6 // tn),
        in_specs=[pl.BlockSpec((SUBLANES, d), lambda l, j: (0, 0)),
                  pl.BlockSpec((1, d, tn), lambda l, j: (l, 0, j)),
                  pl.BlockSpec((1, 1, tn), lambda l, j: (l, 0, j))],
        out_specs=pl.BlockSpec((1, SUBLANES, tn), lambda l, j: (l, 0, j)),
        out_shape=jax.ShapeDtypeStruct((depth, SUBLANES, n6), F32),
        compiler_params=_params(("parallel", "parallel")),
        name="adaln",
    )(cin, w_ada, b_ada.reshape(depth, 1, n6))


def _swap_halves(t):
    w = t.shape[1]
    lane = lax.broadcasted_iota(jnp.int32, (1, w), 1)
    first = (lane & 32) == 0
    return jnp.where(first, pltpu.roll(t, w - 32, axis=1), pltpu.roll(t, 32, axis=1))


def _inproj_kernel(x_ref, xp_ref, xn_ref, xc_ref, xcp_ref, xcn_ref, shift_ref, scale_ref, gpre_ref, w_ref,
                   cos_ref, sin_ref, gq_ref, gk_ref, hm_ref, ws_ref, bs_ref, wl_ref, bl_ref,
                   qa_ref, kat_ref, va_ref, qd_ref, kdt_ref, vd_ref, z_ref, xbc_ref, dt_ref, lu_ref, lg_ref,
                   *, n_lat, seq, ctx_len, split):
    i = pl.program_id(0)
    is_lat = i * TM < n_lat
    first = jnp.logical_or(is_lat, not split)

    def pre(a_ref, b_ref):
        x = jnp.where(first, a_ref[...], b_ref[...])
        return _rms(x, gpre_ref[...]) * (1.0 + scale_ref[0]) + shift_ref[0]

    h = pre(x_ref, xc_ref)
    hb = h.astype(BF16)
    hb_ext = jnp.concatenate([pre(xp_ref, xcp_ref), h, pre(xn_ref, xcn_ref)], axis=0).astype(BF16)

    def sec(a, b):
        return _dot(hb, w_ref[:, a:b])

    row = lax.broadcasted_iota(jnp.int32, (TM, 1), 0)
    pos = jnp.where(is_lat, lax.rem(i * TM + row, seq), lax.rem(i * TM - n_lat + row, ctx_len))
    slen = jnp.where(is_lat, seq, ctx_len)

    def conv(a, b, w, bias):
        ext = _dot(hb_ext, w_ref[:, a:b])
        prev, x, nxt = ext[0:SUBLANES], ext[SUBLANES:SUBLANES + TM], ext[SUBLANES + TM:]
        xm1 = jnp.where(row == 0, prev[7:8, :], pltpu.roll(x, 1, axis=0))
        xm2 = jnp.where(row == 0, prev[6:7, :], jnp.where(row == 1, prev[7:8, :], pltpu.roll(x, 2, axis=0)))
        xp1 = jnp.where(row == TM - 1, nxt[0:1, :], pltpu.roll(x, TM - 1, axis=0))
        xm1 = jnp.where(pos >= 1, xm1, 0.0)
        xm2 = jnp.where(pos >= 2, xm2, 0.0)
        xp1 = jnp.where(pos <= slen - 2, xp1, 0.0)
        return w[0:1, :] * xm2 + w[1:2, :] * xm1 + w[2:3, :] * x + w[3:4, :] * xp1 + bias

    cos = jnp.where(is_lat, cos_ref[...], 1.0)
    sin = jnp.where(is_lat, sin_ref[...], 0.0)

    def rope(t):
        w = t.shape[1]
        return t * cos[:, :w] + _swap_halves(t) * sin[:, :w]

    def head_norm(t, gain):
        w = t.shape[1]
        ms = _dot3(t * t, hm_ref[:w, :w])
        return t * lax.rsqrt(ms + NORM_EPS) * gain

    scale = HEAD_DIM ** -0.5
    qa_ref[...] = (rope(sec(C_QA, C_KA)) * scale).astype(BF16)
    kat_ref[...] = rope(sec(C_KA, C_VA)).T.astype(BF16)
    va_ref[...] = sec(C_VA, C_QD).astype(BF16)
    qd_ref[...] = (rope(head_norm(sec(C_QD, C_KD), gq_ref[...])) * scale).astype(BF16)
    kdt_ref[...] = rope(head_norm(sec(C_KD, C_VD), gk_ref[...])).T.astype(BF16)
    vd_ref[...] = sec(C_VD, C_Z).astype(BF16)
    z_ref[...] = sec(C_Z, C_XBC)
    xbc_ref[...] = _silu(conv(C_XBC, C_DT, ws_ref[...], bs_ref[...]))
    dt_ref[...] = sec(C_DT, C_LX)
    lu_ref[...] = conv(C_LX, C_LG, wl_ref[...], bl_ref[...])
    lg_ref[...] = sec(C_LG, NP_IN)


def _mod_spec(chunk, n_lat, seq, batch, tile):
    def imap(i):
        row0 = i * tile
        seg = jnp.where(row0 < n_lat, row0 // seq, batch)
        return (seg * 6 + chunk, 0, 0)
    return imap


def _row_sources(x_lat, x_ctx, n_lat):
    d = x_lat.shape[1]
    r8 = TM // SUBLANES
    lat_tiles = n_lat // TM

    def specs(arr, tile_of):
        n8 = arr.shape[0] // SUBLANES
        tiles = arr.shape[0] // TM
        t = lambda i: jnp.clip(tile_of(i), 0, tiles - 1)
        return [pl.BlockSpec((TM, d), lambda i: (t(i), 0)),
                pl.BlockSpec((SUBLANES, d), lambda i: (jnp.clip(t(i) * r8 - 1, 0, n8 - 1), 0)),
                pl.BlockSpec((SUBLANES, d), lambda i: (jnp.clip((t(i) + 1) * r8, 0, n8 - 1), 0))]

    if x_ctx is None:
        return specs(x_lat, lambda i: i) + specs(x_lat, lambda i: 0 * i), [x_lat] * 6
    return specs(x_lat, lambda i: i) + specs(x_ctx, lambda i: i - lat_tiles), [x_lat] * 3 + [x_ctx] * 3


def _inproj(x_lat, x_ctx, mod, gpre, w_pad, cos_t, sin_t, gq, gk, hm, ws, bs, wl, bl, *, n_lat, seq, ctx_len, batch):
    n = x_lat.shape[0] + (0 if x_ctx is None else x_ctx.shape[0])
    d = x_lat.shape[1]
    nt = n // TM
    spt = seq // TM
    x_specs, x_args = _row_sources(x_lat, x_ctx, n_lat)
    row = lambda w: pl.BlockSpec((TM, w), lambda i: (i, 0))
    colT = pl.BlockSpec((LANES, TM), lambda i: (0, i))
    const = lambda a: pl.BlockSpec(a.shape, lambda i: (0,) * a.ndim)
    out_shapes = (
        jax.ShapeDtypeStruct((n, 256), BF16), jax.ShapeDtypeStruct((LANES, n), BF16),
        jax.ShapeDtypeStruct((n, LANES), BF16),
        jax.ShapeDtypeStruct((n, 256), BF16), jax.ShapeDtypeStruct((LANES, n), BF16),
        jax.ShapeDtypeStruct((n, LANES), BF16),
        jax.ShapeDtypeStruct((n, 256), F32), jax.ShapeDtypeStruct((n, 512), F32),
        jax.ShapeDtypeStruct((n, LANES), F32), jax.ShapeDtypeStruct((n, 256), F32),
        jax.ShapeDtypeStruct((n, 256), F32))
    return pl.pallas_call(
        functools.partial(_inproj_kernel, n_lat=n_lat, seq=seq, ctx_len=ctx_len, split=x_ctx is not None),
        grid=(nt,),
        in_specs=x_specs + [
                  pl.BlockSpec((1, 1, d), _mod_spec(0, n_lat, seq, batch, TM)),
                  pl.BlockSpec((1, 1, d), _mod_spec(1, n_lat, seq, batch, TM)),
                  const(gpre), const(w_pad),
                  pl.BlockSpec((TM, 256), lambda i: (i % spt, 0)),
                  pl.BlockSpec((TM, 256), lambda i: (i % spt, 0)),
                  const(gq), const(gk), const(hm), const(ws), const(bs), const(wl), const(bl)],
        out_specs=(row(256), colT, row(LANES), row(256), colT, row(LANES),
                   row(256), row(512), row(LANES), row(256), row(256)),
        out_shape=out_shapes,
        compiler_params=_params(("parallel",)),
        name="inproj",
    )(*x_args, mod, mod, gpre, w_pad, cos_t, sin_t, gq, gk, hm, ws, bs, wl, bl)


def _chunk_maps(batch, seq, ctx_len):
    ncx = ctx_len // (SCAN_CHUNKS * CHUNK)
    nl = seq // (SCAN_CHUNKS * CHUNK)
    lat_blocks = batch * nl

    def block(b, c):
        return jnp.where(c < ncx, lat_blocks + b * ncx + c, b * nl + (c - ncx))

    def fwd(b, k):
        return (block(b, k), 0)

    def bwd(b, k):
        c = jnp.where(k < ncx, ncx - 1 - k, ncx + (nl - 1 - (k - ncx)))
        return (block(b, c), 0)

    return fwd, bwd, ncx + nl


def _ssd_kernel(xf_ref, dtf_ref, xb_ref, dtb_ref, dtbias_ref, alog_ref, yf_ref, yb_ref, state_ref):
    k = pl.program_id(1)

    @pl.when(k == 0)
    def _():
        state_ref[...] = jnp.zeros_like(state_ref)

    ri = lax.broadcasted_iota(jnp.int32, (CHUNK, CHUNK), 0)
    ci = lax.broadcasted_iota(jnp.int32, (CHUNK, CHUNK), 1)
    lane_lo = ci < HEAD_DIM
    aneg = -jnp.exp(alog_ref[...])
    dtbias = dtbias_ref[...]

    order = [(d, s if d == 0 else SCAN_CHUNKS - 1 - s) for s in range(SCAN_CHUNKS) for d in range(2)]
    for d, sub in order:
        x_ref, dt_ref, y_ref = ((xf_ref, dtf_ref, yf_ref), (xb_ref, dtb_ref, yb_ref))[d]
        rws = slice(sub * CHUNK, (sub + 1) * CHUNK)
        causal = (ri >= ci) if d == 0 else (ci >= ri)
        tmat = jnp.where(causal, 1.0, 0.0).astype(BF16)
        xs = x_ref[rws, 0:256]
        bm = x_ref[rws, 256:384]
        cm = x_ref[rws, 384:512]
        dtp = _softplus(dt_ref[rws, :] + dtbias)
        acum = _dot3_left(tmat, dtp * aneg)
        acum_t = acum.T
        bt = bm.T.astype(BF16)
        cmb = cm.astype(BF16)
        bmb = bm.astype(BF16)
        tot_row = CHUNK - 1 if d == 0 else 0
        for p in range(2):
            cmask = jnp.where(lane_lo if p == 0 else jnp.logical_not(lane_lo), cmb, jnp.zeros_like(cmb))
            cb = _dot_nt(cmask, bmb)
            cols, dts, ys = [], [], []
            x_pair = xs[:, p * LANES:(p + 1) * LANES]
            for j in range(2):
                col = 4 * d + 2 * p + j
                colb = jnp.broadcast_to(acum[:, col:col + 1], (CHUNK, CHUNK))
                rowb = jnp.broadcast_to(acum_t[col:col + 1, :], (CHUNK, CHUNK))
                cols.append(colb)
                dts.append(jnp.broadcast_to(dtp[:, col:col + 1], (CHUNK, CHUNK)))
            col_pair = jnp.where(lane_lo, cols[0], cols[1])
            dt_pair = jnp.where(lane_lo, dts[0], dts[1])
            xdt = x_pair * dt_pair
            xdt_b = xdt.astype(BF16)
            for j in range(2):
                col = 4 * d + 2 * p + j
                rowb = jnp.broadcast_to(acum_t[col:col + 1, :], (CHUNK, CHUNK))
                decay = jnp.exp(jnp.where(causal, cols[j] - rowb, NEG_INF))
                ys.append(_dot((cb * decay).astype(BF16), xdt_b))
            y_intra = jnp.where(lane_lo, ys[0], ys[1])
            s_old = state_ref[d, p]
            y_inter = _dot(cmask, s_old.astype(BF16)) * jnp.exp(col_pair)
            y_ref[rws, p * LANES:(p + 1) * LANES] = y_intra + y_inter
            tot_pair = col_pair[tot_row:tot_row + 1, :]
            to_end = jnp.exp(tot_pair - col_pair)
            state_ref[d, p] = s_old * jnp.exp(tot_pair) + _dot(bt, (xdt * to_end).astype(BF16))


def _ssd(xbc, dt, dtbias_row, alog_row, *, batch, seq, ctx_len):
    n = xbc.shape[0]
    fwd, bwd, steps = _chunk_maps(batch, seq, ctx_len)
    rows = SCAN_CHUNKS * CHUNK
    const = lambda a: pl.BlockSpec(a.shape, lambda b, k: (0,) * a.ndim)
    return pl.pallas_call(
        _ssd_kernel,
        grid=(batch, steps),
        in_specs=[pl.BlockSpec((rows, 512), fwd), pl.BlockSpec((rows, LANES), fwd),
                  pl.BlockSpec((rows, 512), bwd), pl.BlockSpec((rows, LANES), bwd),
                  const(dtbias_row), const(alog_row)],
        out_specs=(pl.BlockSpec((rows, 256), fwd), pl.BlockSpec((rows, 256), bwd)),
        out_shape=(jax.ShapeDtypeStruct((n, 256), F32), jax.ShapeDtypeStruct((n, 256), F32)),
        scratch_shapes=[pltpu.VMEM((2, 2, CHUNK, LANES), F32)],
        compiler_params=_params(("parallel", "arbitrary")),
        name="ssd_scan",
    )(xbc, dt, xbc, dt, dtbias_row, alog_row)


def _linear_scan(a, b, reverse):
    n = a.shape[0]
    row = lax.broadcasted_iota(jnp.int32, (n, 1), 0)
    s = 1
    while s < n:
        if s < SUBLANES:
            if reverse:
                ok = row < n - s
                a_sh = jnp.where(ok, pltpu.roll(a, n - s, axis=0), 1.0)
                b_sh = jnp.where(ok, pltpu.roll(b, n - s, axis=0), 0.0)
            else:
                ok = row >= s
                a_sh = jnp.where(ok, pltpu.roll(a, s, axis=0), 1.0)
                b_sh = jnp.where(ok, pltpu.roll(b, s, axis=0), 0.0)
            b = b + a * b_sh
            a = a * a_sh
        elif reverse:
            b = jnp.concatenate([b[:n - s] + a[:n - s] * b[s:], b[n - s:]], axis=0)
            a = jnp.concatenate([a[:n - s] * a[s:], a[n - s:]], axis=0)
        else:
            b = jnp.concatenate([b[:s], b[s:] + a[s:] * b[:n - s]], axis=0)
            a = jnp.concatenate([a[:s], a[s:] * a[:n - s]], axis=0)
        s *= 2
    return a, b


def _lru_kernel(uf_ref, ub_ref, wg_ref, bg_ref, lam_ref, hf_ref, hb_ref, carry_ref):
    k = pl.program_id(1)

    @pl.when(k == 0)
    def _():
        carry_ref[...] = jnp.zeros_like(carry_ref)

    for d, (u_ref, h_ref) in enumerate(((uf_ref, hf_ref), (ub_ref, hb_ref))):
        u = u_ref[...]
        gates = _dot(u.astype(BF16), wg_ref[d]) + bg_ref[d]
        r = jax.nn.sigmoid(gates[:, :LRU_WIDTH])
        ig = jax.nn.sigmoid(gates[:, LRU_WIDTH:])
        log_a = -LRU_C * r * _softplus(-lam_ref[d])
        a = jnp.exp(log_a)
        inp = jnp.sqrt(-jnp.tanh(log_a) * (1.0 + a * a)) * (ig * u)
        a_cum, b_cum = _linear_scan(a, inp, reverse=(d == 1))
        h = b_cum + a_cum * carry_ref[d, 0:1, :]
        h_ref[...] = h
        last = 0 if d == 1 else u.shape[0] - 1
        carry_ref[d, 0:1, :] = h[last:last + 1, :]


def _lru(u, wg, bg, lam, *, batch, seq, ctx_len):
    n = u.shape[0]
    fwd, bwd, steps = _chunk_maps(batch, seq, ctx_len)
    rows = SCAN_CHUNKS * CHUNK
    const = lambda a: pl.BlockSpec(a.shape, lambda b, k: (0,) * a.ndim)
    return pl.pallas_call(
        _lru_kernel,
        grid=(batch, steps),
        in_specs=[pl.BlockSpec((rows, LRU_WIDTH), fwd), pl.BlockSpec((rows, LRU_WIDTH), bwd),
                  const(wg), const(bg), const(lam)],
        out_specs=(pl.BlockSpec((rows, LRU_WIDTH), fwd), pl.BlockSpec((rows, LRU_WIDTH), bwd)),
        out_shape=(jax.ShapeDtypeStruct((n, LRU_WIDTH), F32), jax.ShapeDtypeStruct((n, LRU_WIDTH), F32)),
        scratch_shapes=[pltpu.VMEM((2, SUBLANES, LRU_WIDTH), F32)],
        compiler_params=_params(("parallel", "arbitrary")),
        name="lru_scan",
    )(u, u, wg, bg, lam)


def _stack_heads(q, g):
    qf = q.astype(F32)
    lo = g * LANES
    return jnp.concatenate([qf[:, lo:lo + HEAD_DIM], qf[:, lo + HEAD_DIM:lo + LANES]], axis=0).astype(BF16)


def _value_lanes(g):
    lane = lax.broadcasted_iota(jnp.int32, (1, LANES), 1)
    return (lane < HEAD_DIM) if g == 0 else (lane >= HEAD_DIM)


def _aug_values(v, g):
    return jnp.where(_value_lanes(g), v, jnp.ones_like(v))


def _flash_init(rows, g, sink_pair):
    if sink_pair is None:
        return jnp.full((rows, 1), NEG_INF, F32), jnp.zeros((rows, LANES), F32)
    half = lax.broadcasted_iota(jnp.int32, (rows, 1), 0) < rows // 2
    m = jnp.where(half, sink_pair[0], sink_pair[1]).astype(F32)
    acc = jnp.broadcast_to(jnp.where(_value_lanes(g), 0.0, 1.0), (rows, LANES))
    return m, acc


def _flash_update(state, q2, kt, v_aug, mask=None):
    m, acc = state
    s = _dot(q2, kt)
    if mask is not None:
        s = jnp.where(mask, s, NEG_INF)
    m_new = jnp.maximum(m, jnp.max(s, axis=-1, keepdims=True))
    p = jnp.exp(s - m_new).astype(BF16)
    acc = jnp.exp(m - m_new) * acc + _dot(p, v_aug)
    return m_new, acc


def _flash_finish(states, tq):
    pieces = []
    for g, (_, acc) in enumerate(states):
        den = (1 - g) * HEAD_DIM
        o = acc[:, g * HEAD_DIM:(g + 1) * HEAD_DIM] / acc[:, den:den + 1]
        pieces += [o[:tq], o[tq:]]
    return jnp.concatenate(pieces, axis=1)


def _group_rows(g):
    return slice(g * HEAD_DIM, (g + 1) * HEAD_DIM)


def _dense_attn_kernel(*refs, tq, seg_lens, has_sink):
    refs = list(refs)
    sink_ref = refs.pop(0) if has_sink else None
    q_ref = refs.pop(0)
    o_ref = refs.pop()
    segs = [(refs[2 * i], refs[2 * i + 1], n) for i, n in enumerate(seg_lens)]
    q = q_ref[...]
    q2 = [_stack_heads(q, g) for g in range(2)]
    states = tuple(_flash_init(2 * tq, g, (sink_ref[2 * g], sink_ref[2 * g + 1]) if has_sink else None)
                   for g in range(2))
    for kt_ref, v_ref, n_keys in segs:
        if n_keys <= KV_CHUNK:
            v = v_ref[...]
            states = tuple(_flash_update(states[g], q2[g], kt_ref[_group_rows(g), :], _aug_values(v, g))
                           for g in range(2))
        else:
            def body(c, sts, kt_ref=kt_ref, v_ref=v_ref):
                off = pl.multiple_of(c * KV_CHUNK, KV_CHUNK)
                v = v_ref[pl.ds(off, KV_CHUNK), :]
                return tuple(_flash_update(sts[g], q2[g], kt_ref[_group_rows(g), pl.ds(off, KV_CHUNK)],
                                           _aug_values(v, g)) for g in range(2))
            states = lax.fori_loop(0, n_keys // KV_CHUNK, body, states, unroll=KV_UNROLL)
    o_ref[...] = _flash_finish(states, tq).astype(o_ref.dtype)


def _dense_attn(q, kt, v, sink, *, q_row0, q_len, tq, segs, batch):
    n = q.shape[0]
    qpb = q_len // tq
    q0 = q_row0 // tq
    in_specs, args = [], []
    if sink is not None:
        in_specs.append(pl.BlockSpec(memory_space=pltpu.SMEM))
        args.append(sink)
    in_specs.append(pl.BlockSpec((tq, 256), lambda b, i: (q0 + b * qpb + i, 0)))
    args.append(q)
    for row0, klen in segs:
        k0 = row0 // klen
        in_specs.append(pl.BlockSpec((LANES, klen), lambda b, i, k0=k0: (0, k0 + b)))
        in_specs.append(pl.BlockSpec((klen, LANES), lambda b, i, k0=k0: (k0 + b, 0)))
        args += [kt, v]
    return pl.pallas_call(
        functools.partial(_dense_attn_kernel, tq=tq, seg_lens=tuple(s[1] for s in segs), has_sink=sink is not None),
        grid=(batch, qpb),
        in_specs=in_specs,
        out_specs=pl.BlockSpec((tq, 256), lambda b, i: (b * qpb + i, 0)),
        out_shape=jax.ShapeDtypeStruct((batch * q_len, 256), BF16),
        compiler_params=_params(("parallel", "parallel")),
        name="dense_attn",
    )(*args)


def _window_attn_kernel(sink_ref, q_ref, ktc_ref, vc_ref, ktp_ref, vp_ref, ktm_ref, vm_ref, ktn_ref, vn_ref, o_ref,
                        *, n_tiles):
    n = pl.program_id(1)
    nsub = TQ_WINDOW // CHUNK
    iq = lax.broadcasted_iota(jnp.int32, (2 * CHUNK, CHUNK), 0) & (CHUNK - 1)
    jk = lax.broadcasted_iota(jnp.int32, (2 * CHUNK, CHUNK), 1)
    below = jk >= iq
    above = jk <= iq
    vctx = vc_ref[...]
    for j in range(nsub):
        cols = slice(j * CHUNK, (j + 1) * CHUNK)
        q = q_ref[cols, :]
        states = []
        for g in range(2):
            q2 = _stack_heads(q, g)
            rows = _group_rows(g)
            state = _flash_init(2 * CHUNK, g, (sink_ref[2 * g], sink_ref[2 * g + 1]))
            state = _flash_update(state, q2, ktc_ref[rows, :], _aug_values(vctx, g))
            state = _flash_update(state, q2, ktm_ref[rows, cols], _aug_values(vm_ref[cols, :], g))
            if j > 0:
                prev = slice((j - 1) * CHUNK, j * CHUNK)
                state = _flash_update(state, q2, ktm_ref[rows, prev], _aug_values(vm_ref[prev, :], g), below)
            else:
                state = _flash_update(state, q2, ktp_ref[rows, :], _aug_values(vp_ref[...], g),
                                      jnp.logical_and(below, n > 0))
            if j < nsub - 1:
                nxt = slice((j + 1) * CHUNK, (j + 2) * CHUNK)
                state = _flash_update(state, q2, ktm_ref[rows, nxt], _aug_values(vm_ref[nxt, :], g), above)
            else:
                state = _flash_update(state, q2, ktn_ref[rows, :], _aug_values(vn_ref[...], g),
                                      jnp.logical_and(above, n < n_tiles - 1))
            states.append(state)
        o_ref[cols, :] = _flash_finish(states, CHUNK).astype(o_ref.dtype)


def _window_attn(q, kt, v, sink, *, batch, seq, ctx_len):
    nt = seq // TQ_WINDOW
    nsub = TQ_WINDOW // CHUNK
    nb = seq // CHUNK
    ctx0 = (batch * seq) // ctx_len
    prev = lambda b, n: b * nb + jnp.maximum(n * nsub - 1, 0)
    nxt = lambda b, n: b * nb + jnp.minimum((n + 1) * nsub, nb - 1)
    return pl.pallas_call(
        functools.partial(_window_attn_kernel, n_tiles=nt),
        grid=(batch, nt),
        in_specs=[pl.BlockSpec(memory_space=pltpu.SMEM),
                  pl.BlockSpec((TQ_WINDOW, 256), lambda b, n: (b * nt + n, 0)),
                  pl.BlockSpec((LANES, ctx_len), lambda b, n: (0, ctx0 + b)),
                  pl.BlockSpec((ctx_len, LANES), lambda b, n: (ctx0 + b, 0)),
                  pl.BlockSpec((LANES, CHUNK), lambda b, n: (0, prev(b, n))),
                  pl.BlockSpec((CHUNK, LANES), lambda b, n: (prev(b, n), 0)),
                  pl.BlockSpec((LANES, TQ_WINDOW), lambda b, n: (0, b * nt + n)),
                  pl.BlockSpec((TQ_WINDOW, LANES), lambda b, n: (b * nt + n, 0)),
                  pl.BlockSpec((LANES, CHUNK), lambda b, n: (0, nxt(b, n))),
                  pl.BlockSpec((CHUNK, LANES), lambda b, n: (nxt(b, n), 0))],
        out_specs=pl.BlockSpec((TQ_WINDOW, 256), lambda b, n: (b * nt + n, 0)),
        out_shape=jax.ShapeDtypeStruct((batch * seq, 256), BF16),
        compiler_params=_params(("parallel", "parallel")),
        name="window_attn",
    )(sink, q, kt, v, kt, v, kt, v, kt, v)


def _gelu_tanh(x):
    return 0.5 * x * (1.0 + jnp.tanh(math.sqrt(2.0 / math.pi) * (x + 0.044715 * (x * x * x))))


def _outproj_kernel(x_ref, xc_ref, gate_ref, gpost_ref, oa_ref, od_ref, oac_ref, odc_ref, yf_ref, yb_ref, xs_ref,
                    z_ref, dsk_ref, gn_ref, hf_ref, hb_ref, lg_ref, w_ref, o_ref, *, n_lat, split):
    is_lat = pl.program_id(0) * TM < n_lat
    x = jnp.where(jnp.logical_or(is_lat, not split), x_ref[...], xc_ref[...])
    oa = jnp.where(is_lat, oa_ref[...], oac_ref[...])
    od = jnp.where(is_lat, od_ref[...], odc_ref[...])
    y_ssd = (yf_ref[...] + yb_ref[...] + xs_ref[...] * dsk_ref[...]) * _silu(z_ref[...])
    ob = _rms(y_ssd, gn_ref[...])
    oc = (hf_ref[...] + hb_ref[...]) * _gelu_tanh(lg_ref[...])
    y = (_dot(oa, w_ref[0:256, :]) + _dot(ob.astype(BF16), w_ref[256:512, :])
         + _dot(oc.astype(BF16), w_ref[512:768, :]) + _dot(od, w_ref[768:1024, :]))
    o_ref[...] = x + gate_ref[0] * _rms(y, gpost_ref[...])


def _outproj(x_lat, x_ctx, mod, gpost, oa, od, oa_c, od_c, yf, yb, xbc, z, dsk, gn, hf, hb, lg, w_out,
             *, n_rows, n_lat, seq, batch):
    d = x_lat.shape[1]
    lat_tiles = n_lat // TM
    x_specs, x_args = _row_sources(x_lat, x_ctx, n_lat)
    row = lambda w: pl.BlockSpec((TM, w), lambda i: (i, 0))
    lat = pl.BlockSpec((TM, 256), lambda i: (jnp.minimum(i, lat_tiles - 1), 0))
    ctx = pl.BlockSpec((TM, 256), lambda i: (jnp.maximum(i - lat_tiles, 0), 0))
    const = lambda a: pl.BlockSpec(a.shape, lambda i: (0,) * a.ndim)
    return pl.pallas_call(
        functools.partial(_outproj_kernel, n_lat=n_lat, split=x_ctx is not None),
        grid=(n_rows // TM,),
        in_specs=[x_specs[0], x_specs[3], pl.BlockSpec((1, 1, d), _mod_spec(2, n_lat, seq, batch, TM)), const(gpost),
                  lat, lat, ctx, ctx, row(256), row(256), row(256), row(256), const(dsk), const(gn),
                  row(256), row(256), row(256), const(w_out)],
        out_specs=row(d),
        out_shape=jax.ShapeDtypeStruct((n_rows, d), F32),
        compiler_params=_params(("parallel",)),
        name="outproj",
    )(x_args[0], x_args[3], mod, gpost, oa, od, oa_c, od_c, yf, yb, xbc, z, dsk, gn, hf, hb, lg, w_out)


def _ceil_seg(c):
    return jnp.floor((c + (SEG_ALIGN - 1)) * (1.0 / SEG_ALIGN)) * SEG_ALIGN


def _router_kernel(x_ref, shift_ref, scale_ref, gpre_ref, rwt_ref, rb_ref, hb_ref, ld_ref, wk_ref, tab_ref):
    h = _rms(x_ref[...], gpre_ref[...])
    h = h * (1.0 + scale_ref[0]) + shift_ref[0]
    hb = h.astype(BF16)
    hb_ref[...] = hb

    scores = jax.nn.sigmoid(_dot_nt(rwt_ref[...], hb))
    biased = scores + rb_ref[...]
    gsz = N_EXPERTS // N_EXPERT_GROUPS
    sub = lax.broadcasted_iota(jnp.int32, (gsz, TM), 0)
    blocks, gscore = [], []
    for g in range(N_EXPERT_GROUPS):
        blk = biased[g * gsz:(g + 1) * gsz, :]
        m1 = jnp.max(blk, axis=0, keepdims=True)
        first = jnp.min(jnp.where(blk == m1, sub, gsz), axis=0, keepdims=True)
        m2 = jnp.max(jnp.where(sub == first, -jnp.inf, blk), axis=0, keepdims=True)
        blocks.append(blk)
        gscore.append(m1 + m2)
    masked = []
    for g in range(N_EXPERT_GROUPS):
        rank = jnp.zeros((1, TM), F32)
        for g2 in range(N_EXPERT_GROUPS):
            if g2 == g:
                continue
            beats = (gscore[g2] > gscore[g]) | ((gscore[g2] == gscore[g]) if g2 < g else False)
            rank = rank + jnp.where(beats, 1.0, 0.0)
        masked.append(jnp.where(rank < TOPK_GROUPS, blocks[g], -jnp.inf))
    vals = jnp.concatenate(masked, axis=0)
    eidx = lax.broadcasted_iota(jnp.int32, (N_EXPERTS, TM), 0)
    self32 = jnp.zeros((N_EXPERTS, TM), F32)
    rest = vals
    for _ in range(TOP_K):
        top = jnp.max(rest, axis=0, keepdims=True)
        first = jnp.min(jnp.where(rest == top, eidx, N_EXPERTS), axis=0, keepdims=True)
        hit = eidx == first
        self32 = jnp.where(hit, 1.0, self32)
        rest = jnp.where(hit, -jnp.inf, rest)
    sel = self32 > 0.5
    picked = jnp.where(sel, scores, 0.0)
    wdense = picked / jnp.sum(picked, axis=0, keepdims=True) * ROUTED_SCALE

    tr = lax.broadcasted_iota(jnp.int32, (TM, TM), 0)
    tc = lax.broadcasted_iota(jnp.int32, (TM, TM), 1)
    before = jnp.where(tr < tc, 1.0, 0.0).astype(BF16)
    selb = self32.astype(BF16)
    pos = _dot(selb, before)
    er = lax.broadcasted_iota(jnp.int32, (N_EXPERTS, N_EXPERTS), 0)
    ec = lax.broadcasted_iota(jnp.int32, (N_EXPERTS, N_EXPERTS), 1)
    lower = jnp.where(ec < er, 1.0, 0.0).astype(BF16)
    upper = jnp.where(er < ec, 1.0, 0.0).astype(BF16)
    ksel = _dot(lower, selb)
    cnt_col = _ceil_seg(jnp.sum(self32, axis=1, keepdims=True))
    loc_col = _dot3_left(lower, jnp.broadcast_to(cnt_col, (N_EXPERTS, LANES)))[:, 0:1]
    cnt_row = _ceil_seg(_dot_nt(jnp.ones((SUBLANES, TM), BF16), selb))
    loc_row = _dot3(cnt_row, upper)
    tab_ref[...] = jnp.concatenate([cnt_row, loc_row], axis=1).astype(jnp.int32)

    r8 = lax.broadcasted_iota(jnp.int32, (TOP_K, TM), 0)
    ld = jnp.zeros((TOP_K, TM), F32)
    wk = jnp.zeros((TOP_K, TM), F32)
    stage_row = pos + loc_col
    for k in range(TOP_K):
        one = sel & (ksel == float(k))
        ld = jnp.where(r8 == k, jnp.sum(jnp.where(one, stage_row, 0.0), axis=0, keepdims=True), ld)
        wk = jnp.where(r8 == k, jnp.sum(jnp.where(one, wdense, 0.0), axis=0, keepdims=True), wk)
    ld_ref[...] = ld.astype(jnp.int32)
    wk_ref[...] = wk


def _router(xu, mod, gpre, rwt, rb, *, n_rows, n_lat, seq, batch):
    d = xu.shape[1]
    row = lambda w: pl.BlockSpec((TM, w), lambda i: (i, 0))
    col = pl.BlockSpec((TOP_K, TM), lambda i: (0, i))
    const = lambda a: pl.BlockSpec(a.shape, lambda i: (0,) * a.ndim)
    return pl.pallas_call(
        _router_kernel,
        grid=(n_rows // TM,),
        in_specs=[row(d), pl.BlockSpec((1, 1, d), _mod_spec(3, n_lat, seq, batch, TM)),
                  pl.BlockSpec((1, 1, d), _mod_spec(4, n_lat, seq, batch, TM)),
                  const(gpre), const(rwt), const(rb)],
        out_specs=(row(d), col, col, pl.BlockSpec((SUBLANES, 2 * N_EXPERTS), lambda i: (i, 0))),
        out_shape=(jax.ShapeDtypeStruct((n_rows, d), BF16),
                   jax.ShapeDtypeStruct((TOP_K, n_rows), jnp.int32),
                   jax.ShapeDtypeStruct((TOP_K, n_rows), F32),
                   jax.ShapeDtypeStruct((n_rows // TM * SUBLANES, 2 * N_EXPERTS), jnp.int32)),
        compiler_params=_params(("parallel",)),
        name="router",
    )(xu, mod, mod, gpre, rwt, rb)


def _pow2_pieces(limit):
    bits, b = [], limit
    while b >= SEG_ALIGN:
        bits.append(b)
        b //= 2
    return bits


def _copy_pieces(n, src_ref, src0, dst_ref, dst0, sem, limit, wait, same_src=False):
    for bit in _pow2_pieces(limit):
        @pl.when((n & bit) != 0)
        def _():
            off = n & ~(2 * bit - 1)
            cp = pltpu.make_async_copy(src_ref.at[pl.ds(pl.multiple_of(src0 + (0 if same_src else off), SEG_ALIGN),
                                                          bit)],
                                       dst_ref.at[pl.ds(pl.multiple_of(dst0 + off, SEG_ALIGN), bit)], sem)
            cp.wait() if wait else cp.start()


N_PIECE_TABS = 7


def _piece_copies(tile, tabs, stage_ref, slots_ref, sem, to_slots, wait):
    nbig_ref, nsmall_ref, _, bsrc_ref, bdst_ref, ssrc_ref, sdst_ref = tabs
    for rows, n_ref, a_ref, b_ref, cap in ((BIG_PIECE, nbig_ref, bsrc_ref, bdst_ref, BIG_MAX),
                                           (SEG_ALIGN, nsmall_ref, ssrc_ref, sdst_ref, SMALL_MAX)):
        def body(p, c, rows=rows, a_ref=a_ref, b_ref=b_ref, cap=cap):
            src = stage_ref.at[pl.ds(pl.multiple_of(a_ref[tile * cap + p], SEG_ALIGN), rows)]
            dst = slots_ref.at[pl.ds(pl.multiple_of(b_ref[tile * cap + p], SEG_ALIGN), rows)]
            cp = pltpu.make_async_copy(src, dst, sem) if to_slots else pltpu.make_async_copy(dst, src, sem)
            cp.wait() if wait else cp.start()
            return c
        lax.fori_loop(0, n_ref[tile], body, 0)


def _used_blocks(tile, tabs):
    return (tabs[2][tile] + TM - 1) // TM


def _for_used_blocks(used, body):
    for b in range(TOP_K):
        body(b)
    for b in range(TOP_K, STAGE_ROWS // TM):
        @pl.when(b < used)
        def _():
            body(b)


def _stage_rows_iota():
    return lax.broadcasted_iota(jnp.int32, (TM // 2, TM), 0).astype(F32).astype(BF16)


def _pick_matrix(ld, base, vals, jrow):
    rel = (ld - base).astype(F32)
    rel = jnp.where(jnp.logical_and(rel >= 0.0, rel < TM // 2), rel, -1.0).astype(BF16)
    out = jnp.zeros((TM // 2, TM), BF16)
    for k in range(TOP_K):
        out = jnp.where(rel[k:k + 1, :] == jrow, vals[k:k + 1, :], out)
    return out


def _dispatch_kernel(*refs):
    tabs = refs[:N_PIECE_TABS]
    pstart_ref, npad_ref, hb_ref, ld_ref, xs_ref, stage, zbuf, sem, zsem = refs[N_PIECE_TABS:]
    i = pl.program_id(0)

    @pl.when(i == 0)
    def _():
        zbuf[...] = jnp.zeros_like(zbuf)
        for wait in (False, True):
            def body(e, c, wait=wait):
                _copy_pieces(npad_ref[e], zbuf, 0, xs_ref, pstart_ref[e], zsem, BM_EXPERT // 2, wait, same_src=True)
                return c
            lax.fori_loop(0, N_EXPERTS, body, 0)

    ld = ld_ref[...]
    hb = hb_ref[...]
    jrow = _stage_rows_iota()
    ones = jnp.ones((TOP_K, TM), BF16)

    cur = stage.at[i & 1]

    def block(b):
        for half in range(2):
            base = b * TM + half * (TM // 2)
            cur[base:base + TM // 2, :] = _dot(_pick_matrix(ld, base, ones, jrow), hb).astype(BF16)

    _for_used_blocks(_used_blocks(i, tabs), block)

    @pl.when(i > 0)
    def _():
        _piece_copies(i - 1, tabs, stage.at[(i - 1) & 1], xs_ref, sem, True, True)

    _piece_copies(i, tabs, cur, xs_ref, sem, True, False)

    @pl.when(i == pl.num_programs(0) - 1)
    def _():
        _piece_copies(i, tabs, cur, xs_ref, sem, True, True)


def _dispatch(tabs, pad_start, n_pad, hb, ld, n_slots):
    n, d = hb.shape
    grid_spec = pltpu.PrefetchScalarGridSpec(
        num_scalar_prefetch=N_PIECE_TABS + 2,
        grid=(n // TM,),
        in_specs=[pl.BlockSpec((TM, d), lambda i, *_: (i, 0)),
                  pl.BlockSpec((TOP_K, TM), lambda i, *_: (0, i))],
        out_specs=pl.BlockSpec(memory_space=pl.ANY),
        scratch_shapes=[pltpu.VMEM((2, STAGE_ROWS, d), BF16), pltpu.VMEM((BM_EXPERT // 2, d), BF16),
                        pltpu.SemaphoreType.DMA(()), pltpu.SemaphoreType.DMA(())],
    )
    return pl.pallas_call(
        _dispatch_kernel,
        grid_spec=grid_spec,
        out_shape=jax.ShapeDtypeStruct((n_slots, d), BF16),
        compiler_params=_params(("arbitrary",)),
        name="moe_dispatch",
    )(*tabs, pad_start, n_pad, hb, ld)


def _expert_kernel(be_ref, na_ref, nxt_ref, slot_ref, xs_hbm, wg_hbm, wu_hbm, wd_hbm, ys_ref,
                   xbuf, wg_raw, wu_raw, wd_raw, wgub, wdb, xsem, wsem, *, layer):
    i = pl.program_id(0)
    n_act = na_ref[0]

    def weight_copies(e, slot):
        return [pltpu.make_async_copy(src.at[layer, e], dst.at[slot], wsem.at[slot, j])
                for j, (src, dst) in enumerate(((wg_hbm, wg_raw), (wu_hbm, wu_raw), (wd_hbm, wd_raw)))]

    def rows_copy(blk):
        buf = lax.rem(blk, X_BUFFERS)
        return pltpu.make_async_copy(xs_hbm.at[pl.ds(pl.multiple_of(blk * BM_EXPERT, BM_EXPERT), BM_EXPERT)],
                                     xbuf.at[buf], xsem.at[buf])

    @pl.when(i < n_act)
    def _():
        e, slot = be_ref[i], slot_ref[i]

        @pl.when(i == 0)
        def _():
            for cp in weight_copies(e, slot):
                cp.start()
            for j in range(X_BUFFERS - 1):
                @pl.when(j < n_act)
                def _():
                    rows_copy(j).start()

        @pl.when(i + (X_BUFFERS - 1) < n_act)
        def _():
            rows_copy(i + (X_BUFFERS - 1)).start()

        @pl.when(jnp.logical_or(i == 0, e != be_ref[jnp.maximum(i - 1, 0)]))
        def _():
            for cp in weight_copies(e, slot):
                cp.wait()
            wgub[:, :EXPERT_HIDDEN] = wg_raw[slot].astype(BF16)
            wgub[:, EXPERT_HIDDEN:] = wu_raw[slot].astype(BF16)
            wdb[...] = wd_raw[slot].astype(BF16)

            @pl.when(nxt_ref[i] != e)
            def _():
                for cp in weight_copies(nxt_ref[i], 1 - slot):
                    cp.start()

        rows_copy(i).wait()
        gu = _dot(xbuf[lax.rem(i, X_BUFFERS)], wgub[...])
        hid = _silu(gu[:, :EXPERT_HIDDEN]) * gu[:, EXPERT_HIDDEN:]
        ys_ref[...] = _dot(hid.astype(BF16), wdb[...]).astype(ys_ref.dtype)


def _experts(block_e, n_active, next_e, w_slot, xs, wg, wu, wd, layer):
    n_slots, d = xs.shape
    nb = n_slots // BM_EXPERT
    tiles = pl.BlockSpec((BM_EXPERT, d), lambda i, be, na, *_: (jnp.minimum(i, na[0] - 1), 0))
    anywhere = pl.BlockSpec(memory_space=pl.ANY)
    grid_spec = pltpu.PrefetchScalarGridSpec(
        num_scalar_prefetch=4,
        grid=(nb,),
        in_specs=[anywhere, anywhere, anywhere, anywhere],
        out_specs=tiles,
        scratch_shapes=[pltpu.VMEM((X_BUFFERS, BM_EXPERT, d), BF16),
                        pltpu.VMEM((2, d, EXPERT_HIDDEN), F32), pltpu.VMEM((2, d, EXPERT_HIDDEN), F32),
                        pltpu.VMEM((2, EXPERT_HIDDEN, d), F32),
                        pltpu.VMEM((d, 2 * EXPERT_HIDDEN), BF16),
                        pltpu.VMEM((EXPERT_HIDDEN, d), BF16),
                        pltpu.SemaphoreType.DMA((X_BUFFERS,)), pltpu.SemaphoreType.DMA((2, 3))],
    )
    return pl.pallas_call(
        functools.partial(_expert_kernel, layer=layer),
        grid_spec=grid_spec,
        out_shape=jax.ShapeDtypeStruct((n_slots, d), BF16),
        compiler_params=_params(("arbitrary",)),
        name="moe_experts",
    )(block_e, n_active, next_e, w_slot, xs, wg, wu, wd)


def _combine_kernel(*refs):
    tabs = refs[:N_PIECE_TABS]
    (ys_ref, ld_ref, wk_ref, hb_ref, x_ref, gate_ref, gpost_ref, sg_ref, su_ref, sd_ref, o_ref,
     stage, acc_ref, sem) = refs[N_PIECE_TABS:]
    i = pl.program_id(0)

    @pl.when(i == 0)
    def _():
        stage[...] = jnp.zeros_like(stage)
        _piece_copies(0, tabs, stage.at[0], ys_ref, sem, False, False)

    cur = stage.at[i & 1]
    _piece_copies(i, tabs, cur, ys_ref, sem, False, True)

    @pl.when(i + 1 < pl.num_programs(0))
    def _():
        _piece_copies(i + 1, tabs, stage.at[(i + 1) & 1], ys_ref, sem, False, False)

    hb = hb_ref[...]
    acc_ref[...] = _dot((_silu(_dot(hb, sg_ref[...])) * _dot(hb, su_ref[...])).astype(BF16), sd_ref[...])

    ld = ld_ref[...]
    wkb = wk_ref[...].astype(BF16)
    jrow = _stage_rows_iota()

    def block(b):
        for half in range(2):
            base = b * TM + half * (TM // 2)
            weights = _pick_matrix(ld, base, wkb, jrow)
            acc_ref[...] += lax.dot_general(weights, cur[base:base + TM // 2, :], (((0,), (0,)), ((), ())),
                                            preferred_element_type=F32)

    _for_used_blocks(_used_blocks(i, tabs), block)
    o_ref[...] = x_ref[...] + gate_ref[0] * _rms(acc_ref[...], gpost_ref[...])


def _combine(tabs, ys, ld, wk, hb, xu, mod, gpost, sg, su, sd, *, n_rows, n_lat, seq, batch):
    d = xu.shape[1]
    row = lambda w: pl.BlockSpec((TM, w), lambda i, *_: (i, 0))
    col = pl.BlockSpec((TOP_K, TM), lambda i, *_: (0, i))
    const = lambda a: pl.BlockSpec(a.shape, lambda i, *_: (0,) * a.ndim)
    mod_map = _mod_spec(5, n_lat, seq, batch, TM)
    grid_spec = pltpu.PrefetchScalarGridSpec(
        num_scalar_prefetch=N_PIECE_TABS,
        grid=(n_rows // TM,),
        in_specs=[pl.BlockSpec(memory_space=pl.ANY), col, col, row(d), row(d),
                  pl.BlockSpec((1, 1, d), lambda i, *_: mod_map(i)),
                  const(gpost), const(sg), const(su), const(sd)],
        out_specs=row(d),
        scratch_shapes=[pltpu.VMEM((2, STAGE_ROWS, d), BF16), pltpu.VMEM((TM, d), F32), pltpu.SemaphoreType.DMA(())],
    )
    return pl.pallas_call(
        _combine_kernel,
        grid_spec=grid_spec,
        out_shape=jax.ShapeDtypeStruct((n_rows, d), F32),
        compiler_params=_params(("arbitrary",)),
        name="moe_combine",
    )(*tabs, ys, ld, wk, hb, xu, mod, gpost, sg, su, sd)


def _deinterleave(w):
    cols = w.shape[-1]
    perm = jnp.concatenate([jnp.arange(0, HEAD_DIM, 2), jnp.arange(1, HEAD_DIM, 2)])
    idx = (jnp.arange(cols // HEAD_DIM)[:, None] * HEAD_DIM + perm[None, :]).reshape(-1)
    return w[..., idx]


def _pad_in_proj(w_in):
    d = w_in.shape[0]
    o = 0
    parts = {}
    for name, width in (("qa", 256), ("ka", 128), ("va", 128), ("z", 256), ("xs", 256), ("bm", 128), ("cm", 128),
                        ("dtf", 4), ("dtb", 4), ("lx", 256), ("lg", 256), ("qd", 256), ("kd", 128), ("vd", 128)):
        parts[name] = w_in[:, o:o + width]
        o += width
    dt = jnp.concatenate([parts["dtf"], parts["dtb"], jnp.zeros((d, LANES - 8), w_in.dtype)], axis=1)
    cols = [_deinterleave(parts["qa"]), _deinterleave(parts["ka"]), parts["va"],
            _deinterleave(parts["qd"]), _deinterleave(parts["kd"]), parts["vd"],
            parts["z"], parts["xs"], parts["bm"], parts["cm"], dt, parts["lx"], parts["lg"]]
    return jnp.concatenate(cols, axis=1).astype(BF16)


def _rope_tables(seq):
    t = jnp.arange(seq)
    rowp = (t // GRID_W).astype(F32)
    colp = (t % GRID_W).astype(F32)
    axis_dim = HEAD_DIM // 2
    inv_freq = ROPE_THETA ** (-jnp.arange(0, axis_dim, 2, dtype=F32) / axis_dim)
    ang = jnp.concatenate([rowp[:, None] * inv_freq, colp[:, None] * inv_freq], axis=-1)
    cos, sin = jnp.cos(ang), jnp.sin(ang)
    cos_h = jnp.concatenate([cos, cos], axis=-1)
    sin_h = jnp.concatenate([-sin, sin], axis=-1)
    return jnp.tile(cos_h, (1, 4)), jnp.tile(sin_h, (1, 4))


def _block_diag(w):
    nb, bd, _ = w.shape
    eye = jnp.eye(nb, dtype=w.dtype)
    return (eye[:, None, :, None] * w[:, :, None, :]).reshape(nb * bd, nb * bd)


def _piece_table(counts, cap, stage0, slot0, rows, ids):
    ends = jnp.cumsum(counts, axis=1)
    q = jnp.arange(cap, dtype=jnp.int32)
    owner = jnp.sum((ends[:, None, :] <= q[None, :, None]).astype(jnp.int32), axis=-1)
    mine = owner[:, :, None] == ids
    pick = lambda v: jnp.sum(jnp.where(mine, v[:, None, :], 0), axis=-1)
    step = rows * (q[None, :] - pick(ends - counts))
    return (pick(stage0) + step).reshape(-1), (pick(slot0) + step).reshape(-1)


def _lane_row(fwd, bwd):
    return jnp.concatenate([fwd, bwd, jnp.zeros((LANES - 8,), F32)]).reshape(1, LANES)


def kernel(x, c, ctx, c_ctx, w_ada, b_ada, g_mix_pre, g_mix_post, g_ffn_pre, g_ffn_post, w_in, w_out, a_sink,
           ssd_conv_w, ssd_conv_b, ssd_dt_bias, ssd_a_log, ssd_d, ssd_norm, lru_conv_w, lru_conv_b, lru_w_a,
           lru_b_a, lru_w_i, lru_b_i, lru_lambda, d_q_norm, d_k_norm, router_w, router_bias, exp_w_gate,
           exp_w_up, exp_w_down, sh_w_gate, sh_w_up, sh_w_down):
    batch, seq, d = x.shape
    ctx_len = ctx.shape[1]
    depth = w_ada.shape[0]
    n_lat = batch * seq
    n_ctx = batch * ctx_len
    n_all = n_lat + n_ctx
    assert seq % TM == 0 and n_ctx % TM == 0 and ctx_len >= LRU_CONV
    assert seq % TQ_WINDOW == 0 and seq % (SCAN_CHUNKS * CHUNK) == 0 and ctx_len % (SCAN_CHUNKS * CHUNK) == 0
    assert ctx_len <= KV_CHUNK and seq % KV_CHUNK == 0 and seq % TQ_GLOBAL == 0 and batch + 1 <= SUBLANES

    xu, xu_ctx = x.reshape(n_lat, d), ctx.reshape(n_ctx, d)
    cin = jnp.concatenate([c, c_ctx[None, :], jnp.zeros((SUBLANES - batch - 1, d), F32)], axis=0)
    mod_all = _adaln(cin, w_ada, b_ada)
    cos_t, sin_t = _rope_tables(seq)
    hm = jnp.kron(jnp.eye(4, dtype=F32), jnp.full((HEAD_DIM, HEAD_DIM), 1.0 / HEAD_DIM, F32)).astype(BF16)

    for l in range(depth):
        with_ctx = l < depth - 1
        mod = mod_all[l].reshape(SUBLANES * 6, 1, d)
        gq = jnp.tile(_deinterleave(d_q_norm[l]), 4).reshape(1, 256)
        gk = jnp.tile(_deinterleave(d_k_norm[l]), 2).reshape(1, LANES)
        qa, kat, va, qd, kdt, vd, z, xbc, dt, lu, lg = _inproj(
            xu, xu_ctx, mod, g_mix_pre[l].reshape(1, d), _pad_in_proj(w_in[l]), cos_t, sin_t, gq, gk, hm,
            ssd_conv_w[l], ssd_conv_b[l].reshape(1, -1), lru_conv_w[l], lru_conv_b[l].reshape(1, -1),
            n_lat=n_lat, seq=seq, ctx_len=ctx_len, batch=batch)
        yf, yb = _ssd(xbc, dt, _lane_row(ssd_dt_bias[l, 0], ssd_dt_bias[l, 1]),
                      _lane_row(ssd_a_log[l, 0], ssd_a_log[l, 1]), batch=batch, seq=seq, ctx_len=ctx_len)
        wg = jnp.stack([jnp.concatenate([_block_diag(lru_w_a[l, dd]), _block_diag(lru_w_i[l, dd])], axis=1)
                        for dd in range(2)]).astype(BF16)
        bg = jnp.concatenate([lru_b_a[l], lru_b_i[l]], axis=1).reshape(2, 1, 2 * LRU_WIDTH)
        hf, hb = _lru(lu, wg, bg, lru_lambda[l].reshape(2, 1, LRU_WIDTH), batch=batch, seq=seq, ctx_len=ctx_len)

        oa = _window_attn(qa, kat, va, a_sink[l], batch=batch, seq=seq, ctx_len=ctx_len)
        od = _dense_attn(qd, kdt, vd, None, q_row0=0, q_len=seq, tq=TQ_GLOBAL,
                         segs=[(n_lat, ctx_len), (0, seq)], batch=batch)
        if with_ctx:
            oa_c = _dense_attn(qa, kat, va, a_sink[l], q_row0=n_lat, q_len=ctx_len, tq=ctx_len,
                               segs=[(n_lat, ctx_len)], batch=batch)
            od_c = _dense_attn(qd, kdt, vd, None, q_row0=n_lat, q_len=ctx_len, tq=ctx_len,
                               segs=[(n_lat, ctx_len)], batch=batch)
        else:
            oa_c, od_c = oa, od
        n_rows = n_all if with_ctx else n_lat

        dsk = jnp.repeat(ssd_d[l], HEAD_DIM).reshape(1, 256)
        xu_mid = _outproj(xu, xu_ctx, mod, g_mix_post[l].reshape(1, d), oa, od, oa_c, od_c, yf, yb, xbc, z, dsk,
                          ssd_norm[l].reshape(1, 256), hf, hb, lg, w_out[l].astype(BF16),
                          n_rows=n_rows, n_lat=n_lat, seq=seq, batch=batch)

        hb_ffn, ld, wk, tab = _router(xu_mid, mod, g_ffn_pre[l].reshape(1, d), router_w[l].T.astype(BF16),
                                      router_bias[l].reshape(N_EXPERTS, 1), n_rows=n_rows, n_lat=n_lat,
                                      seq=seq, batch=batch)
        n_tiles = n_rows // TM
        tab = tab.reshape(n_tiles, SUBLANES, 2 * N_EXPERTS)[:, 0, :]
        seg_cnt, seg_loc = tab[:, :N_EXPERTS], tab[:, N_EXPERTS:]
        counts = jnp.sum(seg_cnt, axis=0)
        padded = (counts + BM_EXPERT - 1) // BM_EXPERT * BM_EXPERT
        padded_end = jnp.cumsum(padded)
        offs = padded_end - padded
        seg_off = offs[None, :] + jnp.cumsum(seg_cnt, axis=0) - seg_cnt
        n_blocks = (n_rows * TOP_K + n_tiles * N_EXPERTS * SEG_ALIGN) // BM_EXPERT + N_EXPERTS
        n_active = (padded_end[-1] // BM_EXPERT).astype(jnp.int32).reshape(1)
        block_start = jnp.arange(n_blocks, dtype=jnp.int32) * BM_EXPERT
        block_e = jnp.minimum(jnp.sum((padded_end[None, :] <= block_start[:, None]).astype(jnp.int32), axis=1),
                              N_EXPERTS - 1)
        ids = jnp.arange(N_EXPERTS, dtype=jnp.int32)
        n_big = seg_cnt // BIG_PIECE
        n_small = (seg_cnt % BIG_PIECE) // SEG_ALIGN
        tabs = (jnp.sum(n_big, axis=1), jnp.sum(n_small, axis=1), seg_loc[:, -1] + seg_cnt[:, -1],
                *_piece_table(n_big, BIG_MAX, seg_loc, seg_off, BIG_PIECE, ids),
                *_piece_table(n_small, SMALL_MAX, seg_loc + n_big * BIG_PIECE, seg_off + n_big * BIG_PIECE,
                              SEG_ALIGN, ids))
        xs = _dispatch(tabs, offs + counts, padded - counts, hb_ffn, ld, n_blocks * BM_EXPERT)
        has_rows = padded > 0
        later = jnp.logical_and(ids[None, :] > ids[:, None], has_rows[None, :])
        nxt_of = jnp.min(jnp.where(later, ids[None, :], N_EXPERTS), axis=1)
        nxt_of = jnp.where(nxt_of == N_EXPERTS, ids, nxt_of)
        slot_of = (jnp.cumsum(has_rows.astype(jnp.int32)) - 1) & 1
        own = block_e[:, None] == ids[None, :]
        next_e = jnp.sum(jnp.where(own, nxt_of[None, :], 0), axis=1)
        w_slot = jnp.sum(jnp.where(own, slot_of[None, :], 0), axis=1)
        ys = _experts(block_e, n_active, next_e, w_slot, xs, exp_w_gate, exp_w_up, exp_w_down, l)
        xu = _combine(tabs, ys, ld, wk, hb_ffn, xu_mid, mod, g_ffn_post[l].reshape(1, d), sh_w_gate[l].astype(BF16),
                      sh_w_up[l].astype(BF16), sh_w_down[l].astype(BF16),
                      n_rows=n_rows, n_lat=n_lat, seq=seq, batch=batch)
        xu_ctx = None
    return xu[:n_lat].reshape(batch, seq, d)
```

```python
import functools
import math

import jax
import jax.numpy as jnp
from jax import lax
from jax.experimental import pallas as pl
from jax.experimental.pallas import tpu as pltpu

F32 = jnp.float32
BF16 = jnp.bfloat16

HEAD_DIM = 64
GRID_W = 64
ROPE_THETA = 10000.0
NORM_EPS = 1e-6
NEG_INF = -1e30
A_HEADS, A_KV_HEADS, WINDOW = 4, 2, 128
SSD_HEADS, SSD_GROUPS, SSD_STATE, SSD_CONV = 4, 2, 64, 4
LRU_WIDTH, LRU_BLOCKS, LRU_CONV, LRU_C = 256, 4, 4, 8.0
D_HEADS, D_KV_HEADS = 4, 2
N_EXPERTS, N_EXPERT_GROUPS, TOPK_GROUPS, TOP_K = 64, 8, 4, 8
EXPERT_HIDDEN, SHARED_HIDDEN = 256, 256
ROUTED_SCALE = 2.5

LANES = 128
SUBLANES = 8

TM = 512
CHUNK = 128
SCAN_CHUNKS = 2
TQ_GLOBAL = 256
TQ_WINDOW = 512
KV_CHUNK = 256
KV_UNROLL = 16
BM_EXPERT = 512
X_BUFFERS = 3
SEG_ALIGN = 16
STAGE_ROWS = TM * TOP_K + N_EXPERTS * SEG_ALIGN
BIG_PIECE = 64
BIG_MAX = STAGE_ROWS // BIG_PIECE
SMALL_MAX = N_EXPERTS * (BIG_PIECE // SEG_ALIGN - 1)
VMEM_LIMIT = 48 * 1024 * 1024

C_QA, C_KA, C_VA = 0, 256, 384
C_QD, C_KD, C_VD = 512, 768, 896
C_Z, C_XBC, C_DT = 1024, 1280, 1792
C_LX, C_LG = 1920, 2176
NP_IN = 2432


def _dot(a, b):
    return jnp.dot(a, b, preferred_element_type=F32)


def _dot_nt(a, b):
    return lax.dot_general(a, b, (((1,), (1,)), ((), ())), preferred_element_type=F32)


def _dot3(a, b):
    a1 = a.astype(BF16)
    r1 = a - a1.astype(F32)
    a2 = r1.astype(BF16)
    a3 = (r1 - a2.astype(F32)).astype(BF16)
    return _dot(a1, b) + _dot(a2, b) + _dot(a3, b)


def _dot3_left(a, b):
    b1 = b.astype(BF16)
    r1 = b - b1.astype(F32)
    b2 = r1.astype(BF16)
    b3 = (r1 - b2.astype(F32)).astype(BF16)
    return _dot(a, b1) + _dot(a, b2) + _dot(a, b3)


def _silu(x):
    return x * jax.nn.sigmoid(x)


def _softplus(x):
    return jnp.maximum(x, 0.0) + jnp.log1p(jnp.exp(-jnp.abs(x)))


def _rms(x, gain):
    return x * lax.rsqrt(jnp.mean(x * x, axis=-1, keepdims=True) + NORM_EPS) * gain


def _params(sem=None):
    return pltpu.CompilerParams(dimension_semantics=sem, vmem_limit_bytes=VMEM_LIMIT)


def _adaln_kernel(c_ref, w_ref, b_ref, o_ref):
    s = _silu(c_ref[...])
    o_ref[0] = _dot(s.astype(BF16), w_ref[0].astype(BF16)) + b_ref[0]


def _adaln(cin, w_ada, b_ada):
    depth, d, n6 = w_ada.shape
    tn = 1024
    return pl.pallas_call(
        _adaln_kernel,
        grid=(depth, n6 // tn),
        in_specs=[pl.BlockSpec((SUBLANES, d), lambda l, j: (0, 0)),
                  pl.BlockSpec((1, d, tn), lambda l, j: (l, 0, j)),
                  pl.BlockSpec((1, 1, tn), lambda l, j: (l, 0, j))],
        out_specs=pl.BlockSpec((1, SUBLANES, tn), lambda l, j: (l, 0, j)),
        out_shape=jax.ShapeDtypeStruct((depth, SUBLANES, n6), F32),
        compiler_params=_params(("parallel", "parallel")),
        name="adaln",
    )(cin, w_ada, b_ada.reshape(depth, 1, n6))


def _swap_halves(t):
    w = t.shape[1]
    lane = lax.broadcasted_iota(jnp.int32, (1, w), 1)
    first = (lane & 32) == 0
    return jnp.where(first, pltpu.roll(t, w - 32, axis=1), pltpu.roll(t, 32, axis=1))


def _inproj_kernel(x_ref, xp_ref, xn_ref, xc_ref, xcp_ref, xcn_ref, shift_ref, scale_ref, gpre_ref, w_ref,
                   cos_ref, sin_ref, gq_ref, gk_ref, hm_ref, ws_ref, bs_ref, wl_ref, bl_ref,
                   qa_ref, kat_ref, va_ref, qd_ref, kdt_ref, vd_ref, z_ref, xbc_ref, dt_ref, lu_ref, lg_ref,
                   *, n_lat, seq, ctx_len, split):
    i = pl.program_id(0)
    is_lat = i * TM < n_lat
    first = jnp.logical_or(is_lat, not split)

    def pre(a_ref, b_ref):
        x = jnp.where(first, a_ref[...], b_ref[...])
        return _rms(x, gpre_ref[...]) * (1.0 + scale_ref[0]) + shift_ref[0]

    h = pre(x_ref, xc_ref)
    hb = h.astype(BF16)
    hb_ext = jnp.concatenate([pre(xp_ref, xcp_ref), h, pre(xn_ref, xcn_ref)], axis=0).astype(BF16)

    def sec(a, b):
        return _dot(hb, w_ref[:, a:b])

    row = lax.broadcasted_iota(jnp.int32, (TM, 1), 0)
    pos = jnp.where(is_lat, lax.rem(i * TM + row, seq), lax.rem(i * TM - n_lat + row, ctx_len))
    slen = jnp.where(is_lat, seq, ctx_len)

    def conv(a, b, w, bias):
        ext = _dot(hb_ext, w_ref[:, a:b])
        prev, x, nxt = ext[0:SUBLANES], ext[SUBLANES:SUBLANES + TM], ext[SUBLANES + TM:]
        xm1 = jnp.where(row == 0, prev[7:8, :], pltpu.roll(x, 1, axis=0))
        xm2 = jnp.where(row == 0, prev[6:7, :], jnp.where(row == 1, prev[7:8, :], pltpu.roll(x, 2, axis=0)))
        xp1 = jnp.where(row == TM - 1, nxt[0:1, :], pltpu.roll(x, TM - 1, axis=0))
        xm1 = jnp.where(pos >= 1, xm1, 0.0)
        xm2 = jnp.where(pos >= 2, xm2, 0.0)
        xp1 = jnp.where(pos <= slen - 2, xp1, 0.0)
        return w[0:1, :] * xm2 + w[1:2, :] * xm1 + w[2:3, :] * x + w[3:4, :] * xp1 + bias

    cos = jnp.where(is_lat, cos_ref[...], 1.0)
    sin = jnp.where(is_lat, sin_ref[...], 0.0)

    def rope(t):
        w = t.shape[1]
        return t * cos[:, :w] + _swap_halves(t) * sin[:, :w]

    def head_norm(t, gain):
        w = t.shape[1]
        ms = _dot3(t * t, hm_ref[:w, :w])
        return t * lax.rsqrt(ms + NORM_EPS) * gain

    scale = HEAD_DIM ** -0.5
    qa_ref[...] = (rope(sec(C_QA, C_KA)) * scale).astype(BF16)
    kat_ref[...] = rope(sec(C_KA, C_VA)).T.astype(BF16)
    va_ref[...] = sec(C_VA, C_QD).astype(BF16)
    qd_ref[...] = (rope(head_norm(sec(C_QD, C_KD), gq_ref[...])) * scale).astype(BF16)
    kdt_ref[...] = rope(head_norm(sec(C_KD, C_VD), gk_ref[...])).T.astype(BF16)
    vd_ref[...] = sec(C_VD, C_Z).astype(BF16)
    z_ref[...] = sec(C_Z, C_XBC)
    xbc_ref[...] = _silu(conv(C_XBC, C_DT, ws_ref[...], bs_ref[...]))
    dt_ref[...] = sec(C_DT, C_LX)
    lu_ref[...] = conv(C_LX, C_LG, wl_ref[...], bl_ref[...])
    lg_ref[...] = sec(C_LG, NP_IN)


def _mod_spec(chunk, n_lat, seq, batch, tile):
    def imap(i):
        row0 = i * tile
        seg = jnp.where(row0 < n_lat, row0 // seq, batch)
        return (seg * 6 + chunk, 0, 0)
    return imap


def _row_sources(x_lat, x_ctx, n_lat):
    d = x_lat.shape[1]
    r8 = TM // SUBLANES
    lat_tiles = n_lat // TM

    def specs(arr, tile_of):
        n8 = arr.shape[0] // SUBLANES
        tiles = arr.shape[0] // TM
        t = lambda i: jnp.clip(tile_of(i), 0, tiles - 1)
        return [pl.BlockSpec((TM, d), lambda i: (t(i), 0)),
                pl.BlockSpec((SUBLANES, d), lambda i: (jnp.clip(t(i) * r8 - 1, 0, n8 - 1), 0)),
                pl.BlockSpec((SUBLANES, d), lambda i: (jnp.clip((t(i) + 1) * r8, 0, n8 - 1), 0))]

    if x_ctx is None:
        return specs(x_lat, lambda i: i) + specs(x_lat, lambda i: 0 * i), [x_lat] * 6
    return specs(x_lat, lambda i: i) + specs(x_ctx, lambda i: i - lat_tiles), [x_lat] * 3 + [x_ctx] * 3


def _inproj(x_lat, x_ctx, mod, gpre, w_pad, cos_t, sin_t, gq, gk, hm, ws, bs, wl, bl, *, n_lat, seq, ctx_len, batch):
    n = x_lat.shape[0] + (0 if x_ctx is None else x_ctx.shape[0])
    d = x_lat.shape[1]
    nt = n // TM
    spt = seq // TM
    x_specs, x_args = _row_sources(x_lat, x_ctx, n_lat)
    row = lambda w: pl.BlockSpec((TM, w), lambda i: (i, 0))
    colT = pl.BlockSpec((LANES, TM), lambda i: (0, i))
    const = lambda a: pl.BlockSpec(a.shape, lambda i: (0,) * a.ndim)
    out_shapes = (
        jax.ShapeDtypeStruct((n, 256), BF16), jax.ShapeDtypeStruct((LANES, n), BF16),
        jax.ShapeDtypeStruct((n, LANES), BF16),
        jax.ShapeDtypeStruct((n, 256), BF16), jax.ShapeDtypeStruct((LANES, n), BF16),
        jax.ShapeDtypeStruct((n, LANES), BF16),
        jax.ShapeDtypeStruct((n, 256), F32), jax.ShapeDtypeStruct((n, 512), F32),
        jax.ShapeDtypeStruct((n, LANES), F32), jax.ShapeDtypeStruct((n, 256), F32),
        jax.ShapeDtypeStruct((n, 256), F32))
    return pl.pallas_call(
        functools.partial(_inproj_kernel, n_lat=n_lat, seq=seq, ctx_len=ctx_len, split=x_ctx is not None),
        grid=(nt,),
        in_specs=x_specs + [
                  pl.BlockSpec((1, 1, d), _mod_spec(0, n_lat, seq, batch, TM)),
                  pl.BlockSpec((1, 1, d), _mod_spec(1, n_lat, seq, batch, TM)),
                  const(gpre), const(w_pad),
                  pl.BlockSpec((TM, 256), lambda i: (i % spt, 0)),
                  pl.BlockSpec((TM, 256), lambda i: (i % spt, 0)),
                  const(gq), const(gk), const(hm), const(ws), const(bs), const(wl), const(bl)],
        out_specs=(row(256), colT, row(LANES), row(256), colT, row(LANES),
                   row(256), row(512), row(LANES), row(256), row(256)),
        out_shape=out_shapes,
        compiler_params=_params(("parallel",)),
        name="inproj",
    )(*x_args, mod, mod, gpre, w_pad, cos_t, sin_t, gq, gk, hm, ws, bs, wl, bl)


def _chunk_maps(batch, seq, ctx_len):
    ncx = ctx_len // (SCAN_CHUNKS * CHUNK)
    nl = seq // (SCAN_CHUNKS * CHUNK)
    lat_blocks = batch * nl

    def block(b, c):
        return jnp.where(c < ncx, lat_blocks + b * ncx + c, b * nl + (c - ncx))

    def fwd(b, k):
        return (block(b, k), 0)

    def bwd(b, k):
        c = jnp.where(k < ncx, ncx - 1 - k, ncx + (nl - 1 - (k - ncx)))
        return (block(b, c), 0)

    return fwd, bwd, ncx + nl


def _ssd_kernel(xf_ref, dtf_ref, xb_ref, dtb_ref, dtbias_ref, alog_ref, yf_ref, yb_ref, state_ref):
    k = pl.program_id(1)

    @pl.when(k == 0)
    def _():
        state_ref[...] = jnp.zeros_like(state_ref)

    ri = lax.broadcasted_iota(jnp.int32, (CHUNK, CHUNK), 0)
    ci = lax.broadcasted_iota(jnp.int32, (CHUNK, CHUNK), 1)
    lane_lo = ci < HEAD_DIM
    aneg = -jnp.exp(alog_ref[...])
    dtbias = dtbias_ref[...]

    order = [(d, s if d == 0 else SCAN_CHUNKS - 1 - s) for s in range(SCAN_CHUNKS) for d in range(2)]
    for d, sub in order:
        x_ref, dt_ref, y_ref = ((xf_ref, dtf_ref, yf_ref), (xb_ref, dtb_ref, yb_ref))[d]
        rws = slice(sub * CHUNK, (sub + 1) * CHUNK)
        causal = (ri >= ci) if d == 0 else (ci >= ri)
        tmat = jnp.where(causal, 1.0, 0.0).astype(BF16)
        xs = x_ref[rws, 0:256]
        bm = x_ref[rws, 256:384]
        cm = x_ref[rws, 384:512]
        dtp = _softplus(dt_ref[rws, :] + dtbias)
        acum = _dot3_left(tmat, dtp * aneg)
        acum_t = acum.T
        bt = bm.T.astype(BF16)
        cmb = cm.astype(BF16)
        bmb = bm.astype(BF16)
        tot_row = CHUNK - 1 if d == 0 else 0
        for p in range(2):
            cmask = jnp.where(lane_lo if p == 0 else jnp.logical_not(lane_lo), cmb, jnp.zeros_like(cmb))
            cb = _dot_nt(cmask, bmb)
            cols, dts, ys = [], [], []
            x_pair = xs[:, p * LANES:(p + 1) * LANES]
            for j in range(2):
                col = 4 * d + 2 * p + j
                colb = jnp.broadcast_to(acum[:, col:col + 1], (CHUNK, CHUNK))
                rowb = jnp.broadcast_to(acum_t[col:col + 1, :], (CHUNK, CHUNK))
                cols.append(colb)
                dts.append(jnp.broadcast_to(dtp[:, col:col + 1], (CHUNK, CHUNK)))
            col_pair = jnp.where(lane_lo, cols[0], cols[1])
            dt_pair = jnp.where(lane_lo, dts[0], dts[1])
            xdt = x_pair * dt_pair
            xdt_b = xdt.astype(BF16)
            for j in range(2):
                col = 4 * d + 2 * p + j
                rowb = jnp.broadcast_to(acum_t[col:col + 1, :], (CHUNK, CHUNK))
                decay = jnp.exp(jnp.where(causal, cols[j] - rowb, NEG_INF))
                ys.append(_dot((cb * decay).astype(BF16), xdt_b))
            y_intra = jnp.where(lane_lo, ys[0], ys[1])
            s_old = state_ref[d, p]
            y_inter = _dot(cmask, s_old.astype(BF16)) * jnp.exp(col_pair)
            y_ref[rws, p * LANES:(p + 1) * LANES] = y_intra + y_inter
            tot_pair = col_pair[tot_row:tot_row + 1, :]
            to_end = jnp.exp(tot_pair - col_pair)
            state_ref[d, p] = s_old * jnp.exp(tot_pair) + _dot(bt, (xdt * to_end).astype(BF16))


def _ssd(xbc, dt, dtbias_row, alog_row, *, batch, seq, ctx_len):
    n = xbc.shape[0]
    fwd, bwd, steps = _chunk_maps(batch, seq, ctx_len)
    rows = SCAN_CHUNKS * CHUNK
    const = lambda a: pl.BlockSpec(a.shape, lambda b, k: (0,) * a.ndim)
    return pl.pallas_call(
        _ssd_kernel,
        grid=(batch, steps),
        in_specs=[pl.BlockSpec((rows, 512), fwd), pl.BlockSpec((rows, LANES), fwd),
                  pl.BlockSpec((rows, 512), bwd), pl.BlockSpec((rows, LANES), bwd),
                  const(dtbias_row), const(alog_row)],
        out_specs=(pl.BlockSpec((rows, 256), fwd), pl.BlockSpec((rows, 256), bwd)),
        out_shape=(jax.ShapeDtypeStruct((n, 256), F32), jax.ShapeDtypeStruct((n, 256), F32)),
        scratch_shapes=[pltpu.VMEM((2, 2, CHUNK, LANES), F32)],
        compiler_params=_params(("parallel", "arbitrary")),
        name="ssd_scan",
    )(xbc, dt, xbc, dt, dtbias_row, alog_row)


def _linear_scan(a, b, reverse):
    n = a.shape[0]
    row = lax.broadcasted_iota(jnp.int32, (n, 1), 0)
    s = 1
    while s < n:
        if s < SUBLANES:
            if reverse:
                ok = row < n - s
                a_sh = jnp.where(ok, pltpu.roll(a, n - s, axis=0), 1.0)
                b_sh = jnp.where(ok, pltpu.roll(b, n - s, axis=0), 0.0)
            else:
                ok = row >= s
                a_sh = jnp.where(ok, pltpu.roll(a, s, axis=0), 1.0)
                b_sh = jnp.where(ok, pltpu.roll(b, s, axis=0), 0.0)
            b = b + a * b_sh
            a = a * a_sh
        elif reverse:
            b = jnp.concatenate([b[:n - s] + a[:n - s] * b[s:], b[n - s:]], axis=0)
            a = jnp.concatenate([a[:n - s] * a[s:], a[n - s:]], axis=0)
        else:
            b = jnp.concatenate([b[:s], b[s:] + a[s:] * b[:n - s]], axis=0)
            a = jnp.concatenate([a[:s], a[s:] * a[:n - s]], axis=0)
        s *= 2
    return a, b


def _lru_kernel(uf_ref, ub_ref, wg_ref, bg_ref, lam_ref, hf_ref, hb_ref, carry_ref):
    k = pl.program_id(1)

    @pl.when(k == 0)
    def _():
        carry_ref[...] = jnp.zeros_like(carry_ref)

    for d, (u_ref, h_ref) in enumerate(((uf_ref, hf_ref), (ub_ref, hb_ref))):
        u = u_ref[...]
        gates = _dot(u.astype(BF16), wg_ref[d]) + bg_ref[d]
        r = jax.nn.sigmoid(gates[:, :LRU_WIDTH])
        ig = jax.nn.sigmoid(gates[:, LRU_WIDTH:])
        log_a = -LRU_C * r * _softplus(-lam_ref[d])
        a = jnp.exp(log_a)
        inp = jnp.sqrt(-jnp.tanh(log_a) * (1.0 + a * a)) * (ig * u)
        a_cum, b_cum = _linear_scan(a, inp, reverse=(d == 1))
        h = b_cum + a_cum * carry_ref[d, 0:1, :]
        h_ref[...] = h
        last = 0 if d == 1 else u.shape[0] - 1
        carry_ref[d, 0:1, :] = h[last:last + 1, :]


def _lru(u, wg, bg, lam, *, batch, seq, ctx_len):
    n = u.shape[0]
    fwd, bwd, steps = _chunk_maps(batch, seq, ctx_len)
    rows = SCAN_CHUNKS * CHUNK
    const = lambda a: pl.BlockSpec(a.shape, lambda b, k: (0,) * a.ndim)
    return pl.pallas_call(
        _lru_kernel,
        grid=(batch, steps),
        in_specs=[pl.BlockSpec((rows, LRU_WIDTH), fwd), pl.BlockSpec((rows, LRU_WIDTH), bwd),
                  const(wg), const(bg), const(lam)],
        out_specs=(pl.BlockSpec((rows, LRU_WIDTH), fwd), pl.BlockSpec((rows, LRU_WIDTH), bwd)),
        out_shape=(jax.ShapeDtypeStruct((n, LRU_WIDTH), F32), jax.ShapeDtypeStruct((n, LRU_WIDTH), F32)),
        scratch_shapes=[pltpu.VMEM((2, SUBLANES, LRU_WIDTH), F32)],
        compiler_params=_params(("parallel", "arbitrary")),
        name="lru_scan",
    )(u, u, wg, bg, lam)


def _stack_heads(q, g):
    qf = q.astype(F32)
    lo = g * LANES
    return jnp.concatenate([qf[:, lo:lo + HEAD_DIM], qf[:, lo + HEAD_DIM:lo + LANES]], axis=0).astype(BF16)


def _value_lanes(g):
    lane = lax.broadcasted_iota(jnp.int32, (1, LANES), 1)
    return (lane < HEAD_DIM) if g == 0 else (lane >= HEAD_DIM)


def _aug_values(v, g):
    return jnp.where(_value_lanes(g), v, jnp.ones_like(v))


def _flash_init(rows, g, sink_pair):
    if sink_pair is None:
        return jnp.full((rows, 1), NEG_INF, F32), jnp.zeros((rows, LANES), F32)
    half = lax.broadcasted_iota(jnp.int32, (rows, 1), 0) < rows // 2
    m = jnp.where(half, sink_pair[0], sink_pair[1]).astype(F32)
    acc = jnp.broadcast_to(jnp.where(_value_lanes(g), 0.0, 1.0), (rows, LANES))
    return m, acc


def _flash_update(state, q2, kt, v_aug, mask=None):
    m, acc = state
    s = _dot(q2, kt)
    if mask is not None:
        s = jnp.where(mask, s, NEG_INF)
    m_new = jnp.maximum(m, jnp.max(s, axis=-1, keepdims=True))
    p = jnp.exp(s - m_new).astype(BF16)
    acc = jnp.exp(m - m_new) * acc + _dot(p, v_aug)
    return m_new, acc


def _flash_finish(states, tq):
    pieces = []
    for g, (_, acc) in enumerate(states):
        den = (1 - g) * HEAD_DIM
        o = acc[:, g * HEAD_DIM:(g + 1) * HEAD_DIM] / acc[:, den:den + 1]
        pieces += [o[:tq], o[tq:]]
    return jnp.concatenate(pieces, axis=1)


def _group_rows(g):
    return slice(g * HEAD_DIM, (g + 1) * HEAD_DIM)


def _dense_attn_kernel(*refs, tq, seg_lens, has_sink):
    refs = list(refs)
    sink_ref = refs.pop(0) if has_sink else None
    q_ref = refs.pop(0)
    o_ref = refs.pop()
    segs = [(refs[2 * i], refs[2 * i + 1], n) for i, n in enumerate(seg_lens)]
    q = q_ref[...]
    q2 = [_stack_heads(q, g) for g in range(2)]
    states = tuple(_flash_init(2 * tq, g, (sink_ref[2 * g], sink_ref[2 * g + 1]) if has_sink else None)
                   for g in range(2))
    for kt_ref, v_ref, n_keys in segs:
        if n_keys <= KV_CHUNK:
            v = v_ref[...]
            states = tuple(_flash_update(states[g], q2[g], kt_ref[_group_rows(g), :], _aug_values(v, g))
                           for g in range(2))
        else:
            def body(c, sts, kt_ref=kt_ref, v_ref=v_ref):
                off = pl.multiple_of(c * KV_CHUNK, KV_CHUNK)
                v = v_ref[pl.ds(off, KV_CHUNK), :]
                return tuple(_flash_update(sts[g], q2[g], kt_ref[_group_rows(g), pl.ds(off, KV_CHUNK)],
                                           _aug_values(v, g)) for g in range(2))
            states = lax.fori_loop(0, n_keys // KV_CHUNK, body, states, unroll=KV_UNROLL)
    o_ref[...] = _flash_finish(states, tq).astype(o_ref.dtype)


def _dense_attn(q, kt, v, sink, *, q_row0, q_len, tq, segs, batch):
    n = q.shape[0]
    qpb = q_len // tq
    q0 = q_row0 // tq
    in_specs, args = [], []
    if sink is not None:
        in_specs.append(pl.BlockSpec(memory_space=pltpu.SMEM))
        args.append(sink)
    in_specs.append(pl.BlockSpec((tq, 256), lambda b, i: (q0 + b * qpb + i, 0)))
    args.append(q)
    for row0, klen in segs:
        k0 = row0 // klen
        in_specs.append(pl.BlockSpec((LANES, klen), lambda b, i, k0=k0: (0, k0 + b)))
        in_specs.append(pl.BlockSpec((klen, LANES), lambda b, i, k0=k0: (k0 + b, 0)))
        args += [kt, v]
    return pl.pallas_call(
        functools.partial(_dense_attn_kernel, tq=tq, seg_lens=tuple(s[1] for s in segs), has_sink=sink is not None),
        grid=(batch, qpb),
        in_specs=in_specs,
        out_specs=pl.BlockSpec((tq, 256), lambda b, i: (b * qpb + i, 0)),
        out_shape=jax.ShapeDtypeStruct((batch * q_len, 256), BF16),
        compiler_params=_params(("parallel", "parallel")),
        name="dense_attn",
    )(*args)


def _window_attn_kernel(sink_ref, q_ref, ktc_ref, vc_ref, ktp_ref, vp_ref, ktm_ref, vm_ref, ktn_ref, vn_ref, o_ref,
                        *, n_tiles):
    n = pl.program_id(1)
    nsub = TQ_WINDOW // CHUNK
    iq = lax.broadcasted_iota(jnp.int32, (2 * CHUNK, CHUNK), 0) & (CHUNK - 1)
    jk = lax.broadcasted_iota(jnp.int32, (2 * CHUNK, CHUNK), 1)
    below = jk >= iq
    above = jk <= iq
    vctx = vc_ref[...]
    for j in range(nsub):
        cols = slice(j * CHUNK, (j + 1) * CHUNK)
        q = q_ref[cols, :]
        states = []
        for g in range(2):
            q2 = _stack_heads(q, g)
            rows = _group_rows(g)
            state = _flash_init(2 * CHUNK, g, (sink_ref[2 * g], sink_ref[2 * g + 1]))
            state = _flash_update(state, q2, ktc_ref[rows, :], _aug_values(vctx, g))
            state = _flash_update(state, q2, ktm_ref[rows, cols], _aug_values(vm_ref[cols, :], g))
            if j > 0:
                prev = slice((j - 1) * CHUNK, j * CHUNK)
                state = _flash_update(state, q2, ktm_ref[rows, prev], _aug_values(vm_ref[prev, :], g), below)
            else:
                state = _flash_update(state, q2, ktp_ref[rows, :], _aug_values(vp_ref[...], g),
                                      jnp.logical_and(below, n > 0))
            if j < nsub - 1:
                nxt = slice((j + 1) * CHUNK, (j + 2) * CHUNK)
                state = _flash_update(state, q2, ktm_ref[rows, nxt], _aug_values(vm_ref[nxt, :], g), above)
            else:
                state = _flash_update(state, q2, ktn_ref[rows, :], _aug_values(vn_ref[...], g),
                                      jnp.logical_and(above, n < n_tiles - 1))
            states.append(state)
        o_ref[cols, :] = _flash_finish(states, CHUNK).astype(o_ref.dtype)


def _window_attn(q, kt, v, sink, *, batch, seq, ctx_len):
    nt = seq // TQ_WINDOW
    nsub = TQ_WINDOW // CHUNK
    nb = seq // CHUNK
    ctx0 = (batch * seq) // ctx_len
    prev = lambda b, n: b * nb + jnp.maximum(n * nsub - 1, 0)
    nxt = lambda b, n: b * nb + jnp.minimum((n + 1) * nsub, nb - 1)
    return pl.pallas_call(
        functools.partial(_window_attn_kernel, n_tiles=nt),
        grid=(batch, nt),
        in_specs=[pl.BlockSpec(memory_space=pltpu.SMEM),
                  pl.BlockSpec((TQ_WINDOW, 256), lambda b, n: (b * nt + n, 0)),
                  pl.BlockSpec((LANES, ctx_len), lambda b, n: (0, ctx0 + b)),
                  pl.BlockSpec((ctx_len, LANES), lambda b, n: (ctx0 + b, 0)),
                  pl.BlockSpec((LANES, CHUNK), lambda b, n: (0, prev(b, n))),
                  pl.BlockSpec((CHUNK, LANES), lambda b, n: (prev(b, n), 0)),
                  pl.BlockSpec((LANES, TQ_WINDOW), lambda b, n: (0, b * nt + n)),
                  pl.BlockSpec((TQ_WINDOW, LANES), lambda b, n: (b * nt + n, 0)),
                  pl.BlockSpec((LANES, CHUNK), lambda b, n: (0, nxt(b, n))),
                  pl.BlockSpec((CHUNK, LANES), lambda b, n: (nxt(b, n), 0))],
        out_specs=pl.BlockSpec((TQ_WINDOW, 256), lambda b, n: (b * nt + n, 0)),
        out_shape=jax.ShapeDtypeStruct((batch * seq, 256), BF16),
        compiler_params=_params(("parallel", "parallel")),
        name="window_attn",
    )(sink, q, kt, v, kt, v, kt, v, kt, v)


def _gelu_tanh(x):
    return 0.5 * x * (1.0 + jnp.tanh(math.sqrt(2.0 / math.pi) * (x + 0.044715 * (x * x * x))))


def _outproj_kernel(x_ref, xc_ref, gate_ref, gpost_ref, oa_ref, od_ref, oac_ref, odc_ref, yf_ref, yb_ref, xs_ref,
                    z_ref, dsk_ref, gn_ref, hf_ref, hb_ref, lg_ref, w_ref,
                    fshift_ref, fscale_ref, fpre_ref, rwt_ref, rb_ref,
                    o_ref, hffn_ref, ld_ref, wk_ref, tab_ref, *, n_lat, split):
    is_lat = pl.program_id(0) * TM < n_lat
    x = jnp.where(jnp.logical_or(is_lat, not split), x_ref[...], xc_ref[...])
    oa = jnp.where(is_lat, oa_ref[...], oac_ref[...])
    od = jnp.where(is_lat, od_ref[...], odc_ref[...])
    y_ssd = (yf_ref[...] + yb_ref[...] + xs_ref[...] * dsk_ref[...]) * _silu(z_ref[...])
    ob = _rms(y_ssd, gn_ref[...])
    oc = (hf_ref[...] + hb_ref[...]) * _gelu_tanh(lg_ref[...])
    y = (_dot(oa, w_ref[0:256, :]) + _dot(ob.astype(BF16), w_ref[256:512, :])
         + _dot(oc.astype(BF16), w_ref[512:768, :]) + _dot(od, w_ref[768:1024, :]))
    x_mid = x + gate_ref[0] * _rms(y, gpost_ref[...])
    o_ref[...] = x_mid
    _route_tile(x_mid, fshift_ref, fscale_ref, fpre_ref, rwt_ref, rb_ref, hffn_ref, ld_ref, wk_ref, tab_ref)


def _outproj(x_lat, x_ctx, mod, gpost, oa, od, oa_c, od_c, yf, yb, xbc, z, dsk, gn, hf, hb, lg, w_out,
             ffn_pre, rwt, rb, *, n_rows, n_lat, seq, batch):
    d = x_lat.shape[1]
    lat_tiles = n_lat // TM
    x_specs, x_args = _row_sources(x_lat, x_ctx, n_lat)
    row = lambda w: pl.BlockSpec((TM, w), lambda i: (i, 0))
    col = pl.BlockSpec((TOP_K, TM), lambda i: (0, i))
    lat = pl.BlockSpec((TM, 256), lambda i: (jnp.minimum(i, lat_tiles - 1), 0))
    ctx = pl.BlockSpec((TM, 256), lambda i: (jnp.maximum(i - lat_tiles, 0), 0))
    const = lambda a: pl.BlockSpec(a.shape, lambda i: (0,) * a.ndim)
    return pl.pallas_call(
        functools.partial(_outproj_kernel, n_lat=n_lat, split=x_ctx is not None),
        grid=(n_rows // TM,),
        in_specs=[x_specs[0], x_specs[3], pl.BlockSpec((1, 1, d), _mod_spec(2, n_lat, seq, batch, TM)), const(gpost),
                  lat, lat, ctx, ctx, row(256), row(256), row(256), row(256), const(dsk), const(gn),
                  row(256), row(256), row(256), const(w_out),
                  pl.BlockSpec((1, 1, d), _mod_spec(3, n_lat, seq, batch, TM)),
                  pl.BlockSpec((1, 1, d), _mod_spec(4, n_lat, seq, batch, TM)),
                  const(ffn_pre), const(rwt), const(rb)],
        out_specs=(row(d), row(d), col, col, pl.BlockSpec((SUBLANES, 2 * N_EXPERTS), lambda i: (i, 0))),
        out_shape=(jax.ShapeDtypeStruct((n_rows, d), F32),
                   jax.ShapeDtypeStruct((n_rows, d), BF16),
                   jax.ShapeDtypeStruct((TOP_K, n_rows), jnp.int32),
                   jax.ShapeDtypeStruct((TOP_K, n_rows), F32),
                   jax.ShapeDtypeStruct((n_rows // TM * SUBLANES, 2 * N_EXPERTS), jnp.int32)),
        compiler_params=_params(("parallel",)),
        name="outproj_route",
    )(x_args[0], x_args[3], mod, gpost, oa, od, oa_c, od_c, yf, yb, xbc, z, dsk, gn, hf, hb, lg, w_out,
      mod, mod, ffn_pre, rwt, rb)


def _ceil_seg(c):
    return jnp.floor((c + (SEG_ALIGN - 1)) * (1.0 / SEG_ALIGN)) * SEG_ALIGN


def _route_tile(x, shift_ref, scale_ref, gpre_ref, rwt_ref, rb_ref, hb_ref, ld_ref, wk_ref, tab_ref):
    h = _rms(x, gpre_ref[...])
    h = h * (1.0 + scale_ref[0]) + shift_ref[0]
    hb = h.astype(BF16)
    hb_ref[...] = hb

    scores = jax.nn.sigmoid(_dot_nt(rwt_ref[...], hb))
    biased = scores + rb_ref[...]
    gsz = N_EXPERTS // N_EXPERT_GROUPS
    sub = lax.broadcasted_iota(jnp.int32, (gsz, TM), 0)
    blocks, gscore = [], []
    for g in range(N_EXPERT_GROUPS):
        blk = biased[g * gsz:(g + 1) * gsz, :]
        m1 = jnp.max(blk, axis=0, keepdims=True)
        first = jnp.min(jnp.where(blk == m1, sub, gsz), axis=0, keepdims=True)
        m2 = jnp.max(jnp.where(sub == first, -jnp.inf, blk), axis=0, keepdims=True)
        blocks.append(blk)
        gscore.append(m1 + m2)
    masked = []
    for g in range(N_EXPERT_GROUPS):
        rank = jnp.zeros((1, TM), F32)
        for g2 in range(N_EXPERT_GROUPS):
            if g2 == g:
                continue
            beats = (gscore[g2] > gscore[g]) | ((gscore[g2] == gscore[g]) if g2 < g else False)
            rank = rank + jnp.where(beats, 1.0, 0.0)
        masked.append(jnp.where(rank < TOPK_GROUPS, blocks[g], -jnp.inf))
    vals = jnp.concatenate(masked, axis=0)
    eidx = lax.broadcasted_iota(jnp.int32, (N_EXPERTS, TM), 0)
    self32 = jnp.zeros((N_EXPERTS, TM), F32)
    rest = vals
    for _ in range(TOP_K):
        top = jnp.max(rest, axis=0, keepdims=True)
        first = jnp.min(jnp.where(rest == top, eidx, N_EXPERTS), axis=0, keepdims=True)
        hit = eidx == first
        self32 = jnp.where(hit, 1.0, self32)
        rest = jnp.where(hit, -jnp.inf, rest)
    sel = self32 > 0.5
    picked = jnp.where(sel, scores, 0.0)
    wdense = picked / jnp.sum(picked, axis=0, keepdims=True) * ROUTED_SCALE

    tr = lax.broadcasted_iota(jnp.int32, (TM, TM), 0)
    tc = lax.broadcasted_iota(jnp.int32, (TM, TM), 1)
    before = jnp.where(tr < tc, 1.0, 0.0).astype(BF16)
    selb = self32.astype(BF16)
    pos = _dot(selb, before)
    er = lax.broadcasted_iota(jnp.int32, (N_EXPERTS, N_EXPERTS), 0)
    ec = lax.broadcasted_iota(jnp.int32, (N_EXPERTS, N_EXPERTS), 1)
    lower = jnp.where(ec < er, 1.0, 0.0).astype(BF16)
    upper = jnp.where(er < ec, 1.0, 0.0).astype(BF16)
    ksel = _dot(lower, selb)
    cnt_col = _ceil_seg(jnp.sum(self32, axis=1, keepdims=True))
    loc_col = _dot3_left(lower, jnp.broadcast_to(cnt_col, (N_EXPERTS, LANES)))[:, 0:1]
    cnt_row = _ceil_seg(_dot_nt(jnp.ones((SUBLANES, TM), BF16), selb))
    loc_row = _dot3(cnt_row, upper)
    tab_ref[...] = jnp.concatenate([cnt_row, loc_row], axis=1).astype(jnp.int32)

    r8 = lax.broadcasted_iota(jnp.int32, (TOP_K, TM), 0)
    ld = jnp.zeros((TOP_K, TM), F32)
    wk = jnp.zeros((TOP_K, TM), F32)
    stage_row = pos + loc_col
    for k in range(TOP_K):
        one = sel & (ksel == float(k))
        ld = jnp.where(r8 == k, jnp.sum(jnp.where(one, stage_row, 0.0), axis=0, keepdims=True), ld)
        wk = jnp.where(r8 == k, jnp.sum(jnp.where(one, wdense, 0.0), axis=0, keepdims=True), wk)
    ld_ref[...] = ld.astype(jnp.int32)
    wk_ref[...] = wk


def _pow2_pieces(limit):
    bits, b = [], limit
    while b >= SEG_ALIGN:
        bits.append(b)
        b //= 2
    return bits


def _copy_pieces(n, src_ref, src0, dst_ref, dst0, sem, limit, wait, same_src=False):
    for bit in _pow2_pieces(limit):
        @pl.when((n & bit) != 0)
        def _():
            off = n & ~(2 * bit - 1)
            cp = pltpu.make_async_copy(src_ref.at[pl.ds(pl.multiple_of(src0 + (0 if same_src else off), SEG_ALIGN),
                                                          bit)],
                                       dst_ref.at[pl.ds(pl.multiple_of(dst0 + off, SEG_ALIGN), bit)], sem)
            cp.wait() if wait else cp.start()


N_PIECE_TABS = 7


def _piece_copies(tile, tabs, stage_ref, slots_ref, sem, to_slots, wait):
    nbig_ref, nsmall_ref, _, bsrc_ref, bdst_ref, ssrc_ref, sdst_ref = tabs
    for rows, n_ref, a_ref, b_ref, cap in ((BIG_PIECE, nbig_ref, bsrc_ref, bdst_ref, BIG_MAX),
                                           (SEG_ALIGN, nsmall_ref, ssrc_ref, sdst_ref, SMALL_MAX)):
        def body(p, c, rows=rows, a_ref=a_ref, b_ref=b_ref, cap=cap):
            src = stage_ref.at[pl.ds(pl.multiple_of(a_ref[tile * cap + p], SEG_ALIGN), rows)]
            dst = slots_ref.at[pl.ds(pl.multiple_of(b_ref[tile * cap + p], SEG_ALIGN), rows)]
            cp = pltpu.make_async_copy(src, dst, sem) if to_slots else pltpu.make_async_copy(dst, src, sem)
            cp.wait() if wait else cp.start()
            return c
        lax.fori_loop(0, n_ref[tile], body, 0)


def _used_blocks(tile, tabs):
    return (tabs[2][tile] + TM - 1) // TM


def _for_used_blocks(used, body):
    for b in range(TOP_K):
        body(b)
    for b in range(TOP_K, STAGE_ROWS // TM):
        @pl.when(b < used)
        def _():
            body(b)


def _stage_rows_iota():
    return lax.broadcasted_iota(jnp.int32, (TM // 2, TM), 0).astype(F32).astype(BF16)


def _pick_matrix(ld, base, vals, jrow):
    rel = (ld - base).astype(F32)
    rel = jnp.where(jnp.logical_and(rel >= 0.0, rel < TM // 2), rel, -1.0).astype(BF16)
    out = jnp.zeros((TM // 2, TM), BF16)
    for k in range(TOP_K):
        out = jnp.where(rel[k:k + 1, :] == jrow, vals[k:k + 1, :], out)
    return out


def _dispatch_kernel(*refs):
    tabs = refs[:N_PIECE_TABS]
    pstart_ref, npad_ref, hb_ref, ld_ref, xs_ref, stage, zbuf, sem, zsem = refs[N_PIECE_TABS:]
    i = pl.program_id(0)

    @pl.when(i == 0)
    def _():
        zbuf[...] = jnp.zeros_like(zbuf)
        for wait in (False, True):
            def body(e, c, wait=wait):
                _copy_pieces(npad_ref[e], zbuf, 0, xs_ref, pstart_ref[e], zsem, BM_EXPERT // 2, wait, same_src=True)
                return c
            lax.fori_loop(0, N_EXPERTS, body, 0)

    ld = ld_ref[...]
    hb = hb_ref[...]
    jrow = _stage_rows_iota()
    ones = jnp.ones((TOP_K, TM), BF16)

    cur = stage.at[i & 1]

    def block(b):
        for half in range(2):
            base = b * TM + half * (TM // 2)
            cur[base:base + TM // 2, :] = _dot(_pick_matrix(ld, base, ones, jrow), hb).astype(BF16)

    _for_used_blocks(_used_blocks(i, tabs), block)

    @pl.when(i > 0)
    def _():
        _piece_copies(i - 1, tabs, stage.at[(i - 1) & 1], xs_ref, sem, True, True)

    _piece_copies(i, tabs, cur, xs_ref, sem, True, False)

    @pl.when(i == pl.num_programs(0) - 1)
    def _():
        _piece_copies(i, tabs, cur, xs_ref, sem, True, True)


def _dispatch(tabs, pad_start, n_pad, hb, ld, n_slots):
    n, d = hb.shape
    grid_spec = pltpu.PrefetchScalarGridSpec(
        num_scalar_prefetch=N_PIECE_TABS + 2,
        grid=(n // TM,),
        in_specs=[pl.BlockSpec((TM, d), lambda i, *_: (i, 0)),
                  pl.BlockSpec((TOP_K, TM), lambda i, *_: (0, i))],
        out_specs=pl.BlockSpec(memory_space=pl.ANY),
        scratch_shapes=[pltpu.VMEM((2, STAGE_ROWS, d), BF16), pltpu.VMEM((BM_EXPERT // 2, d), BF16),
                        pltpu.SemaphoreType.DMA(()), pltpu.SemaphoreType.DMA(())],
    )
    return pl.pallas_call(
        _dispatch_kernel,
        grid_spec=grid_spec,
        out_shape=jax.ShapeDtypeStruct((n_slots, d), BF16),
        compiler_params=_params(("arbitrary",)),
        name="moe_dispatch",
    )(*tabs, pad_start, n_pad, hb, ld)


def _expert_kernel(be_ref, na_ref, nxt_ref, slot_ref, xs_hbm, wg_hbm, wu_hbm, wd_hbm, ys_ref,
                   xbuf, wg_raw, wu_raw, wd_raw, wgub, wdb, xsem, wsem, *, layer):
    i = pl.program_id(0)
    n_act = na_ref[0]

    def weight_copies(e, slot):
        return [pltpu.make_async_copy(src.at[layer, e], dst.at[slot], wsem.at[slot, j])
                for j, (src, dst) in enumerate(((wg_hbm, wg_raw), (wu_hbm, wu_raw), (wd_hbm, wd_raw)))]

    def rows_copy(blk):
        buf = lax.rem(blk, X_BUFFERS)
        return pltpu.make_async_copy(xs_hbm.at[pl.ds(pl.multiple_of(blk * BM_EXPERT, BM_EXPERT), BM_EXPERT)],
                                     xbuf.at[buf], xsem.at[buf])

    @pl.when(i < n_act)
    def _():
        e, slot = be_ref[i], slot_ref[i]

        @pl.when(i == 0)
        def _():
            for cp in weight_copies(e, slot):
                cp.start()
            for j in range(X_BUFFERS - 1):
                @pl.when(j < n_act)
                def _():
                    rows_copy(j).start()

        @pl.when(i + (X_BUFFERS - 1) < n_act)
        def _():
            rows_copy(i + (X_BUFFERS - 1)).start()

        @pl.when(jnp.logical_or(i == 0, e != be_ref[jnp.maximum(i - 1, 0)]))
        def _():
            for cp in weight_copies(e, slot):
                cp.wait()
            wgub[:, :EXPERT_HIDDEN] = wg_raw[slot].astype(BF16)
            wgub[:, EXPERT_HIDDEN:] = wu_raw[slot].astype(BF16)
            wdb[...] = wd_raw[slot].astype(BF16)

            @pl.when(nxt_ref[i] != e)
            def _():
                for cp in weight_copies(nxt_ref[i], 1 - slot):
                    cp.start()

        rows_copy(i).wait()
        gu = _dot(xbuf[lax.rem(i, X_BUFFERS)], wgub[...])
        hid = _silu(gu[:, :EXPERT_HIDDEN]) * gu[:, EXPERT_HIDDEN:]
        ys_ref[...] = _dot(hid.astype(BF16), wdb[...]).astype(ys_ref.dtype)


def _experts(block_e, n_active, next_e, w_slot, xs, wg, wu, wd, layer):
    n_slots, d = xs.shape
    nb = n_slots // BM_EXPERT
    tiles = pl.BlockSpec((BM_EXPERT, d), lambda i, be, na, *_: (jnp.minimum(i, na[0] - 1), 0))
    anywhere = pl.BlockSpec(memory_space=pl.ANY)
    grid_spec = pltpu.PrefetchScalarGridSpec(
        num_scalar_prefetch=4,
        grid=(nb,),
        in_specs=[anywhere, anywhere, anywhere, anywhere],
        out_specs=tiles,
        scratch_shapes=[pltpu.VMEM((X_BUFFERS, BM_EXPERT, d), BF16),
                        pltpu.VMEM((2, d, EXPERT_HIDDEN), F32), pltpu.VMEM((2, d, EXPERT_HIDDEN), F32),
                        pltpu.VMEM((2, EXPERT_HIDDEN, d), F32),
                        pltpu.VMEM((d, 2 * EXPERT_HIDDEN), BF16),
                        pltpu.VMEM((EXPERT_HIDDEN, d), BF16),
                        pltpu.SemaphoreType.DMA((X_BUFFERS,)), pltpu.SemaphoreType.DMA((2, 3))],
    )
    return pl.pallas_call(
        functools.partial(_expert_kernel, layer=layer),
        grid_spec=grid_spec,
        out_shape=jax.ShapeDtypeStruct((n_slots, d), BF16),
        compiler_params=_params(("arbitrary",)),
        name="moe_experts",
    )(block_e, n_active, next_e, w_slot, xs, wg, wu, wd)


def _combine_kernel(*refs):
    tabs = refs[:N_PIECE_TABS]
    (ys_ref, ld_ref, wk_ref, hb_ref, x_ref, gate_ref, gpost_ref, sg_ref, su_ref, sd_ref, o_ref,
     stage, acc_ref, sem) = refs[N_PIECE_TABS:]
    i = pl.program_id(0)

    @pl.when(i == 0)
    def _():
        stage[...] = jnp.zeros_like(stage)
        _piece_copies(0, tabs, stage.at[0], ys_ref, sem, False, False)

    cur = stage.at[i & 1]
    _piece_copies(i, tabs, cur, ys_ref, sem, False, True)

    @pl.when(i + 1 < pl.num_programs(0))
    def _():
        _piece_copies(i + 1, tabs, stage.at[(i + 1) & 1], ys_ref, sem, False, False)

    hb = hb_ref[...]
    acc_ref[...] = _dot((_silu(_dot(hb, sg_ref[...])) * _dot(hb, su_ref[...])).astype(BF16), sd_ref[...])

    ld = ld_ref[...]
    wkb = wk_ref[...].astype(BF16)
    jrow = _stage_rows_iota()

    def block(b):
        for half in range(2):
            base = b * TM + half * (TM // 2)
            weights = _pick_matrix(ld, base, wkb, jrow)
            acc_ref[...] += lax.dot_general(weights, cur[base:base + TM // 2, :], (((0,), (0,)), ((), ())),
                                            preferred_element_type=F32)

    _for_used_blocks(_used_blocks(i, tabs), block)
    o_ref[...] = x_ref[...] + gate_ref[0] * _rms(acc_ref[...], gpost_ref[...])


def _combine(tabs, ys, ld, wk, hb, xu, mod, gpost, sg, su, sd, *, n_rows, n_lat, seq, batch):
    d = xu.shape[1]
    row = lambda w: pl.BlockSpec((TM, w), lambda i, *_: (i, 0))
    col = pl.BlockSpec((TOP_K, TM), lambda i, *_: (0, i))
    const = lambda a: pl.BlockSpec(a.shape, lambda i, *_: (0,) * a.ndim)
    mod_map = _mod_spec(5, n_lat, seq, batch, TM)
    grid_spec = pltpu.PrefetchScalarGridSpec(
        num_scalar_prefetch=N_PIECE_TABS,
        grid=(n_rows // TM,),
        in_specs=[pl.BlockSpec(memory_space=pl.ANY), col, col, row(d), row(d),
                  pl.BlockSpec((1, 1, d), lambda i, *_: mod_map(i)),
                  const(gpost), const(sg), const(su), const(sd)],
        out_specs=row(d),
        scratch_shapes=[pltpu.VMEM((2, STAGE_ROWS, d), BF16), pltpu.VMEM((TM, d), F32), pltpu.SemaphoreType.DMA(())],
    )
    return pl.pallas_call(
        _combine_kernel,
        grid_spec=grid_spec,
        out_shape=jax.ShapeDtypeStruct((n_rows, d), F32),
        compiler_params=_params(("arbitrary",)),
        name="moe_combine",
    )(*tabs, ys, ld, wk, hb, xu, mod, gpost, sg, su, sd)


def _deinterleave(w):
    cols = w.shape[-1]
    perm = jnp.concatenate([jnp.arange(0, HEAD_DIM, 2), jnp.arange(1, HEAD_DIM, 2)])
    idx = (jnp.arange(cols // HEAD_DIM)[:, None] * HEAD_DIM + perm[None, :]).reshape(-1)
    return w[..., idx]


def _pad_in_proj(w_in):
    d = w_in.shape[0]
    o = 0
    parts = {}
    for name, width in (("qa", 256), ("ka", 128), ("va", 128), ("z", 256), ("xs", 256), ("bm", 128), ("cm", 128),
                        ("dtf", 4), ("dtb", 4), ("lx", 256), ("lg", 256), ("qd", 256), ("kd", 128), ("vd", 128)):
        parts[name] = w_in[:, o:o + width]
        o += width
    dt = jnp.concatenate([parts["dtf"], parts["dtb"], jnp.zeros((d, LANES - 8), w_in.dtype)], axis=1)
    cols = [_deinterleave(parts["qa"]), _deinterleave(parts["ka"]), parts["va"],
            _deinterleave(parts["qd"]), _deinterleave(parts["kd"]), parts["vd"],
            parts["z"], parts["xs"], parts["bm"], parts["cm"], dt, parts["lx"], parts["lg"]]
    return jnp.concatenate(cols, axis=1).astype(BF16)


def _rope_tables(seq):
    t = jnp.arange(seq)
    rowp = (t // GRID_W).astype(F32)
    colp = (t % GRID_W).astype(F32)
    axis_dim = HEAD_DIM // 2
    inv_freq = ROPE_THETA ** (-jnp.arange(0, axis_dim, 2, dtype=F32) / axis_dim)
    ang = jnp.concatenate([rowp[:, None] * inv_freq, colp[:, None] * inv_freq], axis=-1)
    cos, sin = jnp.cos(ang), jnp.sin(ang)
    cos_h = jnp.concatenate([cos, cos], axis=-1)
    sin_h = jnp.concatenate([-sin, sin], axis=-1)
    return jnp.tile(cos_h, (1, 4)), jnp.tile(sin_h, (1, 4))


def _block_diag(w):
    nb, bd, _ = w.shape
    eye = jnp.eye(nb, dtype=w.dtype)
    return (eye[:, None, :, None] * w[:, :, None, :]).reshape(nb * bd, nb * bd)


def _piece_table(counts, cap, stage0, slot0, rows, ids):
    ends = jnp.cumsum(counts, axis=1)
    q = jnp.arange(cap, dtype=jnp.int32)
    owner = jnp.sum((ends[:, None, :] <= q[None, :, None]).astype(jnp.int32), axis=-1)
    mine = owner[:, :, None] == ids
    pick = lambda v: jnp.sum(jnp.where(mine, v[:, None, :], 0), axis=-1)
    step = rows * (q[None, :] - pick(ends - counts))
    return (pick(stage0) + step).reshape(-1), (pick(slot0) + step).reshape(-1)


def _lane_row(fwd, bwd):
    return jnp.concatenate([fwd, bwd, jnp.zeros((LANES - 8,), F32)]).reshape(1, LANES)


def kernel(x, c, ctx, c_ctx, w_ada, b_ada, g_mix_pre, g_mix_post, g_ffn_pre, g_ffn_post, w_in, w_out, a_sink,
           ssd_conv_w, ssd_conv_b, ssd_dt_bias, ssd_a_log, ssd_d, ssd_norm, lru_conv_w, lru_conv_b, lru_w_a,
           lru_b_a, lru_w_i, lru_b_i, lru_lambda, d_q_norm, d_k_norm, router_w, router_bias, exp_w_gate,
           exp_w_up, exp_w_down, sh_w_gate, sh_w_up, sh_w_down):
    batch, seq, d = x.shape
    ctx_len = ctx.shape[1]
    depth = w_ada.shape[0]
    n_lat = batch * seq
    n_ctx = batch * ctx_len
    n_all = n_lat + n_ctx
    assert seq % TM == 0 and n_ctx % TM == 0 and ctx_len >= LRU_CONV
    assert seq % TQ_WINDOW == 0 and seq % (SCAN_CHUNKS * CHUNK) == 0 and ctx_len % (SCAN_CHUNKS * CHUNK) == 0
    assert ctx_len <= KV_CHUNK and seq % KV_CHUNK == 0 and seq % TQ_GLOBAL == 0 and batch + 1 <= SUBLANES

    xu, xu_ctx = x.reshape(n_lat, d), ctx.reshape(n_ctx, d)
    cin = jnp.concatenate([c, c_ctx[None, :], jnp.zeros((SUBLANES - batch - 1, d), F32)], axis=0)
    mod_all = _adaln(cin, w_ada, b_ada)
    cos_t, sin_t = _rope_tables(seq)
    hm = jnp.kron(jnp.eye(4, dtype=F32), jnp.full((HEAD_DIM, HEAD_DIM), 1.0 / HEAD_DIM, F32)).astype(BF16)

    for l in range(depth):
        with_ctx = l < depth - 1
        mod = mod_all[l].reshape(SUBLANES * 6, 1, d)
        gq = jnp.tile(_deinterleave(d_q_norm[l]), 4).reshape(1, 256)
        gk = jnp.tile(_deinterleave(d_k_norm[l]), 2).reshape(1, LANES)
        qa, kat, va, qd, kdt, vd, z, xbc, dt, lu, lg = _inproj(
            xu, xu_ctx, mod, g_mix_pre[l].reshape(1, d), _pad_in_proj(w_in[l]), cos_t, sin_t, gq, gk, hm,
            ssd_conv_w[l], ssd_conv_b[l].reshape(1, -1), lru_conv_w[l], lru_conv_b[l].reshape(1, -1),
            n_lat=n_lat, seq=seq, ctx_len=ctx_len, batch=batch)
        yf, yb = _ssd(xbc, dt, _lane_row(ssd_dt_bias[l, 0], ssd_dt_bias[l, 1]),
                      _lane_row(ssd_a_log[l, 0], ssd_a_log[l, 1]), batch=batch, seq=seq, ctx_len=ctx_len)
        wg = jnp.stack([jnp.concatenate([_block_diag(lru_w_a[l, dd]), _block_diag(lru_w_i[l, dd])], axis=1)
                        for dd in range(2)]).astype(BF16)
        bg = jnp.concatenate([lru_b_a[l], lru_b_i[l]], axis=1).reshape(2, 1, 2 * LRU_WIDTH)
        hf, hb = _lru(lu, wg, bg, lru_lambda[l].reshape(2, 1, LRU_WIDTH), batch=batch, seq=seq, ctx_len=ctx_len)

        oa = _window_attn(qa, kat, va, a_sink[l], batch=batch, seq=seq, ctx_len=ctx_len)
        od = _dense_attn(qd, kdt, vd, None, q_row0=0, q_len=seq, tq=TQ_GLOBAL,
                         segs=[(n_lat, ctx_len), (0, seq)], batch=batch)
        if with_ctx:
            oa_c = _dense_attn(qa, kat, va, a_sink[l], q_row0=n_lat, q_len=ctx_len, tq=ctx_len,
                               segs=[(n_lat, ctx_len)], batch=batch)
            od_c = _dense_attn(qd, kdt, vd, None, q_row0=n_lat, q_len=ctx_len, tq=ctx_len,
                               segs=[(n_lat, ctx_len)], batch=batch)
        else:
            oa_c, od_c = oa, od
        n_rows = n_all if with_ctx else n_lat

        dsk = jnp.repeat(ssd_d[l], HEAD_DIM).reshape(1, 256)
        xu_mid, hb_ffn, ld, wk, tab = _outproj(
            xu, xu_ctx, mod, g_mix_post[l].reshape(1, d), oa, od, oa_c, od_c, yf, yb, xbc, z, dsk,
            ssd_norm[l].reshape(1, 256), hf, hb, lg, w_out[l].astype(BF16),
            g_ffn_pre[l].reshape(1, d), router_w[l].T.astype(BF16), router_bias[l].reshape(N_EXPERTS, 1),
            n_rows=n_rows, n_lat=n_lat, seq=seq, batch=batch)
        n_tiles = n_rows // TM
        tab = tab.reshape(n_tiles, SUBLANES, 2 * N_EXPERTS)[:, 0, :]
        seg_cnt, seg_loc = tab[:, :N_EXPERTS], tab[:, N_EXPERTS:]
        counts = jnp.sum(seg_cnt, axis=0)
        padded = (counts + BM_EXPERT - 1) // BM_EXPERT * BM_EXPERT
        padded_end = jnp.cumsum(padded)
        offs = padded_end - padded
        seg_off = offs[None, :] + jnp.cumsum(seg_cnt, axis=0) - seg_cnt
        n_blocks = (n_rows * TOP_K + n_tiles * N_EXPERTS * SEG_ALIGN) // BM_EXPERT + N_EXPERTS
        n_active = (padded_end[-1] // BM_EXPERT).astype(jnp.int32).reshape(1)
        block_start = jnp.arange(n_blocks, dtype=jnp.int32) * BM_EXPERT
        block_e = jnp.minimum(jnp.sum((padded_end[None, :] <= block_start[:, None]).astype(jnp.int32), axis=1),
                              N_EXPERTS - 1)
        ids = jnp.arange(N_EXPERTS, dtype=jnp.int32)
        n_big = seg_cnt // BIG_PIECE
        n_small = (seg_cnt % BIG_PIECE) // SEG_ALIGN
        tabs = (jnp.sum(n_big, axis=1), jnp.sum(n_small, axis=1), seg_loc[:, -1] + seg_cnt[:, -1],
                *_piece_table(n_big, BIG_MAX, seg_loc, seg_off, BIG_PIECE, ids),
                *_piece_table(n_small, SMALL_MAX, seg_loc + n_big * BIG_PIECE, seg_off + n_big * BIG_PIECE,
                              SEG_ALIGN, ids))
        xs = _dispatch(tabs, offs + counts, padded - counts, hb_ffn, ld, n_blocks * BM_EXPERT)
        has_rows = padded > 0
        later = jnp.logical_and(ids[None, :] > ids[:, None], has_rows[None, :])
        nxt_of = jnp.min(jnp.where(later, ids[None, :], N_EXPERTS), axis=1)
        nxt_of = jnp.where(nxt_of == N_EXPERTS, ids, nxt_of)
        slot_of = (jnp.cumsum(has_rows.astype(jnp.int32)) - 1) & 1
        own = block_e[:, None] == ids[None, :]
        next_e = jnp.sum(jnp.where(own, nxt_of[None, :], 0), axis=1)
        w_slot = jnp.sum(jnp.where(own, slot_of[None, :], 0), axis=1)
        ys = _experts(block_e, n_active, next_e, w_slot, xs, exp_w_gate, exp_w_up, exp_w_down, l)
        xu = _combine(tabs, ys, ld, wk, hb_ffn, xu_mid, mod, g_ffn_post[l].reshape(1, d), sh_w_gate[l].astype(BF16),
                      sh_w_up[l].astype(BF16), sh_w_down[l].astype(BF16),
                      n_rows=n_rows, n_lat=n_lat, seq=seq, batch=batch)
        xu_ctx = None
    return xu[:n_lat].reshape(batch, seq, d)
```

```python
import functools
import math

import jax
import jax.numpy as jnp
from jax import lax
from jax.experimental import pallas as pl
from jax.experimental.pallas import tpu as pltpu

F32 = jnp.float32
BF16 = jnp.bfloat16

HEAD_DIM = 64
GRID_W = 64
ROPE_THETA = 10000.0
NORM_EPS = 1e-6
NEG_INF = -1e30
A_HEADS, A_KV_HEADS, WINDOW = 4, 2, 128
SSD_HEADS, SSD_GROUPS, SSD_STATE, SSD_CONV = 4, 2, 64, 4
LRU_WIDTH, LRU_BLOCKS, LRU_CONV, LRU_C = 256, 4, 4, 8.0
D_HEADS, D_KV_HEADS = 4, 2
N_EXPERTS, N_EXPERT_GROUPS, TOPK_GROUPS, TOP_K = 64, 8, 4, 8
EXPERT_HIDDEN, SHARED_HIDDEN = 256, 256
ROUTED_SCALE = 2.5

LANES = 128
SUBLANES = 8

TM = 512
CHUNK = 128
SCAN_CHUNKS = 2
TQ_GLOBAL = 256
TQ_WINDOW = 512
KV_CHUNK = 256
KV_UNROLL = 16
BM_EXPERT = 512
X_BUFFERS = 3
SEG_ALIGN = 16
STAGE_ROWS = TM * TOP_K + N_EXPERTS * SEG_ALIGN
BIG_PIECE = 64
BIG_MAX = STAGE_ROWS // BIG_PIECE
SMALL_MAX = N_EXPERTS * (BIG_PIECE // SEG_ALIGN - 1)
VMEM_LIMIT = 48 * 1024 * 1024

C_QA, C_KA, C_VA = 0, 256, 384
C_QD, C_KD, C_VD = 512, 768, 896
C_Z, C_XBC, C_DT = 1024, 1280, 1792
C_LX, C_LG = 1920, 2176
NP_IN = 2432


def _dot(a, b):
    return jnp.dot(a, b, preferred_element_type=F32)


def _dot_nt(a, b):
    return lax.dot_general(a, b, (((1,), (1,)), ((), ())), preferred_element_type=F32)


def _dot3(a, b):
    a1 = a.astype(BF16)
    r1 = a - a1.astype(F32)
    a2 = r1.astype(BF16)
    a3 = (r1 - a2.astype(F32)).astype(BF16)
    return _dot(a1, b) + _dot(a2, b) + _dot(a3, b)


def _dot3_left(a, b):
    b1 = b.astype(BF16)
    r1 = b - b1.astype(F32)
    b2 = r1.astype(BF16)
    b3 = (r1 - b2.astype(F32)).astype(BF16)
    return _dot(a, b1) + _dot(a, b2) + _dot(a, b3)


def _silu(x):
    return x * jax.nn.sigmoid(x)


def _softplus(x):
    return jnp.maximum(x, 0.0) + jnp.log1p(jnp.exp(-jnp.abs(x)))


def _rms(x, gain):
    return x * lax.rsqrt(jnp.mean(x * x, axis=-1, keepdims=True) + NORM_EPS) * gain


def _params(sem=None):
    return pltpu.CompilerParams(dimension_semantics=sem, vmem_limit_bytes=VMEM_LIMIT)


def _adaln_kernel(c_ref, w_ref, b_ref, o_ref):
    s = _silu(c_ref[...])
    o_ref[0] = _dot(s.astype(BF16), w_ref[0].astype(BF16)) + b_ref[0]


def _adaln(cin, w_ada, b_ada):
    depth, d, n6 = w_ada.shape
    tn = 1024
    return pl.pallas_call(
        _adaln_kernel,
        grid=(depth, n6 // tn),
        in_specs=[pl.BlockSpec((SUBLANES, d), lambda l, j: (0, 0)),
                  pl.BlockSpec((1, d, tn), lambda l, j: (l, 0, j)),
                  pl.BlockSpec((1, 1, tn), lambda l, j: (l, 0, j))],
        out_specs=pl.BlockSpec((1, SUBLANES, tn), lambda l, j: (l, 0, j)),
        out_shape=jax.ShapeDtypeStruct((depth, SUBLANES, n6), F32),
        compiler_params=_params(("parallel", "parallel")),
        name="adaln",
    )(cin, w_ada, b_ada.reshape(depth, 1, n6))


def _swap_halves(t):
    w = t.shape[1]
    lane = lax.broadcasted_iota(jnp.int32, (1, w), 1)
    first = (lane & 32) == 0
    return jnp.where(first, pltpu.roll(t, w - 32, axis=1), pltpu.roll(t, 32, axis=1))


def _inproj_kernel(x_ref, xp_ref, xn_ref, xc_ref, xcp_ref, xcn_ref, shift_ref, scale_ref, gpre_ref, w_ref,
                   cos_ref, sin_ref, gq_ref, gk_ref, hm_ref, ws_ref, bs_ref, wl_ref, bl_ref,
                   qa_ref, kat_ref, va_ref, qd_ref, kdt_ref, vd_ref, z_ref, xbc_ref, dt_ref, lu_ref, lg_ref,
                   *, n_lat, seq, ctx_len, split):
    i = pl.program_id(0)
    is_lat = i * TM < n_lat
    first = jnp.logical_or(is_lat, not split)

    def pre(a_ref, b_ref):
        x = jnp.where(first, a_ref[...], b_ref[...])
        return _rms(x, gpre_ref[...]) * (1.0 + scale_ref[0]) + shift_ref[0]

    h = pre(x_ref, xc_ref)
    hb = h.astype(BF16)
    hb_ext = jnp.concatenate([pre(xp_ref, xcp_ref), h, pre(xn_ref, xcn_ref)], axis=0).astype(BF16)

    def sec(a, b):
        return _dot(hb, w_ref[:, a:b])

    row = lax.broadcasted_iota(jnp.int32, (TM, 1), 0)
    pos = jnp.where(is_lat, lax.rem(i * TM + row, seq), lax.rem(i * TM - n_lat + row, ctx_len))
    slen = jnp.where(is_lat, seq, ctx_len)

    def conv(a, b, w, bias):
        ext = _dot(hb_ext, w_ref[:, a:b])
        prev, x, nxt = ext[0:SUBLANES], ext[SUBLANES:SUBLANES + TM], ext[SUBLANES + TM:]
        xm1 = jnp.where(row == 0, prev[7:8, :], pltpu.roll(x, 1, axis=0))
        xm2 = jnp.where(row == 0, prev[6:7, :], jnp.where(row == 1, prev[7:8, :], pltpu.roll(x, 2, axis=0)))
        xp1 = jnp.where(row == TM - 1, nxt[0:1, :], pltpu.roll(x, TM - 1, axis=0))
        xm1 = jnp.where(pos >= 1, xm1, 0.0)
        xm2 = jnp.where(pos >= 2, xm2, 0.0)
        xp1 = jnp.where(pos <= slen - 2, xp1, 0.0)
        return w[0:1, :] * xm2 + w[1:2, :] * xm1 + w[2:3, :] * x + w[3:4, :] * xp1 + bias

    cos = jnp.where(is_lat, cos_ref[...], 1.0)
    sin = jnp.where(is_lat, sin_ref[...], 0.0)

    def rope(t):
        w = t.shape[1]
        return t * cos[:, :w] + _swap_halves(t) * sin[:, :w]

    def head_norm(t, gain):
        w = t.shape[1]
        ms = _dot3(t * t, hm_ref[:w, :w])
        return t * lax.rsqrt(ms + NORM_EPS) * gain

    scale = HEAD_DIM ** -0.5
    qa_ref[...] = (rope(sec(C_QA, C_KA)) * scale).astype(BF16)
    kat_ref[...] = rope(sec(C_KA, C_VA)).T.astype(BF16)
    va_ref[...] = sec(C_VA, C_QD).astype(BF16)
    qd_ref[...] = (rope(head_norm(sec(C_QD, C_KD), gq_ref[...])) * scale).astype(BF16)
    kdt_ref[...] = rope(head_norm(sec(C_KD, C_VD), gk_ref[...])).T.astype(BF16)
    vd_ref[...] = sec(C_VD, C_Z).astype(BF16)
    z_ref[...] = sec(C_Z, C_XBC)
    xbc_ref[...] = _silu(conv(C_XBC, C_DT, ws_ref[...], bs_ref[...]))
    dt_ref[...] = sec(C_DT, C_LX)
    lu_ref[...] = conv(C_LX, C_LG, wl_ref[...], bl_ref[...])
    lg_ref[...] = sec(C_LG, NP_IN)


def _mod_spec(chunk, n_lat, seq, batch, tile):
    def imap(i):
        row0 = i * tile
        seg = jnp.where(row0 < n_lat, row0 // seq, batch)
        return (seg * 6 + chunk, 0, 0)
    return imap


def _row_sources(x_lat, x_ctx, n_lat):
    d = x_lat.shape[1]
    r8 = TM // SUBLANES
    lat_tiles = n_lat // TM

    def specs(arr, tile_of):
        n8 = arr.shape[0] // SUBLANES
        tiles = arr.shape[0] // TM
        t = lambda i: jnp.clip(tile_of(i), 0, tiles - 1)
        return [pl.BlockSpec((TM, d), lambda i: (t(i), 0)),
                pl.BlockSpec((SUBLANES, d), lambda i: (jnp.clip(t(i) * r8 - 1, 0, n8 - 1), 0)),
                pl.BlockSpec((SUBLANES, d), lambda i: (jnp.clip((t(i) + 1) * r8, 0, n8 - 1), 0))]

    if x_ctx is None:
        return specs(x_lat, lambda i: i) + specs(x_lat, lambda i: 0 * i), [x_lat] * 6
    return specs(x_lat, lambda i: i) + specs(x_ctx, lambda i: i - lat_tiles), [x_lat] * 3 + [x_ctx] * 3


def _inproj(x_lat, x_ctx, mod, gpre, w_pad, cos_t, sin_t, gq, gk, hm, ws, bs, wl, bl, *, n_lat, seq, ctx_len, batch):
    n = x_lat.shape[0] + (0 if x_ctx is None else x_ctx.shape[0])
    d = x_lat.shape[1]
    nt = n // TM
    spt = seq // TM
    x_specs, x_args = _row_sources(x_lat, x_ctx, n_lat)
    row = lambda w: pl.BlockSpec((TM, w), lambda i: (i, 0))
    colT = pl.BlockSpec((LANES, TM), lambda i: (0, i))
    const = lambda a: pl.BlockSpec(a.shape, lambda i: (0,) * a.ndim)
    out_shapes = (
        jax.ShapeDtypeStruct((n, 256), BF16), jax.ShapeDtypeStruct((LANES, n), BF16),
        jax.ShapeDtypeStruct((n, LANES), BF16),
        jax.ShapeDtypeStruct((n, 256), BF16), jax.ShapeDtypeStruct((LANES, n), BF16),
        jax.ShapeDtypeStruct((n, LANES), BF16),
        jax.ShapeDtypeStruct((n, 256), F32), jax.ShapeDtypeStruct((n, 512), F32),
        jax.ShapeDtypeStruct((n, LANES), F32), jax.ShapeDtypeStruct((n, 256), F32),
        jax.ShapeDtypeStruct((n, 256), F32))
    return pl.pallas_call(
        functools.partial(_inproj_kernel, n_lat=n_lat, seq=seq, ctx_len=ctx_len, split=x_ctx is not None),
        grid=(nt,),
        in_specs=x_specs + [
                  pl.BlockSpec((1, 1, d), _mod_spec(0, n_lat, seq, batch, TM)),
                  pl.BlockSpec((1, 1, d), _mod_spec(1, n_lat, seq, batch, TM)),
                  const(gpre), const(w_pad),
                  pl.BlockSpec((TM, 256), lambda i: (i % spt, 0)),
                  pl.BlockSpec((TM, 256), lambda i: (i % spt, 0)),
                  const(gq), const(gk), const(hm), const(ws), const(bs), const(wl), const(bl)],
        out_specs=(row(256), colT, row(LANES), row(256), colT, row(LANES),
                   row(256), row(512), row(LANES), row(256), row(256)),
        out_shape=out_shapes,
        compiler_params=_params(("parallel",)),
        name="inproj",
    )(*x_args, mod, mod, gpre, w_pad, cos_t, sin_t, gq, gk, hm, ws, bs, wl, bl)


def _chunk_maps(batch, seq, ctx_len):
    ncx = ctx_len // (SCAN_CHUNKS * CHUNK)
    nl = seq // (SCAN_CHUNKS * CHUNK)
    lat_blocks = batch * nl

    def block(b, c):
        return jnp.where(c < ncx, lat_blocks + b * ncx + c, b * nl + (c - ncx))

    def fwd(b, k):
        return (block(b, k), 0)

    def bwd(b, k):
        c = jnp.where(k < ncx, ncx - 1 - k, ncx + (nl - 1 - (k - ncx)))
        return (block(b, c), 0)

    return fwd, bwd, ncx + nl


def _ssd_body(xf_ref, dtf_ref, xb_ref, dtb_ref, dtbias_ref, alog_ref, yf_ref, yb_ref, state_ref):
    ri = lax.broadcasted_iota(jnp.int32, (CHUNK, CHUNK), 0)
    ci = lax.broadcasted_iota(jnp.int32, (CHUNK, CHUNK), 1)
    lane_lo = ci < HEAD_DIM
    aneg = -jnp.exp(alog_ref[...])
    dtbias = dtbias_ref[...]

    order = [(d, s if d == 0 else SCAN_CHUNKS - 1 - s) for s in range(SCAN_CHUNKS) for d in range(2)]
    for d, sub in order:
        x_ref, dt_ref, y_ref = ((xf_ref, dtf_ref, yf_ref), (xb_ref, dtb_ref, yb_ref))[d]
        rws = slice(sub * CHUNK, (sub + 1) * CHUNK)
        causal = (ri >= ci) if d == 0 else (ci >= ri)
        tmat = jnp.where(causal, 1.0, 0.0).astype(BF16)
        xs = x_ref[rws, 0:256]
        bm = x_ref[rws, 256:384]
        cm = x_ref[rws, 384:512]
        dtp = _softplus(dt_ref[rws, :] + dtbias)
        acum = _dot3_left(tmat, dtp * aneg)
        acum_t = acum.T
        bt = bm.T.astype(BF16)
        cmb = cm.astype(BF16)
        bmb = bm.astype(BF16)
        tot_row = CHUNK - 1 if d == 0 else 0
        for p in range(2):
            cmask = jnp.where(lane_lo if p == 0 else jnp.logical_not(lane_lo), cmb, jnp.zeros_like(cmb))
            cb = _dot_nt(cmask, bmb)
            cols, dts, ys = [], [], []
            x_pair = xs[:, p * LANES:(p + 1) * LANES]
            for j in range(2):
                col = 4 * d + 2 * p + j
                colb = jnp.broadcast_to(acum[:, col:col + 1], (CHUNK, CHUNK))
                rowb = jnp.broadcast_to(acum_t[col:col + 1, :], (CHUNK, CHUNK))
                cols.append(colb)
                dts.append(jnp.broadcast_to(dtp[:, col:col + 1], (CHUNK, CHUNK)))
            col_pair = jnp.where(lane_lo, cols[0], cols[1])
            dt_pair = jnp.where(lane_lo, dts[0], dts[1])
            xdt = x_pair * dt_pair
            xdt_b = xdt.astype(BF16)
            for j in range(2):
                col = 4 * d + 2 * p + j
                rowb = jnp.broadcast_to(acum_t[col:col + 1, :], (CHUNK, CHUNK))
                decay = jnp.exp(jnp.where(causal, cols[j] - rowb, NEG_INF))
                ys.append(_dot((cb * decay).astype(BF16), xdt_b))
            y_intra = jnp.where(lane_lo, ys[0], ys[1])
            s_old = state_ref[d, p]
            y_inter = _dot(cmask, s_old.astype(BF16)) * jnp.exp(col_pair)
            y_ref[rws, p * LANES:(p + 1) * LANES] = y_intra + y_inter
            tot_pair = col_pair[tot_row:tot_row + 1, :]
            to_end = jnp.exp(tot_pair - col_pair)
            state_ref[d, p] = s_old * jnp.exp(tot_pair) + _dot(bt, (xdt * to_end).astype(BF16))


def _linear_scan(a, b, reverse):
    n = a.shape[0]
    row = lax.broadcasted_iota(jnp.int32, (n, 1), 0)
    s = 1
    while s < n:
        if s < SUBLANES:
            if reverse:
                ok = row < n - s
                a_sh = jnp.where(ok, pltpu.roll(a, n - s, axis=0), 1.0)
                b_sh = jnp.where(ok, pltpu.roll(b, n - s, axis=0), 0.0)
            else:
                ok = row >= s
                a_sh = jnp.where(ok, pltpu.roll(a, s, axis=0), 1.0)
                b_sh = jnp.where(ok, pltpu.roll(b, s, axis=0), 0.0)
            b = b + a * b_sh
            a = a * a_sh
        elif reverse:
            b = jnp.concatenate([b[:n - s] + a[:n - s] * b[s:], b[n - s:]], axis=0)
            a = jnp.concatenate([a[:n - s] * a[s:], a[n - s:]], axis=0)
        else:
            b = jnp.concatenate([b[:s], b[s:] + a[s:] * b[:n - s]], axis=0)
            a = jnp.concatenate([a[:s], a[s:] * a[:n - s]], axis=0)
        s *= 2
    return a, b


def _lru_body(uf_ref, ub_ref, wg_ref, bg_ref, lam_ref, hf_ref, hb_ref, carry_ref):
    for d, (u_ref, h_ref) in enumerate(((uf_ref, hf_ref), (ub_ref, hb_ref))):
        u = u_ref[...]
        gates = _dot(u.astype(BF16), wg_ref[d]) + bg_ref[d]
        r = jax.nn.sigmoid(gates[:, :LRU_WIDTH])
        ig = jax.nn.sigmoid(gates[:, LRU_WIDTH:])
        log_a = -LRU_C * r * _softplus(-lam_ref[d])
        a = jnp.exp(log_a)
        inp = jnp.sqrt(-jnp.tanh(log_a) * (1.0 + a * a)) * (ig * u)
        a_cum, b_cum = _linear_scan(a, inp, reverse=(d == 1))
        h = b_cum + a_cum * carry_ref[d, 0:1, :]
        h_ref[...] = h
        last = 0 if d == 1 else u.shape[0] - 1
        carry_ref[d, 0:1, :] = h[last:last + 1, :]


def _scans_kernel(xf_ref, dtf_ref, xb_ref, dtb_ref, dtbias_ref, alog_ref, uf_ref, ub_ref, wg_ref, bg_ref, lam_ref,
                  yf_ref, yb_ref, hf_ref, hb_ref, state_ref, carry_ref):
    @pl.when(pl.program_id(1) == 0)
    def _():
        state_ref[...] = jnp.zeros_like(state_ref)
        carry_ref[...] = jnp.zeros_like(carry_ref)

    _ssd_body(xf_ref, dtf_ref, xb_ref, dtb_ref, dtbias_ref, alog_ref, yf_ref, yb_ref, state_ref)
    _lru_body(uf_ref, ub_ref, wg_ref, bg_ref, lam_ref, hf_ref, hb_ref, carry_ref)


def _scans(xbc, dt, dtbias_row, alog_row, u, wg, bg, lam, *, batch, seq, ctx_len):
    n = xbc.shape[0]
    fwd, bwd, steps = _chunk_maps(batch, seq, ctx_len)
    rows = SCAN_CHUNKS * CHUNK
    const = lambda a: pl.BlockSpec(a.shape, lambda b, k: (0,) * a.ndim)
    blk = lambda w, order: pl.BlockSpec((rows, w), order)
    wide = jax.ShapeDtypeStruct((n, 256), F32)
    return pl.pallas_call(
        _scans_kernel,
        grid=(batch, steps),
        in_specs=[blk(512, fwd), blk(LANES, fwd), blk(512, bwd), blk(LANES, bwd), const(dtbias_row), const(alog_row),
                  blk(LRU_WIDTH, fwd), blk(LRU_WIDTH, bwd), const(wg), const(bg), const(lam)],
        out_specs=(blk(256, fwd), blk(256, bwd), blk(LRU_WIDTH, fwd), blk(LRU_WIDTH, bwd)),
        out_shape=(wide, wide, wide, wide),
        scratch_shapes=[pltpu.VMEM((2, 2, CHUNK, LANES), F32), pltpu.VMEM((2, SUBLANES, LRU_WIDTH), F32)],
        compiler_params=_params(("parallel", "arbitrary")),
        name="scans",
    )(xbc, dt, xbc, dt, dtbias_row, alog_row, u, u, wg, bg, lam)


def _stack_heads(q, g):
    qf = q.astype(F32)
    lo = g * LANES
    return jnp.concatenate([qf[:, lo:lo + HEAD_DIM], qf[:, lo + HEAD_DIM:lo + LANES]], axis=0).astype(BF16)


def _value_lanes(g):
    lane = lax.broadcasted_iota(jnp.int32, (1, LANES), 1)
    return (lane < HEAD_DIM) if g == 0 else (lane >= HEAD_DIM)


def _aug_values(v, g):
    return jnp.where(_value_lanes(g), v, jnp.ones_like(v))


def _flash_init(rows, g, sink_pair):
    if sink_pair is None:
        return jnp.full((rows, 1), NEG_INF, F32), jnp.zeros((rows, LANES), F32)
    half = lax.broadcasted_iota(jnp.int32, (rows, 1), 0) < rows // 2
    m = jnp.where(half, sink_pair[0], sink_pair[1]).astype(F32)
    acc = jnp.broadcast_to(jnp.where(_value_lanes(g), 0.0, 1.0), (rows, LANES))
    return m, acc


def _flash_update(state, q2, kt, v_aug, mask=None):
    m, acc = state
    s = _dot(q2, kt)
    if mask is not None:
        s = jnp.where(mask, s, NEG_INF)
    m_new = jnp.maximum(m, jnp.max(s, axis=-1, keepdims=True))
    p = jnp.exp(s - m_new).astype(BF16)
    acc = jnp.exp(m - m_new) * acc + _dot(p, v_aug)
    return m_new, acc


def _flash_finish(states, tq):
    pieces = []
    for g, (_, acc) in enumerate(states):
        den = (1 - g) * HEAD_DIM
        o = acc[:, g * HEAD_DIM:(g + 1) * HEAD_DIM] / acc[:, den:den + 1]
        pieces += [o[:tq], o[tq:]]
    return jnp.concatenate(pieces, axis=1)


def _group_rows(g):
    return slice(g * HEAD_DIM, (g + 1) * HEAD_DIM)


def _dense_attn_kernel(*refs, tq, seg_lens, has_sink):
    refs = list(refs)
    sink_ref = refs.pop(0) if has_sink else None
    q_ref = refs.pop(0)
    o_ref = refs.pop()
    segs = [(refs[2 * i], refs[2 * i + 1], n) for i, n in enumerate(seg_lens)]
    q = q_ref[...]
    q2 = [_stack_heads(q, g) for g in range(2)]
    states = tuple(_flash_init(2 * tq, g, (sink_ref[2 * g], sink_ref[2 * g + 1]) if has_sink else None)
                   for g in range(2))
    for kt_ref, v_ref, n_keys in segs:
        if n_keys <= KV_CHUNK:
            v = v_ref[...]
            states = tuple(_flash_update(states[g], q2[g], kt_ref[_group_rows(g), :], _aug_values(v, g))
                           for g in range(2))
        else:
            def body(c, sts, kt_ref=kt_ref, v_ref=v_ref):
                off = pl.multiple_of(c * KV_CHUNK, KV_CHUNK)
                v = v_ref[pl.ds(off, KV_CHUNK), :]
                return tuple(_flash_update(sts[g], q2[g], kt_ref[_group_rows(g), pl.ds(off, KV_CHUNK)],
                                           _aug_values(v, g)) for g in range(2))
            states = lax.fori_loop(0, n_keys // KV_CHUNK, body, states, unroll=KV_UNROLL)
    o_ref[...] = _flash_finish(states, tq).astype(o_ref.dtype)


def _dense_attn(q, kt, v, sink, *, q_row0, q_len, tq, segs, batch):
    n = q.shape[0]
    qpb = q_len // tq
    q0 = q_row0 // tq
    in_specs, args = [], []
    if sink is not None:
        in_specs.append(pl.BlockSpec(memory_space=pltpu.SMEM))
        args.append(sink)
    in_specs.append(pl.BlockSpec((tq, 256), lambda b, i: (q0 + b * qpb + i, 0)))
    args.append(q)
    for row0, klen in segs:
        k0 = row0 // klen
        in_specs.append(pl.BlockSpec((LANES, klen), lambda b, i, k0=k0: (0, k0 + b)))
        in_specs.append(pl.BlockSpec((klen, LANES), lambda b, i, k0=k0: (k0 + b, 0)))
        args += [kt, v]
    return pl.pallas_call(
        functools.partial(_dense_attn_kernel, tq=tq, seg_lens=tuple(s[1] for s in segs), has_sink=sink is not None),
        grid=(batch, qpb),
        in_specs=in_specs,
        out_specs=pl.BlockSpec((tq, 256), lambda b, i: (b * qpb + i, 0)),
        out_shape=jax.ShapeDtypeStruct((batch * q_len, 256), BF16),
        compiler_params=_params(("parallel", "parallel")),
        name="dense_attn",
    )(*args)


def _window_attn_kernel(sink_ref, q_ref, ktc_ref, vc_ref, ktp_ref, vp_ref, ktm_ref, vm_ref, ktn_ref, vn_ref, o_ref,
                        *, n_tiles):
    n = pl.program_id(1)
    nsub = TQ_WINDOW // CHUNK
    iq = lax.broadcasted_iota(jnp.int32, (2 * CHUNK, CHUNK), 0) & (CHUNK - 1)
    jk = lax.broadcasted_iota(jnp.int32, (2 * CHUNK, CHUNK), 1)
    below = jk >= iq
    above = jk <= iq
    vctx = vc_ref[...]
    for j in range(nsub):
        cols = slice(j * CHUNK, (j + 1) * CHUNK)
        q = q_ref[cols, :]
        states = []
        for g in range(2):
            q2 = _stack_heads(q, g)
            rows = _group_rows(g)
            state = _flash_init(2 * CHUNK, g, (sink_ref[2 * g], sink_ref[2 * g + 1]))
            state = _flash_update(state, q2, ktc_ref[rows, :], _aug_values(vctx, g))
            state = _flash_update(state, q2, ktm_ref[rows, cols], _aug_values(vm_ref[cols, :], g))
            if j > 0:
                prev = slice((j - 1) * CHUNK, j * CHUNK)
                state = _flash_update(state, q2, ktm_ref[rows, prev], _aug_values(vm_ref[prev, :], g), below)
            else:
                state = _flash_update(state, q2, ktp_ref[rows, :], _aug_values(vp_ref[...], g),
                                      jnp.logical_and(below, n > 0))
            if j < nsub - 1:
                nxt = slice((j + 1) * CHUNK, (j + 2) * CHUNK)
                state = _flash_update(state, q2, ktm_ref[rows, nxt], _aug_values(vm_ref[nxt, :], g), above)
            else:
                state = _flash_update(state, q2, ktn_ref[rows, :], _aug_values(vn_ref[...], g),
                                      jnp.logical_and(above, n < n_tiles - 1))
            states.append(state)
        o_ref[cols, :] = _flash_finish(states, CHUNK).astype(o_ref.dtype)


def _window_attn(q, kt, v, sink, *, batch, seq, ctx_len):
    nt = seq // TQ_WINDOW
    nsub = TQ_WINDOW // CHUNK
    nb = seq // CHUNK
    ctx0 = (batch * seq) // ctx_len
    prev = lambda b, n: b * nb + jnp.maximum(n * nsub - 1, 0)
    nxt = lambda b, n: b * nb + jnp.minimum((n + 1) * nsub, nb - 1)
    return pl.pallas_call(
        functools.partial(_window_attn_kernel, n_tiles=nt),
        grid=(batch, nt),
        in_specs=[pl.BlockSpec(memory_space=pltpu.SMEM),
                  pl.BlockSpec((TQ_WINDOW, 256), lambda b, n: (b * nt + n, 0)),
                  pl.BlockSpec((LANES, ctx_len), lambda b, n: (0, ctx0 + b)),
                  pl.BlockSpec((ctx_len, LANES), lambda b, n: (ctx0 + b, 0)),
                  pl.BlockSpec((LANES, CHUNK), lambda b, n: (0, prev(b, n))),
                  pl.BlockSpec((CHUNK, LANES), lambda b, n: (prev(b, n), 0)),
                  pl.BlockSpec((LANES, TQ_WINDOW), lambda b, n: (0, b * nt + n)),
                  pl.BlockSpec((TQ_WINDOW, LANES), lambda b, n: (b * nt + n, 0)),
                  pl.BlockSpec((LANES, CHUNK), lambda b, n: (0, nxt(b, n))),
                  pl.BlockSpec((CHUNK, LANES), lambda b, n: (nxt(b, n), 0))],
        out_specs=pl.BlockSpec((TQ_WINDOW, 256), lambda b, n: (b * nt + n, 0)),
        out_shape=jax.ShapeDtypeStruct((batch * seq, 256), BF16),
        compiler_params=_params(("parallel", "parallel")),
        name="window_attn",
    )(sink, q, kt, v, kt, v, kt, v, kt, v)


def _gelu_tanh(x):
    return 0.5 * x * (1.0 + jnp.tanh(math.sqrt(2.0 / math.pi) * (x + 0.044715 * (x * x * x))))


def _outproj_kernel(x_ref, xc_ref, gate_ref, gpost_ref, oa_ref, od_ref, oac_ref, odc_ref, yf_ref, yb_ref, xs_ref,
                    z_ref, dsk_ref, gn_ref, hf_ref, hb_ref, lg_ref, w_ref,
                    fshift_ref, fscale_ref, fpre_ref, rwt_ref, rb_ref,
                    o_ref, hffn_ref, ld_ref, wk_ref, tab_ref, *, n_lat, split):
    is_lat = pl.program_id(0) * TM < n_lat
    x = jnp.where(jnp.logical_or(is_lat, not split), x_ref[...], xc_ref[...])
    oa = jnp.where(is_lat, oa_ref[...], oac_ref[...])
    od = jnp.where(is_lat, od_ref[...], odc_ref[...])
    y_ssd = (yf_ref[...] + yb_ref[...] + xs_ref[...] * dsk_ref[...]) * _silu(z_ref[...])
    ob = _rms(y_ssd, gn_ref[...])
    oc = (hf_ref[...] + hb_ref[...]) * _gelu_tanh(lg_ref[...])
    y = (_dot(oa, w_ref[0:256, :]) + _dot(ob.astype(BF16), w_ref[256:512, :])
         + _dot(oc.astype(BF16), w_ref[512:768, :]) + _dot(od, w_ref[768:1024, :]))
    x_mid = x + gate_ref[0] * _rms(y, gpost_ref[...])
    o_ref[...] = x_mid
    _route_tile(x_mid, fshift_ref, fscale_ref, fpre_ref, rwt_ref, rb_ref, hffn_ref, ld_ref, wk_ref, tab_ref)


def _outproj(x_lat, x_ctx, mod, gpost, oa, od, oa_c, od_c, yf, yb, xbc, z, dsk, gn, hf, hb, lg, w_out,
             ffn_pre, rwt, rb, *, n_rows, n_lat, seq, batch):
    d = x_lat.shape[1]
    lat_tiles = n_lat // TM
    x_specs, x_args = _row_sources(x_lat, x_ctx, n_lat)
    row = lambda w: pl.BlockSpec((TM, w), lambda i: (i, 0))
    col = pl.BlockSpec((TOP_K, TM), lambda i: (0, i))
    lat = pl.BlockSpec((TM, 256), lambda i: (jnp.minimum(i, lat_tiles - 1), 0))
    ctx = pl.BlockSpec((TM, 256), lambda i: (jnp.maximum(i - lat_tiles, 0), 0))
    const = lambda a: pl.BlockSpec(a.shape, lambda i: (0,) * a.ndim)
    return pl.pallas_call(
        functools.partial(_outproj_kernel, n_lat=n_lat, split=x_ctx is not None),
        grid=(n_rows // TM,),
        in_specs=[x_specs[0], x_specs[3], pl.BlockSpec((1, 1, d), _mod_spec(2, n_lat, seq, batch, TM)), const(gpost),
                  lat, lat, ctx, ctx, row(256), row(256), row(256), row(256), const(dsk), const(gn),
                  row(256), row(256), row(256), const(w_out),
                  pl.BlockSpec((1, 1, d), _mod_spec(3, n_lat, seq, batch, TM)),
                  pl.BlockSpec((1, 1, d), _mod_spec(4, n_lat, seq, batch, TM)),
                  const(ffn_pre), const(rwt), const(rb)],
        out_specs=(row(d), row(d), col, col, pl.BlockSpec((SUBLANES, 2 * N_EXPERTS), lambda i: (i, 0))),
        out_shape=(jax.ShapeDtypeStruct((n_rows, d), F32),
                   jax.ShapeDtypeStruct((n_rows, d), BF16),
                   jax.ShapeDtypeStruct((TOP_K, n_rows), jnp.int32),
                   jax.ShapeDtypeStruct((TOP_K, n_rows), F32),
                   jax.ShapeDtypeStruct((n_rows // TM * SUBLANES, 2 * N_EXPERTS), jnp.int32)),
        compiler_params=_params(("parallel",)),
        name="outproj_route",
    )(x_args[0], x_args[3], mod, gpost, oa, od, oa_c, od_c, yf, yb, xbc, z, dsk, gn, hf, hb, lg, w_out,
      mod, mod, ffn_pre, rwt, rb)


def _ceil_seg(c):
    return jnp.floor((c + (SEG_ALIGN - 1)) * (1.0 / SEG_ALIGN)) * SEG_ALIGN


def _route_tile(x, shift_ref, scale_ref, gpre_ref, rwt_ref, rb_ref, hb_ref, ld_ref, wk_ref, tab_ref):
    h = _rms(x, gpre_ref[...])
    h = h * (1.0 + scale_ref[0]) + shift_ref[0]
    hb = h.astype(BF16)
    hb_ref[...] = hb

    scores = jax.nn.sigmoid(_dot_nt(rwt_ref[...], hb))
    biased = scores + rb_ref[...]
    gsz = N_EXPERTS // N_EXPERT_GROUPS
    sub = lax.broadcasted_iota(jnp.int32, (gsz, TM), 0)
    blocks, gscore = [], []
    for g in range(N_EXPERT_GROUPS):
        blk = biased[g * gsz:(g + 1) * gsz, :]
        m1 = jnp.max(blk, axis=0, keepdims=True)
        first = jnp.min(jnp.where(blk == m1, sub, gsz), axis=0, keepdims=True)
        m2 = jnp.max(jnp.where(sub == first, -jnp.inf, blk), axis=0, keepdims=True)
        blocks.append(blk)
        gscore.append(m1 + m2)
    masked = []
    for g in range(N_EXPERT_GROUPS):
        rank = jnp.zeros((1, TM), F32)
        for g2 in range(N_EXPERT_GROUPS):
            if g2 == g:
                continue
            beats = (gscore[g2] > gscore[g]) | ((gscore[g2] == gscore[g]) if g2 < g else False)
            rank = rank + jnp.where(beats, 1.0, 0.0)
        masked.append(jnp.where(rank < TOPK_GROUPS, blocks[g], -jnp.inf))
    vals = jnp.concatenate(masked, axis=0)
    eidx = lax.broadcasted_iota(jnp.int32, (N_EXPERTS, TM), 0)
    self32 = jnp.zeros((N_EXPERTS, TM), F32)
    rest = vals
    for _ in range(TOP_K):
        top = jnp.max(rest, axis=0, keepdims=True)
        first = jnp.min(jnp.where(rest == top, eidx, N_EXPERTS), axis=0, keepdims=True)
        hit = eidx == first
        self32 = jnp.where(hit, 1.0, self32)
        rest = jnp.where(hit, -jnp.inf, rest)
    sel = self32 > 0.5
    picked = jnp.where(sel, scores, 0.0)
    wdense = picked / jnp.sum(picked, axis=0, keepdims=True) * ROUTED_SCALE

    tr = lax.broadcasted_iota(jnp.int32, (TM, TM), 0)
    tc = lax.broadcasted_iota(jnp.int32, (TM, TM), 1)
    before = jnp.where(tr < tc, 1.0, 0.0).astype(BF16)
    selb = self32.astype(BF16)
    pos = _dot(selb, before)
    er = lax.broadcasted_iota(jnp.int32, (N_EXPERTS, N_EXPERTS), 0)
    ec = lax.broadcasted_iota(jnp.int32, (N_EXPERTS, N_EXPERTS), 1)
    lower = jnp.where(ec < er, 1.0, 0.0).astype(BF16)
    upper = jnp.where(er < ec, 1.0, 0.0).astype(BF16)
    ksel = _dot(lower, selb)
    cnt_col = _ceil_seg(jnp.sum(self32, axis=1, keepdims=True))
    loc_col = _dot3_left(lower, jnp.broadcast_to(cnt_col, (N_EXPERTS, LANES)))[:, 0:1]
    cnt_row = _ceil_seg(_dot_nt(jnp.ones((SUBLANES, TM), BF16), selb))
    loc_row = _dot3(cnt_row, upper)
    tab_ref[...] = jnp.concatenate([cnt_row, loc_row], axis=1).astype(jnp.int32)

    r8 = lax.broadcasted_iota(jnp.int32, (TOP_K, TM), 0)
    ld = jnp.zeros((TOP_K, TM), F32)
    wk = jnp.zeros((TOP_K, TM), F32)
    stage_row = pos + loc_col
    for k in range(TOP_K):
        one = sel & (ksel == float(k))
        ld = jnp.where(r8 == k, jnp.sum(jnp.where(one, stage_row, 0.0), axis=0, keepdims=True), ld)
        wk = jnp.where(r8 == k, jnp.sum(jnp.where(one, wdense, 0.0), axis=0, keepdims=True), wk)
    ld_ref[...] = ld.astype(jnp.int32)
    wk_ref[...] = wk


def _pow2_pieces(limit):
    bits, b = [], limit
    while b >= SEG_ALIGN:
        bits.append(b)
        b //= 2
    return bits


def _copy_pieces(n, src_ref, src0, dst_ref, dst0, sem, limit, wait, same_src=False):
    for bit in _pow2_pieces(limit):
        @pl.when((n & bit) != 0)
        def _():
            off = n & ~(2 * bit - 1)
            cp = pltpu.make_async_copy(src_ref.at[pl.ds(pl.multiple_of(src0 + (0 if same_src else off), SEG_ALIGN),
                                                          bit)],
                                       dst_ref.at[pl.ds(pl.multiple_of(dst0 + off, SEG_ALIGN), bit)], sem)
            cp.wait() if wait else cp.start()


N_PIECE_TABS = 7


def _piece_copies(tile, tabs, stage_ref, slots_ref, sem, to_slots, wait):
    nbig_ref, nsmall_ref, _, bsrc_ref, bdst_ref, ssrc_ref, sdst_ref = tabs
    for rows, n_ref, a_ref, b_ref, cap in ((BIG_PIECE, nbig_ref, bsrc_ref, bdst_ref, BIG_MAX),
                                           (SEG_ALIGN, nsmall_ref, ssrc_ref, sdst_ref, SMALL_MAX)):
        def body(p, c, rows=rows, a_ref=a_ref, b_ref=b_ref, cap=cap):
            src = stage_ref.at[pl.ds(pl.multiple_of(a_ref[tile * cap + p], SEG_ALIGN), rows)]
            dst = slots_ref.at[pl.ds(pl.multiple_of(b_ref[tile * cap + p], SEG_ALIGN), rows)]
            cp = pltpu.make_async_copy(src, dst, sem) if to_slots else pltpu.make_async_copy(dst, src, sem)
            cp.wait() if wait else cp.start()
            return c
        lax.fori_loop(0, n_ref[tile], body, 0)


def _used_blocks(tile, tabs):
    return (tabs[2][tile] + TM - 1) // TM


def _for_used_blocks(used, body):
    for b in range(TOP_K):
        body(b)
    for b in range(TOP_K, STAGE_ROWS // TM):
        @pl.when(b < used)
        def _():
            body(b)


def _stage_rows_iota():
    return lax.broadcasted_iota(jnp.int32, (TM // 2, TM), 0).astype(F32).astype(BF16)


def _pick_matrix(ld, base, vals, jrow):
    rel = (ld - base).astype(F32)
    rel = jnp.where(jnp.logical_and(rel >= 0.0, rel < TM // 2), rel, -1.0).astype(BF16)
    out = jnp.zeros((TM // 2, TM), BF16)
    for k in range(TOP_K):
        out = jnp.where(rel[k:k + 1, :] == jrow, vals[k:k + 1, :], out)
    return out


def _dispatch_kernel(*refs):
    tabs = refs[:N_PIECE_TABS]
    pstart_ref, npad_ref, hb_ref, ld_ref, xs_ref, stage, zbuf, sem, zsem = refs[N_PIECE_TABS:]
    i = pl.program_id(0)

    @pl.when(i == 0)
    def _():
        zbuf[...] = jnp.zeros_like(zbuf)
        for wait in (False, True):
            def body(e, c, wait=wait):
                _copy_pieces(npad_ref[e], zbuf, 0, xs_ref, pstart_ref[e], zsem, BM_EXPERT // 2, wait, same_src=True)
                return c
            lax.fori_loop(0, N_EXPERTS, body, 0)

    ld = ld_ref[...]
    hb = hb_ref[...]
    jrow = _stage_rows_iota()
    ones = jnp.ones((TOP_K, TM), BF16)

    cur = stage.at[i & 1]

    def block(b):
        for half in range(2):
            base = b * TM + half * (TM // 2)
            cur[base:base + TM // 2, :] = _dot(_pick_matrix(ld, base, ones, jrow), hb).astype(BF16)

    _for_used_blocks(_used_blocks(i, tabs), block)

    @pl.when(i > 0)
    def _():
        _piece_copies(i - 1, tabs, stage.at[(i - 1) & 1], xs_ref, sem, True, True)

    _piece_copies(i, tabs, cur, xs_ref, sem, True, False)

    @pl.when(i == pl.num_programs(0) - 1)
    def _():
        _piece_copies(i, tabs, cur, xs_ref, sem, True, True)


def _dispatch(tabs, pad_start, n_pad, hb, ld, n_slots):
    n, d = hb.shape
    grid_spec = pltpu.PrefetchScalarGridSpec(
        num_scalar_prefetch=N_PIECE_TABS + 2,
        grid=(n // TM,),
        in_specs=[pl.BlockSpec((TM, d), lambda i, *_: (i, 0)),
                  pl.BlockSpec((TOP_K, TM), lambda i, *_: (0, i))],
        out_specs=pl.BlockSpec(memory_space=pl.ANY),
        scratch_shapes=[pltpu.VMEM((2, STAGE_ROWS, d), BF16), pltpu.VMEM((BM_EXPERT // 2, d), BF16),
                        pltpu.SemaphoreType.DMA(()), pltpu.SemaphoreType.DMA(())],
    )
    return pl.pallas_call(
        _dispatch_kernel,
        grid_spec=grid_spec,
        out_shape=jax.ShapeDtypeStruct((n_slots, d), BF16),
        compiler_params=_params(("arbitrary",)),
        name="moe_dispatch",
    )(*tabs, pad_start, n_pad, hb, ld)


def _expert_kernel(be_ref, na_ref, nxt_ref, slot_ref, xs_hbm, wg_hbm, wu_hbm, wd_hbm, ys_ref,
                   xbuf, wg_raw, wu_raw, wd_raw, wgub, wdb, xsem, wsem, *, layer):
    i = pl.program_id(0)
    n_act = na_ref[0]

    def weight_copies(e, slot):
        return [pltpu.make_async_copy(src.at[layer, e], dst.at[slot], wsem.at[slot, j])
                for j, (src, dst) in enumerate(((wg_hbm, wg_raw), (wu_hbm, wu_raw), (wd_hbm, wd_raw)))]

    def rows_copy(blk):
        buf = lax.rem(blk, X_BUFFERS)
        return pltpu.make_async_copy(xs_hbm.at[pl.ds(pl.multiple_of(blk * BM_EXPERT, BM_EXPERT), BM_EXPERT)],
                                     xbuf.at[buf], xsem.at[buf])

    @pl.when(i < n_act)
    def _():
        e, slot = be_ref[i], slot_ref[i]

        @pl.when(i == 0)
        def _():
            for cp in weight_copies(e, slot):
                cp.start()
            for j in range(X_BUFFERS - 1):
                @pl.when(j < n_act)
                def _():
                    rows_copy(j).start()

        @pl.when(i + (X_BUFFERS - 1) < n_act)
        def _():
            rows_copy(i + (X_BUFFERS - 1)).start()

        @pl.when(jnp.logical_or(i == 0, e != be_ref[jnp.maximum(i - 1, 0)]))
        def _():
            for cp in weight_copies(e, slot):
                cp.wait()
            wgub[:, :EXPERT_HIDDEN] = wg_raw[slot].astype(BF16)
            wgub[:, EXPERT_HIDDEN:] = wu_raw[slot].astype(BF16)
            wdb[...] = wd_raw[slot].astype(BF16)

            @pl.when(nxt_ref[i] != e)
            def _():
                for cp in weight_copies(nxt_ref[i], 1 - slot):
                    cp.start()

        rows_copy(i).wait()
        gu = _dot(xbuf[lax.rem(i, X_BUFFERS)], wgub[...])
        hid = _silu(gu[:, :EXPERT_HIDDEN]) * gu[:, EXPERT_HIDDEN:]
        ys_ref[...] = _dot(hid.astype(BF16), wdb[...]).astype(ys_ref.dtype)


def _experts(block_e, n_active, next_e, w_slot, xs, wg, wu, wd, layer):
    n_slots, d = xs.shape
    nb = n_slots // BM_EXPERT
    tiles = pl.BlockSpec((BM_EXPERT, d), lambda i, be, na, *_: (jnp.minimum(i, na[0] - 1), 0))
    anywhere = pl.BlockSpec(memory_space=pl.ANY)
    grid_spec = pltpu.PrefetchScalarGridSpec(
        num_scalar_prefetch=4,
        grid=(nb,),
        in_specs=[anywhere, anywhere, anywhere, anywhere],
        out_specs=tiles,
        scratch_shapes=[pltpu.VMEM((X_BUFFERS, BM_EXPERT, d), BF16),
                        pltpu.VMEM((2, d, EXPERT_HIDDEN), F32), pltpu.VMEM((2, d, EXPERT_HIDDEN), F32),
                        pltpu.VMEM((2, EXPERT_HIDDEN, d), F32),
                        pltpu.VMEM((d, 2 * EXPERT_HIDDEN), BF16),
                        pltpu.VMEM((EXPERT_HIDDEN, d), BF16),
                        pltpu.SemaphoreType.DMA((X_BUFFERS,)), pltpu.SemaphoreType.DMA((2, 3))],
    )
    return pl.pallas_call(
        functools.partial(_expert_kernel, layer=layer),
        grid_spec=grid_spec,
        out_shape=jax.ShapeDtypeStruct((n_slots, d), BF16),
        compiler_params=_params(("arbitrary",)),
        name="moe_experts",
    )(block_e, n_active, next_e, w_slot, xs, wg, wu, wd)


def _combine_kernel(*refs):
    tabs = refs[:N_PIECE_TABS]
    (ys_ref, ld_ref, wk_ref, hb_ref, x_ref, gate_ref, gpost_ref, sg_ref, su_ref, sd_ref, o_ref,
     stage, acc_ref, sem) = refs[N_PIECE_TABS:]
    i = pl.program_id(0)

    @pl.when(i == 0)
    def _():
        stage[...] = jnp.zeros_like(stage)
        _piece_copies(0, tabs, stage.at[0], ys_ref, sem, False, False)

    cur = stage.at[i & 1]
    _piece_copies(i, tabs, cur, ys_ref, sem, False, True)

    @pl.when(i + 1 < pl.num_programs(0))
    def _():
        _piece_copies(i + 1, tabs, stage.at[(i + 1) & 1], ys_ref, sem, False, False)

    hb = hb_ref[...]
    acc_ref[...] = _dot((_silu(_dot(hb, sg_ref[...])) * _dot(hb, su_ref[...])).astype(BF16), sd_ref[...])

    ld = ld_ref[...]
    wkb = wk_ref[...].astype(BF16)
    jrow = _stage_rows_iota()

    def block(b):
        for half in range(2):
            base = b * TM + half * (TM // 2)
            weights = _pick_matrix(ld, base, wkb, jrow)
            acc_ref[...] += lax.dot_general(weights, cur[base:base + TM // 2, :], (((0,), (0,)), ((), ())),
                                            preferred_element_type=F32)

    _for_used_blocks(_used_blocks(i, tabs), block)
    o_ref[...] = x_ref[...] + gate_ref[0] * _rms(acc_ref[...], gpost_ref[...])


def _combine(tabs, ys, ld, wk, hb, xu, mod, gpost, sg, su, sd, *, n_rows, n_lat, seq, batch):
    d = xu.shape[1]
    row = lambda w: pl.BlockSpec((TM, w), lambda i, *_: (i, 0))
    col = pl.BlockSpec((TOP_K, TM), lambda i, *_: (0, i))
    const = lambda a: pl.BlockSpec(a.shape, lambda i, *_: (0,) * a.ndim)
    mod_map = _mod_spec(5, n_lat, seq, batch, TM)
    grid_spec = pltpu.PrefetchScalarGridSpec(
        num_scalar_prefetch=N_PIECE_TABS,
        grid=(n_rows // TM,),
        in_specs=[pl.BlockSpec(memory_space=pl.ANY), col, col, row(d), row(d),
                  pl.BlockSpec((1, 1, d), lambda i, *_: mod_map(i)),
                  const(gpost), const(sg), const(su), const(sd)],
        out_specs=row(d),
        scratch_shapes=[pltpu.VMEM((2, STAGE_ROWS, d), BF16), pltpu.VMEM((TM, d), F32), pltpu.SemaphoreType.DMA(())],
    )
    return pl.pallas_call(
        _combine_kernel,
        grid_spec=grid_spec,
        out_shape=jax.ShapeDtypeStruct((n_rows, d), F32),
        compiler_params=_params(("arbitrary",)),
        name="moe_combine",
    )(*tabs, ys, ld, wk, hb, xu, mod, gpost, sg, su, sd)


def _deinterleave(w):
    cols = w.shape[-1]
    perm = jnp.concatenate([jnp.arange(0, HEAD_DIM, 2), jnp.arange(1, HEAD_DIM, 2)])
    idx = (jnp.arange(cols // HEAD_DIM)[:, None] * HEAD_DIM + perm[None, :]).reshape(-1)
    return w[..., idx]


def _pad_in_proj(w_in):
    d = w_in.shape[0]
    o = 0
    parts = {}
    for name, width in (("qa", 256), ("ka", 128), ("va", 128), ("z", 256), ("xs", 256), ("bm", 128), ("cm", 128),
                        ("dtf", 4), ("dtb", 4), ("lx", 256), ("lg", 256), ("qd", 256), ("kd", 128), ("vd", 128)):
        parts[name] = w_in[:, o:o + width]
        o += width
    dt = jnp.concatenate([parts["dtf"], parts["dtb"], jnp.zeros((d, LANES - 8), w_in.dtype)], axis=1)
    cols = [_deinterleave(parts["qa"]), _deinterleave(parts["ka"]), parts["va"],
            _deinterleave(parts["qd"]), _deinterleave(parts["kd"]), parts["vd"],
            parts["z"], parts["xs"], parts["bm"], parts["cm"], dt, parts["lx"], parts["lg"]]
    return jnp.concatenate(cols, axis=1).astype(BF16)


def _rope_tables(seq):
    t = jnp.arange(seq)
    rowp = (t // GRID_W).astype(F32)
    colp = (t % GRID_W).astype(F32)
    axis_dim = HEAD_DIM // 2
    inv_freq = ROPE_THETA ** (-jnp.arange(0, axis_dim, 2, dtype=F32) / axis_dim)
    ang = jnp.concatenate([rowp[:, None] * inv_freq, colp[:, None] * inv_freq], axis=-1)
    cos, sin = jnp.cos(ang), jnp.sin(ang)
    cos_h = jnp.concatenate([cos, cos], axis=-1)
    sin_h = jnp.concatenate([-sin, sin], axis=-1)
    return jnp.tile(cos_h, (1, 4)), jnp.tile(sin_h, (1, 4))


def _block_diag(w):
    nb, bd, _ = w.shape
    eye = jnp.eye(nb, dtype=w.dtype)
    return (eye[:, None, :, None] * w[:, :, None, :]).reshape(nb * bd, nb * bd)


def _piece_table(counts, cap, stage0, slot0, rows, ids):
    ends = jnp.cumsum(counts, axis=1)
    q = jnp.arange(cap, dtype=jnp.int32)
    owner = jnp.sum((ends[:, None, :] <= q[None, :, None]).astype(jnp.int32), axis=-1)
    mine = owner[:, :, None] == ids
    pick = lambda v: jnp.sum(jnp.where(mine, v[:, None, :], 0), axis=-1)
    step = rows * (q[None, :] - pick(ends - counts))
    return (pick(stage0) + step).reshape(-1), (pick(slot0) + step).reshape(-1)


def _lane_row(fwd, bwd):
    return jnp.concatenate([fwd, bwd, jnp.zeros((LANES - 8,), F32)]).reshape(1, LANES)


def kernel(x, c, ctx, c_ctx, w_ada, b_ada, g_mix_pre, g_mix_post, g_ffn_pre, g_ffn_post, w_in, w_out, a_sink,
           ssd_conv_w, ssd_conv_b, ssd_dt_bias, ssd_a_log, ssd_d, ssd_norm, lru_conv_w, lru_conv_b, lru_w_a,
           lru_b_a, lru_w_i, lru_b_i, lru_lambda, d_q_norm, d_k_norm, router_w, router_bias, exp_w_gate,
           exp_w_up, exp_w_down, sh_w_gate, sh_w_up, sh_w_down):
    batch, seq, d = x.shape
    ctx_len = ctx.shape[1]
    depth = w_ada.shape[0]
    n_lat = batch * seq
    n_ctx = batch * ctx_len
    n_all = n_lat + n_ctx
    assert seq % TM == 0 and n_ctx % TM == 0 and ctx_len >= LRU_CONV
    assert seq % TQ_WINDOW == 0 and seq % (SCAN_CHUNKS * CHUNK) == 0 and ctx_len % (SCAN_CHUNKS * CHUNK) == 0
    assert ctx_len <= KV_CHUNK and seq % KV_CHUNK == 0 and seq % TQ_GLOBAL == 0 and batch + 1 <= SUBLANES

    xu, xu_ctx = x.reshape(n_lat, d), ctx.reshape(n_ctx, d)
    cin = jnp.concatenate([c, c_ctx[None, :], jnp.zeros((SUBLANES - batch - 1, d), F32)], axis=0)
    mod_all = _adaln(cin, w_ada, b_ada)
    cos_t, sin_t = _rope_tables(seq)
    hm = jnp.kron(jnp.eye(4, dtype=F32), jnp.full((HEAD_DIM, HEAD_DIM), 1.0 / HEAD_DIM, F32)).astype(BF16)

    for l in range(depth):
        with_ctx = l < depth - 1
        mod = mod_all[l].reshape(SUBLANES * 6, 1, d)
        gq = jnp.tile(_deinterleave(d_q_norm[l]), 4).reshape(1, 256)
        gk = jnp.tile(_deinterleave(d_k_norm[l]), 2).reshape(1, LANES)
        qa, kat, va, qd, kdt, vd, z, xbc, dt, lu, lg = _inproj(
            xu, xu_ctx, mod, g_mix_pre[l].reshape(1, d), _pad_in_proj(w_in[l]), cos_t, sin_t, gq, gk, hm,
            ssd_conv_w[l], ssd_conv_b[l].reshape(1, -1), lru_conv_w[l], lru_conv_b[l].reshape(1, -1),
            n_lat=n_lat, seq=seq, ctx_len=ctx_len, batch=batch)
        wg = jnp.stack([jnp.concatenate([_block_diag(lru_w_a[l, dd]), _block_diag(lru_w_i[l, dd])], axis=1)
                        for dd in range(2)]).astype(BF16)
        bg = jnp.concatenate([lru_b_a[l], lru_b_i[l]], axis=1).reshape(2, 1, 2 * LRU_WIDTH)
        yf, yb, hf, hb = _scans(xbc, dt, _lane_row(ssd_dt_bias[l, 0], ssd_dt_bias[l, 1]),
                                _lane_row(ssd_a_log[l, 0], ssd_a_log[l, 1]), lu, wg, bg,
                                lru_lambda[l].reshape(2, 1, LRU_WIDTH), batch=batch, seq=seq, ctx_len=ctx_len)

        oa = _window_attn(qa, kat, va, a_sink[l], batch=batch, seq=seq, ctx_len=ctx_len)
        od = _dense_attn(qd, kdt, vd, None, q_row0=0, q_len=seq, tq=TQ_GLOBAL,
                         segs=[(n_lat, ctx_len), (0, seq)], batch=batch)
        if with_ctx:
            oa_c = _dense_attn(qa, kat, va, a_sink[l], q_row0=n_lat, q_len=ctx_len, tq=ctx_len,
                               segs=[(n_lat, ctx_len)], batch=batch)
            od_c = _dense_attn(qd, kdt, vd, None, q_row0=n_lat, q_len=ctx_len, tq=ctx_len,
                               segs=[(n_lat, ctx_len)], batch=batch)
        else:
            oa_c, od_c = oa, od
        n_rows = n_all if with_ctx else n_lat

        dsk = jnp.repeat(ssd_d[l], HEAD_DIM).reshape(1, 256)
        xu_mid, hb_ffn, ld, wk, tab = _outproj(
            xu, xu_ctx, mod, g_mix_post[l].reshape(1, d), oa, od, oa_c, od_c, yf, yb, xbc, z, dsk,
            ssd_norm[l].reshape(1, 256), hf, hb, lg, w_out[l].astype(BF16),
            g_ffn_pre[l].reshape(1, d), router_w[l].T.astype(BF16), router_bias[l].reshape(N_EXPERTS, 1),
            n_rows=n_rows, n_lat=n_lat, seq=seq, batch=batch)
        n_tiles = n_rows // TM
        tab = tab.reshape(n_tiles, SUBLANES, 2 * N_EXPERTS)[:, 0, :]
        seg_cnt, seg_loc = tab[:, :N_EXPERTS], tab[:, N_EXPERTS:]
        counts = jnp.sum(seg_cnt, axis=0)
        padded = (counts + BM_EXPERT - 1) // BM_EXPERT * BM_EXPERT
        padded_end = jnp.cumsum(padded)
        offs = padded_end - padded
        seg_off = offs[None, :] + jnp.cumsum(seg_cnt, axis=0) - seg_cnt
        n_blocks = (n_rows * TOP_K + n_tiles * N_EXPERTS * SEG_ALIGN) // BM_EXPERT + N_EXPERTS
        n_active = (padded_end[-1] // BM_EXPERT).astype(jnp.int32).reshape(1)
        block_start = jnp.arange(n_blocks, dtype=jnp.int32) * BM_EXPERT
        block_e = jnp.minimum(jnp.sum((padded_end[None, :] <= block_start[:, None]).astype(jnp.int32), axis=1),
                              N_EXPERTS - 1)
        ids = jnp.arange(N_EXPERTS, dtype=jnp.int32)
        n_big = seg_cnt // BIG_PIECE
        n_small = (seg_cnt % BIG_PIECE) // SEG_ALIGN
        tabs = (jnp.sum(n_big, axis=1), jnp.sum(n_small, axis=1), seg_loc[:, -1] + seg_cnt[:, -1],
                *_piece_table(n_big, BIG_MAX, seg_loc, seg_off, BIG_PIECE, ids),
                *_piece_table(n_small, SMALL_MAX, seg_loc + n_big * BIG_PIECE, seg_off + n_big * BIG_PIECE,
                              SEG_ALIGN, ids))
        xs = _dispatch(tabs, offs + counts, padded - counts, hb_ffn, ld, n_blocks * BM_EXPERT)
        has_rows = padded > 0
        later = jnp.logical_and(ids[None, :] > ids[:, None], has_rows[None, :])
        nxt_of = jnp.min(jnp.where(later, ids[None, :], N_EXPERTS), axis=1)
        nxt_of = jnp.where(nxt_of == N_EXPERTS, ids, nxt_of)
        slot_of = (jnp.cumsum(has_rows.astype(jnp.int32)) - 1) & 1
        own = block_e[:, None] == ids[None, :]
        next_e = jnp.sum(jnp.where(own, nxt_of[None, :], 0), axis=1)
        w_slot = jnp.sum(jnp.where(own, slot_of[None, :], 0), axis=1)
        ys = _experts(block_e, n_active, next_e, w_slot, xs, exp_w_gate, exp_w_up, exp_w_down, l)
        xu = _combine(tabs, ys, ld, wk, hb_ffn, xu_mid, mod, g_ffn_post[l].reshape(1, d), sh_w_gate[l].astype(BF16),
                      sh_w_up[l].astype(BF16), sh_w_down[l].astype(BF16),
                      n_rows=n_rows, n_lat=n_lat, seq=seq, batch=batch)
        xu_ctx = None
    return xu[:n_lat].reshape(batch, seq, d)
```

```python
import functools
import math

import jax
import jax.numpy as jnp
from jax import lax
from jax.experimental import pallas as pl
from jax.experimental.pallas import tpu as pltpu

F32 = jnp.float32
BF16 = jnp.bfloat16

HEAD_DIM = 64
GRID_W = 64
ROPE_THETA = 10000.0
NORM_EPS = 1e-6
NEG_INF = -1e30
A_HEADS, A_KV_HEADS, WINDOW = 4, 2, 128
SSD_HEADS, SSD_GROUPS, SSD_STATE, SSD_CONV = 4, 2, 64, 4
LRU_WIDTH, LRU_BLOCKS, LRU_CONV, LRU_C = 256, 4, 4, 8.0
D_HEADS, D_KV_HEADS = 4, 2
N_EXPERTS, N_EXPERT_GROUPS, TOPK_GROUPS, TOP_K = 64, 8, 4, 8
EXPERT_HIDDEN, SHARED_HIDDEN = 256, 256
ROUTED_SCALE = 2.5

LANES = 128
SUBLANES = 8

TM = 512
CHUNK = 128
SCAN_CHUNKS = 2
TQ_GLOBAL = 256
TQ_WINDOW = 512
KV_CHUNK = 256
KV_UNROLL = 16
BM_EXPERT = 512
X_BUFFERS = 3
SEG_ALIGN = 16
STAGE_ROWS = TM * TOP_K + N_EXPERTS * SEG_ALIGN
BIG_PIECE = 64
BIG_MAX = STAGE_ROWS // BIG_PIECE
SMALL_MAX = N_EXPERTS * (BIG_PIECE // SEG_ALIGN - 1)
VMEM_LIMIT = 48 * 1024 * 1024

C_QA, C_KA, C_VA = 0, 256, 384
C_QD, C_KD, C_VD = 512, 768, 896
C_Z, C_XBC, C_DT = 1024, 1280, 1792
C_LX, C_LG = 1920, 2176
NP_IN = 2432


def _dot(a, b):
    return jnp.dot(a, b, preferred_element_type=F32)


def _dot_nt(a, b):
    return lax.dot_general(a, b, (((1,), (1,)), ((), ())), preferred_element_type=F32)


def _dot3(a, b):
    a1 = a.astype(BF16)
    r1 = a - a1.astype(F32)
    a2 = r1.astype(BF16)
    a3 = (r1 - a2.astype(F32)).astype(BF16)
    return _dot(a1, b) + _dot(a2, b) + _dot(a3, b)


def _dot3_left(a, b):
    b1 = b.astype(BF16)
    r1 = b - b1.astype(F32)
    b2 = r1.astype(BF16)
    b3 = (r1 - b2.astype(F32)).astype(BF16)
    return _dot(a, b1) + _dot(a, b2) + _dot(a, b3)


def _silu(x):
    return x * jax.nn.sigmoid(x)


def _softplus(x):
    return jnp.maximum(x, 0.0) + jnp.log1p(jnp.exp(-jnp.abs(x)))


def _rms(x, gain):
    return x * lax.rsqrt(jnp.mean(x * x, axis=-1, keepdims=True) + NORM_EPS) * gain


def _params(sem=None):
    return pltpu.CompilerParams(dimension_semantics=sem, vmem_limit_bytes=VMEM_LIMIT)


def _adaln_kernel(c_ref, w_ref, b_ref, o_ref):
    s = _silu(c_ref[...])
    o_ref[0] = _dot(s.astype(BF16), w_ref[0].astype(BF16)) + b_ref[0]


def _adaln(cin, w_ada, b_ada):
    depth, d, n6 = w_ada.shape
    tn = 1024
    return pl.pallas_call(
        _adaln_kernel,
        grid=(depth, n6 // tn),
        in_specs=[pl.BlockSpec((SUBLANES, d), lambda l, j: (0, 0)),
                  pl.BlockSpec((1, d, tn), lambda l, j: (l, 0, j)),
                  pl.BlockSpec((1, 1, tn), lambda l, j: (l, 0, j))],
        out_specs=pl.BlockSpec((1, SUBLANES, tn), lambda l, j: (l, 0, j)),
        out_shape=jax.ShapeDtypeStruct((depth, SUBLANES, n6), F32),
        compiler_params=_params(("parallel", "parallel")),
        name="adaln",
    )(cin, w_ada, b_ada.reshape(depth, 1, n6))


def _swap_halves(t):
    w = t.shape[1]
    lane = lax.broadcasted_iota(jnp.int32, (1, w), 1)
    first = (lane & 32) == 0
    return jnp.where(first, pltpu.roll(t, w - 32, axis=1), pltpu.roll(t, 32, axis=1))


def _inproj_kernel(x_ref, xp_ref, xn_ref, xc_ref, xcp_ref, xcn_ref, shift_ref, scale_ref, gpre_ref, w_ref,
                   cos_ref, sin_ref, gq_ref, gk_ref, hm_ref, ws_ref, bs_ref, wl_ref, bl_ref,
                   qa_ref, kat_ref, va_ref, qd_ref, kdt_ref, vd_ref, z_ref, xbc_ref, dt_ref, lu_ref, lg_ref,
                   *, n_lat, seq, ctx_len, split):
    i = pl.program_id(0)
    is_lat = i * TM < n_lat
    first = jnp.logical_or(is_lat, not split)

    def pre(a_ref, b_ref):
        x = jnp.where(first, a_ref[...], b_ref[...])
        return _rms(x, gpre_ref[...]) * (1.0 + scale_ref[0]) + shift_ref[0]

    h = pre(x_ref, xc_ref)
    hb = h.astype(BF16)
    hb_ext = jnp.concatenate([pre(xp_ref, xcp_ref), h, pre(xn_ref, xcn_ref)], axis=0).astype(BF16)

    def sec(a, b):
        return _dot(hb, w_ref[:, a:b])

    row = lax.broadcasted_iota(jnp.int32, (TM, 1), 0)
    pos = jnp.where(is_lat, lax.rem(i * TM + row, seq), lax.rem(i * TM - n_lat + row, ctx_len))
    slen = jnp.where(is_lat, seq, ctx_len)

    def conv(a, b, w, bias):
        ext = _dot(hb_ext, w_ref[:, a:b])
        prev, x, nxt = ext[0:SUBLANES], ext[SUBLANES:SUBLANES + TM], ext[SUBLANES + TM:]
        xm1 = jnp.where(row == 0, prev[7:8, :], pltpu.roll(x, 1, axis=0))
        xm2 = jnp.where(row == 0, prev[6:7, :], jnp.where(row == 1, prev[7:8, :], pltpu.roll(x, 2, axis=0)))
        xp1 = jnp.where(row == TM - 1, nxt[0:1, :], pltpu.roll(x, TM - 1, axis=0))
        xm1 = jnp.where(pos >= 1, xm1, 0.0)
        xm2 = jnp.where(pos >= 2, xm2, 0.0)
        xp1 = jnp.where(pos <= slen - 2, xp1, 0.0)
        return w[0:1, :] * xm2 + w[1:2, :] * xm1 + w[2:3, :] * x + w[3:4, :] * xp1 + bias

    cos = jnp.where(is_lat, cos_ref[...], 1.0)
    sin = jnp.where(is_lat, sin_ref[...], 0.0)

    def rope(t):
        w = t.shape[1]
        return t * cos[:, :w] + _swap_halves(t) * sin[:, :w]

    def head_norm(t, gain):
        w = t.shape[1]
        ms = _dot3(t * t, hm_ref[:w, :w])
        return t * lax.rsqrt(ms + NORM_EPS) * gain

    scale = HEAD_DIM ** -0.5
    qa_ref[...] = (rope(sec(C_QA, C_KA)) * scale).astype(BF16)
    kat_ref[...] = rope(sec(C_KA, C_VA)).T.astype(BF16)
    va_ref[...] = sec(C_VA, C_QD).astype(BF16)
    qd_ref[...] = (rope(head_norm(sec(C_QD, C_KD), gq_ref[...])) * scale).astype(BF16)
    kdt_ref[...] = rope(head_norm(sec(C_KD, C_VD), gk_ref[...])).T.astype(BF16)
    vd_ref[...] = sec(C_VD, C_Z).astype(BF16)
    z_ref[...] = sec(C_Z, C_XBC)
    xbc_ref[...] = _silu(conv(C_XBC, C_DT, ws_ref[...], bs_ref[...]))
    dt_ref[...] = sec(C_DT, C_LX)
    lu_ref[...] = conv(C_LX, C_LG, wl_ref[...], bl_ref[...])
    lg_ref[...] = sec(C_LG, NP_IN)


def _mod_spec(chunk, n_lat, seq, batch, tile):
    def imap(i):
        row0 = i * tile
        seg = jnp.where(row0 < n_lat, row0 // seq, batch)
        return (seg * 6 + chunk, 0, 0)
    return imap


def _row_sources(x_lat, x_ctx, n_lat):
    d = x_lat.shape[1]
    r8 = TM // SUBLANES
    lat_tiles = n_lat // TM

    def specs(arr, tile_of):
        n8 = arr.shape[0] // SUBLANES
        tiles = arr.shape[0] // TM
        t = lambda i: jnp.clip(tile_of(i), 0, tiles - 1)
        return [pl.BlockSpec((TM, d), lambda i: (t(i), 0)),
                pl.BlockSpec((SUBLANES, d), lambda i: (jnp.clip(t(i) * r8 - 1, 0, n8 - 1), 0)),
                pl.BlockSpec((SUBLANES, d), lambda i: (jnp.clip((t(i) + 1) * r8, 0, n8 - 1), 0))]

    if x_ctx is None:
        return specs(x_lat, lambda i: i) + specs(x_lat, lambda i: 0 * i), [x_lat] * 6
    return specs(x_lat, lambda i: i) + specs(x_ctx, lambda i: i - lat_tiles), [x_lat] * 3 + [x_ctx] * 3


def _inproj(x_lat, x_ctx, mod, gpre, w_pad, cos_t, sin_t, gq, gk, hm, ws, bs, wl, bl, *, n_lat, seq, ctx_len, batch):
    n = x_lat.shape[0] + (0 if x_ctx is None else x_ctx.shape[0])
    d = x_lat.shape[1]
    nt = n // TM
    spt = seq // TM
    x_specs, x_args = _row_sources(x_lat, x_ctx, n_lat)
    row = lambda w: pl.BlockSpec((TM, w), lambda i: (i, 0))
    colT = pl.BlockSpec((LANES, TM), lambda i: (0, i))
    const = lambda a: pl.BlockSpec(a.shape, lambda i: (0,) * a.ndim)
    out_shapes = (
        jax.ShapeDtypeStruct((n, 256), BF16), jax.ShapeDtypeStruct((LANES, n), BF16),
        jax.ShapeDtypeStruct((n, LANES), BF16),
        jax.ShapeDtypeStruct((n, 256), BF16), jax.ShapeDtypeStruct((LANES, n), BF16),
        jax.ShapeDtypeStruct((n, LANES), BF16),
        jax.ShapeDtypeStruct((n, 256), F32), jax.ShapeDtypeStruct((n, 512), F32),
        jax.ShapeDtypeStruct((n, LANES), F32), jax.ShapeDtypeStruct((n, 256), F32),
        jax.ShapeDtypeStruct((n, 256), F32))
    return pl.pallas_call(
        functools.partial(_inproj_kernel, n_lat=n_lat, seq=seq, ctx_len=ctx_len, split=x_ctx is not None),
        grid=(nt,),
        in_specs=x_specs + [
                  pl.BlockSpec((1, 1, d), _mod_spec(0, n_lat, seq, batch, TM)),
                  pl.BlockSpec((1, 1, d), _mod_spec(1, n_lat, seq, batch, TM)),
                  const(gpre), const(w_pad),
                  pl.BlockSpec((TM, 256), lambda i: (i % spt, 0)),
                  pl.BlockSpec((TM, 256), lambda i: (i % spt, 0)),
                  const(gq), const(gk), const(hm), const(ws), const(bs), const(wl), const(bl)],
        out_specs=(row(256), colT, row(LANES), row(256), colT, row(LANES),
                   row(256), row(512), row(LANES), row(256), row(256)),
        out_shape=out_shapes,
        compiler_params=_params(("parallel",)),
        name="inproj",
    )(*x_args, mod, mod, gpre, w_pad, cos_t, sin_t, gq, gk, hm, ws, bs, wl, bl)


def _chunk_maps(batch, seq, ctx_len):
    ncx = ctx_len // (SCAN_CHUNKS * CHUNK)
    nl = seq // (SCAN_CHUNKS * CHUNK)
    lat_blocks = batch * nl

    def block(b, c):
        return jnp.where(c < ncx, lat_blocks + b * ncx + c, b * nl + (c - ncx))

    def fwd(b, k):
        return (block(b, k), 0)

    def bwd(b, k):
        c = jnp.where(k < ncx, ncx - 1 - k, ncx + (nl - 1 - (k - ncx)))
        return (block(b, c), 0)

    return fwd, bwd, ncx + nl


def _ssd_body(xf_ref, dtf_ref, xb_ref, dtb_ref, dtbias_ref, alog_ref, yf_ref, yb_ref, state_ref):
    ri = lax.broadcasted_iota(jnp.int32, (CHUNK, CHUNK), 0)
    ci = lax.broadcasted_iota(jnp.int32, (CHUNK, CHUNK), 1)
    lane_lo = ci < HEAD_DIM
    aneg = -jnp.exp(alog_ref[...])
    dtbias = dtbias_ref[...]

    order = [(d, s if d == 0 else SCAN_CHUNKS - 1 - s) for s in range(SCAN_CHUNKS) for d in range(2)]
    for d, sub in order:
        x_ref, dt_ref, y_ref = ((xf_ref, dtf_ref, yf_ref), (xb_ref, dtb_ref, yb_ref))[d]
        rws = slice(sub * CHUNK, (sub + 1) * CHUNK)
        causal = (ri >= ci) if d == 0 else (ci >= ri)
        tmat = jnp.where(causal, 1.0, 0.0).astype(BF16)
        xs = x_ref[rws, 0:256]
        bm = x_ref[rws, 256:384]
        cm = x_ref[rws, 384:512]
        dtp = _softplus(dt_ref[rws, :] + dtbias)
        acum = _dot3_left(tmat, dtp * aneg)
        acum_t = acum.T
        bt = bm.T.astype(BF16)
        cmb = cm.astype(BF16)
        bmb = bm.astype(BF16)
        tot_row = CHUNK - 1 if d == 0 else 0
        for p in range(2):
            cmask = jnp.where(lane_lo if p == 0 else jnp.logical_not(lane_lo), cmb, jnp.zeros_like(cmb))
            cb = _dot_nt(cmask, bmb)
            cols, dts, ys = [], [], []
            x_pair = xs[:, p * LANES:(p + 1) * LANES]
            for j in range(2):
                col = 4 * d + 2 * p + j
                colb = jnp.broadcast_to(acum[:, col:col + 1], (CHUNK, CHUNK))
                rowb = jnp.broadcast_to(acum_t[col:col + 1, :], (CHUNK, CHUNK))
                cols.append(colb)
                dts.append(jnp.broadcast_to(dtp[:, col:col + 1], (CHUNK, CHUNK)))
            col_pair = jnp.where(lane_lo, cols[0], cols[1])
            dt_pair = jnp.where(lane_lo, dts[0], dts[1])
            xdt = x_pair * dt_pair
            xdt_b = xdt.astype(BF16)
            for j in range(2):
                col = 4 * d + 2 * p + j
                rowb = jnp.broadcast_to(acum_t[col:col + 1, :], (CHUNK, CHUNK))
                decay = jnp.exp(jnp.where(causal, cols[j] - rowb, NEG_INF))
                ys.append(_dot((cb * decay).astype(BF16), xdt_b))
            y_intra = jnp.where(lane_lo, ys[0], ys[1])
            s_old = state_ref[d, p]
            y_inter = _dot(cmask, s_old.astype(BF16)) * jnp.exp(col_pair)
            y_ref[rws, p * LANES:(p + 1) * LANES] = y_intra + y_inter
            tot_pair = col_pair[tot_row:tot_row + 1, :]
            to_end = jnp.exp(tot_pair - col_pair)
            state_ref[d, p] = s_old * jnp.exp(tot_pair) + _dot(bt, (xdt * to_end).astype(BF16))


def _linear_scan(a, b, reverse):
    n = a.shape[0]
    row = lax.broadcasted_iota(jnp.int32, (n, 1), 0)
    s = 1
    while s < n:
        if s < SUBLANES:
            if reverse:
                ok = row < n - s
                a_sh = jnp.where(ok, pltpu.roll(a, n - s, axis=0), 1.0)
                b_sh = jnp.where(ok, pltpu.roll(b, n - s, axis=0), 0.0)
            else:
                ok = row >= s
                a_sh = jnp.where(ok, pltpu.roll(a, s, axis=0), 1.0)
                b_sh = jnp.where(ok, pltpu.roll(b, s, axis=0), 0.0)
            b = b + a * b_sh
            a = a * a_sh
        elif reverse:
            b = jnp.concatenate([b[:n - s] + a[:n - s] * b[s:], b[n - s:]], axis=0)
            a = jnp.concatenate([a[:n - s] * a[s:], a[n - s:]], axis=0)
        else:
            b = jnp.concatenate([b[:s], b[s:] + a[s:] * b[:n - s]], axis=0)
            a = jnp.concatenate([a[:s], a[s:] * a[:n - s]], axis=0)
        s *= 2
    return a, b


def _lru_body(uf_ref, ub_ref, wg_ref, bg_ref, lam_ref, hf_ref, hb_ref, carry_ref):
    for d, (u_ref, h_ref) in enumerate(((uf_ref, hf_ref), (ub_ref, hb_ref))):
        u = u_ref[...]
        gates = _dot(u.astype(BF16), wg_ref[d]) + bg_ref[d]
        r = jax.nn.sigmoid(gates[:, :LRU_WIDTH])
        ig = jax.nn.sigmoid(gates[:, LRU_WIDTH:])
        log_a = -LRU_C * r * _softplus(-lam_ref[d])
        a = jnp.exp(log_a)
        inp = jnp.sqrt(-jnp.tanh(log_a) * (1.0 + a * a)) * (ig * u)
        a_cum, b_cum = _linear_scan(a, inp, reverse=(d == 1))
        h = b_cum + a_cum * carry_ref[d, 0:1, :]
        h_ref[...] = h
        last = 0 if d == 1 else u.shape[0] - 1
        carry_ref[d, 0:1, :] = h[last:last + 1, :]


def _scans_kernel(xf_ref, dtf_ref, xb_ref, dtb_ref, dtbias_ref, alog_ref, uf_ref, ub_ref, wg_ref, bg_ref, lam_ref,
                  yf_ref, yb_ref, hf_ref, hb_ref, state_ref, carry_ref):
    @pl.when(pl.program_id(1) == 0)
    def _():
        state_ref[...] = jnp.zeros_like(state_ref)
        carry_ref[...] = jnp.zeros_like(carry_ref)

    _ssd_body(xf_ref, dtf_ref, xb_ref, dtb_ref, dtbias_ref, alog_ref, yf_ref, yb_ref, state_ref)
    _lru_body(uf_ref, ub_ref, wg_ref, bg_ref, lam_ref, hf_ref, hb_ref, carry_ref)


def _scans(xbc, dt, dtbias_row, alog_row, u, wg, bg, lam, *, batch, seq, ctx_len):
    n = xbc.shape[0]
    fwd, bwd, steps = _chunk_maps(batch, seq, ctx_len)
    rows = SCAN_CHUNKS * CHUNK
    const = lambda a: pl.BlockSpec(a.shape, lambda b, k: (0,) * a.ndim)
    blk = lambda w, order: pl.BlockSpec((rows, w), order)
    wide = jax.ShapeDtypeStruct((n, 256), F32)
    return pl.pallas_call(
        _scans_kernel,
        grid=(batch, steps),
        in_specs=[blk(512, fwd), blk(LANES, fwd), blk(512, bwd), blk(LANES, bwd), const(dtbias_row), const(alog_row),
                  blk(LRU_WIDTH, fwd), blk(LRU_WIDTH, bwd), const(wg), const(bg), const(lam)],
        out_specs=(blk(256, fwd), blk(256, bwd), blk(LRU_WIDTH, fwd), blk(LRU_WIDTH, bwd)),
        out_shape=(wide, wide, wide, wide),
        scratch_shapes=[pltpu.VMEM((2, 2, CHUNK, LANES), F32), pltpu.VMEM((2, SUBLANES, LRU_WIDTH), F32)],
        compiler_params=_params(("parallel", "arbitrary")),
        name="scans",
    )(xbc, dt, xbc, dt, dtbias_row, alog_row, u, u, wg, bg, lam)


def _stack_heads(q, g):
    qf = q.astype(F32)
    lo = g * LANES
    return jnp.concatenate([qf[:, lo:lo + HEAD_DIM], qf[:, lo + HEAD_DIM:lo + LANES]], axis=0).astype(BF16)


def _value_lanes(g):
    lane = lax.broadcasted_iota(jnp.int32, (1, LANES), 1)
    return (lane < HEAD_DIM) if g == 0 else (lane >= HEAD_DIM)


def _aug_values(v, g):
    return jnp.where(_value_lanes(g), v, jnp.ones_like(v))


def _flash_init(rows, g, sink_pair):
    if sink_pair is None:
        return jnp.full((rows, 1), NEG_INF, F32), jnp.zeros((rows, LANES), F32)
    half = lax.broadcasted_iota(jnp.int32, (rows, 1), 0) < rows // 2
    m = jnp.where(half, sink_pair[0], sink_pair[1]).astype(F32)
    acc = jnp.broadcast_to(jnp.where(_value_lanes(g), 0.0, 1.0), (rows, LANES))
    return m, acc


def _flash_update(state, q2, kt, v_aug, mask=None):
    m, acc = state
    s = _dot(q2, kt)
    if mask is not None:
        s = jnp.where(mask, s, NEG_INF)
    m_new = jnp.maximum(m, jnp.max(s, axis=-1, keepdims=True))
    p = jnp.exp(s - m_new).astype(BF16)
    acc = jnp.exp(m - m_new) * acc + _dot(p, v_aug)
    return m_new, acc


def _flash_finish(states, tq):
    pieces = []
    for g, (_, acc) in enumerate(states):
        den = (1 - g) * HEAD_DIM
        o = acc[:, g * HEAD_DIM:(g + 1) * HEAD_DIM] / acc[:, den:den + 1]
        pieces += [o[:tq], o[tq:]]
    return jnp.concatenate(pieces, axis=1)


def _group_rows(g):
    return slice(g * HEAD_DIM, (g + 1) * HEAD_DIM)


def _dense_attn_kernel(*refs, tq, seg_lens, has_sink):
    refs = list(refs)
    sink_ref = refs.pop(0) if has_sink else None
    q_ref = refs.pop(0)
    o_ref = refs.pop()
    segs = [(refs[2 * i], refs[2 * i + 1], n) for i, n in enumerate(seg_lens)]
    q = q_ref[...]
    q2 = [_stack_heads(q, g) for g in range(2)]
    states = tuple(_flash_init(2 * tq, g, (sink_ref[2 * g], sink_ref[2 * g + 1]) if has_sink else None)
                   for g in range(2))
    for kt_ref, v_ref, n_keys in segs:
        if n_keys <= KV_CHUNK:
            v = v_ref[...]
            states = tuple(_flash_update(states[g], q2[g], kt_ref[_group_rows(g), :], _aug_values(v, g))
                           for g in range(2))
        else:
            def body(c, sts, kt_ref=kt_ref, v_ref=v_ref):
                off = pl.multiple_of(c * KV_CHUNK, KV_CHUNK)
                v = v_ref[pl.ds(off, KV_CHUNK), :]
                return tuple(_flash_update(sts[g], q2[g], kt_ref[_group_rows(g), pl.ds(off, KV_CHUNK)],
                                           _aug_values(v, g)) for g in range(2))
            states = lax.fori_loop(0, n_keys // KV_CHUNK, body, states, unroll=KV_UNROLL)
    o_ref[...] = _flash_finish(states, tq).astype(o_ref.dtype)


def _dense_attn(q, kt, v, sink, *, q_row0, q_len, tq, segs, batch):
    n = q.shape[0]
    qpb = q_len // tq
    q0 = q_row0 // tq
    in_specs, args = [], []
    if sink is not None:
        in_specs.append(pl.BlockSpec(memory_space=pltpu.SMEM))
        args.append(sink)
    in_specs.append(pl.BlockSpec((tq, 256), lambda b, i: (q0 + b * qpb + i, 0)))
    args.append(q)
    for row0, klen in segs:
        k0 = row0 // klen
        in_specs.append(pl.BlockSpec((LANES, klen), lambda b, i, k0=k0: (0, k0 + b)))
        in_specs.append(pl.BlockSpec((klen, LANES), lambda b, i, k0=k0: (k0 + b, 0)))
        args += [kt, v]
    return pl.pallas_call(
        functools.partial(_dense_attn_kernel, tq=tq, seg_lens=tuple(s[1] for s in segs), has_sink=sink is not None),
        grid=(batch, qpb),
        in_specs=in_specs,
        out_specs=pl.BlockSpec((tq, 256), lambda b, i: (b * qpb + i, 0)),
        out_shape=jax.ShapeDtypeStruct((batch * q_len, 256), BF16),
        compiler_params=_params(("parallel", "parallel")),
        name="dense_attn",
    )(*args)


def _window_attn_kernel(sink_ref, q_ref, ktc_ref, vc_ref, ktp_ref, vp_ref, ktm_ref, vm_ref, ktn_ref, vn_ref, o_ref,
                        *, n_tiles):
    n = pl.program_id(1)
    nsub = TQ_WINDOW // CHUNK
    iq = lax.broadcasted_iota(jnp.int32, (2 * CHUNK, CHUNK), 0) & (CHUNK - 1)
    jk = lax.broadcasted_iota(jnp.int32, (2 * CHUNK, CHUNK), 1)
    below = jk >= iq
    above = jk <= iq
    vctx = vc_ref[...]
    for j in range(nsub):
        cols = slice(j * CHUNK, (j + 1) * CHUNK)
        q = q_ref[cols, :]
        states = []
        for g in range(2):
            q2 = _stack_heads(q, g)
            rows = _group_rows(g)
            state = _flash_init(2 * CHUNK, g, (sink_ref[2 * g], sink_ref[2 * g + 1]))
            state = _flash_update(state, q2, ktc_ref[rows, :], _aug_values(vctx, g))
            state = _flash_update(state, q2, ktm_ref[rows, cols], _aug_values(vm_ref[cols, :], g))
            if j > 0:
                prev = slice((j - 1) * CHUNK, j * CHUNK)
                state = _flash_update(state, q2, ktm_ref[rows, prev], _aug_values(vm_ref[prev, :], g), below)
            else:
                state = _flash_update(state, q2, ktp_ref[rows, :], _aug_values(vp_ref[...], g),
                                      jnp.logical_and(below, n > 0))
            if j < nsub - 1:
                nxt = slice((j + 1) * CHUNK, (j + 2) * CHUNK)
                state = _flash_update(state, q2, ktm_ref[rows, nxt], _aug_values(vm_ref[nxt, :], g), above)
            else:
                state = _flash_update(state, q2, ktn_ref[rows, :], _aug_values(vn_ref[...], g),
                                      jnp.logical_and(above, n < n_tiles - 1))
            states.append(state)
        o_ref[cols, :] = _flash_finish(states, CHUNK).astype(o_ref.dtype)


def _window_attn(q, kt, v, sink, *, batch, seq, ctx_len):
    nt = seq // TQ_WINDOW
    nsub = TQ_WINDOW // CHUNK
    nb = seq // CHUNK
    ctx0 = (batch * seq) // ctx_len
    prev = lambda b, n: b * nb + jnp.maximum(n * nsub - 1, 0)
    nxt = lambda b, n: b * nb + jnp.minimum((n + 1) * nsub, nb - 1)
    return pl.pallas_call(
        functools.partial(_window_attn_kernel, n_tiles=nt),
        grid=(batch, nt),
        in_specs=[pl.BlockSpec(memory_space=pltpu.SMEM),
                  pl.BlockSpec((TQ_WINDOW, 256), lambda b, n: (b * nt + n, 0)),
                  pl.BlockSpec((LANES, ctx_len), lambda b, n: (0, ctx0 + b)),
                  pl.BlockSpec((ctx_len, LANES), lambda b, n: (ctx0 + b, 0)),
                  pl.BlockSpec((LANES, CHUNK), lambda b, n: (0, prev(b, n))),
                  pl.BlockSpec((CHUNK, LANES), lambda b, n: (prev(b, n), 0)),
                  pl.BlockSpec((LANES, TQ_WINDOW), lambda b, n: (0, b * nt + n)),
                  pl.BlockSpec((TQ_WINDOW, LANES), lambda b, n: (b * nt + n, 0)),
                  pl.BlockSpec((LANES, CHUNK), lambda b, n: (0, nxt(b, n))),
                  pl.BlockSpec((CHUNK, LANES), lambda b, n: (nxt(b, n), 0))],
        out_specs=pl.BlockSpec((TQ_WINDOW, 256), lambda b, n: (b * nt + n, 0)),
        out_shape=jax.ShapeDtypeStruct((batch * seq, 256), BF16),
        compiler_params=_params(("parallel", "parallel")),
        name="window_attn",
    )(sink, q, kt, v, kt, v, kt, v, kt, v)


def _gelu_tanh(x):
    return 0.5 * x * (1.0 + jnp.tanh(math.sqrt(2.0 / math.pi) * (x + 0.044715 * (x * x * x))))


def _outproj_kernel(x_ref, xc_ref, gate_ref, gpost_ref, oa_ref, od_ref, oac_ref, odc_ref, yf_ref, yb_ref, xs_ref,
                    z_ref, dsk_ref, gn_ref, hf_ref, hb_ref, lg_ref, w_ref,
                    fshift_ref, fscale_ref, fpre_ref, rwt_ref, rb_ref,
                    o_ref, hffn_ref, ld_ref, wk_ref, tab_ref, *, n_lat, split):
    is_lat = pl.program_id(0) * TM < n_lat
    x = jnp.where(jnp.logical_or(is_lat, not split), x_ref[...], xc_ref[...])
    oa = jnp.where(is_lat, oa_ref[...], oac_ref[...])
    od = jnp.where(is_lat, od_ref[...], odc_ref[...])
    y_ssd = (yf_ref[...] + yb_ref[...] + xs_ref[...] * dsk_ref[...]) * _silu(z_ref[...])
    ob = _rms(y_ssd, gn_ref[...])
    oc = (hf_ref[...] + hb_ref[...]) * _gelu_tanh(lg_ref[...])
    y = (_dot(oa, w_ref[0:256, :]) + _dot(ob.astype(BF16), w_ref[256:512, :])
         + _dot(oc.astype(BF16), w_ref[512:768, :]) + _dot(od, w_ref[768:1024, :]))
    x_mid = x + gate_ref[0] * _rms(y, gpost_ref[...])
    o_ref[...] = x_mid
    _route_tile(x_mid, fshift_ref, fscale_ref, fpre_ref, rwt_ref, rb_ref, hffn_ref, ld_ref, wk_ref, tab_ref)


def _outproj(x_lat, x_ctx, mod, gpost, oa, od, oa_c, od_c, yf, yb, xbc, z, dsk, gn, hf, hb, lg, w_out,
             ffn_pre, rwt, rb, *, n_rows, n_lat, seq, batch):
    d = x_lat.shape[1]
    lat_tiles = n_lat // TM
    x_specs, x_args = _row_sources(x_lat, x_ctx, n_lat)
    row = lambda w: pl.BlockSpec((TM, w), lambda i: (i, 0))
    col = pl.BlockSpec((TOP_K, TM), lambda i: (0, i))
    lat = pl.BlockSpec((TM, 256), lambda i: (jnp.minimum(i, lat_tiles - 1), 0))
    ctx = pl.BlockSpec((TM, 256), lambda i: (jnp.maximum(i - lat_tiles, 0), 0))
    const = lambda a: pl.BlockSpec(a.shape, lambda i: (0,) * a.ndim)
    return pl.pallas_call(
        functools.partial(_outproj_kernel, n_lat=n_lat, split=x_ctx is not None),
        grid=(n_rows // TM,),
        in_specs=[x_specs[0], x_specs[3], pl.BlockSpec((1, 1, d), _mod_spec(2, n_lat, seq, batch, TM)), const(gpost),
                  lat, lat, ctx, ctx, row(256), row(256), row(256), row(256), const(dsk), const(gn),
                  row(256), row(256), row(256), const(w_out),
                  pl.BlockSpec((1, 1, d), _mod_spec(3, n_lat, seq, batch, TM)),
                  pl.BlockSpec((1, 1, d), _mod_spec(4, n_lat, seq, batch, TM)),
                  const(ffn_pre), const(rwt), const(rb)],
        out_specs=(row(d), row(d), col, col, pl.BlockSpec((SUBLANES, 2 * N_EXPERTS), lambda i: (i, 0))),
        out_shape=(jax.ShapeDtypeStruct((n_rows, d), F32),
                   jax.ShapeDtypeStruct((n_rows, d), BF16),
                   jax.ShapeDtypeStruct((TOP_K, n_rows), jnp.int32),
                   jax.ShapeDtypeStruct((TOP_K, n_rows), F32),
                   jax.ShapeDtypeStruct((n_rows // TM * SUBLANES, 2 * N_EXPERTS), jnp.int32)),
        compiler_params=_params(("parallel",)),
        name="outproj_route",
    )(x_args[0], x_args[3], mod, gpost, oa, od, oa_c, od_c, yf, yb, xbc, z, dsk, gn, hf, hb, lg, w_out,
      mod, mod, ffn_pre, rwt, rb)


def _ceil_seg(c):
    return jnp.floor((c + (SEG_ALIGN - 1)) * (1.0 / SEG_ALIGN)) * SEG_ALIGN


def _route_tile(x, shift_ref, scale_ref, gpre_ref, rwt_ref, rb_ref, hb_ref, ld_ref, wk_ref, tab_ref):
    h = _rms(x, gpre_ref[...])
    h = h * (1.0 + scale_ref[0]) + shift_ref[0]
    hb = h.astype(BF16)
    hb_ref[...] = hb

    scores = jax.nn.sigmoid(_dot_nt(rwt_ref[...], hb))
    biased = scores + rb_ref[...]
    gsz = N_EXPERTS // N_EXPERT_GROUPS
    sub = lax.broadcasted_iota(jnp.int32, (gsz, TM), 0)
    blocks, gscore = [], []
    for g in range(N_EXPERT_GROUPS):
        blk = biased[g * gsz:(g + 1) * gsz, :]
        m1 = jnp.max(blk, axis=0, keepdims=True)
        first = jnp.min(jnp.where(blk == m1, sub, gsz), axis=0, keepdims=True)
        m2 = jnp.max(jnp.where(sub == first, -jnp.inf, blk), axis=0, keepdims=True)
        blocks.append(blk)
        gscore.append(m1 + m2)
    masked = []
    for g in range(N_EXPERT_GROUPS):
        rank = jnp.zeros((1, TM), F32)
        for g2 in range(N_EXPERT_GROUPS):
            if g2 == g:
                continue
            beats = (gscore[g2] > gscore[g]) | ((gscore[g2] == gscore[g]) if g2 < g else False)
            rank = rank + jnp.where(beats, 1.0, 0.0)
        masked.append(jnp.where(rank < TOPK_GROUPS, blocks[g], -jnp.inf))
    vals = jnp.concatenate(masked, axis=0)
    eidx = lax.broadcasted_iota(jnp.int32, (N_EXPERTS, TM), 0)
    self32 = jnp.zeros((N_EXPERTS, TM), F32)
    rest = vals
    for _ in range(TOP_K):
        top = jnp.max(rest, axis=0, keepdims=True)
        first = jnp.min(jnp.where(rest == top, eidx, N_EXPERTS), axis=0, keepdims=True)
        hit = eidx == first
        self32 = jnp.where(hit, 1.0, self32)
        rest = jnp.where(hit, -jnp.inf, rest)
    sel = self32 > 0.5
    picked = jnp.where(sel, scores, 0.0)
    wdense = picked / jnp.sum(picked, axis=0, keepdims=True) * ROUTED_SCALE

    tr = lax.broadcasted_iota(jnp.int32, (TM, TM), 0)
    tc = lax.broadcasted_iota(jnp.int32, (TM, TM), 1)
    before = jnp.where(tr < tc, 1.0, 0.0).astype(BF16)
    selb = self32.astype(BF16)
    pos = _dot(selb, before)
    er = lax.broadcasted_iota(jnp.int32, (N_EXPERTS, N_EXPERTS), 0)
    ec = lax.broadcasted_iota(jnp.int32, (N_EXPERTS, N_EXPERTS), 1)
    lower = jnp.where(ec < er, 1.0, 0.0).astype(BF16)
    upper = jnp.where(er < ec, 1.0, 0.0).astype(BF16)
    ksel = _dot(lower, selb)
    cnt_col = _ceil_seg(jnp.sum(self32, axis=1, keepdims=True))
    loc_col = _dot3_left(lower, jnp.broadcast_to(cnt_col, (N_EXPERTS, LANES)))[:, 0:1]
    cnt_row = _ceil_seg(_dot_nt(jnp.ones((SUBLANES, TM), BF16), selb))
    loc_row = _dot3(cnt_row, upper)
    tab_ref[...] = jnp.concatenate([cnt_row, loc_row], axis=1).astype(jnp.int32)

    r8 = lax.broadcasted_iota(jnp.int32, (TOP_K, TM), 0)
    ld = jnp.zeros((TOP_K, TM), F32)
    wk = jnp.zeros((TOP_K, TM), F32)
    stage_row = pos + loc_col
    for k in range(TOP_K):
        one = sel & (ksel == float(k))
        ld = jnp.where(r8 == k, jnp.sum(jnp.where(one, stage_row, 0.0), axis=0, keepdims=True), ld)
        wk = jnp.where(r8 == k, jnp.sum(jnp.where(one, wdense, 0.0), axis=0, keepdims=True), wk)
    ld_ref[...] = ld.astype(jnp.int32)
    wk_ref[...] = wk


def _pow2_pieces(limit):
    bits, b = [], limit
    while b >= SEG_ALIGN:
        bits.append(b)
        b //= 2
    return bits


def _copy_pieces(n, src_ref, src0, dst_ref, dst0, sem, limit, wait, same_src=False):
    for bit in _pow2_pieces(limit):
        @pl.when((n & bit) != 0)
        def _():
            off = n & ~(2 * bit - 1)
            cp = pltpu.make_async_copy(src_ref.at[pl.ds(pl.multiple_of(src0 + (0 if same_src else off), SEG_ALIGN),
                                                          bit)],
                                       dst_ref.at[pl.ds(pl.multiple_of(dst0 + off, SEG_ALIGN), bit)], sem)
            cp.wait() if wait else cp.start()


N_PIECE_TABS = 7


def _piece_copies(tile, tabs, stage_ref, slots_ref, sem, to_slots, wait):
    nbig_ref, nsmall_ref, _, bsrc_ref, bdst_ref, ssrc_ref, sdst_ref = tabs
    for rows, n_ref, a_ref, b_ref, cap in ((BIG_PIECE, nbig_ref, bsrc_ref, bdst_ref, BIG_MAX),
                                           (SEG_ALIGN, nsmall_ref, ssrc_ref, sdst_ref, SMALL_MAX)):
        def body(p, c, rows=rows, a_ref=a_ref, b_ref=b_ref, cap=cap):
            src = stage_ref.at[pl.ds(pl.multiple_of(a_ref[tile * cap + p], SEG_ALIGN), rows)]
            dst = slots_ref.at[pl.ds(pl.multiple_of(b_ref[tile * cap + p], SEG_ALIGN), rows)]
            cp = pltpu.make_async_copy(src, dst, sem) if to_slots else pltpu.make_async_copy(dst, src, sem)
            cp.wait() if wait else cp.start()
            return c
        lax.fori_loop(0, n_ref[tile], body, 0)


def _used_blocks(tile, tabs):
    return (tabs[2][tile] + TM - 1) // TM


def _for_used_blocks(used, body):
    always = TOP_K + 1
    for b in range(always):
        body(b)
    for b in range(always, STAGE_ROWS // TM):
        @pl.when(b < used)
        def _():
            body(b)


def _stage_rows_iota():
    return lax.broadcasted_iota(jnp.int32, (TM // 2, TM), 0).astype(F32).astype(BF16)


def _pick_matrix(ld, base, vals, jrow):
    rel = (ld - base).astype(F32)
    rel = jnp.where(jnp.logical_and(rel >= 0.0, rel < TM // 2), rel, -1.0).astype(BF16)
    out = jnp.zeros((TM // 2, TM), BF16)
    for k in range(TOP_K):
        out = jnp.where(rel[k:k + 1, :] == jrow, vals[k:k + 1, :], out)
    return out


def _dispatch_kernel(*refs):
    tabs = refs[:N_PIECE_TABS]
    pstart_ref, npad_ref, hb_ref, ld_ref, xs_ref, stage, zbuf, sem, zsem = refs[N_PIECE_TABS:]
    i = pl.program_id(0)

    @pl.when(i == 0)
    def _():
        zbuf[...] = jnp.zeros_like(zbuf)
        for wait in (False, True):
            def body(e, c, wait=wait):
                _copy_pieces(npad_ref[e], zbuf, 0, xs_ref, pstart_ref[e], zsem, BM_EXPERT // 2, wait, same_src=True)
                return c
            lax.fori_loop(0, N_EXPERTS, body, 0)

    ld = ld_ref[...]
    hb = hb_ref[...]
    jrow = _stage_rows_iota()
    ones = jnp.ones((TOP_K, TM), BF16)

    cur = stage.at[i & 1]

    def block(b):
        for half in range(2):
            base = b * TM + half * (TM // 2)
            cur[base:base + TM // 2, :] = _dot(_pick_matrix(ld, base, ones, jrow), hb).astype(BF16)

    _for_used_blocks(_used_blocks(i, tabs), block)

    @pl.when(i > 0)
    def _():
        _piece_copies(i - 1, tabs, stage.at[(i - 1) & 1], xs_ref, sem, True, True)

    _piece_copies(i, tabs, cur, xs_ref, sem, True, False)

    @pl.when(i == pl.num_programs(0) - 1)
    def _():
        _piece_copies(i, tabs, cur, xs_ref, sem, True, True)


def _dispatch(tabs, pad_start, n_pad, hb, ld, n_slots):
    n, d = hb.shape
    grid_spec = pltpu.PrefetchScalarGridSpec(
        num_scalar_prefetch=N_PIECE_TABS + 2,
        grid=(n // TM,),
        in_specs=[pl.BlockSpec((TM, d), lambda i, *_: (i, 0)),
                  pl.BlockSpec((TOP_K, TM), lambda i, *_: (0, i))],
        out_specs=pl.BlockSpec(memory_space=pl.ANY),
        scratch_shapes=[pltpu.VMEM((2, STAGE_ROWS, d), BF16), pltpu.VMEM((BM_EXPERT // 2, d), BF16),
                        pltpu.SemaphoreType.DMA(()), pltpu.SemaphoreType.DMA(())],
    )
    return pl.pallas_call(
        _dispatch_kernel,
        grid_spec=grid_spec,
        out_shape=jax.ShapeDtypeStruct((n_slots, d), BF16),
        compiler_params=_params(("arbitrary",)),
        name="moe_dispatch",
    )(*tabs, pad_start, n_pad, hb, ld)


def _expert_kernel(be_ref, na_ref, nxt_ref, slot_ref, xs_hbm, wg_hbm, wu_hbm, wd_hbm, ys_ref,
                   xbuf, wg_raw, wu_raw, wd_raw, wgub, wdb, xsem, wsem, *, layer):
    i = pl.program_id(0)
    n_act = na_ref[0]

    def weight_copies(e, slot):
        return [pltpu.make_async_copy(src.at[layer, e], dst.at[slot], wsem.at[slot, j])
                for j, (src, dst) in enumerate(((wg_hbm, wg_raw), (wu_hbm, wu_raw), (wd_hbm, wd_raw)))]

    def rows_copy(blk):
        buf = lax.rem(blk, X_BUFFERS)
        return pltpu.make_async_copy(xs_hbm.at[pl.ds(pl.multiple_of(blk * BM_EXPERT, BM_EXPERT), BM_EXPERT)],
                                     xbuf.at[buf], xsem.at[buf])

    @pl.when(i < n_act)
    def _():
        e, slot = be_ref[i], slot_ref[i]

        @pl.when(i == 0)
        def _():
            for cp in weight_copies(e, slot):
                cp.start()
            for j in range(X_BUFFERS - 1):
                @pl.when(j < n_act)
                def _():
                    rows_copy(j).start()

        @pl.when(i + (X_BUFFERS - 1) < n_act)
        def _():
            rows_copy(i + (X_BUFFERS - 1)).start()

        @pl.when(jnp.logical_or(i == 0, e != be_ref[jnp.maximum(i - 1, 0)]))
        def _():
            for cp in weight_copies(e, slot):
                cp.wait()
            wgub[:, :EXPERT_HIDDEN] = wg_raw[slot].astype(BF16)
            wgub[:, EXPERT_HIDDEN:] = wu_raw[slot].astype(BF16)
            wdb[...] = wd_raw[slot].astype(BF16)

            @pl.when(nxt_ref[i] != e)
            def _():
                for cp in weight_copies(nxt_ref[i], 1 - slot):
                    cp.start()

        rows_copy(i).wait()
        gu = _dot(xbuf[lax.rem(i, X_BUFFERS)], wgub[...])
        hid = _silu(gu[:, :EXPERT_HIDDEN]) * gu[:, EXPERT_HIDDEN:]
        ys_ref[...] = _dot(hid.astype(BF16), wdb[...]).astype(ys_ref.dtype)


def _experts(block_e, n_active, next_e, w_slot, xs, wg, wu, wd, layer):
    n_slots, d = xs.shape
    nb = n_slots // BM_EXPERT
    tiles = pl.BlockSpec((BM_EXPERT, d), lambda i, be, na, *_: (jnp.minimum(i, na[0] - 1), 0))
    anywhere = pl.BlockSpec(memory_space=pl.ANY)
    grid_spec = pltpu.PrefetchScalarGridSpec(
        num_scalar_prefetch=4,
        grid=(nb,),
        in_specs=[anywhere, anywhere, anywhere, anywhere],
        out_specs=tiles,
        scratch_shapes=[pltpu.VMEM((X_BUFFERS, BM_EXPERT, d), BF16),
                        pltpu.VMEM((2, d, EXPERT_HIDDEN), F32), pltpu.VMEM((2, d, EXPERT_HIDDEN), F32),
                        pltpu.VMEM((2, EXPERT_HIDDEN, d), F32),
                        pltpu.VMEM((d, 2 * EXPERT_HIDDEN), BF16),
                        pltpu.VMEM((EXPERT_HIDDEN, d), BF16),
                        pltpu.SemaphoreType.DMA((X_BUFFERS,)), pltpu.SemaphoreType.DMA((2, 3))],
    )
    return pl.pallas_call(
        functools.partial(_expert_kernel, layer=layer),
        grid_spec=grid_spec,
        out_shape=jax.ShapeDtypeStruct((n_slots, d), BF16),
        compiler_params=_params(("arbitrary",)),
        name="moe_experts",
    )(block_e, n_active, next_e, w_slot, xs, wg, wu, wd)


def _combine_kernel(*refs):
    tabs = refs[:N_PIECE_TABS]
    (ys_ref, ld_ref, wk_ref, hb_ref, x_ref, gate_ref, gpost_ref, sg_ref, su_ref, sd_ref, o_ref,
     stage, acc_ref, sem) = refs[N_PIECE_TABS:]
    i = pl.program_id(0)

    @pl.when(i == 0)
    def _():
        stage[...] = jnp.zeros_like(stage)
        _piece_copies(0, tabs, stage.at[0], ys_ref, sem, False, False)

    cur = stage.at[i & 1]
    _piece_copies(i, tabs, cur, ys_ref, sem, False, True)

    @pl.when(i + 1 < pl.num_programs(0))
    def _():
        _piece_copies(i + 1, tabs, stage.at[(i + 1) & 1], ys_ref, sem, False, False)

    hb = hb_ref[...]
    acc_ref[...] = _dot((_silu(_dot(hb, sg_ref[...])) * _dot(hb, su_ref[...])).astype(BF16), sd_ref[...])

    ld = ld_ref[...]
    wkb = wk_ref[...].astype(BF16)
    jrow = _stage_rows_iota()

    def block(b):
        for half in range(2):
            base = b * TM + half * (TM // 2)
            weights = _pick_matrix(ld, base, wkb, jrow)
            acc_ref[...] += lax.dot_general(weights, cur[base:base + TM // 2, :], (((0,), (0,)), ((), ())),
                                            preferred_element_type=F32)

    _for_used_blocks(_used_blocks(i, tabs), block)
    o_ref[...] = x_ref[...] + gate_ref[0] * _rms(acc_ref[...], gpost_ref[...])


def _combine(tabs, ys, ld, wk, hb, xu, mod, gpost, sg, su, sd, *, n_rows, n_lat, seq, batch):
    d = xu.shape[1]
    row = lambda w: pl.BlockSpec((TM, w), lambda i, *_: (i, 0))
    col = pl.BlockSpec((TOP_K, TM), lambda i, *_: (0, i))
    const = lambda a: pl.BlockSpec(a.shape, lambda i, *_: (0,) * a.ndim)
    mod_map = _mod_spec(5, n_lat, seq, batch, TM)
    grid_spec = pltpu.PrefetchScalarGridSpec(
        num_scalar_prefetch=N_PIECE_TABS,
        grid=(n_rows // TM,),
        in_specs=[pl.BlockSpec(memory_space=pl.ANY), col, col, row(d), row(d),
                  pl.BlockSpec((1, 1, d), lambda i, *_: mod_map(i)),
                  const(gpost), const(sg), const(su), const(sd)],
        out_specs=row(d),
        scratch_shapes=[pltpu.VMEM((2, STAGE_ROWS, d), BF16), pltpu.VMEM((TM, d), F32), pltpu.SemaphoreType.DMA(())],
    )
    return pl.pallas_call(
        _combine_kernel,
        grid_spec=grid_spec,
        out_shape=jax.ShapeDtypeStruct((n_rows, d), F32),
        compiler_params=_params(("arbitrary",)),
        name="moe_combine",
    )(*tabs, ys, ld, wk, hb, xu, mod, gpost, sg, su, sd)


def _deinterleave(w):
    cols = w.shape[-1]
    perm = jnp.concatenate([jnp.arange(0, HEAD_DIM, 2), jnp.arange(1, HEAD_DIM, 2)])
    idx = (jnp.arange(cols // HEAD_DIM)[:, None] * HEAD_DIM + perm[None, :]).reshape(-1)
    return w[..., idx]


def _pad_in_proj(w_in):
    d = w_in.shape[0]
    o = 0
    parts = {}
    for name, width in (("qa", 256), ("ka", 128), ("va", 128), ("z", 256), ("xs", 256), ("bm", 128), ("cm", 128),
                        ("dtf", 4), ("dtb", 4), ("lx", 256), ("lg", 256), ("qd", 256), ("kd", 128), ("vd", 128)):
        parts[name] = w_in[:, o:o + width]
        o += width
    dt = jnp.concatenate([parts["dtf"], parts["dtb"], jnp.zeros((d, LANES - 8), w_in.dtype)], axis=1)
    cols = [_deinterleave(parts["qa"]), _deinterleave(parts["ka"]), parts["va"],
            _deinterleave(parts["qd"]), _deinterleave(parts["kd"]), parts["vd"],
            parts["z"], parts["xs"], parts["bm"], parts["cm"], dt, parts["lx"], parts["lg"]]
    return jnp.concatenate(cols, axis=1).astype(BF16)


def _rope_tables(seq):
    t = jnp.arange(seq)
    rowp = (t // GRID_W).astype(F32)
    colp = (t % GRID_W).astype(F32)
    axis_dim = HEAD_DIM // 2
    inv_freq = ROPE_THETA ** (-jnp.arange(0, axis_dim, 2, dtype=F32) / axis_dim)
    ang = jnp.concatenate([rowp[:, None] * inv_freq, colp[:, None] * inv_freq], axis=-1)
    cos, sin = jnp.cos(ang), jnp.sin(ang)
    cos_h = jnp.concatenate([cos, cos], axis=-1)
    sin_h = jnp.concatenate([-sin, sin], axis=-1)
    return jnp.tile(cos_h, (1, 4)), jnp.tile(sin_h, (1, 4))


def _block_diag(w):
    nb, bd, _ = w.shape
    eye = jnp.eye(nb, dtype=w.dtype)
    return (eye[:, None, :, None] * w[:, :, None, :]).reshape(nb * bd, nb * bd)


def _piece_table(counts, cap, stage0, slot0, rows, ids):
    ends = jnp.cumsum(counts, axis=1)
    q = jnp.arange(cap, dtype=jnp.int32)
    owner = jnp.sum((ends[:, None, :] <= q[None, :, None]).astype(jnp.int32), axis=-1)
    mine = owner[:, :, None] == ids
    pick = lambda v: jnp.sum(jnp.where(mine, v[:, None, :], 0), axis=-1)
    step = rows * (q[None, :] - pick(ends - counts))
    return (pick(stage0) + step).reshape(-1), (pick(slot0) + step).reshape(-1)


def _lane_row(fwd, bwd):
    return jnp.concatenate([fwd, bwd, jnp.zeros((LANES - 8,), F32)]).reshape(1, LANES)


def kernel(x, c, ctx, c_ctx, w_ada, b_ada, g_mix_pre, g_mix_post, g_ffn_pre, g_ffn_post, w_in, w_out, a_sink,
           ssd_conv_w, ssd_conv_b, ssd_dt_bias, ssd_a_log, ssd_d, ssd_norm, lru_conv_w, lru_conv_b, lru_w_a,
           lru_b_a, lru_w_i, lru_b_i, lru_lambda, d_q_norm, d_k_norm, router_w, router_bias, exp_w_gate,
           exp_w_up, exp_w_down, sh_w_gate, sh_w_up, sh_w_down):
    batch, seq, d = x.shape
    ctx_len = ctx.shape[1]
    depth = w_ada.shape[0]
    n_lat = batch * seq
    n_ctx = batch * ctx_len
    n_all = n_lat + n_ctx
    assert seq % TM == 0 and n_ctx % TM == 0 and ctx_len >= LRU_CONV
    assert seq % TQ_WINDOW == 0 and seq % (SCAN_CHUNKS * CHUNK) == 0 and ctx_len % (SCAN_CHUNKS * CHUNK) == 0
    assert ctx_len <= KV_CHUNK and seq % KV_CHUNK == 0 and seq % TQ_GLOBAL == 0 and batch + 1 <= SUBLANES

    xu, xu_ctx = x.reshape(n_lat, d), ctx.reshape(n_ctx, d)
    cin = jnp.concatenate([c, c_ctx[None, :], jnp.zeros((SUBLANES - batch - 1, d), F32)], axis=0)
    mod_all = _adaln(cin, w_ada, b_ada)
    cos_t, sin_t = _rope_tables(seq)
    hm = jnp.kron(jnp.eye(4, dtype=F32), jnp.full((HEAD_DIM, HEAD_DIM), 1.0 / HEAD_DIM, F32)).astype(BF16)

    for l in range(depth):
        with_ctx = l < depth - 1
        mod = mod_all[l].reshape(SUBLANES * 6, 1, d)
        gq = jnp.tile(_deinterleave(d_q_norm[l]), 4).reshape(1, 256)
        gk = jnp.tile(_deinterleave(d_k_norm[l]), 2).reshape(1, LANES)
        qa, kat, va, qd, kdt, vd, z, xbc, dt, lu, lg = _inproj(
            xu, xu_ctx, mod, g_mix_pre[l].reshape(1, d), _pad_in_proj(w_in[l]), cos_t, sin_t, gq, gk, hm,
            ssd_conv_w[l], ssd_conv_b[l].reshape(1, -1), lru_conv_w[l], lru_conv_b[l].reshape(1, -1),
            n_lat=n_lat, seq=seq, ctx_len=ctx_len, batch=batch)
        wg = jnp.stack([jnp.concatenate([_block_diag(lru_w_a[l, dd]), _block_diag(lru_w_i[l, dd])], axis=1)
                        for dd in range(2)]).astype(BF16)
        bg = jnp.concatenate([lru_b_a[l], lru_b_i[l]], axis=1).reshape(2, 1, 2 * LRU_WIDTH)
        yf, yb, hf, hb = _scans(xbc, dt, _lane_row(ssd_dt_bias[l, 0], ssd_dt_bias[l, 1]),
                                _lane_row(ssd_a_log[l, 0], ssd_a_log[l, 1]), lu, wg, bg,
                                lru_lambda[l].reshape(2, 1, LRU_WIDTH), batch=batch, seq=seq, ctx_len=ctx_len)

        oa = _window_attn(qa, kat, va, a_sink[l], batch=batch, seq=seq, ctx_len=ctx_len)
        od = _dense_attn(qd, kdt, vd, None, q_row0=0, q_len=seq, tq=TQ_GLOBAL,
                         segs=[(n_lat, ctx_len), (0, seq)], batch=batch)
        if with_ctx:
            oa_c = _dense_attn(qa, kat, va, a_sink[l], q_row0=n_lat, q_len=ctx_len, tq=ctx_len,
                               segs=[(n_lat, ctx_len)], batch=batch)
            od_c = _dense_attn(qd, kdt, vd, None, q_row0=n_lat, q_len=ctx_len, tq=ctx_len,
                               segs=[(n_lat, ctx_len)], batch=batch)
        else:
            oa_c, od_c = oa, od
        n_rows = n_all if with_ctx else n_lat

        dsk = jnp.repeat(ssd_d[l], HEAD_DIM).reshape(1, 256)
        xu_mid, hb_ffn, ld, wk, tab = _outproj(
            xu, xu_ctx, mod, g_mix_post[l].reshape(1, d), oa, od, oa_c, od_c, yf, yb, xbc, z, dsk,
            ssd_norm[l].reshape(1, 256), hf, hb, lg, w_out[l].astype(BF16),
            g_ffn_pre[l].reshape(1, d), router_w[l].T.astype(BF16), router_bias[l].reshape(N_EXPERTS, 1),
            n_rows=n_rows, n_lat=n_lat, seq=seq, batch=batch)
        n_tiles = n_rows // TM
        tab = tab.reshape(n_tiles, SUBLANES, 2 * N_EXPERTS)[:, 0, :]
        seg_cnt, seg_loc = tab[:, :N_EXPERTS], tab[:, N_EXPERTS:]
        counts = jnp.sum(seg_cnt, axis=0)
        padded = (counts + BM_EXPERT - 1) // BM_EXPERT * BM_EXPERT
        padded_end = jnp.cumsum(padded)
        offs = padded_end - padded
        seg_off = offs[None, :] + jnp.cumsum(seg_cnt, axis=0) - seg_cnt
        n_blocks = (n_rows * TOP_K + n_tiles * N_EXPERTS * SEG_ALIGN) // BM_EXPERT + N_EXPERTS
        n_active = (padded_end[-1] // BM_EXPERT).astype(jnp.int32).reshape(1)
        block_start = jnp.arange(n_blocks, dtype=jnp.int32) * BM_EXPERT
        block_e = jnp.minimum(jnp.sum((padded_end[None, :] <= block_start[:, None]).astype(jnp.int32), axis=1),
                              N_EXPERTS - 1)
        ids = jnp.arange(N_EXPERTS, dtype=jnp.int32)
        n_big = seg_cnt // BIG_PIECE
        n_small = (seg_cnt % BIG_PIECE) // SEG_ALIGN
        tabs = (jnp.sum(n_big, axis=1), jnp.sum(n_small, axis=1), seg_loc[:, -1] + seg_cnt[:, -1],
                *_piece_table(n_big, BIG_MAX, seg_loc, seg_off, BIG_PIECE, ids),
                *_piece_table(n_small, SMALL_MAX, seg_loc + n_big * BIG_PIECE, seg_off + n_big * BIG_PIECE,
                              SEG_ALIGN, ids))
        xs = _dispatch(tabs, offs + counts, padded - counts, hb_ffn, ld, n_blocks * BM_EXPERT)
        has_rows = padded > 0
        later = jnp.logical_and(ids[None, :] > ids[:, None], has_rows[None, :])
        nxt_of = jnp.min(jnp.where(later, ids[None, :], N_EXPERTS), axis=1)
        nxt_of = jnp.where(nxt_of == N_EXPERTS, ids, nxt_of)
        slot_of = (jnp.cumsum(has_rows.astype(jnp.int32)) - 1) & 1
        own = block_e[:, None] == ids[None, :]
        next_e = jnp.sum(jnp.where(own, nxt_of[None, :], 0), axis=1)
        w_slot = jnp.sum(jnp.where(own, slot_of[None, :], 0), axis=1)
        ys = _experts(block_e, n_active, next_e, w_slot, xs, exp_w_gate, exp_w_up, exp_w_down, l)
        xu = _combine(tabs, ys, ld, wk, hb_ffn, xu_mid, mod, g_ffn_post[l].reshape(1, d), sh_w_gate[l].astype(BF16),
                      sh_w_up[l].astype(BF16), sh_w_down[l].astype(BF16),
                      n_rows=n_rows, n_lat=n_lat, seq=seq, batch=batch)
        xu_ctx = None
    return xu[:n_lat].reshape(batch, seq, d)
```

```python
import functools
import math

import jax
import jax.numpy as jnp
from jax import lax
from jax.experimental import pallas as pl
from jax.experimental.pallas import tpu as pltpu

F32 = jnp.float32
BF16 = jnp.bfloat16

HEAD_DIM = 64
GRID_W = 64
ROPE_THETA = 10000.0
NORM_EPS = 1e-6
NEG_INF = -1e30
A_HEADS, A_KV_HEADS, WINDOW = 4, 2, 128
SSD_HEADS, SSD_GROUPS, SSD_STATE, SSD_CONV = 4, 2, 64, 4
LRU_WIDTH, LRU_BLOCKS, LRU_CONV, LRU_C = 256, 4, 4, 8.0
D_HEADS, D_KV_HEADS = 4, 2
N_EXPERTS, N_EXPERT_GROUPS, TOPK_GROUPS, TOP_K = 64, 8, 4, 8
EXPERT_HIDDEN, SHARED_HIDDEN = 256, 256
ROUTED_SCALE = 2.5

LANES = 128
SUBLANES = 8

TM = 512
CHUNK = 128
SCAN_CHUNKS = 2
TQ_GLOBAL = 256
TQ_WINDOW = 512
KV_CHUNK = 256
KV_UNROLL = 16
BM_EXPERT = 512
X_BUFFERS = 3
SEG_ALIGN = 16
STAGE_ROWS = TM * TOP_K + N_EXPERTS * SEG_ALIGN
BIG_PIECE = 64
BIG_MAX = STAGE_ROWS // BIG_PIECE
SMALL_MAX = N_EXPERTS * (BIG_PIECE // SEG_ALIGN - 1)
VMEM_LIMIT = 48 * 1024 * 1024

C_QA, C_KA, C_VA = 0, 256, 384
C_QD, C_KD, C_VD = 512, 768, 896
C_Z, C_XBC, C_DT = 1024, 1280, 1792
C_LX, C_LG = 1920, 2176
NP_IN = 2432


def _dot(a, b):
    return jnp.dot(a, b, preferred_element_type=F32)


def _dot_nt(a, b):
    return lax.dot_general(a, b, (((1,), (1,)), ((), ())), preferred_element_type=F32)


def _dot3(a, b):
    a1 = a.astype(BF16)
    r1 = a - a1.astype(F32)
    a2 = r1.astype(BF16)
    a3 = (r1 - a2.astype(F32)).astype(BF16)
    return _dot(a1, b) + _dot(a2, b) + _dot(a3, b)


def _dot3_left(a, b):
    b1 = b.astype(BF16)
    r1 = b - b1.astype(F32)
    b2 = r1.astype(BF16)
    b3 = (r1 - b2.astype(F32)).astype(BF16)
    return _dot(a, b1) + _dot(a, b2) + _dot(a, b3)


def _silu(x):
    return x * jax.nn.sigmoid(x)


def _softplus(x):
    return jnp.maximum(x, 0.0) + jnp.log1p(jnp.exp(-jnp.abs(x)))


def _rms(x, gain):
    return x * lax.rsqrt(jnp.mean(x * x, axis=-1, keepdims=True) + NORM_EPS) * gain


def _params(sem=None):
    return pltpu.CompilerParams(dimension_semantics=sem, vmem_limit_bytes=VMEM_LIMIT)


def _adaln_kernel(c_ref, w_ref, b_ref, o_ref):
    s = _silu(c_ref[...])
    o_ref[0] = _dot(s.astype(BF16), w_ref[0].astype(BF16)) + b_ref[0]


def _adaln(cin, w_ada, b_ada):
    depth, d, n6 = w_ada.shape
    tn = 1024
    return pl.pallas_call(
        _adaln_kernel,
        grid=(depth, n6 // tn),
        in_specs=[pl.BlockSpec((SUBLANES, d), lambda l, j: (0, 0)),
                  pl.BlockSpec((1, d, tn), lambda l, j: (l, 0, j)),
                  pl.BlockSpec((1, 1, tn), lambda l, j: (l, 0, j))],
        out_specs=pl.BlockSpec((1, SUBLANES, tn), lambda l, j: (l, 0, j)),
        out_shape=jax.ShapeDtypeStruct((depth, SUBLANES, n6), F32),
        compiler_params=_params(("parallel", "parallel")),
        name="adaln",
    )(cin, w_ada, b_ada.reshape(depth, 1, n6))


def _swap_halves(t):
    w = t.shape[1]
    lane = lax.broadcasted_iota(jnp.int32, (1, w), 1)
    first = (lane & 32) == 0
    return jnp.where(first, pltpu.roll(t, w - 32, axis=1), pltpu.roll(t, 32, axis=1))


def _inproj_kernel(x_ref, xp_ref, xn_ref, xc_ref, xcp_ref, xcn_ref, shift_ref, scale_ref, gpre_ref, w_ref,
                   cos_ref, sin_ref, gq_ref, gk_ref, hm_ref, ws_ref, bs_ref, wl_ref, bl_ref,
                   qa_ref, kat_ref, va_ref, qd_ref, kdt_ref, vd_ref, z_ref, xbc_ref, dt_ref, lu_ref, lg_ref,
                   *, n_lat, seq, ctx_len, split):
    i = pl.program_id(0)
    is_lat = i * TM < n_lat
    first = jnp.logical_or(is_lat, not split)

    def pre(a_ref, b_ref):
        x = jnp.where(first, a_ref[...], b_ref[...])
        return _rms(x, gpre_ref[...]) * (1.0 + scale_ref[0]) + shift_ref[0]

    h = pre(x_ref, xc_ref)
    hb = h.astype(BF16)
    hb_ext = jnp.concatenate([pre(xp_ref, xcp_ref), h, pre(xn_ref, xcn_ref)], axis=0).astype(BF16)

    def sec(a, b):
        return _dot(hb, w_ref[:, a:b])

    row = lax.broadcasted_iota(jnp.int32, (TM, 1), 0)
    pos = jnp.where(is_lat, lax.rem(i * TM + row, seq), lax.rem(i * TM - n_lat + row, ctx_len))
    slen = jnp.where(is_lat, seq, ctx_len)

    def conv(a, b, w, bias):
        ext = _dot(hb_ext, w_ref[:, a:b])
        prev, x, nxt = ext[0:SUBLANES], ext[SUBLANES:SUBLANES + TM], ext[SUBLANES + TM:]
        xm1 = jnp.where(row == 0, prev[7:8, :], pltpu.roll(x, 1, axis=0))
        xm2 = jnp.where(row == 0, prev[6:7, :], jnp.where(row == 1, prev[7:8, :], pltpu.roll(x, 2, axis=0)))
        xp1 = jnp.where(row == TM - 1, nxt[0:1, :], pltpu.roll(x, TM - 1, axis=0))
        xm1 = jnp.where(pos >= 1, xm1, 0.0)
        xm2 = jnp.where(pos >= 2, xm2, 0.0)
        xp1 = jnp.where(pos <= slen - 2, xp1, 0.0)
        return w[0:1, :] * xm2 + w[1:2, :] * xm1 + w[2:3, :] * x + w[3:4, :] * xp1 + bias

    cos = jnp.where(is_lat, cos_ref[...], 1.0)
    sin = jnp.where(is_lat, sin_ref[...], 0.0)

    def rope(t):
        w = t.shape[1]
        return t * cos[:, :w] + _swap_halves(t) * sin[:, :w]

    def head_norm(t, gain):
        w = t.shape[1]
        ms = _dot3(t * t, hm_ref[:w, :w])
        return t * lax.rsqrt(ms + NORM_EPS) * gain

    scale = HEAD_DIM ** -0.5
    qa_ref[...] = (rope(sec(C_QA, C_KA)) * scale).astype(BF16)
    kat_ref[...] = rope(sec(C_KA, C_VA)).T.astype(BF16)
    va_ref[...] = sec(C_VA, C_QD).astype(BF16)
    qd_ref[...] = (rope(head_norm(sec(C_QD, C_KD), gq_ref[...])) * scale).astype(BF16)
    kdt_ref[...] = rope(head_norm(sec(C_KD, C_VD), gk_ref[...])).T.astype(BF16)
    vd_ref[...] = sec(C_VD, C_Z).astype(BF16)
    z_ref[...] = sec(C_Z, C_XBC)
    xbc_ref[...] = _silu(conv(C_XBC, C_DT, ws_ref[...], bs_ref[...]))
    dt_ref[...] = sec(C_DT, C_LX)
    lu_ref[...] = conv(C_LX, C_LG, wl_ref[...], bl_ref[...])
    lg_ref[...] = sec(C_LG, NP_IN)


def _mod_spec(chunk, n_lat, seq, batch, tile):
    def imap(i):
        row0 = i * tile
        seg = jnp.where(row0 < n_lat, row0 // seq, batch)
        return (seg * 6 + chunk, 0, 0)
    return imap


def _row_sources(x_lat, x_ctx, n_lat):
    d = x_lat.shape[1]
    r8 = TM // SUBLANES
    lat_tiles = n_lat // TM

    def specs(arr, tile_of):
        n8 = arr.shape[0] // SUBLANES
        tiles = arr.shape[0] // TM
        t = lambda i: jnp.clip(tile_of(i), 0, tiles - 1)
        return [pl.BlockSpec((TM, d), lambda i: (t(i), 0)),
                pl.BlockSpec((SUBLANES, d), lambda i: (jnp.clip(t(i) * r8 - 1, 0, n8 - 1), 0)),
                pl.BlockSpec((SUBLANES, d), lambda i: (jnp.clip((t(i) + 1) * r8, 0, n8 - 1), 0))]

    if x_ctx is None:
        return specs(x_lat, lambda i: i) + specs(x_lat, lambda i: 0 * i), [x_lat] * 6
    return specs(x_lat, lambda i: i) + specs(x_ctx, lambda i: i - lat_tiles), [x_lat] * 3 + [x_ctx] * 3


def _inproj(x_lat, x_ctx, mod, gpre, w_pad, cos_t, sin_t, gq, gk, hm, ws, bs, wl, bl, *, n_lat, seq, ctx_len, batch):
    n = x_lat.shape[0] + (0 if x_ctx is None else x_ctx.shape[0])
    d = x_lat.shape[1]
    nt = n // TM
    spt = seq // TM
    x_specs, x_args = _row_sources(x_lat, x_ctx, n_lat)
    row = lambda w: pl.BlockSpec((TM, w), lambda i: (i, 0))
    colT = pl.BlockSpec((LANES, TM), lambda i: (0, i))
    const = lambda a: pl.BlockSpec(a.shape, lambda i: (0,) * a.ndim)
    out_shapes = (
        jax.ShapeDtypeStruct((n, 256), BF16), jax.ShapeDtypeStruct((LANES, n), BF16),
        jax.ShapeDtypeStruct((n, LANES), BF16),
        jax.ShapeDtypeStruct((n, 256), BF16), jax.ShapeDtypeStruct((LANES, n), BF16),
        jax.ShapeDtypeStruct((n, LANES), BF16),
        jax.ShapeDtypeStruct((n, 256), F32), jax.ShapeDtypeStruct((n, 512), F32),
        jax.ShapeDtypeStruct((n, LANES), F32), jax.ShapeDtypeStruct((n, 256), F32),
        jax.ShapeDtypeStruct((n, 256), F32))
    return pl.pallas_call(
        functools.partial(_inproj_kernel, n_lat=n_lat, seq=seq, ctx_len=ctx_len, split=x_ctx is not None),
        grid=(nt,),
        in_specs=x_specs + [
                  pl.BlockSpec((1, 1, d), _mod_spec(0, n_lat, seq, batch, TM)),
                  pl.BlockSpec((1, 1, d), _mod_spec(1, n_lat, seq, batch, TM)),
                  const(gpre), const(w_pad),
                  pl.BlockSpec((TM, 256), lambda i: (i % spt, 0)),
                  pl.BlockSpec((TM, 256), lambda i: (i % spt, 0)),
                  const(gq), const(gk), const(hm), const(ws), const(bs), const(wl), const(bl)],
        out_specs=(row(256), colT, row(LANES), row(256), colT, row(LANES),
                   row(256), row(512), row(LANES), row(256), row(256)),
        out_shape=out_shapes,
        compiler_params=_params(("parallel",)),
        name="inproj",
    )(*x_args, mod, mod, gpre, w_pad, cos_t, sin_t, gq, gk, hm, ws, bs, wl, bl)


def _chunk_maps(batch, seq, ctx_len):
    ncx = ctx_len // (SCAN_CHUNKS * CHUNK)
    nl = seq // (SCAN_CHUNKS * CHUNK)
    lat_blocks = batch * nl

    def block(b, c):
        return jnp.where(c < ncx, lat_blocks + b * ncx + c, b * nl + (c - ncx))

    def fwd(b, k):
        return (block(b, k), 0)

    def bwd(b, k):
        c = jnp.where(k < ncx, ncx - 1 - k, ncx + (nl - 1 - (k - ncx)))
        return (block(b, c), 0)

    return fwd, bwd, ncx + nl


def _ssd_body(xf_ref, dtf_ref, xb_ref, dtb_ref, dtbias_ref, alog_ref, yf_ref, yb_ref, state_ref):
    ri = lax.broadcasted_iota(jnp.int32, (CHUNK, CHUNK), 0)
    ci = lax.broadcasted_iota(jnp.int32, (CHUNK, CHUNK), 1)
    lane_lo = ci < HEAD_DIM
    aneg = -jnp.exp(alog_ref[...])
    dtbias = dtbias_ref[...]

    order = [(d, s if d == 0 else SCAN_CHUNKS - 1 - s) for s in range(SCAN_CHUNKS) for d in range(2)]
    for d, sub in order:
        x_ref, dt_ref, y_ref = ((xf_ref, dtf_ref, yf_ref), (xb_ref, dtb_ref, yb_ref))[d]
        rws = slice(sub * CHUNK, (sub + 1) * CHUNK)
        causal = (ri >= ci) if d == 0 else (ci >= ri)
        tmat = jnp.where(causal, 1.0, 0.0).astype(BF16)
        xs = x_ref[rws, 0:256]
        bm = x_ref[rws, 256:384]
        cm = x_ref[rws, 384:512]
        dtp = _softplus(dt_ref[rws, :] + dtbias)
        la = dtp * aneg
        acum = _dot3_left(tmat, la)
        acum_t = _dot3(la.T, jnp.where(causal, 0.0, 1.0).astype(BF16) + jnp.where(ri == ci, 1.0, 0.0).astype(BF16))
        bt = bm.T.astype(BF16)
        cmb = cm.astype(BF16)
        bmb = bm.astype(BF16)
        tot_row = CHUNK - 1 if d == 0 else 0
        for p in range(2):
            cmask = jnp.where(lane_lo if p == 0 else jnp.logical_not(lane_lo), cmb, jnp.zeros_like(cmb))
            cb = _dot_nt(cmask, bmb)
            cols, dts, ys = [], [], []
            x_pair = xs[:, p * LANES:(p + 1) * LANES]
            for j in range(2):
                col = 4 * d + 2 * p + j
                colb = jnp.broadcast_to(acum[:, col:col + 1], (CHUNK, CHUNK))
                rowb = jnp.broadcast_to(acum_t[col:col + 1, :], (CHUNK, CHUNK))
                cols.append(colb)
                dts.append(jnp.broadcast_to(dtp[:, col:col + 1], (CHUNK, CHUNK)))
            col_pair = jnp.where(lane_lo, cols[0], cols[1])
            dt_pair = jnp.where(lane_lo, dts[0], dts[1])
            xdt = x_pair * dt_pair
            xdt_b = xdt.astype(BF16)
            for j in range(2):
                col = 4 * d + 2 * p + j
                rowb = jnp.broadcast_to(acum_t[col:col + 1, :], (CHUNK, CHUNK))
                decay = jnp.exp(jnp.where(causal, cols[j] - rowb, NEG_INF))
                ys.append(_dot((cb * decay).astype(BF16), xdt_b))
            y_intra = jnp.where(lane_lo, ys[0], ys[1])
            s_old = state_ref[d, p]
            y_inter = _dot(cmask, s_old.astype(BF16)) * jnp.exp(col_pair)
            y_ref[rws, p * LANES:(p + 1) * LANES] = y_intra + y_inter
            tot_pair = col_pair[tot_row:tot_row + 1, :]
            to_end = jnp.exp(tot_pair - col_pair)
            state_ref[d, p] = s_old * jnp.exp(tot_pair) + _dot(bt, (xdt * to_end).astype(BF16))


def _linear_scan(a, b, reverse):
    n = a.shape[0]
    row = lax.broadcasted_iota(jnp.int32, (n, 1), 0)
    s = 1
    while s < n:
        if s < SUBLANES:
            if reverse:
                ok = row < n - s
                a_sh = jnp.where(ok, pltpu.roll(a, n - s, axis=0), 1.0)
                b_sh = jnp.where(ok, pltpu.roll(b, n - s, axis=0), 0.0)
            else:
                ok = row >= s
                a_sh = jnp.where(ok, pltpu.roll(a, s, axis=0), 1.0)
                b_sh = jnp.where(ok, pltpu.roll(b, s, axis=0), 0.0)
            b = b + a * b_sh
            a = a * a_sh
        elif reverse:
            b = jnp.concatenate([b[:n - s] + a[:n - s] * b[s:], b[n - s:]], axis=0)
            a = jnp.concatenate([a[:n - s] * a[s:], a[n - s:]], axis=0)
        else:
            b = jnp.concatenate([b[:s], b[s:] + a[s:] * b[:n - s]], axis=0)
            a = jnp.concatenate([a[:s], a[s:] * a[:n - s]], axis=0)
        s *= 2
    return a, b


def _lru_body(uf_ref, ub_ref, wg_ref, bg_ref, lam_ref, hf_ref, hb_ref, carry_ref):
    for d, (u_ref, h_ref) in enumerate(((uf_ref, hf_ref), (ub_ref, hb_ref))):
        u = u_ref[...]
        gates = _dot(u.astype(BF16), wg_ref[d]) + bg_ref[d]
        r = jax.nn.sigmoid(gates[:, :LRU_WIDTH])
        ig = jax.nn.sigmoid(gates[:, LRU_WIDTH:])
        log_a = -LRU_C * r * _softplus(-lam_ref[d])
        a = jnp.exp(log_a)
        inp = jnp.sqrt(-jnp.tanh(log_a) * (1.0 + a * a)) * (ig * u)
        a_cum, b_cum = _linear_scan(a, inp, reverse=(d == 1))
        h = b_cum + a_cum * carry_ref[d, 0:1, :]
        h_ref[...] = h
        last = 0 if d == 1 else u.shape[0] - 1
        carry_ref[d, 0:1, :] = h[last:last + 1, :]


def _scans_kernel(xf_ref, dtf_ref, xb_ref, dtb_ref, dtbias_ref, alog_ref, uf_ref, ub_ref, wg_ref, bg_ref, lam_ref,
                  yf_ref, yb_ref, hf_ref, hb_ref, state_ref, carry_ref):
    @pl.when(pl.program_id(1) == 0)
    def _():
        state_ref[...] = jnp.zeros_like(state_ref)
        carry_ref[...] = jnp.zeros_like(carry_ref)

    _ssd_body(xf_ref, dtf_ref, xb_ref, dtb_ref, dtbias_ref, alog_ref, yf_ref, yb_ref, state_ref)
    _lru_body(uf_ref, ub_ref, wg_ref, bg_ref, lam_ref, hf_ref, hb_ref, carry_ref)


def _scans(xbc, dt, dtbias_row, alog_row, u, wg, bg, lam, *, batch, seq, ctx_len):
    n = xbc.shape[0]
    fwd, bwd, steps = _chunk_maps(batch, seq, ctx_len)
    rows = SCAN_CHUNKS * CHUNK
    const = lambda a: pl.BlockSpec(a.shape, lambda b, k: (0,) * a.ndim)
    blk = lambda w, order: pl.BlockSpec((rows, w), order)
    wide = jax.ShapeDtypeStruct((n, 256), F32)
    return pl.pallas_call(
        _scans_kernel,
        grid=(batch, steps),
        in_specs=[blk(512, fwd), blk(LANES, fwd), blk(512, bwd), blk(LANES, bwd), const(dtbias_row), const(alog_row),
                  blk(LRU_WIDTH, fwd), blk(LRU_WIDTH, bwd), const(wg), const(bg), const(lam)],
        out_specs=(blk(256, fwd), blk(256, bwd), blk(LRU_WIDTH, fwd), blk(LRU_WIDTH, bwd)),
        out_shape=(wide, wide, wide, wide),
        scratch_shapes=[pltpu.VMEM((2, 2, CHUNK, LANES), F32), pltpu.VMEM((2, SUBLANES, LRU_WIDTH), F32)],
        compiler_params=_params(("parallel", "arbitrary")),
        name="scans",
    )(xbc, dt, xbc, dt, dtbias_row, alog_row, u, u, wg, bg, lam)


def _stack_heads(q, g):
    qf = q.astype(F32)
    lo = g * LANES
    return jnp.concatenate([qf[:, lo:lo + HEAD_DIM], qf[:, lo + HEAD_DIM:lo + LANES]], axis=0).astype(BF16)


def _value_lanes(g):
    lane = lax.broadcasted_iota(jnp.int32, (1, LANES), 1)
    return (lane < HEAD_DIM) if g == 0 else (lane >= HEAD_DIM)


def _aug_values(v, g):
    return jnp.where(_value_lanes(g), v, jnp.ones_like(v))


def _flash_init(rows, g, sink_pair):
    if sink_pair is None:
        return jnp.full((rows, 1), NEG_INF, F32), jnp.zeros((rows, LANES), F32)
    half = lax.broadcasted_iota(jnp.int32, (rows, 1), 0) < rows // 2
    m = jnp.where(half, sink_pair[0], sink_pair[1]).astype(F32)
    acc = jnp.broadcast_to(jnp.where(_value_lanes(g), 0.0, 1.0), (rows, LANES))
    return m, acc


def _flash_update(state, q2, kt, v_aug, mask=None):
    m, acc = state
    s = _dot(q2, kt)
    if mask is not None:
        s = jnp.where(mask, s, NEG_INF)
    m_new = jnp.maximum(m, jnp.max(s, axis=-1, keepdims=True))
    p = jnp.exp(s - m_new).astype(BF16)
    acc = jnp.exp(m - m_new) * acc + _dot(p, v_aug)
    return m_new, acc


def _flash_finish(states, tq):
    pieces = []
    for g, (_, acc) in enumerate(states):
        den = (1 - g) * HEAD_DIM
        o = acc[:, g * HEAD_DIM:(g + 1) * HEAD_DIM] / acc[:, den:den + 1]
        pieces += [o[:tq], o[tq:]]
    return jnp.concatenate(pieces, axis=1)


def _group_rows(g):
    return slice(g * HEAD_DIM, (g + 1) * HEAD_DIM)


def _dense_attn_kernel(*refs, tq, seg_lens, has_sink):
    refs = list(refs)
    sink_ref = refs.pop(0) if has_sink else None
    q_ref = refs.pop(0)
    o_ref = refs.pop()
    segs = [(refs[2 * i], refs[2 * i + 1], n) for i, n in enumerate(seg_lens)]
    q = q_ref[...]
    q2 = [_stack_heads(q, g) for g in range(2)]
    states = tuple(_flash_init(2 * tq, g, (sink_ref[2 * g], sink_ref[2 * g + 1]) if has_sink else None)
                   for g in range(2))
    for kt_ref, v_ref, n_keys in segs:
        if n_keys <= KV_CHUNK:
            v = v_ref[...]
            states = tuple(_flash_update(states[g], q2[g], kt_ref[_group_rows(g), :], _aug_values(v, g))
                           for g in range(2))
        else:
            def body(c, sts, kt_ref=kt_ref, v_ref=v_ref):
                off = pl.multiple_of(c * KV_CHUNK, KV_CHUNK)
                v = v_ref[pl.ds(off, KV_CHUNK), :]
                return tuple(_flash_update(sts[g], q2[g], kt_ref[_group_rows(g), pl.ds(off, KV_CHUNK)],
                                           _aug_values(v, g)) for g in range(2))
            states = lax.fori_loop(0, n_keys // KV_CHUNK, body, states, unroll=KV_UNROLL)
    o_ref[...] = _flash_finish(states, tq).astype(o_ref.dtype)


def _dense_attn(q, kt, v, sink, *, q_row0, q_len, tq, segs, batch):
    n = q.shape[0]
    qpb = q_len // tq
    q0 = q_row0 // tq
    in_specs, args = [], []
    if sink is not None:
        in_specs.append(pl.BlockSpec(memory_space=pltpu.SMEM))
        args.append(sink)
    in_specs.append(pl.BlockSpec((tq, 256), lambda b, i: (q0 + b * qpb + i, 0)))
    args.append(q)
    for row0, klen in segs:
        k0 = row0 // klen
        in_specs.append(pl.BlockSpec((LANES, klen), lambda b, i, k0=k0: (0, k0 + b)))
        in_specs.append(pl.BlockSpec((klen, LANES), lambda b, i, k0=k0: (k0 + b, 0)))
        args += [kt, v]
    return pl.pallas_call(
        functools.partial(_dense_attn_kernel, tq=tq, seg_lens=tuple(s[1] for s in segs), has_sink=sink is not None),
        grid=(batch, qpb),
        in_specs=in_specs,
        out_specs=pl.BlockSpec((tq, 256), lambda b, i: (b * qpb + i, 0)),
        out_shape=jax.ShapeDtypeStruct((batch * q_len, 256), BF16),
        compiler_params=_params(("parallel", "parallel")),
        name="dense_attn",
    )(*args)


def _window_attn_kernel(sink_ref, q_ref, ktc_ref, vc_ref, ktp_ref, vp_ref, ktm_ref, vm_ref, ktn_ref, vn_ref, o_ref,
                        *, n_tiles):
    n = pl.program_id(1)
    nsub = TQ_WINDOW // CHUNK
    iq = lax.broadcasted_iota(jnp.int32, (2 * CHUNK, CHUNK), 0) & (CHUNK - 1)
    jk = lax.broadcasted_iota(jnp.int32, (2 * CHUNK, CHUNK), 1)
    below = jk >= iq
    above = jk <= iq
    vctx = vc_ref[...]
    for j in range(nsub):
        cols = slice(j * CHUNK, (j + 1) * CHUNK)
        q = q_ref[cols, :]
        states = []
        for g in range(2):
            q2 = _stack_heads(q, g)
            rows = _group_rows(g)
            state = _flash_init(2 * CHUNK, g, (sink_ref[2 * g], sink_ref[2 * g + 1]))
            state = _flash_update(state, q2, ktc_ref[rows, :], _aug_values(vctx, g))
            state = _flash_update(state, q2, ktm_ref[rows, cols], _aug_values(vm_ref[cols, :], g))
            if j > 0:
                prev = slice((j - 1) * CHUNK, j * CHUNK)
                state = _flash_update(state, q2, ktm_ref[rows, prev], _aug_values(vm_ref[prev, :], g), below)
            else:
                state = _flash_update(state, q2, ktp_ref[rows, :], _aug_values(vp_ref[...], g),
                                      jnp.logical_and(below, n > 0))
            if j < nsub - 1:
                nxt = slice((j + 1) * CHUNK, (j + 2) * CHUNK)
                state = _flash_update(state, q2, ktm_ref[rows, nxt], _aug_values(vm_ref[nxt, :], g), above)
            else:
                state = _flash_update(state, q2, ktn_ref[rows, :], _aug_values(vn_ref[...], g),
                                      jnp.logical_and(above, n < n_tiles - 1))
            states.append(state)
        o_ref[cols, :] = _flash_finish(states, CHUNK).astype(o_ref.dtype)


def _window_attn(q, kt, v, sink, *, batch, seq, ctx_len):
    nt = seq // TQ_WINDOW
    nsub = TQ_WINDOW // CHUNK
    nb = seq // CHUNK
    ctx0 = (batch * seq) // ctx_len
    prev = lambda b, n: b * nb + jnp.maximum(n * nsub - 1, 0)
    nxt = lambda b, n: b * nb + jnp.minimum((n + 1) * nsub, nb - 1)
    return pl.pallas_call(
        functools.partial(_window_attn_kernel, n_tiles=nt),
        grid=(batch, nt),
        in_specs=[pl.BlockSpec(memory_space=pltpu.SMEM),
                  pl.BlockSpec((TQ_WINDOW, 256), lambda b, n: (b * nt + n, 0)),
                  pl.BlockSpec((LANES, ctx_len), lambda b, n: (0, ctx0 + b)),
                  pl.BlockSpec((ctx_len, LANES), lambda b, n: (ctx0 + b, 0)),
                  pl.BlockSpec((LANES, CHUNK), lambda b, n: (0, prev(b, n))),
                  pl.BlockSpec((CHUNK, LANES), lambda b, n: (prev(b, n), 0)),
                  pl.BlockSpec((LANES, TQ_WINDOW), lambda b, n: (0, b * nt + n)),
                  pl.BlockSpec((TQ_WINDOW, LANES), lambda b, n: (b * nt + n, 0)),
                  pl.BlockSpec((LANES, CHUNK), lambda b, n: (0, nxt(b, n))),
                  pl.BlockSpec((CHUNK, LANES), lambda b, n: (nxt(b, n), 0))],
        out_specs=pl.BlockSpec((TQ_WINDOW, 256), lambda b, n: (b * nt + n, 0)),
        out_shape=jax.ShapeDtypeStruct((batch * seq, 256), BF16),
        compiler_params=_params(("parallel", "parallel")),
        name="window_attn",
    )(sink, q, kt, v, kt, v, kt, v, kt, v)


def _gelu_tanh(x):
    return 0.5 * x * (1.0 + jnp.tanh(math.sqrt(2.0 / math.pi) * (x + 0.044715 * (x * x * x))))


def _outproj_kernel(x_ref, xc_ref, gate_ref, gpost_ref, oa_ref, od_ref, oac_ref, odc_ref, yf_ref, yb_ref, xs_ref,
                    z_ref, dsk_ref, gn_ref, hf_ref, hb_ref, lg_ref, w_ref,
                    fshift_ref, fscale_ref, fpre_ref, rwt_ref, rb_ref,
                    o_ref, hffn_ref, ld_ref, wk_ref, tab_ref, *, n_lat, split):
    is_lat = pl.program_id(0) * TM < n_lat
    x = jnp.where(jnp.logical_or(is_lat, not split), x_ref[...], xc_ref[...])
    oa = jnp.where(is_lat, oa_ref[...], oac_ref[...])
    od = jnp.where(is_lat, od_ref[...], odc_ref[...])
    y_ssd = (yf_ref[...] + yb_ref[...] + xs_ref[...] * dsk_ref[...]) * _silu(z_ref[...])
    ob = _rms(y_ssd, gn_ref[...])
    oc = (hf_ref[...] + hb_ref[...]) * _gelu_tanh(lg_ref[...])
    y = (_dot(oa, w_ref[0:256, :]) + _dot(ob.astype(BF16), w_ref[256:512, :])
         + _dot(oc.astype(BF16), w_ref[512:768, :]) + _dot(od, w_ref[768:1024, :]))
    x_mid = x + gate_ref[0] * _rms(y, gpost_ref[...])
    o_ref[...] = x_mid
    _route_tile(x_mid, fshift_ref, fscale_ref, fpre_ref, rwt_ref, rb_ref, hffn_ref, ld_ref, wk_ref, tab_ref)


def _outproj(x_lat, x_ctx, mod, gpost, oa, od, oa_c, od_c, yf, yb, xbc, z, dsk, gn, hf, hb, lg, w_out,
             ffn_pre, rwt, rb, *, n_rows, n_lat, seq, batch):
    d = x_lat.shape[1]
    lat_tiles = n_lat // TM
    x_specs, x_args = _row_sources(x_lat, x_ctx, n_lat)
    row = lambda w: pl.BlockSpec((TM, w), lambda i: (i, 0))
    col = pl.BlockSpec((TOP_K, TM), lambda i: (0, i))
    lat = pl.BlockSpec((TM, 256), lambda i: (jnp.minimum(i, lat_tiles - 1), 0))
    ctx = pl.BlockSpec((TM, 256), lambda i: (jnp.maximum(i - lat_tiles, 0), 0))
    const = lambda a: pl.BlockSpec(a.shape, lambda i: (0,) * a.ndim)
    return pl.pallas_call(
        functools.partial(_outproj_kernel, n_lat=n_lat, split=x_ctx is not None),
        grid=(n_rows // TM,),
        in_specs=[x_specs[0], x_specs[3], pl.BlockSpec((1, 1, d), _mod_spec(2, n_lat, seq, batch, TM)), const(gpost),
                  lat, lat, ctx, ctx, row(256), row(256), row(256), row(256), const(dsk), const(gn),
                  row(256), row(256), row(256), const(w_out),
                  pl.BlockSpec((1, 1, d), _mod_spec(3, n_lat, seq, batch, TM)),
                  pl.BlockSpec((1, 1, d), _mod_spec(4, n_lat, seq, batch, TM)),
                  const(ffn_pre), const(rwt), const(rb)],
        out_specs=(row(d), row(d), col, col, pl.BlockSpec((SUBLANES, 2 * N_EXPERTS), lambda i: (i, 0))),
        out_shape=(jax.ShapeDtypeStruct((n_rows, d), F32),
                   jax.ShapeDtypeStruct((n_rows, d), BF16),
                   jax.ShapeDtypeStruct((TOP_K, n_rows), jnp.int32),
                   jax.ShapeDtypeStruct((TOP_K, n_rows), F32),
                   jax.ShapeDtypeStruct((n_rows // TM * SUBLANES, 2 * N_EXPERTS), jnp.int32)),
        compiler_params=_params(("parallel",)),
        name="outproj_route",
    )(x_args[0], x_args[3], mod, gpost, oa, od, oa_c, od_c, yf, yb, xbc, z, dsk, gn, hf, hb, lg, w_out,
      mod, mod, ffn_pre, rwt, rb)


def _ceil_seg(c):
    return jnp.floor((c + (SEG_ALIGN - 1)) * (1.0 / SEG_ALIGN)) * SEG_ALIGN


def _route_tile(x, shift_ref, scale_ref, gpre_ref, rwt_ref, rb_ref, hb_ref, ld_ref, wk_ref, tab_ref):
    h = _rms(x, gpre_ref[...])
    h = h * (1.0 + scale_ref[0]) + shift_ref[0]
    hb = h.astype(BF16)
    hb_ref[...] = hb

    scores = jax.nn.sigmoid(_dot_nt(rwt_ref[...], hb))
    biased = scores + rb_ref[...]
    gsz = N_EXPERTS // N_EXPERT_GROUPS
    sub = lax.broadcasted_iota(jnp.int32, (gsz, TM), 0)
    blocks, gscore = [], []
    for g in range(N_EXPERT_GROUPS):
        blk = biased[g * gsz:(g + 1) * gsz, :]
        m1 = jnp.max(blk, axis=0, keepdims=True)
        first = jnp.min(jnp.where(blk == m1, sub, gsz), axis=0, keepdims=True)
        m2 = jnp.max(jnp.where(sub == first, -jnp.inf, blk), axis=0, keepdims=True)
        blocks.append(blk)
        gscore.append(m1 + m2)
    masked = []
    for g in range(N_EXPERT_GROUPS):
        rank = jnp.zeros((1, TM), F32)
        for g2 in range(N_EXPERT_GROUPS):
            if g2 == g:
                continue
            beats = (gscore[g2] > gscore[g]) | ((gscore[g2] == gscore[g]) if g2 < g else False)
            rank = rank + jnp.where(beats, 1.0, 0.0)
        masked.append(jnp.where(rank < TOPK_GROUPS, blocks[g], -jnp.inf))
    vals = jnp.concatenate(masked, axis=0)
    eidx = lax.broadcasted_iota(jnp.int32, (N_EXPERTS, TM), 0)
    self32 = jnp.zeros((N_EXPERTS, TM), F32)
    rest = vals
    for _ in range(TOP_K):
        top = jnp.max(rest, axis=0, keepdims=True)
        first = jnp.min(jnp.where(rest == top, eidx, N_EXPERTS), axis=0, keepdims=True)
        hit = eidx == first
        self32 = jnp.where(hit, 1.0, self32)
        rest = jnp.where(hit, -jnp.inf, rest)
    sel = self32 > 0.5
    picked = jnp.where(sel, scores, 0.0)
    wdense = picked / jnp.sum(picked, axis=0, keepdims=True) * ROUTED_SCALE

    tr = lax.broadcasted_iota(jnp.int32, (TM, TM), 0)
    tc = lax.broadcasted_iota(jnp.int32, (TM, TM), 1)
    before = jnp.where(tr < tc, 1.0, 0.0).astype(BF16)
    selb = self32.astype(BF16)
    pos = _dot(selb, before)
    er = lax.broadcasted_iota(jnp.int32, (N_EXPERTS, N_EXPERTS), 0)
    ec = lax.broadcasted_iota(jnp.int32, (N_EXPERTS, N_EXPERTS), 1)
    lower = jnp.where(ec < er, 1.0, 0.0).astype(BF16)
    upper = jnp.where(er < ec, 1.0, 0.0).astype(BF16)
    ksel = _dot(lower, selb)
    cnt_col = _ceil_seg(jnp.sum(self32, axis=1, keepdims=True))
    loc_col = _dot3_left(lower, jnp.broadcast_to(cnt_col, (N_EXPERTS, LANES)))[:, 0:1]
    cnt_row = _ceil_seg(_dot_nt(jnp.ones((SUBLANES, TM), BF16), selb))
    loc_row = _dot3(cnt_row, upper)
    tab_ref[...] = jnp.concatenate([cnt_row, loc_row], axis=1).astype(jnp.int32)

    r8 = lax.broadcasted_iota(jnp.int32, (TOP_K, TM), 0)
    ld = jnp.zeros((TOP_K, TM), F32)
    wk = jnp.zeros((TOP_K, TM), F32)
    stage_row = pos + loc_col
    for k in range(TOP_K):
        one = sel & (ksel == float(k))
        ld = jnp.where(r8 == k, jnp.sum(jnp.where(one, stage_row, 0.0), axis=0, keepdims=True), ld)
        wk = jnp.where(r8 == k, jnp.sum(jnp.where(one, wdense, 0.0), axis=0, keepdims=True), wk)
    ld_ref[...] = ld.astype(jnp.int32)
    wk_ref[...] = wk


def _pow2_pieces(limit):
    bits, b = [], limit
    while b >= SEG_ALIGN:
        bits.append(b)
        b //= 2
    return bits


def _copy_pieces(n, src_ref, src0, dst_ref, dst0, sem, limit, wait, same_src=False):
    for bit in _pow2_pieces(limit):
        @pl.when((n & bit) != 0)
        def _():
            off = n & ~(2 * bit - 1)
            cp = pltpu.make_async_copy(src_ref.at[pl.ds(pl.multiple_of(src0 + (0 if same_src else off), SEG_ALIGN),
                                                          bit)],
                                       dst_ref.at[pl.ds(pl.multiple_of(dst0 + off, SEG_ALIGN), bit)], sem)
            cp.wait() if wait else cp.start()


N_PIECE_TABS = 7


def _piece_copies(tile, tabs, stage_ref, slots_ref, sem, to_slots, wait):
    nbig_ref, nsmall_ref, _, bsrc_ref, bdst_ref, ssrc_ref, sdst_ref = tabs
    for rows, n_ref, a_ref, b_ref, cap in ((BIG_PIECE, nbig_ref, bsrc_ref, bdst_ref, BIG_MAX),
                                           (SEG_ALIGN, nsmall_ref, ssrc_ref, sdst_ref, SMALL_MAX)):
        def body(p, c, rows=rows, a_ref=a_ref, b_ref=b_ref, cap=cap):
            src = stage_ref.at[pl.ds(pl.multiple_of(a_ref[tile * cap + p], SEG_ALIGN), rows)]
            dst = slots_ref.at[pl.ds(pl.multiple_of(b_ref[tile * cap + p], SEG_ALIGN), rows)]
            cp = pltpu.make_async_copy(src, dst, sem) if to_slots else pltpu.make_async_copy(dst, src, sem)
            cp.wait() if wait else cp.start()
            return c
        lax.fori_loop(0, n_ref[tile], body, 0)


def _used_blocks(tile, tabs):
    return (tabs[2][tile] + TM - 1) // TM


def _for_used_blocks(used, body):
    always = TOP_K + 1
    for b in range(always):
        body(b)
    for b in range(always, STAGE_ROWS // TM):
        @pl.when(b < used)
        def _():
            body(b)


def _stage_rows_iota():
    return lax.broadcasted_iota(jnp.int32, (TM // 2, TM), 0).astype(F32).astype(BF16)


def _pick_matrix(ld, base, vals, jrow):
    rel = (ld - base).astype(F32)
    rel = jnp.where(jnp.logical_and(rel >= 0.0, rel < TM // 2), rel, -1.0).astype(BF16)
    out = jnp.zeros((TM // 2, TM), BF16)
    for k in range(TOP_K):
        out = jnp.where(rel[k:k + 1, :] == jrow, vals[k:k + 1, :], out)
    return out


def _dispatch_kernel(*refs):
    tabs = refs[:N_PIECE_TABS]
    pstart_ref, npad_ref, hb_ref, ld_ref, xs_ref, stage, zbuf, sem, zsem = refs[N_PIECE_TABS:]
    i = pl.program_id(0)

    @pl.when(i == 0)
    def _():
        zbuf[...] = jnp.zeros_like(zbuf)
        for wait in (False, True):
            def body(e, c, wait=wait):
                _copy_pieces(npad_ref[e], zbuf, 0, xs_ref, pstart_ref[e], zsem, BM_EXPERT // 2, wait, same_src=True)
                return c
            lax.fori_loop(0, N_EXPERTS, body, 0)

    ld = ld_ref[...]
    hb = hb_ref[...]
    jrow = _stage_rows_iota()
    ones = jnp.ones((TOP_K, TM), BF16)

    cur = stage.at[i & 1]

    def block(b):
        for half in range(2):
            base = b * TM + half * (TM // 2)
            cur[base:base + TM // 2, :] = _dot(_pick_matrix(ld, base, ones, jrow), hb).astype(BF16)

    _for_used_blocks(_used_blocks(i, tabs), block)

    @pl.when(i > 0)
    def _():
        _piece_copies(i - 1, tabs, stage.at[(i - 1) & 1], xs_ref, sem, True, True)

    _piece_copies(i, tabs, cur, xs_ref, sem, True, False)

    @pl.when(i == pl.num_programs(0) - 1)
    def _():
        _piece_copies(i, tabs, cur, xs_ref, sem, True, True)


def _dispatch(tabs, pad_start, n_pad, hb, ld, n_slots):
    n, d = hb.shape
    grid_spec = pltpu.PrefetchScalarGridSpec(
        num_scalar_prefetch=N_PIECE_TABS + 2,
        grid=(n // TM,),
        in_specs=[pl.BlockSpec((TM, d), lambda i, *_: (i, 0)),
                  pl.BlockSpec((TOP_K, TM), lambda i, *_: (0, i))],
        out_specs=pl.BlockSpec(memory_space=pl.ANY),
        scratch_shapes=[pltpu.VMEM((2, STAGE_ROWS, d), BF16), pltpu.VMEM((BM_EXPERT // 2, d), BF16),
                        pltpu.SemaphoreType.DMA(()), pltpu.SemaphoreType.DMA(())],
    )
    return pl.pallas_call(
        _dispatch_kernel,
        grid_spec=grid_spec,
        out_shape=jax.ShapeDtypeStruct((n_slots, d), BF16),
        compiler_params=_params(("arbitrary",)),
        name="moe_dispatch",
    )(*tabs, pad_start, n_pad, hb, ld)


def _expert_kernel(be_ref, na_ref, nxt_ref, slot_ref, xs_hbm, wg_hbm, wu_hbm, wd_hbm, ys_ref,
                   xbuf, wg_raw, wu_raw, wd_raw, wgub, wdb, xsem, wsem, *, layer):
    i = pl.program_id(0)
    n_act = na_ref[0]

    def weight_copies(e, slot):
        return [pltpu.make_async_copy(src.at[layer, e], dst.at[slot], wsem.at[slot, j])
                for j, (src, dst) in enumerate(((wg_hbm, wg_raw), (wu_hbm, wu_raw), (wd_hbm, wd_raw)))]

    def rows_copy(blk):
        buf = lax.rem(blk, X_BUFFERS)
        return pltpu.make_async_copy(xs_hbm.at[pl.ds(pl.multiple_of(blk * BM_EXPERT, BM_EXPERT), BM_EXPERT)],
                                     xbuf.at[buf], xsem.at[buf])

    @pl.when(i < n_act)
    def _():
        e, slot = be_ref[i], slot_ref[i]

        @pl.when(i == 0)
        def _():
            for cp in weight_copies(e, slot):
                cp.start()
            for j in range(X_BUFFERS - 1):
                @pl.when(j < n_act)
                def _():
                    rows_copy(j).start()

        @pl.when(i + (X_BUFFERS - 1) < n_act)
        def _():
            rows_copy(i + (X_BUFFERS - 1)).start()

        @pl.when(jnp.logical_or(i == 0, e != be_ref[jnp.maximum(i - 1, 0)]))
        def _():
            for cp in weight_copies(e, slot):
                cp.wait()
            wgub[:, :EXPERT_HIDDEN] = wg_raw[slot].astype(BF16)
            wgub[:, EXPERT_HIDDEN:] = wu_raw[slot].astype(BF16)
            wdb[...] = wd_raw[slot].astype(BF16)

            @pl.when(nxt_ref[i] != e)
            def _():
                for cp in weight_copies(nxt_ref[i], 1 - slot):
                    cp.start()

        rows_copy(i).wait()
        gu = _dot(xbuf[lax.rem(i, X_BUFFERS)], wgub[...])
        hid = _silu(gu[:, :EXPERT_HIDDEN]) * gu[:, EXPERT_HIDDEN:]
        ys_ref[...] = _dot(hid.astype(BF16), wdb[...]).astype(ys_ref.dtype)


def _experts(block_e, n_active, next_e, w_slot, xs, wg, wu, wd, layer):
    n_slots, d = xs.shape
    nb = n_slots // BM_EXPERT
    tiles = pl.BlockSpec((BM_EXPERT, d), lambda i, be, na, *_: (jnp.minimum(i, na[0] - 1), 0))
    anywhere = pl.BlockSpec(memory_space=pl.ANY)
    grid_spec = pltpu.PrefetchScalarGridSpec(
        num_scalar_prefetch=4,
        grid=(nb,),
        in_specs=[anywhere, anywhere, anywhere, anywhere],
        out_specs=tiles,
        scratch_shapes=[pltpu.VMEM((X_BUFFERS, BM_EXPERT, d), BF16),
                        pltpu.VMEM((2, d, EXPERT_HIDDEN), F32), pltpu.VMEM((2, d, EXPERT_HIDDEN), F32),
                        pltpu.VMEM((2, EXPERT_HIDDEN, d), F32),
                        pltpu.VMEM((d, 2 * EXPERT_HIDDEN), BF16),
                        pltpu.VMEM((EXPERT_HIDDEN, d), BF16),
                        pltpu.SemaphoreType.DMA((X_BUFFERS,)), pltpu.SemaphoreType.DMA((2, 3))],
    )
    return pl.pallas_call(
        functools.partial(_expert_kernel, layer=layer),
        grid_spec=grid_spec,
        out_shape=jax.ShapeDtypeStruct((n_slots, d), BF16),
        compiler_params=_params(("arbitrary",)),
        name="moe_experts",
    )(block_e, n_active, next_e, w_slot, xs, wg, wu, wd)


def _combine_kernel(*refs):
    tabs = refs[:N_PIECE_TABS]
    (ys_ref, ld_ref, wk_ref, hb_ref, x_ref, gate_ref, gpost_ref, sg_ref, su_ref, sd_ref, o_ref,
     stage, acc_ref, sem) = refs[N_PIECE_TABS:]
    i = pl.program_id(0)

    @pl.when(i == 0)
    def _():
        stage[...] = jnp.zeros_like(stage)
        _piece_copies(0, tabs, stage.at[0], ys_ref, sem, False, False)

    cur = stage.at[i & 1]
    _piece_copies(i, tabs, cur, ys_ref, sem, False, True)

    @pl.when(i + 1 < pl.num_programs(0))
    def _():
        _piece_copies(i + 1, tabs, stage.at[(i + 1) & 1], ys_ref, sem, False, False)

    hb = hb_ref[...]
    acc_ref[...] = _dot((_silu(_dot(hb, sg_ref[...])) * _dot(hb, su_ref[...])).astype(BF16), sd_ref[...])

    ld = ld_ref[...]
    wkb = wk_ref[...].astype(BF16)
    jrow = _stage_rows_iota()

    def block(b):
        for half in range(2):
            base = b * TM + half * (TM // 2)
            weights = _pick_matrix(ld, base, wkb, jrow)
            acc_ref[...] += lax.dot_general(weights, cur[base:base + TM // 2, :], (((0,), (0,)), ((), ())),
                                            preferred_element_type=F32)

    _for_used_blocks(_used_blocks(i, tabs), block)
    o_ref[...] = x_ref[...] + gate_ref[0] * _rms(acc_ref[...], gpost_ref[...])


def _combine(tabs, ys, ld, wk, hb, xu, mod, gpost, sg, su, sd, *, n_rows, n_lat, seq, batch):
    d = xu.shape[1]
    row = lambda w: pl.BlockSpec((TM, w), lambda i, *_: (i, 0))
    col = pl.BlockSpec((TOP_K, TM), lambda i, *_: (0, i))
    const = lambda a: pl.BlockSpec(a.shape, lambda i, *_: (0,) * a.ndim)
    mod_map = _mod_spec(5, n_lat, seq, batch, TM)
    grid_spec = pltpu.PrefetchScalarGridSpec(
        num_scalar_prefetch=N_PIECE_TABS,
        grid=(n_rows // TM,),
        in_specs=[pl.BlockSpec(memory_space=pl.ANY), col, col, row(d), row(d),
                  pl.BlockSpec((1, 1, d), lambda i, *_: mod_map(i)),
                  const(gpost), const(sg), const(su), const(sd)],
        out_specs=row(d),
        scratch_shapes=[pltpu.VMEM((2, STAGE_ROWS, d), BF16), pltpu.VMEM((TM, d), F32), pltpu.SemaphoreType.DMA(())],
    )
    return pl.pallas_call(
        _combine_kernel,
        grid_spec=grid_spec,
        out_shape=jax.ShapeDtypeStruct((n_rows, d), F32),
        compiler_params=_params(("arbitrary",)),
        name="moe_combine",
    )(*tabs, ys, ld, wk, hb, xu, mod, gpost, sg, su, sd)


def _deinterleave(w):
    cols = w.shape[-1]
    perm = jnp.concatenate([jnp.arange(0, HEAD_DIM, 2), jnp.arange(1, HEAD_DIM, 2)])
    idx = (jnp.arange(cols // HEAD_DIM)[:, None] * HEAD_DIM + perm[None, :]).reshape(-1)
    return w[..., idx]


def _pad_in_proj(w_in):
    d = w_in.shape[0]
    o = 0
    parts = {}
    for name, width in (("qa", 256), ("ka", 128), ("va", 128), ("z", 256), ("xs", 256), ("bm", 128), ("cm", 128),
                        ("dtf", 4), ("dtb", 4), ("lx", 256), ("lg", 256), ("qd", 256), ("kd", 128), ("vd", 128)):
        parts[name] = w_in[:, o:o + width]
        o += width
    dt = jnp.concatenate([parts["dtf"], parts["dtb"], jnp.zeros((d, LANES - 8), w_in.dtype)], axis=1)
    cols = [_deinterleave(parts["qa"]), _deinterleave(parts["ka"]), parts["va"],
            _deinterleave(parts["qd"]), _deinterleave(parts["kd"]), parts["vd"],
            parts["z"], parts["xs"], parts["bm"], parts["cm"], dt, parts["lx"], parts["lg"]]
    return jnp.concatenate(cols, axis=1).astype(BF16)


def _rope_tables(seq):
    t = jnp.arange(seq)
    rowp = (t // GRID_W).astype(F32)
    colp = (t % GRID_W).astype(F32)
    axis_dim = HEAD_DIM // 2
    inv_freq = ROPE_THETA ** (-jnp.arange(0, axis_dim, 2, dtype=F32) / axis_dim)
    ang = jnp.concatenate([rowp[:, None] * inv_freq, colp[:, None] * inv_freq], axis=-1)
    cos, sin = jnp.cos(ang), jnp.sin(ang)
    cos_h = jnp.concatenate([cos, cos], axis=-1)
    sin_h = jnp.concatenate([-sin, sin], axis=-1)
    return jnp.tile(cos_h, (1, 4)), jnp.tile(sin_h, (1, 4))


def _block_diag(w):
    nb, bd, _ = w.shape
    eye = jnp.eye(nb, dtype=w.dtype)
    return (eye[:, None, :, None] * w[:, :, None, :]).reshape(nb * bd, nb * bd)


def _piece_table(counts, cap, stage0, slot0, rows, ids):
    ends = jnp.cumsum(counts, axis=1)
    q = jnp.arange(cap, dtype=jnp.int32)
    owner = jnp.sum((ends[:, None, :] <= q[None, :, None]).astype(jnp.int32), axis=-1)
    mine = owner[:, :, None] == ids
    pick = lambda v: jnp.sum(jnp.where(mine, v[:, None, :], 0), axis=-1)
    step = rows * (q[None, :] - pick(ends - counts))
    return (pick(stage0) + step).reshape(-1), (pick(slot0) + step).reshape(-1)


def _lane_row(fwd, bwd):
    return jnp.concatenate([fwd, bwd, jnp.zeros((LANES - 8,), F32)]).reshape(1, LANES)


def kernel(x, c, ctx, c_ctx, w_ada, b_ada, g_mix_pre, g_mix_post, g_ffn_pre, g_ffn_post, w_in, w_out, a_sink,
           ssd_conv_w, ssd_conv_b, ssd_dt_bias, ssd_a_log, ssd_d, ssd_norm, lru_conv_w, lru_conv_b, lru_w_a,
           lru_b_a, lru_w_i, lru_b_i, lru_lambda, d_q_norm, d_k_norm, router_w, router_bias, exp_w_gate,
           exp_w_up, exp_w_down, sh_w_gate, sh_w_up, sh_w_down):
    batch, seq, d = x.shape
    ctx_len = ctx.shape[1]
    depth = w_ada.shape[0]
    n_lat = batch * seq
    n_ctx = batch * ctx_len
    n_all = n_lat + n_ctx
    assert seq % TM == 0 and n_ctx % TM == 0 and ctx_len >= LRU_CONV
    assert seq % TQ_WINDOW == 0 and seq % (SCAN_CHUNKS * CHUNK) == 0 and ctx_len % (SCAN_CHUNKS * CHUNK) == 0
    assert ctx_len <= KV_CHUNK and seq % KV_CHUNK == 0 and seq % TQ_GLOBAL == 0 and batch + 1 <= SUBLANES

    xu, xu_ctx = x.reshape(n_lat, d), ctx.reshape(n_ctx, d)
    cin = jnp.concatenate([c, c_ctx[None, :], jnp.zeros((SUBLANES - batch - 1, d), F32)], axis=0)
    mod_all = _adaln(cin, w_ada, b_ada)
    cos_t, sin_t = _rope_tables(seq)
    hm = jnp.kron(jnp.eye(4, dtype=F32), jnp.full((HEAD_DIM, HEAD_DIM), 1.0 / HEAD_DIM, F32)).astype(BF16)

    for l in range(depth):
        with_ctx = l < depth - 1
        mod = mod_all[l].reshape(SUBLANES * 6, 1, d)
        gq = jnp.tile(_deinterleave(d_q_norm[l]), 4).reshape(1, 256)
        gk = jnp.tile(_deinterleave(d_k_norm[l]), 2).reshape(1, LANES)
        qa, kat, va, qd, kdt, vd, z, xbc, dt, lu, lg = _inproj(
            xu, xu_ctx, mod, g_mix_pre[l].reshape(1, d), _pad_in_proj(w_in[l]), cos_t, sin_t, gq, gk, hm,
            ssd_conv_w[l], ssd_conv_b[l].reshape(1, -1), lru_conv_w[l], lru_conv_b[l].reshape(1, -1),
            n_lat=n_lat, seq=seq, ctx_len=ctx_len, batch=batch)
        wg = jnp.stack([jnp.concatenate([_block_diag(lru_w_a[l, dd]), _block_diag(lru_w_i[l, dd])], axis=1)
                        for dd in range(2)]).astype(BF16)
        bg = jnp.concatenate([lru_b_a[l], lru_b_i[l]], axis=1).reshape(2, 1, 2 * LRU_WIDTH)
        yf, yb, hf, hb = _scans(xbc, dt, _lane_row(ssd_dt_bias[l, 0], ssd_dt_bias[l, 1]),
                                _lane_row(ssd_a_log[l, 0], ssd_a_log[l, 1]), lu, wg, bg,
                                lru_lambda[l].reshape(2, 1, LRU_WIDTH), batch=batch, seq=seq, ctx_len=ctx_len)

        oa = _window_attn(qa, kat, va, a_sink[l], batch=batch, seq=seq, ctx_len=ctx_len)
        od = _dense_attn(qd, kdt, vd, None, q_row0=0, q_len=seq, tq=TQ_GLOBAL,
                         segs=[(n_lat, ctx_len), (0, seq)], batch=batch)
        if with_ctx:
            oa_c = _dense_attn(qa, kat, va, a_sink[l], q_row0=n_lat, q_len=ctx_len, tq=ctx_len,
                               segs=[(n_lat, ctx_len)], batch=batch)
            od_c = _dense_attn(qd, kdt, vd, None, q_row0=n_lat, q_len=ctx_len, tq=ctx_len,
                               segs=[(n_lat, ctx_len)], batch=batch)
        else:
            oa_c, od_c = oa, od
        n_rows = n_all if with_ctx else n_lat

        dsk = jnp.repeat(ssd_d[l], HEAD_DIM).reshape(1, 256)
        xu_mid, hb_ffn, ld, wk, tab = _outproj(
            xu, xu_ctx, mod, g_mix_post[l].reshape(1, d), oa, od, oa_c, od_c, yf, yb, xbc, z, dsk,
            ssd_norm[l].reshape(1, 256), hf, hb, lg, w_out[l].astype(BF16),
            g_ffn_pre[l].reshape(1, d), router_w[l].T.astype(BF16), router_bias[l].reshape(N_EXPERTS, 1),
            n_rows=n_rows, n_lat=n_lat, seq=seq, batch=batch)
        n_tiles = n_rows // TM
        tab = tab.reshape(n_tiles, SUBLANES, 2 * N_EXPERTS)[:, 0, :]
        seg_cnt, seg_loc = tab[:, :N_EXPERTS], tab[:, N_EXPERTS:]
        counts = jnp.sum(seg_cnt, axis=0)
        padded = (counts + BM_EXPERT - 1) // BM_EXPERT * BM_EXPERT
        padded_end = jnp.cumsum(padded)
        offs = padded_end - padded
        seg_off = offs[None, :] + jnp.cumsum(seg_cnt, axis=0) - seg_cnt
        n_blocks = (n_rows * TOP_K + n_tiles * N_EXPERTS * SEG_ALIGN) // BM_EXPERT + N_EXPERTS
        n_active = (padded_end[-1] // BM_EXPERT).astype(jnp.int32).reshape(1)
        block_start = jnp.arange(n_blocks, dtype=jnp.int32) * BM_EXPERT
        block_e = jnp.minimum(jnp.sum((padded_end[None, :] <= block_start[:, None]).astype(jnp.int32), axis=1),
                              N_EXPERTS - 1)
        ids = jnp.arange(N_EXPERTS, dtype=jnp.int32)
        n_big = seg_cnt // BIG_PIECE
        n_small = (seg_cnt % BIG_PIECE) // SEG_ALIGN
        tabs = (jnp.sum(n_big, axis=1), jnp.sum(n_small, axis=1), seg_loc[:, -1] + seg_cnt[:, -1],
                *_piece_table(n_big, BIG_MAX, seg_loc, seg_off, BIG_PIECE, ids),
                *_piece_table(n_small, SMALL_MAX, seg_loc + n_big * BIG_PIECE, seg_off + n_big * BIG_PIECE,
                              SEG_ALIGN, ids))
        xs = _dispatch(tabs, offs + counts, padded - counts, hb_ffn, ld, n_blocks * BM_EXPERT)
        has_rows = padded > 0
        later = jnp.logical_and(ids[None, :] > ids[:, None], has_rows[None, :])
        nxt_of = jnp.min(jnp.where(later, ids[None, :], N_EXPERTS), axis=1)
        nxt_of = jnp.where(nxt_of == N_EXPERTS, ids, nxt_of)
        slot_of = (jnp.cumsum(has_rows.astype(jnp.int32)) - 1) & 1
        own = block_e[:, None] == ids[None, :]
        next_e = jnp.sum(jnp.where(own, nxt_of[None, :], 0), axis=1)
        w_slot = jnp.sum(jnp.where(own, slot_of[None, :], 0), axis=1)
        ys = _experts(block_e, n_active, next_e, w_slot, xs, exp_w_gate, exp_w_up, exp_w_down, l)
        xu = _combine(tabs, ys, ld, wk, hb_ffn, xu_mid, mod, g_ffn_post[l].reshape(1, d), sh_w_gate[l].astype(BF16),
                      sh_w_up[l].astype(BF16), sh_w_down[l].astype(BF16),
                      n_rows=n_rows, n_lat=n_lat, seq=seq, batch=batch)
        xu_ctx = None
    return xu[:n_lat].reshape(batch, seq, d)
```

```python
import functools
import math

import jax
import jax.numpy as jnp
from jax import lax
from jax.experimental import pallas as pl
from jax.experimental.pallas import tpu as pltpu

F32 = jnp.float32
BF16 = jnp.bfloat16

HEAD_DIM = 64
GRID_W = 64
ROPE_THETA = 10000.0
NORM_EPS = 1e-6
NEG_INF = -1e30
A_HEADS, A_KV_HEADS, WINDOW = 4, 2, 128
SSD_HEADS, SSD_GROUPS, SSD_STATE, SSD_CONV = 4, 2, 64, 4
LRU_WIDTH, LRU_BLOCKS, LRU_CONV, LRU_C = 256, 4, 4, 8.0
D_HEADS, D_KV_HEADS = 4, 2
N_EXPERTS, N_EXPERT_GROUPS, TOPK_GROUPS, TOP_K = 64, 8, 4, 8
EXPERT_HIDDEN, SHARED_HIDDEN = 256, 256
ROUTED_SCALE = 2.5

LANES = 128
SUBLANES = 8

TM = 512
CHUNK = 128
SCAN_CHUNKS = 2
TQ_GLOBAL = 256
TQ_WINDOW = 512
KV_CHUNK = 256
KV_UNROLL = 16
BM_EXPERT = 512
X_BUFFERS = 3
SEG_ALIGN = 16
STAGE_ROWS = TM * TOP_K + N_EXPERTS * SEG_ALIGN
BIG_PIECE = 64
BIG_MAX = STAGE_ROWS // BIG_PIECE
SMALL_MAX = N_EXPERTS * (BIG_PIECE // SEG_ALIGN - 1)
VMEM_LIMIT = 48 * 1024 * 1024

C_QA, C_KA, C_VA = 0, 256, 384
C_QD, C_KD, C_VD = 512, 768, 896
C_Z, C_XBC, C_DT = 1024, 1280, 1792
C_LX, C_LG = 1920, 2176
NP_IN = 2432


def _dot(a, b):
    return jnp.dot(a, b, preferred_element_type=F32)


def _dot_nt(a, b):
    return lax.dot_general(a, b, (((1,), (1,)), ((), ())), preferred_element_type=F32)


def _dot3(a, b):
    a1 = a.astype(BF16)
    r1 = a - a1.astype(F32)
    a2 = r1.astype(BF16)
    a3 = (r1 - a2.astype(F32)).astype(BF16)
    return _dot(a1, b) + _dot(a2, b) + _dot(a3, b)


def _dot3_left(a, b):
    b1 = b.astype(BF16)
    r1 = b - b1.astype(F32)
    b2 = r1.astype(BF16)
    b3 = (r1 - b2.astype(F32)).astype(BF16)
    return _dot(a, b1) + _dot(a, b2) + _dot(a, b3)


def _silu(x):
    return x * jax.nn.sigmoid(x)


def _softplus(x):
    return jnp.maximum(x, 0.0) + jnp.log1p(jnp.exp(-jnp.abs(x)))


def _rms(x, gain):
    return x * lax.rsqrt(jnp.mean(x * x, axis=-1, keepdims=True) + NORM_EPS) * gain


def _params(sem=None):
    return pltpu.CompilerParams(dimension_semantics=sem, vmem_limit_bytes=VMEM_LIMIT)


def _adaln_kernel(c_ref, w_ref, b_ref, o_ref):
    s = _silu(c_ref[...])
    o_ref[0] = _dot(s.astype(BF16), w_ref[0].astype(BF16)) + b_ref[0]


def _adaln(cin, w_ada, b_ada):
    depth, d, n6 = w_ada.shape
    tn = 1024
    return pl.pallas_call(
        _adaln_kernel,
        grid=(depth, n6 // tn),
        in_specs=[pl.BlockSpec((SUBLANES, d), lambda l, j: (0, 0)),
                  pl.BlockSpec((1, d, tn), lambda l, j: (l, 0, j)),
                  pl.BlockSpec((1, 1, tn), lambda l, j: (l, 0, j))],
        out_specs=pl.BlockSpec((1, SUBLANES, tn), lambda l, j: (l, 0, j)),
        out_shape=jax.ShapeDtypeStruct((depth, SUBLANES, n6), F32),
        compiler_params=_params(("parallel", "parallel")),
        name="adaln",
    )(cin, w_ada, b_ada.reshape(depth, 1, n6))


def _swap_halves(t):
    w = t.shape[1]
    lane = lax.broadcasted_iota(jnp.int32, (1, w), 1)
    first = (lane & 32) == 0
    return jnp.where(first, pltpu.roll(t, w - 32, axis=1), pltpu.roll(t, 32, axis=1))


def _inproj_kernel(x_ref, xp_ref, xn_ref, xc_ref, xcp_ref, xcn_ref, shift_ref, scale_ref, gpre_ref, w_ref,
                   cos_ref, sin_ref, gq_ref, gk_ref, hm_ref, ws_ref, bs_ref, wl_ref, bl_ref,
                   qa_ref, kat_ref, va_ref, qd_ref, kdt_ref, vd_ref, z_ref, xbc_ref, dt_ref, lu_ref, lg_ref,
                   *, n_lat, seq, ctx_len, split):
    i = pl.program_id(0)
    is_lat = i * TM < n_lat
    first = jnp.logical_or(is_lat, not split)

    def pre(a_ref, b_ref):
        x = jnp.where(first, a_ref[...], b_ref[...])
        return _rms(x, gpre_ref[...]) * (1.0 + scale_ref[0]) + shift_ref[0]

    h = pre(x_ref, xc_ref)
    hb = h.astype(BF16)
    hb_ext = jnp.concatenate([pre(xp_ref, xcp_ref), h, pre(xn_ref, xcn_ref)], axis=0).astype(BF16)

    def sec(a, b):
        return _dot(hb, w_ref[:, a:b])

    row = lax.broadcasted_iota(jnp.int32, (TM, 1), 0)
    pos = jnp.where(is_lat, lax.rem(i * TM + row, seq), lax.rem(i * TM - n_lat + row, ctx_len))
    slen = jnp.where(is_lat, seq, ctx_len)

    def conv(a, b, w, bias):
        ext = _dot(hb_ext, w_ref[:, a:b])
        prev, x, nxt = ext[0:SUBLANES], ext[SUBLANES:SUBLANES + TM], ext[SUBLANES + TM:]
        xm1 = jnp.where(row == 0, prev[7:8, :], pltpu.roll(x, 1, axis=0))
        xm2 = jnp.where(row == 0, prev[6:7, :], jnp.where(row == 1, prev[7:8, :], pltpu.roll(x, 2, axis=0)))
        xp1 = jnp.where(row == TM - 1, nxt[0:1, :], pltpu.roll(x, TM - 1, axis=0))
        xm1 = jnp.where(pos >= 1, xm1, 0.0)
        xm2 = jnp.where(pos >= 2, xm2, 0.0)
        xp1 = jnp.where(pos <= slen - 2, xp1, 0.0)
        return w[0:1, :] * xm2 + w[1:2, :] * xm1 + w[2:3, :] * x + w[3:4, :] * xp1 + bias

    cos = jnp.where(is_lat, cos_ref[...], 1.0)
    sin = jnp.where(is_lat, sin_ref[...], 0.0)

    def rope(t):
        w = t.shape[1]
        return t * cos[:, :w] + _swap_halves(t) * sin[:, :w]

    def head_norm(t, gain):
        w = t.shape[1]
        ms = _dot3(t * t, hm_ref[:w, :w])
        return t * lax.rsqrt(ms + NORM_EPS) * gain

    scale = HEAD_DIM ** -0.5
    qa_ref[...] = (rope(sec(C_QA, C_KA)) * scale).astype(BF16)
    kat_ref[...] = rope(sec(C_KA, C_VA)).T.astype(BF16)
    va_ref[...] = sec(C_VA, C_QD).astype(BF16)
    qd_ref[...] = (rope(head_norm(sec(C_QD, C_KD), gq_ref[...])) * scale).astype(BF16)
    kdt_ref[...] = rope(head_norm(sec(C_KD, C_VD), gk_ref[...])).T.astype(BF16)
    vd_ref[...] = sec(C_VD, C_Z).astype(BF16)
    z_ref[...] = sec(C_Z, C_XBC)
    xbc_ref[...] = _silu(conv(C_XBC, C_DT, ws_ref[...], bs_ref[...]))
    dt_ref[...] = sec(C_DT, C_LX)
    lu_ref[...] = conv(C_LX, C_LG, wl_ref[...], bl_ref[...])
    lg_ref[...] = sec(C_LG, NP_IN)


def _mod_spec(chunk, n_lat, seq, batch, tile):
    def imap(i):
        row0 = i * tile
        seg = jnp.where(row0 < n_lat, row0 // seq, batch)
        return (seg * 6 + chunk, 0, 0)
    return imap


def _row_sources(x_lat, x_ctx, n_lat):
    d = x_lat.shape[1]
    r8 = TM // SUBLANES
    lat_tiles = n_lat // TM

    def specs(arr, tile_of):
        n8 = arr.shape[0] // SUBLANES
        tiles = arr.shape[0] // TM
        t = lambda i: jnp.clip(tile_of(i), 0, tiles - 1)
        return [pl.BlockSpec((TM, d), lambda i: (t(i), 0)),
                pl.BlockSpec((SUBLANES, d), lambda i: (jnp.clip(t(i) * r8 - 1, 0, n8 - 1), 0)),
                pl.BlockSpec((SUBLANES, d), lambda i: (jnp.clip((t(i) + 1) * r8, 0, n8 - 1), 0))]

    if x_ctx is None:
        return specs(x_lat, lambda i: i) + specs(x_lat, lambda i: 0 * i), [x_lat] * 6
    return specs(x_lat, lambda i: i) + specs(x_ctx, lambda i: i - lat_tiles), [x_lat] * 3 + [x_ctx] * 3


def _inproj(x_lat, x_ctx, mod, gpre, w_pad, cos_t, sin_t, gq, gk, hm, ws, bs, wl, bl, *, n_lat, seq, ctx_len, batch):
    n = x_lat.shape[0] + (0 if x_ctx is None else x_ctx.shape[0])
    d = x_lat.shape[1]
    nt = n // TM
    spt = seq // TM
    x_specs, x_args = _row_sources(x_lat, x_ctx, n_lat)
    row = lambda w: pl.BlockSpec((TM, w), lambda i: (i, 0))
    colT = pl.BlockSpec((LANES, TM), lambda i: (0, i))
    const = lambda a: pl.BlockSpec(a.shape, lambda i: (0,) * a.ndim)
    out_shapes = (
        jax.ShapeDtypeStruct((n, 256), BF16), jax.ShapeDtypeStruct((LANES, n), BF16),
        jax.ShapeDtypeStruct((n, LANES), BF16),
        jax.ShapeDtypeStruct((n, 256), BF16), jax.ShapeDtypeStruct((LANES, n), BF16),
        jax.ShapeDtypeStruct((n, LANES), BF16),
        jax.ShapeDtypeStruct((n, 256), F32), jax.ShapeDtypeStruct((n, 512), F32),
        jax.ShapeDtypeStruct((n, LANES), F32), jax.ShapeDtypeStruct((n, 256), F32),
        jax.ShapeDtypeStruct((n, 256), F32))
    return pl.pallas_call(
        functools.partial(_inproj_kernel, n_lat=n_lat, seq=seq, ctx_len=ctx_len, split=x_ctx is not None),
        grid=(nt,),
        in_specs=x_specs + [
                  pl.BlockSpec((1, 1, d), _mod_spec(0, n_lat, seq, batch, TM)),
                  pl.BlockSpec((1, 1, d), _mod_spec(1, n_lat, seq, batch, TM)),
                  const(gpre), const(w_pad),
                  pl.BlockSpec((TM, 256), lambda i: (i % spt, 0)),
                  pl.BlockSpec((TM, 256), lambda i: (i % spt, 0)),
                  const(gq), const(gk), const(hm), const(ws), const(bs), const(wl), const(bl)],
        out_specs=(row(256), colT, row(LANES), row(256), colT, row(LANES),
                   row(256), row(512), row(LANES), row(256), row(256)),
        out_shape=out_shapes,
        compiler_params=_params(("parallel",)),
        name="inproj",
    )(*x_args, mod, mod, gpre, w_pad, cos_t, sin_t, gq, gk, hm, ws, bs, wl, bl)


def _chunk_maps(batch, seq, ctx_len):
    ncx = ctx_len // (SCAN_CHUNKS * CHUNK)
    nl = seq // (SCAN_CHUNKS * CHUNK)
    lat_blocks = batch * nl

    def block(b, c):
        return jnp.where(c < ncx, lat_blocks + b * ncx + c, b * nl + (c - ncx))

    def fwd(b, k):
        return (block(b, k), 0)

    def bwd(b, k):
        c = jnp.where(k < ncx, ncx - 1 - k, ncx + (nl - 1 - (k - ncx)))
        return (block(b, c), 0)

    return fwd, bwd, ncx + nl


def _ssd_body(xf_ref, dtf_ref, xb_ref, dtb_ref, dtbias_ref, alog_ref, yf_ref, yb_ref, state_ref):
    ri = lax.broadcasted_iota(jnp.int32, (CHUNK, CHUNK), 0)
    ci = lax.broadcasted_iota(jnp.int32, (CHUNK, CHUNK), 1)
    lane_lo = ci < HEAD_DIM
    aneg = -jnp.exp(alog_ref[...])
    dtbias = dtbias_ref[...]

    order = [(d, s if d == 0 else SCAN_CHUNKS - 1 - s) for s in range(SCAN_CHUNKS) for d in range(2)]
    for d, sub in order:
        x_ref, dt_ref, y_ref = ((xf_ref, dtf_ref, yf_ref), (xb_ref, dtb_ref, yb_ref))[d]
        rws = slice(sub * CHUNK, (sub + 1) * CHUNK)
        causal = (ri >= ci) if d == 0 else (ci >= ri)
        tmat = jnp.where(causal, 1.0, 0.0).astype(BF16)
        xs = x_ref[rws, 0:256]
        bm = x_ref[rws, 256:384]
        cm = x_ref[rws, 384:512]
        dtp = _softplus(dt_ref[rws, :] + dtbias)
        la = dtp * aneg
        acum = _dot3_left(tmat, la)
        acum_t = _dot3(la.T, jnp.where(causal, 0.0, 1.0).astype(BF16) + jnp.where(ri == ci, 1.0, 0.0).astype(BF16))
        bt = bm.T.astype(BF16)
        cmb = cm.astype(BF16)
        bmb = bm.astype(BF16)
        tot_row = CHUNK - 1 if d == 0 else 0
        for p in range(2):
            cmask = jnp.where(lane_lo if p == 0 else jnp.logical_not(lane_lo), cmb, jnp.zeros_like(cmb))
            cb = _dot_nt(cmask, bmb)
            cols, dts, ys = [], [], []
            x_pair = xs[:, p * LANES:(p + 1) * LANES]
            for j in range(2):
                col = 4 * d + 2 * p + j
                colb = jnp.broadcast_to(acum[:, col:col + 1], (CHUNK, CHUNK))
                rowb = jnp.broadcast_to(acum_t[col:col + 1, :], (CHUNK, CHUNK))
                cols.append(colb)
                dts.append(jnp.broadcast_to(dtp[:, col:col + 1], (CHUNK, CHUNK)))
            col_pair = jnp.where(lane_lo, cols[0], cols[1])
            dt_pair = jnp.where(lane_lo, dts[0], dts[1])
            xdt = x_pair * dt_pair
            xdt_b = xdt.astype(BF16)
            for j in range(2):
                col = 4 * d + 2 * p + j
                rowb = jnp.broadcast_to(acum_t[col:col + 1, :], (CHUNK, CHUNK))
                decay = jnp.exp(jnp.where(causal, cols[j] - rowb, NEG_INF))
                ys.append(_dot((cb * decay).astype(BF16), xdt_b))
            y_intra = jnp.where(lane_lo, ys[0], ys[1])
            s_old = state_ref[d, p]
            y_inter = _dot(cmask, s_old.astype(BF16)) * jnp.exp(col_pair)
            y_ref[rws, p * LANES:(p + 1) * LANES] = y_intra + y_inter
            tot_pair = col_pair[tot_row:tot_row + 1, :]
            to_end = jnp.exp(tot_pair - col_pair)
            state_ref[d, p] = s_old * jnp.exp(tot_pair) + _dot(bt, (xdt * to_end).astype(BF16))


def _linear_scan(a, b, reverse):
    n = a.shape[0]
    row = lax.broadcasted_iota(jnp.int32, (n, 1), 0)
    s = 1
    while s < n:
        if s < SUBLANES:
            if reverse:
                ok = row < n - s
                a_sh = jnp.where(ok, pltpu.roll(a, n - s, axis=0), 1.0)
                b_sh = jnp.where(ok, pltpu.roll(b, n - s, axis=0), 0.0)
            else:
                ok = row >= s
                a_sh = jnp.where(ok, pltpu.roll(a, s, axis=0), 1.0)
                b_sh = jnp.where(ok, pltpu.roll(b, s, axis=0), 0.0)
            b = b + a * b_sh
            a = a * a_sh
        elif reverse:
            b = jnp.concatenate([b[:n - s] + a[:n - s] * b[s:], b[n - s:]], axis=0)
            a = jnp.concatenate([a[:n - s] * a[s:], a[n - s:]], axis=0)
        else:
            b = jnp.concatenate([b[:s], b[s:] + a[s:] * b[:n - s]], axis=0)
            a = jnp.concatenate([a[:s], a[s:] * a[:n - s]], axis=0)
        s *= 2
    return a, b


def _lru_body(uf_ref, ub_ref, wg_ref, bg_ref, lam_ref, hf_ref, hb_ref, carry_ref):
    for d, (u_ref, h_ref) in enumerate(((uf_ref, hf_ref), (ub_ref, hb_ref))):
        u = u_ref[...]
        gates = _dot(u.astype(BF16), wg_ref[d]) + bg_ref[d]
        r = jax.nn.sigmoid(gates[:, :LRU_WIDTH])
        ig = jax.nn.sigmoid(gates[:, LRU_WIDTH:])
        log_a = -LRU_C * r * _softplus(-lam_ref[d])
        a = jnp.exp(log_a)
        inp = jnp.sqrt(-jnp.tanh(log_a) * (1.0 + a * a)) * (ig * u)
        scans = [_linear_scan(a[s * CHUNK:(s + 1) * CHUNK], inp[s * CHUNK:(s + 1) * CHUNK], reverse=(d == 1))
                 for s in range(SCAN_CHUNKS)]
        carry = carry_ref[d, 0:1, :]
        last = 0 if d == 1 else CHUNK - 1
        for s in (range(SCAN_CHUNKS) if d == 0 else reversed(range(SCAN_CHUNKS))):
            a_cum, b_cum = scans[s]
            h = b_cum + a_cum * carry
            h_ref[s * CHUNK:(s + 1) * CHUNK, :] = h
            carry = h[last:last + 1, :]
        carry_ref[d, 0:1, :] = carry


def _scans_kernel(xf_ref, dtf_ref, xb_ref, dtb_ref, dtbias_ref, alog_ref, uf_ref, ub_ref, wg_ref, bg_ref, lam_ref,
                  yf_ref, yb_ref, hf_ref, hb_ref, state_ref, carry_ref):
    @pl.when(pl.program_id(1) == 0)
    def _():
        state_ref[...] = jnp.zeros_like(state_ref)
        carry_ref[...] = jnp.zeros_like(carry_ref)

    _ssd_body(xf_ref, dtf_ref, xb_ref, dtb_ref, dtbias_ref, alog_ref, yf_ref, yb_ref, state_ref)
    _lru_body(uf_ref, ub_ref, wg_ref, bg_ref, lam_ref, hf_ref, hb_ref, carry_ref)


def _scans(xbc, dt, dtbias_row, alog_row, u, wg, bg, lam, *, batch, seq, ctx_len):
    n = xbc.shape[0]
    fwd, bwd, steps = _chunk_maps(batch, seq, ctx_len)
    rows = SCAN_CHUNKS * CHUNK
    const = lambda a: pl.BlockSpec(a.shape, lambda b, k: (0,) * a.ndim)
    blk = lambda w, order: pl.BlockSpec((rows, w), order)
    wide = jax.ShapeDtypeStruct((n, 256), F32)
    return pl.pallas_call(
        _scans_kernel,
        grid=(batch, steps),
        in_specs=[blk(512, fwd), blk(LANES, fwd), blk(512, bwd), blk(LANES, bwd), const(dtbias_row), const(alog_row),
                  blk(LRU_WIDTH, fwd), blk(LRU_WIDTH, bwd), const(wg), const(bg), const(lam)],
        out_specs=(blk(256, fwd), blk(256, bwd), blk(LRU_WIDTH, fwd), blk(LRU_WIDTH, bwd)),
        out_shape=(wide, wide, wide, wide),
        scratch_shapes=[pltpu.VMEM((2, 2, CHUNK, LANES), F32), pltpu.VMEM((2, SUBLANES, LRU_WIDTH), F32)],
        compiler_params=_params(("parallel", "arbitrary")),
        name="scans",
    )(xbc, dt, xbc, dt, dtbias_row, alog_row, u, u, wg, bg, lam)


def _stack_heads(q, g):
    qf = q.astype(F32)
    lo = g * LANES
    return jnp.concatenate([qf[:, lo:lo + HEAD_DIM], qf[:, lo + HEAD_DIM:lo + LANES]], axis=0).astype(BF16)


def _value_lanes(g):
    lane = lax.broadcasted_iota(jnp.int32, (1, LANES), 1)
    return (lane < HEAD_DIM) if g == 0 else (lane >= HEAD_DIM)


def _aug_values(v, g):
    return jnp.where(_value_lanes(g), v, jnp.ones_like(v))


def _flash_init(rows, g, sink_pair):
    if sink_pair is None:
        return jnp.full((rows, 1), NEG_INF, F32), jnp.zeros((rows, LANES), F32)
    half = lax.broadcasted_iota(jnp.int32, (rows, 1), 0) < rows // 2
    m = jnp.where(half, sink_pair[0], sink_pair[1]).astype(F32)
    acc = jnp.broadcast_to(jnp.where(_value_lanes(g), 0.0, 1.0), (rows, LANES))
    return m, acc


def _flash_update(state, q2, kt, v_aug, mask=None):
    m, acc = state
    s = _dot(q2, kt)
    if mask is not None:
        s = jnp.where(mask, s, NEG_INF)
    m_new = jnp.maximum(m, jnp.max(s, axis=-1, keepdims=True))
    p = jnp.exp(s - m_new).astype(BF16)
    acc = jnp.exp(m - m_new) * acc + _dot(p, v_aug)
    return m_new, acc


def _flash_finish(states, tq):
    pieces = []
    for g, (_, acc) in enumerate(states):
        den = (1 - g) * HEAD_DIM
        o = acc[:, g * HEAD_DIM:(g + 1) * HEAD_DIM] / acc[:, den:den + 1]
        pieces += [o[:tq], o[tq:]]
    return jnp.concatenate(pieces, axis=1)


def _group_rows(g):
    return slice(g * HEAD_DIM, (g + 1) * HEAD_DIM)


def _dense_attn_kernel(*refs, tq, seg_lens, has_sink):
    refs = list(refs)
    sink_ref = refs.pop(0) if has_sink else None
    q_ref = refs.pop(0)
    o_ref = refs.pop()
    segs = [(refs[2 * i], refs[2 * i + 1], n) for i, n in enumerate(seg_lens)]
    q = q_ref[...]
    q2 = [_stack_heads(q, g) for g in range(2)]
    states = tuple(_flash_init(2 * tq, g, (sink_ref[2 * g], sink_ref[2 * g + 1]) if has_sink else None)
                   for g in range(2))
    for kt_ref, v_ref, n_keys in segs:
        if n_keys <= KV_CHUNK:
            v = v_ref[...]
            states = tuple(_flash_update(states[g], q2[g], kt_ref[_group_rows(g), :], _aug_values(v, g))
                           for g in range(2))
        else:
            def body(c, sts, kt_ref=kt_ref, v_ref=v_ref):
                off = pl.multiple_of(c * KV_CHUNK, KV_CHUNK)
                v = v_ref[pl.ds(off, KV_CHUNK), :]
                return tuple(_flash_update(sts[g], q2[g], kt_ref[_group_rows(g), pl.ds(off, KV_CHUNK)],
                                           _aug_values(v, g)) for g in range(2))
            states = lax.fori_loop(0, n_keys // KV_CHUNK, body, states, unroll=KV_UNROLL)
    o_ref[...] = _flash_finish(states, tq).astype(o_ref.dtype)


def _dense_attn(q, kt, v, sink, *, q_row0, q_len, tq, segs, batch):
    n = q.shape[0]
    qpb = q_len // tq
    q0 = q_row0 // tq
    in_specs, args = [], []
    if sink is not None:
        in_specs.append(pl.BlockSpec(memory_space=pltpu.SMEM))
        args.append(sink)
    in_specs.append(pl.BlockSpec((tq, 256), lambda b, i: (q0 + b * qpb + i, 0)))
    args.append(q)
    for row0, klen in segs:
        k0 = row0 // klen
        in_specs.append(pl.BlockSpec((LANES, klen), lambda b, i, k0=k0: (0, k0 + b)))
        in_specs.append(pl.BlockSpec((klen, LANES), lambda b, i, k0=k0: (k0 + b, 0)))
        args += [kt, v]
    return pl.pallas_call(
        functools.partial(_dense_attn_kernel, tq=tq, seg_lens=tuple(s[1] for s in segs), has_sink=sink is not None),
        grid=(batch, qpb),
        in_specs=in_specs,
        out_specs=pl.BlockSpec((tq, 256), lambda b, i: (b * qpb + i, 0)),
        out_shape=jax.ShapeDtypeStruct((batch * q_len, 256), BF16),
        compiler_params=_params(("parallel", "parallel")),
        name="dense_attn",
    )(*args)


def _window_attn_kernel(sink_ref, q_ref, ktc_ref, vc_ref, ktp_ref, vp_ref, ktm_ref, vm_ref, ktn_ref, vn_ref, o_ref,
                        *, n_tiles):
    n = pl.program_id(1)
    nsub = TQ_WINDOW // CHUNK
    iq = lax.broadcasted_iota(jnp.int32, (2 * CHUNK, CHUNK), 0) & (CHUNK - 1)
    jk = lax.broadcasted_iota(jnp.int32, (2 * CHUNK, CHUNK), 1)
    below = jk >= iq
    above = jk <= iq
    vctx = vc_ref[...]
    for j in range(nsub):
        cols = slice(j * CHUNK, (j + 1) * CHUNK)
        q = q_ref[cols, :]
        states = []
        for g in range(2):
            q2 = _stack_heads(q, g)
            rows = _group_rows(g)
            state = _flash_init(2 * CHUNK, g, (sink_ref[2 * g], sink_ref[2 * g + 1]))
            state = _flash_update(state, q2, ktc_ref[rows, :], _aug_values(vctx, g))
            state = _flash_update(state, q2, ktm_ref[rows, cols], _aug_values(vm_ref[cols, :], g))
            if j > 0:
                prev = slice((j - 1) * CHUNK, j * CHUNK)
                state = _flash_update(state, q2, ktm_ref[rows, prev], _aug_values(vm_ref[prev, :], g), below)
            else:
                state = _flash_update(state, q2, ktp_ref[rows, :], _aug_values(vp_ref[...], g),
                                      jnp.logical_and(below, n > 0))
            if j < nsub - 1:
                nxt = slice((j + 1) * CHUNK, (j + 2) * CHUNK)
                state = _flash_update(state, q2, ktm_ref[rows, nxt], _aug_values(vm_ref[nxt, :], g), above)
            else:
                state = _flash_update(state, q2, ktn_ref[rows, :], _aug_values(vn_ref[...], g),
                                      jnp.logical_and(above, n < n_tiles - 1))
            states.append(state)
        o_ref[cols, :] = _flash_finish(states, CHUNK).astype(o_ref.dtype)


def _window_attn(q, kt, v, sink, *, batch, seq, ctx_len):
    nt = seq // TQ_WINDOW
    nsub = TQ_WINDOW // CHUNK
    nb = seq // CHUNK
    ctx0 = (batch * seq) // ctx_len
    prev = lambda b, n: b * nb + jnp.maximum(n * nsub - 1, 0)
    nxt = lambda b, n: b * nb + jnp.minimum((n + 1) * nsub, nb - 1)
    return pl.pallas_call(
        functools.partial(_window_attn_kernel, n_tiles=nt),
        grid=(batch, nt),
        in_specs=[pl.BlockSpec(memory_space=pltpu.SMEM),
                  pl.BlockSpec((TQ_WINDOW, 256), lambda b, n: (b * nt + n, 0)),
                  pl.BlockSpec((LANES, ctx_len), lambda b, n: (0, ctx0 + b)),
                  pl.BlockSpec((ctx_len, LANES), lambda b, n: (ctx0 + b, 0)),
                  pl.BlockSpec((LANES, CHUNK), lambda b, n: (0, prev(b, n))),
                  pl.BlockSpec((CHUNK, LANES), lambda b, n: (prev(b, n), 0)),
                  pl.BlockSpec((LANES, TQ_WINDOW), lambda b, n: (0, b * nt + n)),
                  pl.BlockSpec((TQ_WINDOW, LANES), lambda b, n: (b * nt + n, 0)),
                  pl.BlockSpec((LANES, CHUNK), lambda b, n: (0, nxt(b, n))),
                  pl.BlockSpec((CHUNK, LANES), lambda b, n: (nxt(b, n), 0))],
        out_specs=pl.BlockSpec((TQ_WINDOW, 256), lambda b, n: (b * nt + n, 0)),
        out_shape=jax.ShapeDtypeStruct((batch * seq, 256), BF16),
        compiler_params=_params(("parallel", "parallel")),
        name="window_attn",
    )(sink, q, kt, v, kt, v, kt, v, kt, v)


def _gelu_tanh(x):
    return 0.5 * x * (1.0 + jnp.tanh(math.sqrt(2.0 / math.pi) * (x + 0.044715 * (x * x * x))))


def _outproj_kernel(x_ref, xc_ref, gate_ref, gpost_ref, oa_ref, od_ref, oac_ref, odc_ref, yf_ref, yb_ref, xs_ref,
                    z_ref, dsk_ref, gn_ref, hf_ref, hb_ref, lg_ref, w_ref,
                    fshift_ref, fscale_ref, fpre_ref, rwt_ref, rb_ref,
                    o_ref, hffn_ref, ld_ref, wk_ref, tab_ref, *, n_lat, split):
    is_lat = pl.program_id(0) * TM < n_lat
    x = jnp.where(jnp.logical_or(is_lat, not split), x_ref[...], xc_ref[...])
    oa = jnp.where(is_lat, oa_ref[...], oac_ref[...])
    od = jnp.where(is_lat, od_ref[...], odc_ref[...])
    y_ssd = (yf_ref[...] + yb_ref[...] + xs_ref[...] * dsk_ref[...]) * _silu(z_ref[...])
    ob = _rms(y_ssd, gn_ref[...])
    oc = (hf_ref[...] + hb_ref[...]) * _gelu_tanh(lg_ref[...])
    y = (_dot(oa, w_ref[0:256, :]) + _dot(ob.astype(BF16), w_ref[256:512, :])
         + _dot(oc.astype(BF16), w_ref[512:768, :]) + _dot(od, w_ref[768:1024, :]))
    x_mid = x + gate_ref[0] * _rms(y, gpost_ref[...])
    o_ref[...] = x_mid
    _route_tile(x_mid, fshift_ref, fscale_ref, fpre_ref, rwt_ref, rb_ref, hffn_ref, ld_ref, wk_ref, tab_ref)


def _outproj(x_lat, x_ctx, mod, gpost, oa, od, oa_c, od_c, yf, yb, xbc, z, dsk, gn, hf, hb, lg, w_out,
             ffn_pre, rwt, rb, *, n_rows, n_lat, seq, batch):
    d = x_lat.shape[1]
    lat_tiles = n_lat // TM
    x_specs, x_args = _row_sources(x_lat, x_ctx, n_lat)
    row = lambda w: pl.BlockSpec((TM, w), lambda i: (i, 0))
    col = pl.BlockSpec((TOP_K, TM), lambda i: (0, i))
    lat = pl.BlockSpec((TM, 256), lambda i: (jnp.minimum(i, lat_tiles - 1), 0))
    ctx = pl.BlockSpec((TM, 256), lambda i: (jnp.maximum(i - lat_tiles, 0), 0))
    const = lambda a: pl.BlockSpec(a.shape, lambda i: (0,) * a.ndim)
    return pl.pallas_call(
        functools.partial(_outproj_kernel, n_lat=n_lat, split=x_ctx is not None),
        grid=(n_rows // TM,),
        in_specs=[x_specs[0], x_specs[3], pl.BlockSpec((1, 1, d), _mod_spec(2, n_lat, seq, batch, TM)), const(gpost),
                  lat, lat, ctx, ctx, row(256), row(256), row(256), row(256), const(dsk), const(gn),
                  row(256), row(256), row(256), const(w_out),
                  pl.BlockSpec((1, 1, d), _mod_spec(3, n_lat, seq, batch, TM)),
                  pl.BlockSpec((1, 1, d), _mod_spec(4, n_lat, seq, batch, TM)),
                  const(ffn_pre), const(rwt), const(rb)],
        out_specs=(row(d), row(d), col, col, pl.BlockSpec((SUBLANES, 2 * N_EXPERTS), lambda i: (i, 0))),
        out_shape=(jax.ShapeDtypeStruct((n_rows, d), F32),
                   jax.ShapeDtypeStruct((n_rows, d), BF16),
                   jax.ShapeDtypeStruct((TOP_K, n_rows), jnp.int32),
                   jax.ShapeDtypeStruct((TOP_K, n_rows), F32),
                   jax.ShapeDtypeStruct((n_rows // TM * SUBLANES, 2 * N_EXPERTS), jnp.int32)),
        compiler_params=_params(("parallel",)),
        name="outproj_route",
    )(x_args[0], x_args[3], mod, gpost, oa, od, oa_c, od_c, yf, yb, xbc, z, dsk, gn, hf, hb, lg, w_out,
      mod, mod, ffn_pre, rwt, rb)


def _ceil_seg(c):
    return jnp.floor((c + (SEG_ALIGN - 1)) * (1.0 / SEG_ALIGN)) * SEG_ALIGN


def _route_tile(x, shift_ref, scale_ref, gpre_ref, rwt_ref, rb_ref, hb_ref, ld_ref, wk_ref, tab_ref):
    h = _rms(x, gpre_ref[...])
    h = h * (1.0 + scale_ref[0]) + shift_ref[0]
    hb = h.astype(BF16)
    hb_ref[...] = hb

    scores = jax.nn.sigmoid(_dot_nt(rwt_ref[...], hb))
    biased = scores + rb_ref[...]
    gsz = N_EXPERTS // N_EXPERT_GROUPS
    sub = lax.broadcasted_iota(jnp.int32, (gsz, TM), 0)
    blocks, gscore = [], []
    for g in range(N_EXPERT_GROUPS):
        blk = biased[g * gsz:(g + 1) * gsz, :]
        m1 = jnp.max(blk, axis=0, keepdims=True)
        first = jnp.min(jnp.where(blk == m1, sub, gsz), axis=0, keepdims=True)
        m2 = jnp.max(jnp.where(sub == first, -jnp.inf, blk), axis=0, keepdims=True)
        blocks.append(blk)
        gscore.append(m1 + m2)
    masked = []
    for g in range(N_EXPERT_GROUPS):
        rank = jnp.zeros((1, TM), F32)
        for g2 in range(N_EXPERT_GROUPS):
            if g2 == g:
                continue
            beats = (gscore[g2] > gscore[g]) | ((gscore[g2] == gscore[g]) if g2 < g else False)
            rank = rank + jnp.where(beats, 1.0, 0.0)
        masked.append(jnp.where(rank < TOPK_GROUPS, blocks[g], -jnp.inf))
    vals = jnp.concatenate(masked, axis=0)
    eidx = lax.broadcasted_iota(jnp.int32, (N_EXPERTS, TM), 0)
    self32 = jnp.zeros((N_EXPERTS, TM), F32)
    rest = vals
    for _ in range(TOP_K):
        top = jnp.max(rest, axis=0, keepdims=True)
        first = jnp.min(jnp.where(rest == top, eidx, N_EXPERTS), axis=0, keepdims=True)
        hit = eidx == first
        self32 = jnp.where(hit, 1.0, self32)
        rest = jnp.where(hit, -jnp.inf, rest)
    sel = self32 > 0.5
    picked = jnp.where(sel, scores, 0.0)
    wdense = picked / jnp.sum(picked, axis=0, keepdims=True) * ROUTED_SCALE

    tr = lax.broadcasted_iota(jnp.int32, (TM, TM), 0)
    tc = lax.broadcasted_iota(jnp.int32, (TM, TM), 1)
    before = jnp.where(tr < tc, 1.0, 0.0).astype(BF16)
    selb = self32.astype(BF16)
    pos = _dot(selb, before)
    er = lax.broadcasted_iota(jnp.int32, (N_EXPERTS, N_EXPERTS), 0)
    ec = lax.broadcasted_iota(jnp.int32, (N_EXPERTS, N_EXPERTS), 1)
    lower = jnp.where(ec < er, 1.0, 0.0).astype(BF16)
    upper = jnp.where(er < ec, 1.0, 0.0).astype(BF16)
    ksel = _dot(lower, selb)
    cnt_col = _ceil_seg(jnp.sum(self32, axis=1, keepdims=True))
    loc_col = _dot3_left(lower, jnp.broadcast_to(cnt_col, (N_EXPERTS, LANES)))[:, 0:1]
    cnt_row = _ceil_seg(_dot_nt(jnp.ones((SUBLANES, TM), BF16), selb))
    loc_row = _dot3(cnt_row, upper)
    tab_ref[...] = jnp.concatenate([cnt_row, loc_row], axis=1).astype(jnp.int32)

    r8 = lax.broadcasted_iota(jnp.int32, (TOP_K, TM), 0)
    ld = jnp.zeros((TOP_K, TM), F32)
    wk = jnp.zeros((TOP_K, TM), F32)
    stage_row = pos + loc_col
    for k in range(TOP_K):
        one = sel & (ksel == float(k))
        ld = jnp.where(r8 == k, jnp.sum(jnp.where(one, stage_row, 0.0), axis=0, keepdims=True), ld)
        wk = jnp.where(r8 == k, jnp.sum(jnp.where(one, wdense, 0.0), axis=0, keepdims=True), wk)
    ld_ref[...] = ld.astype(jnp.int32)
    wk_ref[...] = wk


def _pow2_pieces(limit):
    bits, b = [], limit
    while b >= SEG_ALIGN:
        bits.append(b)
        b //= 2
    return bits


def _copy_pieces(n, src_ref, src0, dst_ref, dst0, sem, limit, wait, same_src=False):
    for bit in _pow2_pieces(limit):
        @pl.when((n & bit) != 0)
        def _():
            off = n & ~(2 * bit - 1)
            cp = pltpu.make_async_copy(src_ref.at[pl.ds(pl.multiple_of(src0 + (0 if same_src else off), SEG_ALIGN),
                                                          bit)],
                                       dst_ref.at[pl.ds(pl.multiple_of(dst0 + off, SEG_ALIGN), bit)], sem)
            cp.wait() if wait else cp.start()


N_PIECE_TABS = 7


def _piece_copies(tile, tabs, stage_ref, slots_ref, sem, to_slots, wait):
    nbig_ref, nsmall_ref, _, bsrc_ref, bdst_ref, ssrc_ref, sdst_ref = tabs
    for rows, n_ref, a_ref, b_ref, cap in ((BIG_PIECE, nbig_ref, bsrc_ref, bdst_ref, BIG_MAX),
                                           (SEG_ALIGN, nsmall_ref, ssrc_ref, sdst_ref, SMALL_MAX)):
        def body(p, c, rows=rows, a_ref=a_ref, b_ref=b_ref, cap=cap):
            src = stage_ref.at[pl.ds(pl.multiple_of(a_ref[tile * cap + p], SEG_ALIGN), rows)]
            dst = slots_ref.at[pl.ds(pl.multiple_of(b_ref[tile * cap + p], SEG_ALIGN), rows)]
            cp = pltpu.make_async_copy(src, dst, sem) if to_slots else pltpu.make_async_copy(dst, src, sem)
            cp.wait() if wait else cp.start()
            return c
        lax.fori_loop(0, n_ref[tile], body, 0)


def _used_blocks(tile, tabs):
    return (tabs[2][tile] + TM - 1) // TM


def _for_used_blocks(used, body):
    always = TOP_K + 1
    for b in range(always):
        body(b)
    for b in range(always, STAGE_ROWS // TM):
        @pl.when(b < used)
        def _():
            body(b)


def _stage_rows_iota():
    return lax.broadcasted_iota(jnp.int32, (TM // 2, TM), 0).astype(F32).astype(BF16)


def _pick_matrix(ld, base, vals, jrow):
    rel = (ld - base).astype(F32)
    rel = jnp.where(jnp.logical_and(rel >= 0.0, rel < TM // 2), rel, -1.0).astype(BF16)
    out = jnp.zeros((TM // 2, TM), BF16)
    for k in range(TOP_K):
        out = jnp.where(rel[k:k + 1, :] == jrow, vals[k:k + 1, :], out)
    return out


def _dispatch_kernel(*refs):
    tabs = refs[:N_PIECE_TABS]
    pstart_ref, npad_ref, hb_ref, ld_ref, xs_ref, stage, zbuf, sem, zsem = refs[N_PIECE_TABS:]
    i = pl.program_id(0)

    @pl.when(i == 0)
    def _():
        zbuf[...] = jnp.zeros_like(zbuf)
        for wait in (False, True):
            def body(e, c, wait=wait):
                _copy_pieces(npad_ref[e], zbuf, 0, xs_ref, pstart_ref[e], zsem, BM_EXPERT // 2, wait, same_src=True)
                return c
            lax.fori_loop(0, N_EXPERTS, body, 0)

    ld = ld_ref[...]
    hb = hb_ref[...]
    jrow = _stage_rows_iota()
    ones = jnp.ones((TOP_K, TM), BF16)

    cur = stage.at[i & 1]

    def block(b):
        for half in range(2):
            base = b * TM + half * (TM // 2)
            cur[base:base + TM // 2, :] = _dot(_pick_matrix(ld, base, ones, jrow), hb).astype(BF16)

    _for_used_blocks(_used_blocks(i, tabs), block)

    @pl.when(i > 0)
    def _():
        _piece_copies(i - 1, tabs, stage.at[(i - 1) & 1], xs_ref, sem, True, True)

    _piece_copies(i, tabs, cur, xs_ref, sem, True, False)

    @pl.when(i == pl.num_programs(0) - 1)
    def _():
        _piece_copies(i, tabs, cur, xs_ref, sem, True, True)


def _dispatch(tabs, pad_start, n_pad, hb, ld, n_slots):
    n, d = hb.shape
    grid_spec = pltpu.PrefetchScalarGridSpec(
        num_scalar_prefetch=N_PIECE_TABS + 2,
        grid=(n // TM,),
        in_specs=[pl.BlockSpec((TM, d), lambda i, *_: (i, 0)),
                  pl.BlockSpec((TOP_K, TM), lambda i, *_: (0, i))],
        out_specs=pl.BlockSpec(memory_space=pl.ANY),
        scratch_shapes=[pltpu.VMEM((2, STAGE_ROWS, d), BF16), pltpu.VMEM((BM_EXPERT // 2, d), BF16),
                        pltpu.SemaphoreType.DMA(()), pltpu.SemaphoreType.DMA(())],
    )
    return pl.pallas_call(
        _dispatch_kernel,
        grid_spec=grid_spec,
        out_shape=jax.ShapeDtypeStruct((n_slots, d), BF16),
        compiler_params=_params(("arbitrary",)),
        name="moe_dispatch",
    )(*tabs, pad_start, n_pad, hb, ld)


def _expert_kernel(be_ref, na_ref, nxt_ref, slot_ref, xs_hbm, wg_hbm, wu_hbm, wd_hbm, ys_ref,
                   xbuf, wg_raw, wu_raw, wd_raw, wgub, wdb, xsem, wsem, *, layer):
    i = pl.program_id(0)
    n_act = na_ref[0]

    def weight_copies(e, slot):
        return [pltpu.make_async_copy(src.at[layer, e], dst.at[slot], wsem.at[slot, j])
                for j, (src, dst) in enumerate(((wg_hbm, wg_raw), (wu_hbm, wu_raw), (wd_hbm, wd_raw)))]

    def rows_copy(blk):
        buf = lax.rem(blk, X_BUFFERS)
        return pltpu.make_async_copy(xs_hbm.at[pl.ds(pl.multiple_of(blk * BM_EXPERT, BM_EXPERT), BM_EXPERT)],
                                     xbuf.at[buf], xsem.at[buf])

    @pl.when(i < n_act)
    def _():
        e, slot = be_ref[i], slot_ref[i]

        @pl.when(i == 0)
        def _():
            for cp in weight_copies(e, slot):
                cp.start()
            for j in range(X_BUFFERS - 1):
                @pl.when(j < n_act)
                def _():
                    rows_copy(j).start()

        @pl.when(i + (X_BUFFERS - 1) < n_act)
        def _():
            rows_copy(i + (X_BUFFERS - 1)).start()

        @pl.when(jnp.logical_or(i == 0, e != be_ref[jnp.maximum(i - 1, 0)]))
        def _():
            for cp in weight_copies(e, slot):
                cp.wait()
            wgub[:, :EXPERT_HIDDEN] = wg_raw[slot].astype(BF16)
            wgub[:, EXPERT_HIDDEN:] = wu_raw[slot].astype(BF16)
            wdb[...] = wd_raw[slot].astype(BF16)

            @pl.when(nxt_ref[i] != e)
            def _():
                for cp in weight_copies(nxt_ref[i], 1 - slot):
                    cp.start()

        rows_copy(i).wait()
        gu = _dot(xbuf[lax.rem(i, X_BUFFERS)], wgub[...])
        hid = _silu(gu[:, :EXPERT_HIDDEN]) * gu[:, EXPERT_HIDDEN:]
        ys_ref[...] = _dot(hid.astype(BF16), wdb[...]).astype(ys_ref.dtype)


def _experts(block_e, n_active, next_e, w_slot, xs, wg, wu, wd, layer):
    n_slots, d = xs.shape
    nb = n_slots // BM_EXPERT
    tiles = pl.BlockSpec((BM_EXPERT, d), lambda i, be, na, *_: (jnp.minimum(i, na[0] - 1), 0))
    anywhere = pl.BlockSpec(memory_space=pl.ANY)
    grid_spec = pltpu.PrefetchScalarGridSpec(
        num_scalar_prefetch=4,
        grid=(nb,),
        in_specs=[anywhere, anywhere, anywhere, anywhere],
        out_specs=tiles,
        scratch_shapes=[pltpu.VMEM((X_BUFFERS, BM_EXPERT, d), BF16),
                        pltpu.VMEM((2, d, EXPERT_HIDDEN), F32), pltpu.VMEM((2, d, EXPERT_HIDDEN), F32),
                        pltpu.VMEM((2, EXPERT_HIDDEN, d), F32),
                        pltpu.VMEM((d, 2 * EXPERT_HIDDEN), BF16),
                        pltpu.VMEM((EXPERT_HIDDEN, d), BF16),
                        pltpu.SemaphoreType.DMA((X_BUFFERS,)), pltpu.SemaphoreType.DMA((2, 3))],
    )
    return pl.pallas_call(
        functools.partial(_expert_kernel, layer=layer),
        grid_spec=grid_spec,
        out_shape=jax.ShapeDtypeStruct((n_slots, d), BF16),
        compiler_params=_params(("arbitrary",)),
        name="moe_experts",
    )(block_e, n_active, next_e, w_slot, xs, wg, wu, wd)


def _combine_kernel(*refs):
    tabs = refs[:N_PIECE_TABS]
    (ys_ref, ld_ref, wk_ref, hb_ref, x_ref, gate_ref, gpost_ref, sg_ref, su_ref, sd_ref, o_ref,
     stage, acc_ref, sem) = refs[N_PIECE_TABS:]
    i = pl.program_id(0)

    @pl.when(i == 0)
    def _():
        stage[...] = jnp.zeros_like(stage)
        _piece_copies(0, tabs, stage.at[0], ys_ref, sem, False, False)

    cur = stage.at[i & 1]
    _piece_copies(i, tabs, cur, ys_ref, sem, False, True)

    @pl.when(i + 1 < pl.num_programs(0))
    def _():
        _piece_copies(i + 1, tabs, stage.at[(i + 1) & 1], ys_ref, sem, False, False)

    hb = hb_ref[...]
    acc_ref[...] = _dot((_silu(_dot(hb, sg_ref[...])) * _dot(hb, su_ref[...])).astype(BF16), sd_ref[...])

    ld = ld_ref[...]
    wkb = wk_ref[...].astype(BF16)
    jrow = _stage_rows_iota()

    def block(b):
        for half in range(2):
            base = b * TM + half * (TM // 2)
            weights = _pick_matrix(ld, base, wkb, jrow)
            acc_ref[...] += lax.dot_general(weights, cur[base:base + TM // 2, :], (((0,), (0,)), ((), ())),
                                            preferred_element_type=F32)

    _for_used_blocks(_used_blocks(i, tabs), block)
    o_ref[...] = x_ref[...] + gate_ref[0] * _rms(acc_ref[...], gpost_ref[...])


def _combine(tabs, ys, ld, wk, hb, xu, mod, gpost, sg, su, sd, *, n_rows, n_lat, seq, batch):
    d = xu.shape[1]
    row = lambda w: pl.BlockSpec((TM, w), lambda i, *_: (i, 0))
    col = pl.BlockSpec((TOP_K, TM), lambda i, *_: (0, i))
    const = lambda a: pl.BlockSpec(a.shape, lambda i, *_: (0,) * a.ndim)
    mod_map = _mod_spec(5, n_lat, seq, batch, TM)
    grid_spec = pltpu.PrefetchScalarGridSpec(
        num_scalar_prefetch=N_PIECE_TABS,
        grid=(n_rows // TM,),
        in_specs=[pl.BlockSpec(memory_space=pl.ANY), col, col, row(d), row(d),
                  pl.BlockSpec((1, 1, d), lambda i, *_: mod_map(i)),
                  const(gpost), const(sg), const(su), const(sd)],
        out_specs=row(d),
        scratch_shapes=[pltpu.VMEM((2, STAGE_ROWS, d), BF16), pltpu.VMEM((TM, d), F32), pltpu.SemaphoreType.DMA(())],
    )
    return pl.pallas_call(
        _combine_kernel,
        grid_spec=grid_spec,
        out_shape=jax.ShapeDtypeStruct((n_rows, d), F32),
        compiler_params=_params(("arbitrary",)),
        name="moe_combine",
    )(*tabs, ys, ld, wk, hb, xu, mod, gpost, sg, su, sd)


def _deinterleave(w):
    cols = w.shape[-1]
    perm = jnp.concatenate([jnp.arange(0, HEAD_DIM, 2), jnp.arange(1, HEAD_DIM, 2)])
    idx = (jnp.arange(cols // HEAD_DIM)[:, None] * HEAD_DIM + perm[None, :]).reshape(-1)
    return w[..., idx]


def _pad_in_proj(w_in):
    d = w_in.shape[0]
    o = 0
    parts = {}
    for name, width in (("qa", 256), ("ka", 128), ("va", 128), ("z", 256), ("xs", 256), ("bm", 128), ("cm", 128),
                        ("dtf", 4), ("dtb", 4), ("lx", 256), ("lg", 256), ("qd", 256), ("kd", 128), ("vd", 128)):
        parts[name] = w_in[:, o:o + width]
        o += width
    dt = jnp.concatenate([parts["dtf"], parts["dtb"], jnp.zeros((d, LANES - 8), w_in.dtype)], axis=1)
    cols = [_deinterleave(parts["qa"]), _deinterleave(parts["ka"]), parts["va"],
            _deinterleave(parts["qd"]), _deinterleave(parts["kd"]), parts["vd"],
            parts["z"], parts["xs"], parts["bm"], parts["cm"], dt, parts["lx"], parts["lg"]]
    return jnp.concatenate(cols, axis=1).astype(BF16)


def _rope_tables(seq):
    t = jnp.arange(seq)
    rowp = (t // GRID_W).astype(F32)
    colp = (t % GRID_W).astype(F32)
    axis_dim = HEAD_DIM // 2
    inv_freq = ROPE_THETA ** (-jnp.arange(0, axis_dim, 2, dtype=F32) / axis_dim)
    ang = jnp.concatenate([rowp[:, None] * inv_freq, colp[:, None] * inv_freq], axis=-1)
    cos, sin = jnp.cos(ang), jnp.sin(ang)
    cos_h = jnp.concatenate([cos, cos], axis=-1)
    sin_h = jnp.concatenate([-sin, sin], axis=-1)
    return jnp.tile(cos_h, (1, 4)), jnp.tile(sin_h, (1, 4))


def _block_diag(w):
    nb, bd, _ = w.shape
    eye = jnp.eye(nb, dtype=w.dtype)
    return (eye[:, None, :, None] * w[:, :, None, :]).reshape(nb * bd, nb * bd)


def _piece_table(counts, cap, stage0, slot0, rows, ids):
    ends = jnp.cumsum(counts, axis=1)
    q = jnp.arange(cap, dtype=jnp.int32)
    owner = jnp.sum((ends[:, None, :] <= q[None, :, None]).astype(jnp.int32), axis=-1)
    mine = owner[:, :, None] == ids
    pick = lambda v: jnp.sum(jnp.where(mine, v[:, None, :], 0), axis=-1)
    step = rows * (q[None, :] - pick(ends - counts))
    return (pick(stage0) + step).reshape(-1), (pick(slot0) + step).reshape(-1)


def _lane_row(fwd, bwd):
    return jnp.concatenate([fwd, bwd, jnp.zeros((LANES - 8,), F32)]).reshape(1, LANES)


def kernel(x, c, ctx, c_ctx, w_ada, b_ada, g_mix_pre, g_mix_post, g_ffn_pre, g_ffn_post, w_in, w_out, a_sink,
           ssd_conv_w, ssd_conv_b, ssd_dt_bias, ssd_a_log, ssd_d, ssd_norm, lru_conv_w, lru_conv_b, lru_w_a,
           lru_b_a, lru_w_i, lru_b_i, lru_lambda, d_q_norm, d_k_norm, router_w, router_bias, exp_w_gate,
           exp_w_up, exp_w_down, sh_w_gate, sh_w_up, sh_w_down):
    batch, seq, d = x.shape
    ctx_len = ctx.shape[1]
    depth = w_ada.shape[0]
    n_lat = batch * seq
    n_ctx = batch * ctx_len
    n_all = n_lat + n_ctx
    assert seq % TM == 0 and n_ctx % TM == 0 and ctx_len >= LRU_CONV
    assert seq % TQ_WINDOW == 0 and seq % (SCAN_CHUNKS * CHUNK) == 0 and ctx_len % (SCAN_CHUNKS * CHUNK) == 0
    assert ctx_len <= KV_CHUNK and seq % KV_CHUNK == 0 and seq % TQ_GLOBAL == 0 and batch + 1 <= SUBLANES

    xu, xu_ctx = x.reshape(n_lat, d), ctx.reshape(n_ctx, d)
    cin = jnp.concatenate([c, c_ctx[None, :], jnp.zeros((SUBLANES - batch - 1, d), F32)], axis=0)
    mod_all = _adaln(cin, w_ada, b_ada)
    cos_t, sin_t = _rope_tables(seq)
    hm = jnp.kron(jnp.eye(4, dtype=F32), jnp.full((HEAD_DIM, HEAD_DIM), 1.0 / HEAD_DIM, F32)).astype(BF16)

    for l in range(depth):
        with_ctx = l < depth - 1
        mod = mod_all[l].reshape(SUBLANES * 6, 1, d)
        gq = jnp.tile(_deinterleave(d_q_norm[l]), 4).reshape(1, 256)
        gk = jnp.tile(_deinterleave(d_k_norm[l]), 2).reshape(1, LANES)
        qa, kat, va, qd, kdt, vd, z, xbc, dt, lu, lg = _inproj(
            xu, xu_ctx, mod, g_mix_pre[l].reshape(1, d), _pad_in_proj(w_in[l]), cos_t, sin_t, gq, gk, hm,
            ssd_conv_w[l], ssd_conv_b[l].reshape(1, -1), lru_conv_w[l], lru_conv_b[l].reshape(1, -1),
            n_lat=n_lat, seq=seq, ctx_len=ctx_len, batch=batch)
        wg = jnp.stack([jnp.concatenate([_block_diag(lru_w_a[l, dd]), _block_diag(lru_w_i[l, dd])], axis=1)
                        for dd in range(2)]).astype(BF16)
        bg = jnp.concatenate([lru_b_a[l], lru_b_i[l]], axis=1).reshape(2, 1, 2 * LRU_WIDTH)
        yf, yb, hf, hb = _scans(xbc, dt, _lane_row(ssd_dt_bias[l, 0], ssd_dt_bias[l, 1]),
                                _lane_row(ssd_a_log[l, 0], ssd_a_log[l, 1]), lu, wg, bg,
                                lru_lambda[l].reshape(2, 1, LRU_WIDTH), batch=batch, seq=seq, ctx_len=ctx_len)

        oa = _window_attn(qa, kat, va, a_sink[l], batch=batch, seq=seq, ctx_len=ctx_len)
        od = _dense_attn(qd, kdt, vd, None, q_row0=0, q_len=seq, tq=TQ_GLOBAL,
                         segs=[(n_lat, ctx_len), (0, seq)], batch=batch)
        if with_ctx:
            oa_c = _dense_attn(qa, kat, va, a_sink[l], q_row0=n_lat, q_len=ctx_len, tq=ctx_len,
                               segs=[(n_lat, ctx_len)], batch=batch)
            od_c = _dense_attn(qd, kdt, vd, None, q_row0=n_lat, q_len=ctx_len, tq=ctx_len,
                               segs=[(n_lat, ctx_len)], batch=batch)
        else:
            oa_c, od_c = oa, od
        n_rows = n_all if with_ctx else n_lat

        dsk = jnp.repeat(ssd_d[l], HEAD_DIM).reshape(1, 256)
        xu_mid, hb_ffn, ld, wk, tab = _outproj(
            xu, xu_ctx, mod, g_mix_post[l].reshape(1, d), oa, od, oa_c, od_c, yf, yb, xbc, z, dsk,
            ssd_norm[l].reshape(1, 256), hf, hb, lg, w_out[l].astype(BF16),
            g_ffn_pre[l].reshape(1, d), router_w[l].T.astype(BF16), router_bias[l].reshape(N_EXPERTS, 1),
            n_rows=n_rows, n_lat=n_lat, seq=seq, batch=batch)
        n_tiles = n_rows // TM
        tab = tab.reshape(n_tiles, SUBLANES, 2 * N_EXPERTS)[:, 0, :]
        seg_cnt, seg_loc = tab[:, :N_EXPERTS], tab[:, N_EXPERTS:]
        counts = jnp.sum(seg_cnt, axis=0)
        padded = (counts + BM_EXPERT - 1) // BM_EXPERT * BM_EXPERT
        padded_end = jnp.cumsum(padded)
        offs = padded_end - padded
        seg_off = offs[None, :] + jnp.cumsum(seg_cnt, axis=0) - seg_cnt
        n_blocks = (n_rows * TOP_K + n_tiles * N_EXPERTS * SEG_ALIGN) // BM_EXPERT + N_EXPERTS
        n_active = (padded_end[-1] // BM_EXPERT).astype(jnp.int32).reshape(1)
        block_start = jnp.arange(n_blocks, dtype=jnp.int32) * BM_EXPERT
        block_e = jnp.minimum(jnp.sum((padded_end[None, :] <= block_start[:, None]).astype(jnp.int32), axis=1),
                              N_EXPERTS - 1)
        ids = jnp.arange(N_EXPERTS, dtype=jnp.int32)
        n_big = seg_cnt // BIG_PIECE
        n_small = (seg_cnt % BIG_PIECE) // SEG_ALIGN
        tabs = (jnp.sum(n_big, axis=1), jnp.sum(n_small, axis=1), seg_loc[:, -1] + seg_cnt[:, -1],
                *_piece_table(n_big, BIG_MAX, seg_loc, seg_off, BIG_PIECE, ids),
                *_piece_table(n_small, SMALL_MAX, seg_loc + n_big * BIG_PIECE, seg_off + n_big * BIG_PIECE,
                              SEG_ALIGN, ids))
        xs = _dispatch(tabs, offs + counts, padded - counts, hb_ffn, ld, n_blocks * BM_EXPERT)
        has_rows = padded > 0
        later = jnp.logical_and(ids[None, :] > ids[:, None], has_rows[None, :])
        nxt_of = jnp.min(jnp.where(later, ids[None, :], N_EXPERTS), axis=1)
        nxt_of = jnp.where(nxt_of == N_EXPERTS, ids, nxt_of)
        slot_of = (jnp.cumsum(has_rows.astype(jnp.int32)) - 1) & 1
        own = block_e[:, None] == ids[None, :]
        next_e = jnp.sum(jnp.where(own, nxt_of[None, :], 0), axis=1)
        w_slot = jnp.sum(jnp.where(own, slot_of[None, :], 0), axis=1)
        ys = _experts(block_e, n_active, next_e, w_slot, xs, exp_w_gate, exp_w_up, exp_w_down, l)
        xu = _combine(tabs, ys, ld, wk, hb_ffn, xu_mid, mod, g_ffn_post[l].reshape(1, d), sh_w_gate[l].astype(BF16),
                      sh_w_up[l].astype(BF16), sh_w_down[l].astype(BF16),
                      n_rows=n_rows, n_lat=n_lat, seq=seq, batch=batch)
        xu_ctx = None
    return xu[:n_lat].reshape(batch, seq, d)
```
